```python
import jax, jax.numpy as jnp
from jax import lax
import numpy as np

D_MODEL = 1024
BATCH = 32
SEQ = 2048
DEPTH = 1

D_MIX = D_MODEL
D_CONV = D_MIX // 2
D_ATT = D_MIX - D_CONV
N_ATT_HEADS = 8
HEAD_DIM = D_ATT // N_ATT_HEADS
CONV_WIDTH = 31
CONV_PAD = CONV_WIDTH // 2
DILATED_PATTERNS = ((128, 1), (512, 4), (2048, 16))
D_IN = 2 * D_CONV + 3 * D_ATT
D_FF = -(-8 * D_MODEL // (3 * 256)) * 256
N_MOD = 6
EPS = 1e-6
NEG_INF = -1e30

kernel_name = "hybrid_conformer_dilated_attn_block"


def _rms_norm(x, g):
    xf = x.astype(jnp.float32)
    y = xf * lax.rsqrt(jnp.mean(xf * xf, axis=-1, keepdims=True) + EPS)
    return (y * g.astype(jnp.float32)).astype(x.dtype)


def _layer_norm(x, g, b):
    xf = x.astype(jnp.float32)
    mu = jnp.mean(xf, axis=-1, keepdims=True)
    var = jnp.mean(jnp.square(xf - mu), axis=-1, keepdims=True)
    y = (xf - mu) * lax.rsqrt(var + EPS)
    return (y * g.astype(jnp.float32) + b.astype(jnp.float32)).astype(x.dtype)


def _modulate(h, shift, scale):
    return h * (1 + scale) + shift


def _alibi_slopes(n_heads):
    return jnp.asarray(2.0 ** (-8.0 * np.arange(1, n_heads + 1) / n_heads), dtype=jnp.float32)


def _dilated_band_attention(q, k, v, slopes, window, dilation):
    B, S, H, E = q.shape
    radius = window // (2 * dilation)
    blk = radius
    n_units = -(-S // dilation)
    nb = -(-n_units // blk)
    s_pad = nb * blk * dilation
    pad = ((0, 0), (0, s_pad - S), (0, 0), (0, 0))

    def to_blocks(t):
        return jnp.pad(t, pad).reshape(B, nb, blk, dilation, H, E)

    def band(t):
        tp = jnp.pad(t, ((0, 0), (1, 1), (0, 0), (0, 0), (0, 0), (0, 0)))
        return jnp.concatenate([tp[:, :-2], tp[:, 1:-1], tp[:, 2:]], axis=2)

    qb = to_blocks(q)
    kw = band(to_blocks(k))
    vw = band(to_blocks(v))
    s = jnp.einsum('bnqrhe,bnkrhe->bnrhqk', qb, kw)

    rel = jnp.arange(3 * blk)[None, :] - blk - jnp.arange(blk)[:, None]
    n_idx = jnp.arange(nb)[:, None, None]
    r_idx = jnp.arange(dilation)[None, :, None]
    key_pos = ((n_idx - 1) * blk + jnp.arange(3 * blk)[None, None, :]) * dilation + r_idx
    valid = (jnp.abs(rel) <= radius)[None, None] & ((key_pos >= 0) & (key_pos < S))[:, :, None, :]
    bias = -slopes[:, None, None] * (dilation * jnp.abs(rel)).astype(jnp.float32)[None]
    s = jnp.where(valid[None, :, :, None], s + bias[None, None, None], NEG_INF)

    lse = jax.nn.logsumexp(s, axis=-1)
    p = jnp.exp(s - lse[..., None])
    o = jnp.einsum('bnrhqk,bnkrhe->bnqrhe', p, vw).reshape(B, s_pad, H, E)[:, :S]
    lse = lse.transpose(0, 1, 4, 2, 3).reshape(B, s_pad, H)[:, :S]
    return o, lse


def _conv_module(a, g, w_dw, b_dw, g_ln, b_ln):
    u = a * jax.nn.sigmoid(g)
    u = lax.conv_general_dilated(u, w_dw[:, None, :].astype(u.dtype), window_strides=(1,),
                                 padding=[(CONV_PAD, CONV_PAD)],
                                 dimension_numbers=('NWC', 'WIO', 'NWC'),
                                 feature_group_count=D_CONV) + b_dw
    return jax.nn.silu(_layer_norm(u, g_ln, b_ln))


def _dilated_attention(q, k, v, g_q, g_k):
    B, S, _ = q.shape
    q = q.reshape(B, S, N_ATT_HEADS, HEAD_DIM)
    k = k.reshape(B, S, N_ATT_HEADS, HEAD_DIM)
    v = v.reshape(B, S, N_ATT_HEADS, HEAD_DIM).astype(jnp.float32)
    q = _rms_norm(q, g_q).astype(jnp.float32) * (HEAD_DIM ** -0.5)
    k = _rms_norm(k, g_k).astype(jnp.float32)
    slopes = _alibi_slopes(N_ATT_HEADS)
    results = [_dilated_band_attention(q, k, v, slopes, w, d) for (w, d) in DILATED_PATTERNS]
    lses = jnp.stack([r[1] for r in results], axis=0)
    wts = jax.nn.softmax(lses, axis=0)
    o = sum(wts[i][..., None] * results[i][0] for i in range(len(results)))
    return o.reshape(B, S, D_ATT)


def _fwd_setup_inputs(seed: int = 0) -> dict:
    key = jax.random.key(seed)
    ks = jax.random.split(key, 18)
    f32 = jnp.float32
    nrm = lambda k, shape, s: jax.random.normal(k, shape, f32) * s
    gain = lambda k, shape: 1.0 + 0.02 * jax.random.normal(k, shape, f32)
    return {
        "x": jax.random.normal(ks[0], (BATCH, SEQ, D_MODEL), f32),
        "c": jax.random.normal(ks[1], (BATCH, D_MODEL), f32),
        "w_ada": nrm(ks[2], (DEPTH, D_MODEL, N_MOD * D_MODEL), D_MODEL ** -0.5),
        "b_ada": nrm(ks[3], (DEPTH, N_MOD * D_MODEL), 0.02),
        "g_mix": gain(ks[4], (DEPTH, D_MODEL)),
        "w_in": nrm(ks[5], (DEPTH, D_MODEL, D_IN), D_MODEL ** -0.5),
        "w_dw": nrm(ks[6], (DEPTH, CONV_WIDTH, D_CONV), CONV_WIDTH ** -0.5),
        "b_dw": nrm(ks[7], (DEPTH, D_CONV), 0.02),
        "g_conv_ln": gain(ks[8], (DEPTH, D_CONV)),
        "b_conv_ln": nrm(ks[9], (DEPTH, D_CONV), 0.02),
        "g_q": gain(ks[10], (DEPTH, HEAD_DIM)),
        "g_k": gain(ks[11], (DEPTH, HEAD_DIM)),
        "w_out": nrm(ks[12], (DEPTH, D_MIX, D_MODEL), D_MIX ** -0.5),
        "g_ffn": gain(ks[13], (DEPTH, D_MODEL)),
        "w_gate": nrm(ks[14], (DEPTH, D_MODEL, D_FF), D_MODEL ** -0.5),
        "w_up": nrm(ks[15], (DEPTH, D_MODEL, D_FF), D_MODEL ** -0.5),
        "w_down": nrm(ks[16], (DEPTH, D_FF, D_MODEL), D_FF ** -0.5),
    }


def _fwd_reference(x, c, w_ada, b_ada, g_mix, w_in, w_dw, b_dw, g_conv_ln, b_conv_ln,
              g_q, g_k, w_out, g_ffn, w_gate, w_up, w_down):
    B, S, D = x.shape
    split_at = [D_CONV, 2 * D_CONV, 2 * D_CONV + D_ATT, 2 * D_CONV + 2 * D_ATT]
    for l in range(DEPTH):
        mod = jax.nn.silu(c) @ w_ada[l] + b_ada[l]
        shift_m, scale_m, gate_m, shift_f, scale_f, gate_f = [
            m[:, None, :] for m in jnp.split(mod, N_MOD, axis=-1)]

        h = _modulate(_rms_norm(x, g_mix[l]), shift_m, scale_m)
        proj = h @ w_in[l]
        a, g, q, k, v = jnp.split(proj, split_at, axis=-1)
        y_conv = _conv_module(a, g, w_dw[l], b_dw[l], g_conv_ln[l], b_conv_ln[l])
        y_att = _dilated_attention(q, k, v, g_q[l], g_k[l]).astype(x.dtype)
        mix = jnp.concatenate([y_conv, y_att], axis=-1) @ w_out[l]
        x = x + gate_m * mix

        h = _modulate(_rms_norm(x, g_ffn[l]), shift_f, scale_f)
        f = (jax.nn.silu(h @ w_gate[l]) * (h @ w_up[l])) @ w_down[l]
        x = x + gate_f * f
    return x


import jax as _jax
import jax.numpy as _jnp

TWIN_FORMAT = 'train_step'
FWD_PARAMS = ['x', 'c', 'w_ada', 'b_ada', 'g_mix', 'w_in', 'w_dw', 'b_dw', 'g_conv_ln', 'b_conv_ln', 'g_q', 'g_k', 'w_out', 'g_ffn', 'w_gate', 'w_up', 'w_down']
TWIN_WEIGHTS = ['w_ada', 'b_ada', 'g_mix', 'w_in', 'w_dw', 'b_dw', 'g_conv_ln', 'b_conv_ln', 'g_q', 'g_k', 'w_out', 'g_ffn', 'w_gate', 'w_up', 'w_down']
TWIN_DIFF_INPUT = 'x'
TWIN_INPUTS = ['x', 'c', 'w_ada', 'b_ada', 'g_mix', 'w_in', 'w_dw', 'b_dw', 'g_conv_ln', 'b_conv_ln', 'g_q', 'g_k', 'w_out', 'g_ffn', 'w_gate', 'w_up', 'w_down', 'loss_target', 'm_w_ada', 'm_b_ada', 'm_g_mix', 'm_w_in', 'm_w_dw', 'm_b_dw', 'm_g_conv_ln', 'm_b_conv_ln', 'm_g_q', 'm_g_k', 'm_w_out', 'm_g_ffn', 'm_w_gate', 'm_w_up', 'm_w_down', 'v_w_ada', 'v_b_ada', 'v_g_mix', 'v_w_in', 'v_w_dw', 'v_b_dw', 'v_g_conv_ln', 'v_b_conv_ln', 'v_g_q', 'v_g_k', 'v_w_out', 'v_g_ffn', 'v_w_gate', 'v_w_up', 'v_w_down']
TWIN_OUTPUTS = ['loss', 'grad_x', 'grad_w_ada', 'grad_b_ada', 'grad_g_mix', 'grad_w_in', 'grad_w_dw', 'grad_b_dw', 'grad_g_conv_ln', 'grad_b_conv_ln', 'grad_g_q', 'grad_g_k', 'grad_w_out', 'grad_g_ffn', 'grad_w_gate', 'grad_w_up', 'grad_w_down', 'delta_w_ada', 'delta_b_ada', 'delta_g_mix', 'delta_w_in', 'delta_w_dw', 'delta_b_dw', 'delta_g_conv_ln', 'delta_b_conv_ln', 'delta_g_q', 'delta_g_k', 'delta_w_out', 'delta_g_ffn', 'delta_w_gate', 'delta_w_up', 'delta_w_down', 'new_m_w_ada', 'new_m_b_ada', 'new_m_g_mix', 'new_m_w_in', 'new_m_w_dw', 'new_m_b_dw', 'new_m_g_conv_ln', 'new_m_b_conv_ln', 'new_m_g_q', 'new_m_g_k', 'new_m_w_out', 'new_m_g_ffn', 'new_m_w_gate', 'new_m_w_up', 'new_m_w_down', 'new_v_w_ada', 'new_v_b_ada', 'new_v_g_mix', 'new_v_w_in', 'new_v_w_dw', 'new_v_b_dw', 'new_v_g_conv_ln', 'new_v_b_conv_ln', 'new_v_g_q', 'new_v_g_k', 'new_v_w_out', 'new_v_g_ffn', 'new_v_w_gate', 'new_v_w_up', 'new_v_w_down']
TWIN_LEAF_KINDS = {'loss': 'loss', 'grad_x': 'grad_x', 'grad_w_ada': 'grad_w', 'grad_b_ada': 'grad_w', 'grad_g_mix': 'grad_w', 'grad_w_in': 'grad_w', 'grad_w_dw': 'grad_w', 'grad_b_dw': 'grad_w', 'grad_g_conv_ln': 'grad_w', 'grad_b_conv_ln': 'grad_w', 'grad_g_q': 'grad_w', 'grad_g_k': 'grad_w', 'grad_w_out': 'grad_w', 'grad_g_ffn': 'grad_w', 'grad_w_gate': 'grad_w', 'grad_w_up': 'grad_w', 'grad_w_down': 'grad_w', 'delta_w_ada': 'delta_w', 'delta_b_ada': 'delta_w', 'delta_g_mix': 'delta_w', 'delta_w_in': 'delta_w', 'delta_w_dw': 'delta_w', 'delta_b_dw': 'delta_w', 'delta_g_conv_ln': 'delta_w', 'delta_b_conv_ln': 'delta_w', 'delta_g_q': 'delta_w', 'delta_g_k': 'delta_w', 'delta_w_out': 'delta_w', 'delta_g_ffn': 'delta_w', 'delta_w_gate': 'delta_w', 'delta_w_up': 'delta_w', 'delta_w_down': 'delta_w', 'new_m_w_ada': 'new_m', 'new_m_b_ada': 'new_m', 'new_m_g_mix': 'new_m', 'new_m_w_in': 'new_m', 'new_m_w_dw': 'new_m', 'new_m_b_dw': 'new_m', 'new_m_g_conv_ln': 'new_m', 'new_m_b_conv_ln': 'new_m', 'new_m_g_q': 'new_m', 'new_m_g_k': 'new_m', 'new_m_w_out': 'new_m', 'new_m_g_ffn': 'new_m', 'new_m_w_gate': 'new_m', 'new_m_w_up': 'new_m', 'new_m_w_down': 'new_m', 'new_v_w_ada': 'new_v', 'new_v_b_ada': 'new_v', 'new_v_g_mix': 'new_v', 'new_v_w_in': 'new_v', 'new_v_w_dw': 'new_v', 'new_v_b_dw': 'new_v', 'new_v_g_conv_ln': 'new_v', 'new_v_b_conv_ln': 'new_v', 'new_v_g_q': 'new_v', 'new_v_g_k': 'new_v', 'new_v_w_out': 'new_v', 'new_v_g_ffn': 'new_v', 'new_v_w_gate': 'new_v', 'new_v_w_up': 'new_v', 'new_v_w_down': 'new_v'}


def _forward(args):
    return _fwd_reference(*[args[k] for k in FWD_PARAMS])


def _output_shape():
    out = _jax.eval_shape(lambda: _forward(_fwd_setup_inputs(0)))
    return out.shape, out.dtype

N_MICROBATCH = 1
ADAM_LR = 0.001
ADAM_B1 = 0.9
ADAM_B2 = 0.999
ADAM_EPS = 1e-08
ADAM_WD = 0.01
ADAM_STEP = 10
PER_EXAMPLE_BATCH_AXIS = {'x': 0, 'c': 0, 'loss_target': 0}
SHARED_INPUTS = []
_WEIGHT_DTYPES = {'w_ada': _jnp.float32, 'b_ada': _jnp.float32, 'g_mix': _jnp.float32, 'w_in': _jnp.float32, 'w_dw': _jnp.float32, 'b_dw': _jnp.float32, 'g_conv_ln': _jnp.float32, 'b_conv_ln': _jnp.float32, 'g_q': _jnp.float32, 'g_k': _jnp.float32, 'w_out': _jnp.float32, 'g_ffn': _jnp.float32, 'w_gate': _jnp.float32, 'w_up': _jnp.float32, 'w_down': _jnp.float32}
MOMENT_SCALE = {'w_ada': 1.157639e+01, 'b_ada': 2.406982e+01, 'g_mix': 1.000391e+00, 'w_in': 5.483779e+00, 'w_dw': 4.340747e+00, 'b_dw': 1.419316e+01, 'g_conv_ln': 1.761111e+01, 'b_conv_ln': 1.343131e+01, 'g_q': 5.238115e+00, 'g_k': 5.266902e+00, 'w_out': 8.902238e+00, 'g_ffn': 5.879944e+01, 'w_gate': 3.807492e+00, 'w_up': 2.866331e+00, 'w_down': 3.220232e+00}


def _to_microbatches(a, axis):
    t = _jnp.moveaxis(a, axis, 0)
    t = t.reshape((N_MICROBATCH, t.shape[0] // N_MICROBATCH) + t.shape[1:])
    return _jnp.moveaxis(t, 1, axis + 1)


def setup_inputs(seed: int = 0) -> dict:
    inp = _fwd_setup_inputs(seed)
    key = _jax.random.fold_in(_jax.random.key(seed), 7919)
    shape, _ = _output_shape()
    out = dict(inp)
    out["loss_target"] = _jax.random.normal(_jax.random.fold_in(key, 0), shape, _jnp.float32)
    for i, name in enumerate(TWIN_WEIGHTS):
        w = inp[name].astype(_jnp.float32)
        if MOMENT_SCALE is None:
            s = _jnp.sqrt(_jnp.mean(_jnp.square(w)) + 1e-30)
        else:
            s = MOMENT_SCALE[name]
        km, kv = _jax.random.split(_jax.random.fold_in(key, i + 1))
        out[name] = w
        out["m_" + name] = s * _jax.random.normal(km, w.shape, _jnp.float32)
        out["v_" + name] = (s * s) * _jax.random.uniform(kv, w.shape, _jnp.float32, 0.5, 1.5)
    if N_MICROBATCH > 1:
        for name, axis in PER_EXAMPLE_BATCH_AXIS.items():
            out[name] = _to_microbatches(out[name], axis)
    return {'x': out['x'], 'c': out['c'], 'w_ada': out['w_ada'], 'b_ada': out['b_ada'], 'g_mix': out['g_mix'], 'w_in': out['w_in'], 'w_dw': out['w_dw'], 'b_dw': out['b_dw'], 'g_conv_ln': out['g_conv_ln'], 'b_conv_ln': out['b_conv_ln'], 'g_q': out['g_q'], 'g_k': out['g_k'], 'w_out': out['w_out'], 'g_ffn': out['g_ffn'], 'w_gate': out['w_gate'], 'w_up': out['w_up'], 'w_down': out['w_down'], 'loss_target': out['loss_target'], 'm_w_ada': out['m_w_ada'], 'm_b_ada': out['m_b_ada'], 'm_g_mix': out['m_g_mix'], 'm_w_in': out['m_w_in'], 'm_w_dw': out['m_w_dw'], 'm_b_dw': out['m_b_dw'], 'm_g_conv_ln': out['m_g_conv_ln'], 'm_b_conv_ln': out['m_b_conv_ln'], 'm_g_q': out['m_g_q'], 'm_g_k': out['m_g_k'], 'm_w_out': out['m_w_out'], 'm_g_ffn': out['m_g_ffn'], 'm_w_gate': out['m_w_gate'], 'm_w_up': out['m_w_up'], 'm_w_down': out['m_w_down'], 'v_w_ada': out['v_w_ada'], 'v_b_ada': out['v_b_ada'], 'v_g_mix': out['v_g_mix'], 'v_w_in': out['v_w_in'], 'v_w_dw': out['v_w_dw'], 'v_b_dw': out['v_b_dw'], 'v_g_conv_ln': out['v_g_conv_ln'], 'v_b_conv_ln': out['v_b_conv_ln'], 'v_g_q': out['v_g_q'], 'v_g_k': out['v_g_k'], 'v_w_out': out['v_w_out'], 'v_g_ffn': out['v_g_ffn'], 'v_w_gate': out['v_w_gate'], 'v_w_up': out['v_w_up'], 'v_w_down': out['v_w_down']}


def _loss(weights, diff, rest, loss_target):
    with _jax.named_scope("forward"):
        args = {**rest, TWIN_DIFF_INPUT: diff, **{k: w.astype(_WEIGHT_DTYPES[k]) for k, w in weights.items()}}
        y = _forward(args)
    with _jax.named_scope("loss_head"):
        err = _jnp.square(y.astype(_jnp.float32) - loss_target)
        return 0.5 * _jnp.sum(_jnp.mean(err, axis=-1)) if err.ndim else 0.5 * err


def _adamw(w, g, m, v):
    m = ADAM_B1 * m + (1.0 - ADAM_B1) * g
    v = ADAM_B2 * v + (1.0 - ADAM_B2) * _jnp.square(g)
    m_hat = m / (1.0 - ADAM_B1 ** ADAM_STEP)
    v_hat = v / (1.0 - ADAM_B2 ** ADAM_STEP)
    delta = -ADAM_LR * (m_hat / (_jnp.sqrt(v_hat) + ADAM_EPS) + ADAM_WD * w)
    return delta, m, v


def reference(x, c, w_ada, b_ada, g_mix, w_in, w_dw, b_dw, g_conv_ln, b_conv_ln, g_q, g_k, w_out, g_ffn, w_gate, w_up, w_down, loss_target, m_w_ada, m_b_ada, m_g_mix, m_w_in, m_w_dw, m_b_dw, m_g_conv_ln, m_b_conv_ln, m_g_q, m_g_k, m_w_out, m_g_ffn, m_w_gate, m_w_up, m_w_down, v_w_ada, v_b_ada, v_g_mix, v_w_in, v_w_dw, v_b_dw, v_g_conv_ln, v_b_conv_ln, v_g_q, v_g_k, v_w_out, v_g_ffn, v_w_gate, v_w_up, v_w_down):
    given = dict(x=x, c=c, w_ada=w_ada, b_ada=b_ada, g_mix=g_mix, w_in=w_in, w_dw=w_dw, b_dw=b_dw, g_conv_ln=g_conv_ln, b_conv_ln=b_conv_ln, g_q=g_q, g_k=g_k, w_out=w_out, g_ffn=g_ffn, w_gate=w_gate, w_up=w_up, w_down=w_down, loss_target=loss_target, m_w_ada=m_w_ada, m_b_ada=m_b_ada, m_g_mix=m_g_mix, m_w_in=m_w_in, m_w_dw=m_w_dw, m_b_dw=m_b_dw, m_g_conv_ln=m_g_conv_ln, m_b_conv_ln=m_b_conv_ln, m_g_q=m_g_q, m_g_k=m_g_k, m_w_out=m_w_out, m_g_ffn=m_g_ffn, m_w_gate=m_w_gate, m_w_up=m_w_up, m_w_down=m_w_down, v_w_ada=v_w_ada, v_b_ada=v_b_ada, v_g_mix=v_g_mix, v_w_in=v_w_in, v_w_dw=v_w_dw, v_b_dw=v_b_dw, v_g_conv_ln=v_g_conv_ln, v_b_conv_ln=v_b_conv_ln, v_g_q=v_g_q, v_g_k=v_g_k, v_w_out=v_w_out, v_g_ffn=v_g_ffn, v_w_gate=v_w_gate, v_w_up=v_w_up, v_w_down=v_w_down)
    weights = {n: given[n] for n in TWIN_WEIGHTS}
    shared = {n: given[n] for n in SHARED_INPUTS}
    per_example = {n: given[n] for n in ['x', 'c']}
    grad_fn = _jax.value_and_grad(_loss, argnums=(0, 1))

    def one_microbatch(ex, loss_target):
        ex = dict(ex)
        diff = ex.pop(TWIN_DIFF_INPUT)
        return grad_fn(weights, diff, {**shared, **ex}, loss_target)

    if N_MICROBATCH == 1:
        loss, (grad_w, grad_x) = one_microbatch(per_example, given["loss_target"])
    else:
        def body(carry, xs):
            loss_sum, grad_sum = carry
            l_k, (gw_k, gx_k) = one_microbatch(xs[0], xs[1])
            with _jax.named_scope("update"):
                return (loss_sum + l_k, _jax.tree.map(_jnp.add, grad_sum, gw_k)), gx_k

        init = (_jnp.zeros((), _jnp.float32), _jax.tree.map(_jnp.zeros_like, weights))
        (loss, grad_w), grad_x = _jax.lax.scan(body, init, (per_example, given["loss_target"]))
    with _jax.named_scope("update"):
        delta_w, new_m, new_v = {}, {}, {}
        for n in TWIN_WEIGHTS:
            delta_w[n], new_m[n], new_v[n] = _adamw(weights[n], grad_w[n], given["m_" + n], given["v_" + n])
    return (loss, grad_x, *[grad_w[n] for n in TWIN_WEIGHTS], *[delta_w[n] for n in TWIN_WEIGHTS],
            *[new_m[n] for n in TWIN_WEIGHTS], *[new_v[n] for n in TWIN_WEIGHTS])
```

```python
import functools
import math

import jax
import jax.numpy as jnp
import numpy as np
from jax import lax
from jax.experimental import pallas as pl
from jax.experimental.pallas import tpu as pltpu

F32 = jnp.float32
MXU_DTYPE = jnp.bfloat16
ACT_DTYPE = jnp.bfloat16
EPS = 1e-6
NEG_INF = -1e30
HEAD_DIM = 64
LANES = 128
RADIUS = 64
QBLK = 128
DILATIONS = (1, 4, 16)
CONV_WIDTH = 31
CONV_PAD = CONV_WIDTH // 2
CONV_ROWS = 32
N_MOD = 6
ADAM_LR, ADAM_B1, ADAM_B2, ADAM_EPS, ADAM_WD, ADAM_STEP = 0.001, 0.9, 0.999, 1e-08, 0.01, 10
HIGHEST = lax.Precision.HIGHEST
MESH_DEV = pl.DeviceIdType.MESH
VMEM_LIMIT = 56 << 20


def _cp(sem=None, vmem=VMEM_LIMIT):
    kw = dict(vmem_limit_bytes=vmem)
    if sem is not None:
        kw["dimension_semantics"] = sem
    return pltpu.CompilerParams(**kw)


def _sigmoid(x):
    return 1.0 / (1.0 + jnp.exp(-x))


def _dot(a, b):
    return jnp.dot(a, b, preferred_element_type=F32)


def _dot_nt(a, b):
    return lax.dot_general(a, b, (((1,), (1,)), ((), ())), preferred_element_type=F32)


def _dot_tn(a, b):
    return lax.dot_general(a, b, (((0,), (0,)), ((), ())), preferred_element_type=F32)


def _colsum(v):
    return jnp.sum(v, axis=0, keepdims=True)


def _load_resident(i, pairs, sems):
    @pl.when(i == 0)
    def _():
        cps = [pltpu.make_async_copy(src, dst, sems.at[n]) for n, (src, dst) in enumerate(pairs)]
        for c in cps:
            c.start()
        for c in cps:
            c.wait()


def _fwd_in(x2, mod, g_mix, w_in, *, S, tm, n_ag):
    T, D = x2.shape
    P, _, Nb = w_in.shape
    n_in = P * Nb
    n_slab = (n_in - n_ag) // LANES
    tps = S // tm

    def body(x_ref, mod_ref, g_ref, w_ref, ag_ref, qkv_ref, h_ref):
        x = x_ref[...]
        r = lax.rsqrt(jnp.mean(x * x, axis=-1, keepdims=True) + EPS)
        n = x * r * g_ref[...]
        h = n * (1.0 + mod_ref[:, D:2 * D]) + mod_ref[:, 0:D]
        hb = h.astype(MXU_DTYPE)
        h_ref[...] = hb
        parts = [_dot(hb, w_ref[p]) for p in range(P)]
        proj = jnp.concatenate(parts, axis=1) if P > 1 else parts[0]
        ag_ref[...] = proj[:, :n_ag]
        for j in range(n_slab):
            qkv_ref[j] = proj[:, n_ag + LANES * j:n_ag + LANES * (j + 1)]

    return pl.pallas_call(
        body, grid=(T // tm,), name="fwd_in",
        in_specs=[pl.BlockSpec((tm, D), lambda i: (i, 0)),
                  pl.BlockSpec((None, 1, N_MOD * D), lambda i: (i // tps, 0, 0)),
                  pl.BlockSpec((1, D), lambda i: (0, 0)),
                  pl.BlockSpec((P, D, Nb), lambda i: (0, 0, 0))],
        out_specs=[pl.BlockSpec((tm, n_ag), lambda i: (i, 0)),
                   pl.BlockSpec((n_slab, tm, LANES), lambda i: (0, i, 0)),
                   pl.BlockSpec((tm, D), lambda i: (i, 0))],
        out_shape=[jax.ShapeDtypeStruct((T, n_ag), F32),
                   jax.ShapeDtypeStruct((n_slab, T, LANES), F32),
                   jax.ShapeDtypeStruct((T, D), MXU_DTYPE)],
        compiler_params=_cp(("arbitrary",)),
    )(x2, mod, g_mix, w_in)


CONV_CH = 64


def _conv_taps(win, w_ref, acc, reverse):
    n = win.shape[0]
    for b in range(8):
        wb = win if b == 0 else pltpu.roll(win, shift=n - b, axis=0)
        for a in range(4):
            o = 8 * a + b
            if o < 1 or o > CONV_WIDTH:
                continue
            k = (CONV_WIDTH - o) if reverse else (o - 1)
            acc = acc + w_ref[k:k + 1, :] * wb[8 * a:8 * a + CONV_CH, :]
    return acc


def _conv_fwd(ag, wdw, *, Bl, S, DC):
    T = ag.shape[0]
    nsc = DC // LANES
    CH = CONV_CH

    def body(a_ref, g_ref, w_ref, cv_ref, upad):
        zeros16 = jnp.zeros((16, LANES), F32)
        upad[0:16, :] = zeros16
        upad[S + 16:S + 32, :] = zeros16

        def fill(i, _):
            r0 = pl.multiple_of(i * CH, CH)
            a = a_ref[pl.ds(r0, CH), :]
            g = g_ref[pl.ds(r0, CH), :]
            upad[pl.ds(r0 + 16, CH), :] = a * _sigmoid(g)
            return 0
        lax.fori_loop(0, S // CH, fill, 0)

        def conv(i, _):
            r0 = pl.multiple_of(i * CH, CH)
            win = upad[pl.ds(r0, CH + 32), :]
            acc = jnp.zeros((CH, LANES), F32) + w_ref[CONV_WIDTH:CONV_WIDTH + 1, :]
            cv_ref[pl.ds(r0, CH), :] = _conv_taps(win, w_ref, acc, reverse=False)
            return 0
        lax.fori_loop(0, S // CH, conv, 0)

    return pl.pallas_call(
        body, grid=(Bl, nsc), name="conv_fwd",
        in_specs=[pl.BlockSpec((S, LANES), lambda b, j: (b, j)),
                  pl.BlockSpec((S, LANES), lambda b, j: (b, nsc + j)),
                  pl.BlockSpec((CONV_ROWS, LANES), lambda b, j: (0, j))],
        out_specs=pl.BlockSpec((S, LANES), lambda b, j: (b, j)),
        out_shape=jax.ShapeDtypeStruct((T, DC), F32),
        scratch_shapes=[pltpu.VMEM((S + 32, LANES), F32)],
        compiler_params=_cp(("arbitrary", "arbitrary")),
    )(ag, ag, wdw)


def _conv_bwd(ag, dcv, wdw, *, Bl, S, DC):
    T = ag.shape[0]
    nsc = DC // LANES
    CH = CONV_CH

    def body(a_ref, g_ref, d_ref, w_ref, da_ref, dg_ref, dw_ref, upad, dpad, wacc):
        b = pl.program_id(1)
        zeros16 = jnp.zeros((16, LANES), F32)
        upad[0:16, :] = zeros16
        upad[S + 16:S + 32, :] = zeros16
        dpad[0:16, :] = zeros16
        dpad[S + 16:S + 32, :] = zeros16

        @pl.when(b == 0)
        def _():
            wacc[...] = jnp.zeros_like(wacc)

        def fill(i, _):
            r0 = pl.multiple_of(i * CH, CH)
            a = a_ref[pl.ds(r0, CH), :]
            g = g_ref[pl.ds(r0, CH), :]
            upad[pl.ds(r0 + 16, CH), :] = a * _sigmoid(g)
            dpad[pl.ds(r0 + 16, CH), :] = d_ref[pl.ds(r0, CH), :]
            return 0
        lax.fori_loop(0, S // CH, fill, 0)

        def step(i, _):
            r0 = pl.multiple_of(i * CH, CH)
            dwin = dpad[pl.ds(r0, CH + 32), :]
            du = _conv_taps(dwin, w_ref, jnp.zeros((CH, LANES), F32), reverse=True)
            a = a_ref[pl.ds(r0, CH), :]
            g = g_ref[pl.ds(r0, CH), :]
            sg = _sigmoid(g)
            da_ref[pl.ds(r0, CH), :] = du * sg
            dg_ref[pl.ds(r0, CH), :] = du * a * sg * (1.0 - sg)
            dc = d_ref[pl.ds(r0, CH), :]
            uwin = upad[pl.ds(r0, CH + 32), :]
            n = CH + 32
            for bb in range(8):
                wb = uwin if bb == 0 else pltpu.roll(uwin, shift=n - bb, axis=0)
                for aa in range(4):
                    o = 8 * aa + bb
                    if o < 1 or o > CONV_WIDTH:
                        continue
                    k = o - 1
                    prod = dc * wb[8 * aa:8 * aa + CH, :]
                    part = prod[0:8, :]
                    for q in range(1, CH // 8):
                        part = part + prod[8 * q:8 * q + 8, :]
                    wacc[8 * k:8 * k + 8, :] += part
            part = dc[0:8, :]
            for q in range(1, CH // 8):
                part = part + dc[8 * q:8 * q + 8, :]
            wacc[8 * CONV_WIDTH:8 * CONV_WIDTH + 8, :] += part
            return 0
        lax.fori_loop(0, S // CH, step, 0)

        @pl.when(b == Bl - 1)
        def _():
            for k in range(CONV_ROWS):
                dw_ref[k:k + 1, :] = jnp.sum(wacc[8 * k:8 * k + 8, :], axis=0, keepdims=True)

    return pl.pallas_call(
        body, grid=(nsc, Bl), name="conv_bwd",
        in_specs=[pl.BlockSpec((S, LANES), lambda j, b: (b, j)),
                  pl.BlockSpec((S, LANES), lambda j, b: (b, nsc + j)),
                  pl.BlockSpec((S, LANES), lambda j, b: (b, j)),
                  pl.BlockSpec((CONV_ROWS, LANES), lambda j, b: (0, j))],
        out_specs=[pl.BlockSpec((S, LANES), lambda j, b: (b, j)),
                   pl.BlockSpec((S, LANES), lambda j, b: (b, j)),
                   pl.BlockSpec((CONV_ROWS, LANES), lambda j, b: (0, j))],
        out_shape=[jax.ShapeDtypeStruct((T, DC), F32), jax.ShapeDtypeStruct((T, DC), F32),
                   jax.ShapeDtypeStruct((CONV_ROWS, DC), F32)],
        scratch_shapes=[pltpu.VMEM((S + 32, LANES), F32), pltpu.VMEM((S + 32, LANES), F32),
                        pltpu.VMEM((8 * CONV_ROWS, LANES), F32)],
        compiler_params=_cp(("arbitrary", "arbitrary")),
    )(ag, ag, dcv, wdw)


ROWCH = 256


def _head_mean_matrix():
    r = lax.broadcasted_iota(jnp.int32, (LANES, LANES), 0) // HEAD_DIM
    c = lax.broadcasted_iota(jnp.int32, (LANES, LANES), 1) // HEAD_DIM
    return jnp.where(r == c, 1.0 / HEAD_DIM, 0.0).astype(F32)


def _head_mean(v, mm):
    return jnp.dot(v, mm, preferred_element_type=F32, precision=HIGHEST)


def _stack_heads(blk, lane_lo):
    z = jnp.zeros_like(blk)
    return jnp.concatenate([jnp.where(lane_lo, blk, z), jnp.where(lane_lo, z, blk)], axis=0)


def _merge_heads(v2, lane_lo):
    return jnp.where(lane_lo, v2[:QBLK], v2[QBLK:])


def _bias_tables(bias_ref, slope_ref):
    row = lax.broadcasted_iota(jnp.int32, (2 * QBLK, 2 * QBLK), 0)
    col = lax.broadcasted_iota(jnp.int32, (2 * QBLK, 2 * QBLK), 1)
    rel = jnp.abs(col - RADIUS - (row % QBLK))
    slope = jnp.where(row < QBLK, slope_ref[0:1, 0:1], slope_ref[0:1, HEAD_DIM:HEAD_DIM + 1])
    for pi, d in enumerate(DILATIONS):
        bias_ref[pi] = jnp.where(rel <= RADIUS, -slope * (float(d) * rel.astype(F32)), NEG_INF)


def _edge_masks():
    col = lax.broadcasted_iota(jnp.int32, (1, 2 * QBLK), 1)
    first = jnp.where(col < RADIUS, NEG_INF, 0.0).astype(F32)
    last = jnp.where(col >= QBLK + RADIUS, NEG_INF, 0.0).astype(F32)
    return first, last


def _gather_rows(src_ref, dst_ref, S, d, pad):
    n = S // d
    seg = n + 2 * RADIUS if pad else n
    step = min(n, 512)
    for r in range(d):
        base = r * seg
        if pad:
            dst_ref[base:base + RADIUS, :] = jnp.zeros((RADIUS, LANES), dst_ref.dtype)
            dst_ref[base + RADIUS + n:base + seg, :] = jnp.zeros((RADIUS, LANES), dst_ref.dtype)
            base += RADIUS
        for c0 in range(0, n, step):
            if d == 1:
                v = src_ref[c0:c0 + step, :]
            else:
                v = src_ref[pl.ds(r + c0 * d, step, stride=d), :]
            dst_ref[base + c0:base + c0 + step, :] = v.astype(dst_ref.dtype)


def _scatter_rows(src_ref, dst_ref, S, d, pad, accumulate):
    n = S // d
    seg = n + 2 * RADIUS if pad else n
    step = min(n, 512)
    for r in range(d):
        base = r * seg + (RADIUS if pad else 0)
        for c0 in range(0, n, step):
            v = src_ref[base + c0:base + c0 + step, :]
            if d == 1:
                idx = pl.ds(c0, step)
            else:
                idx = pl.ds(r + c0 * d, step, stride=d)
            if accumulate:
                dst_ref[idx, :] = dst_ref[idx, :] + v
            else:
                dst_ref[idx, :] = v


def _qk_normalize(q_ref, k_ref, gq_ref, gk_ref, qh, kh, S, mm):
    for c0 in range(0, S, ROWCH):
        q = q_ref[c0:c0 + ROWCH, :]
        k = k_ref[c0:c0 + ROWCH, :]
        qh[c0:c0 + ROWCH, :] = q * lax.rsqrt(_head_mean(q * q, mm) + EPS) * (gq_ref[...] * HEAD_DIM ** -0.5)
        kh[c0:c0 + ROWCH, :] = k * lax.rsqrt(_head_mean(k * k, mm) + EPS) * gk_ref[...]


def _attn_fwd(qkv, gq2, gk2, slopes, *, Bl, S):
    n3, T, _ = qkv.shape
    NS = n3 // 3
    NB = S // QBLK
    PADR = S + 2 * RADIUS * DILATIONS[-1]

    def body(q_ref, k_ref, v_ref, gq_ref, gk_ref, slope_ref, o_ref, lse_ref,
             qh, kh, qp, kp, vp, op, lp, onat, lnat, bias_ref):
        mm = _head_mean_matrix()
        lane_lo = lax.broadcasted_iota(jnp.int32, (QBLK, LANES), 1) < HEAD_DIM
        _bias_tables(bias_ref, slope_ref)
        m_first, m_last = _edge_masks()
        _qk_normalize(q_ref, k_ref, gq_ref, gk_ref, qh, kh, S, mm)

        for pi, d in enumerate(DILATIONS):
            n = S // d
            nb = n // QBLK
            _gather_rows(qh, qp, S, d, pad=False)
            _gather_rows(kh, kp, S, d, pad=True)
            _gather_rows(v_ref, vp, S, d, pad=True)

            def blk(i, _, pi=pi, nb=nb):
                r = i // nb
                qb = i % nb
                q0 = pl.multiple_of(i * QBLK, QBLK)
                k0 = pl.multiple_of((i + r) * QBLK, QBLK)
                qs = _stack_heads(qp[pl.ds(q0, QBLK), :], lane_lo)
                kwin = kp[pl.ds(k0, 2 * QBLK), :]
                vwin = vp[pl.ds(k0, 2 * QBLK), :]
                s = _dot_nt(qs, kwin) + bias_ref[pi]
                s = s + jnp.where(qb == 0, m_first, 0.0) + jnp.where(qb == nb - 1, m_last, 0.0)
                m = jnp.max(s, axis=1, keepdims=True)
                p = jnp.exp(s - m)
                l = jnp.sum(p, axis=1, keepdims=True)
                o2 = _dot(p.astype(MXU_DTYPE), vwin) * (1.0 / l)
                lse2 = jnp.broadcast_to(m + jnp.log(l), (2 * QBLK, LANES))
                op[pl.ds(q0, QBLK), :] = _merge_heads(o2, lane_lo)
                lp[pl.ds(q0, QBLK), :] = _merge_heads(lse2, lane_lo)
                return 0
            lax.fori_loop(0, NB, blk, 0)
            _scatter_rows(op, onat.at[pi], S, d, pad=False, accumulate=False)
            _scatter_rows(lp, lnat.at[pi], S, d, pad=False, accumulate=False)

        for c0 in range(0, S, ROWCH):
            ls = [lnat[pi, c0:c0 + ROWCH, :] for pi in range(len(DILATIONS))]
            mx = jnp.maximum(jnp.maximum(ls[0], ls[1]), ls[2])
            es = [jnp.exp(l - mx) for l in ls]
            tot = es[0] + es[1] + es[2]
            inv = 1.0 / tot
            acc = (es[0] * inv) * onat[0, c0:c0 + ROWCH, :]
            for pi in (1, 2):
                acc = acc + (es[pi] * inv) * onat[pi, c0:c0 + ROWCH, :]
            o_ref[c0:c0 + ROWCH, :] = acc
            lse_ref[c0:c0 + ROWCH, :] = mx + jnp.log(tot)

    spec_in = lambda off: pl.BlockSpec((None, S, LANES), lambda b, j: (off * NS + j, b, 0))
    vec = pl.BlockSpec((1, LANES), lambda b, j: (0, 0))
    out = pl.BlockSpec((S, LANES), lambda b, j: (b, j))
    return pl.pallas_call(
        body, grid=(Bl, NS), name="attn_fwd",
        in_specs=[spec_in(0), spec_in(1), spec_in(2), vec, vec,
                  pl.BlockSpec((None, 8, LANES), lambda b, j: (j, 0, 0))],
        out_specs=[out, out],
        out_shape=[jax.ShapeDtypeStruct((T, NS * LANES), F32)] * 2,
        scratch_shapes=[pltpu.VMEM((S, LANES), F32), pltpu.VMEM((S, LANES), F32),
                        pltpu.VMEM((S, LANES), MXU_DTYPE), pltpu.VMEM((PADR, LANES), MXU_DTYPE),
                        pltpu.VMEM((PADR, LANES), MXU_DTYPE),
                        pltpu.VMEM((S, LANES), F32), pltpu.VMEM((S, LANES), F32),
                        pltpu.VMEM((3, S, LANES), F32), pltpu.VMEM((3, S, LANES), F32),
                        pltpu.VMEM((3, 2 * QBLK, 2 * QBLK), F32)],
        compiler_params=_cp(("arbitrary", "arbitrary")),
    )(qkv, qkv, qkv, gq2, gk2, slopes)


def _attn_bwd(qkv, o, lse, do, gq2, gk2, slopes, *, Bl, S):
    n3, T, _ = qkv.shape
    NS = n3 // 3
    NB = S // QBLK
    PADR = S + 2 * RADIUS * DILATIONS[-1]
    QSCALE = HEAD_DIM ** -0.5

    def body(q_ref, k_ref, v_ref, o_ref, lse_ref, do_ref, gq_ref, gk_ref, slope_ref,
             dq_ref, dk_ref, dv_ref, gacc_ref,
             qh, kh, dl, qp, kp, vp, dop, lp, dlp, dqp, dkacc, dvacc, dqn, dkn, bias_ref):
        first_step = jnp.logical_and(pl.program_id(0) == 0, pl.program_id(1) == 0)

        @pl.when(first_step)
        def _():
            gacc_ref[...] = jnp.zeros_like(gacc_ref)

        mm = _head_mean_matrix()
        lane_lo = lax.broadcasted_iota(jnp.int32, (QBLK, LANES), 1) < HEAD_DIM
        _bias_tables(bias_ref, slope_ref)
        m_first, m_last = _edge_masks()
        _qk_normalize(q_ref, k_ref, gq_ref, gk_ref, qh, kh, S, mm)
        for c0 in range(0, S, ROWCH):
            dl[c0:c0 + ROWCH, :] = _head_mean(do_ref[c0:c0 + ROWCH, :] * o_ref[c0:c0 + ROWCH, :], mm) * HEAD_DIM
            dqn[c0:c0 + ROWCH, :] = jnp.zeros((ROWCH, LANES), F32)
            dkn[c0:c0 + ROWCH, :] = jnp.zeros((ROWCH, LANES), F32)
            dv_ref[c0:c0 + ROWCH, :] = jnp.zeros((ROWCH, LANES), F32)

        for pi, d in enumerate(DILATIONS):
            n = S // d
            nb = n // QBLK
            _gather_rows(qh, qp, S, d, pad=False)
            _gather_rows(kh, kp, S, d, pad=True)
            _gather_rows(v_ref, vp, S, d, pad=True)
            _gather_rows(do_ref, dop, S, d, pad=False)
            _gather_rows(lse_ref, lp, S, d, pad=False)
            _gather_rows(dl, dlp, S, d, pad=False)
            used = d * (n + 2 * RADIUS)
            for c0 in range(0, used, ROWCH):
                dkacc[c0:c0 + ROWCH, :] = jnp.zeros((ROWCH, LANES), F32)
                dvacc[c0:c0 + ROWCH, :] = jnp.zeros((ROWCH, LANES), F32)

            def blk(i, _, pi=pi, nb=nb):
                r = i // nb
                qb = i % nb
                q0 = pl.multiple_of(i * QBLK, QBLK)
                k0 = pl.multiple_of((i + r) * QBLK, QBLK)
                qs = _stack_heads(qp[pl.ds(q0, QBLK), :], lane_lo)
                dos = _stack_heads(dop[pl.ds(q0, QBLK), :], lane_lo)
                kwin = kp[pl.ds(k0, 2 * QBLK), :]
                vwin = vp[pl.ds(k0, 2 * QBLK), :]
                s = _dot_nt(qs, kwin) + bias_ref[pi]
                s = s + jnp.where(qb == 0, m_first, 0.0) + jnp.where(qb == nb - 1, m_last, 0.0)
                lblk = lp[pl.ds(q0, QBLK), :]
                dblk = dlp[pl.ds(q0, QBLK), :]
                lcol = jnp.concatenate([lblk[:, 0:1], lblk[:, HEAD_DIM:HEAD_DIM + 1]], axis=0)
                dcol = jnp.concatenate([dblk[:, 0:1], dblk[:, HEAD_DIM:HEAD_DIM + 1]], axis=0)
                p = jnp.exp(s - lcol)
                dp = _dot_nt(dos, vwin)
                ds = (p * (dp - dcol)).astype(MXU_DTYPE)
                pb = p.astype(MXU_DTYPE)
                dvacc[pl.ds(k0, 2 * QBLK), :] += _dot_tn(pb, dos)
                dkacc[pl.ds(k0, 2 * QBLK), :] += _dot_tn(ds, qs)
                dqp[pl.ds(q0, QBLK), :] = _merge_heads(_dot(ds, kwin), lane_lo)
                return 0
            lax.fori_loop(0, NB, blk, 0)
            _scatter_rows(dqp, dqn, S, d, pad=False, accumulate=True)
            _scatter_rows(dkacc, dkn, S, d, pad=True, accumulate=True)
            _scatter_rows(dvacc, dv_ref, S, d, pad=True, accumulate=True)

        gq_sum = jnp.zeros((8, LANES), F32)
        gk_sum = jnp.zeros((8, LANES), F32)
        for c0 in range(0, S, ROWCH):
            for src_ref, dn, g_ref, dst_ref, scale, is_q in ((q_ref, dqn, gq_ref, dq_ref, QSCALE, True),
                                                             (k_ref, dkn, gk_ref, dk_ref, 1.0, False)):
                x = src_ref[c0:c0 + ROWCH, :]
                dh = dn[c0:c0 + ROWCH, :]
                rr = lax.rsqrt(_head_mean(x * x, mm) + EPS)
                e = dh * (g_ref[...] * scale)
                dst_ref[c0:c0 + ROWCH, :] = rr * e - x * (rr * rr * rr) * _head_mean(e * x, mm)
                gpart = dh * (x * rr * scale)
                acc8 = gpart[0:8, :]
                for q8 in range(1, ROWCH // 8):
                    acc8 = acc8 + gpart[8 * q8:8 * q8 + 8, :]
                if is_q:
                    gq_sum = gq_sum + acc8
                else:
                    gk_sum = gk_sum + acc8
        gacc_ref[0:1, :] += jnp.sum(gq_sum, axis=0, keepdims=True)
        gacc_ref[1:2, :] += jnp.sum(gk_sum, axis=0, keepdims=True)

    spec_in = lambda off: pl.BlockSpec((None, S, LANES), lambda b, j: (off * NS + j, b, 0))
    tok = pl.BlockSpec((S, LANES), lambda b, j: (b, j))
    vec = pl.BlockSpec((1, LANES), lambda b, j: (0, 0))
    slab_out = pl.BlockSpec((None, S, LANES), lambda b, j: (j, b, 0))
    f32buf = lambda rows: pltpu.VMEM((rows, LANES), F32)
    bfbuf = lambda rows: pltpu.VMEM((rows, LANES), MXU_DTYPE)
    return pl.pallas_call(
        body, grid=(Bl, NS), name="attn_bwd",
        in_specs=[spec_in(0), spec_in(1), spec_in(2), tok, tok, tok, vec, vec,
                  pl.BlockSpec((None, 8, LANES), lambda b, j: (j, 0, 0))],
        out_specs=[slab_out, slab_out, slab_out, pl.BlockSpec((8, LANES), lambda b, j: (0, 0))],
        out_shape=[jax.ShapeDtypeStruct((NS, T, LANES), F32)] * 3 + [jax.ShapeDtypeStruct((8, LANES), F32)],
        scratch_shapes=[f32buf(S), f32buf(S), f32buf(S),
                        bfbuf(S), bfbuf(PADR), bfbuf(PADR), bfbuf(S),
                        f32buf(S), f32buf(S), f32buf(S), f32buf(PADR), f32buf(PADR),
                        f32buf(S), f32buf(S),
                        pltpu.VMEM((3, 2 * QBLK, 2 * QBLK), F32)],
        compiler_params=_cp(("arbitrary", "arbitrary")),
    )(qkv, qkv, qkv, o, lse, do, gq2, gk2, slopes)


def _layer_norm_parts(cv, g_ln, b_ln):
    mu = jnp.mean(cv, axis=-1, keepdims=True)
    cen = cv - mu
    rs = lax.rsqrt(jnp.mean(cen * cen, axis=-1, keepdims=True) + EPS)
    z = cen * rs
    return z, rs, z * g_ln + b_ln


def _ffn_fwd(x2, cv, ya, tgt, mod, g_ln, b_ln, g_ffn, w_out, w_gate, w_up, w_down, *, S, tm):
    T, D = x2.shape
    DC = cv.shape[1]
    P, Kb, _ = w_out.shape
    Fb = w_gate.shape[2]
    tps = S // tm

    def body(x_ref, cv_ref, ya_ref, t_ref, mod_ref, gln_ref, bln_ref, gf_ref, wo_hbm, wg_hbm, wu_hbm, wd_hbm,
             x1_ref, ycat_ref, mix_ref, h2_ref, g_ref, u_ref, a_ref, f_ref, dy_ref, loss_ref,
             wo, wg, wu, wd, sems):
        i = pl.program_id(0)
        _load_resident(i, [(wo_hbm, wo), (wg_hbm, wg), (wu_hbm, wu), (wd_hbm, wd)], sems)

        @pl.when(i == 0)
        def _():
            loss_ref[...] = jnp.zeros_like(loss_ref)

        _, _, ln = _layer_norm_parts(cv_ref[...], gln_ref[...], bln_ref[...])
        yc = ln * _sigmoid(ln)
        ycat = jnp.concatenate([yc, ya_ref[...]], axis=1).astype(MXU_DTYPE)
        ycat_ref[...] = ycat
        mix = _dot(ycat[:, 0:Kb], wo[0])
        for p in range(1, P):
            mix = mix + _dot(ycat[:, Kb * p:Kb * (p + 1)], wo[p])
        mix_ref[...] = mix.astype(ACT_DTYPE)
        x1 = x_ref[...] + mod_ref[:, 2 * D:3 * D] * mix
        x1_ref[...] = x1
        r2 = lax.rsqrt(jnp.mean(x1 * x1, axis=-1, keepdims=True) + EPS)
        h2 = (x1 * r2 * gf_ref[...]) * (1.0 + mod_ref[:, 4 * D:5 * D]) + mod_ref[:, 3 * D:4 * D]
        h2b = h2.astype(MXU_DTYPE)
        h2_ref[...] = h2b
        f = jnp.zeros((tm, D), F32)
        for p in range(P):
            g = _dot(h2b, wg[p])
            u = _dot(h2b, wu[p])
            a = (g * _sigmoid(g) * u).astype(MXU_DTYPE)
            g_ref[p] = g.astype(ACT_DTYPE)
            u_ref[p] = u.astype(ACT_DTYPE)
            a_ref[p] = a
            f = f + _dot(a, wd[p])
        f_ref[...] = f.astype(ACT_DTYPE)
        err = x1 + mod_ref[:, 5 * D:6 * D] * f - t_ref[...]
        dy_ref[...] = err * (1.0 / D)
        tot = jnp.sum(_colsum(err * err), axis=1, keepdims=True)
        loss_ref[...] += tot * (0.5 / D)

    row = lambda w: pl.BlockSpec((tm, w), lambda i: (i, 0))
    vec = lambda w: pl.BlockSpec((1, w), lambda i: (0, 0))
    blk = pl.BlockSpec((P, tm, Fb), lambda i: (0, i, 0))
    anyspec = pl.BlockSpec(memory_space=pl.ANY)
    return pl.pallas_call(
        body, grid=(T // tm,), name="ffn_fwd",
        in_specs=[row(D), row(DC), row(D - DC), row(D),
                  pl.BlockSpec((None, 1, N_MOD * D), lambda i: (i // tps, 0, 0)),
                  vec(DC), vec(DC), vec(D), anyspec, anyspec, anyspec, anyspec],
        out_specs=[row(D), row(D), row(D), row(D), blk, blk, blk, row(D), row(D),
                   pl.BlockSpec((8, LANES), lambda i: (0, 0))],
        out_shape=[jax.ShapeDtypeStruct((T, D), F32), jax.ShapeDtypeStruct((T, D), MXU_DTYPE),
                   jax.ShapeDtypeStruct((T, D), ACT_DTYPE), jax.ShapeDtypeStruct((T, D), MXU_DTYPE),
                   jax.ShapeDtypeStruct((P, T, Fb), ACT_DTYPE), jax.ShapeDtypeStruct((P, T, Fb), ACT_DTYPE),
                   jax.ShapeDtypeStruct((P, T, Fb), MXU_DTYPE), jax.ShapeDtypeStruct((T, D), ACT_DTYPE),
                   jax.ShapeDtypeStruct((T, D), F32), jax.ShapeDtypeStruct((8, LANES), F32)],
        scratch_shapes=[pltpu.VMEM(w_out.shape, w_out.dtype), pltpu.VMEM(w_gate.shape, w_gate.dtype),
                        pltpu.VMEM(w_up.shape, w_up.dtype), pltpu.VMEM(w_down.shape, w_down.dtype),
                        pltpu.SemaphoreType.DMA((4,))],
        compiler_params=_cp(("arbitrary",)),
    )(x2, cv, ya, tgt, mod, g_ln, b_ln, g_ffn, w_out, w_gate, w_up, w_down)


def _ffn_bwd(dy, x1, gs, us, fo, mixb, cv, mod, g_ln, b_ln, g_ffn, w_out, w_gate, w_up, w_down, *, S, tm):
    T, D = dy.shape
    DC = cv.shape[1]
    P, Kb, _ = w_out.shape
    Fb = w_gate.shape[2]
    tps = S // tm
    Bl = T // S

    def body(dy_ref, x1_ref, g_ref, u_ref, f_ref, mix_ref, cv_ref, mod_ref, gln_ref, bln_ref, gf_ref,
             wo_hbm, wg_hbm, wu_hbm, wd_hbm,
             dg_ref, du_ref, df_ref, dx1_ref, dmix_ref, dya_ref, dcv_ref, macc_ref, gacc_ref, lacc_ref,
             wo, wg, wu, wd, sems):
        i = pl.program_id(0)
        _load_resident(i, [(wo_hbm, wo), (wg_hbm, wg), (wu_hbm, wu), (wd_hbm, wd)], sems)

        @pl.when(i == 0)
        def _():
            gacc_ref[...] = jnp.zeros_like(gacc_ref)
            lacc_ref[...] = jnp.zeros_like(lacc_ref)

        @pl.when(i % tps == 0)
        def _():
            macc_ref[...] = jnp.zeros_like(macc_ref)

        dy_t = dy_ref[...]
        x1 = x1_ref[...]
        gate_f = mod_ref[:, 5 * D:6 * D]
        macc_ref[2:3, :] += _colsum(dy_t * f_ref[...].astype(F32))
        dfb = (dy_t * gate_f).astype(MXU_DTYPE)
        df_ref[...] = dfb
        dh2 = jnp.zeros((tm, D), F32)
        for p in range(P):
            da = _dot_nt(dfb, wd[p])
            g = g_ref[p].astype(F32)
            u = u_ref[p].astype(F32)
            sg = _sigmoid(g)
            dgp = (da * u * (sg * (1.0 + g * (1.0 - sg)))).astype(MXU_DTYPE)
            dup = (da * (g * sg)).astype(MXU_DTYPE)
            dg_ref[p] = dgp
            du_ref[p] = dup
            dh2 = dh2 + _dot_nt(dgp, wg[p]) + _dot_nt(dup, wu[p])
        r2 = lax.rsqrt(jnp.mean(x1 * x1, axis=-1, keepdims=True) + EPS)
        xr = x1 * r2
        n2 = xr * gf_ref[...]
        macc_ref[0:1, :] += _colsum(dh2)
        macc_ref[1:2, :] += _colsum(dh2 * n2)
        dn2 = dh2 * (1.0 + mod_ref[:, 4 * D:5 * D])
        gacc_ref[0:1, :] += _colsum(dn2 * xr)
        e = dn2 * gf_ref[...]
        dx1 = dy_t + r2 * e - xr * (r2 * jnp.mean(e * xr, axis=-1, keepdims=True))
        dx1_ref[...] = dx1
        macc_ref[3:4, :] += _colsum(dx1 * mix_ref[...].astype(F32))
        dmixb = (dx1 * mod_ref[:, 2 * D:3 * D]).astype(MXU_DTYPE)
        dmix_ref[...] = dmixb
        parts = [_dot_nt(dmixb, wo[p]) for p in range(P)]
        dycat = jnp.concatenate(parts, axis=1) if P > 1 else parts[0]
        dya_ref[...] = dycat[:, DC:]
        dyc = dycat[:, :DC]
        z, rs, ln = _layer_norm_parts(cv_ref[...], gln_ref[...], bln_ref[...])
        sg = _sigmoid(ln)
        dln = dyc * (sg * (1.0 + ln * (1.0 - sg)))
        lacc_ref[0:1, :] += _colsum(dln * z)
        lacc_ref[1:2, :] += _colsum(dln)
        dz = dln * gln_ref[...]
        dcv_ref[...] = rs * (dz - jnp.mean(dz, axis=-1, keepdims=True) - z * jnp.mean(dz * z, axis=-1, keepdims=True))

    row = lambda w: pl.BlockSpec((tm, w), lambda i: (i, 0))
    vec = lambda w: pl.BlockSpec((1, w), lambda i: (0, 0))
    blk = pl.BlockSpec((P, tm, Fb), lambda i: (0, i, 0))
    anyspec = pl.BlockSpec(memory_space=pl.ANY)
    return pl.pallas_call(
        body, grid=(T // tm,), name="ffn_bwd",
        in_specs=[row(D), row(D), blk, blk, row(D), row(D), row(DC),
                  pl.BlockSpec((None, 1, N_MOD * D), lambda i: (i // tps, 0, 0)),
                  vec(DC), vec(DC), vec(D), anyspec, anyspec, anyspec, anyspec],
        out_specs=[blk, blk, row(D), row(D), row(D), row(D - DC), row(DC),
                   pl.BlockSpec((None, 8, D), lambda i: (i // tps, 0, 0)),
                   pl.BlockSpec((8, D), lambda i: (0, 0)), pl.BlockSpec((8, DC), lambda i: (0, 0))],
        out_shape=[jax.ShapeDtypeStruct((P, T, Fb), MXU_DTYPE), jax.ShapeDtypeStruct((P, T, Fb), MXU_DTYPE),
                   jax.ShapeDtypeStruct((T, D), MXU_DTYPE), jax.ShapeDtypeStruct((T, D), F32),
                   jax.ShapeDtypeStruct((T, D), MXU_DTYPE), jax.ShapeDtypeStruct((T, D - DC), F32),
                   jax.ShapeDtypeStruct((T, DC), F32), jax.ShapeDtypeStruct((Bl, 8, D), F32),
                   jax.ShapeDtypeStruct((8, D), F32), jax.ShapeDtypeStruct((8, DC), F32)],
        scratch_shapes=[pltpu.VMEM(w_out.shape, w_out.dtype), pltpu.VMEM(w_gate.shape, w_gate.dtype),
                        pltpu.VMEM(w_up.shape, w_up.dtype), pltpu.VMEM(w_down.shape, w_down.dtype),
                        pltpu.SemaphoreType.DMA((4,))],
        compiler_params=_cp(("arbitrary",)),
    )(dy, x1, gs, us, fo, mixb, cv, mod, g_ln, b_ln, g_ffn, w_out, w_gate, w_up, w_down)


def _in_bwd(da, dg, dq, dk, dv, x2, dx1, mod, g_mix, w_in, *, S, tm):
    T, D = x2.shape
    P, _, Nb = w_in.shape
    DC = da.shape[1]
    NS = dq.shape[0]
    n_in = P * Nb
    tps = S // tm
    Bl = T // S

    def body(da_ref, dg_ref, dq_ref, dk_ref, dv_ref, x_ref, dx1_ref, mod_ref, g_ref, w_ref,
             dx_ref, dproj_ref, macc_ref, gacc_ref):
        i = pl.program_id(0)

        @pl.when(i == 0)
        def _():
            gacc_ref[...] = jnp.zeros_like(gacc_ref)

        @pl.when(i % tps == 0)
        def _():
            macc_ref[...] = jnp.zeros_like(macc_ref)

        pieces = [da_ref[...], dg_ref[...]] + [r[j] for r in (dq_ref, dk_ref, dv_ref) for j in range(NS)]
        dproj = jnp.concatenate(pieces, axis=1).astype(MXU_DTYPE)
        dproj_ref[...] = dproj
        dh = _dot_nt(dproj[:, 0:Nb], w_ref[0])
        for p in range(1, P):
            dh = dh + _dot_nt(dproj[:, Nb * p:Nb * (p + 1)], w_ref[p])
        x = x_ref[...]
        r = lax.rsqrt(jnp.mean(x * x, axis=-1, keepdims=True) + EPS)
        xr = x * r
        macc_ref[0:1, :] += _colsum(dh)
        macc_ref[1:2, :] += _colsum(dh * (xr * g_ref[...]))
        dn = dh * (1.0 + mod_ref[:, D:2 * D])
        gacc_ref[0:1, :] += _colsum(dn * xr)
        e = dn * g_ref[...]
        dx_ref[...] = dx1_ref[...] + r * e - xr * (r * jnp.mean(e * xr, axis=-1, keepdims=True))

    row = lambda w: pl.BlockSpec((tm, w), lambda i: (i, 0))
    slab = pl.BlockSpec((NS, tm, LANES), lambda i: (0, i, 0))
    return pl.pallas_call(
        body, grid=(T // tm,), name="in_bwd",
        in_specs=[row(DC), row(DC), slab, slab, slab, row(D), row(D),
                  pl.BlockSpec((None, 1, N_MOD * D), lambda i: (i // tps, 0, 0)),
                  pl.BlockSpec((1, D), lambda i: (0, 0)),
                  pl.BlockSpec((P, D, Nb), lambda i: (0, 0, 0))],
        out_specs=[row(D), row(n_in), pl.BlockSpec((None, 8, D), lambda i: (i // tps, 0, 0)),
                   pl.BlockSpec((8, D), lambda i: (0, 0))],
        out_shape=[jax.ShapeDtypeStruct((T, D), F32), jax.ShapeDtypeStruct((T, n_in), MXU_DTYPE),
                   jax.ShapeDtypeStruct((Bl, 8, D), F32), jax.ShapeDtypeStruct((8, D), F32)],
        compiler_params=_cp(("arbitrary",)),
    )(da, dg, dq, dk, dv, x2, dx1, mod, g_mix, w_in)


def _wgrad(a, b, *, P, name, tk, split=None):
    a_blk, b_blk = a.ndim == 3, b.ndim == 3
    T = a.shape[-2]
    if a_blk:
        R, C = a.shape[2], b.shape[1]
        a_of = lambda av, p: av[p]
        b_of = lambda bv, p: bv[...]
    elif b_blk:
        R, C = a.shape[1], b.shape[2]
        a_of = lambda av, p: av[...]
        b_of = lambda bv, p: bv[p]
    elif split == "a":
        R, C = a.shape[1] // P, b.shape[1]
        a_of = lambda av, p: av[:, R * p:R * (p + 1)]
        b_of = lambda bv, p: bv[...]
    else:
        R, C = a.shape[1], b.shape[1] // P
        a_of = lambda av, p: av[...]
        b_of = lambda bv, p: bv[:, C * p:C * (p + 1)]

    def body(a_ref, b_ref, o_ref):
        @pl.when(pl.program_id(0) == 0)
        def _():
            o_ref[...] = jnp.zeros_like(o_ref)
        for p in range(P):
            o_ref[p] += _dot_tn(a_of(a_ref, p), b_of(b_ref, p))

    def spec(v):
        if v.ndim == 3:
            return pl.BlockSpec((P, tk, v.shape[2]), lambda k: (0, k, 0))
        return pl.BlockSpec((tk, v.shape[1]), lambda k: (k, 0))

    return pl.pallas_call(
        body, grid=(T // tk,), name=name,
        in_specs=[spec(a), spec(b)],
        out_specs=pl.BlockSpec((P, R, C), lambda k: (0, 0, 0)),
        out_shape=jax.ShapeDtypeStruct((P, R, C), F32),
        compiler_params=_cp(("arbitrary",)),
    )(a, b)


TM_IN = 512
TM_FFN = 256
TK_WGRAD = 512


def _alibi_slabs(n_slab):
    heads = 2 * n_slab
    slopes = 2.0 ** (-8.0 * np.arange(1, heads + 1) / heads)
    return jnp.asarray(np.broadcast_to(np.repeat(slopes.reshape(n_slab, 1, 2), HEAD_DIM, axis=2), (n_slab, 8, LANES)),
                       dtype=F32)


def _local_step(x, tgt, mod, g_mix, wdw, g_ln, b_ln, g_q, g_k, g_ffn, w_in, w_out, w_gate, w_up, w_down):
    Bl, S, D = x.shape
    T = Bl * S
    DC = g_ln.shape[1]
    P = w_in.shape[0]
    n_slab = (D - DC) // LANES
    x2 = x.reshape(T, D)
    t2 = tgt.reshape(T, D)
    mod3 = mod.reshape(Bl, 1, N_MOD * D)
    gq2 = jnp.tile(g_q, (1, LANES // HEAD_DIM))
    gk2 = jnp.tile(g_k, (1, LANES // HEAD_DIM))
    slopes = _alibi_slabs(n_slab)

    ag, qkv, h1 = _fwd_in(x2, mod3, g_mix, w_in, S=S, tm=TM_IN, n_ag=2 * DC)
    cv = _conv_fwd(ag, wdw, Bl=Bl, S=S, DC=DC)
    ya, lse = _attn_fwd(qkv, gq2, gk2, slopes, Bl=Bl, S=S)
    x1, ycat, mixb, h2, gs, us, acts, fo, dy, lossb = _ffn_fwd(
        x2, cv, ya, t2, mod3, g_ln, b_ln, g_ffn, w_out, w_gate, w_up, w_down, S=S, tm=TM_FFN)
    dgs, dus, dfb, dx1, dmixb, dya, dcv, macc_f, gacc_f, lacc = _ffn_bwd(
        dy, x1, gs, us, fo, mixb, cv, mod3, g_ln, b_ln, g_ffn, w_out, w_gate, w_up, w_down, S=S, tm=TM_FFN)
    dq, dk, dv, gqk = _attn_bwd(qkv, ya, lse, dya, gq2, gk2, slopes, Bl=Bl, S=S)
    da, dg, dwdw = _conv_bwd(ag, dcv, wdw, Bl=Bl, S=S, DC=DC)
    dx, dprojb, macc_m, gacc_m = _in_bwd(da, dg, dq, dk, dv, x2, dx1, mod3, g_mix, w_in, S=S, tm=TM_IN)

    grads = dict(
        w_in=_wgrad(h1, dprojb, P=P, name="wgrad_in", tk=TK_WGRAD, split="b"),
        w_out=_wgrad(ycat, dmixb, P=P, name="wgrad_out", tk=TK_WGRAD, split="a"),
        w_gate=_wgrad(h2, dgs, P=P, name="wgrad_gate", tk=TK_WGRAD),
        w_up=_wgrad(h2, dus, P=P, name="wgrad_up", tk=TK_WGRAD),
        w_down=_wgrad(acts, dfb, P=P, name="wgrad_down", tk=TK_WGRAD),
    )
    packed = _pack_small(macc_m, macc_f, gacc_m, gacc_f, lacc, gqk, dwdw)
    return dict(loss=lossb[0, 0], dx=dx.reshape(Bl, S, D), grads=grads, packed=packed)


def _small_layout(Bl):
    return 8 * Bl, 8 * Bl + 8, 8 * Bl + 8 + CONV_ROWS


def _pack_small(macc_m, macc_f, gacc_m, gacc_f, lacc, gqk, dwdw):
    Bl, _, D = macc_m.shape
    DC = lacc.shape[1]
    assert 2 * DC <= D
    SMALL_GAIN_ROW, SMALL_TAP_ROW, SMALL_ROWS = _small_layout(Bl)

    def body(mm_ref, mf_ref, gm_ref, gf_ref, la_ref, qk_ref, dw_ref, o_ref):
        o_ref[...] = jnp.zeros_like(o_ref)
        for b in range(Bl):
            o_ref[8 * b + 0:8 * b + 2, :] = mm_ref[b, 0:2, :]
            o_ref[8 * b + 2:8 * b + 3, :] = mf_ref[b, 3:4, :]
            o_ref[8 * b + 3:8 * b + 6, :] = mf_ref[b, 0:3, :]
        r = SMALL_GAIN_ROW
        o_ref[r:r + 1, :] = gm_ref[0:1, :]
        o_ref[r + 1:r + 2, :] = gf_ref[0:1, :]
        o_ref[r + 2:r + 3, 0:DC] = la_ref[0:1, :]
        o_ref[r + 2:r + 3, DC:2 * DC] = la_ref[1:2, :]
        qk = qk_ref[0:2, 0:HEAD_DIM] + qk_ref[0:2, HEAD_DIM:2 * HEAD_DIM]
        o_ref[r + 3:r + 4, 0:HEAD_DIM] = qk[0:1, :]
        o_ref[r + 3:r + 4, HEAD_DIM:2 * HEAD_DIM] = qk[1:2, :]
        o_ref[SMALL_TAP_ROW:SMALL_TAP_ROW + CONV_ROWS, 0:DC] = dw_ref[...]

    return pl.pallas_call(body, name="pack_small", out_shape=jax.ShapeDtypeStruct((SMALL_ROWS, D), F32),
                          compiler_params=_cp())(macc_m, macc_f, gacc_m, gacc_f, lacc, gqk, dwdw)


def _row_tile(rows, cap=512):
    if rows <= cap:
        return rows
    best = rows
    for t in range(8, cap + 1, 8):
        if rows % t == 0:
            best = t
    return best


def _cast_weight(w, name):
    def body(w_ref, o_ref):
        o_ref[...] = w_ref[...].astype(MXU_DTYPE)
    R, C = w.shape
    tr = _row_tile(R)
    return pl.pallas_call(
        body, grid=(R // tr,), name=name,
        in_specs=[pl.BlockSpec((tr, C), lambda i: (i, 0))],
        out_specs=pl.BlockSpec((tr, C), lambda i: (i, 0)),
        out_shape=jax.ShapeDtypeStruct((R, C), MXU_DTYPE),
    )(w)


def _pair_add(g, recv, cidx, name):
    P, R, C = g.shape
    R2 = R // 2

    def body(c_ref, g_ref, r_ref, o_ref, ob_ref):
        s = g_ref[...] + r_ref[...]
        o_ref[...] = s
        ob_ref[...] = s.astype(jnp.bfloat16)

    return pl.pallas_call(
        body, name=name,
        grid_spec=pltpu.PrefetchScalarGridSpec(
            num_scalar_prefetch=1, grid=(P,),
            in_specs=[pl.BlockSpec((None, R2, C), lambda p, c: (p, c[0], 0)),
                      pl.BlockSpec((None, R2, C), lambda p, c: (p, 0, 0))],
            out_specs=[pl.BlockSpec((None, R2, C), lambda p, c: (p, 0, 0)),
                       pl.BlockSpec((None, R2, C), lambda p, c: (p, 0, 0))]),
        out_shape=[jax.ShapeDtypeStruct((P, R2, C), F32), jax.ShapeDtypeStruct((P, R2, C), jnp.bfloat16)],
    )(cidx, g, recv)


def _final_add(chipsum, recv, pidx, name):
    P, R2, C = chipsum.shape

    def body(p_ref, s_ref, r_ref, o_ref):
        acc = s_ref[...]
        for k in range(3):
            acc = acc + r_ref[k].astype(F32)
        o_ref[...] = acc

    return pl.pallas_call(
        body, name=name,
        grid_spec=pltpu.PrefetchScalarGridSpec(
            num_scalar_prefetch=1, grid=(1,),
            in_specs=[pl.BlockSpec((None, R2, C), lambda i, p: (p[0], 0, 0)),
                      pl.BlockSpec((3, R2, C), lambda i, p: (0, 0, 0))],
            out_specs=pl.BlockSpec((R2, C), lambda i, p: (0, 0))),
        out_shape=jax.ShapeDtypeStruct((R2, C), F32),
    )(pidx, chipsum, recv)


def _adamw(w, g, m, v, name):
    R, C = w.shape
    tr = _row_tile(R, 256)
    c1 = 1.0 - ADAM_B1 ** ADAM_STEP
    c2 = 1.0 - ADAM_B2 ** ADAM_STEP

    def body(w_ref, g_ref, m_ref, v_ref, d_ref, nm_ref, nv_ref):
        gg = g_ref[...]
        nm = ADAM_B1 * m_ref[...] + (1.0 - ADAM_B1) * gg
        nv = ADAM_B2 * v_ref[...] + (1.0 - ADAM_B2) * (gg * gg)
        nm_ref[...] = nm
        nv_ref[...] = nv
        d_ref[...] = -ADAM_LR * ((nm / c1) / (jnp.sqrt(nv / c2) + ADAM_EPS) + ADAM_WD * w_ref[...])

    spec = pl.BlockSpec((tr, C), lambda i: (i, 0))
    return pl.pallas_call(
        body, grid=(R // tr,), name=name,
        in_specs=[spec] * 4, out_specs=[spec] * 3,
        out_shape=[jax.ShapeDtypeStruct((R, C), F32)] * 3,
    )(w, g, m, v)


def _ada_fwd(c_all, w_ada, b_cols):
    def body(c_ref, w_ref, b_ref, o_ref):
        c = c_ref[...]
        o_ref[...] = jnp.dot(c * _sigmoid(c), w_ref[...], preferred_element_type=F32, precision=HIGHEST) + b_ref[...]
    return pl.pallas_call(
        body, name="ada_fwd", out_shape=jax.ShapeDtypeStruct((c_all.shape[0], w_ada.shape[1]), F32),
        compiler_params=_cp(),
    )(c_all, w_ada, b_cols)


def _ada_bwd(c_all, dmod_cols):
    def body(c_ref, d_ref, o_ref):
        c = c_ref[...]
        o_ref[...] = lax.dot_general(c * _sigmoid(c), d_ref[...], (((0,), (0,)), ((), ())),
                                     preferred_element_type=F32, precision=HIGHEST)
    return pl.pallas_call(
        body, name="ada_bwd", out_shape=jax.ShapeDtypeStruct((c_all.shape[1], dmod_cols.shape[1]), F32),
        compiler_params=_cp(),
    )(c_all, dmod_cols)


def _small_reduce(gathered, n_dev, Bl):
    mod_rows, _, rows = _small_layout(Bl)
    width = gathered.shape[1]

    def body(g_ref, red_ref, bada_ref):
        acc = g_ref[0:rows, :]
        for d in range(1, n_dev):
            acc = acc + g_ref[d * rows:(d + 1) * rows, :]
        red_ref[...] = acc[mod_rows:, :]
        b = acc[0:8, :]
        for q in range(1, Bl):
            b = b + acc[8 * q:8 * q + 8, :]
        bada_ref[...] = b
    return pl.pallas_call(
        body, name="small_reduce",
        out_shape=[jax.ShapeDtypeStruct((rows - mod_rows, width), F32), jax.ShapeDtypeStruct((8, width), F32)],
        compiler_params=_cp(),
    )(gathered)


def _mesh_pos():
    return lax.axis_index("x"), lax.axis_index("y"), lax.axis_index("c")


def _other_chips(x, y):
    return [(1 - x, y), (x, 1 - y), (1 - x, 1 - y)]


def _allgather8(xs, name):
    m_per, n = xs.shape

    def body(x_ref, out_ref, send_sems, recv_sems, local_sem):
        x, y, c = _mesh_pos()
        me, sibling = (x, y, c), (x, y, 1 - c)
        chips = _other_chips(x, y)

        def rows(px, py, pc):
            return out_ref.at[pl.ds((4 * px + 2 * py + pc) * m_per, m_per), :]

        def copy(k, block, to, src=None):
            return pltpu.make_async_remote_copy(
                src_ref=rows(*block) if src is None else src, dst_ref=rows(*block),
                send_sem=send_sems.at[k], recv_sem=recv_sems.at[k], device_id=to, device_id_type=MESH_DEV)

        mine = pltpu.make_async_copy(x_ref, rows(*me), local_sem)
        mine.start()
        first = [copy(0, me, sibling, src=x_ref)]
        first += [copy(1 + j, me, (*chip, c), src=x_ref) for j, chip in enumerate(chips)]
        for cp in first:
            cp.start()
        passed = [copy(4 + j, (*chip, c), sibling) for j, chip in enumerate(chips)]
        for j, chip in enumerate(chips):
            copy(1 + j, (*chip, c), me).wait_recv()
            passed[j].start()
        copy(0, sibling, me).wait_recv()
        for j, chip in enumerate(chips):
            copy(4 + j, (*chip, 1 - c), me).wait_recv()
        for cp in first + passed:
            cp.wait_send()
        mine.wait()

    return pl.pallas_call(
        body, name=name, out_shape=jax.ShapeDtypeStruct((8 * m_per, n), xs.dtype),
        in_specs=[pl.BlockSpec(memory_space=pltpu.VMEM)], out_specs=pl.BlockSpec(memory_space=pltpu.VMEM),
        scratch_shapes=[pltpu.SemaphoreType.DMA((7,)), pltpu.SemaphoreType.DMA((7,)), pltpu.SemaphoreType.DMA],
        compiler_params=_cp(),
    )(xs)


def _gather_weights(shards):
    n = len(shards)

    def body(*refs):
        ins, outs = refs[:n], refs[n:2 * n]
        send_sems, recv_sems, local_sems = refs[2 * n:]
        x, y, c = _mesh_pos()
        p = 2 * x + y
        sibling = (x, y, 1 - c)
        chips = _other_chips(x, y)

        def half(w, slot, h):
            r2 = shards[w].shape[0] // 2
            return outs[w].at[slot, pl.ds(h * r2, r2), :]

        def copy(w, k, slot, h, to, src=None):
            dst = half(w, slot, h)
            return pltpu.make_async_remote_copy(
                src_ref=dst if src is None else src, dst_ref=dst,
                send_sem=send_sems.at[6 * w + k], recv_sem=recv_sems.at[6 * w + k],
                device_id=to, device_id_type=MESH_DEV)

        local = [pltpu.make_async_copy(ins[w], outs[w].at[p], local_sems.at[w]) for w in range(n)]
        for cp in local:
            cp.start()
        sent = []
        for w in range(n):
            r2 = shards[w].shape[0] // 2
            mine = ins[w].at[pl.ds(c * r2, r2), :]
            for k, chip in enumerate(chips):
                sent.append(copy(w, k, p, c, (*chip, c), src=mine))
                sent[-1].start()
        for w in range(n):
            for k, chip in enumerate(chips):
                slot = 2 * chip[0] + chip[1]
                copy(w, k, slot, c, sibling).wait_recv()
                sent.append(copy(w, 3 + k, slot, c, sibling))
                sent[-1].start()
        for w in range(n):
            for k, chip in enumerate(chips):
                copy(w, 3 + k, 2 * chip[0] + chip[1], 1 - c, sibling).wait_recv()
        for cp in sent:
            cp.wait_send()
        for cp in local:
            cp.wait()

    anyspec = pl.BlockSpec(memory_space=pl.ANY)
    return pl.pallas_call(
        body, name="gather_weights",
        out_shape=[jax.ShapeDtypeStruct((4,) + s.shape, s.dtype) for s in shards],
        in_specs=[anyspec] * n, out_specs=[anyspec] * n,
        scratch_shapes=[pltpu.SemaphoreType.DMA((6 * n,)), pltpu.SemaphoreType.DMA((6 * n,)),
                        pltpu.SemaphoreType.DMA((n,))],
    )(*shards)


def _rs_sibling(grads):
    n = len(grads)

    def body(*refs):
        ins, outs = refs[:n], refs[n:2 * n]
        send_sems, recv_sems = refs[2 * n:]
        x, y, c = _mesh_pos()
        cps = []
        for w in range(n):
            P, R, _ = grads[w].shape
            r2 = R // 2
            for p in range(P):
                cps.append(pltpu.make_async_remote_copy(
                    src_ref=ins[w].at[p, pl.ds((1 - c) * r2, r2), :], dst_ref=outs[w].at[p],
                    send_sem=send_sems.at[4 * w + p], recv_sem=recv_sems.at[4 * w + p],
                    device_id=(x, y, 1 - c), device_id_type=MESH_DEV))
                cps[-1].start()
        for cp in cps:
            cp.wait()

    anyspec = pl.BlockSpec(memory_space=pl.ANY)
    return pl.pallas_call(
        body, name="rs_sibling",
        out_shape=[jax.ShapeDtypeStruct((g.shape[0], g.shape[1] // 2, g.shape[2]), g.dtype) for g in grads],
        in_specs=[anyspec] * n, out_specs=[anyspec] * n,
        scratch_shapes=[pltpu.SemaphoreType.DMA((4 * n,)), pltpu.SemaphoreType.DMA((4 * n,))],
    )(*grads)


def _rs_chips(sums):
    n = len(sums)

    def body(*refs):
        ins, outs = refs[:n], refs[n:2 * n]
        send_sems, recv_sems = refs[2 * n:]
        x, y, c = _mesh_pos()
        cps = []
        for w in range(n):
            for k, chip in enumerate(_other_chips(x, y)):
                cps.append(pltpu.make_async_remote_copy(
                    src_ref=ins[w].at[2 * chip[0] + chip[1]], dst_ref=outs[w].at[k],
                    send_sem=send_sems.at[3 * w + k], recv_sem=recv_sems.at[3 * w + k],
                    device_id=(*chip, c), device_id_type=MESH_DEV))
                cps[-1].start()
        for cp in cps:
            cp.wait()

    anyspec = pl.BlockSpec(memory_space=pl.ANY)
    return pl.pallas_call(
        body, name="rs_chips",
        out_shape=[jax.ShapeDtypeStruct((3,) + s.shape[1:], s.dtype) for s in sums],
        in_specs=[anyspec] * n, out_specs=[anyspec] * n,
        scratch_shapes=[pltpu.SemaphoreType.DMA((3 * n,)), pltpu.SemaphoreType.DMA((3 * n,))],
    )(*sums)


def _rs_final(halves):
    n = len(halves)

    def body(*refs):
        ins, outs = refs[:n], refs[n:2 * n]
        send_sems, recv_sems, local_sems = refs[2 * n:]
        x, y, c = _mesh_pos()
        cps, loc = [], []
        for w in range(n):
            r2 = halves[w].shape[0]
            dst = outs[w].at[pl.ds(c * r2, r2), :]
            loc.append(pltpu.make_async_copy(ins[w], dst, local_sems.at[w]))
            loc[-1].start()
            cps.append(pltpu.make_async_remote_copy(
                src_ref=ins[w], dst_ref=dst, send_sem=send_sems.at[w], recv_sem=recv_sems.at[w],
                device_id=(x, y, 1 - c), device_id_type=MESH_DEV))
            cps[-1].start()
        for cp in cps:
            cp.wait()
        for cp in loc:
            cp.wait()

    anyspec = pl.BlockSpec(memory_space=pl.ANY)
    return pl.pallas_call(
        body, name="rs_final",
        out_shape=[jax.ShapeDtypeStruct((2 * h.shape[0], h.shape[1]), h.dtype) for h in halves],
        in_specs=[anyspec] * n, out_specs=[anyspec] * n,
        scratch_shapes=[pltpu.SemaphoreType.DMA((n,)), pltpu.SemaphoreType.DMA((n,)), pltpu.SemaphoreType.DMA((n,))],
    )(*halves)


BIG = ("w_in", "w_out", "w_gate", "w_up", "w_down")
WEIGHTS = ("w_ada", "b_ada", "g_mix", "w_in", "w_dw", "b_dw", "g_conv_ln", "b_conv_ln", "g_q", "g_k",
           "w_out", "g_ffn", "w_gate", "w_up", "w_down")


def _pad_to(a, rows, cols):
    return jnp.pad(a, ((0, rows - a.shape[0]), (0, cols - a.shape[1])))


def kernel(x, c, w_ada, b_ada, g_mix, w_in, w_dw, b_dw, g_conv_ln, b_conv_ln, g_q, g_k, w_out, g_ffn, w_gate, w_up, w_down, loss_target, m_w_ada, m_b_ada, m_g_mix, m_w_in, m_w_dw, m_b_dw, m_g_conv_ln, m_b_conv_ln, m_g_q, m_g_k, m_w_out, m_g_ffn, m_w_gate, m_w_up, m_w_down, v_w_ada, v_b_ada, v_g_mix, v_w_in, v_w_dw, v_b_dw, v_g_conv_ln, v_b_conv_ln, v_g_q, v_g_k, v_w_out, v_g_ffn, v_w_gate, v_w_up, v_w_down):
    w = dict(w_ada=w_ada, b_ada=b_ada, g_mix=g_mix, w_in=w_in, w_dw=w_dw, b_dw=b_dw, g_conv_ln=g_conv_ln,
             b_conv_ln=b_conv_ln, g_q=g_q, g_k=g_k, w_out=w_out, g_ffn=g_ffn, w_gate=w_gate, w_up=w_up, w_down=w_down)
    m = dict(w_ada=m_w_ada, b_ada=m_b_ada, g_mix=m_g_mix, w_in=m_w_in, w_dw=m_w_dw, b_dw=m_b_dw, g_conv_ln=m_g_conv_ln,
             b_conv_ln=m_b_conv_ln, g_q=m_g_q, g_k=m_g_k, w_out=m_w_out, g_ffn=m_g_ffn, w_gate=m_w_gate, w_up=m_w_up,
             w_down=m_w_down)
    v = dict(w_ada=v_w_ada, b_ada=v_b_ada, g_mix=v_g_mix, w_in=v_w_in, w_dw=v_w_dw, b_dw=v_b_dw, g_conv_ln=v_g_conv_ln,
             b_conv_ln=v_b_conv_ln, g_q=v_g_q, g_k=v_g_k, w_out=v_w_out, g_ffn=v_g_ffn, w_gate=v_w_gate, w_up=v_w_up,
             w_down=v_w_down)
    Bl, S, D = x.shape
    DC = g_conv_ln.shape[1]
    NA = w_ada.shape[2]
    xi, yi, ci = _mesh_pos()
    p = 2 * xi + yi
    dev = 2 * p + ci
    n_dev = 8
    cidx = jnp.reshape(ci, (1,)).astype(jnp.int32)
    pidx = jnp.reshape(p, (1,)).astype(jnp.int32)

    first = jnp.concatenate([_pad_to(c, 8, D), _pad_to(w_dw[0], CONV_ROWS, D)], axis=0)
    g0 = _allgather8(first, "gather_cond").reshape(n_dev, 8 + CONV_ROWS, D)
    c_all = g0[:, :Bl].reshape(n_dev * Bl, D)
    taps = jnp.concatenate([g0[2 * q, 8:, :w_dw.shape[2]] for q in range(4)], axis=1)
    wdw = jnp.where(lax.broadcasted_iota(jnp.int32, taps.shape, 0) == CONV_WIDTH, b_dw, taps)
    shards = [_cast_weight(w[nm][0], "cast_" + nm) for nm in BIG]
    full = dict(zip(BIG, _gather_weights(shards)))

    b_cols = lax.dynamic_slice_in_dim(b_ada, p * NA, NA, axis=1)
    mod_part = _ada_fwd(c_all, w_ada[0], b_cols)
    gm = _allgather8(mod_part, "gather_mod").reshape(n_dev, n_dev * Bl, NA)
    mod = jnp.concatenate([lax.dynamic_slice_in_dim(gm[2 * q], dev * Bl, Bl, axis=0) for q in range(4)], axis=1)

    loc = _local_step(x, loss_target, mod, g_mix, wdw, g_conv_ln, b_conv_ln, g_q, g_k, g_ffn,
                      full["w_in"], full["w_out"], full["w_gate"], full["w_up"], full["w_down"])
    loss = lax.psum(loc["loss"], ("x", "y", "c"))

    parts = [loc["grads"][nm] for nm in BIG]
    from_sib = _rs_sibling(parts)
    sums = [_pair_add(g, r, cidx, "pair_add_" + nm) for nm, g, r in zip(BIG, parts, from_sib)]
    from_chips = _rs_chips([sb for _, sb in sums])
    halves = [_final_add(s32, r, pidx, "final_add_" + nm) for nm, (s32, _), r in zip(BIG, sums, from_chips)]
    grad = dict(zip(BIG, _rs_final(halves)))

    mod_rows, _, small_rows = _small_layout(Bl)
    gs = _allgather8(loc["packed"], "gather_small")
    red, bada8 = _small_reduce(gs, n_dev, Bl)
    dmod_all = gs.reshape(n_dev, small_rows, D)[:, :mod_rows].reshape(n_dev * Bl, 8, D)[:, :N_MOD].reshape(n_dev * Bl, N_MOD * D)
    grad["w_ada"] = _ada_bwd(c_all, lax.dynamic_slice_in_dim(dmod_all, p * NA, NA, axis=1))
    grad["b_ada"] = bada8[:N_MOD].reshape(1, N_MOD * D)
    grad["g_mix"] = red[0:1]
    grad["g_ffn"] = red[1:2]
    grad["g_conv_ln"] = red[2:3, :DC]
    grad["b_conv_ln"] = red[2:3, DC:2 * DC]
    grad["g_q"] = red[3:4, :HEAD_DIM]
    grad["g_k"] = red[3:4, HEAD_DIM:2 * HEAD_DIM]
    dwdw = red[8:8 + CONV_ROWS, :DC]
    grad["w_dw"] = lax.dynamic_slice_in_dim(dwdw[:CONV_WIDTH], p * w_dw.shape[2], w_dw.shape[2], axis=1)
    grad["b_dw"] = dwdw[CONV_WIDTH:CONV_WIDTH + 1]

    delta, new_m, new_v = {}, {}, {}
    for nm in WEIGHTS:
        shp = w[nm].shape
        two_d = (shp[-2], shp[-1]) if len(shp) == 3 else shp
        d_, m_, v_ = _adamw(w[nm].reshape(two_d), grad[nm].reshape(two_d), m[nm].reshape(two_d), v[nm].reshape(two_d),
                            "adamw_" + nm)
        grad[nm] = grad[nm].reshape(shp)
        delta[nm], new_m[nm], new_v[nm] = d_.reshape(shp), m_.reshape(shp), v_.reshape(shp)

    return (loss, loc["dx"], *[grad[nm] for nm in WEIGHTS], *[delta[nm] for nm in WEIGHTS],
            *[new_m[nm] for nm in WEIGHTS], *[new_v[nm] for nm in WEIGHTS])
```

```python
import functools
import math

import jax
import jax.numpy as jnp
import numpy as np
from jax import lax
from jax.experimental import pallas as pl
from jax.experimental.pallas import tpu as pltpu

F32 = jnp.float32
MXU_DTYPE = jnp.bfloat16
ACT_DTYPE = jnp.bfloat16
EPS = 1e-6
NEG_INF = -1e30
HEAD_DIM = 64
LANES = 128
RADIUS = 64
QBLK = 128
DILATIONS = (1, 4, 16)
CONV_WIDTH = 31
CONV_PAD = CONV_WIDTH // 2
CONV_ROWS = 32
N_MOD = 6
ADAM_LR, ADAM_B1, ADAM_B2, ADAM_EPS, ADAM_WD, ADAM_STEP = 0.001, 0.9, 0.999, 1e-08, 0.01, 10
HIGHEST = lax.Precision.HIGHEST
MESH_DEV = pl.DeviceIdType.MESH
VMEM_LIMIT = 56 << 20


def _cp(sem=None, vmem=VMEM_LIMIT):
    kw = dict(vmem_limit_bytes=vmem)
    if sem is not None:
        kw["dimension_semantics"] = sem
    return pltpu.CompilerParams(**kw)


def _sigmoid(x):
    return 1.0 / (1.0 + jnp.exp(-x))


def _dot(a, b):
    return jnp.dot(a, b, preferred_element_type=F32)


def _dot_nt(a, b):
    return lax.dot_general(a, b, (((1,), (1,)), ((), ())), preferred_element_type=F32)


def _dot_tn(a, b):
    return lax.dot_general(a, b, (((0,), (0,)), ((), ())), preferred_element_type=F32)


def _colsum(v):
    return jnp.sum(v, axis=0, keepdims=True)


def _load_resident(i, pairs, sems):
    @pl.when(i == 0)
    def _():
        cps = [pltpu.make_async_copy(src, dst, sems.at[n]) for n, (src, dst) in enumerate(pairs)]
        for c in cps:
            c.start()
        for c in cps:
            c.wait()


def _fwd_in(x2, mod, g_mix, w_in, *, S, tm, n_ag):
    T, D = x2.shape
    P, _, Nb = w_in.shape
    n_in = P * Nb
    n_slab = (n_in - n_ag) // LANES
    tps = S // tm

    def body(x_ref, mod_ref, g_ref, w_ref, ag_ref, qkv_ref, h_ref):
        x = x_ref[...]
        r = lax.rsqrt(jnp.mean(x * x, axis=-1, keepdims=True) + EPS)
        n = x * r * g_ref[...]
        h = n * (1.0 + mod_ref[:, D:2 * D]) + mod_ref[:, 0:D]
        hb = h.astype(MXU_DTYPE)
        h_ref[...] = hb
        parts = [_dot(hb, w_ref[p]) for p in range(P)]
        proj = jnp.concatenate(parts, axis=1) if P > 1 else parts[0]
        ag_ref[...] = proj[:, :n_ag]
        for j in range(n_slab):
            qkv_ref[j] = proj[:, n_ag + LANES * j:n_ag + LANES * (j + 1)]

    return pl.pallas_call(
        body, grid=(T // tm,), name="fwd_in",
        in_specs=[pl.BlockSpec((tm, D), lambda i: (i, 0)),
                  pl.BlockSpec((None, 1, N_MOD * D), lambda i: (i // tps, 0, 0)),
                  pl.BlockSpec((1, D), lambda i: (0, 0)),
                  pl.BlockSpec((P, D, Nb), lambda i: (0, 0, 0))],
        out_specs=[pl.BlockSpec((tm, n_ag), lambda i: (i, 0)),
                   pl.BlockSpec((n_slab, tm, LANES), lambda i: (0, i, 0)),
                   pl.BlockSpec((tm, D), lambda i: (i, 0))],
        out_shape=[jax.ShapeDtypeStruct((T, n_ag), F32),
                   jax.ShapeDtypeStruct((n_slab, T, LANES), F32),
                   jax.ShapeDtypeStruct((T, D), MXU_DTYPE)],
        compiler_params=_cp(("arbitrary",)),
    )(x2, mod, g_mix, w_in)


CONV_CH = 64


def _conv_taps(win, w_ref, acc, reverse):
    n = win.shape[0]
    for b in range(8):
        wb = win if b == 0 else pltpu.roll(win, shift=n - b, axis=0)
        for a in range(4):
            o = 8 * a + b
            if o < 1 or o > CONV_WIDTH:
                continue
            k = (CONV_WIDTH - o) if reverse else (o - 1)
            acc = acc + w_ref[k:k + 1, :] * wb[8 * a:8 * a + CONV_CH, :]
    return acc


def _conv_fwd(ag, wdw, *, Bl, S, DC):
    T = ag.shape[0]
    nsc = DC // LANES
    CH = CONV_CH

    def body(a_ref, g_ref, w_ref, cv_ref, upad):
        zeros16 = jnp.zeros((16, LANES), F32)
        upad[0:16, :] = zeros16
        upad[S + 16:S + 32, :] = zeros16

        def fill(i, _):
            r0 = pl.multiple_of(i * CH, CH)
            a = a_ref[pl.ds(r0, CH), :]
            g = g_ref[pl.ds(r0, CH), :]
            upad[pl.ds(r0 + 16, CH), :] = a * _sigmoid(g)
            return 0
        lax.fori_loop(0, S // CH, fill, 0)

        def conv(i, _):
            r0 = pl.multiple_of(i * CH, CH)
            win = upad[pl.ds(r0, CH + 32), :]
            acc = jnp.zeros((CH, LANES), F32) + w_ref[CONV_WIDTH:CONV_WIDTH + 1, :]
            cv_ref[pl.ds(r0, CH), :] = _conv_taps(win, w_ref, acc, reverse=False)
            return 0
        lax.fori_loop(0, S // CH, conv, 0)

    return pl.pallas_call(
        body, grid=(Bl, nsc), name="conv_fwd",
        in_specs=[pl.BlockSpec((S, LANES), lambda b, j: (b, j)),
                  pl.BlockSpec((S, LANES), lambda b, j: (b, nsc + j)),
                  pl.BlockSpec((CONV_ROWS, LANES), lambda b, j: (0, j))],
        out_specs=pl.BlockSpec((S, LANES), lambda b, j: (b, j)),
        out_shape=jax.ShapeDtypeStruct((T, DC), F32),
        scratch_shapes=[pltpu.VMEM((S + 32, LANES), F32)],
        compiler_params=_cp(("arbitrary", "arbitrary")),
    )(ag, ag, wdw)


def _conv_bwd(ag, dcv, wdw, *, Bl, S, DC):
    T = ag.shape[0]
    nsc = DC // LANES
    CH = CONV_CH

    def body(a_ref, g_ref, d_ref, w_ref, da_ref, dg_ref, dw_ref, upad, dpad, wacc):
        b = pl.program_id(1)
        zeros16 = jnp.zeros((16, LANES), F32)
        upad[0:16, :] = zeros16
        upad[S + 16:S + 32, :] = zeros16
        dpad[0:16, :] = zeros16
        dpad[S + 16:S + 32, :] = zeros16

        @pl.when(b == 0)
        def _():
            wacc[...] = jnp.zeros_like(wacc)

        def fill(i, _):
            r0 = pl.multiple_of(i * CH, CH)
            a = a_ref[pl.ds(r0, CH), :]
            g = g_ref[pl.ds(r0, CH), :]
            upad[pl.ds(r0 + 16, CH), :] = a * _sigmoid(g)
            dpad[pl.ds(r0 + 16, CH), :] = d_ref[pl.ds(r0, CH), :]
            return 0
        lax.fori_loop(0, S // CH, fill, 0)

        def step(i, _):
            r0 = pl.multiple_of(i * CH, CH)
            dwin = dpad[pl.ds(r0, CH + 32), :]
            du = _conv_taps(dwin, w_ref, jnp.zeros((CH, LANES), F32), reverse=True)
            a = a_ref[pl.ds(r0, CH), :]
            g = g_ref[pl.ds(r0, CH), :]
            sg = _sigmoid(g)
            da_ref[pl.ds(r0, CH), :] = du * sg
            dg_ref[pl.ds(r0, CH), :] = du * a * sg * (1.0 - sg)
            dc = d_ref[pl.ds(r0, CH), :]
            uwin = upad[pl.ds(r0, CH + 32), :]
            n = CH + 32
            for bb in range(8):
                wb = uwin if bb == 0 else pltpu.roll(uwin, shift=n - bb, axis=0)
                for aa in range(4):
                    o = 8 * aa + bb
                    if o < 1 or o > CONV_WIDTH:
                        continue
                    k = o - 1
                    prod = dc * wb[8 * aa:8 * aa + CH, :]
                    part = prod[0:8, :]
                    for q in range(1, CH // 8):
                        part = part + prod[8 * q:8 * q + 8, :]
                    wacc[8 * k:8 * k + 8, :] += part
            part = dc[0:8, :]
            for q in range(1, CH // 8):
                part = part + dc[8 * q:8 * q + 8, :]
            wacc[8 * CONV_WIDTH:8 * CONV_WIDTH + 8, :] += part
            return 0
        lax.fori_loop(0, S // CH, step, 0)

        @pl.when(b == Bl - 1)
        def _():
            for k in range(CONV_ROWS):
                dw_ref[k:k + 1, :] = jnp.sum(wacc[8 * k:8 * k + 8, :], axis=0, keepdims=True)

    return pl.pallas_call(
        body, grid=(nsc, Bl), name="conv_bwd",
        in_specs=[pl.BlockSpec((S, LANES), lambda j, b: (b, j)),
                  pl.BlockSpec((S, LANES), lambda j, b: (b, nsc + j)),
                  pl.BlockSpec((S, LANES), lambda j, b: (b, j)),
                  pl.BlockSpec((CONV_ROWS, LANES), lambda j, b: (0, j))],
        out_specs=[pl.BlockSpec((S, LANES), lambda j, b: (b, j)),
                   pl.BlockSpec((S, LANES), lambda j, b: (b, j)),
                   pl.BlockSpec((CONV_ROWS, LANES), lambda j, b: (0, j))],
        out_shape=[jax.ShapeDtypeStruct((T, DC), F32), jax.ShapeDtypeStruct((T, DC), F32),
                   jax.ShapeDtypeStruct((CONV_ROWS, DC), F32)],
        scratch_shapes=[pltpu.VMEM((S + 32, LANES), F32), pltpu.VMEM((S + 32, LANES), F32),
                        pltpu.VMEM((8 * CONV_ROWS, LANES), F32)],
        compiler_params=_cp(("arbitrary", "arbitrary")),
    )(ag, ag, dcv, wdw)


ROWCH = 256
ATTN_UNROLL = 2


LOG2E = 1.4426950408889634
LN2 = 0.6931471805599453
N_EDGE = 4


def _head_mean_matrix():
    r = lax.broadcasted_iota(jnp.int32, (LANES, LANES), 0) // HEAD_DIM
    c = lax.broadcasted_iota(jnp.int32, (LANES, LANES), 1) // HEAD_DIM
    return jnp.where(r == c, 1.0 / HEAD_DIM, 0.0).astype(jnp.bfloat16)


def _head_mean(v, mm):
    hi = v.astype(jnp.bfloat16)
    lo = (v - hi.astype(F32)).astype(jnp.bfloat16)
    return _dot(hi, mm) + _dot(lo, mm)


def _stack_heads(blk, lane_lo):
    z = jnp.zeros_like(blk)
    return jnp.concatenate([jnp.where(lane_lo, blk, z), jnp.where(lane_lo, z, blk)], axis=0)


def _merge_heads(v2, lane_lo):
    return jnp.where(lane_lo, v2[:QBLK], v2[QBLK:])


def _bias_tables(bias_ref, slope_ref):
    row = lax.broadcasted_iota(jnp.int32, (2 * QBLK, 2 * QBLK), 0)
    col = lax.broadcasted_iota(jnp.int32, (2 * QBLK, 2 * QBLK), 1)
    rel = jnp.abs(col - RADIUS - (row % QBLK))
    slope = jnp.where(row < QBLK, slope_ref[0:1, 0:1], slope_ref[0:1, HEAD_DIM:HEAD_DIM + 1]) * LOG2E
    for pi, d in enumerate(DILATIONS):
        inside = jnp.where(rel <= RADIUS, -slope * (float(d) * rel.astype(F32)), NEG_INF)
        for e in range(N_EDGE):
            t = inside
            if e & 1:
                t = jnp.where(col < RADIUS, NEG_INF, t)
            if e & 2:
                t = jnp.where(col >= QBLK + RADIUS, NEG_INF, t)
            bias_ref[N_EDGE * pi + e] = t


def _edge_index(qb, nb):
    return jnp.where(qb == 0, 1, 0) + jnp.where(qb == nb - 1, 2, 0)


def _gather_rows(src_ref, dst_ref, S, d, pad):
    n = S // d
    seg = n + 2 * RADIUS if pad else n
    step = min(n, 512)
    for r in range(d):
        base = r * seg
        if pad:
            dst_ref[base:base + RADIUS, :] = jnp.zeros((RADIUS, LANES), dst_ref.dtype)
            dst_ref[base + RADIUS + n:base + seg, :] = jnp.zeros((RADIUS, LANES), dst_ref.dtype)
            base += RADIUS
        for c0 in range(0, n, step):
            if d == 1:
                v = src_ref[c0:c0 + step, :]
            else:
                v = src_ref[pl.ds(r + c0 * d, step, stride=d), :]
            dst_ref[base + c0:base + c0 + step, :] = v.astype(dst_ref.dtype)


def _scatter_rows(src_ref, dst_ref, S, d, pad, accumulate):
    n = S // d
    seg = n + 2 * RADIUS if pad else n
    step = min(n, 512)
    for r in range(d):
        base = r * seg + (RADIUS if pad else 0)
        for c0 in range(0, n, step):
            v = src_ref[base + c0:base + c0 + step, :]
            if d == 1:
                idx = pl.ds(c0, step)
            else:
                idx = pl.ds(r + c0 * d, step, stride=d)
            if accumulate:
                dst_ref[idx, :] = dst_ref[idx, :] + v
            else:
                dst_ref[idx, :] = v


def _qk_normalize(q_ref, k_ref, gq_ref, gk_ref, qh, kh, S, mm):
    for c0 in range(0, S, ROWCH):
        q = q_ref[c0:c0 + ROWCH, :]
        k = k_ref[c0:c0 + ROWCH, :]
        qh[c0:c0 + ROWCH, :] = q * lax.rsqrt(_head_mean(q * q, mm) + EPS) * (gq_ref[...] * (HEAD_DIM ** -0.5 * LOG2E))
        kh[c0:c0 + ROWCH, :] = k * lax.rsqrt(_head_mean(k * k, mm) + EPS) * gk_ref[...]


def _attn_fwd(qkv, gq2, gk2, slopes, *, Bl, S):
    n3, T, _ = qkv.shape
    NS = n3 // 3
    NB = S // QBLK
    PADR = S + 2 * RADIUS * DILATIONS[-1]

    def body(q_ref, k_ref, v_ref, gq_ref, gk_ref, slope_ref, o_ref, lse_ref,
             qh, kh, qp, kp, vp, op, lp, onat, lnat, bias_ref):
        mm = _head_mean_matrix()
        lane_lo = lax.broadcasted_iota(jnp.int32, (QBLK, LANES), 1) < HEAD_DIM
        _bias_tables(bias_ref, slope_ref)
        _qk_normalize(q_ref, k_ref, gq_ref, gk_ref, qh, kh, S, mm)

        for pi, d in enumerate(DILATIONS):
            n = S // d
            nb = n // QBLK
            _gather_rows(qh, qp, S, d, pad=False)
            _gather_rows(kh, kp, S, d, pad=True)
            _gather_rows(v_ref, vp, S, d, pad=True)

            def blk(i, _, pi=pi, nb=nb):
                r = i // nb
                qb = i % nb
                q0 = pl.multiple_of(i * QBLK, QBLK)
                k0 = pl.multiple_of((i + r) * QBLK, QBLK)
                qs = _stack_heads(qp[pl.ds(q0, QBLK), :], lane_lo)
                kwin = kp[pl.ds(k0, 2 * QBLK), :]
                vwin = vp[pl.ds(k0, 2 * QBLK), :]
                s = _dot_nt(qs, kwin) + bias_ref[N_EDGE * pi + _edge_index(qb, nb)]
                m = jnp.max(s, axis=1, keepdims=True)
                p = jnp.exp2(s - m)
                l = jnp.sum(p, axis=1, keepdims=True)
                o2 = _dot(p.astype(MXU_DTYPE), vwin) * (1.0 / l)
                lse2 = jnp.broadcast_to(m + jnp.log2(l), (2 * QBLK, LANES))
                op[pl.ds(q0, QBLK), :] = _merge_heads(o2, lane_lo)
                lp[pl.ds(q0, QBLK), :] = _merge_heads(lse2, lane_lo)
                return 0
            lax.fori_loop(0, NB, blk, 0, unroll=ATTN_UNROLL)
            _scatter_rows(op, onat.at[pi], S, d, pad=False, accumulate=False)
            _scatter_rows(lp, lnat.at[pi], S, d, pad=False, accumulate=False)

        for c0 in range(0, S, ROWCH):
            ls = [lnat[pi, c0:c0 + ROWCH, :] for pi in range(len(DILATIONS))]
            mx = jnp.maximum(jnp.maximum(ls[0], ls[1]), ls[2])
            es = [jnp.exp2(l - mx) for l in ls]
            tot = es[0] + es[1] + es[2]
            inv = 1.0 / tot
            acc = (es[0] * inv) * onat[0, c0:c0 + ROWCH, :]
            for pi in (1, 2):
                acc = acc + (es[pi] * inv) * onat[pi, c0:c0 + ROWCH, :]
            o_ref[c0:c0 + ROWCH, :] = acc
            lse_ref[c0:c0 + ROWCH, :] = mx + jnp.log2(tot)

    spec_in = lambda off: pl.BlockSpec((None, S, LANES), lambda b, j: (off * NS + j, b, 0))
    vec = pl.BlockSpec((1, LANES), lambda b, j: (0, 0))
    out = pl.BlockSpec((S, LANES), lambda b, j: (b, j))
    return pl.pallas_call(
        body, grid=(Bl, NS), name="attn_fwd",
        in_specs=[spec_in(0), spec_in(1), spec_in(2), vec, vec,
                  pl.BlockSpec((None, 8, LANES), lambda b, j: (j, 0, 0))],
        out_specs=[out, out],
        out_shape=[jax.ShapeDtypeStruct((T, NS * LANES), F32)] * 2,
        scratch_shapes=[pltpu.VMEM((S, LANES), F32), pltpu.VMEM((S, LANES), F32),
                        pltpu.VMEM((S, LANES), MXU_DTYPE), pltpu.VMEM((PADR, LANES), MXU_DTYPE),
                        pltpu.VMEM((PADR, LANES), MXU_DTYPE),
                        pltpu.VMEM((S, LANES), F32), pltpu.VMEM((S, LANES), F32),
                        pltpu.VMEM((3, S, LANES), F32), pltpu.VMEM((3, S, LANES), F32),
                        pltpu.VMEM((N_EDGE * len(DILATIONS), 2 * QBLK, 2 * QBLK), F32)],
        compiler_params=_cp(("arbitrary", "arbitrary")),
    )(qkv, qkv, qkv, gq2, gk2, slopes)


def _attn_bwd(qkv, o, lse, do, gq2, gk2, slopes, *, Bl, S):
    n3, T, _ = qkv.shape
    NS = n3 // 3
    NB = S // QBLK
    PADR = S + 2 * RADIUS * DILATIONS[-1]
    QSCALE = HEAD_DIM ** -0.5

    def body(q_ref, k_ref, v_ref, o_ref, lse_ref, do_ref, gq_ref, gk_ref, slope_ref,
             dq_ref, dk_ref, dv_ref, gacc_ref,
             qh, kh, dl, qp, kp, vp, dop, lp, dlp, dqp, dkacc, dvacc, dqn, dkn, bias_ref):
        first_step = jnp.logical_and(pl.program_id(0) == 0, pl.program_id(1) == 0)

        @pl.when(first_step)
        def _():
            gacc_ref[...] = jnp.zeros_like(gacc_ref)

        mm = _head_mean_matrix()
        lane_lo = lax.broadcasted_iota(jnp.int32, (QBLK, LANES), 1) < HEAD_DIM
        _bias_tables(bias_ref, slope_ref)
        _qk_normalize(q_ref, k_ref, gq_ref, gk_ref, qh, kh, S, mm)
        for c0 in range(0, S, ROWCH):
            dl[c0:c0 + ROWCH, :] = _head_mean(do_ref[c0:c0 + ROWCH, :] * o_ref[c0:c0 + ROWCH, :], mm) * HEAD_DIM
            dqn[c0:c0 + ROWCH, :] = jnp.zeros((ROWCH, LANES), F32)
            dkn[c0:c0 + ROWCH, :] = jnp.zeros((ROWCH, LANES), F32)
            dv_ref[c0:c0 + ROWCH, :] = jnp.zeros((ROWCH, LANES), F32)

        for pi, d in enumerate(DILATIONS):
            n = S // d
            nb = n // QBLK
            _gather_rows(qh, qp, S, d, pad=False)
            _gather_rows(kh, kp, S, d, pad=True)
            _gather_rows(v_ref, vp, S, d, pad=True)
            _gather_rows(do_ref, dop, S, d, pad=False)
            _gather_rows(lse_ref, lp, S, d, pad=False)
            _gather_rows(dl, dlp, S, d, pad=False)
            used = d * (n + 2 * RADIUS)
            for c0 in range(0, used, ROWCH):
                dkacc[c0:c0 + ROWCH, :] = jnp.zeros((ROWCH, LANES), F32)
                dvacc[c0:c0 + ROWCH, :] = jnp.zeros((ROWCH, LANES), F32)

            def blk(i, _, pi=pi, nb=nb):
                r = i // nb
                qb = i % nb
                q0 = pl.multiple_of(i * QBLK, QBLK)
                k0 = pl.multiple_of((i + r) * QBLK, QBLK)
                qs = _stack_heads(qp[pl.ds(q0, QBLK), :], lane_lo)
                dos = _stack_heads(dop[pl.ds(q0, QBLK), :], lane_lo)
                kwin = kp[pl.ds(k0, 2 * QBLK), :]
                vwin = vp[pl.ds(k0, 2 * QBLK), :]
                s = _dot_nt(qs, kwin) + bias_ref[N_EDGE * pi + _edge_index(qb, nb)]
                lblk = lp[pl.ds(q0, QBLK), :]
                dblk = dlp[pl.ds(q0, QBLK), :]
                lcol = jnp.concatenate([lblk[:, 0:1], lblk[:, HEAD_DIM:HEAD_DIM + 1]], axis=0)
                dcol = jnp.concatenate([dblk[:, 0:1], dblk[:, HEAD_DIM:HEAD_DIM + 1]], axis=0)
                p = jnp.exp2(s - lcol)
                dp = _dot_nt(dos, vwin)
                ds = (p * (dp - dcol)).astype(MXU_DTYPE)
                pb = p.astype(MXU_DTYPE)
                dvacc[pl.ds(k0, 2 * QBLK), :] += _dot_tn(pb, dos)
                dkacc[pl.ds(k0, 2 * QBLK), :] += _dot_tn(ds, qs)
                dqp[pl.ds(q0, QBLK), :] = _merge_heads(_dot(ds, kwin), lane_lo)
                return 0
            lax.fori_loop(0, NB, blk, 0, unroll=ATTN_UNROLL)
            _scatter_rows(dqp, dqn, S, d, pad=False, accumulate=True)
            _scatter_rows(dkacc, dkn, S, d, pad=True, accumulate=True)
            _scatter_rows(dvacc, dv_ref, S, d, pad=True, accumulate=True)

        gq_sum = jnp.zeros((8, LANES), F32)
        gk_sum = jnp.zeros((8, LANES), F32)
        for c0 in range(0, S, ROWCH):
            for src_ref, dn, g_ref, dst_ref, scale, is_q in ((q_ref, dqn, gq_ref, dq_ref, QSCALE, True),
                                                             (k_ref, dkn, gk_ref, dk_ref, LN2, False)):
                x = src_ref[c0:c0 + ROWCH, :]
                dh = dn[c0:c0 + ROWCH, :]
                rr = lax.rsqrt(_head_mean(x * x, mm) + EPS)
                e = dh * (g_ref[...] * scale)
                dst_ref[c0:c0 + ROWCH, :] = rr * e - x * (rr * rr * rr) * _head_mean(e * x, mm)
                gpart = dh * (x * rr * scale)
                acc8 = gpart[0:8, :]
                for q8 in range(1, ROWCH // 8):
                    acc8 = acc8 + gpart[8 * q8:8 * q8 + 8, :]
                if is_q:
                    gq_sum = gq_sum + acc8
                else:
                    gk_sum = gk_sum + acc8
        gacc_ref[0:1, :] += jnp.sum(gq_sum, axis=0, keepdims=True)
        gacc_ref[1:2, :] += jnp.sum(gk_sum, axis=0, keepdims=True)

    spec_in = lambda off: pl.BlockSpec((None, S, LANES), lambda b, j: (off * NS + j, b, 0))
    tok = pl.BlockSpec((S, LANES), lambda b, j: (b, j))
    vec = pl.BlockSpec((1, LANES), lambda b, j: (0, 0))
    slab_out = pl.BlockSpec((None, S, LANES), lambda b, j: (j, b, 0))
    f32buf = lambda rows: pltpu.VMEM((rows, LANES), F32)
    bfbuf = lambda rows: pltpu.VMEM((rows, LANES), MXU_DTYPE)
    return pl.pallas_call(
        body, grid=(Bl, NS), name="attn_bwd",
        in_specs=[spec_in(0), spec_in(1), spec_in(2), tok, tok, tok, vec, vec,
                  pl.BlockSpec((None, 8, LANES), lambda b, j: (j, 0, 0))],
        out_specs=[slab_out, slab_out, slab_out, pl.BlockSpec((8, LANES), lambda b, j: (0, 0))],
        out_shape=[jax.ShapeDtypeStruct((NS, T, LANES), F32)] * 3 + [jax.ShapeDtypeStruct((8, LANES), F32)],
        scratch_shapes=[f32buf(S), f32buf(S), f32buf(S),
                        bfbuf(S), bfbuf(PADR), bfbuf(PADR), bfbuf(S),
                        f32buf(S), f32buf(S), f32buf(S), f32buf(PADR), f32buf(PADR),
                        f32buf(S), f32buf(S),
                        pltpu.VMEM((N_EDGE * len(DILATIONS), 2 * QBLK, 2 * QBLK), F32)],
        compiler_params=_cp(("arbitrary", "arbitrary")),
    )(qkv, qkv, qkv, o, lse, do, gq2, gk2, slopes)


def _layer_norm_parts(cv, g_ln, b_ln):
    mu = jnp.mean(cv, axis=-1, keepdims=True)
    cen = cv - mu
    rs = lax.rsqrt(jnp.mean(cen * cen, axis=-1, keepdims=True) + EPS)
    z = cen * rs
    return z, rs, z * g_ln + b_ln


def _ffn_fwd(x2, cv, ya, tgt, mod, g_ln, b_ln, g_ffn, w_out, w_gate, w_up, w_down, *, S, tm):
    T, D = x2.shape
    DC = cv.shape[1]
    P, Kb, _ = w_out.shape
    Fb = w_gate.shape[2]
    tps = S // tm

    def body(x_ref, cv_ref, ya_ref, t_ref, mod_ref, gln_ref, bln_ref, gf_ref, wo_hbm, wg_hbm, wu_hbm, wd_hbm,
             x1_ref, ycat_ref, mix_ref, h2_ref, g_ref, u_ref, a_ref, f_ref, dy_ref, loss_ref,
             wo, wg, wu, wd, sems):
        i = pl.program_id(0)
        _load_resident(i, [(wo_hbm, wo), (wg_hbm, wg), (wu_hbm, wu), (wd_hbm, wd)], sems)

        @pl.when(i == 0)
        def _():
            loss_ref[...] = jnp.zeros_like(loss_ref)

        _, _, ln = _layer_norm_parts(cv_ref[...], gln_ref[...], bln_ref[...])
        yc = ln * _sigmoid(ln)
        ycat = jnp.concatenate([yc, ya_ref[...]], axis=1).astype(MXU_DTYPE)
        ycat_ref[...] = ycat
        mix = _dot(ycat[:, 0:Kb], wo[0])
        for p in range(1, P):
            mix = mix + _dot(ycat[:, Kb * p:Kb * (p + 1)], wo[p])
        mix_ref[...] = mix.astype(ACT_DTYPE)
        x1 = x_ref[...] + mod_ref[:, 2 * D:3 * D] * mix
        x1_ref[...] = x1
        r2 = lax.rsqrt(jnp.mean(x1 * x1, axis=-1, keepdims=True) + EPS)
        h2 = (x1 * r2 * gf_ref[...]) * (1.0 + mod_ref[:, 4 * D:5 * D]) + mod_ref[:, 3 * D:4 * D]
        h2b = h2.astype(MXU_DTYPE)
        h2_ref[...] = h2b
        f = jnp.zeros((tm, D), F32)
        for p in range(P):
            g = _dot(h2b, wg[p])
            u = _dot(h2b, wu[p])
            a = (g * _sigmoid(g) * u).astype(MXU_DTYPE)
            g_ref[p] = g.astype(ACT_DTYPE)
            u_ref[p] = u.astype(ACT_DTYPE)
            a_ref[p] = a
            f = f + _dot(a, wd[p])
        f_ref[...] = f.astype(ACT_DTYPE)
        err = x1 + mod_ref[:, 5 * D:6 * D] * f - t_ref[...]
        dy_ref[...] = err * (1.0 / D)
        tot = jnp.sum(_colsum(err * err), axis=1, keepdims=True)
        loss_ref[...] += tot * (0.5 / D)

    row = lambda w: pl.BlockSpec((tm, w), lambda i: (i, 0))
    vec = lambda w: pl.BlockSpec((1, w), lambda i: (0, 0))
    blk = pl.BlockSpec((P, tm, Fb), lambda i: (0, i, 0))
    anyspec = pl.BlockSpec(memory_space=pl.ANY)
    return pl.pallas_call(
        body, grid=(T // tm,), name="ffn_fwd",
        in_specs=[row(D), row(DC), row(D - DC), row(D),
                  pl.BlockSpec((None, 1, N_MOD * D), lambda i: (i // tps, 0, 0)),
                  vec(DC), vec(DC), vec(D), anyspec, anyspec, anyspec, anyspec],
        out_specs=[row(D), row(D), row(D), row(D), blk, blk, blk, row(D), row(D),
                   pl.BlockSpec((8, LANES), lambda i: (0, 0))],
        out_shape=[jax.ShapeDtypeStruct((T, D), F32), jax.ShapeDtypeStruct((T, D), MXU_DTYPE),
                   jax.ShapeDtypeStruct((T, D), ACT_DTYPE), jax.ShapeDtypeStruct((T, D), MXU_DTYPE),
                   jax.ShapeDtypeStruct((P, T, Fb), ACT_DTYPE), jax.ShapeDtypeStruct((P, T, Fb), ACT_DTYPE),
                   jax.ShapeDtypeStruct((P, T, Fb), MXU_DTYPE), jax.ShapeDtypeStruct((T, D), ACT_DTYPE),
                   jax.ShapeDtypeStruct((T, D), F32), jax.ShapeDtypeStruct((8, LANES), F32)],
        scratch_shapes=[pltpu.VMEM(w_out.shape, w_out.dtype), pltpu.VMEM(w_gate.shape, w_gate.dtype),
                        pltpu.VMEM(w_up.shape, w_up.dtype), pltpu.VMEM(w_down.shape, w_down.dtype),
                        pltpu.SemaphoreType.DMA((4,))],
        compiler_params=_cp(("arbitrary",)),
    )(x2, cv, ya, tgt, mod, g_ln, b_ln, g_ffn, w_out, w_gate, w_up, w_down)


def _ffn_bwd(dy, x1, gs, us, fo, mixb, cv, mod, g_ln, b_ln, g_ffn, w_out, w_gate, w_up, w_down, *, S, tm):
    T, D = dy.shape
    DC = cv.shape[1]
    P, Kb, _ = w_out.shape
    Fb = w_gate.shape[2]
    tps = S // tm
    Bl = T // S

    def body(dy_ref, x1_ref, g_ref, u_ref, f_ref, mix_ref, cv_ref, mod_ref, gln_ref, bln_ref, gf_ref,
             wo_hbm, wg_hbm, wu_hbm, wd_hbm,
             dg_ref, du_ref, df_ref, dx1_ref, dmix_ref, dya_ref, dcv_ref, macc_ref, gacc_ref, lacc_ref,
             wo, wg, wu, wd, sems):
        i = pl.program_id(0)
        _load_resident(i, [(wo_hbm, wo), (wg_hbm, wg), (wu_hbm, wu), (wd_hbm, wd)], sems)

        @pl.when(i == 0)
        def _():
            gacc_ref[...] = jnp.zeros_like(gacc_ref)
            lacc_ref[...] = jnp.zeros_like(lacc_ref)

        @pl.when(i % tps == 0)
        def _():
            macc_ref[...] = jnp.zeros_like(macc_ref)

        dy_t = dy_ref[...]
        x1 = x1_ref[...]
        gate_f = mod_ref[:, 5 * D:6 * D]
        macc_ref[2:3, :] += _colsum(dy_t * f_ref[...].astype(F32))
        dfb = (dy_t * gate_f).astype(MXU_DTYPE)
        df_ref[...] = dfb
        dh2 = jnp.zeros((tm, D), F32)
        for p in range(P):
            da = _dot_nt(dfb, wd[p])
            g = g_ref[p].astype(F32)
            u = u_ref[p].astype(F32)
            sg = _sigmoid(g)
            dgp = (da * u * (sg * (1.0 + g * (1.0 - sg)))).astype(MXU_DTYPE)
            dup = (da * (g * sg)).astype(MXU_DTYPE)
            dg_ref[p] = dgp
            du_ref[p] = dup
            dh2 = dh2 + _dot_nt(dgp, wg[p]) + _dot_nt(dup, wu[p])
        r2 = lax.rsqrt(jnp.mean(x1 * x1, axis=-1, keepdims=True) + EPS)
        xr = x1 * r2
        n2 = xr * gf_ref[...]
        macc_ref[0:1, :] += _colsum(dh2)
        macc_ref[1:2, :] += _colsum(dh2 * n2)
        dn2 = dh2 * (1.0 + mod_ref[:, 4 * D:5 * D])
        gacc_ref[0:1, :] += _colsum(dn2 * xr)
        e = dn2 * gf_ref[...]
        dx1 = dy_t + r2 * e - xr * (r2 * jnp.mean(e * xr, axis=-1, keepdims=True))
        dx1_ref[...] = dx1
        macc_ref[3:4, :] += _colsum(dx1 * mix_ref[...].astype(F32))
        dmixb = (dx1 * mod_ref[:, 2 * D:3 * D]).astype(MXU_DTYPE)
        dmix_ref[...] = dmixb
        parts = [_dot_nt(dmixb, wo[p]) for p in range(P)]
        dycat = jnp.concatenate(parts, axis=1) if P > 1 else parts[0]
        dya_ref[...] = dycat[:, DC:]
        dyc = dycat[:, :DC]
        z, rs, ln = _layer_norm_parts(cv_ref[...], gln_ref[...], bln_ref[...])
        sg = _sigmoid(ln)
        dln = dyc * (sg * (1.0 + ln * (1.0 - sg)))
        lacc_ref[0:1, :] += _colsum(dln * z)
        lacc_ref[1:2, :] += _colsum(dln)
        dz = dln * gln_ref[...]
        dcv_ref[...] = rs * (dz - jnp.mean(dz, axis=-1, keepdims=True) - z * jnp.mean(dz * z, axis=-1, keepdims=True))

    row = lambda w: pl.BlockSpec((tm, w), lambda i: (i, 0))
    vec = lambda w: pl.BlockSpec((1, w), lambda i: (0, 0))
    blk = pl.BlockSpec((P, tm, Fb), lambda i: (0, i, 0))
    anyspec = pl.BlockSpec(memory_space=pl.ANY)
    return pl.pallas_call(
        body, grid=(T // tm,), name="ffn_bwd",
        in_specs=[row(D), row(D), blk, blk, row(D), row(D), row(DC),
                  pl.BlockSpec((None, 1, N_MOD * D), lambda i: (i // tps, 0, 0)),
                  vec(DC), vec(DC), vec(D), anyspec, anyspec, anyspec, anyspec],
        out_specs=[blk, blk, row(D), row(D), row(D), row(D - DC), row(DC),
                   pl.BlockSpec((None, 8, D), lambda i: (i // tps, 0, 0)),
                   pl.BlockSpec((8, D), lambda i: (0, 0)), pl.BlockSpec((8, DC), lambda i: (0, 0))],
        out_shape=[jax.ShapeDtypeStruct((P, T, Fb), MXU_DTYPE), jax.ShapeDtypeStruct((P, T, Fb), MXU_DTYPE),
                   jax.ShapeDtypeStruct((T, D), MXU_DTYPE), jax.ShapeDtypeStruct((T, D), F32),
                   jax.ShapeDtypeStruct((T, D), MXU_DTYPE), jax.ShapeDtypeStruct((T, D - DC), F32),
                   jax.ShapeDtypeStruct((T, DC), F32), jax.ShapeDtypeStruct((Bl, 8, D), F32),
                   jax.ShapeDtypeStruct((8, D), F32), jax.ShapeDtypeStruct((8, DC), F32)],
        scratch_shapes=[pltpu.VMEM(w_out.shape, w_out.dtype), pltpu.VMEM(w_gate.shape, w_gate.dtype),
                        pltpu.VMEM(w_up.shape, w_up.dtype), pltpu.VMEM(w_down.shape, w_down.dtype),
                        pltpu.SemaphoreType.DMA((4,))],
        compiler_params=_cp(("arbitrary",)),
    )(dy, x1, gs, us, fo, mixb, cv, mod, g_ln, b_ln, g_ffn, w_out, w_gate, w_up, w_down)


def _in_bwd(da, dg, dq, dk, dv, x2, dx1, mod, g_mix, w_in, *, S, tm):
    T, D = x2.shape
    P, _, Nb = w_in.shape
    DC = da.shape[1]
    NS = dq.shape[0]
    n_in = P * Nb
    tps = S // tm
    Bl = T // S

    def body(da_ref, dg_ref, dq_ref, dk_ref, dv_ref, x_ref, dx1_ref, mod_ref, g_ref, w_ref,
             dx_ref, dproj_ref, macc_ref, gacc_ref):
        i = pl.program_id(0)

        @pl.when(i == 0)
        def _():
            gacc_ref[...] = jnp.zeros_like(gacc_ref)

        @pl.when(i % tps == 0)
        def _():
            macc_ref[...] = jnp.zeros_like(macc_ref)

        pieces = [da_ref[...], dg_ref[...]] + [r[j] for r in (dq_ref, dk_ref, dv_ref) for j in range(NS)]
        dproj = jnp.concatenate(pieces, axis=1).astype(MXU_DTYPE)
        dproj_ref[...] = dproj
        dh = _dot_nt(dproj[:, 0:Nb], w_ref[0])
        for p in range(1, P):
            dh = dh + _dot_nt(dproj[:, Nb * p:Nb * (p + 1)], w_ref[p])
        x = x_ref[...]
        r = lax.rsqrt(jnp.mean(x * x, axis=-1, keepdims=True) + EPS)
        xr = x * r
        macc_ref[0:1, :] += _colsum(dh)
        macc_ref[1:2, :] += _colsum(dh * (xr * g_ref[...]))
        dn = dh * (1.0 + mod_ref[:, D:2 * D])
        gacc_ref[0:1, :] += _colsum(dn * xr)
        e = dn * g_ref[...]
        dx_ref[...] = dx1_ref[...] + r * e - xr * (r * jnp.mean(e * xr, axis=-1, keepdims=True))

    row = lambda w: pl.BlockSpec((tm, w), lambda i: (i, 0))
    slab = pl.BlockSpec((NS, tm, LANES), lambda i: (0, i, 0))
    return pl.pallas_call(
        body, grid=(T // tm,), name="in_bwd",
        in_specs=[row(DC), row(DC), slab, slab, slab, row(D), row(D),
                  pl.BlockSpec((None, 1, N_MOD * D), lambda i: (i // tps, 0, 0)),
                  pl.BlockSpec((1, D), lambda i: (0, 0)),
                  pl.BlockSpec((P, D, Nb), lambda i: (0, 0, 0))],
        out_specs=[row(D), row(n_in), pl.BlockSpec((None, 8, D), lambda i: (i // tps, 0, 0)),
                   pl.BlockSpec((8, D), lambda i: (0, 0))],
        out_shape=[jax.ShapeDtypeStruct((T, D), F32), jax.ShapeDtypeStruct((T, n_in), MXU_DTYPE),
                   jax.ShapeDtypeStruct((Bl, 8, D), F32), jax.ShapeDtypeStruct((8, D), F32)],
        compiler_params=_cp(("arbitrary",)),
    )(da, dg, dq, dk, dv, x2, dx1, mod, g_mix, w_in)


def _wgrad(a, b, *, P, name, tk, split=None):
    a_blk, b_blk = a.ndim == 3, b.ndim == 3
    T = a.shape[-2]
    if a_blk:
        R, C = a.shape[2], b.shape[1]
        a_of = lambda av, p: av[p]
        b_of = lambda bv, p: bv[...]
    elif b_blk:
        R, C = a.shape[1], b.shape[2]
        a_of = lambda av, p: av[...]
        b_of = lambda bv, p: bv[p]
    elif split == "a":
        R, C = a.shape[1] // P, b.shape[1]
        a_of = lambda av, p: av[:, R * p:R * (p + 1)]
        b_of = lambda bv, p: bv[...]
    else:
        R, C = a.shape[1], b.shape[1] // P
        a_of = lambda av, p: av[...]
        b_of = lambda bv, p: bv[:, C * p:C * (p + 1)]

    def body(a_ref, b_ref, o_ref):
        @pl.when(pl.program_id(0) == 0)
        def _():
            o_ref[...] = jnp.zeros_like(o_ref)
        for p in range(P):
            o_ref[p] += _dot_tn(a_of(a_ref, p), b_of(b_ref, p))

    def spec(v):
        if v.ndim == 3:
            return pl.BlockSpec((P, tk, v.shape[2]), lambda k: (0, k, 0))
        return pl.BlockSpec((tk, v.shape[1]), lambda k: (k, 0))

    return pl.pallas_call(
        body, grid=(T // tk,), name=name,
        in_specs=[spec(a), spec(b)],
        out_specs=pl.BlockSpec((P, R, C), lambda k: (0, 0, 0)),
        out_shape=jax.ShapeDtypeStruct((P, R, C), F32),
        compiler_params=_cp(("arbitrary",)),
    )(a, b)


TM_IN = 512
TM_FFN = 256
TK_WGRAD = 512


def _alibi_slabs(n_slab):
    heads = 2 * n_slab
    slopes = 2.0 ** (-8.0 * np.arange(1, heads + 1) / heads)
    return jnp.asarray(np.broadcast_to(np.repeat(slopes.reshape(n_slab, 1, 2), HEAD_DIM, axis=2), (n_slab, 8, LANES)),
                       dtype=F32)


def _local_step(x, tgt, mod, g_mix, wdw, g_ln, b_ln, g_q, g_k, g_ffn, w_in, w_out, w_gate, w_up, w_down):
    Bl, S, D = x.shape
    T = Bl * S
    DC = g_ln.shape[1]
    P = w_in.shape[0]
    n_slab = (D - DC) // LANES
    x2 = x.reshape(T, D)
    t2 = tgt.reshape(T, D)
    mod3 = mod.reshape(Bl, 1, N_MOD * D)
    gq2 = jnp.tile(g_q, (1, LANES // HEAD_DIM))
    gk2 = jnp.tile(g_k, (1, LANES // HEAD_DIM))
    slopes = _alibi_slabs(n_slab)

    ag, qkv, h1 = _fwd_in(x2, mod3, g_mix, w_in, S=S, tm=TM_IN, n_ag=2 * DC)
    cv = _conv_fwd(ag, wdw, Bl=Bl, S=S, DC=DC)
    ya, lse = _attn_fwd(qkv, gq2, gk2, slopes, Bl=Bl, S=S)
    x1, ycat, mixb, h2, gs, us, acts, fo, dy, lossb = _ffn_fwd(
        x2, cv, ya, t2, mod3, g_ln, b_ln, g_ffn, w_out, w_gate, w_up, w_down, S=S, tm=TM_FFN)
    dgs, dus, dfb, dx1, dmixb, dya, dcv, macc_f, gacc_f, lacc = _ffn_bwd(
        dy, x1, gs, us, fo, mixb, cv, mod3, g_ln, b_ln, g_ffn, w_out, w_gate, w_up, w_down, S=S, tm=TM_FFN)
    dq, dk, dv, gqk = _attn_bwd(qkv, ya, lse, dya, gq2, gk2, slopes, Bl=Bl, S=S)
    da, dg, dwdw = _conv_bwd(ag, dcv, wdw, Bl=Bl, S=S, DC=DC)
    dx, dprojb, macc_m, gacc_m = _in_bwd(da, dg, dq, dk, dv, x2, dx1, mod3, g_mix, w_in, S=S, tm=TM_IN)

    grads = dict(
        w_in=_wgrad(h1, dprojb, P=P, name="wgrad_in", tk=TK_WGRAD, split="b"),
        w_out=_wgrad(ycat, dmixb, P=P, name="wgrad_out", tk=TK_WGRAD, split="a"),
        w_gate=_wgrad(h2, dgs, P=P, name="wgrad_gate", tk=TK_WGRAD),
        w_up=_wgrad(h2, dus, P=P, name="wgrad_up", tk=TK_WGRAD),
        w_down=_wgrad(acts, dfb, P=P, name="wgrad_down", tk=TK_WGRAD),
    )
    packed = _pack_small(macc_m, macc_f, gacc_m, gacc_f, lacc, gqk, dwdw)
    return dict(loss=lossb[0, 0], dx=dx.reshape(Bl, S, D), grads=grads, packed=packed)


def _small_layout(Bl):
    return 8 * Bl, 8 * Bl + 8, 8 * Bl + 8 + CONV_ROWS


def _pack_small(macc_m, macc_f, gacc_m, gacc_f, lacc, gqk, dwdw):
    Bl, _, D = macc_m.shape
    DC = lacc.shape[1]
    assert 2 * DC <= D
    SMALL_GAIN_ROW, SMALL_TAP_ROW, SMALL_ROWS = _small_layout(Bl)

    def body(mm_ref, mf_ref, gm_ref, gf_ref, la_ref, qk_ref, dw_ref, o_ref):
        o_ref[...] = jnp.zeros_like(o_ref)
        for b in range(Bl):
            o_ref[8 * b + 0:8 * b + 2, :] = mm_ref[b, 0:2, :]
            o_ref[8 * b + 2:8 * b + 3, :] = mf_ref[b, 3:4, :]
            o_ref[8 * b + 3:8 * b + 6, :] = mf_ref[b, 0:3, :]
        r = SMALL_GAIN_ROW
        o_ref[r:r + 1, :] = gm_ref[0:1, :]
        o_ref[r + 1:r + 2, :] = gf_ref[0:1, :]
        o_ref[r + 2:r + 3, 0:DC] = la_ref[0:1, :]
        o_ref[r + 2:r + 3, DC:2 * DC] = la_ref[1:2, :]
        qk = qk_ref[0:2, 0:HEAD_DIM] + qk_ref[0:2, HEAD_DIM:2 * HEAD_DIM]
        o_ref[r + 3:r + 4, 0:HEAD_DIM] = qk[0:1, :]
        o_ref[r + 3:r + 4, HEAD_DIM:2 * HEAD_DIM] = qk[1:2, :]
        o_ref[SMALL_TAP_ROW:SMALL_TAP_ROW + CONV_ROWS, 0:DC] = dw_ref[...]

    return pl.pallas_call(body, name="pack_small", out_shape=jax.ShapeDtypeStruct((SMALL_ROWS, D), F32),
                          compiler_params=_cp())(macc_m, macc_f, gacc_m, gacc_f, lacc, gqk, dwdw)


def _row_tile(rows, cap=512):
    if rows <= cap:
        return rows
    best = rows
    for t in range(8, cap + 1, 8):
        if rows % t == 0:
            best = t
    return best


def _cast_weight(w, pidx, name):
    def body(p_ref, w_ref, o_ref):
        o_ref[...] = w_ref[...].astype(MXU_DTYPE)
    R, C = w.shape
    tr = _row_tile(R)
    return pl.pallas_call(
        body, name=name,
        grid_spec=pltpu.PrefetchScalarGridSpec(
            num_scalar_prefetch=1, grid=(R // tr,),
            in_specs=[pl.BlockSpec((tr, C), lambda i, p: (i, 0))],
            out_specs=pl.BlockSpec((None, tr, C), lambda i, p: (p[0], i, 0))),
        out_shape=jax.ShapeDtypeStruct((4, R, C), MXU_DTYPE),
    )(pidx, w)


def _pair_add(g, recv, cidx, name):
    P, R, C = g.shape
    R2 = R // 2

    def body(c_ref, g_ref, r_ref, o_ref, ob_ref):
        s = g_ref[...] + r_ref[...]
        o_ref[...] = s
        ob_ref[...] = s.astype(jnp.bfloat16)

    return pl.pallas_call(
        body, name=name,
        grid_spec=pltpu.PrefetchScalarGridSpec(
            num_scalar_prefetch=1, grid=(P,),
            in_specs=[pl.BlockSpec((None, R2, C), lambda p, c: (p, c[0], 0)),
                      pl.BlockSpec((None, R2, C), lambda p, c: (p, 0, 0))],
            out_specs=[pl.BlockSpec((None, R2, C), lambda p, c: (p, 0, 0)),
                       pl.BlockSpec((None, R2, C), lambda p, c: (p, 0, 0))]),
        out_shape=[jax.ShapeDtypeStruct((P, R2, C), F32), jax.ShapeDtypeStruct((P, R2, C), jnp.bfloat16)],
    )(cidx, g, recv)


def _final_add(chipsum, recv, pc_idx, name):
    P, R2, C = chipsum.shape

    def body(pc_ref, s_ref, r_ref, o_ref):
        acc = s_ref[...]
        for k in range(3):
            acc = acc + r_ref[k].astype(F32)
        o_ref[...] = acc

    return pl.pallas_call(
        body, name=name,
        grid_spec=pltpu.PrefetchScalarGridSpec(
            num_scalar_prefetch=1, grid=(1,),
            in_specs=[pl.BlockSpec((None, R2, C), lambda i, pc: (pc[0], 0, 0)),
                      pl.BlockSpec((3, R2, C), lambda i, pc: (0, 0, 0))],
            out_specs=pl.BlockSpec((R2, C), lambda i, pc: (pc[1], 0))),
        out_shape=jax.ShapeDtypeStruct((2 * R2, C), F32),
    )(pc_idx, chipsum, recv)


def _adamw(w, g, m, v, name):
    R, C = w.shape
    tr = _row_tile(R, 256)
    c1 = 1.0 - ADAM_B1 ** ADAM_STEP
    c2 = 1.0 - ADAM_B2 ** ADAM_STEP

    def body(w_ref, g_ref, m_ref, v_ref, d_ref, nm_ref, nv_ref):
        gg = g_ref[...]
        nm = ADAM_B1 * m_ref[...] + (1.0 - ADAM_B1) * gg
        nv = ADAM_B2 * v_ref[...] + (1.0 - ADAM_B2) * (gg * gg)
        nm_ref[...] = nm
        nv_ref[...] = nv
        d_ref[...] = -ADAM_LR * ((nm / c1) / (jnp.sqrt(nv / c2) + ADAM_EPS) + ADAM_WD * w_ref[...])

    spec = pl.BlockSpec((tr, C), lambda i: (i, 0))
    return pl.pallas_call(
        body, grid=(R // tr,), name=name,
        in_specs=[spec] * 4, out_specs=[spec] * 3,
        out_shape=[jax.ShapeDtypeStruct((R, C), F32)] * 3,
    )(w, g, m, v)


def _ada_fwd(c_all, w_ada, b_cols):
    def body(c_ref, w_ref, b_ref, o_ref):
        c = c_ref[...]
        o_ref[...] = jnp.dot(c * _sigmoid(c), w_ref[...], preferred_element_type=F32, precision=HIGHEST) + b_ref[...]
    return pl.pallas_call(
        body, name="ada_fwd", out_shape=jax.ShapeDtypeStruct((c_all.shape[0], w_ada.shape[1]), F32),
        compiler_params=_cp(),
    )(c_all, w_ada, b_cols)


def _ada_bwd(c_all, dmod_cols):
    def body(c_ref, d_ref, o_ref):
        c = c_ref[...]
        o_ref[...] = lax.dot_general(c * _sigmoid(c), d_ref[...], (((0,), (0,)), ((), ())),
                                     preferred_element_type=F32, precision=HIGHEST)
    return pl.pallas_call(
        body, name="ada_bwd", out_shape=jax.ShapeDtypeStruct((c_all.shape[1], dmod_cols.shape[1]), F32),
        compiler_params=_cp(),
    )(c_all, dmod_cols)


def _small_reduce(gathered, n_dev, Bl):
    mod_rows, _, rows = _small_layout(Bl)
    width = gathered.shape[1]

    def body(g_ref, red_ref, bada_ref):
        acc = g_ref[0:rows, :]
        for d in range(1, n_dev):
            acc = acc + g_ref[d * rows:(d + 1) * rows, :]
        red_ref[...] = acc[mod_rows:, :]
        b = acc[0:8, :]
        for q in range(1, Bl):
            b = b + acc[8 * q:8 * q + 8, :]
        bada_ref[...] = b
    return pl.pallas_call(
        body, name="small_reduce",
        out_shape=[jax.ShapeDtypeStruct((rows - mod_rows, width), F32), jax.ShapeDtypeStruct((8, width), F32)],
        compiler_params=_cp(),
    )(gathered)


def _mesh_pos():
    return lax.axis_index("x"), lax.axis_index("y"), lax.axis_index("c")


def _other_chips(x, y):
    return [(1 - x, y), (x, 1 - y), (1 - x, 1 - y)]


def _allgather8(xs, name):
    m_per, n = xs.shape

    def body(x_ref, out_ref, send_sems, recv_sems, local_sem):
        x, y, c = _mesh_pos()
        me, sibling = (x, y, c), (x, y, 1 - c)
        chips = _other_chips(x, y)

        def rows(px, py, pc):
            return out_ref.at[pl.ds((4 * px + 2 * py + pc) * m_per, m_per), :]

        def copy(k, block, to, src=None):
            return pltpu.make_async_remote_copy(
                src_ref=rows(*block) if src is None else src, dst_ref=rows(*block),
                send_sem=send_sems.at[k], recv_sem=recv_sems.at[k], device_id=to, device_id_type=MESH_DEV)

        mine = pltpu.make_async_copy(x_ref, rows(*me), local_sem)
        mine.start()
        first = [copy(0, me, sibling, src=x_ref)]
        first += [copy(1 + j, me, (*chip, c), src=x_ref) for j, chip in enumerate(chips)]
        for cp in first:
            cp.start()
        passed = [copy(4 + j, (*chip, c), sibling) for j, chip in enumerate(chips)]
        for j, chip in enumerate(chips):
            copy(1 + j, (*chip, c), me).wait_recv()
            passed[j].start()
        copy(0, sibling, me).wait_recv()
        for j, chip in enumerate(chips):
            copy(4 + j, (*chip, 1 - c), me).wait_recv()
        for cp in first + passed:
            cp.wait_send()
        mine.wait()

    return pl.pallas_call(
        body, name=name, out_shape=jax.ShapeDtypeStruct((8 * m_per, n), xs.dtype),
        in_specs=[pl.BlockSpec(memory_space=pltpu.VMEM)], out_specs=pl.BlockSpec(memory_space=pltpu.VMEM),
        scratch_shapes=[pltpu.SemaphoreType.DMA((7,)), pltpu.SemaphoreType.DMA((7,)), pltpu.SemaphoreType.DMA],
        compiler_params=_cp(),
    )(xs)


def _gather_weights(bufs):
    n = len(bufs)

    def body(*refs):
        outs = refs[n:2 * n]
        send_sems, recv_sems = refs[2 * n:]
        x, y, c = _mesh_pos()
        p = 2 * x + y
        sibling = (x, y, 1 - c)
        chips = _other_chips(x, y)

        def copy(w, k, slot, h, to):
            r2 = bufs[w].shape[1] // 2
            blk = outs[w].at[slot, pl.ds(h * r2, r2), :]
            return pltpu.make_async_remote_copy(
                src_ref=blk, dst_ref=blk, send_sem=send_sems.at[6 * w + k], recv_sem=recv_sems.at[6 * w + k],
                device_id=to, device_id_type=MESH_DEV)

        sent = []
        for w in range(n):
            for k, chip in enumerate(chips):
                sent.append(copy(w, k, p, c, (*chip, c)))
                sent[-1].start()
        for w in range(n):
            for k, chip in enumerate(chips):
                slot = 2 * chip[0] + chip[1]
                copy(w, k, slot, c, sibling).wait_recv()
                sent.append(copy(w, 3 + k, slot, c, sibling))
                sent[-1].start()
        for w in range(n):
            for k, chip in enumerate(chips):
                copy(w, 3 + k, 2 * chip[0] + chip[1], 1 - c, sibling).wait_recv()
        for cp in sent:
            cp.wait_send()

    anyspec = pl.BlockSpec(memory_space=pl.ANY)
    return pl.pallas_call(
        body, name="gather_weights",
        out_shape=[jax.ShapeDtypeStruct(b.shape, b.dtype) for b in bufs],
        in_specs=[anyspec] * n, out_specs=[anyspec] * n,
        input_output_aliases={w: w for w in range(n)},
        scratch_shapes=[pltpu.SemaphoreType.DMA((6 * n,)), pltpu.SemaphoreType.DMA((6 * n,))],
    )(*bufs)


def _rs_sibling(grads):
    n = len(grads)

    def body(*refs):
        ins, outs = refs[:n], refs[n:2 * n]
        send_sems, recv_sems = refs[2 * n:]
        x, y, c = _mesh_pos()
        cps = []
        for w in range(n):
            P, R, _ = grads[w].shape
            r2 = R // 2
            for p in range(P):
                cps.append(pltpu.make_async_remote_copy(
                    src_ref=ins[w].at[p, pl.ds((1 - c) * r2, r2), :], dst_ref=outs[w].at[p],
                    send_sem=send_sems.at[4 * w + p], recv_sem=recv_sems.at[4 * w + p],
                    device_id=(x, y, 1 - c), device_id_type=MESH_DEV))
                cps[-1].start()
        for cp in cps:
            cp.wait()

    anyspec = pl.BlockSpec(memory_space=pl.ANY)
    return pl.pallas_call(
        body, name="rs_sibling",
        out_shape=[jax.ShapeDtypeStruct((g.shape[0], g.shape[1] // 2, g.shape[2]), g.dtype) for g in grads],
        in_specs=[anyspec] * n, out_specs=[anyspec] * n,
        scratch_shapes=[pltpu.SemaphoreType.DMA((4 * n,)), pltpu.SemaphoreType.DMA((4 * n,))],
    )(*grads)


def _rs_chips(sums):
    n = len(sums)

    def body(*refs):
        ins, outs = refs[:n], refs[n:2 * n]
        send_sems, recv_sems = refs[2 * n:]
        x, y, c = _mesh_pos()
        cps = []
        for w in range(n):
            for k, chip in enumerate(_other_chips(x, y)):
                cps.append(pltpu.make_async_remote_copy(
                    src_ref=ins[w].at[2 * chip[0] + chip[1]], dst_ref=outs[w].at[k],
                    send_sem=send_sems.at[3 * w + k], recv_sem=recv_sems.at[3 * w + k],
                    device_id=(*chip, c), device_id_type=MESH_DEV))
                cps[-1].start()
        for cp in cps:
            cp.wait()

    anyspec = pl.BlockSpec(memory_space=pl.ANY)
    return pl.pallas_call(
        body, name="rs_chips",
        out_shape=[jax.ShapeDtypeStruct((3,) + s.shape[1:], s.dtype) for s in sums],
        in_specs=[anyspec] * n, out_specs=[anyspec] * n,
        scratch_shapes=[pltpu.SemaphoreType.DMA((3 * n,)), pltpu.SemaphoreType.DMA((3 * n,))],
    )(*sums)


def _rs_final(bufs):
    n = len(bufs)

    def body(*refs):
        outs = refs[n:2 * n]
        send_sems, recv_sems = refs[2 * n:]
        x, y, c = _mesh_pos()
        cps = []
        for w in range(n):
            r2 = bufs[w].shape[0] // 2
            mine = outs[w].at[pl.ds(c * r2, r2), :]
            cps.append(pltpu.make_async_remote_copy(
                src_ref=mine, dst_ref=mine, send_sem=send_sems.at[w], recv_sem=recv_sems.at[w],
                device_id=(x, y, 1 - c), device_id_type=MESH_DEV))
            cps[-1].start()
        for cp in cps:
            cp.wait()

    anyspec = pl.BlockSpec(memory_space=pl.ANY)
    return pl.pallas_call(
        body, name="rs_final",
        out_shape=[jax.ShapeDtypeStruct(b.shape, b.dtype) for b in bufs],
        in_specs=[anyspec] * n, out_specs=[anyspec] * n,
        input_output_aliases={w: w for w in range(n)},
        scratch_shapes=[pltpu.SemaphoreType.DMA((n,)), pltpu.SemaphoreType.DMA((n,))],
    )(*bufs)


BIG = ("w_in", "w_out", "w_gate", "w_up", "w_down")
WEIGHTS = ("w_ada", "b_ada", "g_mix", "w_in", "w_dw", "b_dw", "g_conv_ln", "b_conv_ln", "g_q", "g_k",
           "w_out", "g_ffn", "w_gate", "w_up", "w_down")


def _pad_to(a, rows, cols):
    return jnp.pad(a, ((0, rows - a.shape[0]), (0, cols - a.shape[1])))


def kernel(x, c, w_ada, b_ada, g_mix, w_in, w_dw, b_dw, g_conv_ln, b_conv_ln, g_q, g_k, w_out, g_ffn, w_gate, w_up, w_down, loss_target, m_w_ada, m_b_ada, m_g_mix, m_w_in, m_w_dw, m_b_dw, m_g_conv_ln, m_b_conv_ln, m_g_q, m_g_k, m_w_out, m_g_ffn, m_w_gate, m_w_up, m_w_down, v_w_ada, v_b_ada, v_g_mix, v_w_in, v_w_dw, v_b_dw, v_g_conv_ln, v_b_conv_ln, v_g_q, v_g_k, v_w_out, v_g_ffn, v_w_gate, v_w_up, v_w_down):
    w = dict(w_ada=w_ada, b_ada=b_ada, g_mix=g_mix, w_in=w_in, w_dw=w_dw, b_dw=b_dw, g_conv_ln=g_conv_ln,
             b_conv_ln=b_conv_ln, g_q=g_q, g_k=g_k, w_out=w_out, g_ffn=g_ffn, w_gate=w_gate, w_up=w_up, w_down=w_down)
    m = dict(w_ada=m_w_ada, b_ada=m_b_ada, g_mix=m_g_mix, w_in=m_w_in, w_dw=m_w_dw, b_dw=m_b_dw, g_conv_ln=m_g_conv_ln,
             b_conv_ln=m_b_conv_ln, g_q=m_g_q, g_k=m_g_k, w_out=m_w_out, g_ffn=m_g_ffn, w_gate=m_w_gate, w_up=m_w_up,
             w_down=m_w_down)
    v = dict(w_ada=v_w_ada, b_ada=v_b_ada, g_mix=v_g_mix, w_in=v_w_in, w_dw=v_w_dw, b_dw=v_b_dw, g_conv_ln=v_g_conv_ln,
             b_conv_ln=v_b_conv_ln, g_q=v_g_q, g_k=v_g_k, w_out=v_w_out, g_ffn=v_g_ffn, w_gate=v_w_gate, w_up=v_w_up,
             w_down=v_w_down)
    Bl, S, D = x.shape
    DC = g_conv_ln.shape[1]
    NA = w_ada.shape[2]
    xi, yi, ci = _mesh_pos()
    p = 2 * xi + yi
    dev = 2 * p + ci
    n_dev = 8
    cidx = jnp.reshape(ci, (1,)).astype(jnp.int32)
    pidx = jnp.reshape(p, (1,)).astype(jnp.int32)

    first = jnp.concatenate([_pad_to(c, 8, D), _pad_to(w_dw[0], CONV_ROWS, D)], axis=0)
    g0 = _allgather8(first, "gather_cond").reshape(n_dev, 8 + CONV_ROWS, D)
    c_all = g0[:, :Bl].reshape(n_dev * Bl, D)
    taps = jnp.concatenate([g0[2 * q, 8:, :w_dw.shape[2]] for q in range(4)], axis=1)
    wdw = jnp.where(lax.broadcasted_iota(jnp.int32, taps.shape, 0) == CONV_WIDTH, b_dw, taps)
    full = dict(zip(BIG, _gather_weights([_cast_weight(w[nm][0], pidx, "cast_" + nm) for nm in BIG])))

    b_cols = lax.dynamic_slice_in_dim(b_ada, p * NA, NA, axis=1)
    mod_part = _ada_fwd(c_all, w_ada[0], b_cols)
    gm = _allgather8(mod_part, "gather_mod").reshape(n_dev, n_dev * Bl, NA)
    mod = jnp.concatenate([lax.dynamic_slice_in_dim(gm[2 * q], dev * Bl, Bl, axis=0) for q in range(4)], axis=1)

    loc = _local_step(x, loss_target, mod, g_mix, wdw, g_conv_ln, b_conv_ln, g_q, g_k, g_ffn,
                      full["w_in"], full["w_out"], full["w_gate"], full["w_up"], full["w_down"])
    loss = lax.psum(loc["loss"], ("x", "y", "c"))

    parts = [loc["grads"][nm] for nm in BIG]
    from_sib = _rs_sibling(parts)
    sums = [_pair_add(g, r, cidx, "pair_add_" + nm) for nm, g, r in zip(BIG, parts, from_sib)]
    from_chips = _rs_chips([sb for _, sb in sums])
    pc_idx = jnp.stack([p, ci]).astype(jnp.int32)
    halves = [_final_add(s32, r, pc_idx, "final_add_" + nm) for nm, (s32, _), r in zip(BIG, sums, from_chips)]
    grad = dict(zip(BIG, _rs_final(halves)))

    mod_rows, _, small_rows = _small_layout(Bl)
    gs = _allgather8(loc["packed"], "gather_small")
    red, bada8 = _small_reduce(gs, n_dev, Bl)
    dmod_all = gs.reshape(n_dev, small_rows, D)[:, :mod_rows].reshape(n_dev * Bl, 8, D)[:, :N_MOD].reshape(n_dev * Bl, N_MOD * D)
    grad["w_ada"] = _ada_bwd(c_all, lax.dynamic_slice_in_dim(dmod_all, p * NA, NA, axis=1))
    grad["b_ada"] = bada8[:N_MOD].reshape(1, N_MOD * D)
    grad["g_mix"] = red[0:1]
    grad["g_ffn"] = red[1:2]
    grad["g_conv_ln"] = red[2:3, :DC]
    grad["b_conv_ln"] = red[2:3, DC:2 * DC]
    grad["g_q"] = red[3:4, :HEAD_DIM]
    grad["g_k"] = red[3:4, HEAD_DIM:2 * HEAD_DIM]
    dwdw = red[8:8 + CONV_ROWS, :DC]
    grad["w_dw"] = lax.dynamic_slice_in_dim(dwdw[:CONV_WIDTH], p * w_dw.shape[2], w_dw.shape[2], axis=1)
    grad["b_dw"] = dwdw[CONV_WIDTH:CONV_WIDTH + 1]

    delta, new_m, new_v = {}, {}, {}
    for nm in WEIGHTS:
        shp = w[nm].shape
        two_d = (shp[-2], shp[-1]) if len(shp) == 3 else shp
        d_, m_, v_ = _adamw(w[nm].reshape(two_d), grad[nm].reshape(two_d), m[nm].reshape(two_d), v[nm].reshape(two_d),
                            "adamw_" + nm)
        grad[nm] = grad[nm].reshape(shp)
        delta[nm], new_m[nm], new_v[nm] = d_.reshape(shp), m_.reshape(shp), v_.reshape(shp)

    return (loss, loc["dx"], *[grad[nm] for nm in WEIGHTS], *[delta[nm] for nm in WEIGHTS],
            *[new_m[nm] for nm in WEIGHTS], *[new_v[nm] for nm in WEIGHTS])
```

```python
import functools
import math

import jax
import jax.numpy as jnp
import numpy as np
from jax import lax
from jax.experimental import pallas as pl
from jax.experimental.pallas import tpu as pltpu

F32 = jnp.float32
MXU_DTYPE = jnp.bfloat16
ACT_DTYPE = jnp.bfloat16
EPS = 1e-6
NEG_INF = -1e30
HEAD_DIM = 64
LANES = 128
RADIUS = 64
QBLK = 128
DILATIONS = (1, 4, 16)
CONV_WIDTH = 31
CONV_PAD = CONV_WIDTH // 2
CONV_ROWS = 32
N_MOD = 6
ADAM_LR, ADAM_B1, ADAM_B2, ADAM_EPS, ADAM_WD, ADAM_STEP = 0.001, 0.9, 0.999, 1e-08, 0.01, 10
HIGHEST = lax.Precision.HIGHEST
MESH_DEV = pl.DeviceIdType.MESH
VMEM_LIMIT = 56 << 20


def _cp(sem=None, vmem=VMEM_LIMIT):
    kw = dict(vmem_limit_bytes=vmem)
    if sem is not None:
        kw["dimension_semantics"] = sem
    return pltpu.CompilerParams(**kw)


def _sigmoid(x):
    return 1.0 / (1.0 + jnp.exp(-x))


def _dot(a, b):
    return jnp.dot(a, b, preferred_element_type=F32)


def _dot_nt(a, b):
    return lax.dot_general(a, b, (((1,), (1,)), ((), ())), preferred_element_type=F32)


def _dot_tn(a, b):
    return lax.dot_general(a, b, (((0,), (0,)), ((), ())), preferred_element_type=F32)


def _colsum(v):
    return jnp.sum(v, axis=0, keepdims=True)


def _load_resident(i, pairs, sems):
    @pl.when(i == 0)
    def _():
        cps = [pltpu.make_async_copy(src, dst, sems.at[n]) for n, (src, dst) in enumerate(pairs)]
        for c in cps:
            c.start()
        for c in cps:
            c.wait()


def _fwd_in(x2, mod, g_mix, w_in, *, S, tm, n_ag):
    T, D = x2.shape
    P, _, Nb = w_in.shape
    n_in = P * Nb
    n_slab = (n_in - n_ag) // LANES
    tps = S // tm

    def body(x_ref, mod_ref, g_ref, w_ref, ag_ref, qkv_ref, h_ref):
        x = x_ref[...]
        r = lax.rsqrt(jnp.mean(x * x, axis=-1, keepdims=True) + EPS)
        n = x * r * g_ref[...]
        h = n * (1.0 + mod_ref[:, D:2 * D]) + mod_ref[:, 0:D]
        hb = h.astype(MXU_DTYPE)
        h_ref[...] = hb
        parts = [_dot(hb, w_ref[p]) for p in range(P)]
        proj = jnp.concatenate(parts, axis=1) if P > 1 else parts[0]
        ag_ref[...] = proj[:, :n_ag]
        for j in range(n_slab):
            qkv_ref[j] = proj[:, n_ag + LANES * j:n_ag + LANES * (j + 1)]

    return pl.pallas_call(
        body, grid=(T // tm,), name="fwd_in",
        in_specs=[pl.BlockSpec((tm, D), lambda i: (i, 0)),
                  pl.BlockSpec((None, 1, N_MOD * D), lambda i: (i // tps, 0, 0)),
                  pl.BlockSpec((1, D), lambda i: (0, 0)),
                  pl.BlockSpec((P, D, Nb), lambda i: (0, 0, 0))],
        out_specs=[pl.BlockSpec((tm, n_ag), lambda i: (i, 0)),
                   pl.BlockSpec((n_slab, tm, LANES), lambda i: (0, i, 0)),
                   pl.BlockSpec((tm, D), lambda i: (i, 0))],
        out_shape=[jax.ShapeDtypeStruct((T, n_ag), F32),
                   jax.ShapeDtypeStruct((n_slab, T, LANES), F32),
                   jax.ShapeDtypeStruct((T, D), MXU_DTYPE)],
        compiler_params=_cp(("arbitrary",)),
    )(x2, mod, g_mix, w_in)


CONV_CH = 64


def _conv_taps(win, w_ref, acc, reverse):
    n = win.shape[0]
    for b in range(8):
        wb = win if b == 0 else pltpu.roll(win, shift=n - b, axis=0)
        for a in range(4):
            o = 8 * a + b
            if o < 1 or o > CONV_WIDTH:
                continue
            k = (CONV_WIDTH - o) if reverse else (o - 1)
            acc = acc + w_ref[k:k + 1, :] * wb[8 * a:8 * a + CONV_CH, :]
    return acc


def _conv_fwd(ag, wdw, *, Bl, S, DC):
    T = ag.shape[0]
    nsc = DC // LANES
    CH = CONV_CH

    def body(a_ref, g_ref, w_ref, cv_ref, upad):
        zeros16 = jnp.zeros((16, LANES), F32)
        upad[0:16, :] = zeros16
        upad[S + 16:S + 32, :] = zeros16

        def fill(i, _):
            r0 = pl.multiple_of(i * CH, CH)
            a = a_ref[pl.ds(r0, CH), :]
            g = g_ref[pl.ds(r0, CH), :]
            upad[pl.ds(r0 + 16, CH), :] = a * _sigmoid(g)
            return 0
        lax.fori_loop(0, S // CH, fill, 0)

        def conv(i, _):
            r0 = pl.multiple_of(i * CH, CH)
            win = upad[pl.ds(r0, CH + 32), :]
            acc = jnp.zeros((CH, LANES), F32) + w_ref[CONV_WIDTH:CONV_WIDTH + 1, :]
            cv_ref[pl.ds(r0, CH), :] = _conv_taps(win, w_ref, acc, reverse=False)
            return 0
        lax.fori_loop(0, S // CH, conv, 0)

    return pl.pallas_call(
        body, grid=(Bl, nsc), name="conv_fwd",
        in_specs=[pl.BlockSpec((S, LANES), lambda b, j: (b, j)),
                  pl.BlockSpec((S, LANES), lambda b, j: (b, nsc + j)),
                  pl.BlockSpec((CONV_ROWS, LANES), lambda b, j: (0, j))],
        out_specs=pl.BlockSpec((S, LANES), lambda b, j: (b, j)),
        out_shape=jax.ShapeDtypeStruct((T, DC), F32),
        scratch_shapes=[pltpu.VMEM((S + 32, LANES), F32)],
        compiler_params=_cp(("arbitrary", "arbitrary")),
    )(ag, ag, wdw)


def _conv_bwd(ag, dcv, wdw, *, Bl, S, DC):
    T = ag.shape[0]
    nsc = DC // LANES
    CH = CONV_CH

    def body(a_ref, g_ref, d_ref, w_ref, da_ref, dg_ref, dw_ref, upad, dpad, wacc):
        b = pl.program_id(1)
        zeros16 = jnp.zeros((16, LANES), F32)
        upad[0:16, :] = zeros16
        upad[S + 16:S + 32, :] = zeros16
        dpad[0:16, :] = zeros16
        dpad[S + 16:S + 32, :] = zeros16

        @pl.when(b == 0)
        def _():
            wacc[...] = jnp.zeros_like(wacc)

        def fill(i, _):
            r0 = pl.multiple_of(i * CH, CH)
            a = a_ref[pl.ds(r0, CH), :]
            g = g_ref[pl.ds(r0, CH), :]
            upad[pl.ds(r0 + 16, CH), :] = a * _sigmoid(g)
            dpad[pl.ds(r0 + 16, CH), :] = d_ref[pl.ds(r0, CH), :]
            return 0
        lax.fori_loop(0, S // CH, fill, 0)

        def step(i, _):
            r0 = pl.multiple_of(i * CH, CH)
            dwin = dpad[pl.ds(r0, CH + 32), :]
            du = _conv_taps(dwin, w_ref, jnp.zeros((CH, LANES), F32), reverse=True)
            a = a_ref[pl.ds(r0, CH), :]
            g = g_ref[pl.ds(r0, CH), :]
            sg = _sigmoid(g)
            da_ref[pl.ds(r0, CH), :] = du * sg
            dg_ref[pl.ds(r0, CH), :] = du * a * sg * (1.0 - sg)
            dc = d_ref[pl.ds(r0, CH), :]
            uwin = upad[pl.ds(r0, CH + 32), :]
            n = CH + 32
            for bb in range(8):
                wb = uwin if bb == 0 else pltpu.roll(uwin, shift=n - bb, axis=0)
                for aa in range(4):
                    o = 8 * aa + bb
                    if o < 1 or o > CONV_WIDTH:
                        continue
                    k = o - 1
                    prod = dc * wb[8 * aa:8 * aa + CH, :]
                    part = prod[0:8, :]
                    for q in range(1, CH // 8):
                        part = part + prod[8 * q:8 * q + 8, :]
                    wacc[8 * k:8 * k + 8, :] += part
            part = dc[0:8, :]
            for q in range(1, CH // 8):
                part = part + dc[8 * q:8 * q + 8, :]
            wacc[8 * CONV_WIDTH:8 * CONV_WIDTH + 8, :] += part
            return 0
        lax.fori_loop(0, S // CH, step, 0)

        @pl.when(b == Bl - 1)
        def _():
            for k in range(CONV_ROWS):
                dw_ref[k:k + 1, :] = jnp.sum(wacc[8 * k:8 * k + 8, :], axis=0, keepdims=True)

    return pl.pallas_call(
        body, grid=(nsc, Bl), name="conv_bwd",
        in_specs=[pl.BlockSpec((S, LANES), lambda j, b: (b, j)),
                  pl.BlockSpec((S, LANES), lambda j, b: (b, nsc + j)),
                  pl.BlockSpec((S, LANES), lambda j, b: (b, j)),
                  pl.BlockSpec((CONV_ROWS, LANES), lambda j, b: (0, j))],
        out_specs=[pl.BlockSpec((S, LANES), lambda j, b: (b, j)),
                   pl.BlockSpec((S, LANES), lambda j, b: (b, j)),
                   pl.BlockSpec((CONV_ROWS, LANES), lambda j, b: (0, j))],
        out_shape=[jax.ShapeDtypeStruct((T, DC), F32), jax.ShapeDtypeStruct((T, DC), F32),
                   jax.ShapeDtypeStruct((CONV_ROWS, DC), F32)],
        scratch_shapes=[pltpu.VMEM((S + 32, LANES), F32), pltpu.VMEM((S + 32, LANES), F32),
                        pltpu.VMEM((8 * CONV_ROWS, LANES), F32)],
        compiler_params=_cp(("arbitrary", "arbitrary")),
    )(ag, ag, dcv, wdw)


ROWCH = 256


LOG2E = 1.4426950408889634
LN2 = 0.6931471805599453
N_EDGE = 4


def _head_mean_matrix():
    r = lax.broadcasted_iota(jnp.int32, (LANES, LANES), 0) // HEAD_DIM
    c = lax.broadcasted_iota(jnp.int32, (LANES, LANES), 1) // HEAD_DIM
    return jnp.where(r == c, 1.0 / HEAD_DIM, 0.0).astype(jnp.bfloat16)


def _head_mean(v, mm):
    hi = v.astype(jnp.bfloat16)
    lo = (v - hi.astype(F32)).astype(jnp.bfloat16)
    return _dot(hi, mm) + _dot(lo, mm)


def _stack_heads(blk, lane_lo):
    z = jnp.zeros_like(blk)
    return jnp.concatenate([jnp.where(lane_lo, blk, z), jnp.where(lane_lo, z, blk)], axis=0)


def _merge_heads(v2, lane_lo):
    return jnp.where(lane_lo, v2[:QBLK], v2[QBLK:])


def _bias_tables(bias_ref, slope_ref):
    row = lax.broadcasted_iota(jnp.int32, (2 * QBLK, 2 * QBLK), 0)
    col = lax.broadcasted_iota(jnp.int32, (2 * QBLK, 2 * QBLK), 1)
    rel = jnp.abs(col - RADIUS - (row % QBLK))
    slope = jnp.where(row < QBLK, slope_ref[0:1, 0:1], slope_ref[0:1, HEAD_DIM:HEAD_DIM + 1]) * LOG2E
    for pi, d in enumerate(DILATIONS):
        inside = jnp.where(rel <= RADIUS, -slope * (float(d) * rel.astype(F32)), NEG_INF)
        for e in range(N_EDGE):
            t = inside
            if e & 1:
                t = jnp.where(col < RADIUS, NEG_INF, t)
            if e & 2:
                t = jnp.where(col >= QBLK + RADIUS, NEG_INF, t)
            bias_ref[N_EDGE * pi + e] = t


def _edge_index(qb, nb):
    return jnp.where(qb == 0, 1, 0) + jnp.where(qb == nb - 1, 2, 0)


def _gather_rows(src_ref, dst_ref, S, d, pad):
    n = S // d
    seg = n + 2 * RADIUS if pad else n
    step = min(n, 512)
    for r in range(d):
        base = r * seg
        if pad:
            dst_ref[base:base + RADIUS, :] = jnp.zeros((RADIUS, LANES), dst_ref.dtype)
            dst_ref[base + RADIUS + n:base + seg, :] = jnp.zeros((RADIUS, LANES), dst_ref.dtype)
            base += RADIUS
        for c0 in range(0, n, step):
            if d == 1:
                v = src_ref[c0:c0 + step, :]
            else:
                v = src_ref[pl.ds(r + c0 * d, step, stride=d), :]
            dst_ref[base + c0:base + c0 + step, :] = v.astype(dst_ref.dtype)


def _scatter_rows(src_ref, dst_ref, S, d, pad, accumulate):
    n = S // d
    seg = n + 2 * RADIUS if pad else n
    step = min(n, 512)
    for r in range(d):
        base = r * seg + (RADIUS if pad else 0)
        for c0 in range(0, n, step):
            v = src_ref[base + c0:base + c0 + step, :]
            if d == 1:
                idx = pl.ds(c0, step)
            else:
                idx = pl.ds(r + c0 * d, step, stride=d)
            if accumulate:
                dst_ref[idx, :] = dst_ref[idx, :] + v
            else:
                dst_ref[idx, :] = v


PIPE_UNROLL = 2
PIPE_SLOTS = 8


def _pipeline(n_items, stages, unroll):
    K = len(stages)
    assert n_items % unroll == 0 and K * unroll <= PIPE_SLOTS
    trips = n_items // unroll
    assert trips >= K - 1

    def trip(t, static):
        for s in reversed(range(K)):
            if static and not 0 <= t - s < trips:
                continue
            for u in range(unroll):
                item = unroll * (t - s) + u
                stages[s](jnp.int32(item) if static else item)

    for t in range(K - 1):
        trip(t, True)

    def full(t, carry):
        trip(t, False)
        return carry
    lax.fori_loop(K - 1, trips, full, 0)
    for t in range(trips, trips + K - 1):
        trip(t, True)


def _qk_normalize(q_ref, k_ref, gq_ref, gk_ref, qh, kh, S, mm):
    for c0 in range(0, S, ROWCH):
        q = q_ref[c0:c0 + ROWCH, :]
        k = k_ref[c0:c0 + ROWCH, :]
        qh[c0:c0 + ROWCH, :] = q * lax.rsqrt(_head_mean(q * q, mm) + EPS) * (gq_ref[...] * (HEAD_DIM ** -0.5 * LOG2E))
        kh[c0:c0 + ROWCH, :] = k * lax.rsqrt(_head_mean(k * k, mm) + EPS) * gk_ref[...]


def _attn_fwd(qkv, gq2, gk2, slopes, *, Bl, S):
    n3, T, _ = qkv.shape
    NS = n3 // 3
    NB = S // QBLK
    PADR = S + 2 * RADIUS * DILATIONS[-1]

    def body(q_ref, k_ref, v_ref, gq_ref, gk_ref, slope_ref, o_ref, lse_ref,
             qh, kh, qp, kp, vp, op, lp, onat, lnat, bias_ref, sbuf, pbuf, mbuf, lbuf):
        mm = _head_mean_matrix()
        lane_lo = lax.broadcasted_iota(jnp.int32, (QBLK, LANES), 1) < HEAD_DIM
        _bias_tables(bias_ref, slope_ref)
        _qk_normalize(q_ref, k_ref, gq_ref, gk_ref, qh, kh, S, mm)

        for pi, d in enumerate(DILATIONS):
            n = S // d
            nb = n // QBLK
            _gather_rows(qh, qp, S, d, pad=False)
            _gather_rows(kh, kp, S, d, pad=True)
            _gather_rows(v_ref, vp, S, d, pad=True)

            def offsets(i, nb=nb):
                r = i // nb
                return pl.multiple_of(i * QBLK, QBLK), pl.multiple_of((i + r) * QBLK, QBLK), i % nb

            def scores(i, pi=pi, nb=nb):
                q0, k0, qb = offsets(i)
                qs = _stack_heads(qp[pl.ds(q0, QBLK), :], lane_lo)
                sbuf[i % PIPE_SLOTS] = (_dot_nt(qs, kp[pl.ds(k0, 2 * QBLK), :])
                                        + bias_ref[N_EDGE * pi + _edge_index(qb, nb)])

            def rowmax(i):
                m = jnp.max(sbuf[i % PIPE_SLOTS], axis=1, keepdims=True)
                mbuf[i % PIPE_SLOTS] = jnp.broadcast_to(m, (2 * QBLK, LANES))

            def expsum(i):
                m = mbuf[i % PIPE_SLOTS]
                p = jnp.exp2(sbuf[i % PIPE_SLOTS] - jnp.concatenate([m, m], axis=1))
                pbuf[i % PIPE_SLOTS] = p.astype(MXU_DTYPE)
                lbuf[i % PIPE_SLOTS] = jnp.broadcast_to(jnp.sum(p, axis=1, keepdims=True), (2 * QBLK, LANES))

            def values(i):
                q0, k0, _ = offsets(i)
                l = lbuf[i % PIPE_SLOTS]
                o2 = _dot(pbuf[i % PIPE_SLOTS], vp[pl.ds(k0, 2 * QBLK), :]) * (1.0 / l)
                op[pl.ds(q0, QBLK), :] = _merge_heads(o2, lane_lo)
                lp[pl.ds(q0, QBLK), :] = _merge_heads(mbuf[i % PIPE_SLOTS] + jnp.log2(l), lane_lo)

            _pipeline(NB, [scores, rowmax, expsum, values], PIPE_UNROLL)
            _scatter_rows(op, onat.at[pi], S, d, pad=False, accumulate=False)
            _scatter_rows(lp, lnat.at[pi], S, d, pad=False, accumulate=False)

        for c0 in range(0, S, ROWCH):
            ls = [lnat[pi, c0:c0 + ROWCH, :] for pi in range(len(DILATIONS))]
            mx = jnp.maximum(jnp.maximum(ls[0], ls[1]), ls[2])
            es = [jnp.exp2(l - mx) for l in ls]
            tot = es[0] + es[1] + es[2]
            inv = 1.0 / tot
            acc = (es[0] * inv) * onat[0, c0:c0 + ROWCH, :]
            for pi in (1, 2):
                acc = acc + (es[pi] * inv) * onat[pi, c0:c0 + ROWCH, :]
            o_ref[c0:c0 + ROWCH, :] = acc
            lse_ref[c0:c0 + ROWCH, :] = mx + jnp.log2(tot)

    spec_in = lambda off: pl.BlockSpec((None, S, LANES), lambda b, j: (off * NS + j, b, 0))
    vec = pl.BlockSpec((1, LANES), lambda b, j: (0, 0))
    out = pl.BlockSpec((S, LANES), lambda b, j: (b, j))
    return pl.pallas_call(
        body, grid=(Bl, NS), name="attn_fwd",
        in_specs=[spec_in(0), spec_in(1), spec_in(2), vec, vec,
                  pl.BlockSpec((None, 8, LANES), lambda b, j: (j, 0, 0))],
        out_specs=[out, out],
        out_shape=[jax.ShapeDtypeStruct((T, NS * LANES), F32)] * 2,
        scratch_shapes=[pltpu.VMEM((S, LANES), F32), pltpu.VMEM((S, LANES), F32),
                        pltpu.VMEM((S, LANES), MXU_DTYPE), pltpu.VMEM((PADR, LANES), MXU_DTYPE),
                        pltpu.VMEM((PADR, LANES), MXU_DTYPE),
                        pltpu.VMEM((S, LANES), F32), pltpu.VMEM((S, LANES), F32),
                        pltpu.VMEM((3, S, LANES), F32), pltpu.VMEM((3, S, LANES), F32),
                        pltpu.VMEM((N_EDGE * len(DILATIONS), 2 * QBLK, 2 * QBLK), F32),
                        pltpu.VMEM((PIPE_SLOTS, 2 * QBLK, 2 * QBLK), F32),
                        pltpu.VMEM((PIPE_SLOTS, 2 * QBLK, 2 * QBLK), MXU_DTYPE),
                        pltpu.VMEM((PIPE_SLOTS, 2 * QBLK, LANES), F32), pltpu.VMEM((PIPE_SLOTS, 2 * QBLK, LANES), F32)],
        compiler_params=_cp(("arbitrary", "arbitrary")),
    )(qkv, qkv, qkv, gq2, gk2, slopes)


def _attn_bwd(qkv, o, lse, do, gq2, gk2, slopes, *, Bl, S):
    n3, T, _ = qkv.shape
    NS = n3 // 3
    NB = S // QBLK
    PADR = S + 2 * RADIUS * DILATIONS[-1]
    QSCALE = HEAD_DIM ** -0.5

    def body(q_ref, k_ref, v_ref, o_ref, lse_ref, do_ref, gq_ref, gk_ref, slope_ref,
             dq_ref, dk_ref, dv_ref, gacc_ref,
             qh, kh, dl, qp, kp, vp, dop, lp, dlp, dqp, dkacc, dvacc, dqn, dkn, bias_ref,
             sbuf, dpbuf, pbuf, dsbuf):
        first_step = jnp.logical_and(pl.program_id(0) == 0, pl.program_id(1) == 0)

        @pl.when(first_step)
        def _():
            gacc_ref[...] = jnp.zeros_like(gacc_ref)

        mm = _head_mean_matrix()
        lane_lo = lax.broadcasted_iota(jnp.int32, (QBLK, LANES), 1) < HEAD_DIM
        _bias_tables(bias_ref, slope_ref)
        _qk_normalize(q_ref, k_ref, gq_ref, gk_ref, qh, kh, S, mm)
        for c0 in range(0, S, ROWCH):
            dl[c0:c0 + ROWCH, :] = _head_mean(do_ref[c0:c0 + ROWCH, :] * o_ref[c0:c0 + ROWCH, :], mm) * HEAD_DIM
            dqn[c0:c0 + ROWCH, :] = jnp.zeros((ROWCH, LANES), F32)
            dkn[c0:c0 + ROWCH, :] = jnp.zeros((ROWCH, LANES), F32)
            dv_ref[c0:c0 + ROWCH, :] = jnp.zeros((ROWCH, LANES), F32)

        for pi, d in enumerate(DILATIONS):
            n = S // d
            nb = n // QBLK
            _gather_rows(qh, qp, S, d, pad=False)
            _gather_rows(kh, kp, S, d, pad=True)
            _gather_rows(v_ref, vp, S, d, pad=True)
            _gather_rows(do_ref, dop, S, d, pad=False)
            _gather_rows(lse_ref, lp, S, d, pad=False)
            _gather_rows(dl, dlp, S, d, pad=False)
            used = d * (n + 2 * RADIUS)
            for par in range(2):
                for c0 in range(0, used, ROWCH):
                    dkacc[par, c0:c0 + ROWCH, :] = jnp.zeros((ROWCH, LANES), F32)
                    dvacc[par, c0:c0 + ROWCH, :] = jnp.zeros((ROWCH, LANES), F32)

            def offsets(i, nb=nb):
                r = i // nb
                return pl.multiple_of(i * QBLK, QBLK), pl.multiple_of((i + r) * QBLK, QBLK), i % nb

            def scores(i, pi=pi, nb=nb):
                q0, k0, qb = offsets(i)
                qs = _stack_heads(qp[pl.ds(q0, QBLK), :], lane_lo)
                dos = _stack_heads(dop[pl.ds(q0, QBLK), :], lane_lo)
                sbuf[i % PIPE_SLOTS] = (_dot_nt(qs, kp[pl.ds(k0, 2 * QBLK), :])
                                        + bias_ref[N_EDGE * pi + _edge_index(qb, nb)])
                dpbuf[i % PIPE_SLOTS] = _dot_nt(dos, vp[pl.ds(k0, 2 * QBLK), :])

            def probs(i):
                q0, _, _ = offsets(i)
                lblk = lp[pl.ds(q0, QBLK), :]
                dblk = dlp[pl.ds(q0, QBLK), :]
                lcol = jnp.concatenate([lblk[:, 0:1], lblk[:, HEAD_DIM:HEAD_DIM + 1]], axis=0)
                dcol = jnp.concatenate([dblk[:, 0:1], dblk[:, HEAD_DIM:HEAD_DIM + 1]], axis=0)
                p = jnp.exp2(sbuf[i % PIPE_SLOTS] - lcol)
                pbuf[i % PIPE_SLOTS] = p.astype(MXU_DTYPE)
                dsbuf[i % PIPE_SLOTS] = (p * (dpbuf[i % PIPE_SLOTS] - dcol)).astype(MXU_DTYPE)

            def grads(i):
                q0, k0, _ = offsets(i)
                qs = _stack_heads(qp[pl.ds(q0, QBLK), :], lane_lo)
                dos = _stack_heads(dop[pl.ds(q0, QBLK), :], lane_lo)
                ds = dsbuf[i % PIPE_SLOTS]
                dvacc[i % 2, pl.ds(k0, 2 * QBLK), :] = _dot_tn(pbuf[i % PIPE_SLOTS], dos)
                dkacc[i % 2, pl.ds(k0, 2 * QBLK), :] = _dot_tn(ds, qs)
                dqp[pl.ds(q0, QBLK), :] = _merge_heads(_dot(ds, kp[pl.ds(k0, 2 * QBLK), :]), lane_lo)

            _pipeline(NB, [scores, probs, grads], PIPE_UNROLL)
            _scatter_rows(dqp, dqn, S, d, pad=False, accumulate=True)
            for par in range(2):
                _scatter_rows(dkacc.at[par], dkn, S, d, pad=True, accumulate=True)
                _scatter_rows(dvacc.at[par], dv_ref, S, d, pad=True, accumulate=True)

        gq_sum = jnp.zeros((8, LANES), F32)
        gk_sum = jnp.zeros((8, LANES), F32)
        for c0 in range(0, S, ROWCH):
            for src_ref, dn, g_ref, dst_ref, scale, is_q in ((q_ref, dqn, gq_ref, dq_ref, QSCALE, True),
                                                             (k_ref, dkn, gk_ref, dk_ref, LN2, False)):
                x = src_ref[c0:c0 + ROWCH, :]
                dh = dn[c0:c0 + ROWCH, :]
                rr = lax.rsqrt(_head_mean(x * x, mm) + EPS)
                e = dh * (g_ref[...] * scale)
                dst_ref[c0:c0 + ROWCH, :] = rr * e - x * (rr * rr * rr) * _head_mean(e * x, mm)
                gpart = dh * (x * rr * scale)
                acc8 = gpart[0:8, :]
                for q8 in range(1, ROWCH // 8):
                    acc8 = acc8 + gpart[8 * q8:8 * q8 + 8, :]
                if is_q:
                    gq_sum = gq_sum + acc8
                else:
                    gk_sum = gk_sum + acc8
        gacc_ref[0:1, :] += jnp.sum(gq_sum, axis=0, keepdims=True)
        gacc_ref[1:2, :] += jnp.sum(gk_sum, axis=0, keepdims=True)

    spec_in = lambda off: pl.BlockSpec((None, S, LANES), lambda b, j: (off * NS + j, b, 0))
    tok = pl.BlockSpec((S, LANES), lambda b, j: (b, j))
    vec = pl.BlockSpec((1, LANES), lambda b, j: (0, 0))
    slab_out = pl.BlockSpec((None, S, LANES), lambda b, j: (j, b, 0))
    f32buf = lambda rows: pltpu.VMEM((rows, LANES), F32)
    bfbuf = lambda rows: pltpu.VMEM((rows, LANES), MXU_DTYPE)
    return pl.pallas_call(
        body, grid=(Bl, NS), name="attn_bwd",
        in_specs=[spec_in(0), spec_in(1), spec_in(2), tok, tok, tok, vec, vec,
                  pl.BlockSpec((None, 8, LANES), lambda b, j: (j, 0, 0))],
        out_specs=[slab_out, slab_out, slab_out, pl.BlockSpec((8, LANES), lambda b, j: (0, 0))],
        out_shape=[jax.ShapeDtypeStruct((NS, T, LANES), F32)] * 3 + [jax.ShapeDtypeStruct((8, LANES), F32)],
        scratch_shapes=[f32buf(S), f32buf(S), f32buf(S),
                        bfbuf(S), bfbuf(PADR), bfbuf(PADR), bfbuf(S),
                        f32buf(S), f32buf(S), f32buf(S),
                        pltpu.VMEM((2, PADR, LANES), F32), pltpu.VMEM((2, PADR, LANES), F32),
                        f32buf(S), f32buf(S),
                        pltpu.VMEM((N_EDGE * len(DILATIONS), 2 * QBLK, 2 * QBLK), F32),
                        pltpu.VMEM((PIPE_SLOTS, 2 * QBLK, 2 * QBLK), F32),
                        pltpu.VMEM((PIPE_SLOTS, 2 * QBLK, 2 * QBLK), F32),
                        pltpu.VMEM((PIPE_SLOTS, 2 * QBLK, 2 * QBLK), MXU_DTYPE),
                        pltpu.VMEM((PIPE_SLOTS, 2 * QBLK, 2 * QBLK), MXU_DTYPE)],
        compiler_params=_cp(("arbitrary", "arbitrary")),
    )(qkv, qkv, qkv, o, lse, do, gq2, gk2, slopes)


def _layer_norm_parts(cv, g_ln, b_ln):
    mu = jnp.mean(cv, axis=-1, keepdims=True)
    cen = cv - mu
    rs = lax.rsqrt(jnp.mean(cen * cen, axis=-1, keepdims=True) + EPS)
    z = cen * rs
    return z, rs, z * g_ln + b_ln


def _ffn_fwd(x2, cv, ya, tgt, mod, g_ln, b_ln, g_ffn, w_out, w_gate, w_up, w_down, *, S, tm):
    T, D = x2.shape
    DC = cv.shape[1]
    P, Kb, _ = w_out.shape
    Fb = w_gate.shape[2]
    tps = S // tm

    def body(x_ref, cv_ref, ya_ref, t_ref, mod_ref, gln_ref, bln_ref, gf_ref, wo_hbm, wg_hbm, wu_hbm, wd_hbm,
             x1_ref, ycat_ref, mix_ref, h2_ref, g_ref, u_ref, a_ref, f_ref, dy_ref, loss_ref,
             wo, wg, wu, wd, sems):
        i = pl.program_id(0)
        _load_resident(i, [(wo_hbm, wo), (wg_hbm, wg), (wu_hbm, wu), (wd_hbm, wd)], sems)

        @pl.when(i == 0)
        def _():
            loss_ref[...] = jnp.zeros_like(loss_ref)

        _, _, ln = _layer_norm_parts(cv_ref[...], gln_ref[...], bln_ref[...])
        yc = ln * _sigmoid(ln)
        ycat = jnp.concatenate([yc, ya_ref[...]], axis=1).astype(MXU_DTYPE)
        ycat_ref[...] = ycat
        mix = _dot(ycat[:, 0:Kb], wo[0])
        for p in range(1, P):
            mix = mix + _dot(ycat[:, Kb * p:Kb * (p + 1)], wo[p])
        mix_ref[...] = mix.astype(ACT_DTYPE)
        x1 = x_ref[...] + mod_ref[:, 2 * D:3 * D] * mix
        x1_ref[...] = x1
        r2 = lax.rsqrt(jnp.mean(x1 * x1, axis=-1, keepdims=True) + EPS)
        h2 = (x1 * r2 * gf_ref[...]) * (1.0 + mod_ref[:, 4 * D:5 * D]) + mod_ref[:, 3 * D:4 * D]
        h2b = h2.astype(MXU_DTYPE)
        h2_ref[...] = h2b
        f = jnp.zeros((tm, D), F32)
        for p in range(P):
            g = _dot(h2b, wg[p])
            u = _dot(h2b, wu[p])
            a = (g * _sigmoid(g) * u).astype(MXU_DTYPE)
            g_ref[p] = g.astype(ACT_DTYPE)
            u_ref[p] = u.astype(ACT_DTYPE)
            a_ref[p] = a
            f = f + _dot(a, wd[p])
        f_ref[...] = f.astype(ACT_DTYPE)
        err = x1 + mod_ref[:, 5 * D:6 * D] * f - t_ref[...]
        dy_ref[...] = err * (1.0 / D)
        tot = jnp.sum(_colsum(err * err), axis=1, keepdims=True)
        loss_ref[...] += tot * (0.5 / D)

    row = lambda w: pl.BlockSpec((tm, w), lambda i: (i, 0))
    vec = lambda w: pl.BlockSpec((1, w), lambda i: (0, 0))
    blk = pl.BlockSpec((P, tm, Fb), lambda i: (0, i, 0))
    anyspec = pl.BlockSpec(memory_space=pl.ANY)
    return pl.pallas_call(
        body, grid=(T // tm,), name="ffn_fwd",
        in_specs=[row(D), row(DC), row(D - DC), row(D),
                  pl.BlockSpec((None, 1, N_MOD * D), lambda i: (i // tps, 0, 0)),
                  vec(DC), vec(DC), vec(D), anyspec, anyspec, anyspec, anyspec],
        out_specs=[row(D), row(D), row(D), row(D), blk, blk, blk, row(D), row(D),
                   pl.BlockSpec((8, LANES), lambda i: (0, 0))],
        out_shape=[jax.ShapeDtypeStruct((T, D), F32), jax.ShapeDtypeStruct((T, D), MXU_DTYPE),
                   jax.ShapeDtypeStruct((T, D), ACT_DTYPE), jax.ShapeDtypeStruct((T, D), MXU_DTYPE),
                   jax.ShapeDtypeStruct((P, T, Fb), ACT_DTYPE), jax.ShapeDtypeStruct((P, T, Fb), ACT_DTYPE),
                   jax.ShapeDtypeStruct((P, T, Fb), MXU_DTYPE), jax.ShapeDtypeStruct((T, D), ACT_DTYPE),
                   jax.ShapeDtypeStruct((T, D), F32), jax.ShapeDtypeStruct((8, LANES), F32)],
        scratch_shapes=[pltpu.VMEM(w_out.shape, w_out.dtype), pltpu.VMEM(w_gate.shape, w_gate.dtype),
                        pltpu.VMEM(w_up.shape, w_up.dtype), pltpu.VMEM(w_down.shape, w_down.dtype),
                        pltpu.SemaphoreType.DMA((4,))],
        compiler_params=_cp(("arbitrary",)),
    )(x2, cv, ya, tgt, mod, g_ln, b_ln, g_ffn, w_out, w_gate, w_up, w_down)


def _ffn_bwd(dy, x1, gs, us, fo, mixb, cv, mod, g_ln, b_ln, g_ffn, w_out, w_gate, w_up, w_down, *, S, tm):
    T, D = dy.shape
    DC = cv.shape[1]
    P, Kb, _ = w_out.shape
    Fb = w_gate.shape[2]
    tps = S // tm
    Bl = T // S

    def body(dy_ref, x1_ref, g_ref, u_ref, f_ref, mix_ref, cv_ref, mod_ref, gln_ref, bln_ref, gf_ref,
             wo_hbm, wg_hbm, wu_hbm, wd_hbm,
             dg_ref, du_ref, df_ref, dx1_ref, dmix_ref, dya_ref, dcv_ref, macc_ref, gacc_ref, lacc_ref,
             wo, wg, wu, wd, sems):
        i = pl.program_id(0)
        _load_resident(i, [(wo_hbm, wo), (wg_hbm, wg), (wu_hbm, wu), (wd_hbm, wd)], sems)

        @pl.when(i == 0)
        def _():
            gacc_ref[...] = jnp.zeros_like(gacc_ref)
            lacc_ref[...] = jnp.zeros_like(lacc_ref)

        @pl.when(i % tps == 0)
        def _():
            macc_ref[...] = jnp.zeros_like(macc_ref)

        dy_t = dy_ref[...]
        x1 = x1_ref[...]
        gate_f = mod_ref[:, 5 * D:6 * D]
        macc_ref[2:3, :] += _colsum(dy_t * f_ref[...].astype(F32))
        dfb = (dy_t * gate_f).astype(MXU_DTYPE)
        df_ref[...] = dfb
        dh2 = jnp.zeros((tm, D), F32)
        for p in range(P):
            da = _dot_nt(dfb, wd[p])
            g = g_ref[p].astype(F32)
            u = u_ref[p].astype(F32)
            sg = _sigmoid(g)
            dgp = (da * u * (sg * (1.0 + g * (1.0 - sg)))).astype(MXU_DTYPE)
            dup = (da * (g * sg)).astype(MXU_DTYPE)
            dg_ref[p] = dgp
            du_ref[p] = dup
            dh2 = dh2 + _dot_nt(dgp, wg[p]) + _dot_nt(dup, wu[p])
        r2 = lax.rsqrt(jnp.mean(x1 * x1, axis=-1, keepdims=True) + EPS)
        xr = x1 * r2
        n2 = xr * gf_ref[...]
        macc_ref[0:1, :] += _colsum(dh2)
        macc_ref[1:2, :] += _colsum(dh2 * n2)
        dn2 = dh2 * (1.0 + mod_ref[:, 4 * D:5 * D])
        gacc_ref[0:1, :] += _colsum(dn2 * xr)
        e = dn2 * gf_ref[...]
        dx1 = dy_t + r2 * e - xr * (r2 * jnp.mean(e * xr, axis=-1, keepdims=True))
        dx1_ref[...] = dx1
        macc_ref[3:4, :] += _colsum(dx1 * mix_ref[...].astype(F32))
        dmixb = (dx1 * mod_ref[:, 2 * D:3 * D]).astype(MXU_DTYPE)
        dmix_ref[...] = dmixb
        parts = [_dot_nt(dmixb, wo[p]) for p in range(P)]
        dycat = jnp.concatenate(parts, axis=1) if P > 1 else parts[0]
        dya_ref[...] = dycat[:, DC:]
        dyc = dycat[:, :DC]
        z, rs, ln = _layer_norm_parts(cv_ref[...], gln_ref[...], bln_ref[...])
        sg = _sigmoid(ln)
        dln = dyc * (sg * (1.0 + ln * (1.0 - sg)))
        lacc_ref[0:1, :] += _colsum(dln * z)
        lacc_ref[1:2, :] += _colsum(dln)
        dz = dln * gln_ref[...]
        dcv_ref[...] = rs * (dz - jnp.mean(dz, axis=-1, keepdims=True) - z * jnp.mean(dz * z, axis=-1, keepdims=True))

    row = lambda w: pl.BlockSpec((tm, w), lambda i: (i, 0))
    vec = lambda w: pl.BlockSpec((1, w), lambda i: (0, 0))
    blk = pl.BlockSpec((P, tm, Fb), lambda i: (0, i, 0))
    anyspec = pl.BlockSpec(memory_space=pl.ANY)
    return pl.pallas_call(
        body, grid=(T // tm,), name="ffn_bwd",
        in_specs=[row(D), row(D), blk, blk, row(D), row(D), row(DC),
                  pl.BlockSpec((None, 1, N_MOD * D), lambda i: (i // tps, 0, 0)),
                  vec(DC), vec(DC), vec(D), anyspec, anyspec, anyspec, anyspec],
        out_specs=[blk, blk, row(D), row(D), row(D), row(D - DC), row(DC),
                   pl.BlockSpec((None, 8, D), lambda i: (i // tps, 0, 0)),
                   pl.BlockSpec((8, D), lambda i: (0, 0)), pl.BlockSpec((8, DC), lambda i: (0, 0))],
        out_shape=[jax.ShapeDtypeStruct((P, T, Fb), MXU_DTYPE), jax.ShapeDtypeStruct((P, T, Fb), MXU_DTYPE),
                   jax.ShapeDtypeStruct((T, D), MXU_DTYPE), jax.ShapeDtypeStruct((T, D), F32),
                   jax.ShapeDtypeStruct((T, D), MXU_DTYPE), jax.ShapeDtypeStruct((T, D - DC), F32),
                   jax.ShapeDtypeStruct((T, DC), F32), jax.ShapeDtypeStruct((Bl, 8, D), F32),
                   jax.ShapeDtypeStruct((8, D), F32), jax.ShapeDtypeStruct((8, DC), F32)],
        scratch_shapes=[pltpu.VMEM(w_out.shape, w_out.dtype), pltpu.VMEM(w_gate.shape, w_gate.dtype),
                        pltpu.VMEM(w_up.shape, w_up.dtype), pltpu.VMEM(w_down.shape, w_down.dtype),
                        pltpu.SemaphoreType.DMA((4,))],
        compiler_params=_cp(("arbitrary",)),
    )(dy, x1, gs, us, fo, mixb, cv, mod, g_ln, b_ln, g_ffn, w_out, w_gate, w_up, w_down)


def _in_bwd(da, dg, dq, dk, dv, x2, dx1, mod, g_mix, w_in, *, S, tm):
    T, D = x2.shape
    P, _, Nb = w_in.shape
    DC = da.shape[1]
    NS = dq.shape[0]
    n_in = P * Nb
    tps = S // tm
    Bl = T // S

    def body(da_ref, dg_ref, dq_ref, dk_ref, dv_ref, x_ref, dx1_ref, mod_ref, g_ref, w_ref,
             dx_ref, dproj_ref, macc_ref, gacc_ref):
        i = pl.program_id(0)

        @pl.when(i == 0)
        def _():
            gacc_ref[...] = jnp.zeros_like(gacc_ref)

        @pl.when(i % tps == 0)
        def _():
            macc_ref[...] = jnp.zeros_like(macc_ref)

        pieces = [da_ref[...], dg_ref[...]] + [r[j] for r in (dq_ref, dk_ref, dv_ref) for j in range(NS)]
        dproj = jnp.concatenate(pieces, axis=1).astype(MXU_DTYPE)
        dproj_ref[...] = dproj
        dh = _dot_nt(dproj[:, 0:Nb], w_ref[0])
        for p in range(1, P):
            dh = dh + _dot_nt(dproj[:, Nb * p:Nb * (p + 1)], w_ref[p])
        x = x_ref[...]
        r = lax.rsqrt(jnp.mean(x * x, axis=-1, keepdims=True) + EPS)
        xr = x * r
        macc_ref[0:1, :] += _colsum(dh)
        macc_ref[1:2, :] += _colsum(dh * (xr * g_ref[...]))
        dn = dh * (1.0 + mod_ref[:, D:2 * D])
        gacc_ref[0:1, :] += _colsum(dn * xr)
        e = dn * g_ref[...]
        dx_ref[...] = dx1_ref[...] + r * e - xr * (r * jnp.mean(e * xr, axis=-1, keepdims=True))

    row = lambda w: pl.BlockSpec((tm, w), lambda i: (i, 0))
    slab = pl.BlockSpec((NS, tm, LANES), lambda i: (0, i, 0))
    return pl.pallas_call(
        body, grid=(T // tm,), name="in_bwd",
        in_specs=[row(DC), row(DC), slab, slab, slab, row(D), row(D),
                  pl.BlockSpec((None, 1, N_MOD * D), lambda i: (i // tps, 0, 0)),
                  pl.BlockSpec((1, D), lambda i: (0, 0)),
                  pl.BlockSpec((P, D, Nb), lambda i: (0, 0, 0))],
        out_specs=[row(D), row(n_in), pl.BlockSpec((None, 8, D), lambda i: (i // tps, 0, 0)),
                   pl.BlockSpec((8, D), lambda i: (0, 0))],
        out_shape=[jax.ShapeDtypeStruct((T, D), F32), jax.ShapeDtypeStruct((T, n_in), MXU_DTYPE),
                   jax.ShapeDtypeStruct((Bl, 8, D), F32), jax.ShapeDtypeStruct((8, D), F32)],
        compiler_params=_cp(("arbitrary",)),
    )(da, dg, dq, dk, dv, x2, dx1, mod, g_mix, w_in)


def _wgrad(a, b, *, P, name, tk, split=None):
    a_blk, b_blk = a.ndim == 3, b.ndim == 3
    T = a.shape[-2]
    if a_blk:
        R, C = a.shape[2], b.shape[1]
        a_of = lambda av, p: av[p]
        b_of = lambda bv, p: bv[...]
    elif b_blk:
        R, C = a.shape[1], b.shape[2]
        a_of = lambda av, p: av[...]
        b_of = lambda bv, p: bv[p]
    elif split == "a":
        R, C = a.shape[1] // P, b.shape[1]
        a_of = lambda av, p: av[:, R * p:R * (p + 1)]
        b_of = lambda bv, p: bv[...]
    else:
        R, C = a.shape[1], b.shape[1] // P
        a_of = lambda av, p: av[...]
        b_of = lambda bv, p: bv[:, C * p:C * (p + 1)]

    def body(a_ref, b_ref, o_ref):
        @pl.when(pl.program_id(0) == 0)
        def _():
            o_ref[...] = jnp.zeros_like(o_ref)
        for p in range(P):
            o_ref[p] += _dot_tn(a_of(a_ref, p), b_of(b_ref, p))

    def spec(v):
        if v.ndim == 3:
            return pl.BlockSpec((P, tk, v.shape[2]), lambda k: (0, k, 0))
        return pl.BlockSpec((tk, v.shape[1]), lambda k: (k, 0))

    return pl.pallas_call(
        body, grid=(T // tk,), name=name,
        in_specs=[spec(a), spec(b)],
        out_specs=pl.BlockSpec((P, R, C), lambda k: (0, 0, 0)),
        out_shape=jax.ShapeDtypeStruct((P, R, C), F32),
        compiler_params=_cp(("arbitrary",)),
    )(a, b)


TM_IN = 512
TM_FFN = 256
TK_WGRAD = 512


def _alibi_slabs(n_slab):
    heads = 2 * n_slab
    slopes = 2.0 ** (-8.0 * np.arange(1, heads + 1) / heads)
    return jnp.asarray(np.broadcast_to(np.repeat(slopes.reshape(n_slab, 1, 2), HEAD_DIM, axis=2), (n_slab, 8, LANES)),
                       dtype=F32)


def _local_step(x, tgt, mod, g_mix, wdw, g_ln, b_ln, g_q, g_k, g_ffn, w_in, w_out, w_gate, w_up, w_down):
    Bl, S, D = x.shape
    T = Bl * S
    DC = g_ln.shape[1]
    P = w_in.shape[0]
    n_slab = (D - DC) // LANES
    x2 = x.reshape(T, D)
    t2 = tgt.reshape(T, D)
    mod3 = mod.reshape(Bl, 1, N_MOD * D)
    gq2 = jnp.tile(g_q, (1, LANES // HEAD_DIM))
    gk2 = jnp.tile(g_k, (1, LANES // HEAD_DIM))
    slopes = _alibi_slabs(n_slab)

    ag, qkv, h1 = _fwd_in(x2, mod3, g_mix, w_in, S=S, tm=TM_IN, n_ag=2 * DC)
    cv = _conv_fwd(ag, wdw, Bl=Bl, S=S, DC=DC)
    ya, lse = _attn_fwd(qkv, gq2, gk2, slopes, Bl=Bl, S=S)
    x1, ycat, mixb, h2, gs, us, acts, fo, dy, lossb = _ffn_fwd(
        x2, cv, ya, t2, mod3, g_ln, b_ln, g_ffn, w_out, w_gate, w_up, w_down, S=S, tm=TM_FFN)
    dgs, dus, dfb, dx1, dmixb, dya, dcv, macc_f, gacc_f, lacc = _ffn_bwd(
        dy, x1, gs, us, fo, mixb, cv, mod3, g_ln, b_ln, g_ffn, w_out, w_gate, w_up, w_down, S=S, tm=TM_FFN)
    dq, dk, dv, gqk = _attn_bwd(qkv, ya, lse, dya, gq2, gk2, slopes, Bl=Bl, S=S)
    da, dg, dwdw = _conv_bwd(ag, dcv, wdw, Bl=Bl, S=S, DC=DC)
    dx, dprojb, macc_m, gacc_m = _in_bwd(da, dg, dq, dk, dv, x2, dx1, mod3, g_mix, w_in, S=S, tm=TM_IN)

    grads = dict(
        w_in=_wgrad(h1, dprojb, P=P, name="wgrad_in", tk=TK_WGRAD, split="b"),
        w_out=_wgrad(ycat, dmixb, P=P, name="wgrad_out", tk=TK_WGRAD, split="a"),
        w_gate=_wgrad(h2, dgs, P=P, name="wgrad_gate", tk=TK_WGRAD),
        w_up=_wgrad(h2, dus, P=P, name="wgrad_up", tk=TK_WGRAD),
        w_down=_wgrad(acts, dfb, P=P, name="wgrad_down", tk=TK_WGRAD),
    )
    packed = _pack_small(macc_m, macc_f, gacc_m, gacc_f, lacc, gqk, dwdw)
    return dict(loss=lossb[0, 0], dx=dx.reshape(Bl, S, D), grads=grads, packed=packed)


def _small_layout(Bl):
    return 8 * Bl, 8 * Bl + 8, 8 * Bl + 8 + CONV_ROWS


def _pack_small(macc_m, macc_f, gacc_m, gacc_f, lacc, gqk, dwdw):
    Bl, _, D = macc_m.shape
    DC = lacc.shape[1]
    assert 2 * DC <= D
    SMALL_GAIN_ROW, SMALL_TAP_ROW, SMALL_ROWS = _small_layout(Bl)

    def body(mm_ref, mf_ref, gm_ref, gf_ref, la_ref, qk_ref, dw_ref, o_ref):
        o_ref[...] = jnp.zeros_like(o_ref)
        for b in range(Bl):
            o_ref[8 * b + 0:8 * b + 2, :] = mm_ref[b, 0:2, :]
            o_ref[8 * b + 2:8 * b + 3, :] = mf_ref[b, 3:4, :]
            o_ref[8 * b + 3:8 * b + 6, :] = mf_ref[b, 0:3, :]
        r = SMALL_GAIN_ROW
        o_ref[r:r + 1, :] = gm_ref[0:1, :]
        o_ref[r + 1:r + 2, :] = gf_ref[0:1, :]
        o_ref[r + 2:r + 3, 0:DC] = la_ref[0:1, :]
        o_ref[r + 2:r + 3, DC:2 * DC] = la_ref[1:2, :]
        qk = qk_ref[0:2, 0:HEAD_DIM] + qk_ref[0:2, HEAD_DIM:2 * HEAD_DIM]
        o_ref[r + 3:r + 4, 0:HEAD_DIM] = qk[0:1, :]
        o_ref[r + 3:r + 4, HEAD_DIM:2 * HEAD_DIM] = qk[1:2, :]
        o_ref[SMALL_TAP_ROW:SMALL_TAP_ROW + CONV_ROWS, 0:DC] = dw_ref[...]

    return pl.pallas_call(body, name="pack_small", out_shape=jax.ShapeDtypeStruct((SMALL_ROWS, D), F32),
                          compiler_params=_cp())(macc_m, macc_f, gacc_m, gacc_f, lacc, gqk, dwdw)


def _row_tile(rows, cap=512):
    if rows <= cap:
        return rows
    best = rows
    for t in range(8, cap + 1, 8):
        if rows % t == 0:
            best = t
    return best


def _cast_weight(w, pidx, name):
    def body(p_ref, w_ref, o_ref):
        o_ref[...] = w_ref[...].astype(MXU_DTYPE)
    R, C = w.shape
    tr = _row_tile(R)
    return pl.pallas_call(
        body, name=name,
        grid_spec=pltpu.PrefetchScalarGridSpec(
            num_scalar_prefetch=1, grid=(R // tr,),
            in_specs=[pl.BlockSpec((tr, C), lambda i, p: (i, 0))],
            out_specs=pl.BlockSpec((None, tr, C), lambda i, p: (p[0], i, 0))),
        out_shape=jax.ShapeDtypeStruct((4, R, C), MXU_DTYPE),
    )(pidx, w)


def _pair_add(g, recv, cidx, name):
    P, R, C = g.shape
    R2 = R // 2

    def body(c_ref, g_ref, r_ref, o_ref, ob_ref):
        s = g_ref[...] + r_ref[...]
        o_ref[...] = s
        ob_ref[...] = s.astype(jnp.bfloat16)

    return pl.pallas_call(
        body, name=name,
        grid_spec=pltpu.PrefetchScalarGridSpec(
            num_scalar_prefetch=1, grid=(P,),
            in_specs=[pl.BlockSpec((None, R2, C), lambda p, c: (p, c[0], 0)),
                      pl.BlockSpec((None, R2, C), lambda p, c: (p, 0, 0))],
            out_specs=[pl.BlockSpec((None, R2, C), lambda p, c: (p, 0, 0)),
                       pl.BlockSpec((None, R2, C), lambda p, c: (p, 0, 0))]),
        out_shape=[jax.ShapeDtypeStruct((P, R2, C), F32), jax.ShapeDtypeStruct((P, R2, C), jnp.bfloat16)],
    )(cidx, g, recv)


def _final_add(chipsum, recv, pc_idx, name):
    P, R2, C = chipsum.shape

    def body(pc_ref, s_ref, r_ref, o_ref):
        acc = s_ref[...]
        for k in range(3):
            acc = acc + r_ref[k].astype(F32)
        o_ref[...] = acc

    return pl.pallas_call(
        body, name=name,
        grid_spec=pltpu.PrefetchScalarGridSpec(
            num_scalar_prefetch=1, grid=(1,),
            in_specs=[pl.BlockSpec((None, R2, C), lambda i, pc: (pc[0], 0, 0)),
                      pl.BlockSpec((3, R2, C), lambda i, pc: (0, 0, 0))],
            out_specs=pl.BlockSpec((R2, C), lambda i, pc: (pc[1], 0))),
        out_shape=jax.ShapeDtypeStruct((2 * R2, C), F32),
    )(pc_idx, chipsum, recv)


def _adamw(w, g, m, v, name):
    R, C = w.shape
    tr = _row_tile(R, 256)
    c1 = 1.0 - ADAM_B1 ** ADAM_STEP
    c2 = 1.0 - ADAM_B2 ** ADAM_STEP

    def body(w_ref, g_ref, m_ref, v_ref, d_ref, nm_ref, nv_ref):
        gg = g_ref[...]
        nm = ADAM_B1 * m_ref[...] + (1.0 - ADAM_B1) * gg
        nv = ADAM_B2 * v_ref[...] + (1.0 - ADAM_B2) * (gg * gg)
        nm_ref[...] = nm
        nv_ref[...] = nv
        d_ref[...] = -ADAM_LR * ((nm / c1) / (jnp.sqrt(nv / c2) + ADAM_EPS) + ADAM_WD * w_ref[...])

    spec = pl.BlockSpec((tr, C), lambda i: (i, 0))
    return pl.pallas_call(
        body, grid=(R // tr,), name=name,
        in_specs=[spec] * 4, out_specs=[spec] * 3,
        out_shape=[jax.ShapeDtypeStruct((R, C), F32)] * 3,
    )(w, g, m, v)


def _ada_fwd(c_all, w_ada, b_cols):
    def body(c_ref, w_ref, b_ref, o_ref):
        c = c_ref[...]
        o_ref[...] = jnp.dot(c * _sigmoid(c), w_ref[...], preferred_element_type=F32, precision=HIGHEST) + b_ref[...]
    return pl.pallas_call(
        body, name="ada_fwd", out_shape=jax.ShapeDtypeStruct((c_all.shape[0], w_ada.shape[1]), F32),
        compiler_params=_cp(),
    )(c_all, w_ada, b_cols)


def _ada_bwd(c_all, dmod_cols):
    def body(c_ref, d_ref, o_ref):
        c = c_ref[...]
        o_ref[...] = lax.dot_general(c * _sigmoid(c), d_ref[...], (((0,), (0,)), ((), ())),
                                     preferred_element_type=F32, precision=HIGHEST)
    return pl.pallas_call(
        body, name="ada_bwd", out_shape=jax.ShapeDtypeStruct((c_all.shape[1], dmod_cols.shape[1]), F32),
        compiler_params=_cp(),
    )(c_all, dmod_cols)


def _small_reduce(gathered, n_dev, Bl):
    mod_rows, _, rows = _small_layout(Bl)
    width = gathered.shape[1]

    def body(g_ref, red_ref, bada_ref):
        acc = g_ref[0:rows, :]
        for d in range(1, n_dev):
            acc = acc + g_ref[d * rows:(d + 1) * rows, :]
        red_ref[...] = acc[mod_rows:, :]
        b = acc[0:8, :]
        for q in range(1, Bl):
            b = b + acc[8 * q:8 * q + 8, :]
        bada_ref[...] = b
    return pl.pallas_call(
        body, name="small_reduce",
        out_shape=[jax.ShapeDtypeStruct((rows - mod_rows, width), F32), jax.ShapeDtypeStruct((8, width), F32)],
        compiler_params=_cp(),
    )(gathered)


def _mesh_pos():
    return lax.axis_index("x"), lax.axis_index("y"), lax.axis_index("c")


def _other_chips(x, y):
    return [(1 - x, y), (x, 1 - y), (1 - x, 1 - y)]


def _allgather8(xs, name):
    m_per, n = xs.shape

    def body(x_ref, out_ref, send_sems, recv_sems, local_sem):
        x, y, c = _mesh_pos()
        me, sibling = (x, y, c), (x, y, 1 - c)
        chips = _other_chips(x, y)

        def rows(px, py, pc):
            return out_ref.at[pl.ds((4 * px + 2 * py + pc) * m_per, m_per), :]

        def copy(k, block, to, src=None):
            return pltpu.make_async_remote_copy(
                src_ref=rows(*block) if src is None else src, dst_ref=rows(*block),
                send_sem=send_sems.at[k], recv_sem=recv_sems.at[k], device_id=to, device_id_type=MESH_DEV)

        mine = pltpu.make_async_copy(x_ref, rows(*me), local_sem)
        mine.start()
        first = [copy(0, me, sibling, src=x_ref)]
        first += [copy(1 + j, me, (*chip, c), src=x_ref) for j, chip in enumerate(chips)]
        for cp in first:
            cp.start()
        passed = [copy(4 + j, (*chip, c), sibling) for j, chip in enumerate(chips)]
        for j, chip in enumerate(chips):
            copy(1 + j, (*chip, c), me).wait_recv()
            passed[j].start()
        copy(0, sibling, me).wait_recv()
        for j, chip in enumerate(chips):
            copy(4 + j, (*chip, 1 - c), me).wait_recv()
        for cp in first + passed:
            cp.wait_send()
        mine.wait()

    return pl.pallas_call(
        body, name=name, out_shape=jax.ShapeDtypeStruct((8 * m_per, n), xs.dtype),
        in_specs=[pl.BlockSpec(memory_space=pltpu.VMEM)], out_specs=pl.BlockSpec(memory_space=pltpu.VMEM),
        scratch_shapes=[pltpu.SemaphoreType.DMA((7,)), pltpu.SemaphoreType.DMA((7,)), pltpu.SemaphoreType.DMA],
        compiler_params=_cp(),
    )(xs)


def _gather_weights(bufs):
    n = len(bufs)

    def body(*refs):
        outs = refs[n:2 * n]
        send_sems, recv_sems = refs[2 * n:]
        x, y, c = _mesh_pos()
        p = 2 * x + y
        sibling = (x, y, 1 - c)
        chips = _other_chips(x, y)

        def copy(w, k, slot, h, to):
            r2 = bufs[w].shape[1] // 2
            blk = outs[w].at[slot, pl.ds(h * r2, r2), :]
            return pltpu.make_async_remote_copy(
                src_ref=blk, dst_ref=blk, send_sem=send_sems.at[6 * w + k], recv_sem=recv_sems.at[6 * w + k],
                device_id=to, device_id_type=MESH_DEV)

        sent = []
        for w in range(n):
            for k, chip in enumerate(chips):
                sent.append(copy(w, k, p, c, (*chip, c)))
                sent[-1].start()
        for w in range(n):
            for k, chip in enumerate(chips):
                slot = 2 * chip[0] + chip[1]
                copy(w, k, slot, c, sibling).wait_recv()
                sent.append(copy(w, 3 + k, slot, c, sibling))
                sent[-1].start()
        for w in range(n):
            for k, chip in enumerate(chips):
                copy(w, 3 + k, 2 * chip[0] + chip[1], 1 - c, sibling).wait_recv()
        for cp in sent:
            cp.wait_send()

    anyspec = pl.BlockSpec(memory_space=pl.ANY)
    return pl.pallas_call(
        body, name="gather_weights",
        out_shape=[jax.ShapeDtypeStruct(b.shape, b.dtype) for b in bufs],
        in_specs=[anyspec] * n, out_specs=[anyspec] * n,
        input_output_aliases={w: w for w in range(n)},
        scratch_shapes=[pltpu.SemaphoreType.DMA((6 * n,)), pltpu.SemaphoreType.DMA((6 * n,))],
    )(*bufs)


def _rs_sibling(grads):
    n = len(grads)

    def body(*refs):
        ins, outs = refs[:n], refs[n:2 * n]
        send_sems, recv_sems = refs[2 * n:]
        x, y, c = _mesh_pos()
        cps = []
        for w in range(n):
            P, R, _ = grads[w].shape
            r2 = R // 2
            for p in range(P):
                cps.append(pltpu.make_async_remote_copy(
                    src_ref=ins[w].at[p, pl.ds((1 - c) * r2, r2), :], dst_ref=outs[w].at[p],
                    send_sem=send_sems.at[4 * w + p], recv_sem=recv_sems.at[4 * w + p],
                    device_id=(x, y, 1 - c), device_id_type=MESH_DEV))
                cps[-1].start()
        for cp in cps:
            cp.wait()

    anyspec = pl.BlockSpec(memory_space=pl.ANY)
    return pl.pallas_call(
        body, name="rs_sibling",
        out_shape=[jax.ShapeDtypeStruct((g.shape[0], g.shape[1] // 2, g.shape[2]), g.dtype) for g in grads],
        in_specs=[anyspec] * n, out_specs=[anyspec] * n,
        scratch_shapes=[pltpu.SemaphoreType.DMA((4 * n,)), pltpu.SemaphoreType.DMA((4 * n,))],
    )(*grads)


def _rs_chips(sums):
    n = len(sums)

    def body(*refs):
        ins, outs = refs[:n], refs[n:2 * n]
        send_sems, recv_sems = refs[2 * n:]
        x, y, c = _mesh_pos()
        cps = []
        for w in range(n):
            for k, chip in enumerate(_other_chips(x, y)):
                cps.append(pltpu.make_async_remote_copy(
                    src_ref=ins[w].at[2 * chip[0] + chip[1]], dst_ref=outs[w].at[k],
                    send_sem=send_sems.at[3 * w + k], recv_sem=recv_sems.at[3 * w + k],
                    device_id=(*chip, c), device_id_type=MESH_DEV))
                cps[-1].start()
        for cp in cps:
            cp.wait()

    anyspec = pl.BlockSpec(memory_space=pl.ANY)
    return pl.pallas_call(
        body, name="rs_chips",
        out_shape=[jax.ShapeDtypeStruct((3,) + s.shape[1:], s.dtype) for s in sums],
        in_specs=[anyspec] * n, out_specs=[anyspec] * n,
        scratch_shapes=[pltpu.SemaphoreType.DMA((3 * n,)), pltpu.SemaphoreType.DMA((3 * n,))],
    )(*sums)


def _rs_final(bufs):
    n = len(bufs)

    def body(*refs):
        outs = refs[n:2 * n]
        send_sems, recv_sems = refs[2 * n:]
        x, y, c = _mesh_pos()
        cps = []
        for w in range(n):
            r2 = bufs[w].shape[0] // 2
            mine = outs[w].at[pl.ds(c * r2, r2), :]
            cps.append(pltpu.make_async_remote_copy(
                src_ref=mine, dst_ref=mine, send_sem=send_sems.at[w], recv_sem=recv_sems.at[w],
                device_id=(x, y, 1 - c), device_id_type=MESH_DEV))
            cps[-1].start()
        for cp in cps:
            cp.wait()

    anyspec = pl.BlockSpec(memory_space=pl.ANY)
    return pl.pallas_call(
        body, name="rs_final",
        out_shape=[jax.ShapeDtypeStruct(b.shape, b.dtype) for b in bufs],
        in_specs=[anyspec] * n, out_specs=[anyspec] * n,
        input_output_aliases={w: w for w in range(n)},
        scratch_shapes=[pltpu.SemaphoreType.DMA((n,)), pltpu.SemaphoreType.DMA((n,))],
    )(*bufs)


BIG = ("w_in", "w_out", "w_gate", "w_up", "w_down")
WEIGHTS = ("w_ada", "b_ada", "g_mix", "w_in", "w_dw", "b_dw", "g_conv_ln", "b_conv_ln", "g_q", "g_k",
           "w_out", "g_ffn", "w_gate", "w_up", "w_down")


def _pad_to(a, rows, cols):
    return jnp.pad(a, ((0, rows - a.shape[0]), (0, cols - a.shape[1])))


def kernel(x, c, w_ada, b_ada, g_mix, w_in, w_dw, b_dw, g_conv_ln, b_conv_ln, g_q, g_k, w_out, g_ffn, w_gate, w_up, w_down, loss_target, m_w_ada, m_b_ada, m_g_mix, m_w_in, m_w_dw, m_b_dw, m_g_conv_ln, m_b_conv_ln, m_g_q, m_g_k, m_w_out, m_g_ffn, m_w_gate, m_w_up, m_w_down, v_w_ada, v_b_ada, v_g_mix, v_w_in, v_w_dw, v_b_dw, v_g_conv_ln, v_b_conv_ln, v_g_q, v_g_k, v_w_out, v_g_ffn, v_w_gate, v_w_up, v_w_down):
    w = dict(w_ada=w_ada, b_ada=b_ada, g_mix=g_mix, w_in=w_in, w_dw=w_dw, b_dw=b_dw, g_conv_ln=g_conv_ln,
             b_conv_ln=b_conv_ln, g_q=g_q, g_k=g_k, w_out=w_out, g_ffn=g_ffn, w_gate=w_gate, w_up=w_up, w_down=w_down)
    m = dict(w_ada=m_w_ada, b_ada=m_b_ada, g_mix=m_g_mix, w_in=m_w_in, w_dw=m_w_dw, b_dw=m_b_dw, g_conv_ln=m_g_conv_ln,
             b_conv_ln=m_b_conv_ln, g_q=m_g_q, g_k=m_g_k, w_out=m_w_out, g_ffn=m_g_ffn, w_gate=m_w_gate, w_up=m_w_up,
             w_down=m_w_down)
    v = dict(w_ada=v_w_ada, b_ada=v_b_ada, g_mix=v_g_mix, w_in=v_w_in, w_dw=v_w_dw, b_dw=v_b_dw, g_conv_ln=v_g_conv_ln,
             b_conv_ln=v_b_conv_ln, g_q=v_g_q, g_k=v_g_k, w_out=v_w_out, g_ffn=v_g_ffn, w_gate=v_w_gate, w_up=v_w_up,
             w_down=v_w_down)
    Bl, S, D = x.shape
    DC = g_conv_ln.shape[1]
    NA = w_ada.shape[2]
    xi, yi, ci = _mesh_pos()
    p = 2 * xi + yi
    dev = 2 * p + ci
    n_dev = 8
    cidx = jnp.reshape(ci, (1,)).astype(jnp.int32)
    pidx = jnp.reshape(p, (1,)).astype(jnp.int32)

    first = jnp.concatenate([_pad_to(c, 8, D), _pad_to(w_dw[0], CONV_ROWS, D)], axis=0)
    g0 = _allgather8(first, "gather_cond").reshape(n_dev, 8 + CONV_ROWS, D)
    c_all = g0[:, :Bl].reshape(n_dev * Bl, D)
    taps = jnp.concatenate([g0[2 * q, 8:, :w_dw.shape[2]] for q in range(4)], axis=1)
    wdw = jnp.where(lax.broadcasted_iota(jnp.int32, taps.shape, 0) == CONV_WIDTH, b_dw, taps)
    full = dict(zip(BIG, _gather_weights([_cast_weight(w[nm][0], pidx, "cast_" + nm) for nm in BIG])))

    b_cols = lax.dynamic_slice_in_dim(b_ada, p * NA, NA, axis=1)
    mod_part = _ada_fwd(c_all, w_ada[0], b_cols)
    gm = _allgather8(mod_part, "gather_mod").reshape(n_dev, n_dev * Bl, NA)
    mod = jnp.concatenate([lax.dynamic_slice_in_dim(gm[2 * q], dev * Bl, Bl, axis=0) for q in range(4)], axis=1)

    loc = _local_step(x, loss_target, mod, g_mix, wdw, g_conv_ln, b_conv_ln, g_q, g_k, g_ffn,
                      full["w_in"], full["w_out"], full["w_gate"], full["w_up"], full["w_down"])
    loss = lax.psum(loc["loss"], ("x", "y", "c"))

    parts = [loc["grads"][nm] for nm in BIG]
    from_sib = _rs_sibling(parts)
    sums = [_pair_add(g, r, cidx, "pair_add_" + nm) for nm, g, r in zip(BIG, parts, from_sib)]
    from_chips = _rs_chips([sb for _, sb in sums])
    pc_idx = jnp.stack([p, ci]).astype(jnp.int32)
    halves = [_final_add(s32, r, pc_idx, "final_add_" + nm) for nm, (s32, _), r in zip(BIG, sums, from_chips)]
    grad = dict(zip(BIG, _rs_final(halves)))

    mod_rows, _, small_rows = _small_layout(Bl)
    gs = _allgather8(loc["packed"], "gather_small")
    red, bada8 = _small_reduce(gs, n_dev, Bl)
    dmod_all = gs.reshape(n_dev, small_rows, D)[:, :mod_rows].reshape(n_dev * Bl, 8, D)[:, :N_MOD].reshape(n_dev * Bl, N_MOD * D)
    grad["w_ada"] = _ada_bwd(c_all, lax.dynamic_slice_in_dim(dmod_all, p * NA, NA, axis=1))
    grad["b_ada"] = bada8[:N_MOD].reshape(1, N_MOD * D)
    grad["g_mix"] = red[0:1]
    grad["g_ffn"] = red[1:2]
    grad["g_conv_ln"] = red[2:3, :DC]
    grad["b_conv_ln"] = red[2:3, DC:2 * DC]
    grad["g_q"] = red[3:4, :HEAD_DIM]
    grad["g_k"] = red[3:4, HEAD_DIM:2 * HEAD_DIM]
    dwdw = red[8:8 + CONV_ROWS, :DC]
    grad["w_dw"] = lax.dynamic_slice_in_dim(dwdw[:CONV_WIDTH], p * w_dw.shape[2], w_dw.shape[2], axis=1)
    grad["b_dw"] = dwdw[CONV_WIDTH:CONV_WIDTH + 1]

    delta, new_m, new_v = {}, {}, {}
    for nm in WEIGHTS:
        shp = w[nm].shape
        two_d = (shp[-2], shp[-1]) if len(shp) == 3 else shp
        d_, m_, v_ = _adamw(w[nm].reshape(two_d), grad[nm].reshape(two_d), m[nm].reshape(two_d), v[nm].reshape(two_d),
                            "adamw_" + nm)
        grad[nm] = grad[nm].reshape(shp)
        delta[nm], new_m[nm], new_v[nm] = d_.reshape(shp), m_.reshape(shp), v_.reshape(shp)

    return (loss, loc["dx"], *[grad[nm] for nm in WEIGHTS], *[delta[nm] for nm in WEIGHTS],
            *[new_m[nm] for nm in WEIGHTS], *[new_v[nm] for nm in WEIGHTS])
```

```python
import functools
import math

import jax
import jax.numpy as jnp
import numpy as np
from jax import lax
from jax.experimental import pallas as pl
from jax.experimental.pallas import tpu as pltpu

F32 = jnp.float32
MXU_DTYPE = jnp.bfloat16
ACT_DTYPE = jnp.bfloat16
EPS = 1e-6
NEG_INF = -1e30
HEAD_DIM = 64
LANES = 128
RADIUS = 64
QBLK = 128
DILATIONS = (1, 4, 16)
CONV_WIDTH = 31
CONV_PAD = CONV_WIDTH // 2
CONV_ROWS = 32
N_MOD = 6
ADAM_LR, ADAM_B1, ADAM_B2, ADAM_EPS, ADAM_WD, ADAM_STEP = 0.001, 0.9, 0.999, 1e-08, 0.01, 10
HIGHEST = lax.Precision.HIGHEST
MESH_DEV = pl.DeviceIdType.MESH
VMEM_LIMIT = 56 << 20


def _cp(sem=None, vmem=VMEM_LIMIT):
    kw = dict(vmem_limit_bytes=vmem)
    if sem is not None:
        kw["dimension_semantics"] = sem
    return pltpu.CompilerParams(**kw)


def _sigmoid(x):
    return 1.0 / (1.0 + jnp.exp(-x))


def _dot(a, b):
    return jnp.dot(a, b, preferred_element_type=F32)


def _dot_nt(a, b):
    return lax.dot_general(a, b, (((1,), (1,)), ((), ())), preferred_element_type=F32)


def _dot_tn(a, b):
    return lax.dot_general(a, b, (((0,), (0,)), ((), ())), preferred_element_type=F32)


def _colsum(v):
    return jnp.sum(v, axis=0, keepdims=True)


def _load_resident(i, pairs, sems):
    @pl.when(i == 0)
    def _():
        cps = [pltpu.make_async_copy(src, dst, sems.at[n]) for n, (src, dst) in enumerate(pairs)]
        for c in cps:
            c.start()
        for c in cps:
            c.wait()


def _fwd_in(x2, mod, g_mix, w_in, *, S, tm, n_ag):
    T, D = x2.shape
    P, _, Nb = w_in.shape
    n_in = P * Nb
    n_slab = (n_in - n_ag) // LANES
    tps = S // tm

    def body(x_ref, mod_ref, g_ref, w_ref, ag_ref, qkv_ref, h_ref):
        x = x_ref[...]
        r = lax.rsqrt(jnp.mean(x * x, axis=-1, keepdims=True) + EPS)
        n = x * r * g_ref[...]
        h = n * (1.0 + mod_ref[:, D:2 * D]) + mod_ref[:, 0:D]
        hb = h.astype(MXU_DTYPE)
        h_ref[...] = hb
        parts = [_dot(hb, w_ref[p]) for p in range(P)]
        proj = jnp.concatenate(parts, axis=1) if P > 1 else parts[0]
        ag_ref[...] = proj[:, :n_ag]
        for j in range(n_slab):
            qkv_ref[j] = proj[:, n_ag + LANES * j:n_ag + LANES * (j + 1)]

    return pl.pallas_call(
        body, grid=(T // tm,), name="fwd_in",
        in_specs=[pl.BlockSpec((tm, D), lambda i: (i, 0)),
                  pl.BlockSpec((None, 1, N_MOD * D), lambda i: (i // tps, 0, 0)),
                  pl.BlockSpec((1, D), lambda i: (0, 0)),
                  pl.BlockSpec((P, D, Nb), lambda i: (0, 0, 0))],
        out_specs=[pl.BlockSpec((tm, n_ag), lambda i: (i, 0)),
                   pl.BlockSpec((n_slab, tm, LANES), lambda i: (0, i, 0)),
                   pl.BlockSpec((tm, D), lambda i: (i, 0))],
        out_shape=[jax.ShapeDtypeStruct((T, n_ag), F32),
                   jax.ShapeDtypeStruct((n_slab, T, LANES), F32),
                   jax.ShapeDtypeStruct((T, D), MXU_DTYPE)],
        compiler_params=_cp(("arbitrary",)),
    )(x2, mod, g_mix, w_in)


CONV_CH = 64


def _conv_taps(win, w_ref, acc, reverse):
    n = win.shape[0]
    for b in range(8):
        wb = win if b == 0 else pltpu.roll(win, shift=n - b, axis=0)
        for a in range(4):
            o = 8 * a + b
            if o < 1 or o > CONV_WIDTH:
                continue
            k = (CONV_WIDTH - o) if reverse else (o - 1)
            acc = acc + w_ref[k:k + 1, :] * wb[8 * a:8 * a + CONV_CH, :]
    return acc


def _conv_fwd(ag, wdw, *, Bl, S, DC):
    T = ag.shape[0]
    nsc = DC // LANES
    CH = CONV_CH

    def body(a_ref, g_ref, w_ref, cv_ref, upad):
        zeros16 = jnp.zeros((16, LANES), F32)
        upad[0:16, :] = zeros16
        upad[S + 16:S + 32, :] = zeros16

        def fill(i, _):
            r0 = pl.multiple_of(i * CH, CH)
            a = a_ref[pl.ds(r0, CH), :]
            g = g_ref[pl.ds(r0, CH), :]
            upad[pl.ds(r0 + 16, CH), :] = a * _sigmoid(g)
            return 0
        lax.fori_loop(0, S // CH, fill, 0)

        def conv(i, _):
            r0 = pl.multiple_of(i * CH, CH)
            win = upad[pl.ds(r0, CH + 32), :]
            acc = jnp.zeros((CH, LANES), F32) + w_ref[CONV_WIDTH:CONV_WIDTH + 1, :]
            cv_ref[pl.ds(r0, CH), :] = _conv_taps(win, w_ref, acc, reverse=False)
            return 0
        lax.fori_loop(0, S // CH, conv, 0)

    return pl.pallas_call(
        body, grid=(Bl, nsc), name="conv_fwd",
        in_specs=[pl.BlockSpec((S, LANES), lambda b, j: (b, j)),
                  pl.BlockSpec((S, LANES), lambda b, j: (b, nsc + j)),
                  pl.BlockSpec((CONV_ROWS, LANES), lambda b, j: (0, j))],
        out_specs=pl.BlockSpec((S, LANES), lambda b, j: (b, j)),
        out_shape=jax.ShapeDtypeStruct((T, DC), F32),
        scratch_shapes=[pltpu.VMEM((S + 32, LANES), F32)],
        compiler_params=_cp(("arbitrary", "arbitrary")),
    )(ag, ag, wdw)


def _conv_bwd(ag, dcv, wdw, *, Bl, S, DC):
    T = ag.shape[0]
    nsc = DC // LANES
    CH = CONV_CH

    def body(a_ref, g_ref, d_ref, w_ref, da_ref, dg_ref, dw_ref, upad, dpad, wacc):
        b = pl.program_id(1)
        zeros16 = jnp.zeros((16, LANES), F32)
        upad[0:16, :] = zeros16
        upad[S + 16:S + 32, :] = zeros16
        dpad[0:16, :] = zeros16
        dpad[S + 16:S + 32, :] = zeros16

        @pl.when(b == 0)
        def _():
            wacc[...] = jnp.zeros_like(wacc)

        def fill(i, _):
            r0 = pl.multiple_of(i * CH, CH)
            a = a_ref[pl.ds(r0, CH), :]
            g = g_ref[pl.ds(r0, CH), :]
            upad[pl.ds(r0 + 16, CH), :] = a * _sigmoid(g)
            dpad[pl.ds(r0 + 16, CH), :] = d_ref[pl.ds(r0, CH), :]
            return 0
        lax.fori_loop(0, S // CH, fill, 0)

        def step(i, _):
            r0 = pl.multiple_of(i * CH, CH)
            dwin = dpad[pl.ds(r0, CH + 32), :]
            du = _conv_taps(dwin, w_ref, jnp.zeros((CH, LANES), F32), reverse=True)
            a = a_ref[pl.ds(r0, CH), :]
            g = g_ref[pl.ds(r0, CH), :]
            sg = _sigmoid(g)
            da_ref[pl.ds(r0, CH), :] = du * sg
            dg_ref[pl.ds(r0, CH), :] = du * a * sg * (1.0 - sg)
            dc = d_ref[pl.ds(r0, CH), :]
            uwin = upad[pl.ds(r0, CH + 32), :]
            n = CH + 32
            for bb in range(8):
                wb = uwin if bb == 0 else pltpu.roll(uwin, shift=n - bb, axis=0)
                for aa in range(4):
                    o = 8 * aa + bb
                    if o < 1 or o > CONV_WIDTH:
                        continue
                    k = o - 1
                    prod = dc * wb[8 * aa:8 * aa + CH, :]
                    part = prod[0:8, :]
                    for q in range(1, CH // 8):
                        part = part + prod[8 * q:8 * q + 8, :]
                    wacc[8 * k:8 * k + 8, :] += part
            part = dc[0:8, :]
            for q in range(1, CH // 8):
                part = part + dc[8 * q:8 * q + 8, :]
            wacc[8 * CONV_WIDTH:8 * CONV_WIDTH + 8, :] += part
            return 0
        lax.fori_loop(0, S // CH, step, 0)

        @pl.when(b == Bl - 1)
        def _():
            for k in range(CONV_ROWS):
                dw_ref[k:k + 1, :] = jnp.sum(wacc[8 * k:8 * k + 8, :], axis=0, keepdims=True)

    return pl.pallas_call(
        body, grid=(nsc, Bl), name="conv_bwd",
        in_specs=[pl.BlockSpec((S, LANES), lambda j, b: (b, j)),
                  pl.BlockSpec((S, LANES), lambda j, b: (b, nsc + j)),
                  pl.BlockSpec((S, LANES), lambda j, b: (b, j)),
                  pl.BlockSpec((CONV_ROWS, LANES), lambda j, b: (0, j))],
        out_specs=[pl.BlockSpec((S, LANES), lambda j, b: (b, j)),
                   pl.BlockSpec((S, LANES), lambda j, b: (b, j)),
                   pl.BlockSpec((CONV_ROWS, LANES), lambda j, b: (0, j))],
        out_shape=[jax.ShapeDtypeStruct((T, DC), F32), jax.ShapeDtypeStruct((T, DC), F32),
                   jax.ShapeDtypeStruct((CONV_ROWS, DC), F32)],
        scratch_shapes=[pltpu.VMEM((S + 32, LANES), F32), pltpu.VMEM((S + 32, LANES), F32),
                        pltpu.VMEM((8 * CONV_ROWS, LANES), F32)],
        compiler_params=_cp(("arbitrary", "arbitrary")),
    )(ag, ag, dcv, wdw)


ROWCH = 256


LOG2E = 1.4426950408889634
LN2 = 0.6931471805599453
N_EDGE = 4


def _head_mean_matrix():
    r = lax.broadcasted_iota(jnp.int32, (LANES, LANES), 0) // HEAD_DIM
    c = lax.broadcasted_iota(jnp.int32, (LANES, LANES), 1) // HEAD_DIM
    return jnp.where(r == c, 1.0 / HEAD_DIM, 0.0).astype(jnp.bfloat16)


def _head_mean(v, mm):
    hi = v.astype(jnp.bfloat16)
    lo = (v - hi.astype(F32)).astype(jnp.bfloat16)
    return _dot(hi, mm) + _dot(lo, mm)


def _stack_heads(blk, lane_lo):
    z = jnp.zeros_like(blk)
    return jnp.concatenate([jnp.where(lane_lo, blk, z), jnp.where(lane_lo, z, blk)], axis=0)


def _merge_heads(v2, lane_lo):
    return jnp.where(lane_lo, v2[:QBLK], v2[QBLK:])


def _bias_tables(bias_ref, slope_ref):
    row = lax.broadcasted_iota(jnp.int32, (2 * QBLK, 2 * QBLK), 0)
    col = lax.broadcasted_iota(jnp.int32, (2 * QBLK, 2 * QBLK), 1)
    rel = jnp.abs(col - RADIUS - (row % QBLK))
    slope = jnp.where(row < QBLK, slope_ref[0:1, 0:1], slope_ref[0:1, HEAD_DIM:HEAD_DIM + 1]) * LOG2E
    for pi, d in enumerate(DILATIONS):
        inside = jnp.where(rel <= RADIUS, -slope * (float(d) * rel.astype(F32)), NEG_INF)
        for e in range(N_EDGE):
            t = inside
            if e & 1:
                t = jnp.where(col < RADIUS, NEG_INF, t)
            if e & 2:
                t = jnp.where(col >= QBLK + RADIUS, NEG_INF, t)
            bias_ref[N_EDGE * pi + e] = t


def _edge_index(qb, nb):
    return jnp.where(qb == 0, 1, 0) + jnp.where(qb == nb - 1, 2, 0)


def _gather_rows(src_ref, dst_ref, S, d, pad):
    n = S // d
    seg = n + 2 * RADIUS if pad else n
    step = min(n, 512)
    for r in range(d):
        base = r * seg
        if pad:
            dst_ref[base:base + RADIUS, :] = jnp.zeros((RADIUS, LANES), dst_ref.dtype)
            dst_ref[base + RADIUS + n:base + seg, :] = jnp.zeros((RADIUS, LANES), dst_ref.dtype)
            base += RADIUS
        for c0 in range(0, n, step):
            if d == 1:
                v = src_ref[c0:c0 + step, :]
            else:
                v = src_ref[pl.ds(r + c0 * d, step, stride=d), :]
            dst_ref[base + c0:base + c0 + step, :] = v.astype(dst_ref.dtype)


def _scatter_rows(src_ref, dst_ref, S, d, pad, accumulate):
    n = S // d
    seg = n + 2 * RADIUS if pad else n
    step = min(n, 512)
    for r in range(d):
        base = r * seg + (RADIUS if pad else 0)
        for c0 in range(0, n, step):
            v = src_ref[base + c0:base + c0 + step, :]
            if d == 1:
                idx = pl.ds(c0, step)
            else:
                idx = pl.ds(r + c0 * d, step, stride=d)
            if accumulate:
                dst_ref[idx, :] = dst_ref[idx, :] + v
            else:
                dst_ref[idx, :] = v


def _zero_uncovered(acc, S, d):
    n = S // d
    if (n // QBLK) % 2:
        return
    seg = n + 2 * RADIUS
    for r in range(d):
        acc[0, r * seg + n:r * seg + seg, :] = jnp.zeros((2 * RADIUS, LANES), F32)
        acc[1, r * seg:r * seg + 2 * RADIUS, :] = jnp.zeros((2 * RADIUS, LANES), F32)


def _scatter_parity(acc, dst_ref, S, d):
    n = S // d
    seg = n + 2 * RADIUS
    step = min(n, 512)
    one_block = (n // QBLK) % 2 == 1
    for r in range(d):
        base = r * seg + RADIUS
        for c0 in range(0, n, step):
            rows = slice(base + c0, base + c0 + step)
            v = acc[r % 2, rows, :] if one_block else acc[0, rows, :] + acc[1, rows, :]
            idx = pl.ds(c0, step) if d == 1 else pl.ds(r + c0 * d, step, stride=d)
            dst_ref[idx, :] = dst_ref[idx, :] + v


PIPE_UNROLL = 2
PIPE_SLOTS = 8


def _pipeline(n_items, stages, unroll):
    K = len(stages)
    assert n_items % unroll == 0 and K * unroll <= PIPE_SLOTS
    trips = n_items // unroll
    assert trips >= K - 1

    def trip(t, static):
        for s in reversed(range(K)):
            if static and not 0 <= t - s < trips:
                continue
            for u in range(unroll):
                item = unroll * (t - s) + u
                stages[s](jnp.int32(item) if static else item)

    for t in range(K - 1):
        trip(t, True)

    def full(t, carry):
        trip(t, False)
        return carry
    lax.fori_loop(K - 1, trips, full, 0)
    for t in range(trips, trips + K - 1):
        trip(t, True)


def _qk_normalize(q_ref, k_ref, gq_ref, gk_ref, qh, kh, S, mm):
    for c0 in range(0, S, ROWCH):
        q = q_ref[c0:c0 + ROWCH, :]
        k = k_ref[c0:c0 + ROWCH, :]
        qh[c0:c0 + ROWCH, :] = q * lax.rsqrt(_head_mean(q * q, mm) + EPS) * (gq_ref[...] * (HEAD_DIM ** -0.5 * LOG2E))
        kh[c0:c0 + ROWCH, :] = k * lax.rsqrt(_head_mean(k * k, mm) + EPS) * gk_ref[...]


def _attn_fwd(qkv, gq2, gk2, slopes, *, Bl, S, hosted=()):
    n3, T, _ = qkv.shape
    NS = n3 // 3
    NB = S // QBLK
    PADR = S + 2 * RADIUS * DILATIONS[-1]
    nh = len(hosted)
    plan = _WeightGather([b.shape for b in hosted]) if nh else None
    n_steps = Bl * NS

    def body(q_ref, k_ref, v_ref, gq_ref, gk_ref, slope_ref, *rest):
        o_ref, lse_ref = rest[nh:nh + 2]
        wouts = rest[nh + 2:2 * nh + 2]
        (qh, kh, qp, kp, vp, op, lp, onat, lnat, bias_ref, sbuf, pbuf, mbuf, lbuf) = rest[2 * nh + 2:2 * nh + 16]
        sems = rest[2 * nh + 16:]
        step = pl.program_id(0) * NS + pl.program_id(1)
        if nh:
            @pl.when(step == 0)
            def _():
                plan.start(wouts, sems)

            @pl.when(step == n_steps // 2)
            def _():
                plan.forward(wouts, sems)

        mm = _head_mean_matrix()
        lane_lo = lax.broadcasted_iota(jnp.int32, (QBLK, LANES), 1) < HEAD_DIM
        _bias_tables(bias_ref, slope_ref)
        _qk_normalize(q_ref, k_ref, gq_ref, gk_ref, qh, kh, S, mm)

        for pi, d in enumerate(DILATIONS):
            n = S // d
            nb = n // QBLK
            _gather_rows(qh, qp, S, d, pad=False)
            _gather_rows(kh, kp, S, d, pad=True)
            _gather_rows(v_ref, vp, S, d, pad=True)

            def offsets(i, nb=nb):
                r = i // nb
                return pl.multiple_of(i * QBLK, QBLK), pl.multiple_of((i + r) * QBLK, QBLK), i % nb

            def scores(i, pi=pi, nb=nb):
                q0, k0, qb = offsets(i)
                qs = _stack_heads(qp[pl.ds(q0, QBLK), :], lane_lo)
                sbuf[i % PIPE_SLOTS] = (_dot_nt(qs, kp[pl.ds(k0, 2 * QBLK), :])
                                        + bias_ref[N_EDGE * pi + _edge_index(qb, nb)])

            def rowmax(i):
                m = jnp.max(sbuf[i % PIPE_SLOTS], axis=1, keepdims=True)
                mbuf[i % PIPE_SLOTS] = jnp.broadcast_to(m, (2 * QBLK, LANES))

            def expsum(i):
                m = mbuf[i % PIPE_SLOTS]
                p = jnp.exp2(sbuf[i % PIPE_SLOTS] - jnp.concatenate([m, m], axis=1))
                pbuf[i % PIPE_SLOTS] = p.astype(MXU_DTYPE)
                lbuf[i % PIPE_SLOTS] = jnp.broadcast_to(jnp.sum(p, axis=1, keepdims=True), (2 * QBLK, LANES))

            def values(i):
                q0, k0, _ = offsets(i)
                l = lbuf[i % PIPE_SLOTS]
                o2 = _dot(pbuf[i % PIPE_SLOTS], vp[pl.ds(k0, 2 * QBLK), :]) * (1.0 / l)
                op[pl.ds(q0, QBLK), :] = _merge_heads(o2, lane_lo)
                lp[pl.ds(q0, QBLK), :] = _merge_heads(mbuf[i % PIPE_SLOTS] + jnp.log2(l), lane_lo)

            _pipeline(NB, [scores, rowmax, expsum, values], PIPE_UNROLL)
            _scatter_rows(op, onat.at[pi], S, d, pad=False, accumulate=False)
            _scatter_rows(lp, lnat.at[pi], S, d, pad=False, accumulate=False)

        for c0 in range(0, S, ROWCH):
            ls = [lnat[pi, c0:c0 + ROWCH, :] for pi in range(len(DILATIONS))]
            mx = jnp.maximum(jnp.maximum(ls[0], ls[1]), ls[2])
            es = [jnp.exp2(l - mx) for l in ls]
            tot = es[0] + es[1] + es[2]
            inv = 1.0 / tot
            acc = (es[0] * inv) * onat[0, c0:c0 + ROWCH, :]
            for pi in (1, 2):
                acc = acc + (es[pi] * inv) * onat[pi, c0:c0 + ROWCH, :]
            o_ref[c0:c0 + ROWCH, :] = acc
            lse_ref[c0:c0 + ROWCH, :] = mx + jnp.log2(tot)

        if nh:
            @pl.when(step == n_steps - 1)
            def _():
                plan.finish(wouts, sems)

    spec_in = lambda off: pl.BlockSpec((None, S, LANES), lambda b, j: (off * NS + j, b, 0))
    vec = pl.BlockSpec((1, LANES), lambda b, j: (0, 0))
    out = pl.BlockSpec((S, LANES), lambda b, j: (b, j))
    anyspec = pl.BlockSpec(memory_space=pl.ANY)
    return pl.pallas_call(
        body, grid=(Bl, NS), name="attn_fwd",
        in_specs=[spec_in(0), spec_in(1), spec_in(2), vec, vec,
                  pl.BlockSpec((None, 8, LANES), lambda b, j: (j, 0, 0))] + [anyspec] * nh,
        out_specs=[out, out] + [anyspec] * nh,
        out_shape=[jax.ShapeDtypeStruct((T, NS * LANES), F32)] * 2
                  + [jax.ShapeDtypeStruct(b.shape, b.dtype) for b in hosted],
        input_output_aliases={6 + w: 2 + w for w in range(nh)},
        scratch_shapes=[pltpu.VMEM((S, LANES), F32), pltpu.VMEM((S, LANES), F32),
                        pltpu.VMEM((S, LANES), MXU_DTYPE), pltpu.VMEM((PADR, LANES), MXU_DTYPE),
                        pltpu.VMEM((PADR, LANES), MXU_DTYPE),
                        pltpu.VMEM((S, LANES), F32), pltpu.VMEM((S, LANES), F32),
                        pltpu.VMEM((3, S, LANES), F32), pltpu.VMEM((3, S, LANES), F32),
                        pltpu.VMEM((N_EDGE * len(DILATIONS), 2 * QBLK, 2 * QBLK), F32),
                        pltpu.VMEM((PIPE_SLOTS, 2 * QBLK, 2 * QBLK), F32),
                        pltpu.VMEM((PIPE_SLOTS, 2 * QBLK, 2 * QBLK), MXU_DTYPE),
                        pltpu.VMEM((PIPE_SLOTS, 2 * QBLK, LANES), F32), pltpu.VMEM((PIPE_SLOTS, 2 * QBLK, LANES), F32)]
                       + (plan.scratch() if nh else []),
        compiler_params=_cp(("arbitrary", "arbitrary")),
    )(qkv, qkv, qkv, gq2, gk2, slopes, *hosted)


def _attn_bwd(qkv, o, lse, do, gq2, gk2, slopes, *, Bl, S, hosted=()):
    n3, T, _ = qkv.shape
    NS = n3 // 3
    NB = S // QBLK
    PADR = S + 2 * RADIUS * DILATIONS[-1]
    QSCALE = HEAD_DIM ** -0.5
    nh = len(hosted)
    plan = _ChipExchange(nh)
    n_steps = Bl * NS

    def body(q_ref, k_ref, v_ref, o_ref, lse_ref, do_ref, gq_ref, gk_ref, slope_ref, *rest):
        hin = rest[:nh]
        dq_ref, dk_ref, dv_ref, gacc_ref = rest[nh:nh + 4]
        hout = rest[nh + 4:2 * nh + 4]
        (qh, kh, dl, qp, kp, vp, dop, lp, dlp, dqp, dkacc, dvacc, dqn, dkn, bias_ref,
         sbuf, dpbuf, pbuf, dsbuf) = rest[2 * nh + 4:2 * nh + 23]
        sems = rest[2 * nh + 23:]
        step = pl.program_id(0) * NS + pl.program_id(1)

        @pl.when(step == 0)
        def _():
            gacc_ref[...] = jnp.zeros_like(gacc_ref)
            if nh:
                plan.start(hin, hout, sems)

        mm = _head_mean_matrix()
        lane_lo = lax.broadcasted_iota(jnp.int32, (QBLK, LANES), 1) < HEAD_DIM
        _bias_tables(bias_ref, slope_ref)
        _qk_normalize(q_ref, k_ref, gq_ref, gk_ref, qh, kh, S, mm)
        for c0 in range(0, S, ROWCH):
            dl[c0:c0 + ROWCH, :] = _head_mean(do_ref[c0:c0 + ROWCH, :] * o_ref[c0:c0 + ROWCH, :], mm) * HEAD_DIM
            dqn[c0:c0 + ROWCH, :] = jnp.zeros((ROWCH, LANES), F32)
            dkn[c0:c0 + ROWCH, :] = jnp.zeros((ROWCH, LANES), F32)
            dv_ref[c0:c0 + ROWCH, :] = jnp.zeros((ROWCH, LANES), F32)

        for pi, d in enumerate(DILATIONS):
            n = S // d
            nb = n // QBLK
            _gather_rows(qh, qp, S, d, pad=False)
            _gather_rows(kh, kp, S, d, pad=True)
            _gather_rows(v_ref, vp, S, d, pad=True)
            _gather_rows(do_ref, dop, S, d, pad=False)
            _gather_rows(lse_ref, lp, S, d, pad=False)
            _gather_rows(dl, dlp, S, d, pad=False)
            _zero_uncovered(dkacc, S, d)
            _zero_uncovered(dvacc, S, d)

            def offsets(i, nb=nb):
                r = i // nb
                return pl.multiple_of(i * QBLK, QBLK), pl.multiple_of((i + r) * QBLK, QBLK), i % nb

            def scores(i, pi=pi, nb=nb):
                q0, k0, qb = offsets(i)
                qs = _stack_heads(qp[pl.ds(q0, QBLK), :], lane_lo)
                dos = _stack_heads(dop[pl.ds(q0, QBLK), :], lane_lo)
                sbuf[i % PIPE_SLOTS] = (_dot_nt(qs, kp[pl.ds(k0, 2 * QBLK), :])
                                        + bias_ref[N_EDGE * pi + _edge_index(qb, nb)])
                dpbuf[i % PIPE_SLOTS] = _dot_nt(dos, vp[pl.ds(k0, 2 * QBLK), :])

            def probs(i):
                q0, _, _ = offsets(i)
                lblk = lp[pl.ds(q0, QBLK), :]
                dblk = dlp[pl.ds(q0, QBLK), :]
                lcol = jnp.concatenate([lblk[:, 0:1], lblk[:, HEAD_DIM:HEAD_DIM + 1]], axis=0)
                dcol = jnp.concatenate([dblk[:, 0:1], dblk[:, HEAD_DIM:HEAD_DIM + 1]], axis=0)
                p = jnp.exp2(sbuf[i % PIPE_SLOTS] - lcol)
                pbuf[i % PIPE_SLOTS] = p.astype(MXU_DTYPE)
                dsbuf[i % PIPE_SLOTS] = (p * (dpbuf[i % PIPE_SLOTS] - dcol)).astype(MXU_DTYPE)

            def grads(i):
                q0, k0, _ = offsets(i)
                qs = _stack_heads(qp[pl.ds(q0, QBLK), :], lane_lo)
                dos = _stack_heads(dop[pl.ds(q0, QBLK), :], lane_lo)
                ds = dsbuf[i % PIPE_SLOTS]
                dvacc[i % 2, pl.ds(k0, 2 * QBLK), :] = _dot_tn(pbuf[i % PIPE_SLOTS], dos)
                dkacc[i % 2, pl.ds(k0, 2 * QBLK), :] = _dot_tn(ds, qs)
                dqp[pl.ds(q0, QBLK), :] = _merge_heads(_dot(ds, kp[pl.ds(k0, 2 * QBLK), :]), lane_lo)

            _pipeline(NB, [scores, probs, grads], PIPE_UNROLL)
            _scatter_rows(dqp, dqn, S, d, pad=False, accumulate=True)
            _scatter_parity(dkacc, dkn, S, d)
            _scatter_parity(dvacc, dv_ref, S, d)

        gq_sum = jnp.zeros((8, LANES), F32)
        gk_sum = jnp.zeros((8, LANES), F32)
        for c0 in range(0, S, ROWCH):
            for src_ref, dn, g_ref, dst_ref, scale, is_q in ((q_ref, dqn, gq_ref, dq_ref, QSCALE, True),
                                                             (k_ref, dkn, gk_ref, dk_ref, LN2, False)):
                x = src_ref[c0:c0 + ROWCH, :]
                dh = dn[c0:c0 + ROWCH, :]
                rr = lax.rsqrt(_head_mean(x * x, mm) + EPS)
                e = dh * (g_ref[...] * scale)
                dst_ref[c0:c0 + ROWCH, :] = rr * e - x * (rr * rr * rr) * _head_mean(e * x, mm)
                gpart = dh * (x * rr * scale)
                acc8 = gpart[0:8, :]
                for q8 in range(1, ROWCH // 8):
                    acc8 = acc8 + gpart[8 * q8:8 * q8 + 8, :]
                if is_q:
                    gq_sum = gq_sum + acc8
                else:
                    gk_sum = gk_sum + acc8
        gacc_ref[0:1, :] += jnp.sum(gq_sum, axis=0, keepdims=True)
        gacc_ref[1:2, :] += jnp.sum(gk_sum, axis=0, keepdims=True)

        if nh:
            @pl.when(step == n_steps - 1)
            def _():
                plan.finish(hin, hout, sems)

    spec_in = lambda off: pl.BlockSpec((None, S, LANES), lambda b, j: (off * NS + j, b, 0))
    tok = pl.BlockSpec((S, LANES), lambda b, j: (b, j))
    vec = pl.BlockSpec((1, LANES), lambda b, j: (0, 0))
    slab_out = pl.BlockSpec((None, S, LANES), lambda b, j: (j, b, 0))
    f32buf = lambda rows: pltpu.VMEM((rows, LANES), F32)
    bfbuf = lambda rows: pltpu.VMEM((rows, LANES), MXU_DTYPE)
    anyspec = pl.BlockSpec(memory_space=pl.ANY)
    return pl.pallas_call(
        body, grid=(Bl, NS), name="attn_bwd",
        in_specs=[spec_in(0), spec_in(1), spec_in(2), tok, tok, tok, vec, vec,
                  pl.BlockSpec((None, 8, LANES), lambda b, j: (j, 0, 0))] + [anyspec] * nh,
        out_specs=[slab_out, slab_out, slab_out, pl.BlockSpec((8, LANES), lambda b, j: (0, 0))] + [anyspec] * nh,
        out_shape=[jax.ShapeDtypeStruct((NS, T, LANES), F32)] * 3 + [jax.ShapeDtypeStruct((8, LANES), F32)]
                  + [jax.ShapeDtypeStruct((3,) + h.shape[1:], h.dtype) for h in hosted],
        scratch_shapes=[f32buf(S), f32buf(S), f32buf(S),
                        bfbuf(S), bfbuf(PADR), bfbuf(PADR), bfbuf(S),
                        f32buf(S), f32buf(S), f32buf(S),
                        pltpu.VMEM((2, PADR, LANES), F32), pltpu.VMEM((2, PADR, LANES), F32),
                        f32buf(S), f32buf(S),
                        pltpu.VMEM((N_EDGE * len(DILATIONS), 2 * QBLK, 2 * QBLK), F32),
                        pltpu.VMEM((PIPE_SLOTS, 2 * QBLK, 2 * QBLK), F32),
                        pltpu.VMEM((PIPE_SLOTS, 2 * QBLK, 2 * QBLK), F32),
                        pltpu.VMEM((PIPE_SLOTS, 2 * QBLK, 2 * QBLK), MXU_DTYPE),
                        pltpu.VMEM((PIPE_SLOTS, 2 * QBLK, 2 * QBLK), MXU_DTYPE)]
                       + (plan.scratch() if nh else []),
        compiler_params=_cp(("arbitrary", "arbitrary")),
    )(qkv, qkv, qkv, o, lse, do, gq2, gk2, slopes, *hosted)


def _layer_norm_parts(cv, g_ln, b_ln):
    mu = jnp.mean(cv, axis=-1, keepdims=True)
    cen = cv - mu
    rs = lax.rsqrt(jnp.mean(cen * cen, axis=-1, keepdims=True) + EPS)
    z = cen * rs
    return z, rs, z * g_ln + b_ln


def _ffn_fwd(x2, cv, ya, tgt, mod, g_ln, b_ln, g_ffn, w_out, w_gate, w_up, w_down, *, S, tm):
    T, D = x2.shape
    DC = cv.shape[1]
    P, Kb, _ = w_out.shape
    Fb = w_gate.shape[2]
    tps = S // tm

    def body(x_ref, cv_ref, ya_ref, t_ref, mod_ref, gln_ref, bln_ref, gf_ref, wo_hbm, wg_hbm, wu_hbm, wd_hbm,
             x1_ref, ycat_ref, mix_ref, h2_ref, g_ref, u_ref, a_ref, f_ref, dy_ref, loss_ref,
             wo, wg, wu, wd, sems):
        i = pl.program_id(0)
        _load_resident(i, [(wo_hbm, wo), (wg_hbm, wg), (wu_hbm, wu), (wd_hbm, wd)], sems)

        @pl.when(i == 0)
        def _():
            loss_ref[...] = jnp.zeros_like(loss_ref)

        _, _, ln = _layer_norm_parts(cv_ref[...], gln_ref[...], bln_ref[...])
        yc = ln * _sigmoid(ln)
        ycat = jnp.concatenate([yc, ya_ref[...]], axis=1).astype(MXU_DTYPE)
        ycat_ref[...] = ycat
        mix = _dot(ycat[:, 0:Kb], wo[0])
        for p in range(1, P):
            mix = mix + _dot(ycat[:, Kb * p:Kb * (p + 1)], wo[p])
        mix_ref[...] = mix.astype(ACT_DTYPE)
        x1 = x_ref[...] + mod_ref[:, 2 * D:3 * D] * mix
        x1_ref[...] = x1
        r2 = lax.rsqrt(jnp.mean(x1 * x1, axis=-1, keepdims=True) + EPS)
        h2 = (x1 * r2 * gf_ref[...]) * (1.0 + mod_ref[:, 4 * D:5 * D]) + mod_ref[:, 3 * D:4 * D]
        h2b = h2.astype(MXU_DTYPE)
        h2_ref[...] = h2b
        f = jnp.zeros((tm, D), F32)
        for p in range(P):
            g = _dot(h2b, wg[p])
            u = _dot(h2b, wu[p])
            a = (g * _sigmoid(g) * u).astype(MXU_DTYPE)
            g_ref[p] = g.astype(ACT_DTYPE)
            u_ref[p] = u.astype(ACT_DTYPE)
            a_ref[p] = a
            f = f + _dot(a, wd[p])
        f_ref[...] = f.astype(ACT_DTYPE)
        err = x1 + mod_ref[:, 5 * D:6 * D] * f - t_ref[...]
        dy_ref[...] = err * (1.0 / D)
        tot = jnp.sum(_colsum(err * err), axis=1, keepdims=True)
        loss_ref[...] += tot * (0.5 / D)

    row = lambda w: pl.BlockSpec((tm, w), lambda i: (i, 0))
    vec = lambda w: pl.BlockSpec((1, w), lambda i: (0, 0))
    blk = pl.BlockSpec((P, tm, Fb), lambda i: (0, i, 0))
    anyspec = pl.BlockSpec(memory_space=pl.ANY)
    return pl.pallas_call(
        body, grid=(T // tm,), name="ffn_fwd",
        in_specs=[row(D), row(DC), row(D - DC), row(D),
                  pl.BlockSpec((None, 1, N_MOD * D), lambda i: (i // tps, 0, 0)),
                  vec(DC), vec(DC), vec(D), anyspec, anyspec, anyspec, anyspec],
        out_specs=[row(D), row(D), row(D), row(D), blk, blk, blk, row(D), row(D),
                   pl.BlockSpec((8, LANES), lambda i: (0, 0))],
        out_shape=[jax.ShapeDtypeStruct((T, D), F32), jax.ShapeDtypeStruct((T, D), MXU_DTYPE),
                   jax.ShapeDtypeStruct((T, D), ACT_DTYPE), jax.ShapeDtypeStruct((T, D), MXU_DTYPE),
                   jax.ShapeDtypeStruct((P, T, Fb), ACT_DTYPE), jax.ShapeDtypeStruct((P, T, Fb), ACT_DTYPE),
                   jax.ShapeDtypeStruct((P, T, Fb), MXU_DTYPE), jax.ShapeDtypeStruct((T, D), ACT_DTYPE),
                   jax.ShapeDtypeStruct((T, D), F32), jax.ShapeDtypeStruct((8, LANES), F32)],
        scratch_shapes=[pltpu.VMEM(w_out.shape, w_out.dtype), pltpu.VMEM(w_gate.shape, w_gate.dtype),
                        pltpu.VMEM(w_up.shape, w_up.dtype), pltpu.VMEM(w_down.shape, w_down.dtype),
                        pltpu.SemaphoreType.DMA((4,))],
        compiler_params=_cp(("arbitrary",)),
    )(x2, cv, ya, tgt, mod, g_ln, b_ln, g_ffn, w_out, w_gate, w_up, w_down)


def _ffn_bwd(dy, x1, gs, us, fo, mixb, cv, mod, g_ln, b_ln, g_ffn, w_out, w_gate, w_up, w_down, *, S, tm):
    T, D = dy.shape
    DC = cv.shape[1]
    P, Kb, _ = w_out.shape
    Fb = w_gate.shape[2]
    tps = S // tm
    Bl = T // S

    def body(dy_ref, x1_ref, g_ref, u_ref, f_ref, mix_ref, cv_ref, mod_ref, gln_ref, bln_ref, gf_ref,
             wo_hbm, wg_hbm, wu_hbm, wd_hbm,
             dg_ref, du_ref, df_ref, dx1_ref, dmix_ref, dya_ref, dcv_ref, macc_ref, gacc_ref, lacc_ref,
             wo, wg, wu, wd, sems):
        i = pl.program_id(0)
        _load_resident(i, [(wo_hbm, wo), (wg_hbm, wg), (wu_hbm, wu), (wd_hbm, wd)], sems)

        @pl.when(i == 0)
        def _():
            gacc_ref[...] = jnp.zeros_like(gacc_ref)
            lacc_ref[...] = jnp.zeros_like(lacc_ref)

        @pl.when(i % tps == 0)
        def _():
            macc_ref[...] = jnp.zeros_like(macc_ref)

        dy_t = dy_ref[...]
        x1 = x1_ref[...]
        gate_f = mod_ref[:, 5 * D:6 * D]
        macc_ref[2:3, :] += _colsum(dy_t * f_ref[...].astype(F32))
        dfb = (dy_t * gate_f).astype(MXU_DTYPE)
        df_ref[...] = dfb
        dh2 = jnp.zeros((tm, D), F32)
        for p in range(P):
            da = _dot_nt(dfb, wd[p])
            g = g_ref[p].astype(F32)
            u = u_ref[p].astype(F32)
            sg = _sigmoid(g)
            dgp = (da * u * (sg * (1.0 + g * (1.0 - sg)))).astype(MXU_DTYPE)
            dup = (da * (g * sg)).astype(MXU_DTYPE)
            dg_ref[p] = dgp
            du_ref[p] = dup
            dh2 = dh2 + _dot_nt(dgp, wg[p]) + _dot_nt(dup, wu[p])
        r2 = lax.rsqrt(jnp.mean(x1 * x1, axis=-1, keepdims=True) + EPS)
        xr = x1 * r2
        n2 = xr * gf_ref[...]
        macc_ref[0:1, :] += _colsum(dh2)
        macc_ref[1:2, :] += _colsum(dh2 * n2)
        dn2 = dh2 * (1.0 + mod_ref[:, 4 * D:5 * D])
        gacc_ref[0:1, :] += _colsum(dn2 * xr)
        e = dn2 * gf_ref[...]
        dx1 = dy_t + r2 * e - xr * (r2 * jnp.mean(e * xr, axis=-1, keepdims=True))
        dx1_ref[...] = dx1
        macc_ref[3:4, :] += _colsum(dx1 * mix_ref[...].astype(F32))
        dmixb = (dx1 * mod_ref[:, 2 * D:3 * D]).astype(MXU_DTYPE)
        dmix_ref[...] = dmixb
        parts = [_dot_nt(dmixb, wo[p]) for p in range(P)]
        dycat = jnp.concatenate(parts, axis=1) if P > 1 else parts[0]
        dya_ref[...] = dycat[:, DC:]
        dyc = dycat[:, :DC]
        z, rs, ln = _layer_norm_parts(cv_ref[...], gln_ref[...], bln_ref[...])
        sg = _sigmoid(ln)
        dln = dyc * (sg * (1.0 + ln * (1.0 - sg)))
        lacc_ref[0:1, :] += _colsum(dln * z)
        lacc_ref[1:2, :] += _colsum(dln)
        dz = dln * gln_ref[...]
        dcv_ref[...] = rs * (dz - jnp.mean(dz, axis=-1, keepdims=True) - z * jnp.mean(dz * z, axis=-1, keepdims=True))

    row = lambda w: pl.BlockSpec((tm, w), lambda i: (i, 0))
    vec = lambda w: pl.BlockSpec((1, w), lambda i: (0, 0))
    blk = pl.BlockSpec((P, tm, Fb), lambda i: (0, i, 0))
    anyspec = pl.BlockSpec(memory_space=pl.ANY)
    return pl.pallas_call(
        body, grid=(T // tm,), name="ffn_bwd",
        in_specs=[row(D), row(D), blk, blk, row(D), row(D), row(DC),
                  pl.BlockSpec((None, 1, N_MOD * D), lambda i: (i // tps, 0, 0)),
                  vec(DC), vec(DC), vec(D), anyspec, anyspec, anyspec, anyspec],
        out_specs=[blk, blk, row(D), row(D), row(D), row(D - DC), row(DC),
                   pl.BlockSpec((None, 8, D), lambda i: (i // tps, 0, 0)),
                   pl.BlockSpec((8, D), lambda i: (0, 0)), pl.BlockSpec((8, DC), lambda i: (0, 0))],
        out_shape=[jax.ShapeDtypeStruct((P, T, Fb), MXU_DTYPE), jax.ShapeDtypeStruct((P, T, Fb), MXU_DTYPE),
                   jax.ShapeDtypeStruct((T, D), MXU_DTYPE), jax.ShapeDtypeStruct((T, D), F32),
                   jax.ShapeDtypeStruct((T, D), MXU_DTYPE), jax.ShapeDtypeStruct((T, D - DC), F32),
                   jax.ShapeDtypeStruct((T, DC), F32), jax.ShapeDtypeStruct((Bl, 8, D), F32),
                   jax.ShapeDtypeStruct((8, D), F32), jax.ShapeDtypeStruct((8, DC), F32)],
        scratch_shapes=[pltpu.VMEM(w_out.shape, w_out.dtype), pltpu.VMEM(w_gate.shape, w_gate.dtype),
                        pltpu.VMEM(w_up.shape, w_up.dtype), pltpu.VMEM(w_down.shape, w_down.dtype),
                        pltpu.SemaphoreType.DMA((4,))],
        compiler_params=_cp(("arbitrary",)),
    )(dy, x1, gs, us, fo, mixb, cv, mod, g_ln, b_ln, g_ffn, w_out, w_gate, w_up, w_down)


def _in_bwd(da, dg, dq, dk, dv, x2, dx1, mod, g_mix, w_in, *, S, tm):
    T, D = x2.shape
    P, _, Nb = w_in.shape
    DC = da.shape[1]
    NS = dq.shape[0]
    n_in = P * Nb
    tps = S // tm
    Bl = T // S

    def body(da_ref, dg_ref, dq_ref, dk_ref, dv_ref, x_ref, dx1_ref, mod_ref, g_ref, w_ref,
             dx_ref, dproj_ref, macc_ref, gacc_ref):
        i = pl.program_id(0)

        @pl.when(i == 0)
        def _():
            gacc_ref[...] = jnp.zeros_like(gacc_ref)

        @pl.when(i % tps == 0)
        def _():
            macc_ref[...] = jnp.zeros_like(macc_ref)

        pieces = [da_ref[...], dg_ref[...]] + [r[j] for r in (dq_ref, dk_ref, dv_ref) for j in range(NS)]
        dproj = jnp.concatenate(pieces, axis=1).astype(MXU_DTYPE)
        dproj_ref[...] = dproj
        dh = _dot_nt(dproj[:, 0:Nb], w_ref[0])
        for p in range(1, P):
            dh = dh + _dot_nt(dproj[:, Nb * p:Nb * (p + 1)], w_ref[p])
        x = x_ref[...]
        r = lax.rsqrt(jnp.mean(x * x, axis=-1, keepdims=True) + EPS)
        xr = x * r
        macc_ref[0:1, :] += _colsum(dh)
        macc_ref[1:2, :] += _colsum(dh * (xr * g_ref[...]))
        dn = dh * (1.0 + mod_ref[:, D:2 * D])
        gacc_ref[0:1, :] += _colsum(dn * xr)
        e = dn * g_ref[...]
        dx_ref[...] = dx1_ref[...] + r * e - xr * (r * jnp.mean(e * xr, axis=-1, keepdims=True))

    row = lambda w: pl.BlockSpec((tm, w), lambda i: (i, 0))
    slab = pl.BlockSpec((NS, tm, LANES), lambda i: (0, i, 0))
    return pl.pallas_call(
        body, grid=(T // tm,), name="in_bwd",
        in_specs=[row(DC), row(DC), slab, slab, slab, row(D), row(D),
                  pl.BlockSpec((None, 1, N_MOD * D), lambda i: (i // tps, 0, 0)),
                  pl.BlockSpec((1, D), lambda i: (0, 0)),
                  pl.BlockSpec((P, D, Nb), lambda i: (0, 0, 0))],
        out_specs=[row(D), row(n_in), pl.BlockSpec((None, 8, D), lambda i: (i // tps, 0, 0)),
                   pl.BlockSpec((8, D), lambda i: (0, 0))],
        out_shape=[jax.ShapeDtypeStruct((T, D), F32), jax.ShapeDtypeStruct((T, n_in), MXU_DTYPE),
                   jax.ShapeDtypeStruct((Bl, 8, D), F32), jax.ShapeDtypeStruct((8, D), F32)],
        compiler_params=_cp(("arbitrary",)),
    )(da, dg, dq, dk, dv, x2, dx1, mod, g_mix, w_in)


def _wgrad(a, b, *, P, name, tk, split=None):
    a_blk, b_blk = a.ndim == 3, b.ndim == 3
    T = a.shape[-2]
    if a_blk:
        R, C = a.shape[2], b.shape[1]
        a_of = lambda av, p: av[p]
        b_of = lambda bv, p: bv[...]
    elif b_blk:
        R, C = a.shape[1], b.shape[2]
        a_of = lambda av, p: av[...]
        b_of = lambda bv, p: bv[p]
    elif split == "a":
        R, C = a.shape[1] // P, b.shape[1]
        a_of = lambda av, p: av[:, R * p:R * (p + 1)]
        b_of = lambda bv, p: bv[...]
    else:
        R, C = a.shape[1], b.shape[1] // P
        a_of = lambda av, p: av[...]
        b_of = lambda bv, p: bv[:, C * p:C * (p + 1)]

    def body(a_ref, b_ref, o_ref):
        @pl.when(pl.program_id(0) == 0)
        def _():
            o_ref[...] = jnp.zeros_like(o_ref)
        for p in range(P):
            o_ref[p] += _dot_tn(a_of(a_ref, p), b_of(b_ref, p))

    def spec(v):
        if v.ndim == 3:
            return pl.BlockSpec((P, tk, v.shape[2]), lambda k: (0, k, 0))
        return pl.BlockSpec((tk, v.shape[1]), lambda k: (k, 0))

    return pl.pallas_call(
        body, grid=(T // tk,), name=name,
        in_specs=[spec(a), spec(b)],
        out_specs=pl.BlockSpec((P, R, C), lambda k: (0, 0, 0)),
        out_shape=jax.ShapeDtypeStruct((P, R, C), F32),
        compiler_params=_cp(("arbitrary",)),
    )(a, b)


TM_IN = 512
TM_FFN = 256
TK_WGRAD = 512


def _alibi_slabs(n_slab):
    heads = 2 * n_slab
    slopes = 2.0 ** (-8.0 * np.arange(1, heads + 1) / heads)
    return jnp.asarray(np.broadcast_to(np.repeat(slopes.reshape(n_slab, 1, 2), HEAD_DIM, axis=2), (n_slab, 8, LANES)),
                       dtype=F32)


def _local_step(x, tgt, mod, g_mix, wdw, g_ln, b_ln, g_q, g_k, g_ffn, w_in, w_out, w_gate, w_up, w_down,
                gather_in_attention=False, ffn_partials=None):
    Bl, S, D = x.shape
    T = Bl * S
    DC = g_ln.shape[1]
    P = w_in.shape[0]
    n_slab = (D - DC) // LANES
    x2 = x.reshape(T, D)
    t2 = tgt.reshape(T, D)
    mod3 = mod.reshape(Bl, 1, N_MOD * D)
    gq2 = jnp.tile(g_q, (1, LANES // HEAD_DIM))
    gk2 = jnp.tile(g_k, (1, LANES // HEAD_DIM))
    slopes = _alibi_slabs(n_slab)

    ag, qkv, h1 = _fwd_in(x2, mod3, g_mix, w_in, S=S, tm=TM_IN, n_ag=2 * DC)
    cv = _conv_fwd(ag, wdw, Bl=Bl, S=S, DC=DC)
    if gather_in_attention:
        ya, lse, w_out, w_gate, w_up, w_down = _attn_fwd(qkv, gq2, gk2, slopes, Bl=Bl, S=S,
                                                         hosted=(w_out, w_gate, w_up, w_down))
    else:
        ya, lse = _attn_fwd(qkv, gq2, gk2, slopes, Bl=Bl, S=S)
    x1, ycat, mixb, h2, gs, us, acts, fo, dy, lossb = _ffn_fwd(
        x2, cv, ya, t2, mod3, g_ln, b_ln, g_ffn, w_out, w_gate, w_up, w_down, S=S, tm=TM_FFN)
    dgs, dus, dfb, dx1, dmixb, dya, dcv, macc_f, gacc_f, lacc = _ffn_bwd(
        dy, x1, gs, us, fo, mixb, cv, mod3, g_ln, b_ln, g_ffn, w_out, w_gate, w_up, w_down, S=S, tm=TM_FFN)
    grads = dict(
        w_down=_wgrad(acts, dfb, P=P, name="wgrad_down", tk=TK_WGRAD),
        w_gate=_wgrad(h2, dgs, P=P, name="wgrad_gate", tk=TK_WGRAD),
        w_up=_wgrad(h2, dus, P=P, name="wgrad_up", tk=TK_WGRAD),
    )
    out = {}
    if ffn_partials is not None:
        sums = ffn_partials([grads[nm] for nm in FFN_WEIGHTS])
        res = _attn_bwd(qkv, ya, lse, dya, gq2, gk2, slopes, Bl=Bl, S=S, hosted=tuple(sb for _, sb in sums))
        dq, dk, dv, gqk = res[:4]
        out["ffn_sums"] = [s32 for s32, _ in sums]
        out["ffn_recv"] = list(res[4:])
    else:
        dq, dk, dv, gqk = _attn_bwd(qkv, ya, lse, dya, gq2, gk2, slopes, Bl=Bl, S=S)
    da, dg, dwdw = _conv_bwd(ag, dcv, wdw, Bl=Bl, S=S, DC=DC)
    dx, dprojb, macc_m, gacc_m = _in_bwd(da, dg, dq, dk, dv, x2, dx1, mod3, g_mix, w_in, S=S, tm=TM_IN)
    grads["w_in"] = _wgrad(h1, dprojb, P=P, name="wgrad_in", tk=TK_WGRAD, split="b")
    grads["w_out"] = _wgrad(ycat, dmixb, P=P, name="wgrad_out", tk=TK_WGRAD, split="a")
    packed = _pack_small(macc_m, macc_f, gacc_m, gacc_f, lacc, gqk, dwdw)
    out.update(loss=lossb[0, 0], dx=dx.reshape(Bl, S, D), grads=grads, packed=packed)
    return out


FFN_WEIGHTS = ("w_down", "w_gate", "w_up")


def _small_layout(Bl):
    return 8 * Bl, 8 * Bl + 8, 8 * Bl + 8 + CONV_ROWS


def _pack_small(macc_m, macc_f, gacc_m, gacc_f, lacc, gqk, dwdw):
    Bl, _, D = macc_m.shape
    DC = lacc.shape[1]
    assert 2 * DC <= D
    SMALL_GAIN_ROW, SMALL_TAP_ROW, SMALL_ROWS = _small_layout(Bl)

    def body(mm_ref, mf_ref, gm_ref, gf_ref, la_ref, qk_ref, dw_ref, o_ref):
        o_ref[...] = jnp.zeros_like(o_ref)
        for b in range(Bl):
            o_ref[8 * b + 0:8 * b + 2, :] = mm_ref[b, 0:2, :]
            o_ref[8 * b + 2:8 * b + 3, :] = mf_ref[b, 3:4, :]
            o_ref[8 * b + 3:8 * b + 6, :] = mf_ref[b, 0:3, :]
        r = SMALL_GAIN_ROW
        o_ref[r:r + 1, :] = gm_ref[0:1, :]
        o_ref[r + 1:r + 2, :] = gf_ref[0:1, :]
        o_ref[r + 2:r + 3, 0:DC] = la_ref[0:1, :]
        o_ref[r + 2:r + 3, DC:2 * DC] = la_ref[1:2, :]
        qk = qk_ref[0:2, 0:HEAD_DIM] + qk_ref[0:2, HEAD_DIM:2 * HEAD_DIM]
        o_ref[r + 3:r + 4, 0:HEAD_DIM] = qk[0:1, :]
        o_ref[r + 3:r + 4, HEAD_DIM:2 * HEAD_DIM] = qk[1:2, :]
        o_ref[SMALL_TAP_ROW:SMALL_TAP_ROW + CONV_ROWS, 0:DC] = dw_ref[...]

    return pl.pallas_call(body, name="pack_small", out_shape=jax.ShapeDtypeStruct((SMALL_ROWS, D), F32),
                          compiler_params=_cp())(macc_m, macc_f, gacc_m, gacc_f, lacc, gqk, dwdw)


def _row_tile(rows, cap=512):
    if rows <= cap:
        return rows
    best = rows
    for t in range(8, cap + 1, 8):
        if rows % t == 0:
            best = t
    return best


def _cast_weight(w, pidx, name):
    def body(p_ref, w_ref, o_ref):
        o_ref[...] = w_ref[...].astype(MXU_DTYPE)
    R, C = w.shape
    tr = _row_tile(R)
    return pl.pallas_call(
        body, name=name,
        grid_spec=pltpu.PrefetchScalarGridSpec(
            num_scalar_prefetch=1, grid=(R // tr,),
            in_specs=[pl.BlockSpec((tr, C), lambda i, p: (i, 0))],
            out_specs=pl.BlockSpec((None, tr, C), lambda i, p: (p[0], i, 0))),
        out_shape=jax.ShapeDtypeStruct((4, R, C), MXU_DTYPE),
    )(pidx, w)


def _pair_add(g, recv, cidx, name):
    P, R, C = g.shape
    R2 = R // 2

    def body(c_ref, g_ref, r_ref, o_ref, ob_ref):
        s = g_ref[...] + r_ref[...]
        o_ref[...] = s
        ob_ref[...] = s.astype(jnp.bfloat16)

    return pl.pallas_call(
        body, name=name,
        grid_spec=pltpu.PrefetchScalarGridSpec(
            num_scalar_prefetch=1, grid=(P,),
            in_specs=[pl.BlockSpec((None, R2, C), lambda p, c: (p, c[0], 0)),
                      pl.BlockSpec((None, R2, C), lambda p, c: (p, 0, 0))],
            out_specs=[pl.BlockSpec((None, R2, C), lambda p, c: (p, 0, 0)),
                       pl.BlockSpec((None, R2, C), lambda p, c: (p, 0, 0))]),
        out_shape=[jax.ShapeDtypeStruct((P, R2, C), F32), jax.ShapeDtypeStruct((P, R2, C), jnp.bfloat16)],
    )(cidx, g, recv)


def _final_add(chipsum, recv, pc_idx, name):
    P, R2, C = chipsum.shape

    def body(pc_ref, s_ref, r_ref, o_ref):
        acc = s_ref[...]
        for k in range(3):
            acc = acc + r_ref[k].astype(F32)
        o_ref[...] = acc

    return pl.pallas_call(
        body, name=name,
        grid_spec=pltpu.PrefetchScalarGridSpec(
            num_scalar_prefetch=1, grid=(1,),
            in_specs=[pl.BlockSpec((None, R2, C), lambda i, pc: (pc[0], 0, 0)),
                      pl.BlockSpec((3, R2, C), lambda i, pc: (0, 0, 0))],
            out_specs=pl.BlockSpec((R2, C), lambda i, pc: (pc[1], 0))),
        out_shape=jax.ShapeDtypeStruct((2 * R2, C), F32),
    )(pc_idx, chipsum, recv)


def _adamw(w, g, m, v, name):
    R, C = w.shape
    tr = _row_tile(R, 256)
    c1 = 1.0 - ADAM_B1 ** ADAM_STEP
    c2 = 1.0 - ADAM_B2 ** ADAM_STEP

    def body(w_ref, g_ref, m_ref, v_ref, d_ref, nm_ref, nv_ref):
        gg = g_ref[...]
        nm = ADAM_B1 * m_ref[...] + (1.0 - ADAM_B1) * gg
        nv = ADAM_B2 * v_ref[...] + (1.0 - ADAM_B2) * (gg * gg)
        nm_ref[...] = nm
        nv_ref[...] = nv
        d_ref[...] = -ADAM_LR * ((nm / c1) / (jnp.sqrt(nv / c2) + ADAM_EPS) + ADAM_WD * w_ref[...])

    spec = pl.BlockSpec((tr, C), lambda i: (i, 0))
    return pl.pallas_call(
        body, grid=(R // tr,), name=name,
        in_specs=[spec] * 4, out_specs=[spec] * 3,
        out_shape=[jax.ShapeDtypeStruct((R, C), F32)] * 3,
    )(w, g, m, v)


def _ada_fwd(c_all, w_ada, b_cols):
    def body(c_ref, w_ref, b_ref, o_ref):
        c = c_ref[...]
        o_ref[...] = jnp.dot(c * _sigmoid(c), w_ref[...], preferred_element_type=F32, precision=HIGHEST) + b_ref[...]
    return pl.pallas_call(
        body, name="ada_fwd", out_shape=jax.ShapeDtypeStruct((c_all.shape[0], w_ada.shape[1]), F32),
        compiler_params=_cp(),
    )(c_all, w_ada, b_cols)


def _ada_bwd(c_all, dmod_cols):
    def body(c_ref, d_ref, o_ref):
        c = c_ref[...]
        o_ref[...] = lax.dot_general(c * _sigmoid(c), d_ref[...], (((0,), (0,)), ((), ())),
                                     preferred_element_type=F32, precision=HIGHEST)
    return pl.pallas_call(
        body, name="ada_bwd", out_shape=jax.ShapeDtypeStruct((c_all.shape[1], dmod_cols.shape[1]), F32),
        compiler_params=_cp(),
    )(c_all, dmod_cols)


def _small_reduce(gathered, n_dev, Bl):
    mod_rows, _, rows = _small_layout(Bl)
    width = gathered.shape[1]

    def body(g_ref, red_ref, bada_ref):
        acc = g_ref[0:rows, :]
        for d in range(1, n_dev):
            acc = acc + g_ref[d * rows:(d + 1) * rows, :]
        red_ref[...] = acc[mod_rows:, :]
        b = acc[0:8, :]
        for q in range(1, Bl):
            b = b + acc[8 * q:8 * q + 8, :]
        bada_ref[...] = b
    return pl.pallas_call(
        body, name="small_reduce",
        out_shape=[jax.ShapeDtypeStruct((rows - mod_rows, width), F32), jax.ShapeDtypeStruct((8, width), F32)],
        compiler_params=_cp(),
    )(gathered)


def _mesh_pos():
    return lax.axis_index("x"), lax.axis_index("y"), lax.axis_index("c")


def _other_chips(x, y):
    return [(1 - x, y), (x, 1 - y), (1 - x, 1 - y)]


def _allgather8(xs, name):
    m_per, n = xs.shape

    def body(x_ref, out_ref, send_sems, recv_sems, local_sem):
        x, y, c = _mesh_pos()
        me, sibling = (x, y, c), (x, y, 1 - c)
        chips = _other_chips(x, y)

        def rows(px, py, pc):
            return out_ref.at[pl.ds((4 * px + 2 * py + pc) * m_per, m_per), :]

        def copy(k, block, to, src=None):
            return pltpu.make_async_remote_copy(
                src_ref=rows(*block) if src is None else src, dst_ref=rows(*block),
                send_sem=send_sems.at[k], recv_sem=recv_sems.at[k], device_id=to, device_id_type=MESH_DEV)

        mine = pltpu.make_async_copy(x_ref, rows(*me), local_sem)
        mine.start()
        first = [copy(0, me, sibling, src=x_ref)]
        first += [copy(1 + j, me, (*chip, c), src=x_ref) for j, chip in enumerate(chips)]
        for cp in first:
            cp.start()
        passed = [copy(4 + j, (*chip, c), sibling) for j, chip in enumerate(chips)]
        for j, chip in enumerate(chips):
            copy(1 + j, (*chip, c), me).wait_recv()
            passed[j].start()
        copy(0, sibling, me).wait_recv()
        for j, chip in enumerate(chips):
            copy(4 + j, (*chip, 1 - c), me).wait_recv()
        for cp in first + passed:
            cp.wait_send()
        mine.wait()

    return pl.pallas_call(
        body, name=name, out_shape=jax.ShapeDtypeStruct((8 * m_per, n), xs.dtype),
        in_specs=[pl.BlockSpec(memory_space=pltpu.VMEM)], out_specs=pl.BlockSpec(memory_space=pltpu.VMEM),
        scratch_shapes=[pltpu.SemaphoreType.DMA((7,)), pltpu.SemaphoreType.DMA((7,)), pltpu.SemaphoreType.DMA],
        compiler_params=_cp(),
    )(xs)


class _WeightGather:
    def __init__(self, shapes):
        self.shapes = shapes
        self.n = len(shapes)

    def scratch(self):
        return [pltpu.SemaphoreType.DMA((6 * self.n,)), pltpu.SemaphoreType.DMA((6 * self.n,))]

    def _copy(self, outs, sems, w, k, slot, h, to):
        r2 = self.shapes[w][1] // 2
        blk = outs[w].at[slot, pl.ds(h * r2, r2), :]
        return pltpu.make_async_remote_copy(
            src_ref=blk, dst_ref=blk, send_sem=sems[0].at[6 * w + k], recv_sem=sems[1].at[6 * w + k],
            device_id=to, device_id_type=MESH_DEV)

    def start(self, outs, sems):
        x, y, c = _mesh_pos()
        for w in range(self.n):
            for k, chip in enumerate(_other_chips(x, y)):
                self._copy(outs, sems, w, k, 2 * x + y, c, (*chip, c)).start()

    def forward(self, outs, sems):
        x, y, c = _mesh_pos()
        for w in range(self.n):
            for k, chip in enumerate(_other_chips(x, y)):
                slot = 2 * chip[0] + chip[1]
                self._copy(outs, sems, w, k, slot, c, (x, y, 1 - c)).wait_recv()
                self._copy(outs, sems, w, 3 + k, slot, c, (x, y, 1 - c)).start()

    def finish(self, outs, sems):
        x, y, c = _mesh_pos()
        for w in range(self.n):
            for k, chip in enumerate(_other_chips(x, y)):
                slot = 2 * chip[0] + chip[1]
                self._copy(outs, sems, w, 3 + k, slot, 1 - c, (x, y, 1 - c)).wait_recv()
                self._copy(outs, sems, w, k, 2 * x + y, c, (*chip, c)).wait_send()
                self._copy(outs, sems, w, 3 + k, slot, c, (x, y, 1 - c)).wait_send()


def _gather_weights(bufs, name):
    n = len(bufs)
    plan = _WeightGather([b.shape for b in bufs])

    def body(*refs):
        outs = refs[n:2 * n]
        sems = refs[2 * n:]
        plan.start(outs, sems)
        plan.forward(outs, sems)
        plan.finish(outs, sems)

    anyspec = pl.BlockSpec(memory_space=pl.ANY)
    return pl.pallas_call(
        body, name=name,
        out_shape=[jax.ShapeDtypeStruct(b.shape, b.dtype) for b in bufs],
        in_specs=[anyspec] * n, out_specs=[anyspec] * n,
        input_output_aliases={w: w for w in range(n)},
        scratch_shapes=plan.scratch(),
    )(*bufs)


def _rs_sibling(grads, name):
    n = len(grads)

    def body(*refs):
        ins, outs = refs[:n], refs[n:2 * n]
        send_sems, recv_sems = refs[2 * n:]
        x, y, c = _mesh_pos()
        cps = []
        for w in range(n):
            P, R, _ = grads[w].shape
            r2 = R // 2
            for p in range(P):
                cps.append(pltpu.make_async_remote_copy(
                    src_ref=ins[w].at[p, pl.ds((1 - c) * r2, r2), :], dst_ref=outs[w].at[p],
                    send_sem=send_sems.at[4 * w + p], recv_sem=recv_sems.at[4 * w + p],
                    device_id=(x, y, 1 - c), device_id_type=MESH_DEV))
                cps[-1].start()
        for cp in cps:
            cp.wait()

    anyspec = pl.BlockSpec(memory_space=pl.ANY)
    return pl.pallas_call(
        body, name=name,
        out_shape=[jax.ShapeDtypeStruct((g.shape[0], g.shape[1] // 2, g.shape[2]), g.dtype) for g in grads],
        in_specs=[anyspec] * n, out_specs=[anyspec] * n,
        scratch_shapes=[pltpu.SemaphoreType.DMA((4 * n,)), pltpu.SemaphoreType.DMA((4 * n,))],
    )(*grads)


class _ChipExchange:
    def __init__(self, n):
        self.n = n

    def scratch(self):
        return [pltpu.SemaphoreType.DMA((3 * self.n,)), pltpu.SemaphoreType.DMA((3 * self.n,))]

    def _copies(self, ins, outs, sems):
        x, y, c = _mesh_pos()
        return [pltpu.make_async_remote_copy(
            src_ref=ins[w].at[2 * chip[0] + chip[1]], dst_ref=outs[w].at[k],
            send_sem=sems[0].at[3 * w + k], recv_sem=sems[1].at[3 * w + k],
            device_id=(*chip, c), device_id_type=MESH_DEV)
            for w in range(self.n) for k, chip in enumerate(_other_chips(x, y))]

    def start(self, ins, outs, sems):
        for cp in self._copies(ins, outs, sems):
            cp.start()

    def finish(self, ins, outs, sems):
        for cp in self._copies(ins, outs, sems):
            cp.wait()


def _rs_chips(sums, name):
    n = len(sums)
    plan = _ChipExchange(n)

    def body(*refs):
        ins, outs, sems = refs[:n], refs[n:2 * n], refs[2 * n:]
        plan.start(ins, outs, sems)
        plan.finish(ins, outs, sems)

    anyspec = pl.BlockSpec(memory_space=pl.ANY)
    return pl.pallas_call(
        body, name=name,
        out_shape=[jax.ShapeDtypeStruct((3,) + s.shape[1:], s.dtype) for s in sums],
        in_specs=[anyspec] * n, out_specs=[anyspec] * n,
        scratch_shapes=plan.scratch(),
    )(*sums)


def _rs_final(bufs):
    n = len(bufs)

    def body(*refs):
        outs = refs[n:2 * n]
        send_sems, recv_sems = refs[2 * n:]
        x, y, c = _mesh_pos()
        cps = []
        for w in range(n):
            r2 = bufs[w].shape[0] // 2
            mine = outs[w].at[pl.ds(c * r2, r2), :]
            cps.append(pltpu.make_async_remote_copy(
                src_ref=mine, dst_ref=mine, send_sem=send_sems.at[w], recv_sem=recv_sems.at[w],
                device_id=(x, y, 1 - c), device_id_type=MESH_DEV))
            cps[-1].start()
        for cp in cps:
            cp.wait()

    anyspec = pl.BlockSpec(memory_space=pl.ANY)
    return pl.pallas_call(
        body, name="rs_final",
        out_shape=[jax.ShapeDtypeStruct(b.shape, b.dtype) for b in bufs],
        in_specs=[anyspec] * n, out_specs=[anyspec] * n,
        input_output_aliases={w: w for w in range(n)},
        scratch_shapes=[pltpu.SemaphoreType.DMA((n,)), pltpu.SemaphoreType.DMA((n,))],
    )(*bufs)


BIG = ("w_in", "w_out", "w_gate", "w_up", "w_down")
WEIGHTS = ("w_ada", "b_ada", "g_mix", "w_in", "w_dw", "b_dw", "g_conv_ln", "b_conv_ln", "g_q", "g_k",
           "w_out", "g_ffn", "w_gate", "w_up", "w_down")


def _pad_to(a, rows, cols):
    return jnp.pad(a, ((0, rows - a.shape[0]), (0, cols - a.shape[1])))


def kernel(x, c, w_ada, b_ada, g_mix, w_in, w_dw, b_dw, g_conv_ln, b_conv_ln, g_q, g_k, w_out, g_ffn, w_gate, w_up, w_down, loss_target, m_w_ada, m_b_ada, m_g_mix, m_w_in, m_w_dw, m_b_dw, m_g_conv_ln, m_b_conv_ln, m_g_q, m_g_k, m_w_out, m_g_ffn, m_w_gate, m_w_up, m_w_down, v_w_ada, v_b_ada, v_g_mix, v_w_in, v_w_dw, v_b_dw, v_g_conv_ln, v_b_conv_ln, v_g_q, v_g_k, v_w_out, v_g_ffn, v_w_gate, v_w_up, v_w_down):
    w = dict(w_ada=w_ada, b_ada=b_ada, g_mix=g_mix, w_in=w_in, w_dw=w_dw, b_dw=b_dw, g_conv_ln=g_conv_ln,
             b_conv_ln=b_conv_ln, g_q=g_q, g_k=g_k, w_out=w_out, g_ffn=g_ffn, w_gate=w_gate, w_up=w_up, w_down=w_down)
    m = dict(w_ada=m_w_ada, b_ada=m_b_ada, g_mix=m_g_mix, w_in=m_w_in, w_dw=m_w_dw, b_dw=m_b_dw, g_conv_ln=m_g_conv_ln,
             b_conv_ln=m_b_conv_ln, g_q=m_g_q, g_k=m_g_k, w_out=m_w_out, g_ffn=m_g_ffn, w_gate=m_w_gate, w_up=m_w_up,
             w_down=m_w_down)
    v = dict(w_ada=v_w_ada, b_ada=v_b_ada, g_mix=v_g_mix, w_in=v_w_in, w_dw=v_w_dw, b_dw=v_b_dw, g_conv_ln=v_g_conv_ln,
             b_conv_ln=v_b_conv_ln, g_q=v_g_q, g_k=v_g_k, w_out=v_w_out, g_ffn=v_g_ffn, w_gate=v_w_gate, w_up=v_w_up,
             w_down=v_w_down)
    Bl, S, D = x.shape
    DC = g_conv_ln.shape[1]
    NA = w_ada.shape[2]
    xi, yi, ci = _mesh_pos()
    p = 2 * xi + yi
    dev = 2 * p + ci
    n_dev = 8
    cidx = jnp.reshape(ci, (1,)).astype(jnp.int32)
    pidx = jnp.reshape(p, (1,)).astype(jnp.int32)

    first = jnp.concatenate([_pad_to(c, 8, D), _pad_to(w_dw[0], CONV_ROWS, D)], axis=0)
    g0 = _allgather8(first, "gather_cond").reshape(n_dev, 8 + CONV_ROWS, D)
    c_all = g0[:, :Bl].reshape(n_dev * Bl, D)
    taps = jnp.concatenate([g0[2 * q, 8:, :w_dw.shape[2]] for q in range(4)], axis=1)
    wdw = jnp.where(lax.broadcasted_iota(jnp.int32, taps.shape, 0) == CONV_WIDTH, b_dw, taps)
    owned = {nm: _cast_weight(w[nm][0], pidx, "cast_" + nm) for nm in BIG}
    (w_in_full,) = _gather_weights([owned["w_in"]], "gather_w_in")

    b_cols = lax.dynamic_slice_in_dim(b_ada, p * NA, NA, axis=1)
    mod_part = _ada_fwd(c_all, w_ada[0], b_cols)
    gm = _allgather8(mod_part, "gather_mod").reshape(n_dev, n_dev * Bl, NA)
    mod = jnp.concatenate([lax.dynamic_slice_in_dim(gm[2 * q], dev * Bl, Bl, axis=0) for q in range(4)], axis=1)

    def chip_partials(parts, names, tag):
        from_sib = _rs_sibling(parts, "rs_sibling_" + tag)
        return [_pair_add(g, r, cidx, "pair_add_" + nm) for nm, g, r in zip(names, parts, from_sib)]

    loc = _local_step(x, loss_target, mod, g_mix, wdw, g_conv_ln, b_conv_ln, g_q, g_k, g_ffn,
                      w_in_full, owned["w_out"], owned["w_gate"], owned["w_up"], owned["w_down"],
                      gather_in_attention=True,
                      ffn_partials=lambda parts: chip_partials(parts, FFN_WEIGHTS, "ffn"))
    loss = lax.psum(loc["loss"], ("x", "y", "c"))

    rest = ("w_in", "w_out")
    rest_sums = chip_partials([loc["grads"][nm] for nm in rest], rest, "rest")
    rest_recv = _rs_chips([sb for _, sb in rest_sums], "rs_chips_rest")
    pc_idx = jnp.stack([p, ci]).astype(jnp.int32)
    order = FFN_WEIGHTS + rest
    sums32 = loc["ffn_sums"] + [s32 for s32, _ in rest_sums]
    recv = loc["ffn_recv"] + list(rest_recv)
    halves = [_final_add(s32, r, pc_idx, "final_add_" + nm) for nm, s32, r in zip(order, sums32, recv)]
    grad = dict(zip(order, _rs_final(halves)))

    mod_rows, _, small_rows = _small_layout(Bl)
    gs = _allgather8(loc["packed"], "gather_small")
    red, bada8 = _small_reduce(gs, n_dev, Bl)
    dmod_all = gs.reshape(n_dev, small_rows, D)[:, :mod_rows].reshape(n_dev * Bl, 8, D)[:, :N_MOD].reshape(n_dev * Bl, N_MOD * D)
    grad["w_ada"] = _ada_bwd(c_all, lax.dynamic_slice_in_dim(dmod_all, p * NA, NA, axis=1))
    grad["b_ada"] = bada8[:N_MOD].reshape(1, N_MOD * D)
    grad["g_mix"] = red[0:1]
    grad["g_ffn"] = red[1:2]
    grad["g_conv_ln"] = red[2:3, :DC]
    grad["b_conv_ln"] = red[2:3, DC:2 * DC]
    grad["g_q"] = red[3:4, :HEAD_DIM]
    grad["g_k"] = red[3:4, HEAD_DIM:2 * HEAD_DIM]
    dwdw = red[8:8 + CONV_ROWS, :DC]
    grad["w_dw"] = lax.dynamic_slice_in_dim(dwdw[:CONV_WIDTH], p * w_dw.shape[2], w_dw.shape[2], axis=1)
    grad["b_dw"] = dwdw[CONV_WIDTH:CONV_WIDTH + 1]

    delta, new_m, new_v = {}, {}, {}
    for nm in WEIGHTS:
        shp = w[nm].shape
        two_d = (shp[-2], shp[-1]) if len(shp) == 3 else shp
        d_, m_, v_ = _adamw(w[nm].reshape(two_d), grad[nm].reshape(two_d), m[nm].reshape(two_d), v[nm].reshape(two_d),
                            "adamw_" + nm)
        grad[nm] = grad[nm].reshape(shp)
        delta[nm], new_m[nm], new_v[nm] = d_.reshape(shp), m_.reshape(shp), v_.reshape(shp)

    return (loss, loc["dx"], *[grad[nm] for nm in WEIGHTS], *[delta[nm] for nm in WEIGHTS],
            *[new_m[nm] for nm in WEIGHTS], *[new_v[nm] for nm in WEIGHTS])
```

```python
import functools
import math

import jax
import jax.numpy as jnp
import numpy as np
from jax import lax
from jax.experimental import pallas as pl
from jax.experimental.pallas import tpu as pltpu

F32 = jnp.float32
MXU_DTYPE = jnp.bfloat16
ACT_DTYPE = jnp.bfloat16
EPS = 1e-6
NEG_INF = -1e30
HEAD_DIM = 64
LANES = 128
RADIUS = 64
QBLK = 128
DILATIONS = (1, 4, 16)
CONV_WIDTH = 31
CONV_PAD = CONV_WIDTH // 2
CONV_ROWS = 32
N_MOD = 6
ADAM_LR, ADAM_B1, ADAM_B2, ADAM_EPS, ADAM_WD, ADAM_STEP = 0.001, 0.9, 0.999, 1e-08, 0.01, 10
HIGHEST = lax.Precision.HIGHEST
MESH_DEV = pl.DeviceIdType.MESH
VMEM_LIMIT = 56 << 20


def _cp(sem=None, vmem=VMEM_LIMIT):
    kw = dict(vmem_limit_bytes=vmem)
    if sem is not None:
        kw["dimension_semantics"] = sem
    return pltpu.CompilerParams(**kw)


def _sigmoid(x):
    return 1.0 / (1.0 + jnp.exp(-x))


def _dot(a, b):
    return jnp.dot(a, b, preferred_element_type=F32)


def _dot_nt(a, b):
    return lax.dot_general(a, b, (((1,), (1,)), ((), ())), preferred_element_type=F32)


def _dot_tn(a, b):
    return lax.dot_general(a, b, (((0,), (0,)), ((), ())), preferred_element_type=F32)


def _colsum(v):
    return jnp.sum(v, axis=0, keepdims=True)


def _load_resident(i, pairs, sems):
    @pl.when(i == 0)
    def _():
        cps = [pltpu.make_async_copy(src, dst, sems.at[n]) for n, (src, dst) in enumerate(pairs)]
        for c in cps:
            c.start()
        for c in cps:
            c.wait()


def _fwd_in(x2, mod, g_mix, gq2, gk2, w_in, *, S, tm, n_ag):
    T, D = x2.shape
    P, _, Nb = w_in.shape
    n_in = P * Nb
    n_slab = (n_in - n_ag) // LANES
    NS = n_slab // 3
    tps = S // tm

    def body(x_ref, mod_ref, g_ref, gq_ref, gk_ref, w_ref, ag_ref, qkv_ref, qkh_ref, h_ref):
        x = x_ref[...]
        r = lax.rsqrt(jnp.mean(x * x, axis=-1, keepdims=True) + EPS)
        n = x * r * g_ref[...]
        h = n * (1.0 + mod_ref[:, D:2 * D]) + mod_ref[:, 0:D]
        hb = h.astype(MXU_DTYPE)
        h_ref[...] = hb
        parts = [_dot(hb, w_ref[p]) for p in range(P)]
        proj = jnp.concatenate(parts, axis=1) if P > 1 else parts[0]
        ag_ref[...] = proj[:, :n_ag]
        mm = _head_mean_matrix()
        for j in range(n_slab):
            v = proj[:, n_ag + LANES * j:n_ag + LANES * (j + 1)]
            qkv_ref[j] = v
            if j < 2 * NS:
                gain = gq_ref[...] * (HEAD_DIM ** -0.5 * LOG2E) if j < NS else gk_ref[...]
                qkh_ref[j] = v * lax.rsqrt(_head_mean(v * v, mm) + EPS) * gain

    return pl.pallas_call(
        body, grid=(T // tm,), name="fwd_in",
        in_specs=[pl.BlockSpec((tm, D), lambda i: (i, 0)),
                  pl.BlockSpec((None, 1, N_MOD * D), lambda i: (i // tps, 0, 0)),
                  pl.BlockSpec((1, D), lambda i: (0, 0)),
                  pl.BlockSpec((1, LANES), lambda i: (0, 0)), pl.BlockSpec((1, LANES), lambda i: (0, 0)),
                  pl.BlockSpec((P, D, Nb), lambda i: (0, 0, 0))],
        out_specs=[pl.BlockSpec((tm, n_ag), lambda i: (i, 0)),
                   pl.BlockSpec((n_slab, tm, LANES), lambda i: (0, i, 0)),
                   pl.BlockSpec((2 * NS, tm, LANES), lambda i: (0, i, 0)),
                   pl.BlockSpec((tm, D), lambda i: (i, 0))],
        out_shape=[jax.ShapeDtypeStruct((T, n_ag), F32),
                   jax.ShapeDtypeStruct((n_slab, T, LANES), F32),
                   jax.ShapeDtypeStruct((2 * NS, T, LANES), F32),
                   jax.ShapeDtypeStruct((T, D), MXU_DTYPE)],
        compiler_params=_cp(("arbitrary",)),
    )(x2, mod, g_mix, gq2, gk2, w_in)


CONV_CH = 64


def _conv_taps(win, w_ref, acc, reverse):
    n = win.shape[0]
    for b in range(8):
        wb = win if b == 0 else pltpu.roll(win, shift=n - b, axis=0)
        for a in range(4):
            o = 8 * a + b
            if o < 1 or o > CONV_WIDTH:
                continue
            k = (CONV_WIDTH - o) if reverse else (o - 1)
            acc = acc + w_ref[k:k + 1, :] * wb[8 * a:8 * a + CONV_CH, :]
    return acc


def _conv_fwd(ag, wdw, *, Bl, S, DC):
    T = ag.shape[0]
    nsc = DC // LANES
    CH = CONV_CH

    def body(a_ref, g_ref, w_ref, cv_ref, upad):
        zeros16 = jnp.zeros((16, LANES), F32)
        upad[0:16, :] = zeros16
        upad[S + 16:S + 32, :] = zeros16

        def fill(i, _):
            r0 = pl.multiple_of(i * CH, CH)
            a = a_ref[pl.ds(r0, CH), :]
            g = g_ref[pl.ds(r0, CH), :]
            upad[pl.ds(r0 + 16, CH), :] = a * _sigmoid(g)
            return 0
        lax.fori_loop(0, S // CH, fill, 0)

        def conv(i, _):
            r0 = pl.multiple_of(i * CH, CH)
            win = upad[pl.ds(r0, CH + 32), :]
            acc = jnp.zeros((CH, LANES), F32) + w_ref[CONV_WIDTH:CONV_WIDTH + 1, :]
            cv_ref[pl.ds(r0, CH), :] = _conv_taps(win, w_ref, acc, reverse=False)
            return 0
        lax.fori_loop(0, S // CH, conv, 0)

    return pl.pallas_call(
        body, grid=(Bl, nsc), name="conv_fwd",
        in_specs=[pl.BlockSpec((S, LANES), lambda b, j: (b, j)),
                  pl.BlockSpec((S, LANES), lambda b, j: (b, nsc + j)),
                  pl.BlockSpec((CONV_ROWS, LANES), lambda b, j: (0, j))],
        out_specs=pl.BlockSpec((S, LANES), lambda b, j: (b, j)),
        out_shape=jax.ShapeDtypeStruct((T, DC), F32),
        scratch_shapes=[pltpu.VMEM((S + 32, LANES), F32)],
        compiler_params=_cp(("arbitrary", "arbitrary")),
    )(ag, ag, wdw)


def _conv_bwd(ag, dcv, wdw, *, Bl, S, DC):
    T = ag.shape[0]
    nsc = DC // LANES
    CH = CONV_CH

    def body(a_ref, g_ref, d_ref, w_ref, da_ref, dg_ref, dw_ref, upad, dpad, wacc):
        b = pl.program_id(1)
        zeros16 = jnp.zeros((16, LANES), F32)
        upad[0:16, :] = zeros16
        upad[S + 16:S + 32, :] = zeros16
        dpad[0:16, :] = zeros16
        dpad[S + 16:S + 32, :] = zeros16

        @pl.when(b == 0)
        def _():
            wacc[...] = jnp.zeros_like(wacc)

        def fill(i, _):
            r0 = pl.multiple_of(i * CH, CH)
            a = a_ref[pl.ds(r0, CH), :]
            g = g_ref[pl.ds(r0, CH), :]
            upad[pl.ds(r0 + 16, CH), :] = a * _sigmoid(g)
            dpad[pl.ds(r0 + 16, CH), :] = d_ref[pl.ds(r0, CH), :]
            return 0
        lax.fori_loop(0, S // CH, fill, 0)

        def step(i, _):
            r0 = pl.multiple_of(i * CH, CH)
            dwin = dpad[pl.ds(r0, CH + 32), :]
            du = _conv_taps(dwin, w_ref, jnp.zeros((CH, LANES), F32), reverse=True)
            a = a_ref[pl.ds(r0, CH), :]
            g = g_ref[pl.ds(r0, CH), :]
            sg = _sigmoid(g)
            da_ref[pl.ds(r0, CH), :] = du * sg
            dg_ref[pl.ds(r0, CH), :] = du * a * sg * (1.0 - sg)
            dc = d_ref[pl.ds(r0, CH), :]
            uwin = upad[pl.ds(r0, CH + 32), :]
            n = CH + 32
            for bb in range(8):
                wb = uwin if bb == 0 else pltpu.roll(uwin, shift=n - bb, axis=0)
                for aa in range(4):
                    o = 8 * aa + bb
                    if o < 1 or o > CONV_WIDTH:
                        continue
                    k = o - 1
                    prod = dc * wb[8 * aa:8 * aa + CH, :]
                    part = prod[0:8, :]
                    for q in range(1, CH // 8):
                        part = part + prod[8 * q:8 * q + 8, :]
                    wacc[8 * k:8 * k + 8, :] += part
            part = dc[0:8, :]
            for q in range(1, CH // 8):
                part = part + dc[8 * q:8 * q + 8, :]
            wacc[8 * CONV_WIDTH:8 * CONV_WIDTH + 8, :] += part
            return 0
        lax.fori_loop(0, S // CH, step, 0)

        @pl.when(b == Bl - 1)
        def _():
            for k in range(CONV_ROWS):
                dw_ref[k:k + 1, :] = jnp.sum(wacc[8 * k:8 * k + 8, :], axis=0, keepdims=True)

    return pl.pallas_call(
        body, grid=(nsc, Bl), name="conv_bwd",
        in_specs=[pl.BlockSpec((S, LANES), lambda j, b: (b, j)),
                  pl.BlockSpec((S, LANES), lambda j, b: (b, nsc + j)),
                  pl.BlockSpec((S, LANES), lambda j, b: (b, j)),
                  pl.BlockSpec((CONV_ROWS, LANES), lambda j, b: (0, j))],
        out_specs=[pl.BlockSpec((S, LANES), lambda j, b: (b, j)),
                   pl.BlockSpec((S, LANES), lambda j, b: (b, j)),
                   pl.BlockSpec((CONV_ROWS, LANES), lambda j, b: (0, j))],
        out_shape=[jax.ShapeDtypeStruct((T, DC), F32), jax.ShapeDtypeStruct((T, DC), F32),
                   jax.ShapeDtypeStruct((CONV_ROWS, DC), F32)],
        scratch_shapes=[pltpu.VMEM((S + 32, LANES), F32), pltpu.VMEM((S + 32, LANES), F32),
                        pltpu.VMEM((8 * CONV_ROWS, LANES), F32)],
        compiler_params=_cp(("arbitrary", "arbitrary")),
    )(ag, ag, dcv, wdw)


ROWCH = 256


LOG2E = 1.4426950408889634
LN2 = 0.6931471805599453
N_EDGE = 4


def _head_mean_matrix():
    r = lax.broadcasted_iota(jnp.int32, (LANES, LANES), 0) // HEAD_DIM
    c = lax.broadcasted_iota(jnp.int32, (LANES, LANES), 1) // HEAD_DIM
    return jnp.where(r == c, 1.0 / HEAD_DIM, 0.0).astype(jnp.bfloat16)


def _head_mean(v, mm):
    hi = v.astype(jnp.bfloat16)
    lo = (v - hi.astype(F32)).astype(jnp.bfloat16)
    return _dot(hi, mm) + _dot(lo, mm)


def _stack_heads(blk, lane_lo):
    z = jnp.zeros_like(blk)
    return jnp.concatenate([jnp.where(lane_lo, blk, z), jnp.where(lane_lo, z, blk)], axis=0)


def _merge_heads(v2, lane_lo):
    return jnp.where(lane_lo, v2[:QBLK], v2[QBLK:])


def _bias_tables(bias_ref, slope_ref):
    row = lax.broadcasted_iota(jnp.int32, (2 * QBLK, 2 * QBLK), 0)
    col = lax.broadcasted_iota(jnp.int32, (2 * QBLK, 2 * QBLK), 1)
    rel = jnp.abs(col - RADIUS - (row % QBLK))
    slope = jnp.where(row < QBLK, slope_ref[0:1, 0:1], slope_ref[0:1, HEAD_DIM:HEAD_DIM + 1]) * LOG2E
    for pi, d in enumerate(DILATIONS):
        inside = jnp.where(rel <= RADIUS, -slope * (float(d) * rel.astype(F32)), NEG_INF)
        for e in range(N_EDGE):
            t = inside
            if e & 1:
                t = jnp.where(col < RADIUS, NEG_INF, t)
            if e & 2:
                t = jnp.where(col >= QBLK + RADIUS, NEG_INF, t)
            bias_ref[N_EDGE * pi + e] = t


def _edge_index(qb, nb):
    return jnp.where(qb == 0, 1, 0) + jnp.where(qb == nb - 1, 2, 0)


def _gather_rows(src_ref, dst_ref, S, d, pad):
    n = S // d
    seg = n + 2 * RADIUS if pad else n
    step = min(n, 512)
    for r in range(d):
        base = r * seg
        if pad:
            dst_ref[base:base + RADIUS, :] = jnp.zeros((RADIUS, LANES), dst_ref.dtype)
            dst_ref[base + RADIUS + n:base + seg, :] = jnp.zeros((RADIUS, LANES), dst_ref.dtype)
            base += RADIUS
        for c0 in range(0, n, step):
            if d == 1:
                v = src_ref[c0:c0 + step, :]
            else:
                v = src_ref[pl.ds(r + c0 * d, step, stride=d), :]
            dst_ref[base + c0:base + c0 + step, :] = v.astype(dst_ref.dtype)


def _scatter_rows(src_ref, dst_ref, S, d, pad, accumulate):
    n = S // d
    seg = n + 2 * RADIUS if pad else n
    step = min(n, 512)
    for r in range(d):
        base = r * seg + (RADIUS if pad else 0)
        for c0 in range(0, n, step):
            v = src_ref[base + c0:base + c0 + step, :]
            if d == 1:
                idx = pl.ds(c0, step)
            else:
                idx = pl.ds(r + c0 * d, step, stride=d)
            if accumulate:
                dst_ref[idx, :] = dst_ref[idx, :] + v
            else:
                dst_ref[idx, :] = v


def _zero_uncovered(acc, S, d):
    n = S // d
    if (n // QBLK) % 2:
        return
    seg = n + 2 * RADIUS
    for r in range(d):
        acc[0, r * seg + n:r * seg + seg, :] = jnp.zeros((2 * RADIUS, LANES), F32)
        acc[1, r * seg:r * seg + 2 * RADIUS, :] = jnp.zeros((2 * RADIUS, LANES), F32)


def _scatter_parity(acc, dst_ref, S, d):
    n = S // d
    seg = n + 2 * RADIUS
    step = min(n, 512)
    one_block = (n // QBLK) % 2 == 1
    for r in range(d):
        base = r * seg + RADIUS
        for c0 in range(0, n, step):
            rows = slice(base + c0, base + c0 + step)
            v = acc[r % 2, rows, :] if one_block else acc[0, rows, :] + acc[1, rows, :]
            idx = pl.ds(c0, step) if d == 1 else pl.ds(r + c0 * d, step, stride=d)
            dst_ref[idx, :] = dst_ref[idx, :] + v


PIPE_UNROLL = 2
PIPE_SLOTS = 8


def _pipeline(n_items, stages, unroll):
    K = len(stages)
    assert n_items % unroll == 0 and K * unroll <= PIPE_SLOTS
    trips = n_items // unroll
    assert trips >= K - 1

    def trip(t, static):
        for s in reversed(range(K)):
            if static and not 0 <= t - s < trips:
                continue
            for u in range(unroll):
                item = unroll * (t - s) + u
                stages[s](jnp.int32(item) if static else item)

    for t in range(K - 1):
        trip(t, True)

    def full(t, carry):
        trip(t, False)
        return carry
    lax.fori_loop(K - 1, trips, full, 0)
    for t in range(trips, trips + K - 1):
        trip(t, True)


def _attn_fwd(qkh, qkv, slopes, *, Bl, S, hosted=()):
    n3, T, _ = qkv.shape
    NS = n3 // 3
    NB = S // QBLK
    PADR = S + 2 * RADIUS * DILATIONS[-1]
    nh = len(hosted)
    plan = _WeightGather([b.shape for b in hosted]) if nh else None
    n_steps = Bl * NS

    def body(qh, kh, v_ref, slope_ref, *rest):
        o_ref, lse_ref = rest[nh:nh + 2]
        wouts = rest[nh + 2:2 * nh + 2]
        (qp, kp, vp, op, lp, onat, lnat, bias_ref, sbuf, pbuf, mbuf, lbuf) = rest[2 * nh + 2:2 * nh + 14]
        sems = rest[2 * nh + 14:]
        step = pl.program_id(0) * NS + pl.program_id(1)
        if nh:
            @pl.when(step == 0)
            def _():
                plan.start(wouts, sems)

            @pl.when(step == n_steps // 2)
            def _():
                plan.forward(wouts, sems)

        lane_lo = lax.broadcasted_iota(jnp.int32, (QBLK, LANES), 1) < HEAD_DIM
        _bias_tables(bias_ref, slope_ref)

        for pi, d in enumerate(DILATIONS):
            n = S // d
            nb = n // QBLK
            _gather_rows(qh, qp, S, d, pad=False)
            _gather_rows(kh, kp, S, d, pad=True)
            _gather_rows(v_ref, vp, S, d, pad=True)

            def offsets(i, nb=nb):
                r = i // nb
                return pl.multiple_of(i * QBLK, QBLK), pl.multiple_of((i + r) * QBLK, QBLK), i % nb

            def scores(i, pi=pi, nb=nb):
                q0, k0, qb = offsets(i)
                qs = _stack_heads(qp[pl.ds(q0, QBLK), :], lane_lo)
                sbuf[i % PIPE_SLOTS] = (_dot_nt(qs, kp[pl.ds(k0, 2 * QBLK), :])
                                        + bias_ref[N_EDGE * pi + _edge_index(qb, nb)])

            def rowmax(i):
                m = jnp.max(sbuf[i % PIPE_SLOTS], axis=1, keepdims=True)
                mbuf[i % PIPE_SLOTS] = jnp.broadcast_to(m, (2 * QBLK, LANES))

            def expsum(i):
                m = mbuf[i % PIPE_SLOTS]
                p = jnp.exp2(sbuf[i % PIPE_SLOTS] - jnp.concatenate([m, m], axis=1))
                pbuf[i % PIPE_SLOTS] = p.astype(MXU_DTYPE)
                lbuf[i % PIPE_SLOTS] = jnp.broadcast_to(jnp.sum(p, axis=1, keepdims=True), (2 * QBLK, LANES))

            def values(i):
                q0, k0, _ = offsets(i)
                l = lbuf[i % PIPE_SLOTS]
                o2 = _dot(pbuf[i % PIPE_SLOTS], vp[pl.ds(k0, 2 * QBLK), :]) * (1.0 / l)
                op[pl.ds(q0, QBLK), :] = _merge_heads(o2, lane_lo)
                lp[pl.ds(q0, QBLK), :] = _merge_heads(mbuf[i % PIPE_SLOTS] + jnp.log2(l), lane_lo)

            _pipeline(NB, [scores, rowmax, expsum, values], PIPE_UNROLL)
            _scatter_rows(op, onat.at[pi], S, d, pad=False, accumulate=False)
            _scatter_rows(lp, lnat.at[pi], S, d, pad=False, accumulate=False)

        for c0 in range(0, S, ROWCH):
            ls = [lnat[pi, c0:c0 + ROWCH, :] for pi in range(len(DILATIONS))]
            mx = jnp.maximum(jnp.maximum(ls[0], ls[1]), ls[2])
            es = [jnp.exp2(l - mx) for l in ls]
            tot = es[0] + es[1] + es[2]
            inv = 1.0 / tot
            acc = (es[0] * inv) * onat[0, c0:c0 + ROWCH, :]
            for pi in (1, 2):
                acc = acc + (es[pi] * inv) * onat[pi, c0:c0 + ROWCH, :]
            o_ref[c0:c0 + ROWCH, :] = acc
            lse_ref[c0:c0 + ROWCH, :] = mx + jnp.log2(tot)

        if nh:
            @pl.when(step == n_steps - 1)
            def _():
                plan.finish(wouts, sems)

    spec_in = lambda off: pl.BlockSpec((None, S, LANES), lambda b, j: (off * NS + j, b, 0))
    out = pl.BlockSpec((S, LANES), lambda b, j: (b, j))
    anyspec = pl.BlockSpec(memory_space=pl.ANY)
    return pl.pallas_call(
        body, grid=(Bl, NS), name="attn_fwd",
        in_specs=[spec_in(0), spec_in(1), spec_in(2),
                  pl.BlockSpec((None, 8, LANES), lambda b, j: (j, 0, 0))] + [anyspec] * nh,
        out_specs=[out, out] + [anyspec] * nh,
        out_shape=[jax.ShapeDtypeStruct((T, NS * LANES), F32)] * 2
                  + [jax.ShapeDtypeStruct(b.shape, b.dtype) for b in hosted],
        input_output_aliases={4 + w: 2 + w for w in range(nh)},
        scratch_shapes=[pltpu.VMEM((S, LANES), MXU_DTYPE), pltpu.VMEM((PADR, LANES), MXU_DTYPE),
                        pltpu.VMEM((PADR, LANES), MXU_DTYPE),
                        pltpu.VMEM((S, LANES), F32), pltpu.VMEM((S, LANES), F32),
                        pltpu.VMEM((3, S, LANES), F32), pltpu.VMEM((3, S, LANES), F32),
                        pltpu.VMEM((N_EDGE * len(DILATIONS), 2 * QBLK, 2 * QBLK), F32),
                        pltpu.VMEM((PIPE_SLOTS, 2 * QBLK, 2 * QBLK), F32),
                        pltpu.VMEM((PIPE_SLOTS, 2 * QBLK, 2 * QBLK), MXU_DTYPE),
                        pltpu.VMEM((PIPE_SLOTS, 2 * QBLK, LANES), F32), pltpu.VMEM((PIPE_SLOTS, 2 * QBLK, LANES), F32)]
                       + (plan.scratch() if nh else []),
        compiler_params=_cp(("arbitrary", "arbitrary")),
    )(qkh, qkh, qkv, slopes, *hosted)


def _attn_bwd(qkh, qkv, o, lse, do, gq2, gk2, slopes, *, Bl, S, hosted=()):
    n3, T, _ = qkv.shape
    NS = n3 // 3
    NB = S // QBLK
    PADR = S + 2 * RADIUS * DILATIONS[-1]
    QSCALE = HEAD_DIM ** -0.5
    nh = len(hosted)
    plan = _ChipExchange(nh)
    n_steps = Bl * NS

    def body(qh, kh, q_ref, k_ref, v_ref, o_ref, lse_ref, do_ref, gq_ref, gk_ref, slope_ref, *rest):
        hin = rest[:nh]
        dq_ref, dk_ref, dv_ref, gacc_ref = rest[nh:nh + 4]
        hout = rest[nh + 4:2 * nh + 4]
        (dl, qp, kp, vp, dop, lp, dlp, dqp, dkacc, dvacc, dqn, dkn, bias_ref,
         sbuf, dpbuf, pbuf, dsbuf) = rest[2 * nh + 4:2 * nh + 21]
        sems = rest[2 * nh + 21:]
        step = pl.program_id(0) * NS + pl.program_id(1)

        @pl.when(step == 0)
        def _():
            gacc_ref[...] = jnp.zeros_like(gacc_ref)
            if nh:
                plan.start(hin, hout, sems)

        mm = _head_mean_matrix()
        lane_lo = lax.broadcasted_iota(jnp.int32, (QBLK, LANES), 1) < HEAD_DIM
        _bias_tables(bias_ref, slope_ref)
        for c0 in range(0, S, ROWCH):
            dl[c0:c0 + ROWCH, :] = _head_mean(do_ref[c0:c0 + ROWCH, :] * o_ref[c0:c0 + ROWCH, :], mm) * HEAD_DIM
            dqn[c0:c0 + ROWCH, :] = jnp.zeros((ROWCH, LANES), F32)
            dkn[c0:c0 + ROWCH, :] = jnp.zeros((ROWCH, LANES), F32)
            dv_ref[c0:c0 + ROWCH, :] = jnp.zeros((ROWCH, LANES), F32)

        for pi, d in enumerate(DILATIONS):
            n = S // d
            nb = n // QBLK
            _gather_rows(qh, qp, S, d, pad=False)
            _gather_rows(kh, kp, S, d, pad=True)
            _gather_rows(v_ref, vp, S, d, pad=True)
            _gather_rows(do_ref, dop, S, d, pad=False)
            _gather_rows(lse_ref, lp, S, d, pad=False)
            _gather_rows(dl, dlp, S, d, pad=False)
            _zero_uncovered(dkacc, S, d)
            _zero_uncovered(dvacc, S, d)

            def offsets(i, nb=nb):
                r = i // nb
                return pl.multiple_of(i * QBLK, QBLK), pl.multiple_of((i + r) * QBLK, QBLK), i % nb

            def scores(i, pi=pi, nb=nb):
                q0, k0, qb = offsets(i)
                qs = _stack_heads(qp[pl.ds(q0, QBLK), :], lane_lo)
                dos = _stack_heads(dop[pl.ds(q0, QBLK), :], lane_lo)
                sbuf[i % PIPE_SLOTS] = (_dot_nt(qs, kp[pl.ds(k0, 2 * QBLK), :])
                                        + bias_ref[N_EDGE * pi + _edge_index(qb, nb)])
                dpbuf[i % PIPE_SLOTS] = _dot_nt(dos, vp[pl.ds(k0, 2 * QBLK), :])

            def probs(i):
                q0, _, _ = offsets(i)
                lblk = lp[pl.ds(q0, QBLK), :]
                dblk = dlp[pl.ds(q0, QBLK), :]
                lcol = jnp.concatenate([lblk[:, 0:1], lblk[:, HEAD_DIM:HEAD_DIM + 1]], axis=0)
                dcol = jnp.concatenate([dblk[:, 0:1], dblk[:, HEAD_DIM:HEAD_DIM + 1]], axis=0)
                p = jnp.exp2(sbuf[i % PIPE_SLOTS] - lcol)
                pbuf[i % PIPE_SLOTS] = p.astype(MXU_DTYPE)
                dsbuf[i % PIPE_SLOTS] = (p * (dpbuf[i % PIPE_SLOTS] - dcol)).astype(MXU_DTYPE)

            def grads(i):
                q0, k0, _ = offsets(i)
                qs = _stack_heads(qp[pl.ds(q0, QBLK), :], lane_lo)
                dos = _stack_heads(dop[pl.ds(q0, QBLK), :], lane_lo)
                ds = dsbuf[i % PIPE_SLOTS]
                dvacc[i % 2, pl.ds(k0, 2 * QBLK), :] = _dot_tn(pbuf[i % PIPE_SLOTS], dos)
                dkacc[i % 2, pl.ds(k0, 2 * QBLK), :] = _dot_tn(ds, qs)
                dqp[pl.ds(q0, QBLK), :] = _merge_heads(_dot(ds, kp[pl.ds(k0, 2 * QBLK), :]), lane_lo)

            _pipeline(NB, [scores, probs, grads], PIPE_UNROLL)
            _scatter_rows(dqp, dqn, S, d, pad=False, accumulate=True)
            _scatter_parity(dkacc, dkn, S, d)
            _scatter_parity(dvacc, dv_ref, S, d)

        gq_sum = jnp.zeros((8, LANES), F32)
        gk_sum = jnp.zeros((8, LANES), F32)
        for c0 in range(0, S, ROWCH):
            for src_ref, dn, g_ref, dst_ref, scale, is_q in ((q_ref, dqn, gq_ref, dq_ref, QSCALE, True),
                                                             (k_ref, dkn, gk_ref, dk_ref, LN2, False)):
                x = src_ref[c0:c0 + ROWCH, :]
                dh = dn[c0:c0 + ROWCH, :]
                rr = lax.rsqrt(_head_mean(x * x, mm) + EPS)
                e = dh * (g_ref[...] * scale)
                dst_ref[c0:c0 + ROWCH, :] = rr * e - x * (rr * rr * rr) * _head_mean(e * x, mm)
                gpart = dh * (x * rr * scale)
                acc8 = gpart[0:8, :]
                for q8 in range(1, ROWCH // 8):
                    acc8 = acc8 + gpart[8 * q8:8 * q8 + 8, :]
                if is_q:
                    gq_sum = gq_sum + acc8
                else:
                    gk_sum = gk_sum + acc8
        gacc_ref[0:1, :] += jnp.sum(gq_sum, axis=0, keepdims=True)
        gacc_ref[1:2, :] += jnp.sum(gk_sum, axis=0, keepdims=True)

        if nh:
            @pl.when(step == n_steps - 1)
            def _():
                plan.finish(hin, hout, sems)

    spec_in = lambda off: pl.BlockSpec((None, S, LANES), lambda b, j: (off * NS + j, b, 0))
    tok = pl.BlockSpec((S, LANES), lambda b, j: (b, j))
    vec = pl.BlockSpec((1, LANES), lambda b, j: (0, 0))
    slab_out = pl.BlockSpec((None, S, LANES), lambda b, j: (j, b, 0))
    f32buf = lambda rows: pltpu.VMEM((rows, LANES), F32)
    bfbuf = lambda rows: pltpu.VMEM((rows, LANES), MXU_DTYPE)
    anyspec = pl.BlockSpec(memory_space=pl.ANY)
    return pl.pallas_call(
        body, grid=(Bl, NS), name="attn_bwd",
        in_specs=[spec_in(0), spec_in(1), spec_in(0), spec_in(1), spec_in(2), tok, tok, tok, vec, vec,
                  pl.BlockSpec((None, 8, LANES), lambda b, j: (j, 0, 0))] + [anyspec] * nh,
        out_specs=[slab_out, slab_out, slab_out, pl.BlockSpec((8, LANES), lambda b, j: (0, 0))] + [anyspec] * nh,
        out_shape=[jax.ShapeDtypeStruct((NS, T, LANES), F32)] * 3 + [jax.ShapeDtypeStruct((8, LANES), F32)]
                  + [jax.ShapeDtypeStruct((3,) + h.shape[1:], h.dtype) for h in hosted],
        scratch_shapes=[f32buf(S),
                        bfbuf(S), bfbuf(PADR), bfbuf(PADR), bfbuf(S),
                        f32buf(S), f32buf(S), f32buf(S),
                        pltpu.VMEM((2, PADR, LANES), F32), pltpu.VMEM((2, PADR, LANES), F32),
                        f32buf(S), f32buf(S),
                        pltpu.VMEM((N_EDGE * len(DILATIONS), 2 * QBLK, 2 * QBLK), F32),
                        pltpu.VMEM((PIPE_SLOTS, 2 * QBLK, 2 * QBLK), F32),
                        pltpu.VMEM((PIPE_SLOTS, 2 * QBLK, 2 * QBLK), F32),
                        pltpu.VMEM((PIPE_SLOTS, 2 * QBLK, 2 * QBLK), MXU_DTYPE),
                        pltpu.VMEM((PIPE_SLOTS, 2 * QBLK, 2 * QBLK), MXU_DTYPE)]
                       + (plan.scratch() if nh else []),
        compiler_params=_cp(("arbitrary", "arbitrary")),
    )(qkh, qkh, qkv, qkv, qkv, o, lse, do, gq2, gk2, slopes, *hosted)


def _layer_norm_parts(cv, g_ln, b_ln):
    mu = jnp.mean(cv, axis=-1, keepdims=True)
    cen = cv - mu
    rs = lax.rsqrt(jnp.mean(cen * cen, axis=-1, keepdims=True) + EPS)
    z = cen * rs
    return z, rs, z * g_ln + b_ln


def _ffn_fwd(x2, cv, ya, tgt, mod, g_ln, b_ln, g_ffn, w_out, w_gate, w_up, w_down, *, S, tm):
    T, D = x2.shape
    DC = cv.shape[1]
    P, Kb, _ = w_out.shape
    Fb = w_down.shape[1]
    tps = S // tm

    def body(x_ref, cv_ref, ya_ref, t_ref, mod_ref, gln_ref, bln_ref, gf_ref, wo_hbm, wg_hbm, wu_hbm, wd_hbm,
             x1_ref, ycat_ref, mix_ref, h2_ref, g_ref, u_ref, a_ref, f_ref, dy_ref, loss_ref,
             wo, wg, wu, wd, sems):
        i = pl.program_id(0)
        _load_resident(i, [(wo_hbm, wo), (wg_hbm, wg), (wu_hbm, wu), (wd_hbm, wd)], sems)

        @pl.when(i == 0)
        def _():
            loss_ref[...] = jnp.zeros_like(loss_ref)

        _, _, ln = _layer_norm_parts(cv_ref[...], gln_ref[...], bln_ref[...])
        yc = ln * _sigmoid(ln)
        ycat = jnp.concatenate([yc, ya_ref[...]], axis=1).astype(MXU_DTYPE)
        ycat_ref[...] = ycat
        mix = _dot(ycat[:, 0:Kb], wo[0])
        for p in range(1, P):
            mix = mix + _dot(ycat[:, Kb * p:Kb * (p + 1)], wo[p])
        mix_ref[...] = mix.astype(ACT_DTYPE)
        x1 = x_ref[...] + mod_ref[:, 2 * D:3 * D] * mix
        x1_ref[...] = x1
        r2 = lax.rsqrt(jnp.mean(x1 * x1, axis=-1, keepdims=True) + EPS)
        h2 = (x1 * r2 * gf_ref[...]) * (1.0 + mod_ref[:, 4 * D:5 * D]) + mod_ref[:, 3 * D:4 * D]
        h2b = h2.astype(MXU_DTYPE)
        h2_ref[...] = h2b
        f = jnp.zeros((tm, D), F32)
        for p in range(P):
            g = _dot_nt(h2b, wg[p])
            u = _dot_nt(h2b, wu[p])
            a = (g * _sigmoid(g) * u).astype(MXU_DTYPE)
            g_ref[p] = g.astype(ACT_DTYPE)
            u_ref[p] = u.astype(ACT_DTYPE)
            a_ref[p] = a
            f = f + _dot(a, wd[p])
        f_ref[...] = f.astype(ACT_DTYPE)
        err = x1 + mod_ref[:, 5 * D:6 * D] * f - t_ref[...]
        dy_ref[...] = err * (1.0 / D)
        tot = jnp.sum(_colsum(err * err), axis=1, keepdims=True)
        loss_ref[...] += tot * (0.5 / D)

    row = lambda w: pl.BlockSpec((tm, w), lambda i: (i, 0))
    vec = lambda w: pl.BlockSpec((1, w), lambda i: (0, 0))
    blk = pl.BlockSpec((P, tm, Fb), lambda i: (0, i, 0))
    anyspec = pl.BlockSpec(memory_space=pl.ANY)
    return pl.pallas_call(
        body, grid=(T // tm,), name="ffn_fwd",
        in_specs=[row(D), row(DC), row(D - DC), row(D),
                  pl.BlockSpec((None, 1, N_MOD * D), lambda i: (i // tps, 0, 0)),
                  vec(DC), vec(DC), vec(D), anyspec, anyspec, anyspec, anyspec],
        out_specs=[row(D), row(D), row(D), row(D), blk, blk, blk, row(D), row(D),
                   pl.BlockSpec((8, LANES), lambda i: (0, 0))],
        out_shape=[jax.ShapeDtypeStruct((T, D), F32), jax.ShapeDtypeStruct((T, D), MXU_DTYPE),
                   jax.ShapeDtypeStruct((T, D), ACT_DTYPE), jax.ShapeDtypeStruct((T, D), MXU_DTYPE),
                   jax.ShapeDtypeStruct((P, T, Fb), ACT_DTYPE), jax.ShapeDtypeStruct((P, T, Fb), ACT_DTYPE),
                   jax.ShapeDtypeStruct((P, T, Fb), MXU_DTYPE), jax.ShapeDtypeStruct((T, D), ACT_DTYPE),
                   jax.ShapeDtypeStruct((T, D), F32), jax.ShapeDtypeStruct((8, LANES), F32)],
        scratch_shapes=[pltpu.VMEM(w_out.shape, w_out.dtype), pltpu.VMEM(w_gate.shape, w_gate.dtype),
                        pltpu.VMEM(w_up.shape, w_up.dtype), pltpu.VMEM(w_down.shape, w_down.dtype),
                        pltpu.SemaphoreType.DMA((4,))],
        compiler_params=_cp(("arbitrary",)),
    )(x2, cv, ya, tgt, mod, g_ln, b_ln, g_ffn, w_out, w_gate, w_up, w_down)


def _ffn_bwd(dy, x1, gs, us, fo, mixb, cv, mod, g_ln, b_ln, g_ffn, w_out, w_gate, w_up, w_down, *, S, tm):
    T, D = dy.shape
    DC = cv.shape[1]
    P, Kb, _ = w_out.shape
    Fb = w_down.shape[1]
    tps = S // tm
    Bl = T // S

    def body(dy_ref, x1_ref, g_ref, u_ref, f_ref, mix_ref, cv_ref, mod_ref, gln_ref, bln_ref, gf_ref,
             wo_hbm, wg_hbm, wu_hbm, wd_hbm,
             dg_ref, du_ref, df_ref, dx1_ref, dmix_ref, dya_ref, dcv_ref, macc_ref, gacc_ref, lacc_ref,
             wo, wg, wu, wd, sems):
        i = pl.program_id(0)
        _load_resident(i, [(wo_hbm, wo), (wg_hbm, wg), (wu_hbm, wu), (wd_hbm, wd)], sems)

        @pl.when(i == 0)
        def _():
            gacc_ref[...] = jnp.zeros_like(gacc_ref)
            lacc_ref[...] = jnp.zeros_like(lacc_ref)

        @pl.when(i % tps == 0)
        def _():
            macc_ref[...] = jnp.zeros_like(macc_ref)

        dy_t = dy_ref[...]
        x1 = x1_ref[...]
        gate_f = mod_ref[:, 5 * D:6 * D]
        macc_ref[2:3, :] += _colsum(dy_t * f_ref[...].astype(F32))
        dfb = (dy_t * gate_f).astype(MXU_DTYPE)
        df_ref[...] = dfb
        dh2 = jnp.zeros((tm, D), F32)
        for p in range(P):
            da = _dot_nt(dfb, wd[p])
            g = g_ref[p].astype(F32)
            u = u_ref[p].astype(F32)
            sg = _sigmoid(g)
            dgp = (da * u * (sg * (1.0 + g * (1.0 - sg)))).astype(MXU_DTYPE)
            dup = (da * (g * sg)).astype(MXU_DTYPE)
            dg_ref[p] = dgp
            du_ref[p] = dup
            dh2 = dh2 + _dot(dgp, wg[p]) + _dot(dup, wu[p])
        r2 = lax.rsqrt(jnp.mean(x1 * x1, axis=-1, keepdims=True) + EPS)
        xr = x1 * r2
        n2 = xr * gf_ref[...]
        macc_ref[0:1, :] += _colsum(dh2)
        macc_ref[1:2, :] += _colsum(dh2 * n2)
        dn2 = dh2 * (1.0 + mod_ref[:, 4 * D:5 * D])
        gacc_ref[0:1, :] += _colsum(dn2 * xr)
        e = dn2 * gf_ref[...]
        dx1 = dy_t + r2 * e - xr * (r2 * jnp.mean(e * xr, axis=-1, keepdims=True))
        dx1_ref[...] = dx1
        macc_ref[3:4, :] += _colsum(dx1 * mix_ref[...].astype(F32))
        dmixb = (dx1 * mod_ref[:, 2 * D:3 * D]).astype(MXU_DTYPE)
        dmix_ref[...] = dmixb
        parts = [_dot_nt(dmixb, wo[p]) for p in range(P)]
        dycat = jnp.concatenate(parts, axis=1) if P > 1 else parts[0]
        dya_ref[...] = dycat[:, DC:]
        dyc = dycat[:, :DC]
        z, rs, ln = _layer_norm_parts(cv_ref[...], gln_ref[...], bln_ref[...])
        sg = _sigmoid(ln)
        dln = dyc * (sg * (1.0 + ln * (1.0 - sg)))
        lacc_ref[0:1, :] += _colsum(dln * z)
        lacc_ref[1:2, :] += _colsum(dln)
        dz = dln * gln_ref[...]
        dcv_ref[...] = rs * (dz - jnp.mean(dz, axis=-1, keepdims=True) - z * jnp.mean(dz * z, axis=-1, keepdims=True))

    row = lambda w: pl.BlockSpec((tm, w), lambda i: (i, 0))
    vec = lambda w: pl.BlockSpec((1, w), lambda i: (0, 0))
    blk = pl.BlockSpec((P, tm, Fb), lambda i: (0, i, 0))
    anyspec = pl.BlockSpec(memory_space=pl.ANY)
    return pl.pallas_call(
        body, grid=(T // tm,), name="ffn_bwd",
        in_specs=[row(D), row(D), blk, blk, row(D), row(D), row(DC),
                  pl.BlockSpec((None, 1, N_MOD * D), lambda i: (i // tps, 0, 0)),
                  vec(DC), vec(DC), vec(D), anyspec, anyspec, anyspec, anyspec],
        out_specs=[blk, blk, row(D), row(D), row(D), row(D - DC), row(DC),
                   pl.BlockSpec((None, 8, D), lambda i: (i // tps, 0, 0)),
                   pl.BlockSpec((8, D), lambda i: (0, 0)), pl.BlockSpec((8, DC), lambda i: (0, 0))],
        out_shape=[jax.ShapeDtypeStruct((P, T, Fb), MXU_DTYPE), jax.ShapeDtypeStruct((P, T, Fb), MXU_DTYPE),
                   jax.ShapeDtypeStruct((T, D), MXU_DTYPE), jax.ShapeDtypeStruct((T, D), F32),
                   jax.ShapeDtypeStruct((T, D), MXU_DTYPE), jax.ShapeDtypeStruct((T, D - DC), F32),
                   jax.ShapeDtypeStruct((T, DC), F32), jax.ShapeDtypeStruct((Bl, 8, D), F32),
                   jax.ShapeDtypeStruct((8, D), F32), jax.ShapeDtypeStruct((8, DC), F32)],
        scratch_shapes=[pltpu.VMEM(w_out.shape, w_out.dtype), pltpu.VMEM(w_gate.shape, w_gate.dtype),
                        pltpu.VMEM(w_up.shape, w_up.dtype), pltpu.VMEM(w_down.shape, w_down.dtype),
                        pltpu.SemaphoreType.DMA((4,))],
        compiler_params=_cp(("arbitrary",)),
    )(dy, x1, gs, us, fo, mixb, cv, mod, g_ln, b_ln, g_ffn, w_out, w_gate, w_up, w_down)


def _in_bwd(da, dg, dq, dk, dv, x2, dx1, mod, g_mix, w_in, *, S, tm):
    T, D = x2.shape
    P, _, Nb = w_in.shape
    DC = da.shape[1]
    NS = dq.shape[0]
    n_in = P * Nb
    tps = S // tm
    Bl = T // S

    def body(da_ref, dg_ref, dq_ref, dk_ref, dv_ref, x_ref, dx1_ref, mod_ref, g_ref, w_ref,
             dx_ref, dproj_ref, macc_ref, gacc_ref):
        i = pl.program_id(0)

        @pl.when(i == 0)
        def _():
            gacc_ref[...] = jnp.zeros_like(gacc_ref)

        @pl.when(i % tps == 0)
        def _():
            macc_ref[...] = jnp.zeros_like(macc_ref)

        pieces = [da_ref[...], dg_ref[...]] + [r[j] for r in (dq_ref, dk_ref, dv_ref) for j in range(NS)]
        dproj = jnp.concatenate(pieces, axis=1).astype(MXU_DTYPE)
        dproj_ref[...] = dproj
        dh = _dot_nt(dproj[:, 0:Nb], w_ref[0])
        for p in range(1, P):
            dh = dh + _dot_nt(dproj[:, Nb * p:Nb * (p + 1)], w_ref[p])
        x = x_ref[...]
        r = lax.rsqrt(jnp.mean(x * x, axis=-1, keepdims=True) + EPS)
        xr = x * r
        macc_ref[0:1, :] += _colsum(dh)
        macc_ref[1:2, :] += _colsum(dh * (xr * g_ref[...]))
        dn = dh * (1.0 + mod_ref[:, D:2 * D])
        gacc_ref[0:1, :] += _colsum(dn * xr)
        e = dn * g_ref[...]
        dx_ref[...] = dx1_ref[...] + r * e - xr * (r * jnp.mean(e * xr, axis=-1, keepdims=True))

    row = lambda w: pl.BlockSpec((tm, w), lambda i: (i, 0))
    slab = pl.BlockSpec((NS, tm, LANES), lambda i: (0, i, 0))
    return pl.pallas_call(
        body, grid=(T // tm,), name="in_bwd",
        in_specs=[row(DC), row(DC), slab, slab, slab, row(D), row(D),
                  pl.BlockSpec((None, 1, N_MOD * D), lambda i: (i // tps, 0, 0)),
                  pl.BlockSpec((1, D), lambda i: (0, 0)),
                  pl.BlockSpec((P, D, Nb), lambda i: (0, 0, 0))],
        out_specs=[row(D), row(n_in), pl.BlockSpec((None, 8, D), lambda i: (i // tps, 0, 0)),
                   pl.BlockSpec((8, D), lambda i: (0, 0))],
        out_shape=[jax.ShapeDtypeStruct((T, D), F32), jax.ShapeDtypeStruct((T, n_in), MXU_DTYPE),
                   jax.ShapeDtypeStruct((Bl, 8, D), F32), jax.ShapeDtypeStruct((8, D), F32)],
        compiler_params=_cp(("arbitrary",)),
    )(da, dg, dq, dk, dv, x2, dx1, mod, g_mix, w_in)


def _wgrad(a, b, *, P, name, tk, split=None):
    a_blk, b_blk = a.ndim == 3, b.ndim == 3
    T = a.shape[-2]
    if a_blk:
        R, C = a.shape[2], b.shape[1]
        a_of = lambda av, p: av[p]
        b_of = lambda bv, p: bv[...]
    elif b_blk:
        R, C = a.shape[1], b.shape[2]
        a_of = lambda av, p: av[...]
        b_of = lambda bv, p: bv[p]
    elif split == "a":
        R, C = a.shape[1] // P, b.shape[1]
        a_of = lambda av, p: av[:, R * p:R * (p + 1)]
        b_of = lambda bv, p: bv[...]
    else:
        R, C = a.shape[1], b.shape[1] // P
        a_of = lambda av, p: av[...]
        b_of = lambda bv, p: bv[:, C * p:C * (p + 1)]

    def body(a_ref, b_ref, o_ref):
        @pl.when(pl.program_id(0) == 0)
        def _():
            o_ref[...] = jnp.zeros_like(o_ref)
        for p in range(P):
            o_ref[p] += _dot_tn(a_of(a_ref, p), b_of(b_ref, p))

    def spec(v):
        if v.ndim == 3:
            return pl.BlockSpec((P, tk, v.shape[2]), lambda k: (0, k, 0))
        return pl.BlockSpec((tk, v.shape[1]), lambda k: (k, 0))

    return pl.pallas_call(
        body, grid=(T // tk,), name=name,
        in_specs=[spec(a), spec(b)],
        out_specs=pl.BlockSpec((P, R, C), lambda k: (0, 0, 0)),
        out_shape=jax.ShapeDtypeStruct((P, R, C), F32),
        compiler_params=_cp(("arbitrary",)),
    )(a, b)


TM_IN = 512
TM_FFN = 256
TK_WGRAD = 512


def _alibi_slabs(n_slab):
    heads = 2 * n_slab
    slopes = 2.0 ** (-8.0 * np.arange(1, heads + 1) / heads)
    return jnp.asarray(np.broadcast_to(np.repeat(slopes.reshape(n_slab, 1, 2), HEAD_DIM, axis=2), (n_slab, 8, LANES)),
                       dtype=F32)


def _local_step(x, tgt, mod, g_mix, wdw, g_ln, b_ln, g_q, g_k, g_ffn, w_in, w_out, w_gate, w_up, w_down,
                gather_in_attention=False, early_partials=None):
    Bl, S, D = x.shape
    T = Bl * S
    DC = g_ln.shape[1]
    P = w_in.shape[0]
    n_slab = (D - DC) // LANES
    x2 = x.reshape(T, D)
    t2 = tgt.reshape(T, D)
    mod3 = mod.reshape(Bl, 1, N_MOD * D)
    gq2 = jnp.tile(g_q, (1, LANES // HEAD_DIM))
    gk2 = jnp.tile(g_k, (1, LANES // HEAD_DIM))
    slopes = _alibi_slabs(n_slab)

    ag, qkv, qkh, h1 = _fwd_in(x2, mod3, g_mix, gq2, gk2, w_in, S=S, tm=TM_IN, n_ag=2 * DC)
    cv = _conv_fwd(ag, wdw, Bl=Bl, S=S, DC=DC)
    if gather_in_attention:
        ya, lse, w_out, w_gate, w_up, w_down = _attn_fwd(qkh, qkv, slopes, Bl=Bl, S=S,
                                                         hosted=(w_out, w_gate, w_up, w_down))
    else:
        ya, lse = _attn_fwd(qkh, qkv, slopes, Bl=Bl, S=S)
    x1, ycat, mixb, h2, gs, us, acts, fo, dy, lossb = _ffn_fwd(
        x2, cv, ya, t2, mod3, g_ln, b_ln, g_ffn, w_out, w_gate, w_up, w_down, S=S, tm=TM_FFN)
    dgs, dus, dfb, dx1, dmixb, dya, dcv, macc_f, gacc_f, lacc = _ffn_bwd(
        dy, x1, gs, us, fo, mixb, cv, mod3, g_ln, b_ln, g_ffn, w_out, w_gate, w_up, w_down, S=S, tm=TM_FFN)
    grads = dict(
        w_down=_wgrad(acts, dfb, P=P, name="wgrad_down", tk=TK_WGRAD),
        w_gate=_wgrad(dgs, h2, P=P, name="wgrad_gate", tk=TK_WGRAD),
        w_up=_wgrad(dus, h2, P=P, name="wgrad_up", tk=TK_WGRAD),
        w_out=_wgrad(ycat, dmixb, P=P, name="wgrad_out", tk=TK_WGRAD, split="a"),
    )
    out = {}
    if early_partials is not None:
        sums = early_partials([grads[nm] for nm in EARLY_WEIGHTS])
        res = _attn_bwd(qkh, qkv, ya, lse, dya, gq2, gk2, slopes, Bl=Bl, S=S, hosted=tuple(sb for _, sb in sums))
        dq, dk, dv, gqk = res[:4]
        out["early_sums"] = [s32 for s32, _ in sums]
        out["early_recv"] = list(res[4:])
    else:
        dq, dk, dv, gqk = _attn_bwd(qkh, qkv, ya, lse, dya, gq2, gk2, slopes, Bl=Bl, S=S)
    da, dg, dwdw = _conv_bwd(ag, dcv, wdw, Bl=Bl, S=S, DC=DC)
    dx, dprojb, macc_m, gacc_m = _in_bwd(da, dg, dq, dk, dv, x2, dx1, mod3, g_mix, w_in, S=S, tm=TM_IN)
    grads["w_in"] = _wgrad(h1, dprojb, P=P, name="wgrad_in", tk=TK_WGRAD, split="b")
    packed = _pack_small(macc_m, macc_f, gacc_m, gacc_f, lacc, gqk, dwdw, lossb)
    out.update(dx=dx.reshape(Bl, S, D), grads=grads, packed=packed)
    return out


EARLY_WEIGHTS = ("w_down", "w_gate", "w_up", "w_out")


def _small_layout(Bl):
    return 8 * Bl, 8 * Bl + 8, 8 * Bl + 8 + CONV_ROWS


def _pack_small(macc_m, macc_f, gacc_m, gacc_f, lacc, gqk, dwdw, lossb):
    Bl, _, D = macc_m.shape
    DC = lacc.shape[1]
    assert 2 * DC <= D
    SMALL_GAIN_ROW, SMALL_TAP_ROW, SMALL_ROWS = _small_layout(Bl)

    def body(mm_ref, mf_ref, gm_ref, gf_ref, la_ref, qk_ref, dw_ref, loss_ref, o_ref):
        o_ref[...] = jnp.zeros_like(o_ref)
        for b in range(Bl):
            o_ref[8 * b + 0:8 * b + 2, :] = mm_ref[b, 0:2, :]
            o_ref[8 * b + 2:8 * b + 3, :] = mf_ref[b, 3:4, :]
            o_ref[8 * b + 3:8 * b + 6, :] = mf_ref[b, 0:3, :]
        r = SMALL_GAIN_ROW
        o_ref[r:r + 1, :] = gm_ref[0:1, :]
        o_ref[r + 1:r + 2, :] = gf_ref[0:1, :]
        o_ref[r + 2:r + 3, 0:DC] = la_ref[0:1, :]
        o_ref[r + 2:r + 3, DC:2 * DC] = la_ref[1:2, :]
        qk = qk_ref[0:2, 0:HEAD_DIM] + qk_ref[0:2, HEAD_DIM:2 * HEAD_DIM]
        o_ref[r + 3:r + 4, 0:HEAD_DIM] = qk[0:1, :]
        o_ref[r + 3:r + 4, HEAD_DIM:2 * HEAD_DIM] = qk[1:2, :]
        o_ref[r + 4:r + 5, 0:LANES] = loss_ref[0:1, :]
        o_ref[SMALL_TAP_ROW:SMALL_TAP_ROW + CONV_ROWS, 0:DC] = dw_ref[...]

    return pl.pallas_call(body, name="pack_small", out_shape=jax.ShapeDtypeStruct((SMALL_ROWS, D), F32),
                          compiler_params=_cp())(macc_m, macc_f, gacc_m, gacc_f, lacc, gqk, dwdw, lossb)


def _row_tile(rows, cap=512):
    if rows <= cap:
        return rows
    best = rows
    for t in range(8, cap + 1, 8):
        if rows % t == 0:
            best = t
    return best


def _cast_weight(w, pidx, name):
    def body(p_ref, w_ref, o_ref):
        o_ref[...] = w_ref[...].astype(MXU_DTYPE)
    R, C = w.shape
    tr = _row_tile(R)
    return pl.pallas_call(
        body, name=name,
        grid_spec=pltpu.PrefetchScalarGridSpec(
            num_scalar_prefetch=1, grid=(R // tr,),
            in_specs=[pl.BlockSpec((tr, C), lambda i, p: (i, 0))],
            out_specs=pl.BlockSpec((None, tr, C), lambda i, p: (p[0], i, 0))),
        out_shape=jax.ShapeDtypeStruct((4, R, C), MXU_DTYPE),
    )(pidx, w)


def _pair_add(g, recv, cidx, name):
    P, R, C = g.shape
    R2 = R // 2

    def body(c_ref, g_ref, r_ref, o_ref, ob_ref):
        s = g_ref[...] + r_ref[...]
        o_ref[...] = s
        ob_ref[...] = s.astype(jnp.bfloat16)

    return pl.pallas_call(
        body, name=name,
        grid_spec=pltpu.PrefetchScalarGridSpec(
            num_scalar_prefetch=1, grid=(P,),
            in_specs=[pl.BlockSpec((None, R2, C), lambda p, c: (p, c[0], 0)),
                      pl.BlockSpec((None, R2, C), lambda p, c: (p, 0, 0))],
            out_specs=[pl.BlockSpec((None, R2, C), lambda p, c: (p, 0, 0)),
                       pl.BlockSpec((None, R2, C), lambda p, c: (p, 0, 0))]),
        out_shape=[jax.ShapeDtypeStruct((P, R2, C), F32), jax.ShapeDtypeStruct((P, R2, C), jnp.bfloat16)],
    )(cidx, g, recv)


def _final_add(chipsum, recv, pc_idx, name):
    P, R2, C = chipsum.shape

    def body(pc_ref, s_ref, r_ref, o_ref):
        acc = s_ref[...]
        for k in range(3):
            acc = acc + r_ref[k].astype(F32)
        o_ref[...] = acc

    return pl.pallas_call(
        body, name=name,
        grid_spec=pltpu.PrefetchScalarGridSpec(
            num_scalar_prefetch=1, grid=(1,),
            in_specs=[pl.BlockSpec((None, R2, C), lambda i, pc: (pc[0], 0, 0)),
                      pl.BlockSpec((3, R2, C), lambda i, pc: (0, 0, 0))],
            out_specs=pl.BlockSpec((R2, C), lambda i, pc: (pc[1], 0))),
        out_shape=jax.ShapeDtypeStruct((2 * R2, C), F32),
    )(pc_idx, chipsum, recv)


def _adamw(w, g, m, v, name):
    R, C = w.shape
    tr = _row_tile(R, 256)
    c1 = 1.0 - ADAM_B1 ** ADAM_STEP
    c2 = 1.0 - ADAM_B2 ** ADAM_STEP

    def body(w_ref, g_ref, m_ref, v_ref, d_ref, nm_ref, nv_ref):
        gg = g_ref[...]
        nm = ADAM_B1 * m_ref[...] + (1.0 - ADAM_B1) * gg
        nv = ADAM_B2 * v_ref[...] + (1.0 - ADAM_B2) * (gg * gg)
        nm_ref[...] = nm
        nv_ref[...] = nv
        d_ref[...] = -ADAM_LR * ((nm / c1) / (jnp.sqrt(nv / c2) + ADAM_EPS) + ADAM_WD * w_ref[...])

    spec = pl.BlockSpec((tr, C), lambda i: (i, 0))
    return pl.pallas_call(
        body, grid=(R // tr,), name=name,
        in_specs=[spec] * 4, out_specs=[spec] * 3,
        out_shape=[jax.ShapeDtypeStruct((R, C), F32)] * 3,
    )(w, g, m, v)


def _ada_fwd(c_all, w_ada, b_cols):
    def body(c_ref, w_ref, b_ref, o_ref):
        c = c_ref[...]
        o_ref[...] = jnp.dot(c * _sigmoid(c), w_ref[...], preferred_element_type=F32, precision=HIGHEST) + b_ref[...]
    return pl.pallas_call(
        body, name="ada_fwd", out_shape=jax.ShapeDtypeStruct((c_all.shape[0], w_ada.shape[1]), F32),
        compiler_params=_cp(),
    )(c_all, w_ada, b_cols)


def _ada_bwd(c_all, dmod_cols):
    def body(c_ref, d_ref, o_ref):
        c = c_ref[...]
        o_ref[...] = lax.dot_general(c * _sigmoid(c), d_ref[...], (((0,), (0,)), ((), ())),
                                     preferred_element_type=F32, precision=HIGHEST)
    return pl.pallas_call(
        body, name="ada_bwd", out_shape=jax.ShapeDtypeStruct((c_all.shape[1], dmod_cols.shape[1]), F32),
        compiler_params=_cp(),
    )(c_all, dmod_cols)


def _small_reduce(gathered, n_dev, Bl):
    mod_rows, _, rows = _small_layout(Bl)
    width = gathered.shape[1]

    def body(g_ref, red_ref, bada_ref):
        acc = g_ref[0:rows, :]
        for d in range(1, n_dev):
            acc = acc + g_ref[d * rows:(d + 1) * rows, :]
        red_ref[...] = acc[mod_rows:, :]
        b = acc[0:8, :]
        for q in range(1, Bl):
            b = b + acc[8 * q:8 * q + 8, :]
        bada_ref[...] = b
    return pl.pallas_call(
        body, name="small_reduce",
        out_shape=[jax.ShapeDtypeStruct((rows - mod_rows, width), F32), jax.ShapeDtypeStruct((8, width), F32)],
        compiler_params=_cp(),
    )(gathered)


def _mesh_pos():
    return lax.axis_index("x"), lax.axis_index("y"), lax.axis_index("c")


def _other_chips(x, y):
    return [(1 - x, y), (x, 1 - y), (1 - x, 1 - y)]


def _allgather8(xs, name):
    m_per, n = xs.shape

    def body(x_ref, out_ref, send_sems, recv_sems, local_sem):
        x, y, c = _mesh_pos()
        me, sibling = (x, y, c), (x, y, 1 - c)
        chips = _other_chips(x, y)

        def rows(px, py, pc):
            return out_ref.at[pl.ds((4 * px + 2 * py + pc) * m_per, m_per), :]

        def copy(k, block, to, src=None):
            return pltpu.make_async_remote_copy(
                src_ref=rows(*block) if src is None else src, dst_ref=rows(*block),
                send_sem=send_sems.at[k], recv_sem=recv_sems.at[k], device_id=to, device_id_type=MESH_DEV)

        mine = pltpu.make_async_copy(x_ref, rows(*me), local_sem)
        mine.start()
        first = [copy(0, me, sibling, src=x_ref)]
        first += [copy(1 + j, me, (*chip, c), src=x_ref) for j, chip in enumerate(chips)]
        for cp in first:
            cp.start()
        passed = [copy(4 + j, (*chip, c), sibling) for j, chip in enumerate(chips)]
        for j, chip in enumerate(chips):
            copy(1 + j, (*chip, c), me).wait_recv()
            passed[j].start()
        copy(0, sibling, me).wait_recv()
        for j, chip in enumerate(chips):
            copy(4 + j, (*chip, 1 - c), me).wait_recv()
        for cp in first + passed:
            cp.wait_send()
        mine.wait()

    return pl.pallas_call(
        body, name=name, out_shape=jax.ShapeDtypeStruct((8 * m_per, n), xs.dtype),
        in_specs=[pl.BlockSpec(memory_space=pltpu.VMEM)], out_specs=pl.BlockSpec(memory_space=pltpu.VMEM),
        scratch_shapes=[pltpu.SemaphoreType.DMA((7,)), pltpu.SemaphoreType.DMA((7,)), pltpu.SemaphoreType.DMA],
        compiler_params=_cp(),
    )(xs)


class _WeightGather:
    def __init__(self, shapes):
        self.shapes = shapes
        self.n = len(shapes)

    def scratch(self):
        return [pltpu.SemaphoreType.DMA((6 * self.n,)), pltpu.SemaphoreType.DMA((6 * self.n,))]

    def _copy(self, outs, sems, w, k, slot, h, to):
        r2 = self.shapes[w][1] // 2
        blk = outs[w].at[slot, pl.ds(h * r2, r2), :]
        return pltpu.make_async_remote_copy(
            src_ref=blk, dst_ref=blk, send_sem=sems[0].at[6 * w + k], recv_sem=sems[1].at[6 * w + k],
            device_id=to, device_id_type=MESH_DEV)

    def start(self, outs, sems):
        x, y, c = _mesh_pos()
        for w in range(self.n):
            for k, chip in enumerate(_other_chips(x, y)):
                self._copy(outs, sems, w, k, 2 * x + y, c, (*chip, c)).start()

    def forward(self, outs, sems):
        x, y, c = _mesh_pos()
        for w in range(self.n):
            for k, chip in enumerate(_other_chips(x, y)):
                slot = 2 * chip[0] + chip[1]
                self._copy(outs, sems, w, k, slot, c, (x, y, 1 - c)).wait_recv()
                self._copy(outs, sems, w, 3 + k, slot, c, (x, y, 1 - c)).start()

    def finish(self, outs, sems):
        x, y, c = _mesh_pos()
        for w in range(self.n):
            for k, chip in enumerate(_other_chips(x, y)):
                slot = 2 * chip[0] + chip[1]
                self._copy(outs, sems, w, 3 + k, slot, 1 - c, (x, y, 1 - c)).wait_recv()
                self._copy(outs, sems, w, k, 2 * x + y, c, (*chip, c)).wait_send()
                self._copy(outs, sems, w, 3 + k, slot, c, (x, y, 1 - c)).wait_send()


def _gather_weights(bufs, name):
    n = len(bufs)
    plan = _WeightGather([b.shape for b in bufs])

    def body(*refs):
        outs = refs[n:2 * n]
        sems = refs[2 * n:]
        plan.start(outs, sems)
        plan.forward(outs, sems)
        plan.finish(outs, sems)

    anyspec = pl.BlockSpec(memory_space=pl.ANY)
    return pl.pallas_call(
        body, name=name,
        out_shape=[jax.ShapeDtypeStruct(b.shape, b.dtype) for b in bufs],
        in_specs=[anyspec] * n, out_specs=[anyspec] * n,
        input_output_aliases={w: w for w in range(n)},
        scratch_shapes=plan.scratch(),
    )(*bufs)


def _rs_sibling(grads, name):
    n = len(grads)

    def body(*refs):
        ins, outs = refs[:n], refs[n:2 * n]
        send_sems, recv_sems = refs[2 * n:]
        x, y, c = _mesh_pos()
        cps = []
        for w in range(n):
            P, R, _ = grads[w].shape
            r2 = R // 2
            for p in range(P):
                cps.append(pltpu.make_async_remote_copy(
                    src_ref=ins[w].at[p, pl.ds((1 - c) * r2, r2), :], dst_ref=outs[w].at[p],
                    send_sem=send_sems.at[4 * w + p], recv_sem=recv_sems.at[4 * w + p],
                    device_id=(x, y, 1 - c), device_id_type=MESH_DEV))
                cps[-1].start()
        for cp in cps:
            cp.wait()

    anyspec = pl.BlockSpec(memory_space=pl.ANY)
    return pl.pallas_call(
        body, name=name,
        out_shape=[jax.ShapeDtypeStruct((g.shape[0], g.shape[1] // 2, g.shape[2]), g.dtype) for g in grads],
        in_specs=[anyspec] * n, out_specs=[anyspec] * n,
        scratch_shapes=[pltpu.SemaphoreType.DMA((4 * n,)), pltpu.SemaphoreType.DMA((4 * n,))],
    )(*grads)


class _ChipExchange:
    def __init__(self, n):
        self.n = n

    def scratch(self):
        return [pltpu.SemaphoreType.DMA((3 * self.n,)), pltpu.SemaphoreType.DMA((3 * self.n,))]

    def _copies(self, ins, outs, sems):
        x, y, c = _mesh_pos()
        return [pltpu.make_async_remote_copy(
            src_ref=ins[w].at[2 * chip[0] + chip[1]], dst_ref=outs[w].at[k],
            send_sem=sems[0].at[3 * w + k], recv_sem=sems[1].at[3 * w + k],
            device_id=(*chip, c), device_id_type=MESH_DEV)
            for w in range(self.n) for k, chip in enumerate(_other_chips(x, y))]

    def start(self, ins, outs, sems):
        for cp in self._copies(ins, outs, sems):
            cp.start()

    def finish(self, ins, outs, sems):
        for cp in self._copies(ins, outs, sems):
            cp.wait()


def _rs_chips(sums, name):
    n = len(sums)
    plan = _ChipExchange(n)

    def body(*refs):
        ins, outs, sems = refs[:n], refs[n:2 * n], refs[2 * n:]
        plan.start(ins, outs, sems)
        plan.finish(ins, outs, sems)

    anyspec = pl.BlockSpec(memory_space=pl.ANY)
    return pl.pallas_call(
        body, name=name,
        out_shape=[jax.ShapeDtypeStruct((3,) + s.shape[1:], s.dtype) for s in sums],
        in_specs=[anyspec] * n, out_specs=[anyspec] * n,
        scratch_shapes=plan.scratch(),
    )(*sums)


def _rs_final(bufs):
    n = len(bufs)

    def body(*refs):
        outs = refs[n:2 * n]
        send_sems, recv_sems = refs[2 * n:]
        x, y, c = _mesh_pos()
        cps = []
        for w in range(n):
            r2 = bufs[w].shape[0] // 2
            mine = outs[w].at[pl.ds(c * r2, r2), :]
            cps.append(pltpu.make_async_remote_copy(
                src_ref=mine, dst_ref=mine, send_sem=send_sems.at[w], recv_sem=recv_sems.at[w],
                device_id=(x, y, 1 - c), device_id_type=MESH_DEV))
            cps[-1].start()
        for cp in cps:
            cp.wait()

    anyspec = pl.BlockSpec(memory_space=pl.ANY)
    return pl.pallas_call(
        body, name="rs_final",
        out_shape=[jax.ShapeDtypeStruct(b.shape, b.dtype) for b in bufs],
        in_specs=[anyspec] * n, out_specs=[anyspec] * n,
        input_output_aliases={w: w for w in range(n)},
        scratch_shapes=[pltpu.SemaphoreType.DMA((n,)), pltpu.SemaphoreType.DMA((n,))],
    )(*bufs)


BIG = ("w_in", "w_out", "w_gate", "w_up", "w_down")
TRANSPOSED = ("w_gate", "w_up")
WEIGHTS = ("w_ada", "b_ada", "g_mix", "w_in", "w_dw", "b_dw", "g_conv_ln", "b_conv_ln", "g_q", "g_k",
           "w_out", "g_ffn", "w_gate", "w_up", "w_down")


def _pad_to(a, rows, cols):
    return jnp.pad(a, ((0, rows - a.shape[0]), (0, cols - a.shape[1])))


def kernel(x, c, w_ada, b_ada, g_mix, w_in, w_dw, b_dw, g_conv_ln, b_conv_ln, g_q, g_k, w_out, g_ffn, w_gate, w_up, w_down, loss_target, m_w_ada, m_b_ada, m_g_mix, m_w_in, m_w_dw, m_b_dw, m_g_conv_ln, m_b_conv_ln, m_g_q, m_g_k, m_w_out, m_g_ffn, m_w_gate, m_w_up, m_w_down, v_w_ada, v_b_ada, v_g_mix, v_w_in, v_w_dw, v_b_dw, v_g_conv_ln, v_b_conv_ln, v_g_q, v_g_k, v_w_out, v_g_ffn, v_w_gate, v_w_up, v_w_down):
    w = dict(w_ada=w_ada, b_ada=b_ada, g_mix=g_mix, w_in=w_in, w_dw=w_dw, b_dw=b_dw, g_conv_ln=g_conv_ln,
             b_conv_ln=b_conv_ln, g_q=g_q, g_k=g_k, w_out=w_out, g_ffn=g_ffn, w_gate=w_gate, w_up=w_up, w_down=w_down)
    m = dict(w_ada=m_w_ada, b_ada=m_b_ada, g_mix=m_g_mix, w_in=m_w_in, w_dw=m_w_dw, b_dw=m_b_dw, g_conv_ln=m_g_conv_ln,
             b_conv_ln=m_b_conv_ln, g_q=m_g_q, g_k=m_g_k, w_out=m_w_out, g_ffn=m_g_ffn, w_gate=m_w_gate, w_up=m_w_up,
             w_down=m_w_down)
    v = dict(w_ada=v_w_ada, b_ada=v_b_ada, g_mix=v_g_mix, w_in=v_w_in, w_dw=v_w_dw, b_dw=v_b_dw, g_conv_ln=v_g_conv_ln,
             b_conv_ln=v_b_conv_ln, g_q=v_g_q, g_k=v_g_k, w_out=v_w_out, g_ffn=v_g_ffn, w_gate=v_w_gate, w_up=v_w_up,
             w_down=v_w_down)
    Bl, S, D = x.shape
    DC = g_conv_ln.shape[1]
    NA = w_ada.shape[2]
    xi, yi, ci = _mesh_pos()
    p = 2 * xi + yi
    dev = 2 * p + ci
    n_dev = 8
    cidx = jnp.reshape(ci, (1,)).astype(jnp.int32)
    pidx = jnp.reshape(p, (1,)).astype(jnp.int32)

    first = jnp.concatenate([_pad_to(c, 8, D), _pad_to(w_dw[0], CONV_ROWS, D)], axis=0)
    g0 = _allgather8(first, "gather_cond").reshape(n_dev, 8 + CONV_ROWS, D)
    c_all = g0[:, :Bl].reshape(n_dev * Bl, D)
    taps = jnp.concatenate([g0[2 * q, 8:, :w_dw.shape[2]] for q in range(4)], axis=1)
    wdw = jnp.where(lax.broadcasted_iota(jnp.int32, taps.shape, 0) == CONV_WIDTH, b_dw, taps)
    shard = lambda a, nm: a[0].T if nm in TRANSPOSED else a[0]
    owned = {nm: _cast_weight(shard(w[nm], nm), pidx, "cast_" + nm) for nm in BIG}
    (w_in_full,) = _gather_weights([owned["w_in"]], "gather_w_in")

    b_cols = lax.dynamic_slice_in_dim(b_ada, p * NA, NA, axis=1)
    mod_part = _ada_fwd(c_all, w_ada[0], b_cols)
    gm = _allgather8(mod_part, "gather_mod").reshape(n_dev, n_dev * Bl, NA)
    mod = jnp.concatenate([lax.dynamic_slice_in_dim(gm[2 * q], dev * Bl, Bl, axis=0) for q in range(4)], axis=1)

    def chip_partials(parts, names, tag):
        from_sib = _rs_sibling(parts, "rs_sibling_" + tag)
        return [_pair_add(g, r, cidx, "pair_add_" + nm) for nm, g, r in zip(names, parts, from_sib)]

    loc = _local_step(x, loss_target, mod, g_mix, wdw, g_conv_ln, b_conv_ln, g_q, g_k, g_ffn,
                      w_in_full, owned["w_out"], owned["w_gate"], owned["w_up"], owned["w_down"],
                      gather_in_attention=True,
                      early_partials=lambda parts: chip_partials(parts, EARLY_WEIGHTS, "early"))

    late = ("w_in",)
    late_sums = chip_partials([loc["grads"][nm] for nm in late], late, "late")
    late_recv = _rs_chips([sb for _, sb in late_sums], "rs_chips_late")
    pc_idx = jnp.stack([p, ci]).astype(jnp.int32)
    order = EARLY_WEIGHTS + late
    sums32 = loc["early_sums"] + [s32 for s32, _ in late_sums]
    recv = loc["early_recv"] + list(late_recv)
    halves = [_final_add(s32, r, pc_idx, "final_add_" + nm) for nm, s32, r in zip(order, sums32, recv)]
    grad = dict(zip(order, _rs_final(halves)))

    mod_rows, _, small_rows = _small_layout(Bl)
    gs = _allgather8(loc["packed"], "gather_small")
    red, bada8 = _small_reduce(gs, n_dev, Bl)
    dmod_all = gs.reshape(n_dev, small_rows, D)[:, :mod_rows].reshape(n_dev * Bl, 8, D)[:, :N_MOD].reshape(n_dev * Bl, N_MOD * D)
    grad["w_ada"] = _ada_bwd(c_all, lax.dynamic_slice_in_dim(dmod_all, p * NA, NA, axis=1))
    grad["b_ada"] = bada8[:N_MOD].reshape(1, N_MOD * D)
    grad["g_mix"] = red[0:1]
    grad["g_ffn"] = red[1:2]
    grad["g_conv_ln"] = red[2:3, :DC]
    grad["b_conv_ln"] = red[2:3, DC:2 * DC]
    grad["g_q"] = red[3:4, :HEAD_DIM]
    grad["g_k"] = red[3:4, HEAD_DIM:2 * HEAD_DIM]
    loss = red[4, 0]
    dwdw = red[8:8 + CONV_ROWS, :DC]
    grad["w_dw"] = lax.dynamic_slice_in_dim(dwdw[:CONV_WIDTH], p * w_dw.shape[2], w_dw.shape[2], axis=1)
    grad["b_dw"] = dwdw[CONV_WIDTH:CONV_WIDTH + 1]

    delta, new_m, new_v = {}, {}, {}
    for nm in WEIGHTS:
        shp = w[nm].shape
        if nm in TRANSPOSED:
            d_, m_, v_ = _adamw(w[nm][0].T, grad[nm], m[nm][0].T, v[nm][0].T, "adamw_" + nm)
            grad[nm], delta[nm], new_m[nm], new_v[nm] = (a.T.reshape(shp) for a in (grad[nm], d_, m_, v_))
            continue
        two_d = (shp[-2], shp[-1]) if len(shp) == 3 else shp
        d_, m_, v_ = _adamw(w[nm].reshape(two_d), grad[nm].reshape(two_d), m[nm].reshape(two_d), v[nm].reshape(two_d),
                            "adamw_" + nm)
        grad[nm] = grad[nm].reshape(shp)
        delta[nm], new_m[nm], new_v[nm] = d_.reshape(shp), m_.reshape(shp), v_.reshape(shp)

    return (loss, loc["dx"], *[grad[nm] for nm in WEIGHTS], *[delta[nm] for nm in WEIGHTS],
            *[new_m[nm] for nm in WEIGHTS], *[new_v[nm] for nm in WEIGHTS])
```

```python
import functools
import math

import jax
import jax.numpy as jnp
import numpy as np
from jax import lax
from jax.experimental import pallas as pl
from jax.experimental.pallas import tpu as pltpu

F32 = jnp.float32
MXU_DTYPE = jnp.bfloat16
ACT_DTYPE = jnp.bfloat16
EPS = 1e-6
NEG_INF = -1e30
HEAD_DIM = 64
LANES = 128
RADIUS = 64
QBLK = 128
DILATIONS = (1, 4, 16)
CONV_WIDTH = 31
CONV_PAD = CONV_WIDTH // 2
CONV_ROWS = 32
N_MOD = 6
ADAM_LR, ADAM_B1, ADAM_B2, ADAM_EPS, ADAM_WD, ADAM_STEP = 0.001, 0.9, 0.999, 1e-08, 0.01, 10
HIGHEST = lax.Precision.HIGHEST
MESH_DEV = pl.DeviceIdType.MESH
VMEM_LIMIT = 56 << 20


def _cp(sem=None, vmem=VMEM_LIMIT):
    kw = dict(vmem_limit_bytes=vmem)
    if sem is not None:
        kw["dimension_semantics"] = sem
    return pltpu.CompilerParams(**kw)


def _sigmoid(x):
    return 1.0 / (1.0 + jnp.exp(-x))


def _dot(a, b):
    return jnp.dot(a, b, preferred_element_type=F32)


def _dot_nt(a, b):
    return lax.dot_general(a, b, (((1,), (1,)), ((), ())), preferred_element_type=F32)


def _dot_tn(a, b):
    return lax.dot_general(a, b, (((0,), (0,)), ((), ())), preferred_element_type=F32)


def _colsum(v):
    return jnp.sum(v, axis=0, keepdims=True)


def _load_resident(i, pairs, sems):
    @pl.when(i == 0)
    def _():
        cps = [pltpu.make_async_copy(src, dst, sems.at[n]) for n, (src, dst) in enumerate(pairs)]
        for c in cps:
            c.start()
        for c in cps:
            c.wait()


def _fwd_in(x2, mod, g_mix, gq2, gk2, w_in, *, S, tm, n_ag):
    T, D = x2.shape
    P, _, Nb = w_in.shape
    n_in = P * Nb
    n_slab = (n_in - n_ag) // LANES
    NS = n_slab // 3
    tps = S // tm

    def body(x_ref, mod_ref, g_ref, gq_ref, gk_ref, w_ref, ag_ref, qkv_ref, qkh_ref, h_ref):
        x = x_ref[...]
        r = lax.rsqrt(jnp.mean(x * x, axis=-1, keepdims=True) + EPS)
        n = x * r * g_ref[...]
        h = n * (1.0 + mod_ref[:, D:2 * D]) + mod_ref[:, 0:D]
        hb = h.astype(MXU_DTYPE)
        h_ref[...] = hb
        parts = [_dot(hb, w_ref[p]) for p in range(P)]
        proj = jnp.concatenate(parts, axis=1) if P > 1 else parts[0]
        ag_ref[...] = proj[:, :n_ag]
        mm = _head_mean_matrix()
        for j in range(n_slab):
            v = proj[:, n_ag + LANES * j:n_ag + LANES * (j + 1)]
            qkv_ref[j] = v
            if j < 2 * NS:
                gain = gq_ref[...] * (HEAD_DIM ** -0.5 * LOG2E) if j < NS else gk_ref[...]
                qkh_ref[j] = v * lax.rsqrt(_head_mean(v * v, mm) + EPS) * gain

    return pl.pallas_call(
        body, grid=(T // tm,), name="fwd_in",
        in_specs=[pl.BlockSpec((tm, D), lambda i: (i, 0)),
                  pl.BlockSpec((None, 1, N_MOD * D), lambda i: (i // tps, 0, 0)),
                  pl.BlockSpec((1, D), lambda i: (0, 0)),
                  pl.BlockSpec((1, LANES), lambda i: (0, 0)), pl.BlockSpec((1, LANES), lambda i: (0, 0)),
                  pl.BlockSpec((P, D, Nb), lambda i: (0, 0, 0))],
        out_specs=[pl.BlockSpec((tm, n_ag), lambda i: (i, 0)),
                   pl.BlockSpec((n_slab, tm, LANES), lambda i: (0, i, 0)),
                   pl.BlockSpec((2 * NS, tm, LANES), lambda i: (0, i, 0)),
                   pl.BlockSpec((tm, D), lambda i: (i, 0))],
        out_shape=[jax.ShapeDtypeStruct((T, n_ag), F32),
                   jax.ShapeDtypeStruct((n_slab, T, LANES), F32),
                   jax.ShapeDtypeStruct((2 * NS, T, LANES), F32),
                   jax.ShapeDtypeStruct((T, D), MXU_DTYPE)],
        compiler_params=_cp(("arbitrary",)),
    )(x2, mod, g_mix, gq2, gk2, w_in)


CONV_CH = 64


def _conv_taps(win, w_ref, acc, reverse):
    n = win.shape[0]
    for b in range(8):
        wb = win if b == 0 else pltpu.roll(win, shift=n - b, axis=0)
        for a in range(4):
            o = 8 * a + b
            if o < 1 or o > CONV_WIDTH:
                continue
            k = (CONV_WIDTH - o) if reverse else (o - 1)
            acc = acc + w_ref[k:k + 1, :] * wb[8 * a:8 * a + CONV_CH, :]
    return acc


def _conv_fwd(ag, wdw, *, Bl, S, DC):
    T = ag.shape[0]
    nsc = DC // LANES
    CH = CONV_CH

    def body(a_ref, g_ref, w_ref, cv_ref, upad):
        zeros16 = jnp.zeros((16, LANES), F32)
        upad[0:16, :] = zeros16
        upad[S + 16:S + 32, :] = zeros16

        def fill(i, _):
            r0 = pl.multiple_of(i * CH, CH)
            a = a_ref[pl.ds(r0, CH), :]
            g = g_ref[pl.ds(r0, CH), :]
            upad[pl.ds(r0 + 16, CH), :] = a * _sigmoid(g)
            return 0
        lax.fori_loop(0, S // CH, fill, 0)

        def conv(i, _):
            r0 = pl.multiple_of(i * CH, CH)
            win = upad[pl.ds(r0, CH + 32), :]
            acc = jnp.zeros((CH, LANES), F32) + w_ref[CONV_WIDTH:CONV_WIDTH + 1, :]
            cv_ref[pl.ds(r0, CH), :] = _conv_taps(win, w_ref, acc, reverse=False)
            return 0
        lax.fori_loop(0, S // CH, conv, 0)

    return pl.pallas_call(
        body, grid=(Bl, nsc), name="conv_fwd",
        in_specs=[pl.BlockSpec((S, LANES), lambda b, j: (b, j)),
                  pl.BlockSpec((S, LANES), lambda b, j: (b, nsc + j)),
                  pl.BlockSpec((CONV_ROWS, LANES), lambda b, j: (0, j))],
        out_specs=pl.BlockSpec((S, LANES), lambda b, j: (b, j)),
        out_shape=jax.ShapeDtypeStruct((T, DC), F32),
        scratch_shapes=[pltpu.VMEM((S + 32, LANES), F32)],
        compiler_params=_cp(("arbitrary", "arbitrary")),
    )(ag, ag, wdw)


def _conv_bwd(ag, dcv, wdw, *, Bl, S, DC):
    T = ag.shape[0]
    nsc = DC // LANES
    CH = CONV_CH

    def body(a_ref, g_ref, d_ref, w_ref, da_ref, dg_ref, dw_ref, upad, dpad, wacc):
        b = pl.program_id(1)
        zeros16 = jnp.zeros((16, LANES), F32)
        upad[0:16, :] = zeros16
        upad[S + 16:S + 32, :] = zeros16
        dpad[0:16, :] = zeros16
        dpad[S + 16:S + 32, :] = zeros16

        @pl.when(b == 0)
        def _():
            wacc[...] = jnp.zeros_like(wacc)

        def fill(i, _):
            r0 = pl.multiple_of(i * CH, CH)
            a = a_ref[pl.ds(r0, CH), :]
            g = g_ref[pl.ds(r0, CH), :]
            upad[pl.ds(r0 + 16, CH), :] = a * _sigmoid(g)
            dpad[pl.ds(r0 + 16, CH), :] = d_ref[pl.ds(r0, CH), :]
            return 0
        lax.fori_loop(0, S // CH, fill, 0)

        def step(i, _):
            r0 = pl.multiple_of(i * CH, CH)
            dwin = dpad[pl.ds(r0, CH + 32), :]
            du = _conv_taps(dwin, w_ref, jnp.zeros((CH, LANES), F32), reverse=True)
            a = a_ref[pl.ds(r0, CH), :]
            g = g_ref[pl.ds(r0, CH), :]
            sg = _sigmoid(g)
            da_ref[pl.ds(r0, CH), :] = du * sg
            dg_ref[pl.ds(r0, CH), :] = du * a * sg * (1.0 - sg)
            dc = d_ref[pl.ds(r0, CH), :]
            uwin = upad[pl.ds(r0, CH + 32), :]
            n = CH + 32
            for bb in range(8):
                wb = uwin if bb == 0 else pltpu.roll(uwin, shift=n - bb, axis=0)
                for aa in range(4):
                    o = 8 * aa + bb
                    if o < 1 or o > CONV_WIDTH:
                        continue
                    k = o - 1
                    prod = dc * wb[8 * aa:8 * aa + CH, :]
                    part = prod[0:8, :]
                    for q in range(1, CH // 8):
                        part = part + prod[8 * q:8 * q + 8, :]
                    wacc[8 * k:8 * k + 8, :] += part
            part = dc[0:8, :]
            for q in range(1, CH // 8):
                part = part + dc[8 * q:8 * q + 8, :]
            wacc[8 * CONV_WIDTH:8 * CONV_WIDTH + 8, :] += part
            return 0
        lax.fori_loop(0, S // CH, step, 0)

        @pl.when(b == Bl - 1)
        def _():
            for k in range(CONV_ROWS):
                dw_ref[k:k + 1, :] = jnp.sum(wacc[8 * k:8 * k + 8, :], axis=0, keepdims=True)

    return pl.pallas_call(
        body, grid=(nsc, Bl), name="conv_bwd",
        in_specs=[pl.BlockSpec((S, LANES), lambda j, b: (b, j)),
                  pl.BlockSpec((S, LANES), lambda j, b: (b, nsc + j)),
                  pl.BlockSpec((S, LANES), lambda j, b: (b, j)),
                  pl.BlockSpec((CONV_ROWS, LANES), lambda j, b: (0, j))],
        out_specs=[pl.BlockSpec((S, LANES), lambda j, b: (b, j)),
                   pl.BlockSpec((S, LANES), lambda j, b: (b, j)),
                   pl.BlockSpec((CONV_ROWS, LANES), lambda j, b: (0, j))],
        out_shape=[jax.ShapeDtypeStruct((T, DC), F32), jax.ShapeDtypeStruct((T, DC), F32),
                   jax.ShapeDtypeStruct((CONV_ROWS, DC), F32)],
        scratch_shapes=[pltpu.VMEM((S + 32, LANES), F32), pltpu.VMEM((S + 32, LANES), F32),
                        pltpu.VMEM((8 * CONV_ROWS, LANES), F32)],
        compiler_params=_cp(("arbitrary", "arbitrary")),
    )(ag, ag, dcv, wdw)


ROWCH = 256


LOG2E = 1.4426950408889634
LN2 = 0.6931471805599453
N_EDGE = 4


def _head_mean_matrix():
    r = lax.broadcasted_iota(jnp.int32, (LANES, LANES), 0) // HEAD_DIM
    c = lax.broadcasted_iota(jnp.int32, (LANES, LANES), 1) // HEAD_DIM
    return jnp.where(r == c, 1.0 / HEAD_DIM, 0.0).astype(jnp.bfloat16)


def _head_mean(v, mm):
    hi = v.astype(jnp.bfloat16)
    lo = (v - hi.astype(F32)).astype(jnp.bfloat16)
    return _dot(hi, mm) + _dot(lo, mm)


def _stack_heads(blk, lane_lo):
    z = jnp.zeros_like(blk)
    return jnp.concatenate([jnp.where(lane_lo, blk, z), jnp.where(lane_lo, z, blk)], axis=0)


def _merge_heads(v2, lane_lo):
    return jnp.where(lane_lo, v2[:QBLK], v2[QBLK:])


def _bias_tables(bias_ref, slope_ref):
    row = lax.broadcasted_iota(jnp.int32, (2 * QBLK, 2 * QBLK), 0)
    col = lax.broadcasted_iota(jnp.int32, (2 * QBLK, 2 * QBLK), 1)
    rel = jnp.abs(col - RADIUS - (row % QBLK))
    slope = jnp.where(row < QBLK, slope_ref[0:1, 0:1], slope_ref[0:1, HEAD_DIM:HEAD_DIM + 1]) * LOG2E
    for pi, d in enumerate(DILATIONS):
        inside = jnp.where(rel <= RADIUS, -slope * (float(d) * rel.astype(F32)), NEG_INF)
        for e in range(N_EDGE):
            t = inside
            if e & 1:
                t = jnp.where(col < RADIUS, NEG_INF, t)
            if e & 2:
                t = jnp.where(col >= QBLK + RADIUS, NEG_INF, t)
            bias_ref[N_EDGE * pi + e] = t


def _edge_index(qb, nb):
    return jnp.where(qb == 0, 1, 0) + jnp.where(qb == nb - 1, 2, 0)


def _gather_rows(src_ref, dst_ref, S, d, pad):
    n = S // d
    seg = n + 2 * RADIUS if pad else n
    step = min(n, 512)
    for r in range(d):
        base = r * seg
        if pad:
            dst_ref[base:base + RADIUS, :] = jnp.zeros((RADIUS, LANES), dst_ref.dtype)
            dst_ref[base + RADIUS + n:base + seg, :] = jnp.zeros((RADIUS, LANES), dst_ref.dtype)
            base += RADIUS
        for c0 in range(0, n, step):
            if d == 1:
                v = src_ref[c0:c0 + step, :]
            else:
                v = src_ref[pl.ds(r + c0 * d, step, stride=d), :]
            dst_ref[base + c0:base + c0 + step, :] = v.astype(dst_ref.dtype)


def _scatter_rows(src_ref, dst_ref, S, d, pad, accumulate):
    n = S // d
    seg = n + 2 * RADIUS if pad else n
    step = min(n, 512)
    for r in range(d):
        base = r * seg + (RADIUS if pad else 0)
        for c0 in range(0, n, step):
            v = src_ref[base + c0:base + c0 + step, :]
            if d == 1:
                idx = pl.ds(c0, step)
            else:
                idx = pl.ds(r + c0 * d, step, stride=d)
            if accumulate:
                dst_ref[idx, :] = dst_ref[idx, :] + v
            else:
                dst_ref[idx, :] = v


def _zero_uncovered(acc, S, d):
    n = S // d
    if (n // QBLK) % 2:
        return
    seg = n + 2 * RADIUS
    for r in range(d):
        acc[0, r * seg + n:r * seg + seg, :] = jnp.zeros((2 * RADIUS, LANES), F32)
        acc[1, r * seg:r * seg + 2 * RADIUS, :] = jnp.zeros((2 * RADIUS, LANES), F32)


def _scatter_parity(acc, dst_ref, S, d):
    n = S // d
    seg = n + 2 * RADIUS
    step = min(n, 512)
    one_block = (n // QBLK) % 2 == 1
    for r in range(d):
        base = r * seg + RADIUS
        for c0 in range(0, n, step):
            rows = slice(base + c0, base + c0 + step)
            v = acc[r % 2, rows, :] if one_block else acc[0, rows, :] + acc[1, rows, :]
            idx = pl.ds(c0, step) if d == 1 else pl.ds(r + c0 * d, step, stride=d)
            dst_ref[idx, :] = dst_ref[idx, :] + v


PIPE_UNROLL = 4
PIPE_SLOTS = 16
BWD_SLOTS = 12


def _pipeline(n_items, stages, unroll):
    K = len(stages)
    assert n_items % unroll == 0 and K * unroll <= (PIPE_SLOTS if K == 4 else BWD_SLOTS)
    trips = n_items // unroll
    assert trips >= K - 1

    def trip(t, static):
        for s in reversed(range(K)):
            if static and not 0 <= t - s < trips:
                continue
            for u in range(unroll):
                item = unroll * (t - s) + u
                stages[s](jnp.int32(item) if static else item)

    for t in range(K - 1):
        trip(t, True)

    def full(t, carry):
        trip(t, False)
        return carry
    lax.fori_loop(K - 1, trips, full, 0)
    for t in range(trips, trips + K - 1):
        trip(t, True)


def _attn_fwd(qkh, qkv, slopes, *, Bl, S, hosted=()):
    n3, T, _ = qkv.shape
    NS = n3 // 3
    NB = S // QBLK
    PADR = S + 2 * RADIUS * DILATIONS[-1]
    nh = len(hosted)
    plan = _WeightGather([b.shape for b in hosted]) if nh else None
    n_steps = Bl * NS

    def body(qh, kh, v_ref, slope_ref, *rest):
        o_ref, lse_ref = rest[nh:nh + 2]
        wouts = rest[nh + 2:2 * nh + 2]
        (qp, kp, vp, op, lp, onat, lnat, bias_ref, sbuf, pbuf, mbuf, lbuf) = rest[2 * nh + 2:2 * nh + 14]
        sems = rest[2 * nh + 14:]
        step = pl.program_id(0) * NS + pl.program_id(1)
        if nh:
            @pl.when(step == 0)
            def _():
                plan.start(wouts, sems)

            @pl.when(step == n_steps // 2)
            def _():
                plan.forward(wouts, sems)

        lane_lo = lax.broadcasted_iota(jnp.int32, (QBLK, LANES), 1) < HEAD_DIM
        _bias_tables(bias_ref, slope_ref)

        for pi, d in enumerate(DILATIONS):
            n = S // d
            nb = n // QBLK
            _gather_rows(qh, qp, S, d, pad=False)
            _gather_rows(kh, kp, S, d, pad=True)
            _gather_rows(v_ref, vp, S, d, pad=True)

            def offsets(i, nb=nb):
                r = i // nb
                return pl.multiple_of(i * QBLK, QBLK), pl.multiple_of((i + r) * QBLK, QBLK), i % nb

            def scores(i, pi=pi, nb=nb):
                q0, k0, qb = offsets(i)
                qs = _stack_heads(qp[pl.ds(q0, QBLK), :], lane_lo)
                sbuf[i % PIPE_SLOTS] = (_dot_nt(qs, kp[pl.ds(k0, 2 * QBLK), :])
                                        + bias_ref[N_EDGE * pi + _edge_index(qb, nb)])

            def rowmax(i):
                m = jnp.max(sbuf[i % PIPE_SLOTS], axis=1, keepdims=True)
                mbuf[i % PIPE_SLOTS] = jnp.broadcast_to(m, (2 * QBLK, LANES))

            def expsum(i):
                m = mbuf[i % PIPE_SLOTS]
                p = jnp.exp2(sbuf[i % PIPE_SLOTS] - jnp.concatenate([m, m], axis=1))
                pbuf[i % PIPE_SLOTS] = p.astype(MXU_DTYPE)
                lbuf[i % PIPE_SLOTS] = jnp.broadcast_to(jnp.sum(p, axis=1, keepdims=True), (2 * QBLK, LANES))

            def values(i):
                q0, k0, _ = offsets(i)
                l = lbuf[i % PIPE_SLOTS]
                o2 = _dot(pbuf[i % PIPE_SLOTS], vp[pl.ds(k0, 2 * QBLK), :]) * (1.0 / l)
                op[pl.ds(q0, QBLK), :] = _merge_heads(o2, lane_lo)
                lp[pl.ds(q0, QBLK), :] = _merge_heads(mbuf[i % PIPE_SLOTS] + jnp.log2(l), lane_lo)

            _pipeline(NB, [scores, rowmax, expsum, values], PIPE_UNROLL)
            _scatter_rows(op, onat.at[pi], S, d, pad=False, accumulate=False)
            _scatter_rows(lp, lnat.at[pi], S, d, pad=False, accumulate=False)

        for c0 in range(0, S, ROWCH):
            ls = [lnat[pi, c0:c0 + ROWCH, :] for pi in range(len(DILATIONS))]
            mx = jnp.maximum(jnp.maximum(ls[0], ls[1]), ls[2])
            es = [jnp.exp2(l - mx) for l in ls]
            tot = es[0] + es[1] + es[2]
            inv = 1.0 / tot
            acc = (es[0] * inv) * onat[0, c0:c0 + ROWCH, :]
            for pi in (1, 2):
                acc = acc + (es[pi] * inv) * onat[pi, c0:c0 + ROWCH, :]
            o_ref[c0:c0 + ROWCH, :] = acc
            lse_ref[c0:c0 + ROWCH, :] = mx + jnp.log2(tot)

        if nh:
            @pl.when(step == n_steps - 1)
            def _():
                plan.finish(wouts, sems)

    spec_in = lambda off: pl.BlockSpec((None, S, LANES), lambda b, j: (off * NS + j, b, 0))
    out = pl.BlockSpec((S, LANES), lambda b, j: (b, j))
    anyspec = pl.BlockSpec(memory_space=pl.ANY)
    return pl.pallas_call(
        body, grid=(Bl, NS), name="attn_fwd",
        in_specs=[spec_in(0), spec_in(1), spec_in(2),
                  pl.BlockSpec((None, 8, LANES), lambda b, j: (j, 0, 0))] + [anyspec] * nh,
        out_specs=[out, out] + [anyspec] * nh,
        out_shape=[jax.ShapeDtypeStruct((T, NS * LANES), F32)] * 2
                  + [jax.ShapeDtypeStruct(b.shape, b.dtype) for b in hosted],
        input_output_aliases={4 + w: 2 + w for w in range(nh)},
        scratch_shapes=[pltpu.VMEM((S, LANES), MXU_DTYPE), pltpu.VMEM((PADR, LANES), MXU_DTYPE),
                        pltpu.VMEM((PADR, LANES), MXU_DTYPE),
                        pltpu.VMEM((S, LANES), F32), pltpu.VMEM((S, LANES), F32),
                        pltpu.VMEM((3, S, LANES), F32), pltpu.VMEM((3, S, LANES), F32),
                        pltpu.VMEM((N_EDGE * len(DILATIONS), 2 * QBLK, 2 * QBLK), F32),
                        pltpu.VMEM((PIPE_SLOTS, 2 * QBLK, 2 * QBLK), F32),
                        pltpu.VMEM((PIPE_SLOTS, 2 * QBLK, 2 * QBLK), MXU_DTYPE),
                        pltpu.VMEM((PIPE_SLOTS, 2 * QBLK, LANES), F32), pltpu.VMEM((PIPE_SLOTS, 2 * QBLK, LANES), F32)]
                       + (plan.scratch() if nh else []),
        compiler_params=_cp(("arbitrary", "arbitrary")),
    )(qkh, qkh, qkv, slopes, *hosted)


def _attn_bwd(qkh, qkv, o, lse, do, gq2, gk2, slopes, *, Bl, S, hosted=()):
    n3, T, _ = qkv.shape
    NS = n3 // 3
    NB = S // QBLK
    PADR = S + 2 * RADIUS * DILATIONS[-1]
    QSCALE = HEAD_DIM ** -0.5
    nh = len(hosted)
    plan = _ChipExchange(nh)
    n_steps = Bl * NS

    def body(qh, kh, q_ref, k_ref, v_ref, o_ref, lse_ref, do_ref, gq_ref, gk_ref, slope_ref, *rest):
        hin = rest[:nh]
        dq_ref, dk_ref, dv_ref, gacc_ref = rest[nh:nh + 4]
        hout = rest[nh + 4:2 * nh + 4]
        (dl, qp, kp, vp, dop, lp, dlp, dqp, dkacc, dvacc, dqn, dkn, bias_ref,
         sbuf, dpbuf, pbuf, dsbuf) = rest[2 * nh + 4:2 * nh + 21]
        sems = rest[2 * nh + 21:]
        step = pl.program_id(0) * NS + pl.program_id(1)

        @pl.when(step == 0)
        def _():
            gacc_ref[...] = jnp.zeros_like(gacc_ref)
            if nh:
                plan.start(hin, hout, sems)

        mm = _head_mean_matrix()
        lane_lo = lax.broadcasted_iota(jnp.int32, (QBLK, LANES), 1) < HEAD_DIM
        _bias_tables(bias_ref, slope_ref)
        for c0 in range(0, S, ROWCH):
            dl[c0:c0 + ROWCH, :] = _head_mean(do_ref[c0:c0 + ROWCH, :] * o_ref[c0:c0 + ROWCH, :], mm) * HEAD_DIM
            dqn[c0:c0 + ROWCH, :] = jnp.zeros((ROWCH, LANES), F32)
            dkn[c0:c0 + ROWCH, :] = jnp.zeros((ROWCH, LANES), F32)
            dv_ref[c0:c0 + ROWCH, :] = jnp.zeros((ROWCH, LANES), F32)

        for pi, d in enumerate(DILATIONS):
            n = S // d
            nb = n // QBLK
            _gather_rows(qh, qp, S, d, pad=False)
            _gather_rows(kh, kp, S, d, pad=True)
            _gather_rows(v_ref, vp, S, d, pad=True)
            _gather_rows(do_ref, dop, S, d, pad=False)
            _gather_rows(lse_ref, lp, S, d, pad=False)
            _gather_rows(dl, dlp, S, d, pad=False)
            _zero_uncovered(dkacc, S, d)
            _zero_uncovered(dvacc, S, d)

            def offsets(i, nb=nb):
                r = i // nb
                return pl.multiple_of(i * QBLK, QBLK), pl.multiple_of((i + r) * QBLK, QBLK), i % nb

            def scores(i, pi=pi, nb=nb):
                q0, k0, qb = offsets(i)
                qs = _stack_heads(qp[pl.ds(q0, QBLK), :], lane_lo)
                dos = _stack_heads(dop[pl.ds(q0, QBLK), :], lane_lo)
                sbuf[i % BWD_SLOTS] = (_dot_nt(qs, kp[pl.ds(k0, 2 * QBLK), :])
                                       + bias_ref[N_EDGE * pi + _edge_index(qb, nb)])
                dpbuf[i % BWD_SLOTS] = _dot_nt(dos, vp[pl.ds(k0, 2 * QBLK), :])

            def probs(i):
                q0, _, _ = offsets(i)
                lblk = lp[pl.ds(q0, QBLK), :]
                dblk = dlp[pl.ds(q0, QBLK), :]
                lcol = jnp.concatenate([lblk[:, 0:1], lblk[:, HEAD_DIM:HEAD_DIM + 1]], axis=0)
                dcol = jnp.concatenate([dblk[:, 0:1], dblk[:, HEAD_DIM:HEAD_DIM + 1]], axis=0)
                p = jnp.exp2(sbuf[i % BWD_SLOTS] - lcol)
                pbuf[i % BWD_SLOTS] = p.astype(MXU_DTYPE)
                dsbuf[i % BWD_SLOTS] = (p * (dpbuf[i % BWD_SLOTS] - dcol)).astype(MXU_DTYPE)

            def grads(i):
                q0, k0, _ = offsets(i)
                qs = _stack_heads(qp[pl.ds(q0, QBLK), :], lane_lo)
                dos = _stack_heads(dop[pl.ds(q0, QBLK), :], lane_lo)
                ds = dsbuf[i % BWD_SLOTS]
                dvacc[i % 2, pl.ds(k0, 2 * QBLK), :] = _dot_tn(pbuf[i % BWD_SLOTS], dos)
                dkacc[i % 2, pl.ds(k0, 2 * QBLK), :] = _dot_tn(ds, qs)
                dqp[pl.ds(q0, QBLK), :] = _merge_heads(_dot(ds, kp[pl.ds(k0, 2 * QBLK), :]), lane_lo)

            _pipeline(NB, [scores, probs, grads], PIPE_UNROLL)
            _scatter_rows(dqp, dqn, S, d, pad=False, accumulate=True)
            _scatter_parity(dkacc, dkn, S, d)
            _scatter_parity(dvacc, dv_ref, S, d)

        gq_sum = jnp.zeros((8, LANES), F32)
        gk_sum = jnp.zeros((8, LANES), F32)
        for c0 in range(0, S, ROWCH):
            for src_ref, dn, g_ref, dst_ref, scale, is_q in ((q_ref, dqn, gq_ref, dq_ref, QSCALE, True),
                                                             (k_ref, dkn, gk_ref, dk_ref, LN2, False)):
                x = src_ref[c0:c0 + ROWCH, :]
                dh = dn[c0:c0 + ROWCH, :]
                rr = lax.rsqrt(_head_mean(x * x, mm) + EPS)
                e = dh * (g_ref[...] * scale)
                dst_ref[c0:c0 + ROWCH, :] = rr * e - x * (rr * rr * rr) * _head_mean(e * x, mm)
                gpart = dh * (x * rr * scale)
                acc8 = gpart[0:8, :]
                for q8 in range(1, ROWCH // 8):
                    acc8 = acc8 + gpart[8 * q8:8 * q8 + 8, :]
                if is_q:
                    gq_sum = gq_sum + acc8
                else:
                    gk_sum = gk_sum + acc8
        gacc_ref[0:1, :] += jnp.sum(gq_sum, axis=0, keepdims=True)
        gacc_ref[1:2, :] += jnp.sum(gk_sum, axis=0, keepdims=True)

        if nh:
            @pl.when(step == n_steps - 1)
            def _():
                plan.finish(hin, hout, sems)

    spec_in = lambda off: pl.BlockSpec((None, S, LANES), lambda b, j: (off * NS + j, b, 0))
    tok = pl.BlockSpec((S, LANES), lambda b, j: (b, j))
    vec = pl.BlockSpec((1, LANES), lambda b, j: (0, 0))
    slab_out = pl.BlockSpec((None, S, LANES), lambda b, j: (j, b, 0))
    f32buf = lambda rows: pltpu.VMEM((rows, LANES), F32)
    bfbuf = lambda rows: pltpu.VMEM((rows, LANES), MXU_DTYPE)
    anyspec = pl.BlockSpec(memory_space=pl.ANY)
    return pl.pallas_call(
        body, grid=(Bl, NS), name="attn_bwd",
        in_specs=[spec_in(0), spec_in(1), spec_in(0), spec_in(1), spec_in(2), tok, tok, tok, vec, vec,
                  pl.BlockSpec((None, 8, LANES), lambda b, j: (j, 0, 0))] + [anyspec] * nh,
        out_specs=[slab_out, slab_out, slab_out, pl.BlockSpec((8, LANES), lambda b, j: (0, 0))] + [anyspec] * nh,
        out_shape=[jax.ShapeDtypeStruct((NS, T, LANES), F32)] * 3 + [jax.ShapeDtypeStruct((8, LANES), F32)]
                  + [jax.ShapeDtypeStruct((3,) + h.shape[1:], h.dtype) for h in hosted],
        scratch_shapes=[f32buf(S),
                        bfbuf(S), bfbuf(PADR), bfbuf(PADR), bfbuf(S),
                        f32buf(S), f32buf(S), f32buf(S),
                        pltpu.VMEM((2, PADR, LANES), F32), pltpu.VMEM((2, PADR, LANES), F32),
                        f32buf(S), f32buf(S),
                        pltpu.VMEM((N_EDGE * len(DILATIONS), 2 * QBLK, 2 * QBLK), F32),
                        pltpu.VMEM((BWD_SLOTS, 2 * QBLK, 2 * QBLK), F32),
                        pltpu.VMEM((BWD_SLOTS, 2 * QBLK, 2 * QBLK), F32),
                        pltpu.VMEM((BWD_SLOTS, 2 * QBLK, 2 * QBLK), MXU_DTYPE),
                        pltpu.VMEM((BWD_SLOTS, 2 * QBLK, 2 * QBLK), MXU_DTYPE)]
                       + (plan.scratch() if nh else []),
        compiler_params=_cp(("arbitrary", "arbitrary")),
    )(qkh, qkh, qkv, qkv, qkv, o, lse, do, gq2, gk2, slopes, *hosted)


def _layer_norm_parts(cv, g_ln, b_ln):
    mu = jnp.mean(cv, axis=-1, keepdims=True)
    cen = cv - mu
    rs = lax.rsqrt(jnp.mean(cen * cen, axis=-1, keepdims=True) + EPS)
    z = cen * rs
    return z, rs, z * g_ln + b_ln


def _ffn_fwd(x2, cv, ya, tgt, mod, g_ln, b_ln, g_ffn, w_out, w_gate, w_up, w_down, *, S, tm):
    T, D = x2.shape
    DC = cv.shape[1]
    P, Kb, _ = w_out.shape
    Fb = w_down.shape[1]
    tps = S // tm

    def body(x_ref, cv_ref, ya_ref, t_ref, mod_ref, gln_ref, bln_ref, gf_ref, wo_hbm, wg_hbm, wu_hbm, wd_hbm,
             x1_ref, ycat_ref, mix_ref, h2_ref, g_ref, u_ref, a_ref, f_ref, dy_ref, loss_ref,
             wo, wg, wu, wd, sems):
        i = pl.program_id(0)
        _load_resident(i, [(wo_hbm, wo), (wg_hbm, wg), (wu_hbm, wu), (wd_hbm, wd)], sems)

        @pl.when(i == 0)
        def _():
            loss_ref[...] = jnp.zeros_like(loss_ref)

        _, _, ln = _layer_norm_parts(cv_ref[...], gln_ref[...], bln_ref[...])
        yc = ln * _sigmoid(ln)
        ycat = jnp.concatenate([yc, ya_ref[...]], axis=1).astype(MXU_DTYPE)
        ycat_ref[...] = ycat
        mix = _dot(ycat[:, 0:Kb], wo[0])
        for p in range(1, P):
            mix = mix + _dot(ycat[:, Kb * p:Kb * (p + 1)], wo[p])
        mix_ref[...] = mix.astype(ACT_DTYPE)
        x1 = x_ref[...] + mod_ref[:, 2 * D:3 * D] * mix
        x1_ref[...] = x1
        r2 = lax.rsqrt(jnp.mean(x1 * x1, axis=-1, keepdims=True) + EPS)
        h2 = (x1 * r2 * gf_ref[...]) * (1.0 + mod_ref[:, 4 * D:5 * D]) + mod_ref[:, 3 * D:4 * D]
        h2b = h2.astype(MXU_DTYPE)
        h2_ref[...] = h2b
        f = jnp.zeros((tm, D), F32)
        for p in range(P):
            g = _dot_nt(h2b, wg[p])
            u = _dot_nt(h2b, wu[p])
            a = (g * _sigmoid(g) * u).astype(MXU_DTYPE)
            g_ref[p] = g.astype(ACT_DTYPE)
            u_ref[p] = u.astype(ACT_DTYPE)
            a_ref[p] = a
            f = f + _dot(a, wd[p])
        f_ref[...] = f.astype(ACT_DTYPE)
        err = x1 + mod_ref[:, 5 * D:6 * D] * f - t_ref[...]
        dy_ref[...] = err * (1.0 / D)
        tot = jnp.sum(_colsum(err * err), axis=1, keepdims=True)
        loss_ref[...] += tot * (0.5 / D)

    row = lambda w: pl.BlockSpec((tm, w), lambda i: (i, 0))
    vec = lambda w: pl.BlockSpec((1, w), lambda i: (0, 0))
    blk = pl.BlockSpec((P, tm, Fb), lambda i: (0, i, 0))
    anyspec = pl.BlockSpec(memory_space=pl.ANY)
    return pl.pallas_call(
        body, grid=(T // tm,), name="ffn_fwd",
        in_specs=[row(D), row(DC), row(D - DC), row(D),
                  pl.BlockSpec((None, 1, N_MOD * D), lambda i: (i // tps, 0, 0)),
                  vec(DC), vec(DC), vec(D), anyspec, anyspec, anyspec, anyspec],
        out_specs=[row(D), row(D), row(D), row(D), blk, blk, blk, row(D), row(D),
                   pl.BlockSpec((8, LANES), lambda i: (0, 0))],
        out_shape=[jax.ShapeDtypeStruct((T, D), F32), jax.ShapeDtypeStruct((T, D), MXU_DTYPE),
                   jax.ShapeDtypeStruct((T, D), ACT_DTYPE), jax.ShapeDtypeStruct((T, D), MXU_DTYPE),
                   jax.ShapeDtypeStruct((P, T, Fb), ACT_DTYPE), jax.ShapeDtypeStruct((P, T, Fb), ACT_DTYPE),
                   jax.ShapeDtypeStruct((P, T, Fb), MXU_DTYPE), jax.ShapeDtypeStruct((T, D), ACT_DTYPE),
                   jax.ShapeDtypeStruct((T, D), F32), jax.ShapeDtypeStruct((8, LANES), F32)],
        scratch_shapes=[pltpu.VMEM(w_out.shape, w_out.dtype), pltpu.VMEM(w_gate.shape, w_gate.dtype),
                        pltpu.VMEM(w_up.shape, w_up.dtype), pltpu.VMEM(w_down.shape, w_down.dtype),
                        pltpu.SemaphoreType.DMA((4,))],
        compiler_params=_cp(("arbitrary",)),
    )(x2, cv, ya, tgt, mod, g_ln, b_ln, g_ffn, w_out, w_gate, w_up, w_down)


def _ffn_bwd(dy, x1, gs, us, fo, mixb, cv, mod, g_ln, b_ln, g_ffn, w_out, w_gate, w_up, w_down, *, S, tm):
    T, D = dy.shape
    DC = cv.shape[1]
    P, Kb, _ = w_out.shape
    Fb = w_down.shape[1]
    tps = S // tm
    Bl = T // S

    def body(dy_ref, x1_ref, g_ref, u_ref, f_ref, mix_ref, cv_ref, mod_ref, gln_ref, bln_ref, gf_ref,
             wo_hbm, wg_hbm, wu_hbm, wd_hbm,
             dg_ref, du_ref, df_ref, dx1_ref, dmix_ref, dya_ref, dcv_ref, macc_ref, gacc_ref, lacc_ref,
             wo, wg, wu, wd, sems):
        i = pl.program_id(0)
        _load_resident(i, [(wo_hbm, wo), (wg_hbm, wg), (wu_hbm, wu), (wd_hbm, wd)], sems)

        @pl.when(i == 0)
        def _():
            gacc_ref[...] = jnp.zeros_like(gacc_ref)
            lacc_ref[...] = jnp.zeros_like(lacc_ref)

        @pl.when(i % tps == 0)
        def _():
            macc_ref[...] = jnp.zeros_like(macc_ref)

        dy_t = dy_ref[...]
        x1 = x1_ref[...]
        gate_f = mod_ref[:, 5 * D:6 * D]
        macc_ref[2:3, :] += _colsum(dy_t * f_ref[...].astype(F32))
        dfb = (dy_t * gate_f).astype(MXU_DTYPE)
        df_ref[...] = dfb
        dh2 = jnp.zeros((tm, D), F32)
        for p in range(P):
            da = _dot_nt(dfb, wd[p])
            g = g_ref[p].astype(F32)
            u = u_ref[p].astype(F32)
            sg = _sigmoid(g)
            dgp = (da * u * (sg * (1.0 + g * (1.0 - sg)))).astype(MXU_DTYPE)
            dup = (da * (g * sg)).astype(MXU_DTYPE)
            dg_ref[p] = dgp
            du_ref[p] = dup
            dh2 = dh2 + _dot(dgp, wg[p]) + _dot(dup, wu[p])
        r2 = lax.rsqrt(jnp.mean(x1 * x1, axis=-1, keepdims=True) + EPS)
        xr = x1 * r2
        n2 = xr * gf_ref[...]
        macc_ref[0:1, :] += _colsum(dh2)
        macc_ref[1:2, :] += _colsum(dh2 * n2)
        dn2 = dh2 * (1.0 + mod_ref[:, 4 * D:5 * D])
        gacc_ref[0:1, :] += _colsum(dn2 * xr)
        e = dn2 * gf_ref[...]
        dx1 = dy_t + r2 * e - xr * (r2 * jnp.mean(e * xr, axis=-1, keepdims=True))
        dx1_ref[...] = dx1
        macc_ref[3:4, :] += _colsum(dx1 * mix_ref[...].astype(F32))
        dmixb = (dx1 * mod_ref[:, 2 * D:3 * D]).astype(MXU_DTYPE)
        dmix_ref[...] = dmixb
        parts = [_dot_nt(dmixb, wo[p]) for p in range(P)]
        dycat = jnp.concatenate(parts, axis=1) if P > 1 else parts[0]
        dya_ref[...] = dycat[:, DC:]
        dyc = dycat[:, :DC]
        z, rs, ln = _layer_norm_parts(cv_ref[...], gln_ref[...], bln_ref[...])
        sg = _sigmoid(ln)
        dln = dyc * (sg * (1.0 + ln * (1.0 - sg)))
        lacc_ref[0:1, :] += _colsum(dln * z)
        lacc_ref[1:2, :] += _colsum(dln)
        dz = dln * gln_ref[...]
        dcv_ref[...] = rs * (dz - jnp.mean(dz, axis=-1, keepdims=True) - z * jnp.mean(dz * z, axis=-1, keepdims=True))

    row = lambda w: pl.BlockSpec((tm, w), lambda i: (i, 0))
    vec = lambda w: pl.BlockSpec((1, w), lambda i: (0, 0))
    blk = pl.BlockSpec((P, tm, Fb), lambda i: (0, i, 0))
    anyspec = pl.BlockSpec(memory_space=pl.ANY)
    return pl.pallas_call(
        body, grid=(T // tm,), name="ffn_bwd",
        in_specs=[row(D), row(D), blk, blk, row(D), row(D), row(DC),
                  pl.BlockSpec((None, 1, N_MOD * D), lambda i: (i // tps, 0, 0)),
                  vec(DC), vec(DC), vec(D), anyspec, anyspec, anyspec, anyspec],
        out_specs=[blk, blk, row(D), row(D), row(D), row(D - DC), row(DC),
                   pl.BlockSpec((None, 8, D), lambda i: (i // tps, 0, 0)),
                   pl.BlockSpec((8, D), lambda i: (0, 0)), pl.BlockSpec((8, DC), lambda i: (0, 0))],
        out_shape=[jax.ShapeDtypeStruct((P, T, Fb), MXU_DTYPE), jax.ShapeDtypeStruct((P, T, Fb), MXU_DTYPE),
                   jax.ShapeDtypeStruct((T, D), MXU_DTYPE), jax.ShapeDtypeStruct((T, D), F32),
                   jax.ShapeDtypeStruct((T, D), MXU_DTYPE), jax.ShapeDtypeStruct((T, D - DC), F32),
                   jax.ShapeDtypeStruct((T, DC), F32), jax.ShapeDtypeStruct((Bl, 8, D), F32),
                   jax.ShapeDtypeStruct((8, D), F32), jax.ShapeDtypeStruct((8, DC), F32)],
        scratch_shapes=[pltpu.VMEM(w_out.shape, w_out.dtype), pltpu.VMEM(w_gate.shape, w_gate.dtype),
                        pltpu.VMEM(w_up.shape, w_up.dtype), pltpu.VMEM(w_down.shape, w_down.dtype),
                        pltpu.SemaphoreType.DMA((4,))],
        compiler_params=_cp(("arbitrary",)),
    )(dy, x1, gs, us, fo, mixb, cv, mod, g_ln, b_ln, g_ffn, w_out, w_gate, w_up, w_down)


def _in_bwd(da, dg, dq, dk, dv, x2, dx1, mod, g_mix, w_in, *, S, tm):
    T, D = x2.shape
    P, _, Nb = w_in.shape
    DC = da.shape[1]
    NS = dq.shape[0]
    n_in = P * Nb
    tps = S // tm
    Bl = T // S

    def body(da_ref, dg_ref, dq_ref, dk_ref, dv_ref, x_ref, dx1_ref, mod_ref, g_ref, w_ref,
             dx_ref, dproj_ref, macc_ref, gacc_ref):
        i = pl.program_id(0)

        @pl.when(i == 0)
        def _():
            gacc_ref[...] = jnp.zeros_like(gacc_ref)

        @pl.when(i % tps == 0)
        def _():
            macc_ref[...] = jnp.zeros_like(macc_ref)

        pieces = [da_ref[...], dg_ref[...]] + [r[j] for r in (dq_ref, dk_ref, dv_ref) for j in range(NS)]
        dproj = jnp.concatenate(pieces, axis=1).astype(MXU_DTYPE)
        dproj_ref[...] = dproj
        dh = _dot_nt(dproj[:, 0:Nb], w_ref[0])
        for p in range(1, P):
            dh = dh + _dot_nt(dproj[:, Nb * p:Nb * (p + 1)], w_ref[p])
        x = x_ref[...]
        r = lax.rsqrt(jnp.mean(x * x, axis=-1, keepdims=True) + EPS)
        xr = x * r
        macc_ref[0:1, :] += _colsum(dh)
        macc_ref[1:2, :] += _colsum(dh * (xr * g_ref[...]))
        dn = dh * (1.0 + mod_ref[:, D:2 * D])
        gacc_ref[0:1, :] += _colsum(dn * xr)
        e = dn * g_ref[...]
        dx_ref[...] = dx1_ref[...] + r * e - xr * (r * jnp.mean(e * xr, axis=-1, keepdims=True))

    row = lambda w: pl.BlockSpec((tm, w), lambda i: (i, 0))
    slab = pl.BlockSpec((NS, tm, LANES), lambda i: (0, i, 0))
    return pl.pallas_call(
        body, grid=(T // tm,), name="in_bwd",
        in_specs=[row(DC), row(DC), slab, slab, slab, row(D), row(D),
                  pl.BlockSpec((None, 1, N_MOD * D), lambda i: (i // tps, 0, 0)),
                  pl.BlockSpec((1, D), lambda i: (0, 0)),
                  pl.BlockSpec((P, D, Nb), lambda i: (0, 0, 0))],
        out_specs=[row(D), row(n_in), pl.BlockSpec((None, 8, D), lambda i: (i // tps, 0, 0)),
                   pl.BlockSpec((8, D), lambda i: (0, 0))],
        out_shape=[jax.ShapeDtypeStruct((T, D), F32), jax.ShapeDtypeStruct((T, n_in), MXU_DTYPE),
                   jax.ShapeDtypeStruct((Bl, 8, D), F32), jax.ShapeDtypeStruct((8, D), F32)],
        compiler_params=_cp(("arbitrary",)),
    )(da, dg, dq, dk, dv, x2, dx1, mod, g_mix, w_in)


def _wgrad(a, b, *, P, name, tk, split=None, host=None):
    a_blk, b_blk = a.ndim == 3, b.ndim == 3
    plan, h_in, h_out = host if host is not None else (None, (), ())
    ni, no = len(h_in), len(h_out)
    T = a.shape[-2]
    if a_blk:
        R, C = a.shape[2], b.shape[1]
        a_of = lambda av, p: av[p]
        b_of = lambda bv, p: bv[...]
    elif b_blk:
        R, C = a.shape[1], b.shape[2]
        a_of = lambda av, p: av[...]
        b_of = lambda bv, p: bv[p]
    elif split == "a":
        R, C = a.shape[1] // P, b.shape[1]
        a_of = lambda av, p: av[:, R * p:R * (p + 1)]
        b_of = lambda bv, p: bv[...]
    else:
        R, C = a.shape[1], b.shape[1] // P
        a_of = lambda av, p: av[...]
        b_of = lambda bv, p: bv[:, C * p:C * (p + 1)]

    n_steps = T // tk

    def body(a_ref, b_ref, *rest):
        hin, o_ref, hout, sems = rest[:ni], rest[ni], rest[ni + 1:ni + 1 + no], rest[ni + 1 + no:]
        step = pl.program_id(0)

        @pl.when(step == 0)
        def _():
            o_ref[...] = jnp.zeros_like(o_ref)
            if plan is not None:
                plan.start(hin, hout, sems)

        if plan is not None:
            @pl.when(step == n_steps // 2)
            def _():
                plan.forward(hin, hout, sems)

        for p in range(P):
            o_ref[p] += _dot_tn(a_of(a_ref, p), b_of(b_ref, p))

        if plan is not None:
            @pl.when(step == n_steps - 1)
            def _():
                plan.finish(hin, hout, sems)

    def spec(v):
        if v.ndim == 3:
            return pl.BlockSpec((P, tk, v.shape[2]), lambda k: (0, k, 0))
        return pl.BlockSpec((tk, v.shape[1]), lambda k: (k, 0))

    anyspec = pl.BlockSpec(memory_space=pl.ANY)
    res = pl.pallas_call(
        body, grid=(n_steps,), name=name,
        in_specs=[spec(a), spec(b)] + [anyspec] * ni,
        out_specs=[pl.BlockSpec((P, R, C), lambda k: (0, 0, 0))] + [anyspec] * no,
        out_shape=[jax.ShapeDtypeStruct((P, R, C), F32)] + list(h_out),
        scratch_shapes=plan.scratch() if plan is not None else [],
        compiler_params=_cp(("arbitrary",)),
    )(a, b, *h_in)
    return res if plan is not None else res[0]


TM_IN = 512
TM_FFN = 256
TK_WGRAD = 512


def _alibi_slabs(n_slab):
    heads = 2 * n_slab
    slopes = 2.0 ** (-8.0 * np.arange(1, heads + 1) / heads)
    return jnp.asarray(np.broadcast_to(np.repeat(slopes.reshape(n_slab, 1, 2), HEAD_DIM, axis=2), (n_slab, 8, LANES)),
                       dtype=F32)


def _local_step(x, tgt, mod, g_mix, wdw, g_ln, b_ln, g_q, g_k, g_ffn, w_in, w_out, w_gate, w_up, w_down,
                cidx=None):
    Bl, S, D = x.shape
    T = Bl * S
    DC = g_ln.shape[1]
    P = w_in.shape[0]
    n_slab = (D - DC) // LANES
    x2 = x.reshape(T, D)
    t2 = tgt.reshape(T, D)
    mod3 = mod.reshape(Bl, 1, N_MOD * D)
    gq2 = jnp.tile(g_q, (1, LANES // HEAD_DIM))
    gk2 = jnp.tile(g_k, (1, LANES // HEAD_DIM))
    slopes = _alibi_slabs(n_slab)

    ag, qkv, qkh, h1 = _fwd_in(x2, mod3, g_mix, gq2, gk2, w_in, S=S, tm=TM_IN, n_ag=2 * DC)
    cv = _conv_fwd(ag, wdw, Bl=Bl, S=S, DC=DC)
    if cidx is not None:
        ya, lse, w_out, w_gate, w_up, w_down = _attn_fwd(qkh, qkv, slopes, Bl=Bl, S=S,
                                                         hosted=(w_out, w_gate, w_up, w_down))
    else:
        ya, lse = _attn_fwd(qkh, qkv, slopes, Bl=Bl, S=S)
    x1, ycat, mixb, h2, gs, us, acts, fo, dy, lossb = _ffn_fwd(
        x2, cv, ya, t2, mod3, g_ln, b_ln, g_ffn, w_out, w_gate, w_up, w_down, S=S, tm=TM_FFN)
    dgs, dus, dfb, dx1, dmixb, dya, dcv, macc_f, gacc_f, lacc = _ffn_bwd(
        dy, x1, gs, us, fo, mixb, cv, mod3, g_ln, b_ln, g_ffn, w_out, w_gate, w_up, w_down, S=S, tm=TM_FFN)
    wg = functools.partial(_wgrad, P=P, tk=TK_WGRAD)
    out = {}
    if cidx is None:
        grads = dict(w_down=wg(acts, dfb, name="wgrad_down"), w_gate=wg(dgs, h2, name="wgrad_gate"),
                     w_up=wg(dus, h2, name="wgrad_up"), w_out=wg(ycat, dmixb, name="wgrad_out", split="a"))
        dq, dk, dv, gqk = _attn_bwd(qkh, qkv, ya, lse, dya, gq2, gk2, slopes, Bl=Bl, S=S)
    else:
        g_down = wg(acts, dfb, name="wgrad_down")
        g_gate, r_down = wg(dgs, h2, name="wgrad_gate", host=_sibling_host([g_down]))
        g_up, r_gate = wg(dus, h2, name="wgrad_up", host=_sibling_host([g_gate]))
        g_out, r_up = wg(ycat, dmixb, name="wgrad_out", split="a", host=_sibling_host([g_up]))
        (r_out,) = _rs_sibling([g_out], "rs_sibling_out")
        grads = dict(w_down=g_down, w_gate=g_gate, w_up=g_up, w_out=g_out)
        sums = [_pair_add(grads[nm], r, cidx, "pair_add_" + nm)
                for nm, r in zip(EARLY_WEIGHTS, (r_down, r_gate, r_up, r_out))]
        res = _attn_bwd(qkh, qkv, ya, lse, dya, gq2, gk2, slopes, Bl=Bl, S=S, hosted=tuple(sb for _, sb in sums))
        dq, dk, dv, gqk = res[:4]
        out["early_sums"] = [s32 for s32, _ in sums]
        out["early_recv"] = list(res[4:])
    da, dg, dwdw = _conv_bwd(ag, dcv, wdw, Bl=Bl, S=S, DC=DC)
    dx, dprojb, macc_m, gacc_m = _in_bwd(da, dg, dq, dk, dv, x2, dx1, mod3, g_mix, w_in, S=S, tm=TM_IN)
    packed = _pack_small(macc_m, macc_f, gacc_m, gacc_f, lacc, gqk, dwdw, lossb)
    if cidx is None:
        grads["w_in"] = wg(h1, dprojb, name="wgrad_in", split="b")
    else:
        grads["w_in"], out["gathered_small"] = wg(h1, dprojb, name="wgrad_in", split="b",
                                                  host=_small_gather_host(packed))
    out.update(dx=dx.reshape(Bl, S, D), grads=grads, packed=packed)
    return out


EARLY_WEIGHTS = ("w_down", "w_gate", "w_up", "w_out")


def _small_layout(Bl):
    return 8 * Bl, 8 * Bl + 8, 8 * Bl + 8 + CONV_ROWS


def _pack_small(macc_m, macc_f, gacc_m, gacc_f, lacc, gqk, dwdw, lossb):
    Bl, _, D = macc_m.shape
    DC = lacc.shape[1]
    assert 2 * DC <= D
    SMALL_GAIN_ROW, SMALL_TAP_ROW, SMALL_ROWS = _small_layout(Bl)

    def body(mm_ref, mf_ref, gm_ref, gf_ref, la_ref, qk_ref, dw_ref, loss_ref, o_ref):
        o_ref[...] = jnp.zeros_like(o_ref)
        for b in range(Bl):
            o_ref[8 * b + 0:8 * b + 2, :] = mm_ref[b, 0:2, :]
            o_ref[8 * b + 2:8 * b + 3, :] = mf_ref[b, 3:4, :]
            o_ref[8 * b + 3:8 * b + 6, :] = mf_ref[b, 0:3, :]
        r = SMALL_GAIN_ROW
        o_ref[r:r + 1, :] = gm_ref[0:1, :]
        o_ref[r + 1:r + 2, :] = gf_ref[0:1, :]
        o_ref[r + 2:r + 3, 0:DC] = la_ref[0:1, :]
        o_ref[r + 2:r + 3, DC:2 * DC] = la_ref[1:2, :]
        qk = qk_ref[0:2, 0:HEAD_DIM] + qk_ref[0:2, HEAD_DIM:2 * HEAD_DIM]
        o_ref[r + 3:r + 4, 0:HEAD_DIM] = qk[0:1, :]
        o_ref[r + 3:r + 4, HEAD_DIM:2 * HEAD_DIM] = qk[1:2, :]
        o_ref[r + 4:r + 5, 0:LANES] = loss_ref[0:1, :]
        o_ref[SMALL_TAP_ROW:SMALL_TAP_ROW + CONV_ROWS, 0:DC] = dw_ref[...]

    return pl.pallas_call(body, name="pack_small", out_shape=jax.ShapeDtypeStruct((SMALL_ROWS, D), F32),
                          compiler_params=_cp())(macc_m, macc_f, gacc_m, gacc_f, lacc, gqk, dwdw, lossb)


def _row_tile(rows, cap=512):
    if rows <= cap:
        return rows
    best = rows
    for t in range(8, cap + 1, 8):
        if rows % t == 0:
            best = t
    return best


def _cast_weight(w, pidx, name):
    def body(p_ref, w_ref, o_ref):
        o_ref[...] = w_ref[...].astype(MXU_DTYPE)
    R, C = w.shape
    tr = _row_tile(R)
    return pl.pallas_call(
        body, name=name,
        grid_spec=pltpu.PrefetchScalarGridSpec(
            num_scalar_prefetch=1, grid=(R // tr,),
            in_specs=[pl.BlockSpec((tr, C), lambda i, p: (i, 0))],
            out_specs=pl.BlockSpec((None, tr, C), lambda i, p: (p[0], i, 0))),
        out_shape=jax.ShapeDtypeStruct((4, R, C), MXU_DTYPE),
    )(pidx, w)


def _pair_add(g, recv, cidx, name):
    P, R, C = g.shape
    R2 = R // 2

    def body(c_ref, g_ref, r_ref, o_ref, ob_ref):
        s = g_ref[...] + r_ref[...]
        o_ref[...] = s
        ob_ref[...] = s.astype(jnp.bfloat16)

    return pl.pallas_call(
        body, name=name,
        grid_spec=pltpu.PrefetchScalarGridSpec(
            num_scalar_prefetch=1, grid=(P,),
            in_specs=[pl.BlockSpec((None, R2, C), lambda p, c: (p, c[0], 0)),
                      pl.BlockSpec((None, R2, C), lambda p, c: (p, 0, 0))],
            out_specs=[pl.BlockSpec((None, R2, C), lambda p, c: (p, 0, 0)),
                       pl.BlockSpec((None, R2, C), lambda p, c: (p, 0, 0))]),
        out_shape=[jax.ShapeDtypeStruct((P, R2, C), F32), jax.ShapeDtypeStruct((P, R2, C), jnp.bfloat16)],
    )(cidx, g, recv)


def _final_add(chipsum, recv, pc_idx, name):
    P, R2, C = chipsum.shape

    def body(pc_ref, s_ref, r_ref, o_ref):
        acc = s_ref[...]
        for k in range(3):
            acc = acc + r_ref[k].astype(F32)
        o_ref[...] = acc

    return pl.pallas_call(
        body, name=name,
        grid_spec=pltpu.PrefetchScalarGridSpec(
            num_scalar_prefetch=1, grid=(1,),
            in_specs=[pl.BlockSpec((None, R2, C), lambda i, pc: (pc[0], 0, 0)),
                      pl.BlockSpec((3, R2, C), lambda i, pc: (0, 0, 0))],
            out_specs=pl.BlockSpec((R2, C), lambda i, pc: (pc[1], 0))),
        out_shape=jax.ShapeDtypeStruct((2 * R2, C), F32),
    )(pc_idx, chipsum, recv)


def _adamw(w, g, m, v, name):
    R, C = w.shape
    tr = _row_tile(R, 256)
    c1 = 1.0 - ADAM_B1 ** ADAM_STEP
    c2 = 1.0 - ADAM_B2 ** ADAM_STEP

    def body(w_ref, g_ref, m_ref, v_ref, d_ref, nm_ref, nv_ref):
        gg = g_ref[...]
        nm = ADAM_B1 * m_ref[...] + (1.0 - ADAM_B1) * gg
        nv = ADAM_B2 * v_ref[...] + (1.0 - ADAM_B2) * (gg * gg)
        nm_ref[...] = nm
        nv_ref[...] = nv
        d_ref[...] = -ADAM_LR * ((nm / c1) / (jnp.sqrt(nv / c2) + ADAM_EPS) + ADAM_WD * w_ref[...])

    spec = pl.BlockSpec((tr, C), lambda i: (i, 0))
    return pl.pallas_call(
        body, grid=(R // tr,), name=name,
        in_specs=[spec] * 4, out_specs=[spec] * 3,
        out_shape=[jax.ShapeDtypeStruct((R, C), F32)] * 3,
    )(w, g, m, v)


def _ada_fwd(c_all, w_ada, b_cols):
    def body(c_ref, w_ref, b_ref, o_ref):
        c = c_ref[...]
        o_ref[...] = jnp.dot(c * _sigmoid(c), w_ref[...], preferred_element_type=F32, precision=HIGHEST) + b_ref[...]
    return pl.pallas_call(
        body, name="ada_fwd", out_shape=jax.ShapeDtypeStruct((c_all.shape[0], w_ada.shape[1]), F32),
        compiler_params=_cp(),
    )(c_all, w_ada, b_cols)


def _ada_bwd(c_all, dmod_cols):
    def body(c_ref, d_ref, o_ref):
        c = c_ref[...]
        o_ref[...] = lax.dot_general(c * _sigmoid(c), d_ref[...], (((0,), (0,)), ((), ())),
                                     preferred_element_type=F32, precision=HIGHEST)
    return pl.pallas_call(
        body, name="ada_bwd", out_shape=jax.ShapeDtypeStruct((c_all.shape[1], dmod_cols.shape[1]), F32),
        compiler_params=_cp(),
    )(c_all, dmod_cols)


def _small_reduce(gathered, n_dev, Bl):
    mod_rows, _, rows = _small_layout(Bl)
    width = gathered.shape[1]

    def body(g_ref, red_ref, bada_ref):
        acc = g_ref[0:rows, :]
        for d in range(1, n_dev):
            acc = acc + g_ref[d * rows:(d + 1) * rows, :]
        red_ref[...] = acc[mod_rows:, :]
        b = acc[0:8, :]
        for q in range(1, Bl):
            b = b + acc[8 * q:8 * q + 8, :]
        bada_ref[...] = b
    return pl.pallas_call(
        body, name="small_reduce",
        out_shape=[jax.ShapeDtypeStruct((rows - mod_rows, width), F32), jax.ShapeDtypeStruct((8, width), F32)],
        compiler_params=_cp(),
    )(gathered)


def _mesh_pos():
    return lax.axis_index("x"), lax.axis_index("y"), lax.axis_index("c")


def _other_chips(x, y):
    return [(1 - x, y), (x, 1 - y), (1 - x, 1 - y)]


def _allgather8(xs, name):
    m_per, n = xs.shape

    def body(x_ref, out_ref, send_sems, recv_sems, local_sem):
        x, y, c = _mesh_pos()
        me, sibling = (x, y, c), (x, y, 1 - c)
        chips = _other_chips(x, y)

        def rows(px, py, pc):
            return out_ref.at[pl.ds((4 * px + 2 * py + pc) * m_per, m_per), :]

        def copy(k, block, to, src=None):
            return pltpu.make_async_remote_copy(
                src_ref=rows(*block) if src is None else src, dst_ref=rows(*block),
                send_sem=send_sems.at[k], recv_sem=recv_sems.at[k], device_id=to, device_id_type=MESH_DEV)

        mine = pltpu.make_async_copy(x_ref, rows(*me), local_sem)
        mine.start()
        first = [copy(0, me, sibling, src=x_ref)]
        first += [copy(1 + j, me, (*chip, c), src=x_ref) for j, chip in enumerate(chips)]
        for cp in first:
            cp.start()
        passed = [copy(4 + j, (*chip, c), sibling) for j, chip in enumerate(chips)]
        for j, chip in enumerate(chips):
            copy(1 + j, (*chip, c), me).wait_recv()
            passed[j].start()
        copy(0, sibling, me).wait_recv()
        for j, chip in enumerate(chips):
            copy(4 + j, (*chip, 1 - c), me).wait_recv()
        for cp in first + passed:
            cp.wait_send()
        mine.wait()

    return pl.pallas_call(
        body, name=name, out_shape=jax.ShapeDtypeStruct((8 * m_per, n), xs.dtype),
        in_specs=[pl.BlockSpec(memory_space=pltpu.VMEM)], out_specs=pl.BlockSpec(memory_space=pltpu.VMEM),
        scratch_shapes=[pltpu.SemaphoreType.DMA((7,)), pltpu.SemaphoreType.DMA((7,)), pltpu.SemaphoreType.DMA],
        compiler_params=_cp(),
    )(xs)


class _WeightGather:
    def __init__(self, shapes):
        self.shapes = shapes
        self.n = len(shapes)

    def scratch(self):
        return [pltpu.SemaphoreType.DMA((6 * self.n,)), pltpu.SemaphoreType.DMA((6 * self.n,))]

    def _copy(self, outs, sems, w, k, slot, h, to):
        r2 = self.shapes[w][1] // 2
        blk = outs[w].at[slot, pl.ds(h * r2, r2), :]
        return pltpu.make_async_remote_copy(
            src_ref=blk, dst_ref=blk, send_sem=sems[0].at[6 * w + k], recv_sem=sems[1].at[6 * w + k],
            device_id=to, device_id_type=MESH_DEV)

    def start(self, outs, sems):
        x, y, c = _mesh_pos()
        for w in range(self.n):
            for k, chip in enumerate(_other_chips(x, y)):
                self._copy(outs, sems, w, k, 2 * x + y, c, (*chip, c)).start()

    def forward(self, outs, sems):
        x, y, c = _mesh_pos()
        for w in range(self.n):
            for k, chip in enumerate(_other_chips(x, y)):
                slot = 2 * chip[0] + chip[1]
                self._copy(outs, sems, w, k, slot, c, (x, y, 1 - c)).wait_recv()
                self._copy(outs, sems, w, 3 + k, slot, c, (x, y, 1 - c)).start()

    def finish(self, outs, sems):
        x, y, c = _mesh_pos()
        for w in range(self.n):
            for k, chip in enumerate(_other_chips(x, y)):
                slot = 2 * chip[0] + chip[1]
                self._copy(outs, sems, w, 3 + k, slot, 1 - c, (x, y, 1 - c)).wait_recv()
                self._copy(outs, sems, w, k, 2 * x + y, c, (*chip, c)).wait_send()
                self._copy(outs, sems, w, 3 + k, slot, c, (x, y, 1 - c)).wait_send()


def _gather_weights(bufs, name):
    n = len(bufs)
    plan = _WeightGather([b.shape for b in bufs])

    def body(*refs):
        outs = refs[n:2 * n]
        sems = refs[2 * n:]
        plan.start(outs, sems)
        plan.forward(outs, sems)
        plan.finish(outs, sems)

    anyspec = pl.BlockSpec(memory_space=pl.ANY)
    return pl.pallas_call(
        body, name=name,
        out_shape=[jax.ShapeDtypeStruct(b.shape, b.dtype) for b in bufs],
        in_specs=[anyspec] * n, out_specs=[anyspec] * n,
        input_output_aliases={w: w for w in range(n)},
        scratch_shapes=plan.scratch(),
    )(*bufs)


class _SiblingExchange:
    def __init__(self, shapes):
        self.shapes = shapes

    def scratch(self):
        n = sum(s[0] for s in self.shapes)
        return [pltpu.SemaphoreType.DMA((n,)), pltpu.SemaphoreType.DMA((n,))]

    def out_shapes(self, dtype):
        return [jax.ShapeDtypeStruct((s[0], s[1] // 2, s[2]), dtype) for s in self.shapes]

    def _copies(self, ins, outs, sems):
        x, y, c = _mesh_pos()
        cps, k = [], 0
        for w, (P, R, _) in enumerate(self.shapes):
            r2 = R // 2
            for p in range(P):
                cps.append(pltpu.make_async_remote_copy(
                    src_ref=ins[w].at[p, pl.ds((1 - c) * r2, r2), :], dst_ref=outs[w].at[p],
                    send_sem=sems[0].at[k], recv_sem=sems[1].at[k],
                    device_id=(x, y, 1 - c), device_id_type=MESH_DEV))
                k += 1
        return cps

    def start(self, ins, outs, sems):
        for cp in self._copies(ins, outs, sems):
            cp.start()

    def forward(self, ins, outs, sems):
        pass

    def finish(self, ins, outs, sems):
        for cp in self._copies(ins, outs, sems):
            cp.wait()


def _sibling_host(grads):
    plan = _SiblingExchange([g.shape for g in grads])
    return plan, tuple(grads), tuple(plan.out_shapes(grads[0].dtype))


def _rs_sibling(grads, name):
    n = len(grads)
    plan, _, out_shapes = _sibling_host(grads)

    def body(*refs):
        ins, outs, sems = refs[:n], refs[n:2 * n], refs[2 * n:]
        plan.start(ins, outs, sems)
        plan.finish(ins, outs, sems)

    anyspec = pl.BlockSpec(memory_space=pl.ANY)
    return pl.pallas_call(
        body, name=name, out_shape=list(out_shapes),
        in_specs=[anyspec] * n, out_specs=[anyspec] * n, scratch_shapes=plan.scratch(),
    )(*grads)


class _SmallGather:
    def __init__(self, m_per):
        self.m = m_per

    def scratch(self):
        return [pltpu.SemaphoreType.DMA((7,)), pltpu.SemaphoreType.DMA((7,)), pltpu.SemaphoreType.DMA]

    def _rows(self, out, pos):
        px, py, pc = pos
        return out.at[pl.ds((4 * px + 2 * py + pc) * self.m, self.m), :]

    def _copy(self, out, sems, k, block, to, src=None):
        dst = self._rows(out, block)
        return pltpu.make_async_remote_copy(
            src_ref=dst if src is None else src, dst_ref=dst, send_sem=sems[0].at[k], recv_sem=sems[1].at[k],
            device_id=to, device_id_type=MESH_DEV)

    def start(self, ins, outs, sems):
        x, y, c = _mesh_pos()
        me = (x, y, c)
        pltpu.make_async_copy(ins[0], self._rows(outs[0], me), sems[2]).start()
        self._copy(outs[0], sems, 0, me, (x, y, 1 - c), src=ins[0]).start()
        for j, chip in enumerate(_other_chips(x, y)):
            self._copy(outs[0], sems, 1 + j, me, (*chip, c), src=ins[0]).start()

    def forward(self, ins, outs, sems):
        x, y, c = _mesh_pos()
        for j, chip in enumerate(_other_chips(x, y)):
            self._copy(outs[0], sems, 1 + j, (*chip, c), (x, y, c)).wait_recv()
            self._copy(outs[0], sems, 4 + j, (*chip, c), (x, y, 1 - c)).start()

    def finish(self, ins, outs, sems):
        x, y, c = _mesh_pos()
        me = (x, y, c)
        self._copy(outs[0], sems, 0, (x, y, 1 - c), me).wait_recv()
        for j, chip in enumerate(_other_chips(x, y)):
            self._copy(outs[0], sems, 4 + j, (*chip, 1 - c), me).wait_recv()
        self._copy(outs[0], sems, 0, me, (x, y, 1 - c), src=ins[0]).wait_send()
        for j, chip in enumerate(_other_chips(x, y)):
            self._copy(outs[0], sems, 1 + j, me, (*chip, c), src=ins[0]).wait_send()
            self._copy(outs[0], sems, 4 + j, (*chip, c), (x, y, 1 - c)).wait_send()
        pltpu.make_async_copy(ins[0], self._rows(outs[0], me), sems[2]).wait()


def _small_gather_host(packed):
    m, n = packed.shape
    return _SmallGather(m), (packed,), (jax.ShapeDtypeStruct((8 * m, n), packed.dtype),)


class _ChipExchange:
    def __init__(self, n):
        self.n = n

    def scratch(self):
        return [pltpu.SemaphoreType.DMA((3 * self.n,)), pltpu.SemaphoreType.DMA((3 * self.n,))]

    def _copies(self, ins, outs, sems):
        x, y, c = _mesh_pos()
        return [pltpu.make_async_remote_copy(
            src_ref=ins[w].at[2 * chip[0] + chip[1]], dst_ref=outs[w].at[k],
            send_sem=sems[0].at[3 * w + k], recv_sem=sems[1].at[3 * w + k],
            device_id=(*chip, c), device_id_type=MESH_DEV)
            for w in range(self.n) for k, chip in enumerate(_other_chips(x, y))]

    def start(self, ins, outs, sems):
        for cp in self._copies(ins, outs, sems):
            cp.start()

    def forward(self, ins, outs, sems):
        pass

    def finish(self, ins, outs, sems):
        for cp in self._copies(ins, outs, sems):
            cp.wait()


def _rs_chips(sums, name):
    n = len(sums)
    plan = _ChipExchange(n)

    def body(*refs):
        ins, outs, sems = refs[:n], refs[n:2 * n], refs[2 * n:]
        plan.start(ins, outs, sems)
        plan.finish(ins, outs, sems)

    anyspec = pl.BlockSpec(memory_space=pl.ANY)
    return pl.pallas_call(
        body, name=name,
        out_shape=[jax.ShapeDtypeStruct((3,) + s.shape[1:], s.dtype) for s in sums],
        in_specs=[anyspec] * n, out_specs=[anyspec] * n,
        scratch_shapes=plan.scratch(),
    )(*sums)


def _rs_final(bufs):
    n = len(bufs)

    def body(*refs):
        outs = refs[n:2 * n]
        send_sems, recv_sems = refs[2 * n:]
        x, y, c = _mesh_pos()
        cps = []
        for w in range(n):
            r2 = bufs[w].shape[0] // 2
            mine = outs[w].at[pl.ds(c * r2, r2), :]
            cps.append(pltpu.make_async_remote_copy(
                src_ref=mine, dst_ref=mine, send_sem=send_sems.at[w], recv_sem=recv_sems.at[w],
                device_id=(x, y, 1 - c), device_id_type=MESH_DEV))
            cps[-1].start()
        for cp in cps:
            cp.wait()

    anyspec = pl.BlockSpec(memory_space=pl.ANY)
    return pl.pallas_call(
        body, name="rs_final",
        out_shape=[jax.ShapeDtypeStruct(b.shape, b.dtype) for b in bufs],
        in_specs=[anyspec] * n, out_specs=[anyspec] * n,
        input_output_aliases={w: w for w in range(n)},
        scratch_shapes=[pltpu.SemaphoreType.DMA((n,)), pltpu.SemaphoreType.DMA((n,))],
    )(*bufs)


BIG = ("w_in", "w_out", "w_gate", "w_up", "w_down")
TRANSPOSED = ("w_gate", "w_up")
WEIGHTS = ("w_ada", "b_ada", "g_mix", "w_in", "w_dw", "b_dw", "g_conv_ln", "b_conv_ln", "g_q", "g_k",
           "w_out", "g_ffn", "w_gate", "w_up", "w_down")


def _pad_to(a, rows, cols):
    return jnp.pad(a, ((0, rows - a.shape[0]), (0, cols - a.shape[1])))


def kernel(x, c, w_ada, b_ada, g_mix, w_in, w_dw, b_dw, g_conv_ln, b_conv_ln, g_q, g_k, w_out, g_ffn, w_gate, w_up, w_down, loss_target, m_w_ada, m_b_ada, m_g_mix, m_w_in, m_w_dw, m_b_dw, m_g_conv_ln, m_b_conv_ln, m_g_q, m_g_k, m_w_out, m_g_ffn, m_w_gate, m_w_up, m_w_down, v_w_ada, v_b_ada, v_g_mix, v_w_in, v_w_dw, v_b_dw, v_g_conv_ln, v_b_conv_ln, v_g_q, v_g_k, v_w_out, v_g_ffn, v_w_gate, v_w_up, v_w_down):
    w = dict(w_ada=w_ada, b_ada=b_ada, g_mix=g_mix, w_in=w_in, w_dw=w_dw, b_dw=b_dw, g_conv_ln=g_conv_ln,
             b_conv_ln=b_conv_ln, g_q=g_q, g_k=g_k, w_out=w_out, g_ffn=g_ffn, w_gate=w_gate, w_up=w_up, w_down=w_down)
    m = dict(w_ada=m_w_ada, b_ada=m_b_ada, g_mix=m_g_mix, w_in=m_w_in, w_dw=m_w_dw, b_dw=m_b_dw, g_conv_ln=m_g_conv_ln,
             b_conv_ln=m_b_conv_ln, g_q=m_g_q, g_k=m_g_k, w_out=m_w_out, g_ffn=m_g_ffn, w_gate=m_w_gate, w_up=m_w_up,
             w_down=m_w_down)
    v = dict(w_ada=v_w_ada, b_ada=v_b_ada, g_mix=v_g_mix, w_in=v_w_in, w_dw=v_w_dw, b_dw=v_b_dw, g_conv_ln=v_g_conv_ln,
             b_conv_ln=v_b_conv_ln, g_q=v_g_q, g_k=v_g_k, w_out=v_w_out, g_ffn=v_g_ffn, w_gate=v_w_gate, w_up=v_w_up,
             w_down=v_w_down)
    Bl, S, D = x.shape
    DC = g_conv_ln.shape[1]
    NA = w_ada.shape[2]
    xi, yi, ci = _mesh_pos()
    p = 2 * xi + yi
    dev = 2 * p + ci
    n_dev = 8
    cidx = jnp.reshape(ci, (1,)).astype(jnp.int32)
    pidx = jnp.reshape(p, (1,)).astype(jnp.int32)

    first = jnp.concatenate([_pad_to(c, 8, D), _pad_to(w_dw[0], CONV_ROWS, D)], axis=0)
    g0 = _allgather8(first, "gather_cond").reshape(n_dev, 8 + CONV_ROWS, D)
    c_all = g0[:, :Bl].reshape(n_dev * Bl, D)
    taps = jnp.concatenate([g0[2 * q, 8:, :w_dw.shape[2]] for q in range(4)], axis=1)
    wdw = jnp.where(lax.broadcasted_iota(jnp.int32, taps.shape, 0) == CONV_WIDTH, b_dw, taps)
    shard = lambda a, nm: a[0].T if nm in TRANSPOSED else a[0]
    owned = {nm: _cast_weight(shard(w[nm], nm), pidx, "cast_" + nm) for nm in BIG}
    (w_in_full,) = _gather_weights([owned["w_in"]], "gather_w_in")

    b_cols = lax.dynamic_slice_in_dim(b_ada, p * NA, NA, axis=1)
    mod_part = _ada_fwd(c_all, w_ada[0], b_cols)
    gm = _allgather8(mod_part, "gather_mod").reshape(n_dev, n_dev * Bl, NA)
    mod = jnp.concatenate([lax.dynamic_slice_in_dim(gm[2 * q], dev * Bl, Bl, axis=0) for q in range(4)], axis=1)

    loc = _local_step(x, loss_target, mod, g_mix, wdw, g_conv_ln, b_conv_ln, g_q, g_k, g_ffn,
                      w_in_full, owned["w_out"], owned["w_gate"], owned["w_up"], owned["w_down"], cidx=cidx)

    (late_sib,) = _rs_sibling([loc["grads"]["w_in"]], "rs_sibling_in")
    late32, late16 = _pair_add(loc["grads"]["w_in"], late_sib, cidx, "pair_add_w_in")
    (late_recv,) = _rs_chips([late16], "rs_chips_in")
    pc_idx = jnp.stack([p, ci]).astype(jnp.int32)
    order = EARLY_WEIGHTS + ("w_in",)
    sums32 = loc["early_sums"] + [late32]
    recv = loc["early_recv"] + [late_recv]
    halves = [_final_add(s32, r, pc_idx, "final_add_" + nm) for nm, s32, r in zip(order, sums32, recv)]
    grad = dict(zip(order, _rs_final(halves)))

    mod_rows, _, small_rows = _small_layout(Bl)
    gs = loc["gathered_small"]
    red, bada8 = _small_reduce(gs, n_dev, Bl)
    dmod_all = gs.reshape(n_dev, small_rows, D)[:, :mod_rows].reshape(n_dev * Bl, 8, D)[:, :N_MOD].reshape(n_dev * Bl, N_MOD * D)
    grad["w_ada"] = _ada_bwd(c_all, lax.dynamic_slice_in_dim(dmod_all, p * NA, NA, axis=1))
    grad["b_ada"] = bada8[:N_MOD].reshape(1, N_MOD * D)
    grad["g_mix"] = red[0:1]
    grad["g_ffn"] = red[1:2]
    grad["g_conv_ln"] = red[2:3, :DC]
    grad["b_conv_ln"] = red[2:3, DC:2 * DC]
    grad["g_q"] = red[3:4, :HEAD_DIM]
    grad["g_k"] = red[3:4, HEAD_DIM:2 * HEAD_DIM]
    loss = red[4, 0]
    dwdw = red[8:8 + CONV_ROWS, :DC]
    grad["w_dw"] = lax.dynamic_slice_in_dim(dwdw[:CONV_WIDTH], p * w_dw.shape[2], w_dw.shape[2], axis=1)
    grad["b_dw"] = dwdw[CONV_WIDTH:CONV_WIDTH + 1]

    delta, new_m, new_v = {}, {}, {}
    for nm in WEIGHTS:
        shp = w[nm].shape
        if nm in TRANSPOSED:
            d_, m_, v_ = _adamw(w[nm][0].T, grad[nm], m[nm][0].T, v[nm][0].T, "adamw_" + nm)
            grad[nm], delta[nm], new_m[nm], new_v[nm] = (a.T.reshape(shp) for a in (grad[nm], d_, m_, v_))
            continue
        two_d = (shp[-2], shp[-1]) if len(shp) == 3 else shp
        d_, m_, v_ = _adamw(w[nm].reshape(two_d), grad[nm].reshape(two_d), m[nm].reshape(two_d), v[nm].reshape(two_d),
                            "adamw_" + nm)
        grad[nm] = grad[nm].reshape(shp)
        delta[nm], new_m[nm], new_v[nm] = d_.reshape(shp), m_.reshape(shp), v_.reshape(shp)

    return (loss, loc["dx"], *[grad[nm] for nm in WEIGHTS], *[delta[nm] for nm in WEIGHTS],
            *[new_m[nm] for nm in WEIGHTS], *[new_v[nm] for nm in WEIGHTS])
```

```python
import functools
import math

import jax
import jax.numpy as jnp
import numpy as np
from jax import lax
from jax.experimental import pallas as pl
from jax.experimental.pallas import tpu as pltpu

F32 = jnp.float32
MXU_DTYPE = jnp.bfloat16
ACT_DTYPE = jnp.bfloat16
EPS = 1e-6
NEG_INF = -1e30
HEAD_DIM = 64
LANES = 128
RADIUS = 64
QBLK = 128
DILATIONS = (1, 4, 16)
CONV_WIDTH = 31
CONV_PAD = CONV_WIDTH // 2
CONV_ROWS = 32
N_MOD = 6
ADAM_LR, ADAM_B1, ADAM_B2, ADAM_EPS, ADAM_WD, ADAM_STEP = 0.001, 0.9, 0.999, 1e-08, 0.01, 10
HIGHEST = lax.Precision.HIGHEST
MESH_DEV = pl.DeviceIdType.MESH
VMEM_LIMIT = 56 << 20


def _cp(sem=None, vmem=VMEM_LIMIT):
    kw = dict(vmem_limit_bytes=vmem)
    if sem is not None:
        kw["dimension_semantics"] = sem
    return pltpu.CompilerParams(**kw)


def _sigmoid(x):
    return 1.0 / (1.0 + jnp.exp(-x))


def _dot(a, b):
    return jnp.dot(a, b, preferred_element_type=F32)


def _dot_nt(a, b):
    return lax.dot_general(a, b, (((1,), (1,)), ((), ())), preferred_element_type=F32)


def _dot_tn(a, b):
    return lax.dot_general(a, b, (((0,), (0,)), ((), ())), preferred_element_type=F32)


def _colsum(v):
    return jnp.sum(v, axis=0, keepdims=True)


def _load_resident(i, pairs, sems):
    @pl.when(i == 0)
    def _():
        cps = [pltpu.make_async_copy(src, dst, sems.at[n]) for n, (src, dst) in enumerate(pairs)]
        for c in cps:
            c.start()
        for c in cps:
            c.wait()


def _fwd_in(x2, mod, g_mix, gq2, gk2, w_in, *, S, tm, n_ag):
    T, D = x2.shape
    P, _, Nb = w_in.shape
    n_in = P * Nb
    n_slab = (n_in - n_ag) // LANES
    NS = n_slab // 3
    tps = S // tm

    def body(x_ref, mod_ref, g_ref, gq_ref, gk_ref, w_ref, ag_ref, qkv_ref, qkh_ref, h_ref):
        x = x_ref[...]
        r = lax.rsqrt(jnp.mean(x * x, axis=-1, keepdims=True) + EPS)
        n = x * r * g_ref[...]
        h = n * (1.0 + mod_ref[:, D:2 * D]) + mod_ref[:, 0:D]
        hb = h.astype(MXU_DTYPE)
        h_ref[...] = hb
        parts = [_dot(hb, w_ref[p]) for p in range(P)]
        proj = jnp.concatenate(parts, axis=1) if P > 1 else parts[0]
        ag_ref[...] = proj[:, :n_ag]
        mm = _head_mean_matrix()
        for j in range(n_slab):
            v = proj[:, n_ag + LANES * j:n_ag + LANES * (j + 1)]
            qkv_ref[j] = v
            if j < 2 * NS:
                gain = gq_ref[...] * (HEAD_DIM ** -0.5 * LOG2E) if j < NS else gk_ref[...]
                qkh_ref[j] = v * lax.rsqrt(_head_mean(v * v, mm) + EPS) * gain

    return pl.pallas_call(
        body, grid=(T // tm,), name="fwd_in",
        in_specs=[pl.BlockSpec((tm, D), lambda i: (i, 0)),
                  pl.BlockSpec((None, 1, N_MOD * D), lambda i: (i // tps, 0, 0)),
                  pl.BlockSpec((1, D), lambda i: (0, 0)),
                  pl.BlockSpec((1, LANES), lambda i: (0, 0)), pl.BlockSpec((1, LANES), lambda i: (0, 0)),
                  pl.BlockSpec((P, D, Nb), lambda i: (0, 0, 0))],
        out_specs=[pl.BlockSpec((tm, n_ag), lambda i: (i, 0)),
                   pl.BlockSpec((n_slab, tm, LANES), lambda i: (0, i, 0)),
                   pl.BlockSpec((2 * NS, tm, LANES), lambda i: (0, i, 0)),
                   pl.BlockSpec((tm, D), lambda i: (i, 0))],
        out_shape=[jax.ShapeDtypeStruct((T, n_ag), F32),
                   jax.ShapeDtypeStruct((n_slab, T, LANES), F32),
                   jax.ShapeDtypeStruct((2 * NS, T, LANES), F32),
                   jax.ShapeDtypeStruct((T, D), MXU_DTYPE)],
        compiler_params=_cp(("arbitrary",)),
    )(x2, mod, g_mix, gq2, gk2, w_in)


CONV_CH = 64


def _conv_taps(win, w_ref, acc, reverse):
    n = win.shape[0]
    for b in range(8):
        wb = win if b == 0 else pltpu.roll(win, shift=n - b, axis=0)
        for a in range(4):
            o = 8 * a + b
            if o < 1 or o > CONV_WIDTH:
                continue
            k = (CONV_WIDTH - o) if reverse else (o - 1)
            acc = acc + w_ref[k:k + 1, :] * wb[8 * a:8 * a + CONV_CH, :]
    return acc


def _conv_fwd(ag, wdw, *, Bl, S, DC):
    T = ag.shape[0]
    nsc = DC // LANES
    CH = CONV_CH

    def body(a_ref, g_ref, w_ref, cv_ref, upad):
        zeros16 = jnp.zeros((16, LANES), F32)
        upad[0:16, :] = zeros16
        upad[S + 16:S + 32, :] = zeros16

        def fill(i, _):
            r0 = pl.multiple_of(i * CH, CH)
            a = a_ref[pl.ds(r0, CH), :]
            g = g_ref[pl.ds(r0, CH), :]
            upad[pl.ds(r0 + 16, CH), :] = a * _sigmoid(g)
            return 0
        lax.fori_loop(0, S // CH, fill, 0)

        def conv(i, _):
            r0 = pl.multiple_of(i * CH, CH)
            win = upad[pl.ds(r0, CH + 32), :]
            acc = jnp.zeros((CH, LANES), F32) + w_ref[CONV_WIDTH:CONV_WIDTH + 1, :]
            cv_ref[pl.ds(r0, CH), :] = _conv_taps(win, w_ref, acc, reverse=False)
            return 0
        lax.fori_loop(0, S // CH, conv, 0)

    return pl.pallas_call(
        body, grid=(Bl, nsc), name="conv_fwd",
        in_specs=[pl.BlockSpec((S, LANES), lambda b, j: (b, j)),
                  pl.BlockSpec((S, LANES), lambda b, j: (b, nsc + j)),
                  pl.BlockSpec((CONV_ROWS, LANES), lambda b, j: (0, j))],
        out_specs=pl.BlockSpec((S, LANES), lambda b, j: (b, j)),
        out_shape=jax.ShapeDtypeStruct((T, DC), F32),
        scratch_shapes=[pltpu.VMEM((S + 32, LANES), F32)],
        compiler_params=_cp(("arbitrary", "arbitrary")),
    )(ag, ag, wdw)


def _conv_bwd(ag, dcv, wdw, *, Bl, S, DC):
    T = ag.shape[0]
    nsc = DC // LANES
    CH = CONV_CH

    def body(a_ref, g_ref, d_ref, w_ref, da_ref, dg_ref, dw_ref, upad, dpad, wacc):
        b = pl.program_id(1)
        zeros16 = jnp.zeros((16, LANES), F32)
        upad[0:16, :] = zeros16
        upad[S + 16:S + 32, :] = zeros16
        dpad[0:16, :] = zeros16
        dpad[S + 16:S + 32, :] = zeros16

        @pl.when(b == 0)
        def _():
            wacc[...] = jnp.zeros_like(wacc)

        def fill(i, _):
            r0 = pl.multiple_of(i * CH, CH)
            a = a_ref[pl.ds(r0, CH), :]
            g = g_ref[pl.ds(r0, CH), :]
            upad[pl.ds(r0 + 16, CH), :] = a * _sigmoid(g)
            dpad[pl.ds(r0 + 16, CH), :] = d_ref[pl.ds(r0, CH), :]
            return 0
        lax.fori_loop(0, S // CH, fill, 0)

        def step(i, _):
            r0 = pl.multiple_of(i * CH, CH)
            dwin = dpad[pl.ds(r0, CH + 32), :]
            du = _conv_taps(dwin, w_ref, jnp.zeros((CH, LANES), F32), reverse=True)
            a = a_ref[pl.ds(r0, CH), :]
            g = g_ref[pl.ds(r0, CH), :]
            sg = _sigmoid(g)
            da_ref[pl.ds(r0, CH), :] = du * sg
            dg_ref[pl.ds(r0, CH), :] = du * a * sg * (1.0 - sg)
            dc = d_ref[pl.ds(r0, CH), :]
            uwin = upad[pl.ds(r0, CH + 32), :]
            n = CH + 32
            for bb in range(8):
                wb = uwin if bb == 0 else pltpu.roll(uwin, shift=n - bb, axis=0)
                for aa in range(4):
                    o = 8 * aa + bb
                    if o < 1 or o > CONV_WIDTH:
                        continue
                    k = o - 1
                    prod = dc * wb[8 * aa:8 * aa + CH, :]
                    part = prod[0:8, :]
                    for q in range(1, CH // 8):
                        part = part + prod[8 * q:8 * q + 8, :]
                    wacc[8 * k:8 * k + 8, :] += part
            part = dc[0:8, :]
            for q in range(1, CH // 8):
                part = part + dc[8 * q:8 * q + 8, :]
            wacc[8 * CONV_WIDTH:8 * CONV_WIDTH + 8, :] += part
            return 0
        lax.fori_loop(0, S // CH, step, 0)

        @pl.when(b == Bl - 1)
        def _():
            for k in range(CONV_ROWS):
                dw_ref[k:k + 1, :] = jnp.sum(wacc[8 * k:8 * k + 8, :], axis=0, keepdims=True)

    return pl.pallas_call(
        body, grid=(nsc, Bl), name="conv_bwd",
        in_specs=[pl.BlockSpec((S, LANES), lambda j, b: (b, j)),
                  pl.BlockSpec((S, LANES), lambda j, b: (b, nsc + j)),
                  pl.BlockSpec((S, LANES), lambda j, b: (b, j)),
                  pl.BlockSpec((CONV_ROWS, LANES), lambda j, b: (0, j))],
        out_specs=[pl.BlockSpec((S, LANES), lambda j, b: (b, j)),
                   pl.BlockSpec((S, LANES), lambda j, b: (b, j)),
                   pl.BlockSpec((CONV_ROWS, LANES), lambda j, b: (0, j))],
        out_shape=[jax.ShapeDtypeStruct((T, DC), F32), jax.ShapeDtypeStruct((T, DC), F32),
                   jax.ShapeDtypeStruct((CONV_ROWS, DC), F32)],
        scratch_shapes=[pltpu.VMEM((S + 32, LANES), F32), pltpu.VMEM((S + 32, LANES), F32),
                        pltpu.VMEM((8 * CONV_ROWS, LANES), F32)],
        compiler_params=_cp(("arbitrary", "arbitrary")),
    )(ag, ag, dcv, wdw)


ROWCH = 256


LOG2E = 1.4426950408889634
LN2 = 0.6931471805599453
N_EDGE = 4


def _head_mean_matrix():
    r = lax.broadcasted_iota(jnp.int32, (LANES, LANES), 0) // HEAD_DIM
    c = lax.broadcasted_iota(jnp.int32, (LANES, LANES), 1) // HEAD_DIM
    return jnp.where(r == c, 1.0 / HEAD_DIM, 0.0).astype(jnp.bfloat16)


def _head_mean(v, mm):
    hi = v.astype(jnp.bfloat16)
    lo = (v - hi.astype(F32)).astype(jnp.bfloat16)
    return _dot(hi, mm) + _dot(lo, mm)


def _stack_heads(blk, lane_lo):
    z = jnp.zeros_like(blk)
    return jnp.concatenate([jnp.where(lane_lo, blk, z), jnp.where(lane_lo, z, blk)], axis=0)


def _merge_heads(v2, lane_lo):
    return jnp.where(lane_lo, v2[:QBLK], v2[QBLK:])


def _bias_tables(bias_ref, slope_ref):
    row = lax.broadcasted_iota(jnp.int32, (2 * QBLK, 2 * QBLK), 0)
    col = lax.broadcasted_iota(jnp.int32, (2 * QBLK, 2 * QBLK), 1)
    rel = jnp.abs(col - RADIUS - (row % QBLK))
    slope = jnp.where(row < QBLK, slope_ref[0:1, 0:1], slope_ref[0:1, HEAD_DIM:HEAD_DIM + 1]) * LOG2E
    for pi, d in enumerate(DILATIONS):
        inside = jnp.where(rel <= RADIUS, -slope * (float(d) * rel.astype(F32)), NEG_INF)
        for e in range(N_EDGE):
            t = inside
            if e & 1:
                t = jnp.where(col < RADIUS, NEG_INF, t)
            if e & 2:
                t = jnp.where(col >= QBLK + RADIUS, NEG_INF, t)
            bias_ref[N_EDGE * pi + e] = t


def _edge_index(qb, nb):
    return jnp.where(qb == 0, 1, 0) + jnp.where(qb == nb - 1, 2, 0)


def _gather_rows(src_ref, dst_ref, S, d, pad):
    n = S // d
    seg = n + 2 * RADIUS if pad else n
    step = min(n, 512)
    for r in range(d):
        base = r * seg
        if pad:
            dst_ref[base:base + RADIUS, :] = jnp.zeros((RADIUS, LANES), dst_ref.dtype)
            dst_ref[base + RADIUS + n:base + seg, :] = jnp.zeros((RADIUS, LANES), dst_ref.dtype)
            base += RADIUS
        for c0 in range(0, n, step):
            if d == 1:
                v = src_ref[c0:c0 + step, :]
            else:
                v = src_ref[pl.ds(r + c0 * d, step, stride=d), :]
            dst_ref[base + c0:base + c0 + step, :] = v.astype(dst_ref.dtype)


def _scatter_rows(src_ref, dst_ref, S, d, pad, accumulate):
    n = S // d
    seg = n + 2 * RADIUS if pad else n
    step = min(n, 512)
    for r in range(d):
        base = r * seg + (RADIUS if pad else 0)
        for c0 in range(0, n, step):
            v = src_ref[base + c0:base + c0 + step, :]
            if d == 1:
                idx = pl.ds(c0, step)
            else:
                idx = pl.ds(r + c0 * d, step, stride=d)
            if accumulate:
                dst_ref[idx, :] = dst_ref[idx, :] + v
            else:
                dst_ref[idx, :] = v


def _zero_uncovered(acc, S, d):
    n = S // d
    if (n // QBLK) % 2:
        return
    seg = n + 2 * RADIUS
    for r in range(d):
        acc[0, r * seg + n:r * seg + seg, :] = jnp.zeros((2 * RADIUS, LANES), F32)
        acc[1, r * seg:r * seg + 2 * RADIUS, :] = jnp.zeros((2 * RADIUS, LANES), F32)


def _scatter_parity(acc, dst_ref, S, d):
    n = S // d
    seg = n + 2 * RADIUS
    step = min(n, 512)
    one_block = (n // QBLK) % 2 == 1
    for r in range(d):
        base = r * seg + RADIUS
        for c0 in range(0, n, step):
            rows = slice(base + c0, base + c0 + step)
            v = acc[r % 2, rows, :] if one_block else acc[0, rows, :] + acc[1, rows, :]
            idx = pl.ds(c0, step) if d == 1 else pl.ds(r + c0 * d, step, stride=d)
            dst_ref[idx, :] = dst_ref[idx, :] + v


PIPE_UNROLL = 4
PIPE_SLOTS = 16
BWD_SLOTS = 12


def _pipeline(n_items, stages, unroll):
    K = len(stages)
    assert n_items % unroll == 0 and K * unroll <= (PIPE_SLOTS if K == 4 else BWD_SLOTS)
    trips = n_items // unroll
    assert trips >= K - 1

    def trip(t, static):
        for s in reversed(range(K)):
            if static and not 0 <= t - s < trips:
                continue
            for u in range(unroll):
                item = unroll * (t - s) + u
                stages[s](jnp.int32(item) if static else item)

    for t in range(K - 1):
        trip(t, True)

    def full(t, carry):
        trip(t, False)
        return carry
    lax.fori_loop(K - 1, trips, full, 0)
    for t in range(trips, trips + K - 1):
        trip(t, True)


def _attn_fwd(qkh, qkv, slopes, *, Bl, S, hosted=()):
    n3, T, _ = qkv.shape
    NS = n3 // 3
    NB = S // QBLK
    PADR = S + 2 * RADIUS * DILATIONS[-1]
    nh = len(hosted)
    plan = _WeightGather([b.shape for b in hosted]) if nh else None
    n_steps = Bl * NS

    def body(qh, kh, v_ref, slope_ref, *rest):
        o_ref, lse_ref = rest[nh:nh + 2]
        wouts = rest[nh + 2:2 * nh + 2]
        (qp, kp, vp, op, lp, onat, lnat, bias_ref, sbuf, pbuf, mbuf, lbuf) = rest[2 * nh + 2:2 * nh + 14]
        sems = rest[2 * nh + 14:]
        step = pl.program_id(0) * NS + pl.program_id(1)
        if nh:
            @pl.when(step == 0)
            def _():
                plan.start(wouts, sems)

            @pl.when(step == n_steps // 2)
            def _():
                plan.forward(wouts, sems)

        lane_lo = lax.broadcasted_iota(jnp.int32, (QBLK, LANES), 1) < HEAD_DIM
        _bias_tables(bias_ref, slope_ref)

        for pi, d in enumerate(DILATIONS):
            n = S // d
            nb = n // QBLK
            _gather_rows(qh, qp, S, d, pad=False)
            _gather_rows(kh, kp, S, d, pad=True)
            _gather_rows(v_ref, vp, S, d, pad=True)

            def offsets(i, nb=nb):
                r = i // nb
                return pl.multiple_of(i * QBLK, QBLK), pl.multiple_of((i + r) * QBLK, QBLK), i % nb

            def scores(i, pi=pi, nb=nb):
                q0, k0, qb = offsets(i)
                qs = _stack_heads(qp[pl.ds(q0, QBLK), :], lane_lo)
                sbuf[i % PIPE_SLOTS] = (_dot_nt(qs, kp[pl.ds(k0, 2 * QBLK), :])
                                        + bias_ref[N_EDGE * pi + _edge_index(qb, nb)])

            def rowmax(i):
                m = jnp.max(sbuf[i % PIPE_SLOTS], axis=1, keepdims=True)
                mbuf[i % PIPE_SLOTS] = jnp.broadcast_to(m, (2 * QBLK, LANES))

            def expsum(i):
                m = mbuf[i % PIPE_SLOTS]
                p = jnp.exp2(sbuf[i % PIPE_SLOTS] - jnp.concatenate([m, m], axis=1))
                pbuf[i % PIPE_SLOTS] = p.astype(MXU_DTYPE)
                lbuf[i % PIPE_SLOTS] = jnp.broadcast_to(jnp.sum(p, axis=1, keepdims=True), (2 * QBLK, LANES))

            def values(i):
                q0, k0, _ = offsets(i)
                l = lbuf[i % PIPE_SLOTS]
                o2 = _dot(pbuf[i % PIPE_SLOTS], vp[pl.ds(k0, 2 * QBLK), :]) * (1.0 / l)
                op[pl.ds(q0, QBLK), :] = _merge_heads(o2, lane_lo)
                lp[pl.ds(q0, QBLK), :] = _merge_heads(mbuf[i % PIPE_SLOTS] + jnp.log2(l), lane_lo)

            _pipeline(NB, [scores, rowmax, expsum, values], PIPE_UNROLL)
            _scatter_rows(op, onat.at[pi], S, d, pad=False, accumulate=False)
            _scatter_rows(lp, lnat.at[pi], S, d, pad=False, accumulate=False)

        for c0 in range(0, S, ROWCH):
            ls = [lnat[pi, c0:c0 + ROWCH, :] for pi in range(len(DILATIONS))]
            mx = jnp.maximum(jnp.maximum(ls[0], ls[1]), ls[2])
            es = [jnp.exp2(l - mx) for l in ls]
            tot = es[0] + es[1] + es[2]
            inv = 1.0 / tot
            acc = (es[0] * inv) * onat[0, c0:c0 + ROWCH, :]
            for pi in (1, 2):
                acc = acc + (es[pi] * inv) * onat[pi, c0:c0 + ROWCH, :]
            o_ref[c0:c0 + ROWCH, :] = acc
            lse_ref[c0:c0 + ROWCH, :] = mx + jnp.log2(tot)

        if nh:
            @pl.when(step == n_steps - 1)
            def _():
                plan.finish(wouts, sems)

    spec_in = lambda off: pl.BlockSpec((None, S, LANES), lambda b, j: (off * NS + j, b, 0))
    out = pl.BlockSpec((S, LANES), lambda b, j: (b, j))
    anyspec = pl.BlockSpec(memory_space=pl.ANY)
    return pl.pallas_call(
        body, grid=(Bl, NS), name="attn_fwd",
        in_specs=[spec_in(0), spec_in(1), spec_in(2),
                  pl.BlockSpec((None, 8, LANES), lambda b, j: (j, 0, 0))] + [anyspec] * nh,
        out_specs=[out, out] + [anyspec] * nh,
        out_shape=[jax.ShapeDtypeStruct((T, NS * LANES), F32)] * 2
                  + [jax.ShapeDtypeStruct(b.shape, b.dtype) for b in hosted],
        input_output_aliases={4 + w: 2 + w for w in range(nh)},
        scratch_shapes=[pltpu.VMEM((S, LANES), MXU_DTYPE), pltpu.VMEM((PADR, LANES), MXU_DTYPE),
                        pltpu.VMEM((PADR, LANES), MXU_DTYPE),
                        pltpu.VMEM((S, LANES), F32), pltpu.VMEM((S, LANES), F32),
                        pltpu.VMEM((3, S, LANES), F32), pltpu.VMEM((3, S, LANES), F32),
                        pltpu.VMEM((N_EDGE * len(DILATIONS), 2 * QBLK, 2 * QBLK), F32),
                        pltpu.VMEM((PIPE_SLOTS, 2 * QBLK, 2 * QBLK), F32),
                        pltpu.VMEM((PIPE_SLOTS, 2 * QBLK, 2 * QBLK), MXU_DTYPE),
                        pltpu.VMEM((PIPE_SLOTS, 2 * QBLK, LANES), F32), pltpu.VMEM((PIPE_SLOTS, 2 * QBLK, LANES), F32)]
                       + (plan.scratch() if nh else []),
        compiler_params=_cp(("arbitrary", "arbitrary")),
    )(qkh, qkh, qkv, slopes, *hosted)


def _attn_bwd(qkh, qkv, o, lse, do, gq2, gk2, slopes, *, Bl, S, hosted=()):
    n3, T, _ = qkv.shape
    NS = n3 // 3
    NB = S // QBLK
    PADR = S + 2 * RADIUS * DILATIONS[-1]
    QSCALE = HEAD_DIM ** -0.5
    nh = len(hosted)
    plan = _ChipExchange(nh)
    n_steps = Bl * NS

    def body(qh, kh, q_ref, k_ref, v_ref, o_ref, lse_ref, do_ref, gq_ref, gk_ref, slope_ref, *rest):
        hin = rest[:nh]
        dq_ref, dk_ref, dv_ref, gacc_ref = rest[nh:nh + 4]
        hout = rest[nh + 4:2 * nh + 4]
        (dl, qp, kp, vp, dop, lp, dlp, dqp, dkacc, dvacc, dqn, dkn, bias_ref,
         sbuf, dpbuf, pbuf, dsbuf) = rest[2 * nh + 4:2 * nh + 21]
        sems = rest[2 * nh + 21:]
        step = pl.program_id(0) * NS + pl.program_id(1)

        @pl.when(step == 0)
        def _():
            gacc_ref[...] = jnp.zeros_like(gacc_ref)
            if nh:
                plan.start(hin, hout, sems)

        mm = _head_mean_matrix()
        lane_lo = lax.broadcasted_iota(jnp.int32, (QBLK, LANES), 1) < HEAD_DIM
        _bias_tables(bias_ref, slope_ref)
        for c0 in range(0, S, ROWCH):
            dl[c0:c0 + ROWCH, :] = _head_mean(do_ref[c0:c0 + ROWCH, :] * o_ref[c0:c0 + ROWCH, :], mm) * HEAD_DIM
            dqn[c0:c0 + ROWCH, :] = jnp.zeros((ROWCH, LANES), F32)
            dkn[c0:c0 + ROWCH, :] = jnp.zeros((ROWCH, LANES), F32)
            dv_ref[c0:c0 + ROWCH, :] = jnp.zeros((ROWCH, LANES), F32)

        for pi, d in enumerate(DILATIONS):
            n = S // d
            nb = n // QBLK
            _gather_rows(qh, qp, S, d, pad=False)
            _gather_rows(kh, kp, S, d, pad=True)
            _gather_rows(v_ref, vp, S, d, pad=True)
            _gather_rows(do_ref, dop, S, d, pad=False)
            _gather_rows(lse_ref, lp, S, d, pad=False)
            _gather_rows(dl, dlp, S, d, pad=False)
            _zero_uncovered(dkacc, S, d)
            _zero_uncovered(dvacc, S, d)

            def offsets(i, nb=nb):
                r = i // nb
                return pl.multiple_of(i * QBLK, QBLK), pl.multiple_of((i + r) * QBLK, QBLK), i % nb

            def scores(i, pi=pi, nb=nb):
                q0, k0, qb = offsets(i)
                qs = _stack_heads(qp[pl.ds(q0, QBLK), :], lane_lo)
                dos = _stack_heads(dop[pl.ds(q0, QBLK), :], lane_lo)
                sbuf[i % BWD_SLOTS] = (_dot_nt(qs, kp[pl.ds(k0, 2 * QBLK), :])
                                       + bias_ref[N_EDGE * pi + _edge_index(qb, nb)])
                dpbuf[i % BWD_SLOTS] = _dot_nt(dos, vp[pl.ds(k0, 2 * QBLK), :])

            def probs(i):
                q0, _, _ = offsets(i)
                lblk = lp[pl.ds(q0, QBLK), :]
                dblk = dlp[pl.ds(q0, QBLK), :]
                lcol = jnp.concatenate([lblk[:, 0:1], lblk[:, HEAD_DIM:HEAD_DIM + 1]], axis=0)
                dcol = jnp.concatenate([dblk[:, 0:1], dblk[:, HEAD_DIM:HEAD_DIM + 1]], axis=0)
                p = jnp.exp2(sbuf[i % BWD_SLOTS] - lcol)
                pbuf[i % BWD_SLOTS] = p.astype(MXU_DTYPE)
                dsbuf[i % BWD_SLOTS] = (p * (dpbuf[i % BWD_SLOTS] - dcol)).astype(MXU_DTYPE)

            def grads(i):
                q0, k0, _ = offsets(i)
                qs = _stack_heads(qp[pl.ds(q0, QBLK), :], lane_lo)
                dos = _stack_heads(dop[pl.ds(q0, QBLK), :], lane_lo)
                ds = dsbuf[i % BWD_SLOTS]
                dvacc[i % 2, pl.ds(k0, 2 * QBLK), :] = _dot_tn(pbuf[i % BWD_SLOTS], dos)
                dkacc[i % 2, pl.ds(k0, 2 * QBLK), :] = _dot_tn(ds, qs)
                dqp[pl.ds(q0, QBLK), :] = _merge_heads(_dot(ds, kp[pl.ds(k0, 2 * QBLK), :]), lane_lo)

            _pipeline(NB, [scores, probs, grads], PIPE_UNROLL)
            _scatter_rows(dqp, dqn, S, d, pad=False, accumulate=True)
            _scatter_parity(dkacc, dkn, S, d)
            _scatter_parity(dvacc, dv_ref, S, d)

        gq_sum = jnp.zeros((8, LANES), F32)
        gk_sum = jnp.zeros((8, LANES), F32)
        for c0 in range(0, S, ROWCH):
            for src_ref, dn, g_ref, dst_ref, scale, is_q in ((q_ref, dqn, gq_ref, dq_ref, QSCALE, True),
                                                             (k_ref, dkn, gk_ref, dk_ref, LN2, False)):
                x = src_ref[c0:c0 + ROWCH, :]
                dh = dn[c0:c0 + ROWCH, :]
                rr = lax.rsqrt(_head_mean(x * x, mm) + EPS)
                e = dh * (g_ref[...] * scale)
                dst_ref[c0:c0 + ROWCH, :] = rr * e - x * (rr * rr * rr) * _head_mean(e * x, mm)
                gpart = dh * (x * rr * scale)
                acc8 = gpart[0:8, :]
                for q8 in range(1, ROWCH // 8):
                    acc8 = acc8 + gpart[8 * q8:8 * q8 + 8, :]
                if is_q:
                    gq_sum = gq_sum + acc8
                else:
                    gk_sum = gk_sum + acc8
        gacc_ref[0:1, :] += jnp.sum(gq_sum, axis=0, keepdims=True)
        gacc_ref[1:2, :] += jnp.sum(gk_sum, axis=0, keepdims=True)

        if nh:
            @pl.when(step == n_steps - 1)
            def _():
                plan.finish(hin, hout, sems)

    spec_in = lambda off: pl.BlockSpec((None, S, LANES), lambda b, j: (off * NS + j, b, 0))
    tok = pl.BlockSpec((S, LANES), lambda b, j: (b, j))
    vec = pl.BlockSpec((1, LANES), lambda b, j: (0, 0))
    slab_out = pl.BlockSpec((None, S, LANES), lambda b, j: (j, b, 0))
    f32buf = lambda rows: pltpu.VMEM((rows, LANES), F32)
    bfbuf = lambda rows: pltpu.VMEM((rows, LANES), MXU_DTYPE)
    anyspec = pl.BlockSpec(memory_space=pl.ANY)
    return pl.pallas_call(
        body, grid=(Bl, NS), name="attn_bwd",
        in_specs=[spec_in(0), spec_in(1), spec_in(0), spec_in(1), spec_in(2), tok, tok, tok, vec, vec,
                  pl.BlockSpec((None, 8, LANES), lambda b, j: (j, 0, 0))] + [anyspec] * nh,
        out_specs=[slab_out, slab_out, slab_out, pl.BlockSpec((8, LANES), lambda b, j: (0, 0))] + [anyspec] * nh,
        out_shape=[jax.ShapeDtypeStruct((NS, T, LANES), F32)] * 3 + [jax.ShapeDtypeStruct((8, LANES), F32)]
                  + [jax.ShapeDtypeStruct((3,) + h.shape[1:], h.dtype) for h in hosted],
        scratch_shapes=[f32buf(S),
                        bfbuf(S), bfbuf(PADR), bfbuf(PADR), bfbuf(S),
                        f32buf(S), f32buf(S), f32buf(S),
                        pltpu.VMEM((2, PADR, LANES), F32), pltpu.VMEM((2, PADR, LANES), F32),
                        f32buf(S), f32buf(S),
                        pltpu.VMEM((N_EDGE * len(DILATIONS), 2 * QBLK, 2 * QBLK), F32),
                        pltpu.VMEM((BWD_SLOTS, 2 * QBLK, 2 * QBLK), F32),
                        pltpu.VMEM((BWD_SLOTS, 2 * QBLK, 2 * QBLK), F32),
                        pltpu.VMEM((BWD_SLOTS, 2 * QBLK, 2 * QBLK), MXU_DTYPE),
                        pltpu.VMEM((BWD_SLOTS, 2 * QBLK, 2 * QBLK), MXU_DTYPE)]
                       + (plan.scratch() if nh else []),
        compiler_params=_cp(("arbitrary", "arbitrary")),
    )(qkh, qkh, qkv, qkv, qkv, o, lse, do, gq2, gk2, slopes, *hosted)


def _layer_norm_parts(cv, g_ln, b_ln):
    mu = jnp.mean(cv, axis=-1, keepdims=True)
    cen = cv - mu
    rs = lax.rsqrt(jnp.mean(cen * cen, axis=-1, keepdims=True) + EPS)
    z = cen * rs
    return z, rs, z * g_ln + b_ln


def _ffn_fwd(x2, cv, ya, tgt, mod, g_ln, b_ln, g_ffn, w_out, w_gate, w_up, w_down, *, S, tm):
    T, D = x2.shape
    DC = cv.shape[1]
    P, Kb, _ = w_out.shape
    Fb = w_down.shape[1]
    tps = S // tm

    def body(x_ref, cv_ref, ya_ref, t_ref, mod_ref, gln_ref, bln_ref, gf_ref, wo_hbm, wg_hbm, wu_hbm, wd_hbm,
             x1_ref, ycat_ref, mix_ref, h2_ref, g_ref, u_ref, a_ref, f_ref, dy_ref, loss_ref,
             wo, wg, wu, wd, sems):
        i = pl.program_id(0)
        _load_resident(i, [(wo_hbm, wo), (wg_hbm, wg), (wu_hbm, wu), (wd_hbm, wd)], sems)

        @pl.when(i == 0)
        def _():
            loss_ref[...] = jnp.zeros_like(loss_ref)

        _, _, ln = _layer_norm_parts(cv_ref[...], gln_ref[...], bln_ref[...])
        yc = ln * _sigmoid(ln)
        ycat = jnp.concatenate([yc, ya_ref[...]], axis=1).astype(MXU_DTYPE)
        ycat_ref[...] = ycat
        mix = _dot(ycat[:, 0:Kb], wo[0])
        for p in range(1, P):
            mix = mix + _dot(ycat[:, Kb * p:Kb * (p + 1)], wo[p])
        mix_ref[...] = mix.astype(ACT_DTYPE)
        x1 = x_ref[...] + mod_ref[:, 2 * D:3 * D] * mix
        x1_ref[...] = x1
        r2 = lax.rsqrt(jnp.mean(x1 * x1, axis=-1, keepdims=True) + EPS)
        h2 = (x1 * r2 * gf_ref[...]) * (1.0 + mod_ref[:, 4 * D:5 * D]) + mod_ref[:, 3 * D:4 * D]
        h2b = h2.astype(MXU_DTYPE)
        h2_ref[...] = h2b
        f = jnp.zeros((tm, D), F32)
        for p in range(P):
            g = _dot_nt(h2b, wg[p])
            u = _dot_nt(h2b, wu[p])
            a = (g * _sigmoid(g) * u).astype(MXU_DTYPE)
            g_ref[p] = g.astype(ACT_DTYPE)
            u_ref[p] = u.astype(ACT_DTYPE)
            a_ref[p] = a
            f = f + _dot(a, wd[p])
        f_ref[...] = f.astype(ACT_DTYPE)
        err = x1 + mod_ref[:, 5 * D:6 * D] * f - t_ref[...]
        dy_ref[...] = err * (1.0 / D)
        tot = jnp.sum(_colsum(err * err), axis=1, keepdims=True)
        loss_ref[...] += tot * (0.5 / D)

    row = lambda w: pl.BlockSpec((tm, w), lambda i: (i, 0))
    vec = lambda w: pl.BlockSpec((1, w), lambda i: (0, 0))
    blk = pl.BlockSpec((P, tm, Fb), lambda i: (0, i, 0))
    anyspec = pl.BlockSpec(memory_space=pl.ANY)
    return pl.pallas_call(
        body, grid=(T // tm,), name="ffn_fwd",
        in_specs=[row(D), row(DC), row(D - DC), row(D),
                  pl.BlockSpec((None, 1, N_MOD * D), lambda i: (i // tps, 0, 0)),
                  vec(DC), vec(DC), vec(D), anyspec, anyspec, anyspec, anyspec],
        out_specs=[row(D), row(D), row(D), row(D), blk, blk, blk, row(D), row(D),
                   pl.BlockSpec((8, LANES), lambda i: (0, 0))],
        out_shape=[jax.ShapeDtypeStruct((T, D), F32), jax.ShapeDtypeStruct((T, D), MXU_DTYPE),
                   jax.ShapeDtypeStruct((T, D), ACT_DTYPE), jax.ShapeDtypeStruct((T, D), MXU_DTYPE),
                   jax.ShapeDtypeStruct((P, T, Fb), ACT_DTYPE), jax.ShapeDtypeStruct((P, T, Fb), ACT_DTYPE),
                   jax.ShapeDtypeStruct((P, T, Fb), MXU_DTYPE), jax.ShapeDtypeStruct((T, D), ACT_DTYPE),
                   jax.ShapeDtypeStruct((T, D), F32), jax.ShapeDtypeStruct((8, LANES), F32)],
        scratch_shapes=[pltpu.VMEM(w_out.shape, w_out.dtype), pltpu.VMEM(w_gate.shape, w_gate.dtype),
                        pltpu.VMEM(w_up.shape, w_up.dtype), pltpu.VMEM(w_down.shape, w_down.dtype),
                        pltpu.SemaphoreType.DMA((4,))],
        compiler_params=_cp(("arbitrary",)),
    )(x2, cv, ya, tgt, mod, g_ln, b_ln, g_ffn, w_out, w_gate, w_up, w_down)


def _ffn_bwd(dy, x1, gs, us, fo, mixb, cv, mod, g_ln, b_ln, g_ffn, w_out, w_gate, w_up, w_down, *, S, tm):
    T, D = dy.shape
    DC = cv.shape[1]
    P, Kb, _ = w_out.shape
    Fb = w_down.shape[1]
    tps = S // tm
    Bl = T // S

    def body(dy_ref, x1_ref, g_ref, u_ref, f_ref, mix_ref, cv_ref, mod_ref, gln_ref, bln_ref, gf_ref,
             wo_hbm, wg_hbm, wu_hbm, wd_hbm,
             dg_ref, du_ref, df_ref, dx1_ref, dmix_ref, dya_ref, dcv_ref, macc_ref, gacc_ref, lacc_ref,
             wo, wg, wu, wd, sems):
        i = pl.program_id(0)
        _load_resident(i, [(wo_hbm, wo), (wg_hbm, wg), (wu_hbm, wu), (wd_hbm, wd)], sems)

        @pl.when(i == 0)
        def _():
            gacc_ref[...] = jnp.zeros_like(gacc_ref)
            lacc_ref[...] = jnp.zeros_like(lacc_ref)

        @pl.when(i % tps == 0)
        def _():
            macc_ref[...] = jnp.zeros_like(macc_ref)

        dy_t = dy_ref[...]
        x1 = x1_ref[...]
        gate_f = mod_ref[:, 5 * D:6 * D]
        macc_ref[2:3, :] += _colsum(dy_t * f_ref[...].astype(F32))
        dfb = (dy_t * gate_f).astype(MXU_DTYPE)
        df_ref[...] = dfb
        dh2 = jnp.zeros((tm, D), F32)
        for p in range(P):
            da = _dot_nt(dfb, wd[p])
            g = g_ref[p].astype(F32)
            u = u_ref[p].astype(F32)
            sg = _sigmoid(g)
            dgp = (da * u * (sg * (1.0 + g * (1.0 - sg)))).astype(MXU_DTYPE)
            dup = (da * (g * sg)).astype(MXU_DTYPE)
            dg_ref[p] = dgp
            du_ref[p] = dup
            dh2 = dh2 + _dot(dgp, wg[p]) + _dot(dup, wu[p])
        r2 = lax.rsqrt(jnp.mean(x1 * x1, axis=-1, keepdims=True) + EPS)
        xr = x1 * r2
        n2 = xr * gf_ref[...]
        macc_ref[0:1, :] += _colsum(dh2)
        macc_ref[1:2, :] += _colsum(dh2 * n2)
        dn2 = dh2 * (1.0 + mod_ref[:, 4 * D:5 * D])
        gacc_ref[0:1, :] += _colsum(dn2 * xr)
        e = dn2 * gf_ref[...]
        dx1 = dy_t + r2 * e - xr * (r2 * jnp.mean(e * xr, axis=-1, keepdims=True))
        dx1_ref[...] = dx1
        macc_ref[3:4, :] += _colsum(dx1 * mix_ref[...].astype(F32))
        dmixb = (dx1 * mod_ref[:, 2 * D:3 * D]).astype(MXU_DTYPE)
        dmix_ref[...] = dmixb
        parts = [_dot_nt(dmixb, wo[p]) for p in range(P)]
        dycat = jnp.concatenate(parts, axis=1) if P > 1 else parts[0]
        dya_ref[...] = dycat[:, DC:]
        dyc = dycat[:, :DC]
        z, rs, ln = _layer_norm_parts(cv_ref[...], gln_ref[...], bln_ref[...])
        sg = _sigmoid(ln)
        dln = dyc * (sg * (1.0 + ln * (1.0 - sg)))
        lacc_ref[0:1, :] += _colsum(dln * z)
        lacc_ref[1:2, :] += _colsum(dln)
        dz = dln * gln_ref[...]
        dcv_ref[...] = rs * (dz - jnp.mean(dz, axis=-1, keepdims=True) - z * jnp.mean(dz * z, axis=-1, keepdims=True))

    row = lambda w: pl.BlockSpec((tm, w), lambda i: (i, 0))
    vec = lambda w: pl.BlockSpec((1, w), lambda i: (0, 0))
    blk = pl.BlockSpec((P, tm, Fb), lambda i: (0, i, 0))
    anyspec = pl.BlockSpec(memory_space=pl.ANY)
    return pl.pallas_call(
        body, grid=(T // tm,), name="ffn_bwd",
        in_specs=[row(D), row(D), blk, blk, row(D), row(D), row(DC),
                  pl.BlockSpec((None, 1, N_MOD * D), lambda i: (i // tps, 0, 0)),
                  vec(DC), vec(DC), vec(D), anyspec, anyspec, anyspec, anyspec],
        out_specs=[blk, blk, row(D), row(D), row(D), row(D - DC), row(DC),
                   pl.BlockSpec((None, 8, D), lambda i: (i // tps, 0, 0)),
                   pl.BlockSpec((8, D), lambda i: (0, 0)), pl.BlockSpec((8, DC), lambda i: (0, 0))],
        out_shape=[jax.ShapeDtypeStruct((P, T, Fb), MXU_DTYPE), jax.ShapeDtypeStruct((P, T, Fb), MXU_DTYPE),
                   jax.ShapeDtypeStruct((T, D), MXU_DTYPE), jax.ShapeDtypeStruct((T, D), F32),
                   jax.ShapeDtypeStruct((T, D), MXU_DTYPE), jax.ShapeDtypeStruct((T, D - DC), F32),
                   jax.ShapeDtypeStruct((T, DC), F32), jax.ShapeDtypeStruct((Bl, 8, D), F32),
                   jax.ShapeDtypeStruct((8, D), F32), jax.ShapeDtypeStruct((8, DC), F32)],
        scratch_shapes=[pltpu.VMEM(w_out.shape, w_out.dtype), pltpu.VMEM(w_gate.shape, w_gate.dtype),
                        pltpu.VMEM(w_up.shape, w_up.dtype), pltpu.VMEM(w_down.shape, w_down.dtype),
                        pltpu.SemaphoreType.DMA((4,))],
        compiler_params=_cp(("arbitrary",)),
    )(dy, x1, gs, us, fo, mixb, cv, mod, g_ln, b_ln, g_ffn, w_out, w_gate, w_up, w_down)


def _in_bwd(da, dg, dq, dk, dv, x2, dx1, mod, g_mix, w_in, *, S, tm):
    T, D = x2.shape
    P, _, Nb = w_in.shape
    DC = da.shape[1]
    NS = dq.shape[0]
    n_in = P * Nb
    tps = S // tm
    Bl = T // S

    def body(da_ref, dg_ref, dq_ref, dk_ref, dv_ref, x_ref, dx1_ref, mod_ref, g_ref, w_ref,
             dx_ref, dproj_ref, macc_ref, gacc_ref):
        i = pl.program_id(0)

        @pl.when(i == 0)
        def _():
            gacc_ref[...] = jnp.zeros_like(gacc_ref)

        @pl.when(i % tps == 0)
        def _():
            macc_ref[...] = jnp.zeros_like(macc_ref)

        pieces = [da_ref[...], dg_ref[...]] + [r[j] for r in (dq_ref, dk_ref, dv_ref) for j in range(NS)]
        dproj = jnp.concatenate(pieces, axis=1).astype(MXU_DTYPE)
        dproj_ref[...] = dproj
        dh = _dot_nt(dproj[:, 0:Nb], w_ref[0])
        for p in range(1, P):
            dh = dh + _dot_nt(dproj[:, Nb * p:Nb * (p + 1)], w_ref[p])
        x = x_ref[...]
        r = lax.rsqrt(jnp.mean(x * x, axis=-1, keepdims=True) + EPS)
        xr = x * r
        macc_ref[0:1, :] += _colsum(dh)
        macc_ref[1:2, :] += _colsum(dh * (xr * g_ref[...]))
        dn = dh * (1.0 + mod_ref[:, D:2 * D])
        gacc_ref[0:1, :] += _colsum(dn * xr)
        e = dn * g_ref[...]
        dx_ref[...] = dx1_ref[...] + r * e - xr * (r * jnp.mean(e * xr, axis=-1, keepdims=True))

    row = lambda w: pl.BlockSpec((tm, w), lambda i: (i, 0))
    slab = pl.BlockSpec((NS, tm, LANES), lambda i: (0, i, 0))
    return pl.pallas_call(
        body, grid=(T // tm,), name="in_bwd",
        in_specs=[row(DC), row(DC), slab, slab, slab, row(D), row(D),
                  pl.BlockSpec((None, 1, N_MOD * D), lambda i: (i // tps, 0, 0)),
                  pl.BlockSpec((1, D), lambda i: (0, 0)),
                  pl.BlockSpec((P, D, Nb), lambda i: (0, 0, 0))],
        out_specs=[row(D), row(n_in), pl.BlockSpec((None, 8, D), lambda i: (i // tps, 0, 0)),
                   pl.BlockSpec((8, D), lambda i: (0, 0))],
        out_shape=[jax.ShapeDtypeStruct((T, D), F32), jax.ShapeDtypeStruct((T, n_in), MXU_DTYPE),
                   jax.ShapeDtypeStruct((Bl, 8, D), F32), jax.ShapeDtypeStruct((8, D), F32)],
        compiler_params=_cp(("arbitrary",)),
    )(da, dg, dq, dk, dv, x2, dx1, mod, g_mix, w_in)


def _wgrad(a, b, *, P, name, tk, split=None, host=None):
    a_blk, b_blk = a.ndim == 3, b.ndim == 3
    plan, h_in, h_out = host if host is not None else (None, (), ())
    ni, no = len(h_in), len(h_out)
    T = a.shape[-2]
    if a_blk:
        R, C = a.shape[2], b.shape[1]
        a_of = lambda av, p: av[p]
        b_of = lambda bv, p: bv[...]
    elif b_blk:
        R, C = a.shape[1], b.shape[2]
        a_of = lambda av, p: av[...]
        b_of = lambda bv, p: bv[p]
    elif split == "a":
        R, C = a.shape[1] // P, b.shape[1]
        a_of = lambda av, p: av[:, R * p:R * (p + 1)]
        b_of = lambda bv, p: bv[...]
    else:
        R, C = a.shape[1], b.shape[1] // P
        a_of = lambda av, p: av[...]
        b_of = lambda bv, p: bv[:, C * p:C * (p + 1)]

    n_steps = T // tk

    def body(a_ref, b_ref, *rest):
        hin, o_ref, hout, sems = rest[:ni], rest[ni], rest[ni + 1:ni + 1 + no], rest[ni + 1 + no:]
        step = pl.program_id(0)

        @pl.when(step == 0)
        def _():
            o_ref[...] = jnp.zeros_like(o_ref)
            if plan is not None:
                plan.start(hin, hout, sems)

        if plan is not None:
            @pl.when(step == n_steps // 2)
            def _():
                plan.forward(hin, hout, sems)

        for p in range(P):
            o_ref[p] += _dot_tn(a_of(a_ref, p), b_of(b_ref, p))

        if plan is not None:
            @pl.when(step == n_steps - 1)
            def _():
                plan.finish(hin, hout, sems)

    def spec(v):
        if v.ndim == 3:
            return pl.BlockSpec((P, tk, v.shape[2]), lambda k: (0, k, 0))
        return pl.BlockSpec((tk, v.shape[1]), lambda k: (k, 0))

    anyspec = pl.BlockSpec(memory_space=pl.ANY)
    res = pl.pallas_call(
        body, grid=(n_steps,), name=name,
        in_specs=[spec(a), spec(b)] + [anyspec] * ni,
        out_specs=[pl.BlockSpec((P, R, C), lambda k: (0, 0, 0))] + [anyspec] * no,
        out_shape=[jax.ShapeDtypeStruct((P, R, C), F32)] + list(h_out),
        scratch_shapes=plan.scratch() if plan is not None else [],
        compiler_params=_cp(("arbitrary",)),
    )(a, b, *h_in)
    return res if plan is not None else res[0]


TM_IN = 512
TM_FFN = 256
TK_WGRAD = 512


def _alibi_slabs(n_slab):
    heads = 2 * n_slab
    slopes = 2.0 ** (-8.0 * np.arange(1, heads + 1) / heads)
    return jnp.asarray(np.broadcast_to(np.repeat(slopes.reshape(n_slab, 1, 2), HEAD_DIM, axis=2), (n_slab, 8, LANES)),
                       dtype=F32)


def _local_step(x, tgt, mod, g_mix, wdw, g_ln, b_ln, g_q, g_k, g_ffn, w_in, w_out, w_gate, w_up, w_down,
                cidx=None):
    Bl, S, D = x.shape
    T = Bl * S
    DC = g_ln.shape[1]
    P = w_in.shape[0]
    n_slab = (D - DC) // LANES
    x2 = x.reshape(T, D)
    t2 = tgt.reshape(T, D)
    mod3 = mod.reshape(Bl, 1, N_MOD * D)
    gq2 = jnp.tile(g_q, (1, LANES // HEAD_DIM))
    gk2 = jnp.tile(g_k, (1, LANES // HEAD_DIM))
    slopes = _alibi_slabs(n_slab)

    ag, qkv, qkh, h1 = _fwd_in(x2, mod3, g_mix, gq2, gk2, w_in, S=S, tm=TM_IN, n_ag=2 * DC)
    cv = _conv_fwd(ag, wdw, Bl=Bl, S=S, DC=DC)
    if cidx is not None:
        ya, lse, w_out, w_gate, w_up, w_down = _attn_fwd(qkh, qkv, slopes, Bl=Bl, S=S,
                                                         hosted=(w_out, w_gate, w_up, w_down))
    else:
        ya, lse = _attn_fwd(qkh, qkv, slopes, Bl=Bl, S=S)
    x1, ycat, mixb, h2, gs, us, acts, fo, dy, lossb = _ffn_fwd(
        x2, cv, ya, t2, mod3, g_ln, b_ln, g_ffn, w_out, w_gate, w_up, w_down, S=S, tm=TM_FFN)
    dgs, dus, dfb, dx1, dmixb, dya, dcv, macc_f, gacc_f, lacc = _ffn_bwd(
        dy, x1, gs, us, fo, mixb, cv, mod3, g_ln, b_ln, g_ffn, w_out, w_gate, w_up, w_down, S=S, tm=TM_FFN)
    wg = functools.partial(_wgrad, P=P, tk=TK_WGRAD)
    out = {}
    if cidx is None:
        grads = dict(w_down=wg(acts, dfb, name="wgrad_down"), w_gate=wg(dgs, h2, name="wgrad_gate"),
                     w_up=wg(dus, h2, name="wgrad_up"), w_out=wg(ycat, dmixb, name="wgrad_out", split="a"))
        dq, dk, dv, gqk = _attn_bwd(qkh, qkv, ya, lse, dya, gq2, gk2, slopes, Bl=Bl, S=S)
    else:
        g_down = wg(acts, dfb, name="wgrad_down")
        g_gate, r_down = wg(dgs, h2, name="wgrad_gate", host=_sibling_host([g_down]))
        g_up, r_gate = wg(dus, h2, name="wgrad_up", host=_sibling_host([g_gate]))
        g_out, r_up = wg(ycat, dmixb, name="wgrad_out", split="a", host=_sibling_host([g_up]))
        (r_out,) = _rs_sibling([g_out], "rs_sibling_out")
        grads = dict(w_down=g_down, w_gate=g_gate, w_up=g_up, w_out=g_out)
        sums = [_pair_add(grads[nm], r, cidx, "pair_add_" + nm)
                for nm, r in zip(EARLY_WEIGHTS, (r_down, r_gate, r_up, r_out))]
        res = _attn_bwd(qkh, qkv, ya, lse, dya, gq2, gk2, slopes, Bl=Bl, S=S, hosted=tuple(sb for _, sb in sums))
        dq, dk, dv, gqk = res[:4]
        out["early_sums"] = [s32 for s32, _ in sums]
        out["early_recv"] = list(res[4:])
    da, dg, dwdw = _conv_bwd(ag, dcv, wdw, Bl=Bl, S=S, DC=DC)
    dx, dprojb, macc_m, gacc_m = _in_bwd(da, dg, dq, dk, dv, x2, dx1, mod3, g_mix, w_in, S=S, tm=TM_IN)
    packed = _pack_small(macc_m, macc_f, gacc_m, gacc_f, lacc, gqk, dwdw, lossb)
    if cidx is None:
        grads["w_in"] = wg(h1, dprojb, name="wgrad_in", split="b")
    else:
        grads["w_in"], out["gathered_small"] = wg(h1, dprojb, name="wgrad_in", split="b",
                                                  host=_small_gather_host(packed))
    out.update(dx=dx.reshape(Bl, S, D), grads=grads, packed=packed)
    return out


EARLY_WEIGHTS = ("w_down", "w_gate", "w_up", "w_out")


def _small_layout(Bl):
    return 8 * Bl, 8 * Bl + 8, 8 * Bl + 8 + CONV_ROWS


def _pack_small(macc_m, macc_f, gacc_m, gacc_f, lacc, gqk, dwdw, lossb):
    Bl, _, D = macc_m.shape
    DC = lacc.shape[1]
    assert 2 * DC <= D
    SMALL_GAIN_ROW, SMALL_TAP_ROW, SMALL_ROWS = _small_layout(Bl)

    def body(mm_ref, mf_ref, gm_ref, gf_ref, la_ref, qk_ref, dw_ref, loss_ref, o_ref):
        o_ref[...] = jnp.zeros_like(o_ref)
        for b in range(Bl):
            o_ref[8 * b + 0:8 * b + 2, :] = mm_ref[b, 0:2, :]
            o_ref[8 * b + 2:8 * b + 3, :] = mf_ref[b, 3:4, :]
            o_ref[8 * b + 3:8 * b + 6, :] = mf_ref[b, 0:3, :]
        r = SMALL_GAIN_ROW
        o_ref[r:r + 1, :] = gm_ref[0:1, :]
        o_ref[r + 1:r + 2, :] = gf_ref[0:1, :]
        o_ref[r + 2:r + 3, 0:DC] = la_ref[0:1, :]
        o_ref[r + 2:r + 3, DC:2 * DC] = la_ref[1:2, :]
        qk = qk_ref[0:2, 0:HEAD_DIM] + qk_ref[0:2, HEAD_DIM:2 * HEAD_DIM]
        o_ref[r + 3:r + 4, 0:HEAD_DIM] = qk[0:1, :]
        o_ref[r + 3:r + 4, HEAD_DIM:2 * HEAD_DIM] = qk[1:2, :]
        o_ref[r + 4:r + 5, 0:LANES] = loss_ref[0:1, :]
        o_ref[SMALL_TAP_ROW:SMALL_TAP_ROW + CONV_ROWS, 0:DC] = dw_ref[...]

    return pl.pallas_call(body, name="pack_small", out_shape=jax.ShapeDtypeStruct((SMALL_ROWS, D), F32),
                          compiler_params=_cp())(macc_m, macc_f, gacc_m, gacc_f, lacc, gqk, dwdw, lossb)


def _row_tile(rows, cap=512):
    if rows <= cap:
        return rows
    best = rows
    for t in range(8, cap + 1, 8):
        if rows % t == 0:
            best = t
    return best


def _cast_weight(w, pidx, name):
    def body(p_ref, w_ref, o_ref):
        o_ref[...] = w_ref[...].astype(MXU_DTYPE)
    R, C = w.shape
    tr = _row_tile(R)
    return pl.pallas_call(
        body, name=name,
        grid_spec=pltpu.PrefetchScalarGridSpec(
            num_scalar_prefetch=1, grid=(R // tr,),
            in_specs=[pl.BlockSpec((tr, C), lambda i, p: (i, 0))],
            out_specs=pl.BlockSpec((None, tr, C), lambda i, p: (p[0], i, 0))),
        out_shape=jax.ShapeDtypeStruct((4, R, C), MXU_DTYPE),
    )(pidx, w)


def _pair_add(g, recv, cidx, name):
    P, R, C = g.shape
    R2 = R // 2

    def body(c_ref, g_ref, r_ref, o_ref, ob_ref):
        s = g_ref[...] + r_ref[...]
        o_ref[...] = s
        ob_ref[...] = s.astype(jnp.bfloat16)

    return pl.pallas_call(
        body, name=name,
        grid_spec=pltpu.PrefetchScalarGridSpec(
            num_scalar_prefetch=1, grid=(P,),
            in_specs=[pl.BlockSpec((None, R2, C), lambda p, c: (p, c[0], 0)),
                      pl.BlockSpec((None, R2, C), lambda p, c: (p, 0, 0))],
            out_specs=[pl.BlockSpec((None, R2, C), lambda p, c: (p, 0, 0)),
                       pl.BlockSpec((None, R2, C), lambda p, c: (p, 0, 0))]),
        out_shape=[jax.ShapeDtypeStruct((P, R2, C), F32), jax.ShapeDtypeStruct((P, R2, C), jnp.bfloat16)],
    )(cidx, g, recv)


def _final_add(chipsum, recv, pc_idx, name):
    P, R2, C = chipsum.shape

    def body(pc_ref, s_ref, r_ref, o_ref):
        acc = s_ref[...]
        for k in range(3):
            acc = acc + r_ref[k].astype(F32)
        o_ref[...] = acc

    return pl.pallas_call(
        body, name=name,
        grid_spec=pltpu.PrefetchScalarGridSpec(
            num_scalar_prefetch=1, grid=(1,),
            in_specs=[pl.BlockSpec((None, R2, C), lambda i, pc: (pc[0], 0, 0)),
                      pl.BlockSpec((3, R2, C), lambda i, pc: (0, 0, 0))],
            out_specs=pl.BlockSpec((R2, C), lambda i, pc: (pc[1], 0))),
        out_shape=jax.ShapeDtypeStruct((2 * R2, C), F32),
    )(pc_idx, chipsum, recv)


def _adamw_update(w_ref, g_ref, m_ref, v_ref, d_ref, nm_ref, nv_ref):
    c1 = 1.0 - ADAM_B1 ** ADAM_STEP
    c2 = 1.0 - ADAM_B2 ** ADAM_STEP
    gg = g_ref[...]
    nm = ADAM_B1 * m_ref[...] + (1.0 - ADAM_B1) * gg
    nv = ADAM_B2 * v_ref[...] + (1.0 - ADAM_B2) * (gg * gg)
    nm_ref[...] = nm
    nv_ref[...] = nv
    d_ref[...] = -ADAM_LR * ((nm / c1) / (jnp.sqrt(nv / c2) + ADAM_EPS) + ADAM_WD * w_ref[...])


def _adamw(w, g, m, v, name):
    R, C = w.shape
    tr = _row_tile(R, 256)
    spec = pl.BlockSpec((tr, C), lambda i: (i, 0))
    return pl.pallas_call(
        _adamw_update_fn(), grid=(R // tr,), name=name,
        in_specs=[spec] * 4, out_specs=[spec] * 3,
        out_shape=[jax.ShapeDtypeStruct((R, C), F32)] * 3,
    )(w, g, m, v)


def _adamw_update_fn():
    return functools.partial(_adamw_update)


ADAMW_STEPS = 8


def _adamw_many(ws, gs, ms, vs, name, host=None):
    n = len(ws)
    plan, h_in, h_out = host if host is not None else (None, (), ())
    ni, no = len(h_in), len(h_out)
    tiles = [w.shape[0] // ADAMW_STEPS for w in ws]
    assert all(t % 8 == 0 and t * ADAMW_STEPS == w.shape[0] for t, w in zip(tiles, ws))

    def body(*refs):
        ins = refs[:4 * n]
        hin = refs[4 * n:4 * n + ni]
        outs = refs[4 * n + ni:7 * n + ni]
        hout = refs[7 * n + ni:7 * n + ni + no]
        sems = refs[7 * n + ni + no:]
        step = pl.program_id(0)
        if plan is not None:
            @pl.when(step == 0)
            def _():
                plan.start(hin, hout, sems)

            @pl.when(step == ADAMW_STEPS // 2)
            def _():
                plan.forward(hin, hout, sems)

        for k in range(n):
            _adamw_update(ins[k], ins[n + k], ins[2 * n + k], ins[3 * n + k],
                          outs[3 * k], outs[3 * k + 1], outs[3 * k + 2])

        if plan is not None:
            @pl.when(step == ADAMW_STEPS - 1)
            def _():
                plan.finish(hin, hout, sems)

    specs = [pl.BlockSpec((t, w.shape[1]), lambda i: (i, 0)) for t, w in zip(tiles, ws)]
    anyspec = pl.BlockSpec(memory_space=pl.ANY)
    res = pl.pallas_call(
        body, grid=(ADAMW_STEPS,), name=name,
        in_specs=specs * 4 + [anyspec] * ni,
        out_specs=[s for s in specs for _ in range(3)] + [anyspec] * no,
        out_shape=[jax.ShapeDtypeStruct(w.shape, F32) for w in ws for _ in range(3)] + list(h_out),
        scratch_shapes=plan.scratch() if plan is not None else [],
        compiler_params=_cp(("arbitrary",)),
    )(*ws, *gs, *ms, *vs, *h_in)
    return [tuple(res[3 * k:3 * k + 3]) for k in range(n)], list(res[3 * n:])


def _ada_fwd(c_all, w_ada, b_cols):
    def body(c_ref, w_ref, b_ref, o_ref):
        c = c_ref[...]
        o_ref[...] = jnp.dot(c * _sigmoid(c), w_ref[...], preferred_element_type=F32, precision=HIGHEST) + b_ref[...]
    return pl.pallas_call(
        body, name="ada_fwd", out_shape=jax.ShapeDtypeStruct((c_all.shape[0], w_ada.shape[1]), F32),
        compiler_params=_cp(),
    )(c_all, w_ada, b_cols)


def _ada_bwd(c_all, dmod_cols):
    def body(c_ref, d_ref, o_ref):
        c = c_ref[...]
        o_ref[...] = lax.dot_general(c * _sigmoid(c), d_ref[...], (((0,), (0,)), ((), ())),
                                     preferred_element_type=F32, precision=HIGHEST)
    return pl.pallas_call(
        body, name="ada_bwd", out_shape=jax.ShapeDtypeStruct((c_all.shape[1], dmod_cols.shape[1]), F32),
        compiler_params=_cp(),
    )(c_all, dmod_cols)


def _small_reduce(gathered, n_dev, Bl):
    mod_rows, _, rows = _small_layout(Bl)
    width = gathered.shape[1]

    def body(g_ref, red_ref, bada_ref):
        acc = g_ref[0:rows, :]
        for d in range(1, n_dev):
            acc = acc + g_ref[d * rows:(d + 1) * rows, :]
        red_ref[...] = acc[mod_rows:, :]
        b = acc[0:8, :]
        for q in range(1, Bl):
            b = b + acc[8 * q:8 * q + 8, :]
        bada_ref[...] = b
    return pl.pallas_call(
        body, name="small_reduce",
        out_shape=[jax.ShapeDtypeStruct((rows - mod_rows, width), F32), jax.ShapeDtypeStruct((8, width), F32)],
        compiler_params=_cp(),
    )(gathered)


def _mesh_pos():
    return lax.axis_index("x"), lax.axis_index("y"), lax.axis_index("c")


def _other_chips(x, y):
    return [(1 - x, y), (x, 1 - y), (1 - x, 1 - y)]


def _allgather8(xs, name):
    m_per, n = xs.shape

    def body(x_ref, out_ref, send_sems, recv_sems, local_sem):
        x, y, c = _mesh_pos()
        me, sibling = (x, y, c), (x, y, 1 - c)
        chips = _other_chips(x, y)

        def rows(px, py, pc):
            return out_ref.at[pl.ds((4 * px + 2 * py + pc) * m_per, m_per), :]

        def copy(k, block, to, src=None):
            return pltpu.make_async_remote_copy(
                src_ref=rows(*block) if src is None else src, dst_ref=rows(*block),
                send_sem=send_sems.at[k], recv_sem=recv_sems.at[k], device_id=to, device_id_type=MESH_DEV)

        mine = pltpu.make_async_copy(x_ref, rows(*me), local_sem)
        mine.start()
        first = [copy(0, me, sibling, src=x_ref)]
        first += [copy(1 + j, me, (*chip, c), src=x_ref) for j, chip in enumerate(chips)]
        for cp in first:
            cp.start()
        passed = [copy(4 + j, (*chip, c), sibling) for j, chip in enumerate(chips)]
        for j, chip in enumerate(chips):
            copy(1 + j, (*chip, c), me).wait_recv()
            passed[j].start()
        copy(0, sibling, me).wait_recv()
        for j, chip in enumerate(chips):
            copy(4 + j, (*chip, 1 - c), me).wait_recv()
        for cp in first + passed:
            cp.wait_send()
        mine.wait()

    return pl.pallas_call(
        body, name=name, out_shape=jax.ShapeDtypeStruct((8 * m_per, n), xs.dtype),
        in_specs=[pl.BlockSpec(memory_space=pltpu.VMEM)], out_specs=pl.BlockSpec(memory_space=pltpu.VMEM),
        scratch_shapes=[pltpu.SemaphoreType.DMA((7,)), pltpu.SemaphoreType.DMA((7,)), pltpu.SemaphoreType.DMA],
        compiler_params=_cp(),
    )(xs)


class _WeightGather:
    def __init__(self, shapes):
        self.shapes = shapes
        self.n = len(shapes)

    def scratch(self):
        return [pltpu.SemaphoreType.DMA((6 * self.n,)), pltpu.SemaphoreType.DMA((6 * self.n,))]

    def _copy(self, outs, sems, w, k, slot, h, to):
        r2 = self.shapes[w][1] // 2
        blk = outs[w].at[slot, pl.ds(h * r2, r2), :]
        return pltpu.make_async_remote_copy(
            src_ref=blk, dst_ref=blk, send_sem=sems[0].at[6 * w + k], recv_sem=sems[1].at[6 * w + k],
            device_id=to, device_id_type=MESH_DEV)

    def start(self, outs, sems):
        x, y, c = _mesh_pos()
        for w in range(self.n):
            for k, chip in enumerate(_other_chips(x, y)):
                self._copy(outs, sems, w, k, 2 * x + y, c, (*chip, c)).start()

    def forward(self, outs, sems):
        x, y, c = _mesh_pos()
        for w in range(self.n):
            for k, chip in enumerate(_other_chips(x, y)):
                slot = 2 * chip[0] + chip[1]
                self._copy(outs, sems, w, k, slot, c, (x, y, 1 - c)).wait_recv()
                self._copy(outs, sems, w, 3 + k, slot, c, (x, y, 1 - c)).start()

    def finish(self, outs, sems):
        x, y, c = _mesh_pos()
        for w in range(self.n):
            for k, chip in enumerate(_other_chips(x, y)):
                slot = 2 * chip[0] + chip[1]
                self._copy(outs, sems, w, 3 + k, slot, 1 - c, (x, y, 1 - c)).wait_recv()
                self._copy(outs, sems, w, k, 2 * x + y, c, (*chip, c)).wait_send()
                self._copy(outs, sems, w, 3 + k, slot, c, (x, y, 1 - c)).wait_send()


def _gather_weights(bufs, name):
    n = len(bufs)
    plan = _WeightGather([b.shape for b in bufs])

    def body(*refs):
        outs = refs[n:2 * n]
        sems = refs[2 * n:]
        plan.start(outs, sems)
        plan.forward(outs, sems)
        plan.finish(outs, sems)

    anyspec = pl.BlockSpec(memory_space=pl.ANY)
    return pl.pallas_call(
        body, name=name,
        out_shape=[jax.ShapeDtypeStruct(b.shape, b.dtype) for b in bufs],
        in_specs=[anyspec] * n, out_specs=[anyspec] * n,
        input_output_aliases={w: w for w in range(n)},
        scratch_shapes=plan.scratch(),
    )(*bufs)


class _SiblingExchange:
    def __init__(self, shapes):
        self.shapes = shapes

    def scratch(self):
        n = sum(s[0] for s in self.shapes)
        return [pltpu.SemaphoreType.DMA((n,)), pltpu.SemaphoreType.DMA((n,))]

    def out_shapes(self, dtype):
        return [jax.ShapeDtypeStruct((s[0], s[1] // 2, s[2]), dtype) for s in self.shapes]

    def _copies(self, ins, outs, sems):
        x, y, c = _mesh_pos()
        cps, k = [], 0
        for w, (P, R, _) in enumerate(self.shapes):
            r2 = R // 2
            for p in range(P):
                cps.append(pltpu.make_async_remote_copy(
                    src_ref=ins[w].at[p, pl.ds((1 - c) * r2, r2), :], dst_ref=outs[w].at[p],
                    send_sem=sems[0].at[k], recv_sem=sems[1].at[k],
                    device_id=(x, y, 1 - c), device_id_type=MESH_DEV))
                k += 1
        return cps

    def start(self, ins, outs, sems):
        for cp in self._copies(ins, outs, sems):
            cp.start()

    def forward(self, ins, outs, sems):
        pass

    def finish(self, ins, outs, sems):
        for cp in self._copies(ins, outs, sems):
            cp.wait()


def _sibling_host(grads):
    plan = _SiblingExchange([g.shape for g in grads])
    return plan, tuple(grads), tuple(plan.out_shapes(grads[0].dtype))


def _rs_sibling(grads, name):
    n = len(grads)
    plan, _, out_shapes = _sibling_host(grads)

    def body(*refs):
        ins, outs, sems = refs[:n], refs[n:2 * n], refs[2 * n:]
        plan.start(ins, outs, sems)
        plan.finish(ins, outs, sems)

    anyspec = pl.BlockSpec(memory_space=pl.ANY)
    return pl.pallas_call(
        body, name=name, out_shape=list(out_shapes),
        in_specs=[anyspec] * n, out_specs=[anyspec] * n, scratch_shapes=plan.scratch(),
    )(*grads)


class _SmallGather:
    def __init__(self, m_per):
        self.m = m_per

    def scratch(self):
        return [pltpu.SemaphoreType.DMA((7,)), pltpu.SemaphoreType.DMA((7,)), pltpu.SemaphoreType.DMA]

    def _rows(self, out, pos):
        px, py, pc = pos
        return out.at[pl.ds((4 * px + 2 * py + pc) * self.m, self.m), :]

    def _copy(self, out, sems, k, block, to, src=None):
        dst = self._rows(out, block)
        return pltpu.make_async_remote_copy(
            src_ref=dst if src is None else src, dst_ref=dst, send_sem=sems[0].at[k], recv_sem=sems[1].at[k],
            device_id=to, device_id_type=MESH_DEV)

    def start(self, ins, outs, sems):
        x, y, c = _mesh_pos()
        me = (x, y, c)
        pltpu.make_async_copy(ins[0], self._rows(outs[0], me), sems[2]).start()
        self._copy(outs[0], sems, 0, me, (x, y, 1 - c), src=ins[0]).start()
        for j, chip in enumerate(_other_chips(x, y)):
            self._copy(outs[0], sems, 1 + j, me, (*chip, c), src=ins[0]).start()

    def forward(self, ins, outs, sems):
        x, y, c = _mesh_pos()
        for j, chip in enumerate(_other_chips(x, y)):
            self._copy(outs[0], sems, 1 + j, (*chip, c), (x, y, c)).wait_recv()
            self._copy(outs[0], sems, 4 + j, (*chip, c), (x, y, 1 - c)).start()

    def finish(self, ins, outs, sems):
        x, y, c = _mesh_pos()
        me = (x, y, c)
        self._copy(outs[0], sems, 0, (x, y, 1 - c), me).wait_recv()
        for j, chip in enumerate(_other_chips(x, y)):
            self._copy(outs[0], sems, 4 + j, (*chip, 1 - c), me).wait_recv()
        self._copy(outs[0], sems, 0, me, (x, y, 1 - c), src=ins[0]).wait_send()
        for j, chip in enumerate(_other_chips(x, y)):
            self._copy(outs[0], sems, 1 + j, me, (*chip, c), src=ins[0]).wait_send()
            self._copy(outs[0], sems, 4 + j, (*chip, c), (x, y, 1 - c)).wait_send()
        pltpu.make_async_copy(ins[0], self._rows(outs[0], me), sems[2]).wait()


def _small_gather_host(packed):
    m, n = packed.shape
    return _SmallGather(m), (packed,), (jax.ShapeDtypeStruct((8 * m, n), packed.dtype),)


class _ChipExchange:
    def __init__(self, n):
        self.n = n

    def scratch(self):
        return [pltpu.SemaphoreType.DMA((3 * self.n,)), pltpu.SemaphoreType.DMA((3 * self.n,))]

    def _copies(self, ins, outs, sems):
        x, y, c = _mesh_pos()
        return [pltpu.make_async_remote_copy(
            src_ref=ins[w].at[2 * chip[0] + chip[1]], dst_ref=outs[w].at[k],
            send_sem=sems[0].at[3 * w + k], recv_sem=sems[1].at[3 * w + k],
            device_id=(*chip, c), device_id_type=MESH_DEV)
            for w in range(self.n) for k, chip in enumerate(_other_chips(x, y))]

    def start(self, ins, outs, sems):
        for cp in self._copies(ins, outs, sems):
            cp.start()

    def forward(self, ins, outs, sems):
        pass

    def finish(self, ins, outs, sems):
        for cp in self._copies(ins, outs, sems):
            cp.wait()


def _rs_chips(sums, name):
    n = len(sums)
    plan = _ChipExchange(n)

    def body(*refs):
        ins, outs, sems = refs[:n], refs[n:2 * n], refs[2 * n:]
        plan.start(ins, outs, sems)
        plan.finish(ins, outs, sems)

    anyspec = pl.BlockSpec(memory_space=pl.ANY)
    return pl.pallas_call(
        body, name=name,
        out_shape=[jax.ShapeDtypeStruct((3,) + s.shape[1:], s.dtype) for s in sums],
        in_specs=[anyspec] * n, out_specs=[anyspec] * n,
        scratch_shapes=plan.scratch(),
    )(*sums)


def _rs_final(bufs, name):
    n = len(bufs)

    def body(*refs):
        outs = refs[n:2 * n]
        send_sems, recv_sems = refs[2 * n:]
        x, y, c = _mesh_pos()
        cps = []
        for w in range(n):
            r2 = bufs[w].shape[0] // 2
            mine = outs[w].at[pl.ds(c * r2, r2), :]
            cps.append(pltpu.make_async_remote_copy(
                src_ref=mine, dst_ref=mine, send_sem=send_sems.at[w], recv_sem=recv_sems.at[w],
                device_id=(x, y, 1 - c), device_id_type=MESH_DEV))
            cps[-1].start()
        for cp in cps:
            cp.wait()

    anyspec = pl.BlockSpec(memory_space=pl.ANY)
    return pl.pallas_call(
        body, name=name,
        out_shape=[jax.ShapeDtypeStruct(b.shape, b.dtype) for b in bufs],
        in_specs=[anyspec] * n, out_specs=[anyspec] * n,
        input_output_aliases={w: w for w in range(n)},
        scratch_shapes=[pltpu.SemaphoreType.DMA((n,)), pltpu.SemaphoreType.DMA((n,))],
    )(*bufs)


BIG = ("w_in", "w_out", "w_gate", "w_up", "w_down")
TRANSPOSED = ("w_gate", "w_up")
WEIGHTS = ("w_ada", "b_ada", "g_mix", "w_in", "w_dw", "b_dw", "g_conv_ln", "b_conv_ln", "g_q", "g_k",
           "w_out", "g_ffn", "w_gate", "w_up", "w_down")


def _pad_to(a, rows, cols):
    return jnp.pad(a, ((0, rows - a.shape[0]), (0, cols - a.shape[1])))


def kernel(x, c, w_ada, b_ada, g_mix, w_in, w_dw, b_dw, g_conv_ln, b_conv_ln, g_q, g_k, w_out, g_ffn, w_gate, w_up, w_down, loss_target, m_w_ada, m_b_ada, m_g_mix, m_w_in, m_w_dw, m_b_dw, m_g_conv_ln, m_b_conv_ln, m_g_q, m_g_k, m_w_out, m_g_ffn, m_w_gate, m_w_up, m_w_down, v_w_ada, v_b_ada, v_g_mix, v_w_in, v_w_dw, v_b_dw, v_g_conv_ln, v_b_conv_ln, v_g_q, v_g_k, v_w_out, v_g_ffn, v_w_gate, v_w_up, v_w_down):
    w = dict(w_ada=w_ada, b_ada=b_ada, g_mix=g_mix, w_in=w_in, w_dw=w_dw, b_dw=b_dw, g_conv_ln=g_conv_ln,
             b_conv_ln=b_conv_ln, g_q=g_q, g_k=g_k, w_out=w_out, g_ffn=g_ffn, w_gate=w_gate, w_up=w_up, w_down=w_down)
    m = dict(w_ada=m_w_ada, b_ada=m_b_ada, g_mix=m_g_mix, w_in=m_w_in, w_dw=m_w_dw, b_dw=m_b_dw, g_conv_ln=m_g_conv_ln,
             b_conv_ln=m_b_conv_ln, g_q=m_g_q, g_k=m_g_k, w_out=m_w_out, g_ffn=m_g_ffn, w_gate=m_w_gate, w_up=m_w_up,
             w_down=m_w_down)
    v = dict(w_ada=v_w_ada, b_ada=v_b_ada, g_mix=v_g_mix, w_in=v_w_in, w_dw=v_w_dw, b_dw=v_b_dw, g_conv_ln=v_g_conv_ln,
             b_conv_ln=v_b_conv_ln, g_q=v_g_q, g_k=v_g_k, w_out=v_w_out, g_ffn=v_g_ffn, w_gate=v_w_gate, w_up=v_w_up,
             w_down=v_w_down)
    Bl, S, D = x.shape
    DC = g_conv_ln.shape[1]
    NA = w_ada.shape[2]
    xi, yi, ci = _mesh_pos()
    p = 2 * xi + yi
    dev = 2 * p + ci
    n_dev = 8
    cidx = jnp.reshape(ci, (1,)).astype(jnp.int32)
    pidx = jnp.reshape(p, (1,)).astype(jnp.int32)

    first = jnp.concatenate([_pad_to(c, 8, D), _pad_to(w_dw[0], CONV_ROWS, D)], axis=0)
    g0 = _allgather8(first, "gather_cond").reshape(n_dev, 8 + CONV_ROWS, D)
    c_all = g0[:, :Bl].reshape(n_dev * Bl, D)
    taps = jnp.concatenate([g0[2 * q, 8:, :w_dw.shape[2]] for q in range(4)], axis=1)
    wdw = jnp.where(lax.broadcasted_iota(jnp.int32, taps.shape, 0) == CONV_WIDTH, b_dw, taps)
    shard = lambda a, nm: a[0].T if nm in TRANSPOSED else a[0]
    owned = {nm: _cast_weight(shard(w[nm], nm), pidx, "cast_" + nm) for nm in BIG}
    (w_in_full,) = _gather_weights([owned["w_in"]], "gather_w_in")

    b_cols = lax.dynamic_slice_in_dim(b_ada, p * NA, NA, axis=1)
    mod_part = _ada_fwd(c_all, w_ada[0], b_cols)
    gm = _allgather8(mod_part, "gather_mod").reshape(n_dev, n_dev * Bl, NA)
    mod = jnp.concatenate([lax.dynamic_slice_in_dim(gm[2 * q], dev * Bl, Bl, axis=0) for q in range(4)], axis=1)

    loc = _local_step(x, loss_target, mod, g_mix, wdw, g_conv_ln, b_conv_ln, g_q, g_k, g_ffn,
                      w_in_full, owned["w_out"], owned["w_gate"], owned["w_up"], owned["w_down"], cidx=cidx)

    pc_idx = jnp.stack([p, ci]).astype(jnp.int32)
    halves = [_final_add(s32, r, pc_idx, "final_add_" + nm)
              for nm, s32, r in zip(EARLY_WEIGHTS, loc["early_sums"], loc["early_recv"])]
    grad = dict(zip(EARLY_WEIGHTS, _rs_final(halves, "rs_final_early")))
    (late_sib,) = _rs_sibling([loc["grads"]["w_in"]], "rs_sibling_in")
    late32, late16 = _pair_add(loc["grads"]["w_in"], late_sib, cidx, "pair_add_w_in")
    view = lambda a, nm: a[0].T if nm in TRANSPOSED else a[0]
    early_upd, (late_recv,) = _adamw_many(
        [view(w[nm], nm) for nm in EARLY_WEIGHTS], [grad[nm] for nm in EARLY_WEIGHTS],
        [view(m[nm], nm) for nm in EARLY_WEIGHTS], [view(v[nm], nm) for nm in EARLY_WEIGHTS], "adamw_early",
        host=(_ChipExchange(1), (late16,), (jax.ShapeDtypeStruct((3,) + late16.shape[1:], late16.dtype),)))
    (grad["w_in"],) = _rs_final([_final_add(late32, late_recv, pc_idx, "final_add_w_in")], "rs_final_in")

    mod_rows, _, small_rows = _small_layout(Bl)
    gs = loc["gathered_small"]
    red, bada8 = _small_reduce(gs, n_dev, Bl)
    dmod_all = gs.reshape(n_dev, small_rows, D)[:, :mod_rows].reshape(n_dev * Bl, 8, D)[:, :N_MOD].reshape(n_dev * Bl, N_MOD * D)
    grad["w_ada"] = _ada_bwd(c_all, lax.dynamic_slice_in_dim(dmod_all, p * NA, NA, axis=1))
    grad["b_ada"] = bada8[:N_MOD].reshape(1, N_MOD * D)
    grad["g_mix"] = red[0:1]
    grad["g_ffn"] = red[1:2]
    grad["g_conv_ln"] = red[2:3, :DC]
    grad["b_conv_ln"] = red[2:3, DC:2 * DC]
    grad["g_q"] = red[3:4, :HEAD_DIM]
    grad["g_k"] = red[3:4, HEAD_DIM:2 * HEAD_DIM]
    loss = red[4, 0]
    dwdw = red[8:8 + CONV_ROWS, :DC]
    grad["w_dw"] = lax.dynamic_slice_in_dim(dwdw[:CONV_WIDTH], p * w_dw.shape[2], w_dw.shape[2], axis=1)
    grad["b_dw"] = dwdw[CONV_WIDTH:CONV_WIDTH + 1]

    delta, new_m, new_v = {}, {}, {}
    for nm, (d_, m_, v_) in zip(EARLY_WEIGHTS, early_upd):
        shp = w[nm].shape
        back = (lambda a: a.T.reshape(shp)) if nm in TRANSPOSED else (lambda a: a.reshape(shp))
        grad[nm], delta[nm], new_m[nm], new_v[nm] = back(grad[nm]), back(d_), back(m_), back(v_)
    for nm in WEIGHTS:
        if nm in EARLY_WEIGHTS:
            continue
        shp = w[nm].shape
        two_d = (shp[-2], shp[-1]) if len(shp) == 3 else shp
        d_, m_, v_ = _adamw(w[nm].reshape(two_d), grad[nm].reshape(two_d), m[nm].reshape(two_d), v[nm].reshape(two_d),
                            "adamw_" + nm)
        grad[nm] = grad[nm].reshape(shp)
        delta[nm], new_m[nm], new_v[nm] = d_.reshape(shp), m_.reshape(shp), v_.reshape(shp)

    return (loss, loc["dx"], *[grad[nm] for nm in WEIGHTS], *[delta[nm] for nm in WEIGHTS],
            *[new_m[nm] for nm in WEIGHTS], *[new_v[nm] for nm in WEIGHTS])
```

```python
import functools
import math

import jax
import jax.numpy as jnp
import numpy as np
from jax import lax
from jax.experimental import pallas as pl
from jax.experimental.pallas import tpu as pltpu

F32 = jnp.float32
MXU_DTYPE = jnp.bfloat16
ACT_DTYPE = jnp.bfloat16
EPS = 1e-6
NEG_INF = -1e30
HEAD_DIM = 64
LANES = 128
RADIUS = 64
QBLK = 128
DILATIONS = (1, 4, 16)
CONV_WIDTH = 31
CONV_PAD = CONV_WIDTH // 2
CONV_ROWS = 32
N_MOD = 6
ADAM_LR, ADAM_B1, ADAM_B2, ADAM_EPS, ADAM_WD, ADAM_STEP = 0.001, 0.9, 0.999, 1e-08, 0.01, 10
HIGHEST = lax.Precision.HIGHEST
MESH_DEV = pl.DeviceIdType.MESH
VMEM_LIMIT = 56 << 20


def _cp(sem=None, vmem=VMEM_LIMIT):
    kw = dict(vmem_limit_bytes=vmem)
    if sem is not None:
        kw["dimension_semantics"] = sem
    return pltpu.CompilerParams(**kw)


def _sigmoid(x):
    return 1.0 / (1.0 + jnp.exp(-x))


def _dot(a, b):
    return jnp.dot(a, b, preferred_element_type=F32)


def _dot_nt(a, b):
    return lax.dot_general(a, b, (((1,), (1,)), ((), ())), preferred_element_type=F32)


def _dot_tn(a, b):
    return lax.dot_general(a, b, (((0,), (0,)), ((), ())), preferred_element_type=F32)


def _colsum(v):
    return jnp.sum(v, axis=0, keepdims=True)


def _load_resident(i, pairs, sems):
    @pl.when(i == 0)
    def _():
        cps = [pltpu.make_async_copy(src, dst, sems.at[n]) for n, (src, dst) in enumerate(pairs)]
        for c in cps:
            c.start()
        for c in cps:
            c.wait()


def _fwd_in(x2, mod, g_mix, gq2, gk2, w_in, *, S, tm, n_ag):
    T, D = x2.shape
    P, _, Nb = w_in.shape
    n_in = P * Nb
    n_slab = (n_in - n_ag) // LANES
    NS = n_slab // 3
    tps = S // tm

    def body(x_ref, mod_ref, g_ref, gq_ref, gk_ref, w_ref, ag_ref, qkv_ref, qkh_ref, h_ref):
        x = x_ref[...]
        r = lax.rsqrt(jnp.mean(x * x, axis=-1, keepdims=True) + EPS)
        n = x * r * g_ref[...]
        h = n * (1.0 + mod_ref[:, D:2 * D]) + mod_ref[:, 0:D]
        hb = h.astype(MXU_DTYPE)
        h_ref[...] = hb
        parts = [_dot(hb, w_ref[p]) for p in range(P)]
        proj = jnp.concatenate(parts, axis=1) if P > 1 else parts[0]
        ag_ref[...] = proj[:, :n_ag]
        mm = _head_mean_matrix()
        for j in range(n_slab):
            v = proj[:, n_ag + LANES * j:n_ag + LANES * (j + 1)]
            qkv_ref[j] = v
            if j < 2 * NS:
                gain = gq_ref[...] * (HEAD_DIM ** -0.5 * LOG2E) if j < NS else gk_ref[...]
                qkh_ref[j] = v * lax.rsqrt(_head_mean(v * v, mm) + EPS) * gain

    return pl.pallas_call(
        body, grid=(T // tm,), name="fwd_in",
        in_specs=[pl.BlockSpec((tm, D), lambda i: (i, 0)),
                  pl.BlockSpec((None, 1, N_MOD * D), lambda i: (i // tps, 0, 0)),
                  pl.BlockSpec((1, D), lambda i: (0, 0)),
                  pl.BlockSpec((1, LANES), lambda i: (0, 0)), pl.BlockSpec((1, LANES), lambda i: (0, 0)),
                  pl.BlockSpec((P, D, Nb), lambda i: (0, 0, 0))],
        out_specs=[pl.BlockSpec((tm, n_ag), lambda i: (i, 0)),
                   pl.BlockSpec((n_slab, tm, LANES), lambda i: (0, i, 0)),
                   pl.BlockSpec((2 * NS, tm, LANES), lambda i: (0, i, 0)),
                   pl.BlockSpec((tm, D), lambda i: (i, 0))],
        out_shape=[jax.ShapeDtypeStruct((T, n_ag), F32),
                   jax.ShapeDtypeStruct((n_slab, T, LANES), F32),
                   jax.ShapeDtypeStruct((2 * NS, T, LANES), F32),
                   jax.ShapeDtypeStruct((T, D), MXU_DTYPE)],
        compiler_params=_cp(("arbitrary",)),
    )(x2, mod, g_mix, gq2, gk2, w_in)


CONV_CH = 64


def _conv_taps(win, w_ref, acc, reverse):
    n = win.shape[0]
    for b in range(8):
        wb = win if b == 0 else pltpu.roll(win, shift=n - b, axis=0)
        for a in range(4):
            o = 8 * a + b
            if o < 1 or o > CONV_WIDTH:
                continue
            k = (CONV_WIDTH - o) if reverse else (o - 1)
            acc = acc + w_ref[k:k + 1, :] * wb[8 * a:8 * a + CONV_CH, :]
    return acc


def _conv_fwd(ag, wdw, *, Bl, S, DC):
    T = ag.shape[0]
    nsc = DC // LANES
    CH = CONV_CH

    def body(a_ref, g_ref, w_ref, cv_ref, upad):
        zeros16 = jnp.zeros((16, LANES), F32)
        upad[0:16, :] = zeros16
        upad[S + 16:S + 32, :] = zeros16

        def fill(i, _):
            r0 = pl.multiple_of(i * CH, CH)
            a = a_ref[pl.ds(r0, CH), :]
            g = g_ref[pl.ds(r0, CH), :]
            upad[pl.ds(r0 + 16, CH), :] = a * _sigmoid(g)
            return 0
        lax.fori_loop(0, S // CH, fill, 0)

        def conv(i, _):
            r0 = pl.multiple_of(i * CH, CH)
            win = upad[pl.ds(r0, CH + 32), :]
            acc = jnp.zeros((CH, LANES), F32) + w_ref[CONV_WIDTH:CONV_WIDTH + 1, :]
            cv_ref[pl.ds(r0, CH), :] = _conv_taps(win, w_ref, acc, reverse=False)
            return 0
        lax.fori_loop(0, S // CH, conv, 0)

    return pl.pallas_call(
        body, grid=(Bl, nsc), name="conv_fwd",
        in_specs=[pl.BlockSpec((S, LANES), lambda b, j: (b, j)),
                  pl.BlockSpec((S, LANES), lambda b, j: (b, nsc + j)),
                  pl.BlockSpec((CONV_ROWS, LANES), lambda b, j: (0, j))],
        out_specs=pl.BlockSpec((S, LANES), lambda b, j: (b, j)),
        out_shape=jax.ShapeDtypeStruct((T, DC), F32),
        scratch_shapes=[pltpu.VMEM((S + 32, LANES), F32)],
        compiler_params=_cp(("arbitrary", "arbitrary")),
    )(ag, ag, wdw)


def _conv_bwd(ag, dcv, wdw, *, Bl, S, DC):
    T = ag.shape[0]
    nsc = DC // LANES
    CH = CONV_CH

    def body(a_ref, g_ref, d_ref, w_ref, da_ref, dg_ref, dw_ref, upad, dpad, wacc):
        b = pl.program_id(1)
        zeros16 = jnp.zeros((16, LANES), F32)
        upad[0:16, :] = zeros16
        upad[S + 16:S + 32, :] = zeros16
        dpad[0:16, :] = zeros16
        dpad[S + 16:S + 32, :] = zeros16

        @pl.when(b == 0)
        def _():
            wacc[...] = jnp.zeros_like(wacc)

        def fill(i, _):
            r0 = pl.multiple_of(i * CH, CH)
            a = a_ref[pl.ds(r0, CH), :]
            g = g_ref[pl.ds(r0, CH), :]
            upad[pl.ds(r0 + 16, CH), :] = a * _sigmoid(g)
            dpad[pl.ds(r0 + 16, CH), :] = d_ref[pl.ds(r0, CH), :]
            return 0
        lax.fori_loop(0, S // CH, fill, 0)

        def step(i, _):
            r0 = pl.multiple_of(i * CH, CH)
            dwin = dpad[pl.ds(r0, CH + 32), :]
            du = _conv_taps(dwin, w_ref, jnp.zeros((CH, LANES), F32), reverse=True)
            a = a_ref[pl.ds(r0, CH), :]
            g = g_ref[pl.ds(r0, CH), :]
            sg = _sigmoid(g)
            da_ref[pl.ds(r0, CH), :] = du * sg
            dg_ref[pl.ds(r0, CH), :] = du * a * sg * (1.0 - sg)
            dc = d_ref[pl.ds(r0, CH), :]
            uwin = upad[pl.ds(r0, CH + 32), :]
            n = CH + 32
            for bb in range(8):
                wb = uwin if bb == 0 else pltpu.roll(uwin, shift=n - bb, axis=0)
                for aa in range(4):
                    o = 8 * aa + bb
                    if o < 1 or o > CONV_WIDTH:
                        continue
                    k = o - 1
                    prod = dc * wb[8 * aa:8 * aa + CH, :]
                    part = prod[0:8, :]
                    for q in range(1, CH // 8):
                        part = part + prod[8 * q:8 * q + 8, :]
                    wacc[8 * k:8 * k + 8, :] += part
            part = dc[0:8, :]
            for q in range(1, CH // 8):
                part = part + dc[8 * q:8 * q + 8, :]
            wacc[8 * CONV_WIDTH:8 * CONV_WIDTH + 8, :] += part
            return 0
        lax.fori_loop(0, S // CH, step, 0)

        @pl.when(b == Bl - 1)
        def _():
            for k in range(CONV_ROWS):
                dw_ref[k:k + 1, :] = jnp.sum(wacc[8 * k:8 * k + 8, :], axis=0, keepdims=True)

    return pl.pallas_call(
        body, grid=(nsc, Bl), name="conv_bwd",
        in_specs=[pl.BlockSpec((S, LANES), lambda j, b: (b, j)),
                  pl.BlockSpec((S, LANES), lambda j, b: (b, nsc + j)),
                  pl.BlockSpec((S, LANES), lambda j, b: (b, j)),
                  pl.BlockSpec((CONV_ROWS, LANES), lambda j, b: (0, j))],
        out_specs=[pl.BlockSpec((S, LANES), lambda j, b: (b, j)),
                   pl.BlockSpec((S, LANES), lambda j, b: (b, j)),
                   pl.BlockSpec((CONV_ROWS, LANES), lambda j, b: (0, j))],
        out_shape=[jax.ShapeDtypeStruct((T, DC), F32), jax.ShapeDtypeStruct((T, DC), F32),
                   jax.ShapeDtypeStruct((CONV_ROWS, DC), F32)],
        scratch_shapes=[pltpu.VMEM((S + 32, LANES), F32), pltpu.VMEM((S + 32, LANES), F32),
                        pltpu.VMEM((8 * CONV_ROWS, LANES), F32)],
        compiler_params=_cp(("arbitrary", "arbitrary")),
    )(ag, ag, dcv, wdw)


ROWCH = 256


LOG2E = 1.4426950408889634
LN2 = 0.6931471805599453
N_EDGE = 4


def _head_mean_matrix():
    r = lax.broadcasted_iota(jnp.int32, (LANES, LANES), 0) // HEAD_DIM
    c = lax.broadcasted_iota(jnp.int32, (LANES, LANES), 1) // HEAD_DIM
    return jnp.where(r == c, 1.0 / HEAD_DIM, 0.0).astype(jnp.bfloat16)


def _head_mean(v, mm):
    hi = v.astype(jnp.bfloat16)
    lo = (v - hi.astype(F32)).astype(jnp.bfloat16)
    return _dot(hi, mm) + _dot(lo, mm)


def _stack_heads(blk, lane_lo):
    z = jnp.zeros_like(blk)
    return jnp.concatenate([jnp.where(lane_lo, blk, z), jnp.where(lane_lo, z, blk)], axis=0)


def _merge_heads(v2, lane_lo):
    return jnp.where(lane_lo, v2[:QBLK], v2[QBLK:])


def _bias_tables(bias_ref, slope_ref):
    row = lax.broadcasted_iota(jnp.int32, (2 * QBLK, 2 * QBLK), 0)
    col = lax.broadcasted_iota(jnp.int32, (2 * QBLK, 2 * QBLK), 1)
    rel = jnp.abs(col - RADIUS - (row % QBLK))
    slope = jnp.where(row < QBLK, slope_ref[0:1, 0:1], slope_ref[0:1, HEAD_DIM:HEAD_DIM + 1]) * LOG2E
    for pi, d in enumerate(DILATIONS):
        inside = jnp.where(rel <= RADIUS, -slope * (float(d) * rel.astype(F32)), NEG_INF)
        for e in range(N_EDGE):
            t = inside
            if e & 1:
                t = jnp.where(col < RADIUS, NEG_INF, t)
            if e & 2:
                t = jnp.where(col >= QBLK + RADIUS, NEG_INF, t)
            bias_ref[N_EDGE * pi + e] = t


def _edge_index(qb, nb):
    return jnp.where(qb == 0, 1, 0) + jnp.where(qb == nb - 1, 2, 0)


def _gather_rows(src_ref, dst_ref, S, d, pad):
    n = S // d
    seg = n + 2 * RADIUS if pad else n
    step = min(n, 512)
    for r in range(d):
        base = r * seg
        if pad:
            dst_ref[base:base + RADIUS, :] = jnp.zeros((RADIUS, LANES), dst_ref.dtype)
            dst_ref[base + RADIUS + n:base + seg, :] = jnp.zeros((RADIUS, LANES), dst_ref.dtype)
            base += RADIUS
        for c0 in range(0, n, step):
            if d == 1:
                v = src_ref[c0:c0 + step, :]
            else:
                v = src_ref[pl.ds(r + c0 * d, step, stride=d), :]
            dst_ref[base + c0:base + c0 + step, :] = v.astype(dst_ref.dtype)


def _scatter_rows(src_ref, dst_ref, S, d, pad, accumulate):
    n = S // d
    seg = n + 2 * RADIUS if pad else n
    step = min(n, 512)
    for r in range(d):
        base = r * seg + (RADIUS if pad else 0)
        for c0 in range(0, n, step):
            v = src_ref[base + c0:base + c0 + step, :]
            if d == 1:
                idx = pl.ds(c0, step)
            else:
                idx = pl.ds(r + c0 * d, step, stride=d)
            if accumulate:
                dst_ref[idx, :] = dst_ref[idx, :] + v
            else:
                dst_ref[idx, :] = v


def _zero_uncovered(acc, S, d):
    n = S // d
    if (n // QBLK) % 2:
        return
    seg = n + 2 * RADIUS
    for r in range(d):
        acc[0, r * seg + n:r * seg + seg, :] = jnp.zeros((2 * RADIUS, LANES), F32)
        acc[1, r * seg:r * seg + 2 * RADIUS, :] = jnp.zeros((2 * RADIUS, LANES), F32)


def _scatter_parity(acc, dst_ref, S, d):
    n = S // d
    seg = n + 2 * RADIUS
    step = min(n, 512)
    one_block = (n // QBLK) % 2 == 1
    for r in range(d):
        base = r * seg + RADIUS
        for c0 in range(0, n, step):
            rows = slice(base + c0, base + c0 + step)
            v = acc[r % 2, rows, :] if one_block else acc[0, rows, :] + acc[1, rows, :]
            idx = pl.ds(c0, step) if d == 1 else pl.ds(r + c0 * d, step, stride=d)
            dst_ref[idx, :] = dst_ref[idx, :] + v


PIPE_UNROLL = 4
PIPE_SLOTS = 16
BWD_SLOTS = 12


def _pipeline(n_items, stages, unroll):
    K = len(stages)
    assert n_items % unroll == 0 and K * unroll <= (PIPE_SLOTS if K == 4 else BWD_SLOTS)
    trips = n_items // unroll
    assert trips >= K - 1

    def trip(t, static):
        for s in reversed(range(K)):
            if static and not 0 <= t - s < trips:
                continue
            for u in range(unroll):
                item = unroll * (t - s) + u
                stages[s](jnp.int32(item) if static else item)

    for t in range(K - 1):
        trip(t, True)

    def full(t, carry):
        trip(t, False)
        return carry
    lax.fori_loop(K - 1, trips, full, 0)
    for t in range(trips, trips + K - 1):
        trip(t, True)


def _attn_fwd(qkh, qkv, slopes, *, Bl, S, hosted=()):
    n3, T, _ = qkv.shape
    NS = n3 // 3
    NB = S // QBLK
    PADR = S + 2 * RADIUS * DILATIONS[-1]
    nh = len(hosted)
    plan = _WeightGather([b.shape for b in hosted]) if nh else None
    n_steps = Bl * NS

    def body(qh, kh, v_ref, slope_ref, *rest):
        o_ref, lse_ref = rest[nh:nh + 2]
        wouts = rest[nh + 2:2 * nh + 2]
        (qp, kp, vp, op, lp, onat, lnat, bias_ref, sbuf, pbuf, mbuf, lbuf) = rest[2 * nh + 2:2 * nh + 14]
        sems = rest[2 * nh + 14:]
        step = pl.program_id(0) * NS + pl.program_id(1)
        if nh:
            @pl.when(step == 0)
            def _():
                plan.start(wouts, sems)

            @pl.when(step == (5 * n_steps) // 8)
            def _():
                plan.forward(wouts, sems)

        lane_lo = lax.broadcasted_iota(jnp.int32, (QBLK, LANES), 1) < HEAD_DIM
        _bias_tables(bias_ref, slope_ref)

        for pi, d in enumerate(DILATIONS):
            n = S // d
            nb = n // QBLK
            _gather_rows(qh, qp, S, d, pad=False)
            _gather_rows(kh, kp, S, d, pad=True)
            _gather_rows(v_ref, vp, S, d, pad=True)

            def offsets(i, nb=nb):
                r = i // nb
                return pl.multiple_of(i * QBLK, QBLK), pl.multiple_of((i + r) * QBLK, QBLK), i % nb

            def scores(i, pi=pi, nb=nb):
                q0, k0, qb = offsets(i)
                qs = _stack_heads(qp[pl.ds(q0, QBLK), :], lane_lo)
                sbuf[i % PIPE_SLOTS] = (_dot_nt(qs, kp[pl.ds(k0, 2 * QBLK), :])
                                        + bias_ref[N_EDGE * pi + _edge_index(qb, nb)])

            def rowmax(i):
                m = jnp.max(sbuf[i % PIPE_SLOTS], axis=1, keepdims=True)
                mbuf[i % PIPE_SLOTS] = jnp.broadcast_to(m, (2 * QBLK, LANES))

            def expsum(i):
                m = mbuf[i % PIPE_SLOTS]
                p = jnp.exp2(sbuf[i % PIPE_SLOTS] - jnp.concatenate([m, m], axis=1))
                pbuf[i % PIPE_SLOTS] = p.astype(MXU_DTYPE)
                lbuf[i % PIPE_SLOTS] = jnp.broadcast_to(jnp.sum(p, axis=1, keepdims=True), (2 * QBLK, LANES))

            def values(i):
                q0, k0, _ = offsets(i)
                l = lbuf[i % PIPE_SLOTS]
                o2 = _dot(pbuf[i % PIPE_SLOTS], vp[pl.ds(k0, 2 * QBLK), :]) * (1.0 / l)
                op[pl.ds(q0, QBLK), :] = _merge_heads(o2, lane_lo)
                lp[pl.ds(q0, QBLK), :] = _merge_heads(mbuf[i % PIPE_SLOTS] + jnp.log2(l), lane_lo)

            _pipeline(NB, [scores, rowmax, expsum, values], PIPE_UNROLL)
            _scatter_rows(op, onat.at[pi], S, d, pad=False, accumulate=False)
            _scatter_rows(lp, lnat.at[pi], S, d, pad=False, accumulate=False)

        for c0 in range(0, S, ROWCH):
            ls = [lnat[pi, c0:c0 + ROWCH, :] for pi in range(len(DILATIONS))]
            mx = jnp.maximum(jnp.maximum(ls[0], ls[1]), ls[2])
            es = [jnp.exp2(l - mx) for l in ls]
            tot = es[0] + es[1] + es[2]
            inv = 1.0 / tot
            acc = (es[0] * inv) * onat[0, c0:c0 + ROWCH, :]
            for pi in (1, 2):
                acc = acc + (es[pi] * inv) * onat[pi, c0:c0 + ROWCH, :]
            o_ref[c0:c0 + ROWCH, :] = acc
            lse_ref[c0:c0 + ROWCH, :] = mx + jnp.log2(tot)

        if nh:
            @pl.when(step == n_steps - 1)
            def _():
                plan.finish(wouts, sems)

    spec_in = lambda off: pl.BlockSpec((None, S, LANES), lambda b, j: (off * NS + j, b, 0))
    out = pl.BlockSpec((S, LANES), lambda b, j: (b, j))
    anyspec = pl.BlockSpec(memory_space=pl.ANY)
    return pl.pallas_call(
        body, grid=(Bl, NS), name="attn_fwd",
        in_specs=[spec_in(0), spec_in(1), spec_in(2),
                  pl.BlockSpec((None, 8, LANES), lambda b, j: (j, 0, 0))] + [anyspec] * nh,
        out_specs=[out, out] + [anyspec] * nh,
        out_shape=[jax.ShapeDtypeStruct((T, NS * LANES), F32)] * 2
                  + [jax.ShapeDtypeStruct(b.shape, b.dtype) for b in hosted],
        input_output_aliases={4 + w: 2 + w for w in range(nh)},
        scratch_shapes=[pltpu.VMEM((S, LANES), MXU_DTYPE), pltpu.VMEM((PADR, LANES), MXU_DTYPE),
                        pltpu.VMEM((PADR, LANES), MXU_DTYPE),
                        pltpu.VMEM((S, LANES), F32), pltpu.VMEM((S, LANES), F32),
                        pltpu.VMEM((3, S, LANES), F32), pltpu.VMEM((3, S, LANES), F32),
                        pltpu.VMEM((N_EDGE * len(DILATIONS), 2 * QBLK, 2 * QBLK), F32),
                        pltpu.VMEM((PIPE_SLOTS, 2 * QBLK, 2 * QBLK), F32),
                        pltpu.VMEM((PIPE_SLOTS, 2 * QBLK, 2 * QBLK), MXU_DTYPE),
                        pltpu.VMEM((PIPE_SLOTS, 2 * QBLK, LANES), F32), pltpu.VMEM((PIPE_SLOTS, 2 * QBLK, LANES), F32)]
                       + (plan.scratch() if nh else []),
        compiler_params=_cp(("arbitrary", "arbitrary")),
    )(qkh, qkh, qkv, slopes, *hosted)


def _attn_bwd(qkh, qkv, o, lse, do, gq2, gk2, slopes, *, Bl, S, hosted=()):
    n3, T, _ = qkv.shape
    NS = n3 // 3
    NB = S // QBLK
    PADR = S + 2 * RADIUS * DILATIONS[-1]
    QSCALE = HEAD_DIM ** -0.5
    nh = len(hosted)
    plan = _ChipExchange(nh)
    n_steps = Bl * NS

    def body(qh, kh, q_ref, k_ref, v_ref, o_ref, lse_ref, do_ref, gq_ref, gk_ref, slope_ref, *rest):
        hin = rest[:nh]
        dq_ref, dk_ref, dv_ref, gacc_ref = rest[nh:nh + 4]
        hout = rest[nh + 4:2 * nh + 4]
        (dl, qp, kp, vp, dop, lp, dlp, dqp, dkacc, dvacc, dqn, dkn, bias_ref,
         sbuf, dpbuf, pbuf, dsbuf) = rest[2 * nh + 4:2 * nh + 21]
        sems = rest[2 * nh + 21:]
        step = pl.program_id(0) * NS + pl.program_id(1)

        @pl.when(step == 0)
        def _():
            gacc_ref[...] = jnp.zeros_like(gacc_ref)
            if nh:
                plan.start(hin, hout, sems)

        mm = _head_mean_matrix()
        lane_lo = lax.broadcasted_iota(jnp.int32, (QBLK, LANES), 1) < HEAD_DIM
        _bias_tables(bias_ref, slope_ref)
        for c0 in range(0, S, ROWCH):
            dl[c0:c0 + ROWCH, :] = _head_mean(do_ref[c0:c0 + ROWCH, :] * o_ref[c0:c0 + ROWCH, :], mm) * HEAD_DIM
            dqn[c0:c0 + ROWCH, :] = jnp.zeros((ROWCH, LANES), F32)
            dkn[c0:c0 + ROWCH, :] = jnp.zeros((ROWCH, LANES), F32)
            dv_ref[c0:c0 + ROWCH, :] = jnp.zeros((ROWCH, LANES), F32)

        for pi, d in enumerate(DILATIONS):
            n = S // d
            nb = n // QBLK
            _gather_rows(qh, qp, S, d, pad=False)
            _gather_rows(kh, kp, S, d, pad=True)
            _gather_rows(v_ref, vp, S, d, pad=True)
            _gather_rows(do_ref, dop, S, d, pad=False)
            _gather_rows(lse_ref, lp, S, d, pad=False)
            _gather_rows(dl, dlp, S, d, pad=False)
            _zero_uncovered(dkacc, S, d)
            _zero_uncovered(dvacc, S, d)

            def offsets(i, nb=nb):
                r = i // nb
                return pl.multiple_of(i * QBLK, QBLK), pl.multiple_of((i + r) * QBLK, QBLK), i % nb

            def scores(i, pi=pi, nb=nb):
                q0, k0, qb = offsets(i)
                qs = _stack_heads(qp[pl.ds(q0, QBLK), :], lane_lo)
                dos = _stack_heads(dop[pl.ds(q0, QBLK), :], lane_lo)
                sbuf[i % BWD_SLOTS] = (_dot_nt(qs, kp[pl.ds(k0, 2 * QBLK), :])
                                       + bias_ref[N_EDGE * pi + _edge_index(qb, nb)])
                dpbuf[i % BWD_SLOTS] = _dot_nt(dos, vp[pl.ds(k0, 2 * QBLK), :])

            def probs(i):
                q0, _, _ = offsets(i)
                lblk = lp[pl.ds(q0, QBLK), :]
                dblk = dlp[pl.ds(q0, QBLK), :]
                lcol = jnp.concatenate([lblk[:, 0:1], lblk[:, HEAD_DIM:HEAD_DIM + 1]], axis=0)
                dcol = jnp.concatenate([dblk[:, 0:1], dblk[:, HEAD_DIM:HEAD_DIM + 1]], axis=0)
                p = jnp.exp2(sbuf[i % BWD_SLOTS] - lcol)
                pbuf[i % BWD_SLOTS] = p.astype(MXU_DTYPE)
                dsbuf[i % BWD_SLOTS] = (p * (dpbuf[i % BWD_SLOTS] - dcol)).astype(MXU_DTYPE)

            def grads(i):
                q0, k0, _ = offsets(i)
                qs = _stack_heads(qp[pl.ds(q0, QBLK), :], lane_lo)
                dos = _stack_heads(dop[pl.ds(q0, QBLK), :], lane_lo)
                ds = dsbuf[i % BWD_SLOTS]
                dvacc[i % 2, pl.ds(k0, 2 * QBLK), :] = _dot_tn(pbuf[i % BWD_SLOTS], dos)
                dkacc[i % 2, pl.ds(k0, 2 * QBLK), :] = _dot_tn(ds, qs)
                dqp[pl.ds(q0, QBLK), :] = _merge_heads(_dot(ds, kp[pl.ds(k0, 2 * QBLK), :]), lane_lo)

            _pipeline(NB, [scores, probs, grads], PIPE_UNROLL)
            _scatter_rows(dqp, dqn, S, d, pad=False, accumulate=True)
            _scatter_parity(dkacc, dkn, S, d)
            _scatter_parity(dvacc, dv_ref, S, d)

        gq_sum = jnp.zeros((8, LANES), F32)
        gk_sum = jnp.zeros((8, LANES), F32)
        for c0 in range(0, S, ROWCH):
            for src_ref, dn, g_ref, dst_ref, scale, is_q in ((q_ref, dqn, gq_ref, dq_ref, QSCALE, True),
                                                             (k_ref, dkn, gk_ref, dk_ref, LN2, False)):
                x = src_ref[c0:c0 + ROWCH, :]
                dh = dn[c0:c0 + ROWCH, :]
                rr = lax.rsqrt(_head_mean(x * x, mm) + EPS)
                e = dh * (g_ref[...] * scale)
                dst_ref[c0:c0 + ROWCH, :] = rr * e - x * (rr * rr * rr) * _head_mean(e * x, mm)
                gpart = dh * (x * rr * scale)
                acc8 = gpart[0:8, :]
                for q8 in range(1, ROWCH // 8):
                    acc8 = acc8 + gpart[8 * q8:8 * q8 + 8, :]
                if is_q:
                    gq_sum = gq_sum + acc8
                else:
                    gk_sum = gk_sum + acc8
        gacc_ref[0:1, :] += jnp.sum(gq_sum, axis=0, keepdims=True)
        gacc_ref[1:2, :] += jnp.sum(gk_sum, axis=0, keepdims=True)

        if nh:
            @pl.when(step == n_steps - 1)
            def _():
                plan.finish(hin, hout, sems)

    spec_in = lambda off: pl.BlockSpec((None, S, LANES), lambda b, j: (off * NS + j, b, 0))
    tok = pl.BlockSpec((S, LANES), lambda b, j: (b, j))
    vec = pl.BlockSpec((1, LANES), lambda b, j: (0, 0))
    slab_out = pl.BlockSpec((None, S, LANES), lambda b, j: (j, b, 0))
    f32buf = lambda rows: pltpu.VMEM((rows, LANES), F32)
    bfbuf = lambda rows: pltpu.VMEM((rows, LANES), MXU_DTYPE)
    anyspec = pl.BlockSpec(memory_space=pl.ANY)
    return pl.pallas_call(
        body, grid=(Bl, NS), name="attn_bwd",
        in_specs=[spec_in(0), spec_in(1), spec_in(0), spec_in(1), spec_in(2), tok, tok, tok, vec, vec,
                  pl.BlockSpec((None, 8, LANES), lambda b, j: (j, 0, 0))] + [anyspec] * nh,
        out_specs=[slab_out, slab_out, slab_out, pl.BlockSpec((8, LANES), lambda b, j: (0, 0))] + [anyspec] * nh,
        out_shape=[jax.ShapeDtypeStruct((NS, T, LANES), F32)] * 3 + [jax.ShapeDtypeStruct((8, LANES), F32)]
                  + [jax.ShapeDtypeStruct((3,) + h.shape[1:], h.dtype) for h in hosted],
        scratch_shapes=[f32buf(S),
                        bfbuf(S), bfbuf(PADR), bfbuf(PADR), bfbuf(S),
                        f32buf(S), f32buf(S), f32buf(S),
                        pltpu.VMEM((2, PADR, LANES), F32), pltpu.VMEM((2, PADR, LANES), F32),
                        f32buf(S), f32buf(S),
                        pltpu.VMEM((N_EDGE * len(DILATIONS), 2 * QBLK, 2 * QBLK), F32),
                        pltpu.VMEM((BWD_SLOTS, 2 * QBLK, 2 * QBLK), F32),
                        pltpu.VMEM((BWD_SLOTS, 2 * QBLK, 2 * QBLK), F32),
                        pltpu.VMEM((BWD_SLOTS, 2 * QBLK, 2 * QBLK), MXU_DTYPE),
                        pltpu.VMEM((BWD_SLOTS, 2 * QBLK, 2 * QBLK), MXU_DTYPE)]
                       + (plan.scratch() if nh else []),
        compiler_params=_cp(("arbitrary", "arbitrary")),
    )(qkh, qkh, qkv, qkv, qkv, o, lse, do, gq2, gk2, slopes, *hosted)


def _layer_norm_parts(cv, g_ln, b_ln):
    mu = jnp.mean(cv, axis=-1, keepdims=True)
    cen = cv - mu
    rs = lax.rsqrt(jnp.mean(cen * cen, axis=-1, keepdims=True) + EPS)
    z = cen * rs
    return z, rs, z * g_ln + b_ln


def _ffn_fwd(x2, cv, ya, tgt, mod, g_ln, b_ln, g_ffn, w_out, w_gate, w_up, w_down, *, S, tm):
    T, D = x2.shape
    DC = cv.shape[1]
    P, Kb, _ = w_out.shape
    Fb = w_down.shape[1]
    tps = S // tm

    def body(x_ref, cv_ref, ya_ref, t_ref, mod_ref, gln_ref, bln_ref, gf_ref, wo_hbm, wg_hbm, wu_hbm, wd_hbm,
             x1_ref, ycat_ref, mix_ref, h2_ref, g_ref, u_ref, a_ref, f_ref, dy_ref, loss_ref,
             wo, wg, wu, wd, sems):
        i = pl.program_id(0)
        _load_resident(i, [(wo_hbm, wo), (wg_hbm, wg), (wu_hbm, wu), (wd_hbm, wd)], sems)

        @pl.when(i == 0)
        def _():
            loss_ref[...] = jnp.zeros_like(loss_ref)

        _, _, ln = _layer_norm_parts(cv_ref[...], gln_ref[...], bln_ref[...])
        yc = ln * _sigmoid(ln)
        ycat = jnp.concatenate([yc, ya_ref[...]], axis=1).astype(MXU_DTYPE)
        ycat_ref[...] = ycat
        mix = _dot(ycat[:, 0:Kb], wo[0])
        for p in range(1, P):
            mix = mix + _dot(ycat[:, Kb * p:Kb * (p + 1)], wo[p])
        mix_ref[...] = mix.astype(ACT_DTYPE)
        x1 = x_ref[...] + mod_ref[:, 2 * D:3 * D] * mix
        x1_ref[...] = x1
        r2 = lax.rsqrt(jnp.mean(x1 * x1, axis=-1, keepdims=True) + EPS)
        h2 = (x1 * r2 * gf_ref[...]) * (1.0 + mod_ref[:, 4 * D:5 * D]) + mod_ref[:, 3 * D:4 * D]
        h2b = h2.astype(MXU_DTYPE)
        h2_ref[...] = h2b
        f = jnp.zeros((tm, D), F32)
        for p in range(P):
            g = _dot_nt(h2b, wg[p])
            u = _dot_nt(h2b, wu[p])
            a = (g * _sigmoid(g) * u).astype(MXU_DTYPE)
            g_ref[p] = g.astype(ACT_DTYPE)
            u_ref[p] = u.astype(ACT_DTYPE)
            a_ref[p] = a
            f = f + _dot(a, wd[p])
        f_ref[...] = f.astype(ACT_DTYPE)
        err = x1 + mod_ref[:, 5 * D:6 * D] * f - t_ref[...]
        dy_ref[...] = err * (1.0 / D)
        tot = jnp.sum(_colsum(err * err), axis=1, keepdims=True)
        loss_ref[...] += tot * (0.5 / D)

    row = lambda w: pl.BlockSpec((tm, w), lambda i: (i, 0))
    vec = lambda w: pl.BlockSpec((1, w), lambda i: (0, 0))
    blk = pl.BlockSpec((P, tm, Fb), lambda i: (0, i, 0))
    anyspec = pl.BlockSpec(memory_space=pl.ANY)
    return pl.pallas_call(
        body, grid=(T // tm,), name="ffn_fwd",
        in_specs=[row(D), row(DC), row(D - DC), row(D),
                  pl.BlockSpec((None, 1, N_MOD * D), lambda i: (i // tps, 0, 0)),
                  vec(DC), vec(DC), vec(D), anyspec, anyspec, anyspec, anyspec],
        out_specs=[row(D), row(D), row(D), row(D), blk, blk, blk, row(D), row(D),
                   pl.BlockSpec((8, LANES), lambda i: (0, 0))],
        out_shape=[jax.ShapeDtypeStruct((T, D), F32), jax.ShapeDtypeStruct((T, D), MXU_DTYPE),
                   jax.ShapeDtypeStruct((T, D), ACT_DTYPE), jax.ShapeDtypeStruct((T, D), MXU_DTYPE),
                   jax.ShapeDtypeStruct((P, T, Fb), ACT_DTYPE), jax.ShapeDtypeStruct((P, T, Fb), ACT_DTYPE),
                   jax.ShapeDtypeStruct((P, T, Fb), MXU_DTYPE), jax.ShapeDtypeStruct((T, D), ACT_DTYPE),
                   jax.ShapeDtypeStruct((T, D), F32), jax.ShapeDtypeStruct((8, LANES), F32)],
        scratch_shapes=[pltpu.VMEM(w_out.shape, w_out.dtype), pltpu.VMEM(w_gate.shape, w_gate.dtype),
                        pltpu.VMEM(w_up.shape, w_up.dtype), pltpu.VMEM(w_down.shape, w_down.dtype),
                        pltpu.SemaphoreType.DMA((4,))],
        compiler_params=_cp(("arbitrary",)),
    )(x2, cv, ya, tgt, mod, g_ln, b_ln, g_ffn, w_out, w_gate, w_up, w_down)


def _ffn_bwd(dy, x1, gs, us, fo, mixb, cv, mod, g_ln, b_ln, g_ffn, w_out, w_gate, w_up, w_down, *, S, tm):
    T, D = dy.shape
    DC = cv.shape[1]
    P, Kb, _ = w_out.shape
    Fb = w_down.shape[1]
    tps = S // tm
    Bl = T // S

    def body(dy_ref, x1_ref, g_ref, u_ref, f_ref, mix_ref, cv_ref, mod_ref, gln_ref, bln_ref, gf_ref,
             wo_hbm, wg_hbm, wu_hbm, wd_hbm,
             dg_ref, du_ref, df_ref, dx1_ref, dmix_ref, dya_ref, dcv_ref, macc_ref, gacc_ref, lacc_ref,
             wo, wg, wu, wd, sems):
        i = pl.program_id(0)
        _load_resident(i, [(wo_hbm, wo), (wg_hbm, wg), (wu_hbm, wu), (wd_hbm, wd)], sems)

        @pl.when(i == 0)
        def _():
            gacc_ref[...] = jnp.zeros_like(gacc_ref)
            lacc_ref[...] = jnp.zeros_like(lacc_ref)

        @pl.when(i % tps == 0)
        def _():
            macc_ref[...] = jnp.zeros_like(macc_ref)

        dy_t = dy_ref[...]
        x1 = x1_ref[...]
        gate_f = mod_ref[:, 5 * D:6 * D]
        macc_ref[2:3, :] += _colsum(dy_t * f_ref[...].astype(F32))
        dfb = (dy_t * gate_f).astype(MXU_DTYPE)
        df_ref[...] = dfb
        dh2 = jnp.zeros((tm, D), F32)
        for p in range(P):
            da = _dot_nt(dfb, wd[p])
            g = g_ref[p].astype(F32)
            u = u_ref[p].astype(F32)
            sg = _sigmoid(g)
            dgp = (da * u * (sg * (1.0 + g * (1.0 - sg)))).astype(MXU_DTYPE)
            dup = (da * (g * sg)).astype(MXU_DTYPE)
            dg_ref[p] = dgp
            du_ref[p] = dup
            dh2 = dh2 + _dot(dgp, wg[p]) + _dot(dup, wu[p])
        r2 = lax.rsqrt(jnp.mean(x1 * x1, axis=-1, keepdims=True) + EPS)
        xr = x1 * r2
        n2 = xr * gf_ref[...]
        macc_ref[0:1, :] += _colsum(dh2)
        macc_ref[1:2, :] += _colsum(dh2 * n2)
        dn2 = dh2 * (1.0 + mod_ref[:, 4 * D:5 * D])
        gacc_ref[0:1, :] += _colsum(dn2 * xr)
        e = dn2 * gf_ref[...]
        dx1 = dy_t + r2 * e - xr * (r2 * jnp.mean(e * xr, axis=-1, keepdims=True))
        dx1_ref[...] = dx1
        macc_ref[3:4, :] += _colsum(dx1 * mix_ref[...].astype(F32))
        dmixb = (dx1 * mod_ref[:, 2 * D:3 * D]).astype(MXU_DTYPE)
        dmix_ref[...] = dmixb
        parts = [_dot_nt(dmixb, wo[p]) for p in range(P)]
        dycat = jnp.concatenate(parts, axis=1) if P > 1 else parts[0]
        dya_ref[...] = dycat[:, DC:]
        dyc = dycat[:, :DC]
        z, rs, ln = _layer_norm_parts(cv_ref[...], gln_ref[...], bln_ref[...])
        sg = _sigmoid(ln)
        dln = dyc * (sg * (1.0 + ln * (1.0 - sg)))
        lacc_ref[0:1, :] += _colsum(dln * z)
        lacc_ref[1:2, :] += _colsum(dln)
        dz = dln * gln_ref[...]
        dcv_ref[...] = rs * (dz - jnp.mean(dz, axis=-1, keepdims=True) - z * jnp.mean(dz * z, axis=-1, keepdims=True))

    row = lambda w: pl.BlockSpec((tm, w), lambda i: (i, 0))
    vec = lambda w: pl.BlockSpec((1, w), lambda i: (0, 0))
    blk = pl.BlockSpec((P, tm, Fb), lambda i: (0, i, 0))
    anyspec = pl.BlockSpec(memory_space=pl.ANY)
    return pl.pallas_call(
        body, grid=(T // tm,), name="ffn_bwd",
        in_specs=[row(D), row(D), blk, blk, row(D), row(D), row(DC),
                  pl.BlockSpec((None, 1, N_MOD * D), lambda i: (i // tps, 0, 0)),
                  vec(DC), vec(DC), vec(D), anyspec, anyspec, anyspec, anyspec],
        out_specs=[blk, blk, row(D), row(D), row(D), row(D - DC), row(DC),
                   pl.BlockSpec((None, 8, D), lambda i: (i // tps, 0, 0)),
                   pl.BlockSpec((8, D), lambda i: (0, 0)), pl.BlockSpec((8, DC), lambda i: (0, 0))],
        out_shape=[jax.ShapeDtypeStruct((P, T, Fb), MXU_DTYPE), jax.ShapeDtypeStruct((P, T, Fb), MXU_DTYPE),
                   jax.ShapeDtypeStruct((T, D), MXU_DTYPE), jax.ShapeDtypeStruct((T, D), F32),
                   jax.ShapeDtypeStruct((T, D), MXU_DTYPE), jax.ShapeDtypeStruct((T, D - DC), F32),
                   jax.ShapeDtypeStruct((T, DC), F32), jax.ShapeDtypeStruct((Bl, 8, D), F32),
                   jax.ShapeDtypeStruct((8, D), F32), jax.ShapeDtypeStruct((8, DC), F32)],
        scratch_shapes=[pltpu.VMEM(w_out.shape, w_out.dtype), pltpu.VMEM(w_gate.shape, w_gate.dtype),
                        pltpu.VMEM(w_up.shape, w_up.dtype), pltpu.VMEM(w_down.shape, w_down.dtype),
                        pltpu.SemaphoreType.DMA((4,))],
        compiler_params=_cp(("arbitrary",)),
    )(dy, x1, gs, us, fo, mixb, cv, mod, g_ln, b_ln, g_ffn, w_out, w_gate, w_up, w_down)


def _in_bwd(da, dg, dq, dk, dv, x2, dx1, mod, g_mix, w_in, *, S, tm):
    T, D = x2.shape
    P, _, Nb = w_in.shape
    DC = da.shape[1]
    NS = dq.shape[0]
    n_in = P * Nb
    tps = S // tm
    Bl = T // S

    def body(da_ref, dg_ref, dq_ref, dk_ref, dv_ref, x_ref, dx1_ref, mod_ref, g_ref, w_ref,
             dx_ref, dproj_ref, macc_ref, gacc_ref):
        i = pl.program_id(0)

        @pl.when(i == 0)
        def _():
            gacc_ref[...] = jnp.zeros_like(gacc_ref)

        @pl.when(i % tps == 0)
        def _():
            macc_ref[...] = jnp.zeros_like(macc_ref)

        pieces = [da_ref[...], dg_ref[...]] + [r[j] for r in (dq_ref, dk_ref, dv_ref) for j in range(NS)]
        dproj = jnp.concatenate(pieces, axis=1).astype(MXU_DTYPE)
        dproj_ref[...] = dproj
        dh = _dot_nt(dproj[:, 0:Nb], w_ref[0])
        for p in range(1, P):
            dh = dh + _dot_nt(dproj[:, Nb * p:Nb * (p + 1)], w_ref[p])
        x = x_ref[...]
        r = lax.rsqrt(jnp.mean(x * x, axis=-1, keepdims=True) + EPS)
        xr = x * r
        macc_ref[0:1, :] += _colsum(dh)
        macc_ref[1:2, :] += _colsum(dh * (xr * g_ref[...]))
        dn = dh * (1.0 + mod_ref[:, D:2 * D])
        gacc_ref[0:1, :] += _colsum(dn * xr)
        e = dn * g_ref[...]
        dx_ref[...] = dx1_ref[...] + r * e - xr * (r * jnp.mean(e * xr, axis=-1, keepdims=True))

    row = lambda w: pl.BlockSpec((tm, w), lambda i: (i, 0))
    slab = pl.BlockSpec((NS, tm, LANES), lambda i: (0, i, 0))
    return pl.pallas_call(
        body, grid=(T // tm,), name="in_bwd",
        in_specs=[row(DC), row(DC), slab, slab, slab, row(D), row(D),
                  pl.BlockSpec((None, 1, N_MOD * D), lambda i: (i // tps, 0, 0)),
                  pl.BlockSpec((1, D), lambda i: (0, 0)),
                  pl.BlockSpec((P, D, Nb), lambda i: (0, 0, 0))],
        out_specs=[row(D), row(n_in), pl.BlockSpec((None, 8, D), lambda i: (i // tps, 0, 0)),
                   pl.BlockSpec((8, D), lambda i: (0, 0))],
        out_shape=[jax.ShapeDtypeStruct((T, D), F32), jax.ShapeDtypeStruct((T, n_in), MXU_DTYPE),
                   jax.ShapeDtypeStruct((Bl, 8, D), F32), jax.ShapeDtypeStruct((8, D), F32)],
        compiler_params=_cp(("arbitrary",)),
    )(da, dg, dq, dk, dv, x2, dx1, mod, g_mix, w_in)


def _wgrad(a, b, *, P, name, tk, split=None, host=None):
    a_blk, b_blk = a.ndim == 3, b.ndim == 3
    plan, h_in, h_out = host if host is not None else (None, (), ())
    ni, no = len(h_in), len(h_out)
    T = a.shape[-2]
    if a_blk:
        R, C = a.shape[2], b.shape[1]
        a_of = lambda av, p: av[p]
        b_of = lambda bv, p: bv[...]
    elif b_blk:
        R, C = a.shape[1], b.shape[2]
        a_of = lambda av, p: av[...]
        b_of = lambda bv, p: bv[p]
    elif split == "a":
        R, C = a.shape[1] // P, b.shape[1]
        a_of = lambda av, p: av[:, R * p:R * (p + 1)]
        b_of = lambda bv, p: bv[...]
    else:
        R, C = a.shape[1], b.shape[1] // P
        a_of = lambda av, p: av[...]
        b_of = lambda bv, p: bv[:, C * p:C * (p + 1)]

    n_steps = T // tk

    def body(a_ref, b_ref, *rest):
        hin, o_ref, hout, sems = rest[:ni], rest[ni], rest[ni + 1:ni + 1 + no], rest[ni + 1 + no:]
        step = pl.program_id(0)

        @pl.when(step == 0)
        def _():
            o_ref[...] = jnp.zeros_like(o_ref)
            if plan is not None:
                plan.start(hin, hout, sems)

        if plan is not None:
            @pl.when(step == n_steps // 2)
            def _():
                plan.forward(hin, hout, sems)

        for p in range(P):
            o_ref[p] += _dot_tn(a_of(a_ref, p), b_of(b_ref, p))

        if plan is not None:
            @pl.when(step == n_steps - 1)
            def _():
                plan.finish(hin, hout, sems)

    def spec(v):
        if v.ndim == 3:
            return pl.BlockSpec((P, tk, v.shape[2]), lambda k: (0, k, 0))
        return pl.BlockSpec((tk, v.shape[1]), lambda k: (k, 0))

    anyspec = pl.BlockSpec(memory_space=pl.ANY)
    res = pl.pallas_call(
        body, grid=(n_steps,), name=name,
        in_specs=[spec(a), spec(b)] + [anyspec] * ni,
        out_specs=[pl.BlockSpec((P, R, C), lambda k: (0, 0, 0))] + [anyspec] * no,
        out_shape=[jax.ShapeDtypeStruct((P, R, C), F32)] + list(h_out),
        scratch_shapes=plan.scratch() if plan is not None else [],
        compiler_params=_cp(("arbitrary",)),
    )(a, b, *h_in)
    return res if plan is not None else res[0]


TM_IN = 512
TM_FFN = 256
TK_WGRAD = 512


def _alibi_slabs(n_slab):
    heads = 2 * n_slab
    slopes = 2.0 ** (-8.0 * np.arange(1, heads + 1) / heads)
    return jnp.asarray(np.broadcast_to(np.repeat(slopes.reshape(n_slab, 1, 2), HEAD_DIM, axis=2), (n_slab, 8, LANES)),
                       dtype=F32)


def _local_step(x, tgt, mod, g_mix, wdw, g_ln, b_ln, g_q, g_k, g_ffn, w_in, w_out, w_gate, w_up, w_down,
                pc_idx=None):
    Bl, S, D = x.shape
    T = Bl * S
    DC = g_ln.shape[1]
    P = w_in.shape[0]
    n_slab = (D - DC) // LANES
    x2 = x.reshape(T, D)
    t2 = tgt.reshape(T, D)
    mod3 = mod.reshape(Bl, 1, N_MOD * D)
    gq2 = jnp.tile(g_q, (1, LANES // HEAD_DIM))
    gk2 = jnp.tile(g_k, (1, LANES // HEAD_DIM))
    slopes = _alibi_slabs(n_slab)

    ag, qkv, qkh, h1 = _fwd_in(x2, mod3, g_mix, gq2, gk2, w_in, S=S, tm=TM_IN, n_ag=2 * DC)
    cv = _conv_fwd(ag, wdw, Bl=Bl, S=S, DC=DC)
    if pc_idx is not None:
        ya, lse, w_out, w_gate, w_up, w_down = _attn_fwd(qkh, qkv, slopes, Bl=Bl, S=S,
                                                         hosted=(w_out, w_gate, w_up, w_down))
    else:
        ya, lse = _attn_fwd(qkh, qkv, slopes, Bl=Bl, S=S)
    x1, ycat, mixb, h2, gs, us, acts, fo, dy, lossb = _ffn_fwd(
        x2, cv, ya, t2, mod3, g_ln, b_ln, g_ffn, w_out, w_gate, w_up, w_down, S=S, tm=TM_FFN)
    dgs, dus, dfb, dx1, dmixb, dya, dcv, macc_f, gacc_f, lacc = _ffn_bwd(
        dy, x1, gs, us, fo, mixb, cv, mod3, g_ln, b_ln, g_ffn, w_out, w_gate, w_up, w_down, S=S, tm=TM_FFN)
    wg = functools.partial(_wgrad, P=P, tk=TK_WGRAD)
    out = {}
    if pc_idx is None:
        grads = dict(w_down=wg(acts, dfb, name="wgrad_down"), w_gate=wg(dgs, h2, name="wgrad_gate"),
                     w_up=wg(dus, h2, name="wgrad_up"), w_out=wg(ycat, dmixb, name="wgrad_out", split="a"))
        dq, dk, dv, gqk = _attn_bwd(qkh, qkv, ya, lse, dya, gq2, gk2, slopes, Bl=Bl, S=S)
    else:
        g_down = wg(acts, dfb, name="wgrad_down")
        g_gate, r_down = wg(dgs, h2, name="wgrad_gate", host=_sibling_host([g_down]))
        g_up, r_gate = wg(dus, h2, name="wgrad_up", host=_sibling_host([g_gate]))
        g_out, r_up = wg(ycat, dmixb, name="wgrad_out", split="a", host=_sibling_host([g_up]))
        (r_out,) = _rs_sibling([g_out], "rs_sibling_out")
        grads = dict(w_down=g_down, w_gate=g_gate, w_up=g_up, w_out=g_out)
        sums = [_pair_add(grads[nm], r, pc_idx, "pair_add_" + nm)
                for nm, r in zip(EARLY_WEIGHTS, (r_down, r_gate, r_up, r_out))]
        res = _attn_bwd(qkh, qkv, ya, lse, dya, gq2, gk2, slopes, Bl=Bl, S=S, hosted=tuple(sb for _, sb in sums))
        dq, dk, dv, gqk = res[:4]
        out["early_sums"] = [s32 for s32, _ in sums]
        out["early_recv"] = list(res[4:])
    da, dg, dwdw = _conv_bwd(ag, dcv, wdw, Bl=Bl, S=S, DC=DC)
    dx, dprojb, macc_m, gacc_m = _in_bwd(da, dg, dq, dk, dv, x2, dx1, mod3, g_mix, w_in, S=S, tm=TM_IN)
    packed = _pack_small(macc_m, macc_f, gacc_m, gacc_f, lacc, gqk, dwdw, lossb)
    if pc_idx is None:
        grads["w_in"] = wg(h1, dprojb, name="wgrad_in", split="b")
    else:
        grads["w_in"], out["gathered_small"] = wg(h1, dprojb, name="wgrad_in", split="b",
                                                  host=_small_gather_host(packed))
    out.update(dx=dx.reshape(Bl, S, D), grads=grads, packed=packed)
    return out


EARLY_WEIGHTS = ("w_down", "w_gate", "w_up", "w_out")


def _small_layout(Bl):
    return 8 * Bl, 8 * Bl + 8, 8 * Bl + 8 + CONV_ROWS


def _pack_small(macc_m, macc_f, gacc_m, gacc_f, lacc, gqk, dwdw, lossb):
    Bl, _, D = macc_m.shape
    DC = lacc.shape[1]
    assert 2 * DC <= D
    SMALL_GAIN_ROW, SMALL_TAP_ROW, SMALL_ROWS = _small_layout(Bl)

    def body(mm_ref, mf_ref, gm_ref, gf_ref, la_ref, qk_ref, dw_ref, loss_ref, o_ref):
        o_ref[...] = jnp.zeros_like(o_ref)
        for b in range(Bl):
            o_ref[8 * b + 0:8 * b + 2, :] = mm_ref[b, 0:2, :]
            o_ref[8 * b + 2:8 * b + 3, :] = mf_ref[b, 3:4, :]
            o_ref[8 * b + 3:8 * b + 6, :] = mf_ref[b, 0:3, :]
        r = SMALL_GAIN_ROW
        o_ref[r:r + 1, :] = gm_ref[0:1, :]
        o_ref[r + 1:r + 2, :] = gf_ref[0:1, :]
        o_ref[r + 2:r + 3, 0:DC] = la_ref[0:1, :]
        o_ref[r + 2:r + 3, DC:2 * DC] = la_ref[1:2, :]
        qk = qk_ref[0:2, 0:HEAD_DIM] + qk_ref[0:2, HEAD_DIM:2 * HEAD_DIM]
        o_ref[r + 3:r + 4, 0:HEAD_DIM] = qk[0:1, :]
        o_ref[r + 3:r + 4, HEAD_DIM:2 * HEAD_DIM] = qk[1:2, :]
        o_ref[r + 4:r + 5, 0:LANES] = loss_ref[0:1, :]
        o_ref[SMALL_TAP_ROW:SMALL_TAP_ROW + CONV_ROWS, 0:DC] = dw_ref[...]

    return pl.pallas_call(body, name="pack_small", out_shape=jax.ShapeDtypeStruct((SMALL_ROWS, D), F32),
                          compiler_params=_cp())(macc_m, macc_f, gacc_m, gacc_f, lacc, gqk, dwdw, lossb)


def _row_tile(rows, cap=512):
    if rows <= cap:
        return rows
    best = rows
    for t in range(8, cap + 1, 8):
        if rows % t == 0:
            best = t
    return best


def _cast_weight(w, pidx, name):
    def body(p_ref, w_ref, o_ref):
        o_ref[...] = w_ref[...].astype(MXU_DTYPE)
    R, C = w.shape
    tr = _row_tile(R)
    return pl.pallas_call(
        body, name=name,
        grid_spec=pltpu.PrefetchScalarGridSpec(
            num_scalar_prefetch=1, grid=(R // tr,),
            in_specs=[pl.BlockSpec((tr, C), lambda i, p: (i, 0))],
            out_specs=pl.BlockSpec((None, tr, C), lambda i, p: (p[0], i, 0))),
        out_shape=jax.ShapeDtypeStruct((4, R, C), MXU_DTYPE),
    )(pidx, w)


def _pair_add(g, recv, pc_idx, name):
    P, R, C = g.shape
    R2 = R // 2

    def body(pc_ref, g_ref, r_ref, o_ref, ob_ref):
        s = g_ref[...] + r_ref[...]
        ob_ref[...] = s.astype(jnp.bfloat16)

        @pl.when(pl.program_id(0) == pc_ref[0])
        def _():
            o_ref[...] = s

    return pl.pallas_call(
        body, name=name,
        grid_spec=pltpu.PrefetchScalarGridSpec(
            num_scalar_prefetch=1, grid=(P,),
            in_specs=[pl.BlockSpec((None, R2, C), lambda p, pc: (p, pc[1], 0)),
                      pl.BlockSpec((None, R2, C), lambda p, pc: (p, 0, 0))],
            out_specs=[pl.BlockSpec((R2, C), lambda p, pc: (0, 0)),
                       pl.BlockSpec((None, R2, C), lambda p, pc: (p, 0, 0))]),
        out_shape=[jax.ShapeDtypeStruct((R2, C), F32), jax.ShapeDtypeStruct((P, R2, C), jnp.bfloat16)],
    )(pc_idx, g, recv)


def _final_add(own, recv, pc_idx, name):
    R2, C = own.shape

    def body(pc_ref, s_ref, r_ref, o_ref):
        acc = s_ref[...]
        for k in range(3):
            acc = acc + r_ref[k].astype(F32)
        o_ref[...] = acc

    return pl.pallas_call(
        body, name=name,
        grid_spec=pltpu.PrefetchScalarGridSpec(
            num_scalar_prefetch=1, grid=(1,),
            in_specs=[pl.BlockSpec((R2, C), lambda i, pc: (0, 0)),
                      pl.BlockSpec((3, R2, C), lambda i, pc: (0, 0, 0))],
            out_specs=pl.BlockSpec((R2, C), lambda i, pc: (pc[1], 0))),
        out_shape=jax.ShapeDtypeStruct((2 * R2, C), F32),
    )(pc_idx, own, recv)


def _adamw_update(w_ref, g_ref, m_ref, v_ref, d_ref, nm_ref, nv_ref):
    c1 = 1.0 - ADAM_B1 ** ADAM_STEP
    c2 = 1.0 - ADAM_B2 ** ADAM_STEP
    gg = g_ref[...]
    nm = ADAM_B1 * m_ref[...] + (1.0 - ADAM_B1) * gg
    nv = ADAM_B2 * v_ref[...] + (1.0 - ADAM_B2) * (gg * gg)
    nm_ref[...] = nm
    nv_ref[...] = nv
    d_ref[...] = -ADAM_LR * ((nm / c1) / (jnp.sqrt(nv / c2) + ADAM_EPS) + ADAM_WD * w_ref[...])


def _adamw(w, g, m, v, name):
    R, C = w.shape
    tr = _row_tile(R, 256)
    spec = pl.BlockSpec((tr, C), lambda i: (i, 0))
    return pl.pallas_call(
        functools.partial(_adamw_update), grid=(R // tr,), name=name,
        in_specs=[spec] * 4, out_specs=[spec] * 3,
        out_shape=[jax.ShapeDtypeStruct((R, C), F32)] * 3,
    )(w, g, m, v)


def _ada_fwd(c_all, w_ada, b_cols):
    def body(c_ref, w_ref, b_ref, o_ref):
        c = c_ref[...]
        o_ref[...] = jnp.dot(c * _sigmoid(c), w_ref[...], preferred_element_type=F32, precision=HIGHEST) + b_ref[...]
    return pl.pallas_call(
        body, name="ada_fwd", out_shape=jax.ShapeDtypeStruct((c_all.shape[0], w_ada.shape[1]), F32),
        compiler_params=_cp(),
    )(c_all, w_ada, b_cols)


def _ada_bwd(c_all, dmod_cols):
    def body(c_ref, d_ref, o_ref):
        c = c_ref[...]
        o_ref[...] = lax.dot_general(c * _sigmoid(c), d_ref[...], (((0,), (0,)), ((), ())),
                                     preferred_element_type=F32, precision=HIGHEST)
    return pl.pallas_call(
        body, name="ada_bwd", out_shape=jax.ShapeDtypeStruct((c_all.shape[1], dmod_cols.shape[1]), F32),
        compiler_params=_cp(),
    )(c_all, dmod_cols)


def _small_reduce(gathered, n_dev, Bl):
    mod_rows, _, rows = _small_layout(Bl)
    width = gathered.shape[1]

    def body(g_ref, red_ref, bada_ref):
        acc = g_ref[0:rows, :]
        for d in range(1, n_dev):
            acc = acc + g_ref[d * rows:(d + 1) * rows, :]
        red_ref[...] = acc[mod_rows:, :]
        b = acc[0:8, :]
        for q in range(1, Bl):
            b = b + acc[8 * q:8 * q + 8, :]
        bada_ref[...] = b
    return pl.pallas_call(
        body, name="small_reduce",
        out_shape=[jax.ShapeDtypeStruct((rows - mod_rows, width), F32), jax.ShapeDtypeStruct((8, width), F32)],
        compiler_params=_cp(),
    )(gathered)


def _mesh_pos():
    return lax.axis_index("x"), lax.axis_index("y"), lax.axis_index("c")


def _other_chips(x, y):
    return [(1 - x, y), (x, 1 - y), (1 - x, 1 - y)]


def _allgather8(xs, name):
    m_per, n = xs.shape

    def body(x_ref, out_ref, send_sems, recv_sems, local_sem):
        x, y, c = _mesh_pos()
        me, sibling = (x, y, c), (x, y, 1 - c)
        chips = _other_chips(x, y)

        def rows(px, py, pc):
            return out_ref.at[pl.ds((4 * px + 2 * py + pc) * m_per, m_per), :]

        def copy(k, block, to, src=None):
            return pltpu.make_async_remote_copy(
                src_ref=rows(*block) if src is None else src, dst_ref=rows(*block),
                send_sem=send_sems.at[k], recv_sem=recv_sems.at[k], device_id=to, device_id_type=MESH_DEV)

        mine = pltpu.make_async_copy(x_ref, rows(*me), local_sem)
        mine.start()
        first = [copy(0, me, sibling, src=x_ref)]
        first += [copy(1 + j, me, (*chip, c), src=x_ref) for j, chip in enumerate(chips)]
        for cp in first:
            cp.start()
        passed = [copy(4 + j, (*chip, c), sibling) for j, chip in enumerate(chips)]
        for j, chip in enumerate(chips):
            copy(1 + j, (*chip, c), me).wait_recv()
            passed[j].start()
        copy(0, sibling, me).wait_recv()
        for j, chip in enumerate(chips):
            copy(4 + j, (*chip, 1 - c), me).wait_recv()
        for cp in first + passed:
            cp.wait_send()
        mine.wait()

    return pl.pallas_call(
        body, name=name, out_shape=jax.ShapeDtypeStruct((8 * m_per, n), xs.dtype),
        in_specs=[pl.BlockSpec(memory_space=pltpu.VMEM)], out_specs=pl.BlockSpec(memory_space=pltpu.VMEM),
        scratch_shapes=[pltpu.SemaphoreType.DMA((7,)), pltpu.SemaphoreType.DMA((7,)), pltpu.SemaphoreType.DMA],
        compiler_params=_cp(),
    )(xs)


class _WeightGather:
    def __init__(self, shapes):
        self.shapes = shapes
        self.n = len(shapes)

    def scratch(self):
        return [pltpu.SemaphoreType.DMA((6 * self.n,)), pltpu.SemaphoreType.DMA((6 * self.n,))]

    def _copy(self, outs, sems, w, k, slot, h, to):
        r2 = self.shapes[w][1] // 2
        blk = outs[w].at[slot, pl.ds(h * r2, r2), :]
        return pltpu.make_async_remote_copy(
            src_ref=blk, dst_ref=blk, send_sem=sems[0].at[6 * w + k], recv_sem=sems[1].at[6 * w + k],
            device_id=to, device_id_type=MESH_DEV)

    def start(self, outs, sems):
        x, y, c = _mesh_pos()
        for w in range(self.n):
            for k, chip in enumerate(_other_chips(x, y)):
                self._copy(outs, sems, w, k, 2 * x + y, c, (*chip, c)).start()

    def forward(self, outs, sems):
        x, y, c = _mesh_pos()
        for w in range(self.n):
            for k, chip in enumerate(_other_chips(x, y)):
                slot = 2 * chip[0] + chip[1]
                self._copy(outs, sems, w, k, slot, c, (x, y, 1 - c)).wait_recv()
                self._copy(outs, sems, w, 3 + k, slot, c, (x, y, 1 - c)).start()

    def finish(self, outs, sems):
        x, y, c = _mesh_pos()
        for w in range(self.n):
            for k, chip in enumerate(_other_chips(x, y)):
                slot = 2 * chip[0] + chip[1]
                self._copy(outs, sems, w, 3 + k, slot, 1 - c, (x, y, 1 - c)).wait_recv()
                self._copy(outs, sems, w, k, 2 * x + y, c, (*chip, c)).wait_send()
                self._copy(outs, sems, w, 3 + k, slot, c, (x, y, 1 - c)).wait_send()


def _gather_weights(bufs, name):
    n = len(bufs)
    plan = _WeightGather([b.shape for b in bufs])

    def body(*refs):
        outs = refs[n:2 * n]
        sems = refs[2 * n:]
        plan.start(outs, sems)
        plan.forward(outs, sems)
        plan.finish(outs, sems)

    anyspec = pl.BlockSpec(memory_space=pl.ANY)
    return pl.pallas_call(
        body, name=name,
        out_shape=[jax.ShapeDtypeStruct(b.shape, b.dtype) for b in bufs],
        in_specs=[anyspec] * n, out_specs=[anyspec] * n,
        input_output_aliases={w: w for w in range(n)},
        scratch_shapes=plan.scratch(),
    )(*bufs)


class _SiblingExchange:
    def __init__(self, shapes):
        self.shapes = shapes

    def scratch(self):
        n = sum(s[0] for s in self.shapes)
        return [pltpu.SemaphoreType.DMA((n,)), pltpu.SemaphoreType.DMA((n,))]

    def out_shapes(self, dtype):
        return [jax.ShapeDtypeStruct((s[0], s[1] // 2, s[2]), dtype) for s in self.shapes]

    def _copies(self, ins, outs, sems):
        x, y, c = _mesh_pos()
        cps, k = [], 0
        for w, (P, R, _) in enumerate(self.shapes):
            r2 = R // 2
            for p in range(P):
                cps.append(pltpu.make_async_remote_copy(
                    src_ref=ins[w].at[p, pl.ds((1 - c) * r2, r2), :], dst_ref=outs[w].at[p],
                    send_sem=sems[0].at[k], recv_sem=sems[1].at[k],
                    device_id=(x, y, 1 - c), device_id_type=MESH_DEV))
                k += 1
        return cps

    def start(self, ins, outs, sems):
        for cp in self._copies(ins, outs, sems):
            cp.start()

    def forward(self, ins, outs, sems):
        pass

    def finish(self, ins, outs, sems):
        for cp in self._copies(ins, outs, sems):
            cp.wait()


def _sibling_host(grads):
    plan = _SiblingExchange([g.shape for g in grads])
    return plan, tuple(grads), tuple(plan.out_shapes(grads[0].dtype))


def _rs_sibling(grads, name):
    n = len(grads)
    plan, _, out_shapes = _sibling_host(grads)

    def body(*refs):
        ins, outs, sems = refs[:n], refs[n:2 * n], refs[2 * n:]
        plan.start(ins, outs, sems)
        plan.finish(ins, outs, sems)

    anyspec = pl.BlockSpec(memory_space=pl.ANY)
    return pl.pallas_call(
        body, name=name, out_shape=list(out_shapes),
        in_specs=[anyspec] * n, out_specs=[anyspec] * n, scratch_shapes=plan.scratch(),
    )(*grads)


class _SmallGather:
    def __init__(self, m_per):
        self.m = m_per

    def scratch(self):
        return [pltpu.SemaphoreType.DMA((7,)), pltpu.SemaphoreType.DMA((7,)), pltpu.SemaphoreType.DMA]

    def _rows(self, out, pos):
        px, py, pc = pos
        return out.at[pl.ds((4 * px + 2 * py + pc) * self.m, self.m), :]

    def _copy(self, out, sems, k, block, to, src=None):
        dst = self._rows(out, block)
        return pltpu.make_async_remote_copy(
            src_ref=dst if src is None else src, dst_ref=dst, send_sem=sems[0].at[k], recv_sem=sems[1].at[k],
            device_id=to, device_id_type=MESH_DEV)

    def start(self, ins, outs, sems):
        x, y, c = _mesh_pos()
        me = (x, y, c)
        pltpu.make_async_copy(ins[0], self._rows(outs[0], me), sems[2]).start()
        self._copy(outs[0], sems, 0, me, (x, y, 1 - c), src=ins[0]).start()
        for j, chip in enumerate(_other_chips(x, y)):
            self._copy(outs[0], sems, 1 + j, me, (*chip, c), src=ins[0]).start()

    def forward(self, ins, outs, sems):
        x, y, c = _mesh_pos()
        for j, chip in enumerate(_other_chips(x, y)):
            self._copy(outs[0], sems, 1 + j, (*chip, c), (x, y, c)).wait_recv()
            self._copy(outs[0], sems, 4 + j, (*chip, c), (x, y, 1 - c)).start()

    def finish(self, ins, outs, sems):
        x, y, c = _mesh_pos()
        me = (x, y, c)
        self._copy(outs[0], sems, 0, (x, y, 1 - c), me).wait_recv()
        for j, chip in enumerate(_other_chips(x, y)):
            self._copy(outs[0], sems, 4 + j, (*chip, 1 - c), me).wait_recv()
        self._copy(outs[0], sems, 0, me, (x, y, 1 - c), src=ins[0]).wait_send()
        for j, chip in enumerate(_other_chips(x, y)):
            self._copy(outs[0], sems, 1 + j, me, (*chip, c), src=ins[0]).wait_send()
            self._copy(outs[0], sems, 4 + j, (*chip, c), (x, y, 1 - c)).wait_send()
        pltpu.make_async_copy(ins[0], self._rows(outs[0], me), sems[2]).wait()


def _small_gather_host(packed):
    m, n = packed.shape
    return _SmallGather(m), (packed,), (jax.ShapeDtypeStruct((8 * m, n), packed.dtype),)


class _ChipExchange:
    def __init__(self, n):
        self.n = n

    def scratch(self):
        return [pltpu.SemaphoreType.DMA((3 * self.n,)), pltpu.SemaphoreType.DMA((3 * self.n,))]

    def _copies(self, ins, outs, sems):
        x, y, c = _mesh_pos()
        return [pltpu.make_async_remote_copy(
            src_ref=ins[w].at[2 * chip[0] + chip[1]], dst_ref=outs[w].at[k],
            send_sem=sems[0].at[3 * w + k], recv_sem=sems[1].at[3 * w + k],
            device_id=(*chip, c), device_id_type=MESH_DEV)
            for w in range(self.n) for k, chip in enumerate(_other_chips(x, y))]

    def start(self, ins, outs, sems):
        for cp in self._copies(ins, outs, sems):
            cp.start()

    def forward(self, ins, outs, sems):
        pass

    def finish(self, ins, outs, sems):
        for cp in self._copies(ins, outs, sems):
            cp.wait()


def _rs_chips(sums, name):
    n = len(sums)
    plan = _ChipExchange(n)

    def body(*refs):
        ins, outs, sems = refs[:n], refs[n:2 * n], refs[2 * n:]
        plan.start(ins, outs, sems)
        plan.finish(ins, outs, sems)

    anyspec = pl.BlockSpec(memory_space=pl.ANY)
    return pl.pallas_call(
        body, name=name,
        out_shape=[jax.ShapeDtypeStruct((3,) + s.shape[1:], s.dtype) for s in sums],
        in_specs=[anyspec] * n, out_specs=[anyspec] * n,
        scratch_shapes=plan.scratch(),
    )(*sums)


def _rs_final(bufs, name, chips=()):
    n, nc = len(bufs), len(chips)
    plan = _ChipExchange(nc)

    def body(*refs):
        cin = refs[n:n + nc]
        outs = refs[n + nc:2 * n + nc]
        cout = refs[2 * n + nc:2 * n + 2 * nc]
        send_sems, recv_sems = refs[2 * n + 2 * nc:2 * n + 2 * nc + 2]
        csems = refs[2 * n + 2 * nc + 2:]
        x, y, c = _mesh_pos()
        if nc:
            plan.start(cin, cout, csems)
        cps = []
        for w in range(n):
            r2 = bufs[w].shape[0] // 2
            mine = outs[w].at[pl.ds(c * r2, r2), :]
            cps.append(pltpu.make_async_remote_copy(
                src_ref=mine, dst_ref=mine, send_sem=send_sems.at[w], recv_sem=recv_sems.at[w],
                device_id=(x, y, 1 - c), device_id_type=MESH_DEV))
            cps[-1].start()
        for cp in cps:
            cp.wait()
        if nc:
            plan.finish(cin, cout, csems)

    anyspec = pl.BlockSpec(memory_space=pl.ANY)
    return pl.pallas_call(
        body, name=name,
        out_shape=[jax.ShapeDtypeStruct(b.shape, b.dtype) for b in bufs]
                  + [jax.ShapeDtypeStruct((3,) + s.shape[1:], s.dtype) for s in chips],
        in_specs=[anyspec] * (n + nc), out_specs=[anyspec] * (n + nc),
        input_output_aliases={w: w for w in range(n)},
        scratch_shapes=[pltpu.SemaphoreType.DMA((n,)), pltpu.SemaphoreType.DMA((n,))] + (plan.scratch() if nc else []),
    )(*bufs, *chips)


BIG = ("w_in", "w_out", "w_gate", "w_up", "w_down")
TRANSPOSED = ("w_gate", "w_up")
WEIGHTS = ("w_ada", "b_ada", "g_mix", "w_in", "w_dw", "b_dw", "g_conv_ln", "b_conv_ln", "g_q", "g_k",
           "w_out", "g_ffn", "w_gate", "w_up", "w_down")


def _pad_to(a, rows, cols):
    return jnp.pad(a, ((0, rows - a.shape[0]), (0, cols - a.shape[1])))


def kernel(x, c, w_ada, b_ada, g_mix, w_in, w_dw, b_dw, g_conv_ln, b_conv_ln, g_q, g_k, w_out, g_ffn, w_gate, w_up, w_down, loss_target, m_w_ada, m_b_ada, m_g_mix, m_w_in, m_w_dw, m_b_dw, m_g_conv_ln, m_b_conv_ln, m_g_q, m_g_k, m_w_out, m_g_ffn, m_w_gate, m_w_up, m_w_down, v_w_ada, v_b_ada, v_g_mix, v_w_in, v_w_dw, v_b_dw, v_g_conv_ln, v_b_conv_ln, v_g_q, v_g_k, v_w_out, v_g_ffn, v_w_gate, v_w_up, v_w_down):
    w = dict(w_ada=w_ada, b_ada=b_ada, g_mix=g_mix, w_in=w_in, w_dw=w_dw, b_dw=b_dw, g_conv_ln=g_conv_ln,
             b_conv_ln=b_conv_ln, g_q=g_q, g_k=g_k, w_out=w_out, g_ffn=g_ffn, w_gate=w_gate, w_up=w_up, w_down=w_down)
    m = dict(w_ada=m_w_ada, b_ada=m_b_ada, g_mix=m_g_mix, w_in=m_w_in, w_dw=m_w_dw, b_dw=m_b_dw, g_conv_ln=m_g_conv_ln,
             b_conv_ln=m_b_conv_ln, g_q=m_g_q, g_k=m_g_k, w_out=m_w_out, g_ffn=m_g_ffn, w_gate=m_w_gate, w_up=m_w_up,
             w_down=m_w_down)
    v = dict(w_ada=v_w_ada, b_ada=v_b_ada, g_mix=v_g_mix, w_in=v_w_in, w_dw=v_w_dw, b_dw=v_b_dw, g_conv_ln=v_g_conv_ln,
             b_conv_ln=v_b_conv_ln, g_q=v_g_q, g_k=v_g_k, w_out=v_w_out, g_ffn=v_g_ffn, w_gate=v_w_gate, w_up=v_w_up,
             w_down=v_w_down)
    Bl, S, D = x.shape
    DC = g_conv_ln.shape[1]
    NA = w_ada.shape[2]
    xi, yi, ci = _mesh_pos()
    p = 2 * xi + yi
    dev = 2 * p + ci
    n_dev = 8
    pidx = jnp.reshape(p, (1,)).astype(jnp.int32)
    pc_idx = jnp.stack([p, ci]).astype(jnp.int32)

    first = jnp.concatenate([_pad_to(c, 8, D), _pad_to(w_dw[0], CONV_ROWS, D)], axis=0)
    g0 = _allgather8(first, "gather_cond").reshape(n_dev, 8 + CONV_ROWS, D)
    c_all = g0[:, :Bl].reshape(n_dev * Bl, D)
    taps = jnp.concatenate([g0[2 * q, 8:, :w_dw.shape[2]] for q in range(4)], axis=1)
    wdw = jnp.where(lax.broadcasted_iota(jnp.int32, taps.shape, 0) == CONV_WIDTH, b_dw, taps)
    shard = lambda a, nm: a[0].T if nm in TRANSPOSED else a[0]
    owned = {nm: _cast_weight(shard(w[nm], nm), pidx, "cast_" + nm) for nm in BIG}
    (w_in_full,) = _gather_weights([owned["w_in"]], "gather_w_in")

    b_cols = lax.dynamic_slice_in_dim(b_ada, p * NA, NA, axis=1)
    mod_part = _ada_fwd(c_all, w_ada[0], b_cols)
    gm = _allgather8(mod_part, "gather_mod").reshape(n_dev, n_dev * Bl, NA)
    mod = jnp.concatenate([lax.dynamic_slice_in_dim(gm[2 * q], dev * Bl, Bl, axis=0) for q in range(4)], axis=1)

    loc = _local_step(x, loss_target, mod, g_mix, wdw, g_conv_ln, b_conv_ln, g_q, g_k, g_ffn,
                      w_in_full, owned["w_out"], owned["w_gate"], owned["w_up"], owned["w_down"], pc_idx=pc_idx)

    halves = [_final_add(s32, r, pc_idx, "final_add_" + nm)
              for nm, s32, r in zip(EARLY_WEIGHTS, loc["early_sums"], loc["early_recv"])]
    (late_sib,) = _rs_sibling([loc["grads"]["w_in"]], "rs_sibling_in")
    late32, late16 = _pair_add(loc["grads"]["w_in"], late_sib, pc_idx, "pair_add_w_in")
    *early_full, late_recv = _rs_final(halves, "rs_final_early", chips=(late16,))
    grad = dict(zip(EARLY_WEIGHTS, early_full))
    (grad["w_in"],) = _rs_final([_final_add(late32, late_recv, pc_idx, "final_add_w_in")], "rs_final_in")

    mod_rows, _, small_rows = _small_layout(Bl)
    gs = loc["gathered_small"]
    red, bada8 = _small_reduce(gs, n_dev, Bl)
    dmod_all = gs.reshape(n_dev, small_rows, D)[:, :mod_rows].reshape(n_dev * Bl, 8, D)[:, :N_MOD].reshape(n_dev * Bl, N_MOD * D)
    grad["w_ada"] = _ada_bwd(c_all, lax.dynamic_slice_in_dim(dmod_all, p * NA, NA, axis=1))
    grad["b_ada"] = bada8[:N_MOD].reshape(1, N_MOD * D)
    grad["g_mix"] = red[0:1]
    grad["g_ffn"] = red[1:2]
    grad["g_conv_ln"] = red[2:3, :DC]
    grad["b_conv_ln"] = red[2:3, DC:2 * DC]
    grad["g_q"] = red[3:4, :HEAD_DIM]
    grad["g_k"] = red[3:4, HEAD_DIM:2 * HEAD_DIM]
    loss = red[4, 0]
    dwdw = red[8:8 + CONV_ROWS, :DC]
    grad["w_dw"] = lax.dynamic_slice_in_dim(dwdw[:CONV_WIDTH], p * w_dw.shape[2], w_dw.shape[2], axis=1)
    grad["b_dw"] = dwdw[CONV_WIDTH:CONV_WIDTH + 1]

    delta, new_m, new_v = {}, {}, {}
    for nm in WEIGHTS:
        shp = w[nm].shape
        if nm in TRANSPOSED:
            d_, m_, v_ = _adamw(w[nm][0].T, grad[nm], m[nm][0].T, v[nm][0].T, "adamw_" + nm)
            grad[nm], delta[nm], new_m[nm], new_v[nm] = (a.T.reshape(shp) for a in (grad[nm], d_, m_, v_))
            continue
        two_d = (shp[-2], shp[-1]) if len(shp) == 3 else shp
        d_, m_, v_ = _adamw(w[nm].reshape(two_d), grad[nm].reshape(two_d), m[nm].reshape(two_d), v[nm].reshape(two_d),
                            "adamw_" + nm)
        grad[nm] = grad[nm].reshape(shp)
        delta[nm], new_m[nm], new_v[nm] = d_.reshape(shp), m_.reshape(shp), v_.reshape(shp)

    return (loss, loc["dx"], *[grad[nm] for nm in WEIGHTS], *[delta[nm] for nm in WEIGHTS],
            *[new_m[nm] for nm in WEIGHTS], *[new_v[nm] for nm in WEIGHTS])
```

```python
import functools
import math

import jax
import jax.numpy as jnp
import numpy as np
from jax import lax
from jax.experimental import pallas as pl
from jax.experimental.pallas import tpu as pltpu

F32 = jnp.float32
MXU_DTYPE = jnp.bfloat16
ACT_DTYPE = jnp.bfloat16
EPS = 1e-6
NEG_INF = -1e30
HEAD_DIM = 64
LANES = 128
RADIUS = 64
QBLK = 128
DILATIONS = (1, 4, 16)
CONV_WIDTH = 31
CONV_PAD = CONV_WIDTH // 2
CONV_ROWS = 32
N_MOD = 6
ADAM_LR, ADAM_B1, ADAM_B2, ADAM_EPS, ADAM_WD, ADAM_STEP = 0.001, 0.9, 0.999, 1e-08, 0.01, 10
HIGHEST = lax.Precision.HIGHEST
MESH_DEV = pl.DeviceIdType.MESH
VMEM_LIMIT = 56 << 20


def _cp(sem=None, vmem=VMEM_LIMIT):
    kw = dict(vmem_limit_bytes=vmem)
    if sem is not None:
        kw["dimension_semantics"] = sem
    return pltpu.CompilerParams(**kw)


def _sigmoid(x):
    return 1.0 / (1.0 + jnp.exp(-x))


def _dot(a, b):
    return jnp.dot(a, b, preferred_element_type=F32)


def _dot_nt(a, b):
    return lax.dot_general(a, b, (((1,), (1,)), ((), ())), preferred_element_type=F32)


def _dot_tn(a, b):
    return lax.dot_general(a, b, (((0,), (0,)), ((), ())), preferred_element_type=F32)


def _colsum(v):
    return jnp.sum(v, axis=0, keepdims=True)


def _load_resident(i, pairs, sems):
    @pl.when(i == 0)
    def _():
        cps = [pltpu.make_async_copy(src, dst, sems.at[n]) for n, (src, dst) in enumerate(pairs)]
        for c in cps:
            c.start()
        for c in cps:
            c.wait()


def _fwd_in(x2, mod, g_mix, gq2, gk2, w_in, *, S, tm, n_ag):
    T, D = x2.shape
    P, _, Nb = w_in.shape
    n_in = P * Nb
    n_slab = (n_in - n_ag) // LANES
    NS = n_slab // 3
    tps = S // tm

    def body(x_ref, mod_ref, g_ref, gq_ref, gk_ref, w_ref, ag_ref, qkv_ref, qkh_ref, h_ref):
        x = x_ref[...]
        r = lax.rsqrt(jnp.mean(x * x, axis=-1, keepdims=True) + EPS)
        n = x * r * g_ref[...]
        h = n * (1.0 + mod_ref[:, D:2 * D]) + mod_ref[:, 0:D]
        hb = h.astype(MXU_DTYPE)
        h_ref[...] = hb
        parts = [_dot(hb, w_ref[p]) for p in range(P)]
        proj = jnp.concatenate(parts, axis=1) if P > 1 else parts[0]
        ag_ref[...] = proj[:, :n_ag]
        mm = _head_mean_matrix()
        for j in range(n_slab):
            v = proj[:, n_ag + LANES * j:n_ag + LANES * (j + 1)]
            qkv_ref[j] = v
            if j < 2 * NS:
                gain = gq_ref[...] * (HEAD_DIM ** -0.5 * LOG2E) if j < NS else gk_ref[...]
                qkh_ref[j] = v * lax.rsqrt(_head_mean(v * v, mm) + EPS) * gain

    return pl.pallas_call(
        body, grid=(T // tm,), name="fwd_in",
        in_specs=[pl.BlockSpec((tm, D), lambda i: (i, 0)),
                  pl.BlockSpec((None, 1, N_MOD * D), lambda i: (i // tps, 0, 0)),
                  pl.BlockSpec((1, D), lambda i: (0, 0)),
                  pl.BlockSpec((1, LANES), lambda i: (0, 0)), pl.BlockSpec((1, LANES), lambda i: (0, 0)),
                  pl.BlockSpec((P, D, Nb), lambda i: (0, 0, 0))],
        out_specs=[pl.BlockSpec((tm, n_ag), lambda i: (i, 0)),
                   pl.BlockSpec((n_slab, tm, LANES), lambda i: (0, i, 0)),
                   pl.BlockSpec((2 * NS, tm, LANES), lambda i: (0, i, 0)),
                   pl.BlockSpec((tm, D), lambda i: (i, 0))],
        out_shape=[jax.ShapeDtypeStruct((T, n_ag), F32),
                   jax.ShapeDtypeStruct((n_slab, T, LANES), F32),
                   jax.ShapeDtypeStruct((2 * NS, T, LANES), F32),
                   jax.ShapeDtypeStruct((T, D), MXU_DTYPE)],
        compiler_params=_cp(("arbitrary",)),
    )(x2, mod, g_mix, gq2, gk2, w_in)


CONV_CH = 64


def _conv_taps(win, w_ref, acc, reverse):
    n = win.shape[0]
    for b in range(8):
        wb = win if b == 0 else pltpu.roll(win, shift=n - b, axis=0)
        for a in range(4):
            o = 8 * a + b
            if o < 1 or o > CONV_WIDTH:
                continue
            k = (CONV_WIDTH - o) if reverse else (o - 1)
            acc = acc + w_ref[k:k + 1, :] * wb[8 * a:8 * a + CONV_CH, :]
    return acc


def _conv_fwd(ag, wdw, *, Bl, S, DC):
    T = ag.shape[0]
    nsc = DC // LANES
    CH = CONV_CH

    def body(a_ref, g_ref, w_ref, cv_ref, upad):
        zeros16 = jnp.zeros((16, LANES), F32)
        upad[0:16, :] = zeros16
        upad[S + 16:S + 32, :] = zeros16

        def fill(i, _):
            r0 = pl.multiple_of(i * CH, CH)
            a = a_ref[pl.ds(r0, CH), :]
            g = g_ref[pl.ds(r0, CH), :]
            upad[pl.ds(r0 + 16, CH), :] = a * _sigmoid(g)
            return 0
        lax.fori_loop(0, S // CH, fill, 0)

        def conv(i, _):
            r0 = pl.multiple_of(i * CH, CH)
            win = upad[pl.ds(r0, CH + 32), :]
            acc = jnp.zeros((CH, LANES), F32) + w_ref[CONV_WIDTH:CONV_WIDTH + 1, :]
            cv_ref[pl.ds(r0, CH), :] = _conv_taps(win, w_ref, acc, reverse=False)
            return 0
        lax.fori_loop(0, S // CH, conv, 0)

    return pl.pallas_call(
        body, grid=(Bl, nsc), name="conv_fwd",
        in_specs=[pl.BlockSpec((S, LANES), lambda b, j: (b, j)),
                  pl.BlockSpec((S, LANES), lambda b, j: (b, nsc + j)),
                  pl.BlockSpec((CONV_ROWS, LANES), lambda b, j: (0, j))],
        out_specs=pl.BlockSpec((S, LANES), lambda b, j: (b, j)),
        out_shape=jax.ShapeDtypeStruct((T, DC), F32),
        scratch_shapes=[pltpu.VMEM((S + 32, LANES), F32)],
        compiler_params=_cp(("arbitrary", "arbitrary")),
    )(ag, ag, wdw)


def _conv_bwd(ag, dcv, wdw, *, Bl, S, DC):
    T = ag.shape[0]
    nsc = DC // LANES
    CH = CONV_CH

    def body(a_ref, g_ref, d_ref, w_ref, da_ref, dg_ref, dw_ref, upad, dpad, wacc):
        b = pl.program_id(1)
        zeros16 = jnp.zeros((16, LANES), F32)
        upad[0:16, :] = zeros16
        upad[S + 16:S + 32, :] = zeros16
        dpad[0:16, :] = zeros16
        dpad[S + 16:S + 32, :] = zeros16

        @pl.when(b == 0)
        def _():
            wacc[...] = jnp.zeros_like(wacc)

        def fill(i, _):
            r0 = pl.multiple_of(i * CH, CH)
            a = a_ref[pl.ds(r0, CH), :]
            g = g_ref[pl.ds(r0, CH), :]
            upad[pl.ds(r0 + 16, CH), :] = a * _sigmoid(g)
            dpad[pl.ds(r0 + 16, CH), :] = d_ref[pl.ds(r0, CH), :]
            return 0
        lax.fori_loop(0, S // CH, fill, 0)

        def step(i, _):
            r0 = pl.multiple_of(i * CH, CH)
            dwin = dpad[pl.ds(r0, CH + 32), :]
            du = _conv_taps(dwin, w_ref, jnp.zeros((CH, LANES), F32), reverse=True)
            a = a_ref[pl.ds(r0, CH), :]
            g = g_ref[pl.ds(r0, CH), :]
            sg = _sigmoid(g)
            da_ref[pl.ds(r0, CH), :] = du * sg
            dg_ref[pl.ds(r0, CH), :] = du * a * sg * (1.0 - sg)
            dc = d_ref[pl.ds(r0, CH), :]
            uwin = upad[pl.ds(r0, CH + 32), :]
            n = CH + 32
            for bb in range(8):
                wb = uwin if bb == 0 else pltpu.roll(uwin, shift=n - bb, axis=0)
                for aa in range(4):
                    o = 8 * aa + bb
                    if o < 1 or o > CONV_WIDTH:
                        continue
                    k = o - 1
                    prod = dc * wb[8 * aa:8 * aa + CH, :]
                    part = prod[0:8, :]
                    for q in range(1, CH // 8):
                        part = part + prod[8 * q:8 * q + 8, :]
                    wacc[8 * k:8 * k + 8, :] += part
            part = dc[0:8, :]
            for q in range(1, CH // 8):
                part = part + dc[8 * q:8 * q + 8, :]
            wacc[8 * CONV_WIDTH:8 * CONV_WIDTH + 8, :] += part
            return 0
        lax.fori_loop(0, S // CH, step, 0)

        @pl.when(b == Bl - 1)
        def _():
            for k in range(CONV_ROWS):
                dw_ref[k:k + 1, :] = jnp.sum(wacc[8 * k:8 * k + 8, :], axis=0, keepdims=True)

    return pl.pallas_call(
        body, grid=(nsc, Bl), name="conv_bwd",
        in_specs=[pl.BlockSpec((S, LANES), lambda j, b: (b, j)),
                  pl.BlockSpec((S, LANES), lambda j, b: (b, nsc + j)),
                  pl.BlockSpec((S, LANES), lambda j, b: (b, j)),
                  pl.BlockSpec((CONV_ROWS, LANES), lambda j, b: (0, j))],
        out_specs=[pl.BlockSpec((S, LANES), lambda j, b: (b, j)),
                   pl.BlockSpec((S, LANES), lambda j, b: (b, j)),
                   pl.BlockSpec((CONV_ROWS, LANES), lambda j, b: (0, j))],
        out_shape=[jax.ShapeDtypeStruct((T, DC), F32), jax.ShapeDtypeStruct((T, DC), F32),
                   jax.ShapeDtypeStruct((CONV_ROWS, DC), F32)],
        scratch_shapes=[pltpu.VMEM((S + 32, LANES), F32), pltpu.VMEM((S + 32, LANES), F32),
                        pltpu.VMEM((8 * CONV_ROWS, LANES), F32)],
        compiler_params=_cp(("arbitrary", "arbitrary")),
    )(ag, ag, dcv, wdw)


ROWCH = 256


LOG2E = 1.4426950408889634
LN2 = 0.6931471805599453
N_EDGE = 4


def _head_mean_matrix():
    r = lax.broadcasted_iota(jnp.int32, (LANES, LANES), 0) // HEAD_DIM
    c = lax.broadcasted_iota(jnp.int32, (LANES, LANES), 1) // HEAD_DIM
    return jnp.where(r == c, 1.0 / HEAD_DIM, 0.0).astype(jnp.bfloat16)


def _head_mean(v, mm):
    hi = v.astype(jnp.bfloat16)
    lo = (v - hi.astype(F32)).astype(jnp.bfloat16)
    return _dot(hi, mm) + _dot(lo, mm)


def _stack_heads(blk, lane_lo):
    z = jnp.zeros_like(blk)
    return jnp.concatenate([jnp.where(lane_lo, blk, z), jnp.where(lane_lo, z, blk)], axis=0)


def _merge_heads(v2, lane_lo):
    return jnp.where(lane_lo, v2[:QBLK], v2[QBLK:])


def _bias_tables(bias_ref, slope_ref):
    row = lax.broadcasted_iota(jnp.int32, (2 * QBLK, 2 * QBLK), 0)
    col = lax.broadcasted_iota(jnp.int32, (2 * QBLK, 2 * QBLK), 1)
    rel = jnp.abs(col - RADIUS - (row % QBLK))
    slope = jnp.where(row < QBLK, slope_ref[0:1, 0:1], slope_ref[0:1, HEAD_DIM:HEAD_DIM + 1]) * LOG2E
    for pi, d in enumerate(DILATIONS):
        inside = jnp.where(rel <= RADIUS, -slope * (float(d) * rel.astype(F32)), NEG_INF)
        for e in range(N_EDGE):
            t = inside
            if e & 1:
                t = jnp.where(col < RADIUS, NEG_INF, t)
            if e & 2:
                t = jnp.where(col >= QBLK + RADIUS, NEG_INF, t)
            bias_ref[N_EDGE * pi + e] = t


def _edge_index(qb, nb):
    return jnp.where(qb == 0, 1, 0) + jnp.where(qb == nb - 1, 2, 0)


def _gather_rows(src_ref, dst_ref, S, d, pad):
    n = S // d
    seg = n + 2 * RADIUS if pad else n
    step = min(n, 512)
    for r in range(d):
        base = r * seg
        if pad:
            dst_ref[base:base + RADIUS, :] = jnp.zeros((RADIUS, LANES), dst_ref.dtype)
            dst_ref[base + RADIUS + n:base + seg, :] = jnp.zeros((RADIUS, LANES), dst_ref.dtype)
            base += RADIUS
        for c0 in range(0, n, step):
            if d == 1:
                v = src_ref[c0:c0 + step, :]
            else:
                v = src_ref[pl.ds(r + c0 * d, step, stride=d), :]
            dst_ref[base + c0:base + c0 + step, :] = v.astype(dst_ref.dtype)


def _scatter_rows(src_ref, dst_ref, S, d, pad, accumulate):
    n = S // d
    seg = n + 2 * RADIUS if pad else n
    step = min(n, 512)
    for r in range(d):
        base = r * seg + (RADIUS if pad else 0)
        for c0 in range(0, n, step):
            v = src_ref[base + c0:base + c0 + step, :]
            if d == 1:
                idx = pl.ds(c0, step)
            else:
                idx = pl.ds(r + c0 * d, step, stride=d)
            if accumulate:
                dst_ref[idx, :] = dst_ref[idx, :] + v
            else:
                dst_ref[idx, :] = v


def _zero_uncovered(acc, S, d):
    n = S // d
    if (n // QBLK) % 2:
        return
    seg = n + 2 * RADIUS
    for r in range(d):
        acc[0, r * seg + n:r * seg + seg, :] = jnp.zeros((2 * RADIUS, LANES), F32)
        acc[1, r * seg:r * seg + 2 * RADIUS, :] = jnp.zeros((2 * RADIUS, LANES), F32)


def _scatter_parity(acc, dst_ref, S, d):
    n = S // d
    seg = n + 2 * RADIUS
    step = min(n, 512)
    one_block = (n // QBLK) % 2 == 1
    for r in range(d):
        base = r * seg + RADIUS
        for c0 in range(0, n, step):
            rows = slice(base + c0, base + c0 + step)
            v = acc[r % 2, rows, :] if one_block else acc[0, rows, :] + acc[1, rows, :]
            idx = pl.ds(c0, step) if d == 1 else pl.ds(r + c0 * d, step, stride=d)
            dst_ref[idx, :] = dst_ref[idx, :] + v


PIPE_UNROLL = 4
PIPE_SLOTS = 16
BWD_SLOTS = 12


def _pipeline(n_items, stages, unroll):
    K = len(stages)
    assert n_items % unroll == 0 and K * unroll <= (PIPE_SLOTS if K == 4 else BWD_SLOTS)
    trips = n_items // unroll
    assert trips >= K - 1

    def trip(t, static):
        for s in reversed(range(K)):
            if static and not 0 <= t - s < trips:
                continue
            for u in range(unroll):
                item = unroll * (t - s) + u
                stages[s](jnp.int32(item) if static else item)

    for t in range(K - 1):
        trip(t, True)

    def full(t, carry):
        trip(t, False)
        return carry
    lax.fori_loop(K - 1, trips, full, 0)
    for t in range(trips, trips + K - 1):
        trip(t, True)


def _attn_fwd(qkh, qkv, slopes, *, Bl, S, hosted=()):
    n3, T, _ = qkv.shape
    NS = n3 // 3
    NB = S // QBLK
    PADR = S + 2 * RADIUS * DILATIONS[-1]
    nh = len(hosted)
    plan = _WeightGather([b.shape for b in hosted]) if nh else None
    n_steps = Bl * NS

    def body(qh, kh, v_ref, slope_ref, *rest):
        o_ref, lse_ref = rest[nh:nh + 2]
        wouts = rest[nh + 2:2 * nh + 2]
        (qp, kp, vp, op, lp, onat, lnat, bias_ref, sbuf, pbuf, mbuf, lbuf) = rest[2 * nh + 2:2 * nh + 14]
        sems = rest[2 * nh + 14:]
        step = pl.program_id(0) * NS + pl.program_id(1)
        if nh:
            @pl.when(step == 0)
            def _():
                plan.start(wouts, sems)

            @pl.when(step == (3 * n_steps) // 4)
            def _():
                plan.forward(wouts, sems)

        lane_lo = lax.broadcasted_iota(jnp.int32, (QBLK, LANES), 1) < HEAD_DIM
        _bias_tables(bias_ref, slope_ref)

        for pi, d in enumerate(DILATIONS):
            n = S // d
            nb = n // QBLK
            _gather_rows(qh, qp, S, d, pad=False)
            _gather_rows(kh, kp, S, d, pad=True)
            _gather_rows(v_ref, vp, S, d, pad=True)

            def offsets(i, nb=nb):
                r = i // nb
                return pl.multiple_of(i * QBLK, QBLK), pl.multiple_of((i + r) * QBLK, QBLK), i % nb

            def scores(i, pi=pi, nb=nb):
                q0, k0, qb = offsets(i)
                qs = _stack_heads(qp[pl.ds(q0, QBLK), :], lane_lo)
                sbuf[i % PIPE_SLOTS] = (_dot_nt(qs, kp[pl.ds(k0, 2 * QBLK), :])
                                        + bias_ref[N_EDGE * pi + _edge_index(qb, nb)])

            def rowmax(i):
                m = jnp.max(sbuf[i % PIPE_SLOTS], axis=1, keepdims=True)
                mbuf[i % PIPE_SLOTS] = jnp.broadcast_to(m, (2 * QBLK, LANES))

            def expsum(i):
                m = mbuf[i % PIPE_SLOTS]
                p = jnp.exp2(sbuf[i % PIPE_SLOTS] - jnp.concatenate([m, m], axis=1))
                pbuf[i % PIPE_SLOTS] = p.astype(MXU_DTYPE)
                lbuf[i % PIPE_SLOTS] = jnp.broadcast_to(jnp.sum(p, axis=1, keepdims=True), (2 * QBLK, LANES))

            def values(i):
                q0, k0, _ = offsets(i)
                l = lbuf[i % PIPE_SLOTS]
                o2 = _dot(pbuf[i % PIPE_SLOTS], vp[pl.ds(k0, 2 * QBLK), :]) * (1.0 / l)
                op[pl.ds(q0, QBLK), :] = _merge_heads(o2, lane_lo)
                lp[pl.ds(q0, QBLK), :] = _merge_heads(mbuf[i % PIPE_SLOTS] + jnp.log2(l), lane_lo)

            _pipeline(NB, [scores, rowmax, expsum, values], PIPE_UNROLL)
            _scatter_rows(op, onat.at[pi], S, d, pad=False, accumulate=False)
            _scatter_rows(lp, lnat.at[pi], S, d, pad=False, accumulate=False)

        for c0 in range(0, S, ROWCH):
            ls = [lnat[pi, c0:c0 + ROWCH, :] for pi in range(len(DILATIONS))]
            mx = jnp.maximum(jnp.maximum(ls[0], ls[1]), ls[2])
            es = [jnp.exp2(l - mx) for l in ls]
            tot = es[0] + es[1] + es[2]
            inv = 1.0 / tot
            acc = (es[0] * inv) * onat[0, c0:c0 + ROWCH, :]
            for pi in (1, 2):
                acc = acc + (es[pi] * inv) * onat[pi, c0:c0 + ROWCH, :]
            o_ref[c0:c0 + ROWCH, :] = acc
            lse_ref[c0:c0 + ROWCH, :] = mx + jnp.log2(tot)

        if nh:
            @pl.when(step == n_steps - 1)
            def _():
                plan.finish(wouts, sems)

    spec_in = lambda off: pl.BlockSpec((None, S, LANES), lambda b, j: (off * NS + j, b, 0))
    out = pl.BlockSpec((S, LANES), lambda b, j: (b, j))
    anyspec = pl.BlockSpec(memory_space=pl.ANY)
    return pl.pallas_call(
        body, grid=(Bl, NS), name="attn_fwd",
        in_specs=[spec_in(0), spec_in(1), spec_in(2),
                  pl.BlockSpec((None, 8, LANES), lambda b, j: (j, 0, 0))] + [anyspec] * nh,
        out_specs=[out, out] + [anyspec] * nh,
        out_shape=[jax.ShapeDtypeStruct((T, NS * LANES), F32)] * 2
                  + [jax.ShapeDtypeStruct(b.shape, b.dtype) for b in hosted],
        input_output_aliases={4 + w: 2 + w for w in range(nh)},
        scratch_shapes=[pltpu.VMEM((S, LANES), MXU_DTYPE), pltpu.VMEM((PADR, LANES), MXU_DTYPE),
                        pltpu.VMEM((PADR, LANES), MXU_DTYPE),
                        pltpu.VMEM((S, LANES), F32), pltpu.VMEM((S, LANES), F32),
                        pltpu.VMEM((3, S, LANES), F32), pltpu.VMEM((3, S, LANES), F32),
                        pltpu.VMEM((N_EDGE * len(DILATIONS), 2 * QBLK, 2 * QBLK), F32),
                        pltpu.VMEM((PIPE_SLOTS, 2 * QBLK, 2 * QBLK), F32),
                        pltpu.VMEM((PIPE_SLOTS, 2 * QBLK, 2 * QBLK), MXU_DTYPE),
                        pltpu.VMEM((PIPE_SLOTS, 2 * QBLK, LANES), F32), pltpu.VMEM((PIPE_SLOTS, 2 * QBLK, LANES), F32)]
                       + (plan.scratch() if nh else []),
        compiler_params=_cp(("arbitrary", "arbitrary")),
    )(qkh, qkh, qkv, slopes, *hosted)


def _attn_bwd(qkh, qkv, o, lse, do, gq2, gk2, slopes, *, Bl, S, hosted=()):
    n3, T, _ = qkv.shape
    NS = n3 // 3
    NB = S // QBLK
    PADR = S + 2 * RADIUS * DILATIONS[-1]
    QSCALE = HEAD_DIM ** -0.5
    nh = len(hosted)
    plan = _ChipExchange(nh)
    n_steps = Bl * NS

    def body(qh, kh, q_ref, k_ref, v_ref, o_ref, lse_ref, do_ref, gq_ref, gk_ref, slope_ref, *rest):
        hin = rest[:nh]
        dq_ref, dk_ref, dv_ref, gacc_ref = rest[nh:nh + 4]
        hout = rest[nh + 4:2 * nh + 4]
        (dl, qp, kp, vp, dop, lp, dlp, dqp, dkacc, dvacc, dqn, dkn, bias_ref,
         sbuf, dpbuf, pbuf, dsbuf) = rest[2 * nh + 4:2 * nh + 21]
        sems = rest[2 * nh + 21:]
        step = pl.program_id(0) * NS + pl.program_id(1)

        @pl.when(step == 0)
        def _():
            gacc_ref[...] = jnp.zeros_like(gacc_ref)
            if nh:
                plan.start(hin, hout, sems)

        mm = _head_mean_matrix()
        lane_lo = lax.broadcasted_iota(jnp.int32, (QBLK, LANES), 1) < HEAD_DIM
        _bias_tables(bias_ref, slope_ref)
        for c0 in range(0, S, ROWCH):
            dl[c0:c0 + ROWCH, :] = _head_mean(do_ref[c0:c0 + ROWCH, :] * o_ref[c0:c0 + ROWCH, :], mm) * HEAD_DIM
            dqn[c0:c0 + ROWCH, :] = jnp.zeros((ROWCH, LANES), F32)
            dkn[c0:c0 + ROWCH, :] = jnp.zeros((ROWCH, LANES), F32)
            dv_ref[c0:c0 + ROWCH, :] = jnp.zeros((ROWCH, LANES), F32)

        for pi, d in enumerate(DILATIONS):
            n = S // d
            nb = n // QBLK
            _gather_rows(qh, qp, S, d, pad=False)
            _gather_rows(kh, kp, S, d, pad=True)
            _gather_rows(v_ref, vp, S, d, pad=True)
            _gather_rows(do_ref, dop, S, d, pad=False)
            _gather_rows(lse_ref, lp, S, d, pad=False)
            _gather_rows(dl, dlp, S, d, pad=False)
            _zero_uncovered(dkacc, S, d)
            _zero_uncovered(dvacc, S, d)

            def offsets(i, nb=nb):
                r = i // nb
                return pl.multiple_of(i * QBLK, QBLK), pl.multiple_of((i + r) * QBLK, QBLK), i % nb

            def scores(i, pi=pi, nb=nb):
                q0, k0, qb = offsets(i)
                qs = _stack_heads(qp[pl.ds(q0, QBLK), :], lane_lo)
                dos = _stack_heads(dop[pl.ds(q0, QBLK), :], lane_lo)
                sbuf[i % BWD_SLOTS] = (_dot_nt(qs, kp[pl.ds(k0, 2 * QBLK), :])
                                       + bias_ref[N_EDGE * pi + _edge_index(qb, nb)])
                dpbuf[i % BWD_SLOTS] = _dot_nt(dos, vp[pl.ds(k0, 2 * QBLK), :])

            def probs(i):
                q0, _, _ = offsets(i)
                lblk = lp[pl.ds(q0, QBLK), :]
                dblk = dlp[pl.ds(q0, QBLK), :]
                lcol = jnp.concatenate([lblk[:, 0:1], lblk[:, HEAD_DIM:HEAD_DIM + 1]], axis=0)
                dcol = jnp.concatenate([dblk[:, 0:1], dblk[:, HEAD_DIM:HEAD_DIM + 1]], axis=0)
                p = jnp.exp2(sbuf[i % BWD_SLOTS] - lcol)
                pbuf[i % BWD_SLOTS] = p.astype(MXU_DTYPE)
                dsbuf[i % BWD_SLOTS] = (p * (dpbuf[i % BWD_SLOTS] - dcol)).astype(MXU_DTYPE)

            def grads(i):
                q0, k0, _ = offsets(i)
                qs = _stack_heads(qp[pl.ds(q0, QBLK), :], lane_lo)
                dos = _stack_heads(dop[pl.ds(q0, QBLK), :], lane_lo)
                ds = dsbuf[i % BWD_SLOTS]
                dvacc[i % 2, pl.ds(k0, 2 * QBLK), :] = _dot_tn(pbuf[i % BWD_SLOTS], dos)
                dkacc[i % 2, pl.ds(k0, 2 * QBLK), :] = _dot_tn(ds, qs)
                dqp[pl.ds(q0, QBLK), :] = _merge_heads(_dot(ds, kp[pl.ds(k0, 2 * QBLK), :]), lane_lo)

            _pipeline(NB, [scores, probs, grads], PIPE_UNROLL)
            _scatter_rows(dqp, dqn, S, d, pad=False, accumulate=True)
            _scatter_parity(dkacc, dkn, S, d)
            _scatter_parity(dvacc, dv_ref, S, d)

        gq_sum = jnp.zeros((8, LANES), F32)
        gk_sum = jnp.zeros((8, LANES), F32)
        for c0 in range(0, S, ROWCH):
            for src_ref, dn, g_ref, dst_ref, scale, is_q in ((q_ref, dqn, gq_ref, dq_ref, QSCALE, True),
                                                             (k_ref, dkn, gk_ref, dk_ref, LN2, False)):
                x = src_ref[c0:c0 + ROWCH, :]
                dh = dn[c0:c0 + ROWCH, :]
                rr = lax.rsqrt(_head_mean(x * x, mm) + EPS)
                e = dh * (g_ref[...] * scale)
                dst_ref[c0:c0 + ROWCH, :] = rr * e - x * (rr * rr * rr) * _head_mean(e * x, mm)
                gpart = dh * (x * rr * scale)
                acc8 = gpart[0:8, :]
                for q8 in range(1, ROWCH // 8):
                    acc8 = acc8 + gpart[8 * q8:8 * q8 + 8, :]
                if is_q:
                    gq_sum = gq_sum + acc8
                else:
                    gk_sum = gk_sum + acc8
        gacc_ref[0:1, :] += jnp.sum(gq_sum, axis=0, keepdims=True)
        gacc_ref[1:2, :] += jnp.sum(gk_sum, axis=0, keepdims=True)

        if nh:
            @pl.when(step == n_steps - 1)
            def _():
                plan.finish(hin, hout, sems)

    spec_in = lambda off: pl.BlockSpec((None, S, LANES), lambda b, j: (off * NS + j, b, 0))
    tok = pl.BlockSpec((S, LANES), lambda b, j: (b, j))
    vec = pl.BlockSpec((1, LANES), lambda b, j: (0, 0))
    slab_out = pl.BlockSpec((None, S, LANES), lambda b, j: (j, b, 0))
    f32buf = lambda rows: pltpu.VMEM((rows, LANES), F32)
    bfbuf = lambda rows: pltpu.VMEM((rows, LANES), MXU_DTYPE)
    anyspec = pl.BlockSpec(memory_space=pl.ANY)
    return pl.pallas_call(
        body, grid=(Bl, NS), name="attn_bwd",
        in_specs=[spec_in(0), spec_in(1), spec_in(0), spec_in(1), spec_in(2), tok, tok, tok, vec, vec,
                  pl.BlockSpec((None, 8, LANES), lambda b, j: (j, 0, 0))] + [anyspec] * nh,
        out_specs=[slab_out, slab_out, slab_out, pl.BlockSpec((8, LANES), lambda b, j: (0, 0))] + [anyspec] * nh,
        out_shape=[jax.ShapeDtypeStruct((NS, T, LANES), F32)] * 3 + [jax.ShapeDtypeStruct((8, LANES), F32)]
                  + [jax.ShapeDtypeStruct((3,) + h.shape[1:], h.dtype) for h in hosted],
        scratch_shapes=[f32buf(S),
                        bfbuf(S), bfbuf(PADR), bfbuf(PADR), bfbuf(S),
                        f32buf(S), f32buf(S), f32buf(S),
                        pltpu.VMEM((2, PADR, LANES), F32), pltpu.VMEM((2, PADR, LANES), F32),
                        f32buf(S), f32buf(S),
                        pltpu.VMEM((N_EDGE * len(DILATIONS), 2 * QBLK, 2 * QBLK), F32),
                        pltpu.VMEM((BWD_SLOTS, 2 * QBLK, 2 * QBLK), F32),
                        pltpu.VMEM((BWD_SLOTS, 2 * QBLK, 2 * QBLK), F32),
                        pltpu.VMEM((BWD_SLOTS, 2 * QBLK, 2 * QBLK), MXU_DTYPE),
                        pltpu.VMEM((BWD_SLOTS, 2 * QBLK, 2 * QBLK), MXU_DTYPE)]
                       + (plan.scratch() if nh else []),
        compiler_params=_cp(("arbitrary", "arbitrary")),
    )(qkh, qkh, qkv, qkv, qkv, o, lse, do, gq2, gk2, slopes, *hosted)


def _layer_norm_parts(cv, g_ln, b_ln):
    mu = jnp.mean(cv, axis=-1, keepdims=True)
    cen = cv - mu
    rs = lax.rsqrt(jnp.mean(cen * cen, axis=-1, keepdims=True) + EPS)
    z = cen * rs
    return z, rs, z * g_ln + b_ln


def _ffn_fwd(x2, cv, ya, tgt, mod, g_ln, b_ln, g_ffn, w_out, w_gate, w_up, w_down, *, S, tm):
    T, D = x2.shape
    DC = cv.shape[1]
    P, Kb, _ = w_out.shape
    Fb = w_down.shape[1]
    tps = S // tm

    def body(x_ref, cv_ref, ya_ref, t_ref, mod_ref, gln_ref, bln_ref, gf_ref, wo_hbm, wg_hbm, wu_hbm, wd_hbm,
             x1_ref, ycat_ref, mix_ref, h2_ref, g_ref, u_ref, a_ref, f_ref, dy_ref, loss_ref,
             wo, wg, wu, wd, sems):
        i = pl.program_id(0)
        _load_resident(i, [(wo_hbm, wo), (wg_hbm, wg), (wu_hbm, wu), (wd_hbm, wd)], sems)

        @pl.when(i == 0)
        def _():
            loss_ref[...] = jnp.zeros_like(loss_ref)

        _, _, ln = _layer_norm_parts(cv_ref[...], gln_ref[...], bln_ref[...])
        yc = ln * _sigmoid(ln)
        ycat = jnp.concatenate([yc, ya_ref[...]], axis=1).astype(MXU_DTYPE)
        ycat_ref[...] = ycat
        mix = _dot(ycat[:, 0:Kb], wo[0])
        for p in range(1, P):
            mix = mix + _dot(ycat[:, Kb * p:Kb * (p + 1)], wo[p])
        mix_ref[...] = mix.astype(ACT_DTYPE)
        x1 = x_ref[...] + mod_ref[:, 2 * D:3 * D] * mix
        x1_ref[...] = x1
        r2 = lax.rsqrt(jnp.mean(x1 * x1, axis=-1, keepdims=True) + EPS)
        h2 = (x1 * r2 * gf_ref[...]) * (1.0 + mod_ref[:, 4 * D:5 * D]) + mod_ref[:, 3 * D:4 * D]
        h2b = h2.astype(MXU_DTYPE)
        h2_ref[...] = h2b
        f = jnp.zeros((tm, D), F32)
        for p in range(P):
            g = _dot_nt(h2b, wg[p])
            u = _dot_nt(h2b, wu[p])
            a = (g * _sigmoid(g) * u).astype(MXU_DTYPE)
            g_ref[p] = g.astype(ACT_DTYPE)
            u_ref[p] = u.astype(ACT_DTYPE)
            a_ref[p] = a
            f = f + _dot(a, wd[p])
        f_ref[...] = f.astype(ACT_DTYPE)
        err = x1 + mod_ref[:, 5 * D:6 * D] * f - t_ref[...]
        dy_ref[...] = err * (1.0 / D)
        tot = jnp.sum(_colsum(err * err), axis=1, keepdims=True)
        loss_ref[...] += tot * (0.5 / D)

    row = lambda w: pl.BlockSpec((tm, w), lambda i: (i, 0))
    vec = lambda w: pl.BlockSpec((1, w), lambda i: (0, 0))
    blk = pl.BlockSpec((P, tm, Fb), lambda i: (0, i, 0))
    anyspec = pl.BlockSpec(memory_space=pl.ANY)
    return pl.pallas_call(
        body, grid=(T // tm,), name="ffn_fwd",
        in_specs=[row(D), row(DC), row(D - DC), row(D),
                  pl.BlockSpec((None, 1, N_MOD * D), lambda i: (i // tps, 0, 0)),
                  vec(DC), vec(DC), vec(D), anyspec, anyspec, anyspec, anyspec],
        out_specs=[row(D), row(D), row(D), row(D), blk, blk, blk, row(D), row(D),
                   pl.BlockSpec((8, LANES), lambda i: (0, 0))],
        out_shape=[jax.ShapeDtypeStruct((T, D), F32), jax.ShapeDtypeStruct((T, D), MXU_DTYPE),
                   jax.ShapeDtypeStruct((T, D), ACT_DTYPE), jax.ShapeDtypeStruct((T, D), MXU_DTYPE),
                   jax.ShapeDtypeStruct((P, T, Fb), ACT_DTYPE), jax.ShapeDtypeStruct((P, T, Fb), ACT_DTYPE),
                   jax.ShapeDtypeStruct((P, T, Fb), MXU_DTYPE), jax.ShapeDtypeStruct((T, D), ACT_DTYPE),
                   jax.ShapeDtypeStruct((T, D), F32), jax.ShapeDtypeStruct((8, LANES), F32)],
        scratch_shapes=[pltpu.VMEM(w_out.shape, w_out.dtype), pltpu.VMEM(w_gate.shape, w_gate.dtype),
                        pltpu.VMEM(w_up.shape, w_up.dtype), pltpu.VMEM(w_down.shape, w_down.dtype),
                        pltpu.SemaphoreType.DMA((4,))],
        compiler_params=_cp(("arbitrary",)),
    )(x2, cv, ya, tgt, mod, g_ln, b_ln, g_ffn, w_out, w_gate, w_up, w_down)


def _ffn_bwd(dy, x1, gs, us, fo, mixb, cv, mod, g_ln, b_ln, g_ffn, w_out, w_gate, w_up, w_down, *, S, tm):
    T, D = dy.shape
    DC = cv.shape[1]
    P, Kb, _ = w_out.shape
    Fb = w_down.shape[1]
    tps = S // tm
    Bl = T // S

    def body(dy_ref, x1_ref, g_ref, u_ref, f_ref, mix_ref, cv_ref, mod_ref, gln_ref, bln_ref, gf_ref,
             wo_hbm, wg_hbm, wu_hbm, wd_hbm,
             dg_ref, du_ref, df_ref, dx1_ref, dmix_ref, dya_ref, dcv_ref, macc_ref, gacc_ref, lacc_ref,
             wo, wg, wu, wd, sems):
        i = pl.program_id(0)
        _load_resident(i, [(wo_hbm, wo), (wg_hbm, wg), (wu_hbm, wu), (wd_hbm, wd)], sems)

        @pl.when(i == 0)
        def _():
            gacc_ref[...] = jnp.zeros_like(gacc_ref)
            lacc_ref[...] = jnp.zeros_like(lacc_ref)

        @pl.when(i % tps == 0)
        def _():
            macc_ref[...] = jnp.zeros_like(macc_ref)

        dy_t = dy_ref[...]
        x1 = x1_ref[...]
        gate_f = mod_ref[:, 5 * D:6 * D]
        macc_ref[2:3, :] += _colsum(dy_t * f_ref[...].astype(F32))
        dfb = (dy_t * gate_f).astype(MXU_DTYPE)
        df_ref[...] = dfb
        dh2 = jnp.zeros((tm, D), F32)
        for p in range(P):
            da = _dot_nt(dfb, wd[p])
            g = g_ref[p].astype(F32)
            u = u_ref[p].astype(F32)
            sg = _sigmoid(g)
            dgp = (da * u * (sg * (1.0 + g * (1.0 - sg)))).astype(MXU_DTYPE)
            dup = (da * (g * sg)).astype(MXU_DTYPE)
            dg_ref[p] = dgp
            du_ref[p] = dup
            dh2 = dh2 + _dot(dgp, wg[p]) + _dot(dup, wu[p])
        r2 = lax.rsqrt(jnp.mean(x1 * x1, axis=-1, keepdims=True) + EPS)
        xr = x1 * r2
        n2 = xr * gf_ref[...]
        macc_ref[0:1, :] += _colsum(dh2)
        macc_ref[1:2, :] += _colsum(dh2 * n2)
        dn2 = dh2 * (1.0 + mod_ref[:, 4 * D:5 * D])
        gacc_ref[0:1, :] += _colsum(dn2 * xr)
        e = dn2 * gf_ref[...]
        dx1 = dy_t + r2 * e - xr * (r2 * jnp.mean(e * xr, axis=-1, keepdims=True))
        dx1_ref[...] = dx1
        macc_ref[3:4, :] += _colsum(dx1 * mix_ref[...].astype(F32))
        dmixb = (dx1 * mod_ref[:, 2 * D:3 * D]).astype(MXU_DTYPE)
        dmix_ref[...] = dmixb
        parts = [_dot_nt(dmixb, wo[p]) for p in range(P)]
        dycat = jnp.concatenate(parts, axis=1) if P > 1 else parts[0]
        dya_ref[...] = dycat[:, DC:]
        dyc = dycat[:, :DC]
        z, rs, ln = _layer_norm_parts(cv_ref[...], gln_ref[...], bln_ref[...])
        sg = _sigmoid(ln)
        dln = dyc * (sg * (1.0 + ln * (1.0 - sg)))
        lacc_ref[0:1, :] += _colsum(dln * z)
        lacc_ref[1:2, :] += _colsum(dln)
        dz = dln * gln_ref[...]
        dcv_ref[...] = rs * (dz - jnp.mean(dz, axis=-1, keepdims=True) - z * jnp.mean(dz * z, axis=-1, keepdims=True))

    row = lambda w: pl.BlockSpec((tm, w), lambda i: (i, 0))
    vec = lambda w: pl.BlockSpec((1, w), lambda i: (0, 0))
    blk = pl.BlockSpec((P, tm, Fb), lambda i: (0, i, 0))
    anyspec = pl.BlockSpec(memory_space=pl.ANY)
    return pl.pallas_call(
        body, grid=(T // tm,), name="ffn_bwd",
        in_specs=[row(D), row(D), blk, blk, row(D), row(D), row(DC),
                  pl.BlockSpec((None, 1, N_MOD * D), lambda i: (i // tps, 0, 0)),
                  vec(DC), vec(DC), vec(D), anyspec, anyspec, anyspec, anyspec],
        out_specs=[blk, blk, row(D), row(D), row(D), row(D - DC), row(DC),
                   pl.BlockSpec((None, 8, D), lambda i: (i // tps, 0, 0)),
                   pl.BlockSpec((8, D), lambda i: (0, 0)), pl.BlockSpec((8, DC), lambda i: (0, 0))],
        out_shape=[jax.ShapeDtypeStruct((P, T, Fb), MXU_DTYPE), jax.ShapeDtypeStruct((P, T, Fb), MXU_DTYPE),
                   jax.ShapeDtypeStruct((T, D), MXU_DTYPE), jax.ShapeDtypeStruct((T, D), F32),
                   jax.ShapeDtypeStruct((T, D), MXU_DTYPE), jax.ShapeDtypeStruct((T, D - DC), F32),
                   jax.ShapeDtypeStruct((T, DC), F32), jax.ShapeDtypeStruct((Bl, 8, D), F32),
                   jax.ShapeDtypeStruct((8, D), F32), jax.ShapeDtypeStruct((8, DC), F32)],
        scratch_shapes=[pltpu.VMEM(w_out.shape, w_out.dtype), pltpu.VMEM(w_gate.shape, w_gate.dtype),
                        pltpu.VMEM(w_up.shape, w_up.dtype), pltpu.VMEM(w_down.shape, w_down.dtype),
                        pltpu.SemaphoreType.DMA((4,))],
        compiler_params=_cp(("arbitrary",)),
    )(dy, x1, gs, us, fo, mixb, cv, mod, g_ln, b_ln, g_ffn, w_out, w_gate, w_up, w_down)


def _in_bwd(da, dg, dq, dk, dv, x2, dx1, mod, g_mix, w_in, *, S, tm):
    T, D = x2.shape
    P, _, Nb = w_in.shape
    DC = da.shape[1]
    NS = dq.shape[0]
    n_in = P * Nb
    tps = S // tm
    Bl = T // S

    def body(da_ref, dg_ref, dq_ref, dk_ref, dv_ref, x_ref, dx1_ref, mod_ref, g_ref, w_ref,
             dx_ref, dproj_ref, macc_ref, gacc_ref):
        i = pl.program_id(0)

        @pl.when(i == 0)
        def _():
            gacc_ref[...] = jnp.zeros_like(gacc_ref)

        @pl.when(i % tps == 0)
        def _():
            macc_ref[...] = jnp.zeros_like(macc_ref)

        pieces = [da_ref[...], dg_ref[...]] + [r[j] for r in (dq_ref, dk_ref, dv_ref) for j in range(NS)]
        dproj = jnp.concatenate(pieces, axis=1).astype(MXU_DTYPE)
        dproj_ref[...] = dproj
        dh = _dot_nt(dproj[:, 0:Nb], w_ref[0])
        for p in range(1, P):
            dh = dh + _dot_nt(dproj[:, Nb * p:Nb * (p + 1)], w_ref[p])
        x = x_ref[...]
        r = lax.rsqrt(jnp.mean(x * x, axis=-1, keepdims=True) + EPS)
        xr = x * r
        macc_ref[0:1, :] += _colsum(dh)
        macc_ref[1:2, :] += _colsum(dh * (xr * g_ref[...]))
        dn = dh * (1.0 + mod_ref[:, D:2 * D])
        gacc_ref[0:1, :] += _colsum(dn * xr)
        e = dn * g_ref[...]
        dx_ref[...] = dx1_ref[...] + r * e - xr * (r * jnp.mean(e * xr, axis=-1, keepdims=True))

    row = lambda w: pl.BlockSpec((tm, w), lambda i: (i, 0))
    slab = pl.BlockSpec((NS, tm, LANES), lambda i: (0, i, 0))
    return pl.pallas_call(
        body, grid=(T // tm,), name="in_bwd",
        in_specs=[row(DC), row(DC), slab, slab, slab, row(D), row(D),
                  pl.BlockSpec((None, 1, N_MOD * D), lambda i: (i // tps, 0, 0)),
                  pl.BlockSpec((1, D), lambda i: (0, 0)),
                  pl.BlockSpec((P, D, Nb), lambda i: (0, 0, 0))],
        out_specs=[row(D), row(n_in), pl.BlockSpec((None, 8, D), lambda i: (i // tps, 0, 0)),
                   pl.BlockSpec((8, D), lambda i: (0, 0))],
        out_shape=[jax.ShapeDtypeStruct((T, D), F32), jax.ShapeDtypeStruct((T, n_in), MXU_DTYPE),
                   jax.ShapeDtypeStruct((Bl, 8, D), F32), jax.ShapeDtypeStruct((8, D), F32)],
        compiler_params=_cp(("arbitrary",)),
    )(da, dg, dq, dk, dv, x2, dx1, mod, g_mix, w_in)


def _wgrad(a, b, *, P, name, tk, split=None, host=None):
    a_blk, b_blk = a.ndim == 3, b.ndim == 3
    plan, h_in, h_out = host if host is not None else (None, (), ())
    ni, no = len(h_in), len(h_out)
    T = a.shape[-2]
    if a_blk:
        R, C = a.shape[2], b.shape[1]
        a_of = lambda av, p: av[p]
        b_of = lambda bv, p: bv[...]
    elif b_blk:
        R, C = a.shape[1], b.shape[2]
        a_of = lambda av, p: av[...]
        b_of = lambda bv, p: bv[p]
    elif split == "a":
        R, C = a.shape[1] // P, b.shape[1]
        a_of = lambda av, p: av[:, R * p:R * (p + 1)]
        b_of = lambda bv, p: bv[...]
    else:
        R, C = a.shape[1], b.shape[1] // P
        a_of = lambda av, p: av[...]
        b_of = lambda bv, p: bv[:, C * p:C * (p + 1)]

    n_steps = T // tk

    def body(a_ref, b_ref, *rest):
        hin, o_ref, hout, sems = rest[:ni], rest[ni], rest[ni + 1:ni + 1 + no], rest[ni + 1 + no:]
        step = pl.program_id(0)

        @pl.when(step == 0)
        def _():
            o_ref[...] = jnp.zeros_like(o_ref)
            if plan is not None:
                plan.start(hin, hout, sems)

        if plan is not None:
            @pl.when(step == n_steps // 2)
            def _():
                plan.forward(hin, hout, sems)

        for p in range(P):
            o_ref[p] += _dot_tn(a_of(a_ref, p), b_of(b_ref, p))

        if plan is not None:
            @pl.when(step == n_steps - 1)
            def _():
                plan.finish(hin, hout, sems)

    def spec(v):
        if v.ndim == 3:
            return pl.BlockSpec((P, tk, v.shape[2]), lambda k: (0, k, 0))
        return pl.BlockSpec((tk, v.shape[1]), lambda k: (k, 0))

    anyspec = pl.BlockSpec(memory_space=pl.ANY)
    res = pl.pallas_call(
        body, grid=(n_steps,), name=name,
        in_specs=[spec(a), spec(b)] + [anyspec] * ni,
        out_specs=[pl.BlockSpec((P, R, C), lambda k: (0, 0, 0))] + [anyspec] * no,
        out_shape=[jax.ShapeDtypeStruct((P, R, C), F32)] + list(h_out),
        scratch_shapes=plan.scratch() if plan is not None else [],
        compiler_params=_cp(("arbitrary",)),
    )(a, b, *h_in)
    return res if plan is not None else res[0]


TM_IN = 512
TM_FFN = 256
TK_WGRAD = 512


def _alibi_slabs(n_slab):
    heads = 2 * n_slab
    slopes = 2.0 ** (-8.0 * np.arange(1, heads + 1) / heads)
    return jnp.asarray(np.broadcast_to(np.repeat(slopes.reshape(n_slab, 1, 2), HEAD_DIM, axis=2), (n_slab, 8, LANES)),
                       dtype=F32)


def _local_step(x, tgt, mod, g_mix, wdw, g_ln, b_ln, g_q, g_k, g_ffn, w_in, w_out, w_gate, w_up, w_down,
                pc_idx=None):
    Bl, S, D = x.shape
    T = Bl * S
    DC = g_ln.shape[1]
    P = w_in.shape[0]
    n_slab = (D - DC) // LANES
    x2 = x.reshape(T, D)
    t2 = tgt.reshape(T, D)
    mod3 = mod.reshape(Bl, 1, N_MOD * D)
    gq2 = jnp.tile(g_q, (1, LANES // HEAD_DIM))
    gk2 = jnp.tile(g_k, (1, LANES // HEAD_DIM))
    slopes = _alibi_slabs(n_slab)

    ag, qkv, qkh, h1 = _fwd_in(x2, mod3, g_mix, gq2, gk2, w_in, S=S, tm=TM_IN, n_ag=2 * DC)
    cv = _conv_fwd(ag, wdw, Bl=Bl, S=S, DC=DC)
    if pc_idx is not None:
        ya, lse, w_out, w_gate, w_up, w_down = _attn_fwd(qkh, qkv, slopes, Bl=Bl, S=S,
                                                         hosted=(w_out, w_gate, w_up, w_down))
    else:
        ya, lse = _attn_fwd(qkh, qkv, slopes, Bl=Bl, S=S)
    x1, ycat, mixb, h2, gs, us, acts, fo, dy, lossb = _ffn_fwd(
        x2, cv, ya, t2, mod3, g_ln, b_ln, g_ffn, w_out, w_gate, w_up, w_down, S=S, tm=TM_FFN)
    dgs, dus, dfb, dx1, dmixb, dya, dcv, macc_f, gacc_f, lacc = _ffn_bwd(
        dy, x1, gs, us, fo, mixb, cv, mod3, g_ln, b_ln, g_ffn, w_out, w_gate, w_up, w_down, S=S, tm=TM_FFN)
    wg = functools.partial(_wgrad, P=P, tk=TK_WGRAD)
    out = {}
    if pc_idx is None:
        grads = dict(w_down=wg(acts, dfb, name="wgrad_down"), w_gate=wg(dgs, h2, name="wgrad_gate"),
                     w_up=wg(dus, h2, name="wgrad_up"), w_out=wg(ycat, dmixb, name="wgrad_out", split="a"))
        dq, dk, dv, gqk = _attn_bwd(qkh, qkv, ya, lse, dya, gq2, gk2, slopes, Bl=Bl, S=S)
    else:
        g_down = wg(acts, dfb, name="wgrad_down")
        g_gate, r_down = wg(dgs, h2, name="wgrad_gate", host=_sibling_host([g_down]))
        g_up, r_gate = wg(dus, h2, name="wgrad_up", host=_sibling_host([g_gate]))
        g_out, r_up = wg(ycat, dmixb, name="wgrad_out", split="a", host=_sibling_host([g_up]))
        (r_out,) = _rs_sibling([g_out], "rs_sibling_out")
        grads = dict(w_down=g_down, w_gate=g_gate, w_up=g_up, w_out=g_out)
        sums = [_pair_add(grads[nm], r, pc_idx, "pair_add_" + nm)
                for nm, r in zip(EARLY_WEIGHTS, (r_down, r_gate, r_up, r_out))]
        res = _attn_bwd(qkh, qkv, ya, lse, dya, gq2, gk2, slopes, Bl=Bl, S=S, hosted=tuple(sb for _, sb in sums))
        dq, dk, dv, gqk = res[:4]
        out["early_sums"] = [s32 for s32, _ in sums]
        out["early_recv"] = list(res[4:])
    da, dg, dwdw = _conv_bwd(ag, dcv, wdw, Bl=Bl, S=S, DC=DC)
    dx, dprojb, macc_m, gacc_m = _in_bwd(da, dg, dq, dk, dv, x2, dx1, mod3, g_mix, w_in, S=S, tm=TM_IN)
    packed = _pack_small(macc_m, macc_f, gacc_m, gacc_f, lacc, gqk, dwdw, lossb)
    if pc_idx is None:
        grads["w_in"] = wg(h1, dprojb, name="wgrad_in", split="b")
    else:
        grads["w_in"], out["gathered_small"] = wg(h1, dprojb, name="wgrad_in", split="b",
                                                  host=_small_gather_host(packed))
    out.update(dx=dx.reshape(Bl, S, D), grads=grads, packed=packed)
    return out


EARLY_WEIGHTS = ("w_down", "w_gate", "w_up", "w_out")


def _small_layout(Bl):
    return 8 * Bl, 8 * Bl + 8, 8 * Bl + 8 + CONV_ROWS


def _pack_small(macc_m, macc_f, gacc_m, gacc_f, lacc, gqk, dwdw, lossb):
    Bl, _, D = macc_m.shape
    DC = lacc.shape[1]
    assert 2 * DC <= D
    SMALL_GAIN_ROW, SMALL_TAP_ROW, SMALL_ROWS = _small_layout(Bl)

    def body(mm_ref, mf_ref, gm_ref, gf_ref, la_ref, qk_ref, dw_ref, loss_ref, o_ref):
        o_ref[...] = jnp.zeros_like(o_ref)
        for b in range(Bl):
            o_ref[8 * b + 0:8 * b + 2, :] = mm_ref[b, 0:2, :]
            o_ref[8 * b + 2:8 * b + 3, :] = mf_ref[b, 3:4, :]
            o_ref[8 * b + 3:8 * b + 6, :] = mf_ref[b, 0:3, :]
        r = SMALL_GAIN_ROW
        o_ref[r:r + 1, :] = gm_ref[0:1, :]
        o_ref[r + 1:r + 2, :] = gf_ref[0:1, :]
        o_ref[r + 2:r + 3, 0:DC] = la_ref[0:1, :]
        o_ref[r + 2:r + 3, DC:2 * DC] = la_ref[1:2, :]
        qk = qk_ref[0:2, 0:HEAD_DIM] + qk_ref[0:2, HEAD_DIM:2 * HEAD_DIM]
        o_ref[r + 3:r + 4, 0:HEAD_DIM] = qk[0:1, :]
        o_ref[r + 3:r + 4, HEAD_DIM:2 * HEAD_DIM] = qk[1:2, :]
        o_ref[r + 4:r + 5, 0:LANES] = loss_ref[0:1, :]
        o_ref[SMALL_TAP_ROW:SMALL_TAP_ROW + CONV_ROWS, 0:DC] = dw_ref[...]

    return pl.pallas_call(body, name="pack_small", out_shape=jax.ShapeDtypeStruct((SMALL_ROWS, D), F32),
                          compiler_params=_cp())(macc_m, macc_f, gacc_m, gacc_f, lacc, gqk, dwdw, lossb)


def _row_tile(rows, cap=512):
    if rows <= cap:
        return rows
    best = rows
    for t in range(8, cap + 1, 8):
        if rows % t == 0:
            best = t
    return best


def _cast_weight(w, pidx, name):
    def body(p_ref, w_ref, o_ref):
        o_ref[...] = w_ref[...].astype(MXU_DTYPE)
    R, C = w.shape
    tr = _row_tile(R)
    return pl.pallas_call(
        body, name=name,
        grid_spec=pltpu.PrefetchScalarGridSpec(
            num_scalar_prefetch=1, grid=(R // tr,),
            in_specs=[pl.BlockSpec((tr, C), lambda i, p: (i, 0))],
            out_specs=pl.BlockSpec((None, tr, C), lambda i, p: (p[0], i, 0))),
        out_shape=jax.ShapeDtypeStruct((4, R, C), MXU_DTYPE),
    )(pidx, w)


def _pair_add(g, recv, pc_idx, name):
    P, R, C = g.shape
    R2 = R // 2

    def body(pc_ref, g_ref, r_ref, o_ref, ob_ref):
        s = g_ref[...] + r_ref[...]
        ob_ref[...] = s.astype(jnp.bfloat16)

        @pl.when(pl.program_id(0) == pc_ref[0])
        def _():
            o_ref[...] = s

    return pl.pallas_call(
        body, name=name,
        grid_spec=pltpu.PrefetchScalarGridSpec(
            num_scalar_prefetch=1, grid=(P,),
            in_specs=[pl.BlockSpec((None, R2, C), lambda p, pc: (p, pc[1], 0)),
                      pl.BlockSpec((None, R2, C), lambda p, pc: (p, 0, 0))],
            out_specs=[pl.BlockSpec((R2, C), lambda p, pc: (0, 0)),
                       pl.BlockSpec((None, R2, C), lambda p, pc: (p, 0, 0))]),
        out_shape=[jax.ShapeDtypeStruct((R2, C), F32), jax.ShapeDtypeStruct((P, R2, C), jnp.bfloat16)],
    )(pc_idx, g, recv)


def _final_add(own, recv, pc_idx, name):
    R2, C = own.shape

    def body(pc_ref, s_ref, r_ref, o_ref):
        acc = s_ref[...]
        for k in range(3):
            acc = acc + r_ref[k].astype(F32)
        o_ref[...] = acc

    return pl.pallas_call(
        body, name=name,
        grid_spec=pltpu.PrefetchScalarGridSpec(
            num_scalar_prefetch=1, grid=(1,),
            in_specs=[pl.BlockSpec((R2, C), lambda i, pc: (0, 0)),
                      pl.BlockSpec((3, R2, C), lambda i, pc: (0, 0, 0))],
            out_specs=pl.BlockSpec((R2, C), lambda i, pc: (pc[1], 0))),
        out_shape=jax.ShapeDtypeStruct((2 * R2, C), F32),
    )(pc_idx, own, recv)


def _adamw_update(w_ref, g_ref, m_ref, v_ref, d_ref, nm_ref, nv_ref):
    c1 = 1.0 - ADAM_B1 ** ADAM_STEP
    c2 = 1.0 - ADAM_B2 ** ADAM_STEP
    gg = g_ref[...]
    nm = ADAM_B1 * m_ref[...] + (1.0 - ADAM_B1) * gg
    nv = ADAM_B2 * v_ref[...] + (1.0 - ADAM_B2) * (gg * gg)
    nm_ref[...] = nm
    nv_ref[...] = nv
    d_ref[...] = -ADAM_LR * ((nm / c1) / (jnp.sqrt(nv / c2) + ADAM_EPS) + ADAM_WD * w_ref[...])


def _adamw(w, g, m, v, name):
    R, C = w.shape
    tr = _row_tile(R, 256)
    spec = pl.BlockSpec((tr, C), lambda i: (i, 0))
    return pl.pallas_call(
        functools.partial(_adamw_update), grid=(R // tr,), name=name,
        in_specs=[spec] * 4, out_specs=[spec] * 3,
        out_shape=[jax.ShapeDtypeStruct((R, C), F32)] * 3,
    )(w, g, m, v)


def _startup(first, w_ada, b_cols, w_in_buf, *, Bl):
    rows, D = first.shape
    NA = w_ada.shape[1]
    n_dev = 8
    g_w = _WeightGather([w_in_buf.shape])
    g_c = _SmallGather(rows)
    g_m = _SmallGather(n_dev * Bl)

    def body(first_ref, wada_ref, b_ref, win_in, g0_ref, call_ref, gm_ref, win_out, modp,
             ws0, ws1, cs0, cs1, cs2, ms0, ms1, ms2):
        g_w.start([win_out], (ws0, ws1))
        for phase in (g_c.start, g_c.forward, g_c.finish):
            phase([first_ref], [g0_ref], (cs0, cs1, cs2))
        for d in range(n_dev):
            call_ref[Bl * d:Bl * (d + 1), :] = g0_ref[rows * d:rows * d + Bl, :]
        c = call_ref[...]
        modp[...] = jnp.dot(c * _sigmoid(c), wada_ref[...], preferred_element_type=F32, precision=HIGHEST) + b_ref[...]
        for phase in (g_m.start, g_m.forward, g_m.finish):
            phase([modp], [gm_ref], (ms0, ms1, ms2))
        g_w.forward([win_out], (ws0, ws1))
        g_w.finish([win_out], (ws0, ws1))

    vmem = pl.BlockSpec(memory_space=pltpu.VMEM)
    anyspec = pl.BlockSpec(memory_space=pl.ANY)
    return pl.pallas_call(
        body, name="startup",
        in_specs=[vmem, vmem, vmem, anyspec], out_specs=[vmem, vmem, vmem, anyspec],
        out_shape=[jax.ShapeDtypeStruct((n_dev * rows, D), F32), jax.ShapeDtypeStruct((n_dev * Bl, D), F32),
                   jax.ShapeDtypeStruct((n_dev * n_dev * Bl, NA), F32),
                   jax.ShapeDtypeStruct(w_in_buf.shape, w_in_buf.dtype)],
        input_output_aliases={3: 3},
        scratch_shapes=[pltpu.VMEM((n_dev * Bl, NA), F32)] + g_w.scratch() + g_c.scratch() + g_m.scratch(),
        compiler_params=_cp(),
    )(first, w_ada, b_cols, w_in_buf)


def _ada_bwd(c_all, dmod_cols):
    def body(c_ref, d_ref, o_ref):
        c = c_ref[...]
        o_ref[...] = lax.dot_general(c * _sigmoid(c), d_ref[...], (((0,), (0,)), ((), ())),
                                     preferred_element_type=F32, precision=HIGHEST)
    return pl.pallas_call(
        body, name="ada_bwd", out_shape=jax.ShapeDtypeStruct((c_all.shape[1], dmod_cols.shape[1]), F32),
        compiler_params=_cp(),
    )(c_all, dmod_cols)


def _small_reduce(gathered, n_dev, Bl):
    mod_rows, _, rows = _small_layout(Bl)
    width = gathered.shape[1]

    def body(g_ref, red_ref, bada_ref):
        acc = g_ref[0:rows, :]
        for d in range(1, n_dev):
            acc = acc + g_ref[d * rows:(d + 1) * rows, :]
        red_ref[...] = acc[mod_rows:, :]
        b = acc[0:8, :]
        for q in range(1, Bl):
            b = b + acc[8 * q:8 * q + 8, :]
        bada_ref[...] = b
    return pl.pallas_call(
        body, name="small_reduce",
        out_shape=[jax.ShapeDtypeStruct((rows - mod_rows, width), F32), jax.ShapeDtypeStruct((8, width), F32)],
        compiler_params=_cp(),
    )(gathered)


def _mesh_pos():
    return lax.axis_index("x"), lax.axis_index("y"), lax.axis_index("c")


def _other_chips(x, y):
    return [(1 - x, y), (x, 1 - y), (1 - x, 1 - y)]


class _WeightGather:
    def __init__(self, shapes):
        self.shapes = shapes
        self.n = len(shapes)

    def scratch(self):
        return [pltpu.SemaphoreType.DMA((6 * self.n,)), pltpu.SemaphoreType.DMA((6 * self.n,))]

    def _copy(self, outs, sems, w, k, slot, h, to):
        r2 = self.shapes[w][1] // 2
        blk = outs[w].at[slot, pl.ds(h * r2, r2), :]
        return pltpu.make_async_remote_copy(
            src_ref=blk, dst_ref=blk, send_sem=sems[0].at[6 * w + k], recv_sem=sems[1].at[6 * w + k],
            device_id=to, device_id_type=MESH_DEV)

    def start(self, outs, sems):
        x, y, c = _mesh_pos()
        for w in range(self.n):
            for k, chip in enumerate(_other_chips(x, y)):
                self._copy(outs, sems, w, k, 2 * x + y, c, (*chip, c)).start()

    def forward(self, outs, sems):
        x, y, c = _mesh_pos()
        for w in range(self.n):
            for k, chip in enumerate(_other_chips(x, y)):
                slot = 2 * chip[0] + chip[1]
                self._copy(outs, sems, w, k, slot, c, (x, y, 1 - c)).wait_recv()
                self._copy(outs, sems, w, 3 + k, slot, c, (x, y, 1 - c)).start()

    def finish(self, outs, sems):
        x, y, c = _mesh_pos()
        for w in range(self.n):
            for k, chip in enumerate(_other_chips(x, y)):
                slot = 2 * chip[0] + chip[1]
                self._copy(outs, sems, w, 3 + k, slot, 1 - c, (x, y, 1 - c)).wait_recv()
                self._copy(outs, sems, w, k, 2 * x + y, c, (*chip, c)).wait_send()
                self._copy(outs, sems, w, 3 + k, slot, c, (x, y, 1 - c)).wait_send()


class _SiblingExchange:
    def __init__(self, shapes):
        self.shapes = shapes

    def scratch(self):
        n = sum(s[0] for s in self.shapes)
        return [pltpu.SemaphoreType.DMA((n,)), pltpu.SemaphoreType.DMA((n,))]

    def out_shapes(self, dtype):
        return [jax.ShapeDtypeStruct((s[0], s[1] // 2, s[2]), dtype) for s in self.shapes]

    def _copies(self, ins, outs, sems):
        x, y, c = _mesh_pos()
        cps, k = [], 0
        for w, (P, R, _) in enumerate(self.shapes):
            r2 = R // 2
            for p in range(P):
                cps.append(pltpu.make_async_remote_copy(
                    src_ref=ins[w].at[p, pl.ds((1 - c) * r2, r2), :], dst_ref=outs[w].at[p],
                    send_sem=sems[0].at[k], recv_sem=sems[1].at[k],
                    device_id=(x, y, 1 - c), device_id_type=MESH_DEV))
                k += 1
        return cps

    def start(self, ins, outs, sems):
        for cp in self._copies(ins, outs, sems):
            cp.start()

    def forward(self, ins, outs, sems):
        pass

    def finish(self, ins, outs, sems):
        for cp in self._copies(ins, outs, sems):
            cp.wait()


def _sibling_host(grads):
    plan = _SiblingExchange([g.shape for g in grads])
    return plan, tuple(grads), tuple(plan.out_shapes(grads[0].dtype))


def _rs_sibling(grads, name):
    n = len(grads)
    plan, _, out_shapes = _sibling_host(grads)

    def body(*refs):
        ins, outs, sems = refs[:n], refs[n:2 * n], refs[2 * n:]
        plan.start(ins, outs, sems)
        plan.finish(ins, outs, sems)

    anyspec = pl.BlockSpec(memory_space=pl.ANY)
    return pl.pallas_call(
        body, name=name, out_shape=list(out_shapes),
        in_specs=[anyspec] * n, out_specs=[anyspec] * n, scratch_shapes=plan.scratch(),
    )(*grads)


class _SmallGather:
    def __init__(self, m_per):
        self.m = m_per

    def scratch(self):
        return [pltpu.SemaphoreType.DMA((7,)), pltpu.SemaphoreType.DMA((7,)), pltpu.SemaphoreType.DMA]

    def _rows(self, out, pos):
        px, py, pc = pos
        return out.at[pl.ds((4 * px + 2 * py + pc) * self.m, self.m), :]

    def _copy(self, out, sems, k, block, to, src=None):
        dst = self._rows(out, block)
        return pltpu.make_async_remote_copy(
            src_ref=dst if src is None else src, dst_ref=dst, send_sem=sems[0].at[k], recv_sem=sems[1].at[k],
            device_id=to, device_id_type=MESH_DEV)

    def start(self, ins, outs, sems):
        x, y, c = _mesh_pos()
        me = (x, y, c)
        pltpu.make_async_copy(ins[0], self._rows(outs[0], me), sems[2]).start()
        self._copy(outs[0], sems, 0, me, (x, y, 1 - c), src=ins[0]).start()
        for j, chip in enumerate(_other_chips(x, y)):
            self._copy(outs[0], sems, 1 + j, me, (*chip, c), src=ins[0]).start()

    def forward(self, ins, outs, sems):
        x, y, c = _mesh_pos()
        for j, chip in enumerate(_other_chips(x, y)):
            self._copy(outs[0], sems, 1 + j, (*chip, c), (x, y, c)).wait_recv()
            self._copy(outs[0], sems, 4 + j, (*chip, c), (x, y, 1 - c)).start()

    def finish(self, ins, outs, sems):
        x, y, c = _mesh_pos()
        me = (x, y, c)
        self._copy(outs[0], sems, 0, (x, y, 1 - c), me).wait_recv()
        for j, chip in enumerate(_other_chips(x, y)):
            self._copy(outs[0], sems, 4 + j, (*chip, 1 - c), me).wait_recv()
        self._copy(outs[0], sems, 0, me, (x, y, 1 - c), src=ins[0]).wait_send()
        for j, chip in enumerate(_other_chips(x, y)):
            self._copy(outs[0], sems, 1 + j, me, (*chip, c), src=ins[0]).wait_send()
            self._copy(outs[0], sems, 4 + j, (*chip, c), (x, y, 1 - c)).wait_send()
        pltpu.make_async_copy(ins[0], self._rows(outs[0], me), sems[2]).wait()


def _small_gather_host(packed):
    m, n = packed.shape
    return _SmallGather(m), (packed,), (jax.ShapeDtypeStruct((8 * m, n), packed.dtype),)


class _ChipExchange:
    def __init__(self, n):
        self.n = n

    def scratch(self):
        return [pltpu.SemaphoreType.DMA((3 * self.n,)), pltpu.SemaphoreType.DMA((3 * self.n,))]

    def _copies(self, ins, outs, sems):
        x, y, c = _mesh_pos()
        return [pltpu.make_async_remote_copy(
            src_ref=ins[w].at[2 * chip[0] + chip[1]], dst_ref=outs[w].at[k],
            send_sem=sems[0].at[3 * w + k], recv_sem=sems[1].at[3 * w + k],
            device_id=(*chip, c), device_id_type=MESH_DEV)
            for w in range(self.n) for k, chip in enumerate(_other_chips(x, y))]

    def start(self, ins, outs, sems):
        for cp in self._copies(ins, outs, sems):
            cp.start()

    def forward(self, ins, outs, sems):
        pass

    def finish(self, ins, outs, sems):
        for cp in self._copies(ins, outs, sems):
            cp.wait()


def _rs_final(bufs, name, chips=()):
    n, nc = len(bufs), len(chips)
    plan = _ChipExchange(nc)

    def body(*refs):
        cin = refs[n:n + nc]
        outs = refs[n + nc:2 * n + nc]
        cout = refs[2 * n + nc:2 * n + 2 * nc]
        send_sems, recv_sems = refs[2 * n + 2 * nc:2 * n + 2 * nc + 2]
        csems = refs[2 * n + 2 * nc + 2:]
        x, y, c = _mesh_pos()
        if nc:
            plan.start(cin, cout, csems)
        cps = []
        for w in range(n):
            r2 = bufs[w].shape[0] // 2
            mine = outs[w].at[pl.ds(c * r2, r2), :]
            cps.append(pltpu.make_async_remote_copy(
                src_ref=mine, dst_ref=mine, send_sem=send_sems.at[w], recv_sem=recv_sems.at[w],
                device_id=(x, y, 1 - c), device_id_type=MESH_DEV))
            cps[-1].start()
        for cp in cps:
            cp.wait()
        if nc:
            plan.finish(cin, cout, csems)

    anyspec = pl.BlockSpec(memory_space=pl.ANY)
    return pl.pallas_call(
        body, name=name,
        out_shape=[jax.ShapeDtypeStruct(b.shape, b.dtype) for b in bufs]
                  + [jax.ShapeDtypeStruct((3,) + s.shape[1:], s.dtype) for s in chips],
        in_specs=[anyspec] * (n + nc), out_specs=[anyspec] * (n + nc),
        input_output_aliases={w: w for w in range(n)},
        scratch_shapes=[pltpu.SemaphoreType.DMA((n,)), pltpu.SemaphoreType.DMA((n,))] + (plan.scratch() if nc else []),
    )(*bufs, *chips)


BIG = ("w_in", "w_out", "w_gate", "w_up", "w_down")
TRANSPOSED = ("w_gate", "w_up")
WEIGHTS = ("w_ada", "b_ada", "g_mix", "w_in", "w_dw", "b_dw", "g_conv_ln", "b_conv_ln", "g_q", "g_k",
           "w_out", "g_ffn", "w_gate", "w_up", "w_down")


def _pad_to(a, rows, cols):
    return jnp.pad(a, ((0, rows - a.shape[0]), (0, cols - a.shape[1])))


def kernel(x, c, w_ada, b_ada, g_mix, w_in, w_dw, b_dw, g_conv_ln, b_conv_ln, g_q, g_k, w_out, g_ffn, w_gate, w_up, w_down, loss_target, m_w_ada, m_b_ada, m_g_mix, m_w_in, m_w_dw, m_b_dw, m_g_conv_ln, m_b_conv_ln, m_g_q, m_g_k, m_w_out, m_g_ffn, m_w_gate, m_w_up, m_w_down, v_w_ada, v_b_ada, v_g_mix, v_w_in, v_w_dw, v_b_dw, v_g_conv_ln, v_b_conv_ln, v_g_q, v_g_k, v_w_out, v_g_ffn, v_w_gate, v_w_up, v_w_down):
    w = dict(w_ada=w_ada, b_ada=b_ada, g_mix=g_mix, w_in=w_in, w_dw=w_dw, b_dw=b_dw, g_conv_ln=g_conv_ln,
             b_conv_ln=b_conv_ln, g_q=g_q, g_k=g_k, w_out=w_out, g_ffn=g_ffn, w_gate=w_gate, w_up=w_up, w_down=w_down)
    m = dict(w_ada=m_w_ada, b_ada=m_b_ada, g_mix=m_g_mix, w_in=m_w_in, w_dw=m_w_dw, b_dw=m_b_dw, g_conv_ln=m_g_conv_ln,
             b_conv_ln=m_b_conv_ln, g_q=m_g_q, g_k=m_g_k, w_out=m_w_out, g_ffn=m_g_ffn, w_gate=m_w_gate, w_up=m_w_up,
             w_down=m_w_down)
    v = dict(w_ada=v_w_ada, b_ada=v_b_ada, g_mix=v_g_mix, w_in=v_w_in, w_dw=v_w_dw, b_dw=v_b_dw, g_conv_ln=v_g_conv_ln,
             b_conv_ln=v_b_conv_ln, g_q=v_g_q, g_k=v_g_k, w_out=v_w_out, g_ffn=v_g_ffn, w_gate=v_w_gate, w_up=v_w_up,
             w_down=v_w_down)
    Bl, S, D = x.shape
    DC = g_conv_ln.shape[1]
    NA = w_ada.shape[2]
    xi, yi, ci = _mesh_pos()
    p = 2 * xi + yi
    dev = 2 * p + ci
    n_dev = 8
    pidx = jnp.reshape(p, (1,)).astype(jnp.int32)
    pc_idx = jnp.stack([p, ci]).astype(jnp.int32)

    first = jnp.concatenate([_pad_to(c, 8, D), _pad_to(w_dw[0], CONV_ROWS, D)], axis=0)
    shard = lambda a, nm: a[0].T if nm in TRANSPOSED else a[0]
    owned = {nm: _cast_weight(shard(w[nm], nm), pidx, "cast_" + nm) for nm in BIG}
    b_cols = lax.dynamic_slice_in_dim(b_ada, p * NA, NA, axis=1)
    g0, c_all, gm, w_in_full = _startup(first, w_ada[0], b_cols, owned["w_in"], Bl=Bl)
    g0 = g0.reshape(n_dev, 8 + CONV_ROWS, D)
    taps = jnp.concatenate([g0[2 * q, 8:, :w_dw.shape[2]] for q in range(4)], axis=1)
    wdw = jnp.where(lax.broadcasted_iota(jnp.int32, taps.shape, 0) == CONV_WIDTH, b_dw, taps)
    gm = gm.reshape(n_dev, n_dev * Bl, NA)
    mod = jnp.concatenate([lax.dynamic_slice_in_dim(gm[2 * q], dev * Bl, Bl, axis=0) for q in range(4)], axis=1)

    loc = _local_step(x, loss_target, mod, g_mix, wdw, g_conv_ln, b_conv_ln, g_q, g_k, g_ffn,
                      w_in_full, owned["w_out"], owned["w_gate"], owned["w_up"], owned["w_down"], pc_idx=pc_idx)

    halves = [_final_add(s32, r, pc_idx, "final_add_" + nm)
              for nm, s32, r in zip(EARLY_WEIGHTS, loc["early_sums"], loc["early_recv"])]
    (late_sib,) = _rs_sibling([loc["grads"]["w_in"]], "rs_sibling_in")
    late32, late16 = _pair_add(loc["grads"]["w_in"], late_sib, pc_idx, "pair_add_w_in")
    *early_full, late_recv = _rs_final(halves, "rs_final_early", chips=(late16,))
    grad = dict(zip(EARLY_WEIGHTS, early_full))
    (grad["w_in"],) = _rs_final([_final_add(late32, late_recv, pc_idx, "final_add_w_in")], "rs_final_in")

    mod_rows, _, small_rows = _small_layout(Bl)
    gs = loc["gathered_small"]
    red, bada8 = _small_reduce(gs, n_dev, Bl)
    dmod_all = gs.reshape(n_dev, small_rows, D)[:, :mod_rows].reshape(n_dev * Bl, 8, D)[:, :N_MOD].reshape(n_dev * Bl, N_MOD * D)
    grad["w_ada"] = _ada_bwd(c_all, lax.dynamic_slice_in_dim(dmod_all, p * NA, NA, axis=1))
    grad["b_ada"] = bada8[:N_MOD].reshape(1, N_MOD * D)
    grad["g_mix"] = red[0:1]
    grad["g_ffn"] = red[1:2]
    grad["g_conv_ln"] = red[2:3, :DC]
    grad["b_conv_ln"] = red[2:3, DC:2 * DC]
    grad["g_q"] = red[3:4, :HEAD_DIM]
    grad["g_k"] = red[3:4, HEAD_DIM:2 * HEAD_DIM]
    loss = red[4, 0]
    dwdw = red[8:8 + CONV_ROWS, :DC]
    grad["w_dw"] = lax.dynamic_slice_in_dim(dwdw[:CONV_WIDTH], p * w_dw.shape[2], w_dw.shape[2], axis=1)
    grad["b_dw"] = dwdw[CONV_WIDTH:CONV_WIDTH + 1]

    delta, new_m, new_v = {}, {}, {}
    for nm in WEIGHTS:
        shp = w[nm].shape
        if nm in TRANSPOSED:
            d_, m_, v_ = _adamw(w[nm][0].T, grad[nm], m[nm][0].T, v[nm][0].T, "adamw_" + nm)
            grad[nm], delta[nm], new_m[nm], new_v[nm] = (a.T.reshape(shp) for a in (grad[nm], d_, m_, v_))
            continue
        two_d = (shp[-2], shp[-1]) if len(shp) == 3 else shp
        d_, m_, v_ = _adamw(w[nm].reshape(two_d), grad[nm].reshape(two_d), m[nm].reshape(two_d), v[nm].reshape(two_d),
                            "adamw_" + nm)
        grad[nm] = grad[nm].reshape(shp)
        delta[nm], new_m[nm], new_v[nm] = d_.reshape(shp), m_.reshape(shp), v_.reshape(shp)

    return (loss, loc["dx"], *[grad[nm] for nm in WEIGHTS], *[delta[nm] for nm in WEIGHTS],
            *[new_m[nm] for nm in WEIGHTS], *[new_v[nm] for nm in WEIGHTS])
```

```python
import functools

import jax
import jax.numpy as jnp
import numpy as np
from jax import lax
from jax.experimental import pallas as pl
from jax.experimental.pallas import tpu as pltpu

F32 = jnp.float32
MXU_DTYPE = jnp.bfloat16
ACT_DTYPE = jnp.bfloat16
EPS = 1e-6
NEG_INF = -1e30
HEAD_DIM = 64
LANES = 128
RADIUS = 64
QBLK = 128
DILATIONS = (1, 4, 16)
CONV_WIDTH = 31
CONV_PAD = CONV_WIDTH // 2
CONV_ROWS = 32
N_MOD = 6
ADAM_LR, ADAM_B1, ADAM_B2, ADAM_EPS, ADAM_WD, ADAM_STEP = 0.001, 0.9, 0.999, 1e-08, 0.01, 10
MESH_DEV = pl.DeviceIdType.MESH
VMEM_LIMIT = 56 << 20
ATTN_BWD_VMEM = 60 << 20


def _cp(sem=None, vmem=VMEM_LIMIT):
    kw = dict(vmem_limit_bytes=vmem)
    if sem is not None:
        kw["dimension_semantics"] = sem
    return pltpu.CompilerParams(**kw)


def _sigmoid(x):
    return 1.0 / (1.0 + jnp.exp(-x))


def _dot(a, b):
    return jnp.dot(a, b, preferred_element_type=F32)


def _dot_nt(a, b):
    return lax.dot_general(a, b, (((1,), (1,)), ((), ())), preferred_element_type=F32)


def _dot_tn(a, b):
    return lax.dot_general(a, b, (((0,), (0,)), ((), ())), preferred_element_type=F32)


def _colsum(v):
    return jnp.sum(v, axis=0, keepdims=True)


def _load_resident(i, pairs, sems):
    @pl.when(i == 0)
    def _():
        cps = [pltpu.make_async_copy(src, dst, sems.at[n]) for n, (src, dst) in enumerate(pairs)]
        for c in cps:
            c.start()
        for c in cps:
            c.wait()


def _fwd_in(x2, mod, g_mix, gq2, gk2, w_in, *, S, tm, n_ag):
    T, D = x2.shape
    P, _, Nb = w_in.shape
    n_in = P * Nb
    n_slab = (n_in - n_ag) // LANES
    NS = n_slab // 3
    tps = S // tm

    def body(x_ref, mod_ref, g_ref, gq_ref, gk_ref, w_ref, ag_ref, qkv_ref, qkh_ref, h_ref):
        x = x_ref[...]
        r = lax.rsqrt(jnp.mean(x * x, axis=-1, keepdims=True) + EPS)
        n = x * r * g_ref[...]
        h = n * (1.0 + mod_ref[:, D:2 * D]) + mod_ref[:, 0:D]
        hb = h.astype(MXU_DTYPE)
        h_ref[...] = hb
        parts = [_dot(hb, w_ref[p]) for p in range(P)]
        proj = jnp.concatenate(parts, axis=1) if P > 1 else parts[0]
        ag_ref[...] = proj[:, :n_ag]
        mm = _head_mean_matrix()
        for j in range(n_slab):
            v = proj[:, n_ag + LANES * j:n_ag + LANES * (j + 1)]
            qkv_ref[j] = v
            if j < 2 * NS:
                gain = gq_ref[...] * (HEAD_DIM ** -0.5 * LOG2E) if j < NS else gk_ref[...]
                qkh_ref[j] = v * lax.rsqrt(_head_mean(v * v, mm) + EPS) * gain

    return pl.pallas_call(
        body, grid=(T // tm,), name="fwd_in",
        in_specs=[pl.BlockSpec((tm, D), lambda i: (i, 0)),
                  pl.BlockSpec((None, 1, N_MOD * D), lambda i: (i // tps, 0, 0)),
                  pl.BlockSpec((1, D), lambda i: (0, 0)),
                  pl.BlockSpec((1, LANES), lambda i: (0, 0)), pl.BlockSpec((1, LANES), lambda i: (0, 0)),
                  pl.BlockSpec((P, D, Nb), lambda i: (0, 0, 0))],
        out_specs=[pl.BlockSpec((tm, n_ag), lambda i: (i, 0)),
                   pl.BlockSpec((n_slab, tm, LANES), lambda i: (0, i, 0)),
                   pl.BlockSpec((2 * NS, tm, LANES), lambda i: (0, i, 0)),
                   pl.BlockSpec((tm, D), lambda i: (i, 0))],
        out_shape=[jax.ShapeDtypeStruct((T, n_ag), F32),
                   jax.ShapeDtypeStruct((n_slab, T, LANES), F32),
                   jax.ShapeDtypeStruct((2 * NS, T, LANES), F32),
                   jax.ShapeDtypeStruct((T, D), MXU_DTYPE)],
        compiler_params=_cp(("arbitrary",)),
    )(x2, mod, g_mix, gq2, gk2, w_in)


CONV_CH = 128


def _conv_taps(win, w_ref, acc, reverse):
    n = win.shape[0]
    for b in range(8):
        wb = win if b == 0 else pltpu.roll(win, shift=n - b, axis=0)
        for a in range(4):
            o = 8 * a + b
            if o < 1 or o > CONV_WIDTH:
                continue
            k = (CONV_WIDTH - o) if reverse else (o - 1)
            acc = acc + w_ref[k:k + 1, :] * wb[8 * a:8 * a + CONV_CH, :]
    return acc


def _conv_fwd(ag, wdw, *, Bl, S, DC):
    T = ag.shape[0]
    nsc = DC // LANES
    CH = CONV_CH

    def body(a_ref, g_ref, w_ref, cv_ref, upad):
        zeros16 = jnp.zeros((16, LANES), F32)
        upad[0:16, :] = zeros16
        upad[S + 16:S + 32, :] = zeros16

        def fill(i, _):
            r0 = pl.multiple_of(i * CH, CH)
            a = a_ref[pl.ds(r0, CH), :]
            g = g_ref[pl.ds(r0, CH), :]
            upad[pl.ds(r0 + 16, CH), :] = a * _sigmoid(g)
            return 0
        lax.fori_loop(0, S // CH, fill, 0)

        def conv(i, _):
            r0 = pl.multiple_of(i * CH, CH)
            win = upad[pl.ds(r0, CH + 32), :]
            acc = jnp.zeros((CH, LANES), F32) + w_ref[CONV_WIDTH:CONV_WIDTH + 1, :]
            cv_ref[pl.ds(r0, CH), :] = _conv_taps(win, w_ref, acc, reverse=False)
            return 0
        lax.fori_loop(0, S // CH, conv, 0)

    return pl.pallas_call(
        body, grid=(Bl, nsc), name="conv_fwd",
        in_specs=[pl.BlockSpec((S, LANES), lambda b, j: (b, j)),
                  pl.BlockSpec((S, LANES), lambda b, j: (b, nsc + j)),
                  pl.BlockSpec((CONV_ROWS, LANES), lambda b, j: (0, j))],
        out_specs=pl.BlockSpec((S, LANES), lambda b, j: (b, j)),
        out_shape=jax.ShapeDtypeStruct((T, DC), F32),
        scratch_shapes=[pltpu.VMEM((S + 32, LANES), F32)],
        compiler_params=_cp(("arbitrary", "arbitrary")),
    )(ag, ag, wdw)


def _conv_bwd(ag, dcv, wdw, *, Bl, S, DC):
    T = ag.shape[0]
    nsc = DC // LANES
    CH = CONV_CH

    def body(a_ref, g_ref, d_ref, w_ref, da_ref, dg_ref, dw_ref, upad, dpad, wacc):
        b = pl.program_id(1)
        zeros16 = jnp.zeros((16, LANES), F32)
        upad[0:16, :] = zeros16
        upad[S + 16:S + 32, :] = zeros16
        dpad[0:16, :] = zeros16
        dpad[S + 16:S + 32, :] = zeros16

        @pl.when(b == 0)
        def _():
            wacc[...] = jnp.zeros_like(wacc)

        def fill(i, _):
            r0 = pl.multiple_of(i * CH, CH)
            a = a_ref[pl.ds(r0, CH), :]
            g = g_ref[pl.ds(r0, CH), :]
            upad[pl.ds(r0 + 16, CH), :] = a * _sigmoid(g)
            dpad[pl.ds(r0 + 16, CH), :] = d_ref[pl.ds(r0, CH), :]
            return 0
        lax.fori_loop(0, S // CH, fill, 0)

        def step(i, _):
            r0 = pl.multiple_of(i * CH, CH)
            dwin = dpad[pl.ds(r0, CH + 32), :]
            du = _conv_taps(dwin, w_ref, jnp.zeros((CH, LANES), F32), reverse=True)
            a = a_ref[pl.ds(r0, CH), :]
            g = g_ref[pl.ds(r0, CH), :]
            sg = _sigmoid(g)
            da_ref[pl.ds(r0, CH), :] = du * sg
            dg_ref[pl.ds(r0, CH), :] = du * a * sg * (1.0 - sg)
            dc = d_ref[pl.ds(r0, CH), :]
            uwin = upad[pl.ds(r0, CH + 32), :]
            n = CH + 32
            for bb in range(8):
                wb = uwin if bb == 0 else pltpu.roll(uwin, shift=n - bb, axis=0)
                for aa in range(4):
                    o = 8 * aa + bb
                    if o < 1 or o > CONV_WIDTH:
                        continue
                    k = o - 1
                    prod = dc * wb[8 * aa:8 * aa + CH, :]
                    part = prod[0:8, :]
                    for q in range(1, CH // 8):
                        part = part + prod[8 * q:8 * q + 8, :]
                    wacc[8 * k:8 * k + 8, :] += part
            part = dc[0:8, :]
            for q in range(1, CH // 8):
                part = part + dc[8 * q:8 * q + 8, :]
            wacc[8 * CONV_WIDTH:8 * CONV_WIDTH + 8, :] += part
            return 0
        lax.fori_loop(0, S // CH, step, 0)

        @pl.when(b == Bl - 1)
        def _():
            for k in range(CONV_ROWS):
                dw_ref[k:k + 1, :] = jnp.sum(wacc[8 * k:8 * k + 8, :], axis=0, keepdims=True)

    return pl.pallas_call(
        body, grid=(nsc, Bl), name="conv_bwd",
        in_specs=[pl.BlockSpec((S, LANES), lambda j, b: (b, j)),
                  pl.BlockSpec((S, LANES), lambda j, b: (b, nsc + j)),
                  pl.BlockSpec((S, LANES), lambda j, b: (b, j)),
                  pl.BlockSpec((CONV_ROWS, LANES), lambda j, b: (0, j))],
        out_specs=[pl.BlockSpec((S, LANES), lambda j, b: (b, j)),
                   pl.BlockSpec((S, LANES), lambda j, b: (b, j)),
                   pl.BlockSpec((CONV_ROWS, LANES), lambda j, b: (0, j))],
        out_shape=[jax.ShapeDtypeStruct((T, DC), F32), jax.ShapeDtypeStruct((T, DC), F32),
                   jax.ShapeDtypeStruct((CONV_ROWS, DC), F32)],
        scratch_shapes=[pltpu.VMEM((S + 32, LANES), F32), pltpu.VMEM((S + 32, LANES), F32),
                        pltpu.VMEM((8 * CONV_ROWS, LANES), F32)],
        compiler_params=_cp(("arbitrary", "arbitrary")),
    )(ag, ag, dcv, wdw)


ROWCH = 256


LOG2E = 1.4426950408889634
LN2 = 0.6931471805599453
N_EDGE = 4


def _head_mean_matrix():
    r = lax.broadcasted_iota(jnp.int32, (LANES, LANES), 0) // HEAD_DIM
    c = lax.broadcasted_iota(jnp.int32, (LANES, LANES), 1) // HEAD_DIM
    return jnp.where(r == c, 1.0 / HEAD_DIM, 0.0).astype(jnp.bfloat16)


def _head_mean(v, mm):
    hi = v.astype(jnp.bfloat16)
    lo = (v - hi.astype(F32)).astype(jnp.bfloat16)
    return _dot(hi, mm) + _dot(lo, mm)


def _stack_heads(blk, lane_lo):
    z = jnp.zeros_like(blk)
    return jnp.concatenate([jnp.where(lane_lo, blk, z), jnp.where(lane_lo, z, blk)], axis=0)


def _merge_heads(v2, lane_lo):
    return jnp.where(lane_lo, v2[:QBLK], v2[QBLK:])


def _bias_tables(bias_ref, slope_ref):
    row = lax.broadcasted_iota(jnp.int32, (2 * QBLK, 2 * QBLK), 0)
    col = lax.broadcasted_iota(jnp.int32, (2 * QBLK, 2 * QBLK), 1)
    rel = jnp.abs(col - RADIUS - (row % QBLK))
    slope = jnp.where(row < QBLK, slope_ref[0:1, 0:1], slope_ref[0:1, HEAD_DIM:HEAD_DIM + 1]) * LOG2E
    for pi, d in enumerate(DILATIONS):
        inside = jnp.where(rel <= RADIUS, -slope * (float(d) * rel.astype(F32)), NEG_INF)
        for e in range(N_EDGE):
            t = inside
            if e & 1:
                t = jnp.where(col < RADIUS, NEG_INF, t)
            if e & 2:
                t = jnp.where(col >= QBLK + RADIUS, NEG_INF, t)
            bias_ref[N_EDGE * pi + e] = t


def _edge_index(qb, nb):
    return jnp.where(qb == 0, 1, 0) + jnp.where(qb == nb - 1, 2, 0)


VIA = 4


def _residue(d, s):
    return (s % VIA) * VIA + s // VIA if d == VIA * VIA else s


def _gather_rows(src_ref, dst_ref, S, d, pad, f32_copy=None):
    n = S // d
    seg = n + 2 * RADIUS if pad else n
    step = min(n, 512)
    two_step = d == VIA * VIA and f32_copy is not None
    for s in range(d):
        base = s * seg
        if pad:
            dst_ref[base:base + RADIUS, :] = jnp.zeros((RADIUS, LANES), dst_ref.dtype)
            dst_ref[base + RADIUS + n:base + seg, :] = jnp.zeros((RADIUS, LANES), dst_ref.dtype)
            base += RADIUS
        for c0 in range(0, n, step):
            if d == 1:
                v = src_ref[c0:c0 + step, :]
            elif two_step:
                v = f32_copy[pl.ds((s // VIA) * (S // VIA) + s % VIA + c0 * VIA, step, stride=VIA), :]
            else:
                v = src_ref[pl.ds(_residue(d, s) + c0 * d, step, stride=d), :]
                if d == VIA and f32_copy is not None:
                    f32_copy[s * n + c0:s * n + c0 + step, :] = v
            dst_ref[base + c0:base + c0 + step, :] = v.astype(dst_ref.dtype)


def _scatter_rows(src_ref, dst_ref, S, d, pad, accumulate):
    n = S // d
    seg = n + 2 * RADIUS if pad else n
    step = min(n, 512)
    for s in range(d):
        base = s * seg + (RADIUS if pad else 0)
        for c0 in range(0, n, step):
            v = src_ref[base + c0:base + c0 + step, :]
            if d == 1:
                idx = pl.ds(c0, step)
            else:
                idx = pl.ds(_residue(d, s) + c0 * d, step, stride=d)
            if accumulate:
                dst_ref[idx, :] = dst_ref[idx, :] + v
            else:
                dst_ref[idx, :] = v


def _zero_uncovered(acc, S, d):
    n = S // d
    if (n // QBLK) % 2:
        return
    seg = n + 2 * RADIUS
    for r in range(d):
        acc[0, r * seg + n:r * seg + seg, :] = jnp.zeros((2 * RADIUS, LANES), F32)
        acc[1, r * seg:r * seg + 2 * RADIUS, :] = jnp.zeros((2 * RADIUS, LANES), F32)


def _scatter_parity(acc, dst_ref, S, d):
    n = S // d
    seg = n + 2 * RADIUS
    step = min(n, 512)
    one_block = (n // QBLK) % 2 == 1
    for r in range(d):
        base = r * seg + RADIUS
        for c0 in range(0, n, step):
            rows = slice(base + c0, base + c0 + step)
            v = acc[r % 2, rows, :] if one_block else acc[0, rows, :] + acc[1, rows, :]
            idx = pl.ds(c0, step) if d == 1 else pl.ds(_residue(d, r) + c0 * d, step, stride=d)
            dst_ref[idx, :] = dst_ref[idx, :] + v


PIPE_UNROLL = 4
PIPE_SLOTS = 16
BWD_SLOTS = 12


def _pipeline(n_items, stages, unroll):
    K = len(stages)
    assert n_items % unroll == 0 and K * unroll <= (PIPE_SLOTS if K == 4 else BWD_SLOTS)
    trips = n_items // unroll
    assert trips >= K - 1

    def trip(t, static):
        for s in reversed(range(K)):
            if static and not 0 <= t - s < trips:
                continue
            for u in range(unroll):
                item = unroll * (t - s) + u
                stages[s](jnp.int32(item) if static else item)

    for t in range(K - 1):
        trip(t, True)

    def full(t, carry):
        trip(t, False)
        return carry
    lax.fori_loop(K - 1, trips, full, 0)
    for t in range(trips, trips + K - 1):
        trip(t, True)


def _attn_fwd(qkh, qkv, slopes, *, Bl, S, hosted=()):
    n3, T, _ = qkv.shape
    NS = n3 // 3
    NB = S // QBLK
    PADR = S + 2 * RADIUS * DILATIONS[-1]
    nh = len(hosted)
    plan = _WeightGather([b.shape for b in hosted]) if nh else None
    n_steps = Bl * NS

    def body(qh, kh, v_ref, slope_ref, *rest):
        o_ref, lse_ref = rest[nh:nh + 2]
        wouts = rest[nh + 2:2 * nh + 2]
        (qp, kp, vp, op, lp, onat, lnat, bias_ref, sbuf, pbuf, mbuf, lbuf, tmps) = rest[2 * nh + 2:2 * nh + 15]
        sems = rest[2 * nh + 15:]
        step = pl.program_id(0) * NS + pl.program_id(1)
        if nh:
            @pl.when(step == 0)
            def _():
                plan.start(wouts, sems)

            @pl.when(step == (3 * n_steps) // 4)
            def _():
                plan.forward(wouts, sems)

        lane_lo = lax.broadcasted_iota(jnp.int32, (QBLK, LANES), 1) < HEAD_DIM
        _bias_tables(bias_ref, slope_ref)

        for pi, d in enumerate(DILATIONS):
            n = S // d
            nb = n // QBLK
            _gather_rows(qh, qp, S, d, pad=False, f32_copy=tmps.at[0])
            _gather_rows(kh, kp, S, d, pad=True, f32_copy=tmps.at[1])
            _gather_rows(v_ref, vp, S, d, pad=True, f32_copy=tmps.at[2])

            def offsets(i, nb=nb):
                r = i // nb
                return pl.multiple_of(i * QBLK, QBLK), pl.multiple_of((i + r) * QBLK, QBLK), i % nb

            def scores(i, pi=pi, nb=nb):
                q0, k0, qb = offsets(i)
                qs = _stack_heads(qp[pl.ds(q0, QBLK), :], lane_lo)
                sbuf[i % PIPE_SLOTS] = (_dot_nt(qs, kp[pl.ds(k0, 2 * QBLK), :])
                                        + bias_ref[N_EDGE * pi + _edge_index(qb, nb)])

            def rowmax(i):
                m = jnp.max(sbuf[i % PIPE_SLOTS], axis=1, keepdims=True)
                mbuf[i % PIPE_SLOTS] = jnp.broadcast_to(m, (2 * QBLK, LANES))

            def expsum(i):
                m = mbuf[i % PIPE_SLOTS]
                p = jnp.exp2(sbuf[i % PIPE_SLOTS] - jnp.concatenate([m, m], axis=1))
                pbuf[i % PIPE_SLOTS] = p.astype(MXU_DTYPE)
                lbuf[i % PIPE_SLOTS] = jnp.broadcast_to(jnp.sum(p, axis=1, keepdims=True), (2 * QBLK, LANES))

            def values(i):
                q0, k0, _ = offsets(i)
                l = lbuf[i % PIPE_SLOTS]
                o2 = _dot(pbuf[i % PIPE_SLOTS], vp[pl.ds(k0, 2 * QBLK), :]) * (1.0 / l)
                op[pl.ds(q0, QBLK), :] = _merge_heads(o2, lane_lo)
                lp[pl.ds(q0, QBLK), :] = _merge_heads(mbuf[i % PIPE_SLOTS] + jnp.log2(l), lane_lo)

            _pipeline(NB, [scores, rowmax, expsum, values], PIPE_UNROLL)
            _scatter_rows(op, onat.at[pi], S, d, pad=False, accumulate=False)
            _scatter_rows(lp, lnat.at[pi], S, d, pad=False, accumulate=False)

        for c0 in range(0, S, ROWCH):
            ls = [lnat[pi, c0:c0 + ROWCH, :] for pi in range(len(DILATIONS))]
            mx = jnp.maximum(jnp.maximum(ls[0], ls[1]), ls[2])
            es = [jnp.exp2(l - mx) for l in ls]
            tot = es[0] + es[1] + es[2]
            inv = 1.0 / tot
            acc = (es[0] * inv) * onat[0, c0:c0 + ROWCH, :]
            for pi in (1, 2):
                acc = acc + (es[pi] * inv) * onat[pi, c0:c0 + ROWCH, :]
            o_ref[c0:c0 + ROWCH, :] = acc
            lse_ref[c0:c0 + ROWCH, :] = mx + jnp.log2(tot)

        if nh:
            @pl.when(step == n_steps - 1)
            def _():
                plan.finish(wouts, sems)

    spec_in = lambda off: pl.BlockSpec((None, S, LANES), lambda b, j: (off * NS + j, b, 0))
    out = pl.BlockSpec((S, LANES), lambda b, j: (b, j))
    anyspec = pl.BlockSpec(memory_space=pl.ANY)
    return pl.pallas_call(
        body, grid=(Bl, NS), name="attn_fwd",
        in_specs=[spec_in(0), spec_in(1), spec_in(2),
                  pl.BlockSpec((None, 8, LANES), lambda b, j: (j, 0, 0))] + [anyspec] * nh,
        out_specs=[out, out] + [anyspec] * nh,
        out_shape=[jax.ShapeDtypeStruct((T, NS * LANES), F32)] * 2
                  + [jax.ShapeDtypeStruct(b.shape, b.dtype) for b in hosted],
        input_output_aliases={4 + w: 2 + w for w in range(nh)},
        scratch_shapes=[pltpu.VMEM((S, LANES), MXU_DTYPE), pltpu.VMEM((PADR, LANES), MXU_DTYPE),
                        pltpu.VMEM((PADR, LANES), MXU_DTYPE),
                        pltpu.VMEM((S, LANES), F32), pltpu.VMEM((S, LANES), F32),
                        pltpu.VMEM((3, S, LANES), F32), pltpu.VMEM((3, S, LANES), F32),
                        pltpu.VMEM((N_EDGE * len(DILATIONS), 2 * QBLK, 2 * QBLK), F32),
                        pltpu.VMEM((PIPE_SLOTS, 2 * QBLK, 2 * QBLK), F32),
                        pltpu.VMEM((PIPE_SLOTS, 2 * QBLK, 2 * QBLK), MXU_DTYPE),
                        pltpu.VMEM((PIPE_SLOTS, 2 * QBLK, LANES), F32), pltpu.VMEM((PIPE_SLOTS, 2 * QBLK, LANES), F32),
                        pltpu.VMEM((3, S, LANES), F32)]
                       + (plan.scratch() if nh else []),
        compiler_params=_cp(("arbitrary", "arbitrary")),
    )(qkh, qkh, qkv, slopes, *hosted)


def _attn_bwd(qkh, qkv, o, lse, do, gq2, gk2, slopes, *, Bl, S, hosted=()):
    n3, T, _ = qkv.shape
    NS = n3 // 3
    NB = S // QBLK
    PADR = S + 2 * RADIUS * DILATIONS[-1]
    QSCALE = HEAD_DIM ** -0.5
    nh = len(hosted)
    plan = _ChipExchange(nh)
    n_steps = Bl * NS

    def body(qh, kh, q_ref, k_ref, v_ref, o_ref, lse_ref, do_ref, gq_ref, gk_ref, slope_ref, *rest):
        hin = rest[:nh]
        dq_ref, dk_ref, dv_ref, gacc_ref = rest[nh:nh + 4]
        hout = rest[nh + 4:2 * nh + 4]
        (ld, qp, kp, vp, dop, ldp, dqp, dkacc, dvacc, dqn, dkn, bias_ref,
         sbuf, dpbuf, pbuf, dsbuf, tmps) = rest[2 * nh + 4:2 * nh + 21]
        sems = rest[2 * nh + 21:]
        step = pl.program_id(0) * NS + pl.program_id(1)

        @pl.when(step == 0)
        def _():
            gacc_ref[...] = jnp.zeros_like(gacc_ref)
            if nh:
                plan.start(hin, hout, sems)

        mm = _head_mean_matrix()
        lane_lo = lax.broadcasted_iota(jnp.int32, (QBLK, LANES), 1) < HEAD_DIM
        _bias_tables(bias_ref, slope_ref)
        lse_lanes = lax.broadcasted_iota(jnp.int32, (ROWCH, LANES), 1) % HEAD_DIM < HEAD_DIM // 2
        for c0 in range(0, S, ROWCH):
            delta = _head_mean(do_ref[c0:c0 + ROWCH, :] * o_ref[c0:c0 + ROWCH, :], mm) * HEAD_DIM
            ld[c0:c0 + ROWCH, :] = jnp.where(lse_lanes, lse_ref[c0:c0 + ROWCH, :], delta)
            dqn[c0:c0 + ROWCH, :] = jnp.zeros((ROWCH, LANES), F32)
            dkn[c0:c0 + ROWCH, :] = jnp.zeros((ROWCH, LANES), F32)
            dv_ref[c0:c0 + ROWCH, :] = jnp.zeros((ROWCH, LANES), F32)

        for pi, d in enumerate(DILATIONS):
            n = S // d
            nb = n // QBLK
            _gather_rows(qh, qp, S, d, pad=False, f32_copy=tmps.at[0])
            _gather_rows(kh, kp, S, d, pad=True, f32_copy=tmps.at[1])
            _gather_rows(v_ref, vp, S, d, pad=True, f32_copy=tmps.at[2])
            _gather_rows(do_ref, dop, S, d, pad=False, f32_copy=tmps.at[3])
            _gather_rows(ld, ldp, S, d, pad=False, f32_copy=tmps.at[4])
            _zero_uncovered(dkacc, S, d)
            _zero_uncovered(dvacc, S, d)

            def offsets(i, nb=nb):
                r = i // nb
                return pl.multiple_of(i * QBLK, QBLK), pl.multiple_of((i + r) * QBLK, QBLK), i % nb

            def scores(i, pi=pi, nb=nb):
                q0, k0, qb = offsets(i)
                qs = _stack_heads(qp[pl.ds(q0, QBLK), :], lane_lo)
                dos = _stack_heads(dop[pl.ds(q0, QBLK), :], lane_lo)
                sbuf[i % BWD_SLOTS] = (_dot_nt(qs, kp[pl.ds(k0, 2 * QBLK), :])
                                       + bias_ref[N_EDGE * pi + _edge_index(qb, nb)])
                dpbuf[i % BWD_SLOTS] = _dot_nt(dos, vp[pl.ds(k0, 2 * QBLK), :])

            def probs(i):
                q0, _, _ = offsets(i)
                blk = ldp[pl.ds(q0, QBLK), :]
                half = HEAD_DIM // 2
                lcol = jnp.concatenate([blk[:, 0:1], blk[:, HEAD_DIM:HEAD_DIM + 1]], axis=0)
                dcol = jnp.concatenate([blk[:, half:half + 1], blk[:, HEAD_DIM + half:HEAD_DIM + half + 1]], axis=0)
                p = jnp.exp2(sbuf[i % BWD_SLOTS] - lcol)
                pbuf[i % BWD_SLOTS] = p.astype(MXU_DTYPE)
                dsbuf[i % BWD_SLOTS] = (p * (dpbuf[i % BWD_SLOTS] - dcol)).astype(MXU_DTYPE)

            def grads(i):
                q0, k0, _ = offsets(i)
                qs = _stack_heads(qp[pl.ds(q0, QBLK), :], lane_lo)
                dos = _stack_heads(dop[pl.ds(q0, QBLK), :], lane_lo)
                ds = dsbuf[i % BWD_SLOTS]
                dvacc[i % 2, pl.ds(k0, 2 * QBLK), :] = _dot_tn(pbuf[i % BWD_SLOTS], dos)
                dkacc[i % 2, pl.ds(k0, 2 * QBLK), :] = _dot_tn(ds, qs)
                dqp[pl.ds(q0, QBLK), :] = _merge_heads(_dot(ds, kp[pl.ds(k0, 2 * QBLK), :]), lane_lo)

            _pipeline(NB, [scores, probs, grads], PIPE_UNROLL)
            _scatter_rows(dqp, dqn, S, d, pad=False, accumulate=True)
            _scatter_parity(dkacc, dkn, S, d)
            _scatter_parity(dvacc, dv_ref, S, d)

        gq_sum = jnp.zeros((8, LANES), F32)
        gk_sum = jnp.zeros((8, LANES), F32)
        for c0 in range(0, S, ROWCH):
            for src_ref, dn, g_ref, dst_ref, scale, is_q in ((q_ref, dqn, gq_ref, dq_ref, QSCALE, True),
                                                             (k_ref, dkn, gk_ref, dk_ref, LN2, False)):
                x = src_ref[c0:c0 + ROWCH, :]
                dh = dn[c0:c0 + ROWCH, :]
                rr = lax.rsqrt(_head_mean(x * x, mm) + EPS)
                e = dh * (g_ref[...] * scale)
                dst_ref[c0:c0 + ROWCH, :] = rr * e - x * (rr * rr * rr) * _head_mean(e * x, mm)
                gpart = dh * (x * rr * scale)
                acc8 = gpart[0:8, :]
                for q8 in range(1, ROWCH // 8):
                    acc8 = acc8 + gpart[8 * q8:8 * q8 + 8, :]
                if is_q:
                    gq_sum = gq_sum + acc8
                else:
                    gk_sum = gk_sum + acc8
        gacc_ref[0:1, :] += jnp.sum(gq_sum, axis=0, keepdims=True)
        gacc_ref[1:2, :] += jnp.sum(gk_sum, axis=0, keepdims=True)

        if nh:
            @pl.when(step == n_steps - 1)
            def _():
                plan.finish(hin, hout, sems)

    spec_in = lambda off: pl.BlockSpec((None, S, LANES), lambda b, j: (off * NS + j, b, 0))
    tok = pl.BlockSpec((S, LANES), lambda b, j: (b, j))
    vec = pl.BlockSpec((1, LANES), lambda b, j: (0, 0))
    slab_out = pl.BlockSpec((None, S, LANES), lambda b, j: (j, b, 0))
    f32buf = lambda rows: pltpu.VMEM((rows, LANES), F32)
    bfbuf = lambda rows: pltpu.VMEM((rows, LANES), MXU_DTYPE)
    anyspec = pl.BlockSpec(memory_space=pl.ANY)
    return pl.pallas_call(
        body, grid=(Bl, NS), name="attn_bwd",
        in_specs=[spec_in(0), spec_in(1), spec_in(0), spec_in(1), spec_in(2), tok, tok, tok, vec, vec,
                  pl.BlockSpec((None, 8, LANES), lambda b, j: (j, 0, 0))] + [anyspec] * nh,
        out_specs=[slab_out, slab_out, slab_out, pl.BlockSpec((8, LANES), lambda b, j: (0, 0))] + [anyspec] * nh,
        out_shape=[jax.ShapeDtypeStruct((NS, T, LANES), F32)] * 3 + [jax.ShapeDtypeStruct((8, LANES), F32)]
                  + [jax.ShapeDtypeStruct((3,) + h.shape[1:], h.dtype) for h in hosted],
        scratch_shapes=[f32buf(S),
                        bfbuf(S), bfbuf(PADR), bfbuf(PADR), bfbuf(S),
                        f32buf(S), f32buf(S),
                        pltpu.VMEM((2, PADR, LANES), F32), pltpu.VMEM((2, PADR, LANES), F32),
                        f32buf(S), f32buf(S),
                        pltpu.VMEM((N_EDGE * len(DILATIONS), 2 * QBLK, 2 * QBLK), F32),
                        pltpu.VMEM((BWD_SLOTS, 2 * QBLK, 2 * QBLK), F32),
                        pltpu.VMEM((BWD_SLOTS, 2 * QBLK, 2 * QBLK), F32),
                        pltpu.VMEM((BWD_SLOTS, 2 * QBLK, 2 * QBLK), MXU_DTYPE),
                        pltpu.VMEM((BWD_SLOTS, 2 * QBLK, 2 * QBLK), MXU_DTYPE),
                        pltpu.VMEM((5, S, LANES), F32)]
                       + (plan.scratch() if nh else []),
        compiler_params=_cp(("arbitrary", "arbitrary"), vmem=ATTN_BWD_VMEM),
    )(qkh, qkh, qkv, qkv, qkv, o, lse, do, gq2, gk2, slopes, *hosted)


def _layer_norm_parts(cv, g_ln, b_ln):
    mu = jnp.mean(cv, axis=-1, keepdims=True)
    cen = cv - mu
    rs = lax.rsqrt(jnp.mean(cen * cen, axis=-1, keepdims=True) + EPS)
    z = cen * rs
    return z, rs, z * g_ln + b_ln


def _ffn_fwd(x2, cv, ya, tgt, mod, g_ln, b_ln, g_ffn, w_out, w_gate, w_up, w_down, *, S, tm):
    T, D = x2.shape
    DC = cv.shape[1]
    P, Kb, _ = w_out.shape
    Fb = w_down.shape[1]
    tps = S // tm

    def body(x_ref, cv_ref, ya_ref, t_ref, mod_ref, gln_ref, bln_ref, gf_ref, wo_hbm, wg_hbm, wu_hbm, wd_hbm,
             x1_ref, ycat_ref, mix_ref, h2_ref, g_ref, u_ref, a_ref, f_ref, dy_ref, loss_ref,
             wo, wg, wu, wd, sems):
        i = pl.program_id(0)
        _load_resident(i, [(wo_hbm, wo), (wg_hbm, wg), (wu_hbm, wu), (wd_hbm, wd)], sems)

        @pl.when(i == 0)
        def _():
            loss_ref[...] = jnp.zeros_like(loss_ref)

        _, _, ln = _layer_norm_parts(cv_ref[...], gln_ref[...], bln_ref[...])
        yc = ln * _sigmoid(ln)
        ycat = jnp.concatenate([yc, ya_ref[...]], axis=1).astype(MXU_DTYPE)
        ycat_ref[...] = ycat
        mix = _dot(ycat[:, 0:Kb], wo[0])
        for p in range(1, P):
            mix = mix + _dot(ycat[:, Kb * p:Kb * (p + 1)], wo[p])
        mix_ref[...] = mix.astype(ACT_DTYPE)
        x1 = x_ref[...] + mod_ref[:, 2 * D:3 * D] * mix
        x1_ref[...] = x1
        r2 = lax.rsqrt(jnp.mean(x1 * x1, axis=-1, keepdims=True) + EPS)
        h2 = (x1 * r2 * gf_ref[...]) * (1.0 + mod_ref[:, 4 * D:5 * D]) + mod_ref[:, 3 * D:4 * D]
        h2b = h2.astype(MXU_DTYPE)
        h2_ref[...] = h2b
        f = jnp.zeros((tm, D), F32)
        for p in range(P):
            g = _dot_nt(h2b, wg[p])
            u = _dot_nt(h2b, wu[p])
            a = (g * _sigmoid(g) * u).astype(MXU_DTYPE)
            g_ref[p] = g.astype(ACT_DTYPE)
            u_ref[p] = u.astype(ACT_DTYPE)
            a_ref[p] = a
            f = f + _dot(a, wd[p])
        f_ref[...] = f.astype(ACT_DTYPE)
        err = x1 + mod_ref[:, 5 * D:6 * D] * f - t_ref[...]
        dy_ref[...] = err * (1.0 / D)
        tot = jnp.sum(_colsum(err * err), axis=1, keepdims=True)
        loss_ref[...] += tot * (0.5 / D)

    row = lambda w: pl.BlockSpec((tm, w), lambda i: (i, 0))
    vec = lambda w: pl.BlockSpec((1, w), lambda i: (0, 0))
    blk = pl.BlockSpec((P, tm, Fb), lambda i: (0, i, 0))
    anyspec = pl.BlockSpec(memory_space=pl.ANY)
    return pl.pallas_call(
        body, grid=(T // tm,), name="ffn_fwd",
        in_specs=[row(D), row(DC), row(D - DC), row(D),
                  pl.BlockSpec((None, 1, N_MOD * D), lambda i: (i // tps, 0, 0)),
                  vec(DC), vec(DC), vec(D), anyspec, anyspec, anyspec, anyspec],
        out_specs=[row(D), row(D), row(D), row(D), blk, blk, blk, row(D), row(D),
                   pl.BlockSpec((8, LANES), lambda i: (0, 0))],
        out_shape=[jax.ShapeDtypeStruct((T, D), F32), jax.ShapeDtypeStruct((T, D), MXU_DTYPE),
                   jax.ShapeDtypeStruct((T, D), ACT_DTYPE), jax.ShapeDtypeStruct((T, D), MXU_DTYPE),
                   jax.ShapeDtypeStruct((P, T, Fb), ACT_DTYPE), jax.ShapeDtypeStruct((P, T, Fb), ACT_DTYPE),
                   jax.ShapeDtypeStruct((P, T, Fb), MXU_DTYPE), jax.ShapeDtypeStruct((T, D), ACT_DTYPE),
                   jax.ShapeDtypeStruct((T, D), F32), jax.ShapeDtypeStruct((8, LANES), F32)],
        scratch_shapes=[pltpu.VMEM(w_out.shape, w_out.dtype), pltpu.VMEM(w_gate.shape, w_gate.dtype),
                        pltpu.VMEM(w_up.shape, w_up.dtype), pltpu.VMEM(w_down.shape, w_down.dtype),
                        pltpu.SemaphoreType.DMA((4,))],
        compiler_params=_cp(("arbitrary",)),
    )(x2, cv, ya, tgt, mod, g_ln, b_ln, g_ffn, w_out, w_gate, w_up, w_down)


def _ffn_bwd(dy, x1, gs, us, fo, mixb, cv, mod, g_ln, b_ln, g_ffn, w_out, w_gate, w_up, w_down, *, S, tm):
    T, D = dy.shape
    DC = cv.shape[1]
    P, Kb, _ = w_out.shape
    Fb = w_down.shape[1]
    tps = S // tm
    Bl = T // S

    def body(dy_ref, x1_ref, g_ref, u_ref, f_ref, mix_ref, cv_ref, mod_ref, gln_ref, bln_ref, gf_ref,
             wo_hbm, wg_hbm, wu_hbm, wd_hbm,
             dg_ref, du_ref, df_ref, dx1_ref, dmix_ref, dya_ref, dcv_ref, macc_ref, gacc_ref, lacc_ref,
             wo, wg, wu, wd, sems):
        i = pl.program_id(0)
        _load_resident(i, [(wo_hbm, wo), (wg_hbm, wg), (wu_hbm, wu), (wd_hbm, wd)], sems)

        @pl.when(i == 0)
        def _():
            gacc_ref[...] = jnp.zeros_like(gacc_ref)
            lacc_ref[...] = jnp.zeros_like(lacc_ref)

        @pl.when(i % tps == 0)
        def _():
            macc_ref[...] = jnp.zeros_like(macc_ref)

        dy_t = dy_ref[...]
        x1 = x1_ref[...]
        gate_f = mod_ref[:, 5 * D:6 * D]
        macc_ref[2:3, :] += _colsum(dy_t * f_ref[...].astype(F32))
        dfb = (dy_t * gate_f).astype(MXU_DTYPE)
        df_ref[...] = dfb
        dh2 = jnp.zeros((tm, D), F32)
        for p in range(P):
            da = _dot_nt(dfb, wd[p])
            g = g_ref[p].astype(F32)
            u = u_ref[p].astype(F32)
            sg = _sigmoid(g)
            dgp = (da * u * (sg * (1.0 + g * (1.0 - sg)))).astype(MXU_DTYPE)
            dup = (da * (g * sg)).astype(MXU_DTYPE)
            dg_ref[p] = dgp
            du_ref[p] = dup
            dh2 = dh2 + _dot(dgp, wg[p]) + _dot(dup, wu[p])
        r2 = lax.rsqrt(jnp.mean(x1 * x1, axis=-1, keepdims=True) + EPS)
        xr = x1 * r2
        n2 = xr * gf_ref[...]
        macc_ref[0:1, :] += _colsum(dh2)
        macc_ref[1:2, :] += _colsum(dh2 * n2)
        dn2 = dh2 * (1.0 + mod_ref[:, 4 * D:5 * D])
        gacc_ref[0:1, :] += _colsum(dn2 * xr)
        e = dn2 * gf_ref[...]
        dx1 = dy_t + r2 * e - xr * (r2 * jnp.mean(e * xr, axis=-1, keepdims=True))
        dx1_ref[...] = dx1
        macc_ref[3:4, :] += _colsum(dx1 * mix_ref[...].astype(F32))
        dmixb = (dx1 * mod_ref[:, 2 * D:3 * D]).astype(MXU_DTYPE)
        dmix_ref[...] = dmixb
        parts = [_dot_nt(dmixb, wo[p]) for p in range(P)]
        dycat = jnp.concatenate(parts, axis=1) if P > 1 else parts[0]
        dya_ref[...] = dycat[:, DC:]
        dyc = dycat[:, :DC]
        z, rs, ln = _layer_norm_parts(cv_ref[...], gln_ref[...], bln_ref[...])
        sg = _sigmoid(ln)
        dln = dyc * (sg * (1.0 + ln * (1.0 - sg)))
        lacc_ref[0:1, :] += _colsum(dln * z)
        lacc_ref[1:2, :] += _colsum(dln)
        dz = dln * gln_ref[...]
        dcv_ref[...] = rs * (dz - jnp.mean(dz, axis=-1, keepdims=True) - z * jnp.mean(dz * z, axis=-1, keepdims=True))

    row = lambda w: pl.BlockSpec((tm, w), lambda i: (i, 0))
    vec = lambda w: pl.BlockSpec((1, w), lambda i: (0, 0))
    blk = pl.BlockSpec((P, tm, Fb), lambda i: (0, i, 0))
    anyspec = pl.BlockSpec(memory_space=pl.ANY)
    return pl.pallas_call(
        body, grid=(T // tm,), name="ffn_bwd",
        in_specs=[row(D), row(D), blk, blk, row(D), row(D), row(DC),
                  pl.BlockSpec((None, 1, N_MOD * D), lambda i: (i // tps, 0, 0)),
                  vec(DC), vec(DC), vec(D), anyspec, anyspec, anyspec, anyspec],
        out_specs=[blk, blk, row(D), row(D), row(D), row(D - DC), row(DC),
                   pl.BlockSpec((None, 8, D), lambda i: (i // tps, 0, 0)),
                   pl.BlockSpec((8, D), lambda i: (0, 0)), pl.BlockSpec((8, DC), lambda i: (0, 0))],
        out_shape=[jax.ShapeDtypeStruct((P, T, Fb), MXU_DTYPE), jax.ShapeDtypeStruct((P, T, Fb), MXU_DTYPE),
                   jax.ShapeDtypeStruct((T, D), MXU_DTYPE), jax.ShapeDtypeStruct((T, D), F32),
                   jax.ShapeDtypeStruct((T, D), MXU_DTYPE), jax.ShapeDtypeStruct((T, D - DC), F32),
                   jax.ShapeDtypeStruct((T, DC), F32), jax.ShapeDtypeStruct((Bl, 8, D), F32),
                   jax.ShapeDtypeStruct((8, D), F32), jax.ShapeDtypeStruct((8, DC), F32)],
        scratch_shapes=[pltpu.VMEM(w_out.shape, w_out.dtype), pltpu.VMEM(w_gate.shape, w_gate.dtype),
                        pltpu.VMEM(w_up.shape, w_up.dtype), pltpu.VMEM(w_down.shape, w_down.dtype),
                        pltpu.SemaphoreType.DMA((4,))],
        compiler_params=_cp(("arbitrary",)),
    )(dy, x1, gs, us, fo, mixb, cv, mod, g_ln, b_ln, g_ffn, w_out, w_gate, w_up, w_down)


def _in_bwd(da, dg, dq, dk, dv, x2, dx1, mod, g_mix, w_in, *, S, tm):
    T, D = x2.shape
    P, _, Nb = w_in.shape
    DC = da.shape[1]
    NS = dq.shape[0]
    n_in = P * Nb
    tps = S // tm
    Bl = T // S

    def body(da_ref, dg_ref, dq_ref, dk_ref, dv_ref, x_ref, dx1_ref, mod_ref, g_ref, w_ref,
             dx_ref, dproj_ref, macc_ref, gacc_ref):
        i = pl.program_id(0)

        @pl.when(i == 0)
        def _():
            gacc_ref[...] = jnp.zeros_like(gacc_ref)

        @pl.when(i % tps == 0)
        def _():
            macc_ref[...] = jnp.zeros_like(macc_ref)

        pieces = [da_ref[...], dg_ref[...]] + [r[j] for r in (dq_ref, dk_ref, dv_ref) for j in range(NS)]
        dproj = jnp.concatenate(pieces, axis=1).astype(MXU_DTYPE)
        dproj_ref[...] = dproj
        dh = _dot_nt(dproj[:, 0:Nb], w_ref[0])
        for p in range(1, P):
            dh = dh + _dot_nt(dproj[:, Nb * p:Nb * (p + 1)], w_ref[p])
        x = x_ref[...]
        r = lax.rsqrt(jnp.mean(x * x, axis=-1, keepdims=True) + EPS)
        xr = x * r
        macc_ref[0:1, :] += _colsum(dh)
        macc_ref[1:2, :] += _colsum(dh * (xr * g_ref[...]))
        dn = dh * (1.0 + mod_ref[:, D:2 * D])
        gacc_ref[0:1, :] += _colsum(dn * xr)
        e = dn * g_ref[...]
        dx_ref[...] = dx1_ref[...] + r * e - xr * (r * jnp.mean(e * xr, axis=-1, keepdims=True))

    row = lambda w: pl.BlockSpec((tm, w), lambda i: (i, 0))
    slab = pl.BlockSpec((NS, tm, LANES), lambda i: (0, i, 0))
    return pl.pallas_call(
        body, grid=(T // tm,), name="in_bwd",
        in_specs=[row(DC), row(DC), slab, slab, slab, row(D), row(D),
                  pl.BlockSpec((None, 1, N_MOD * D), lambda i: (i // tps, 0, 0)),
                  pl.BlockSpec((1, D), lambda i: (0, 0)),
                  pl.BlockSpec((P, D, Nb), lambda i: (0, 0, 0))],
        out_specs=[row(D), row(n_in), pl.BlockSpec((None, 8, D), lambda i: (i // tps, 0, 0)),
                   pl.BlockSpec((8, D), lambda i: (0, 0))],
        out_shape=[jax.ShapeDtypeStruct((T, D), F32), jax.ShapeDtypeStruct((T, n_in), MXU_DTYPE),
                   jax.ShapeDtypeStruct((Bl, 8, D), F32), jax.ShapeDtypeStruct((8, D), F32)],
        compiler_params=_cp(("arbitrary",)),
    )(da, dg, dq, dk, dv, x2, dx1, mod, g_mix, w_in)


def _wgrad(a, b, *, P, name, tk, split=None, host=None):
    a_blk, b_blk = a.ndim == 3, b.ndim == 3
    plan, h_in, h_out = host if host is not None else (None, (), ())
    ni, no = len(h_in), len(h_out)
    T = a.shape[-2]
    if a_blk:
        R, C = a.shape[2], b.shape[1]
        a_of = lambda av, p: av[p]
        b_of = lambda bv, p: bv[...]
    elif b_blk:
        R, C = a.shape[1], b.shape[2]
        a_of = lambda av, p: av[...]
        b_of = lambda bv, p: bv[p]
    elif split == "a":
        R, C = a.shape[1] // P, b.shape[1]
        a_of = lambda av, p: av[:, R * p:R * (p + 1)]
        b_of = lambda bv, p: bv[...]
    else:
        R, C = a.shape[1], b.shape[1] // P
        a_of = lambda av, p: av[...]
        b_of = lambda bv, p: bv[:, C * p:C * (p + 1)]

    n_steps = T // tk

    def body(a_ref, b_ref, *rest):
        hin, o_ref, hout, sems = rest[:ni], rest[ni], rest[ni + 1:ni + 1 + no], rest[ni + 1 + no:]
        step = pl.program_id(0)

        @pl.when(step == 0)
        def _():
            o_ref[...] = jnp.zeros_like(o_ref)
            if plan is not None:
                plan.start(hin, hout, sems)

        if plan is not None:
            @pl.when(step == n_steps // 2)
            def _():
                plan.forward(hin, hout, sems)

        for p in range(P):
            o_ref[p] += _dot_tn(a_of(a_ref, p), b_of(b_ref, p))

        if plan is not None:
            @pl.when(step == n_steps - 1)
            def _():
                plan.finish(hin, hout, sems)

    def spec(v):
        if v.ndim == 3:
            return pl.BlockSpec((P, tk, v.shape[2]), lambda k: (0, k, 0))
        return pl.BlockSpec((tk, v.shape[1]), lambda k: (k, 0))

    anyspec = pl.BlockSpec(memory_space=pl.ANY)
    res = pl.pallas_call(
        body, grid=(n_steps,), name=name,
        in_specs=[spec(a), spec(b)] + [anyspec] * ni,
        out_specs=[pl.BlockSpec((P, R, C), lambda k: (0, 0, 0))] + [anyspec] * no,
        out_shape=[jax.ShapeDtypeStruct((P, R, C), F32)] + list(h_out),
        scratch_shapes=plan.scratch() if plan is not None else [],
        compiler_params=_cp(("arbitrary",)),
    )(a, b, *h_in)
    return res if plan is not None else res[0]


TM_IN = 512
TM_FFN = 256
TK_WGRAD = 512


def _alibi_slabs(n_slab):
    heads = 2 * n_slab
    slopes = 2.0 ** (-8.0 * np.arange(1, heads + 1) / heads)
    return jnp.asarray(np.broadcast_to(np.repeat(slopes.reshape(n_slab, 1, 2), HEAD_DIM, axis=2), (n_slab, 8, LANES)),
                       dtype=F32)


def _local_step(x, tgt, mod, g_mix, wdw, g_ln, b_ln, g_q, g_k, g_ffn, w_in, w_out, w_gate, w_up, w_down,
                pc_idx=None):
    Bl, S, D = x.shape
    T = Bl * S
    DC = g_ln.shape[1]
    P = w_in.shape[0]
    n_slab = (D - DC) // LANES
    x2 = x.reshape(T, D)
    t2 = tgt.reshape(T, D)
    mod3 = mod.reshape(Bl, 1, N_MOD * D)
    gq2 = jnp.tile(g_q, (1, LANES // HEAD_DIM))
    gk2 = jnp.tile(g_k, (1, LANES // HEAD_DIM))
    slopes = _alibi_slabs(n_slab)

    ag, qkv, qkh, h1 = _fwd_in(x2, mod3, g_mix, gq2, gk2, w_in, S=S, tm=TM_IN, n_ag=2 * DC)
    cv = _conv_fwd(ag, wdw, Bl=Bl, S=S, DC=DC)
    if pc_idx is not None:
        ya, lse, w_out, w_gate, w_up, w_down = _attn_fwd(qkh, qkv, slopes, Bl=Bl, S=S,
                                                         hosted=(w_out, w_gate, w_up, w_down))
    else:
        ya, lse = _attn_fwd(qkh, qkv, slopes, Bl=Bl, S=S)
    x1, ycat, mixb, h2, gs, us, acts, fo, dy, lossb = _ffn_fwd(
        x2, cv, ya, t2, mod3, g_ln, b_ln, g_ffn, w_out, w_gate, w_up, w_down, S=S, tm=TM_FFN)
    dgs, dus, dfb, dx1, dmixb, dya, dcv, macc_f, gacc_f, lacc = _ffn_bwd(
        dy, x1, gs, us, fo, mixb, cv, mod3, g_ln, b_ln, g_ffn, w_out, w_gate, w_up, w_down, S=S, tm=TM_FFN)
    wg = functools.partial(_wgrad, P=P, tk=TK_WGRAD)
    out = {}
    if pc_idx is None:
        grads = dict(w_down=wg(acts, dfb, name="wgrad_down"), w_gate=wg(dgs, h2, name="wgrad_gate"),
                     w_up=wg(dus, h2, name="wgrad_up"), w_out=wg(ycat, dmixb, name="wgrad_out", split="a"))
        dq, dk, dv, gqk = _attn_bwd(qkh, qkv, ya, lse, dya, gq2, gk2, slopes, Bl=Bl, S=S)
    else:
        g_down = wg(acts, dfb, name="wgrad_down")
        g_gate, r_down = wg(dgs, h2, name="wgrad_gate", host=_sibling_host([g_down]))
        g_up, r_gate = wg(dus, h2, name="wgrad_up", host=_sibling_host([g_gate]))
        g_out, r_up = wg(ycat, dmixb, name="wgrad_out", split="a", host=_sibling_host([g_up]))
        (r_out,) = _rs_sibling([g_out], "rs_sibling_out")
        grads = dict(w_down=g_down, w_gate=g_gate, w_up=g_up, w_out=g_out)
        sums = [_pair_add(grads[nm], r, pc_idx, "pair_add_" + nm)
                for nm, r in zip(EARLY_WEIGHTS, (r_down, r_gate, r_up, r_out))]
        res = _attn_bwd(qkh, qkv, ya, lse, dya, gq2, gk2, slopes, Bl=Bl, S=S, hosted=tuple(sb for _, sb in sums))
        dq, dk, dv, gqk = res[:4]
        out["early_sums"] = [s32 for s32, _ in sums]
        out["early_recv"] = list(res[4:])
    da, dg, dwdw = _conv_bwd(ag, dcv, wdw, Bl=Bl, S=S, DC=DC)
    dx, dprojb, macc_m, gacc_m = _in_bwd(da, dg, dq, dk, dv, x2, dx1, mod3, g_mix, w_in, S=S, tm=TM_IN)
    packed = _pack_small(macc_m, macc_f, gacc_m, gacc_f, lacc, gqk, dwdw, lossb)
    if pc_idx is None:
        grads["w_in"] = wg(h1, dprojb, name="wgrad_in", split="b")
    else:
        grads["w_in"], out["gathered_small"] = wg(h1, dprojb, name="wgrad_in", split="b",
                                                  host=_small_gather_host(packed))
    out.update(dx=dx.reshape(Bl, S, D), grads=grads, packed=packed)
    return out


EARLY_WEIGHTS = ("w_down", "w_gate", "w_up", "w_out")


def _small_layout(Bl):
    return 8 * Bl, 8 * Bl + 8, 8 * Bl + 8 + CONV_ROWS


def _pack_small(macc_m, macc_f, gacc_m, gacc_f, lacc, gqk, dwdw, lossb):
    Bl, _, D = macc_m.shape
    DC = lacc.shape[1]
    assert 2 * DC <= D
    SMALL_GAIN_ROW, SMALL_TAP_ROW, SMALL_ROWS = _small_layout(Bl)

    def body(mm_ref, mf_ref, gm_ref, gf_ref, la_ref, qk_ref, dw_ref, loss_ref, o_ref):
        o_ref[...] = jnp.zeros_like(o_ref)
        for b in range(Bl):
            o_ref[8 * b + 0:8 * b + 2, :] = mm_ref[b, 0:2, :]
            o_ref[8 * b + 2:8 * b + 3, :] = mf_ref[b, 3:4, :]
            o_ref[8 * b + 3:8 * b + 6, :] = mf_ref[b, 0:3, :]
        r = SMALL_GAIN_ROW
        o_ref[r:r + 1, :] = gm_ref[0:1, :]
        o_ref[r + 1:r + 2, :] = gf_ref[0:1, :]
        o_ref[r + 2:r + 3, 0:DC] = la_ref[0:1, :]
        o_ref[r + 2:r + 3, DC:2 * DC] = la_ref[1:2, :]
        qk = qk_ref[0:2, 0:HEAD_DIM] + qk_ref[0:2, HEAD_DIM:2 * HEAD_DIM]
        o_ref[r + 3:r + 4, 0:HEAD_DIM] = qk[0:1, :]
        o_ref[r + 3:r + 4, HEAD_DIM:2 * HEAD_DIM] = qk[1:2, :]
        o_ref[r + 4:r + 5, 0:LANES] = loss_ref[0:1, :]
        o_ref[SMALL_TAP_ROW:SMALL_TAP_ROW + CONV_ROWS, 0:DC] = dw_ref[...]

    return pl.pallas_call(body, name="pack_small", out_shape=jax.ShapeDtypeStruct((SMALL_ROWS, D), F32),
                          compiler_params=_cp())(macc_m, macc_f, gacc_m, gacc_f, lacc, gqk, dwdw, lossb)


def _row_tile(rows, cap=512):
    if rows <= cap:
        return rows
    best = rows
    for t in range(8, cap + 1, 8):
        if rows % t == 0:
            best = t
    return best


def _cast_weight(w, pidx, name):
    def body(p_ref, w_ref, o_ref):
        o_ref[...] = w_ref[...].astype(MXU_DTYPE)
    R, C = w.shape
    tr = _row_tile(R)
    return pl.pallas_call(
        body, name=name,
        grid_spec=pltpu.PrefetchScalarGridSpec(
            num_scalar_prefetch=1, grid=(R // tr,),
            in_specs=[pl.BlockSpec((tr, C), lambda i, p: (i, 0))],
            out_specs=pl.BlockSpec((None, tr, C), lambda i, p: (p[0], i, 0))),
        out_shape=jax.ShapeDtypeStruct((4, R, C), MXU_DTYPE),
    )(pidx, w)


def _pair_add(g, recv, pc_idx, name):
    P, R, C = g.shape
    R2 = R // 2

    def body(pc_ref, g_ref, r_ref, o_ref, ob_ref):
        s = g_ref[...] + r_ref[...]
        ob_ref[...] = s.astype(jnp.bfloat16)

        @pl.when(pl.program_id(0) == pc_ref[0])
        def _():
            o_ref[...] = s

    return pl.pallas_call(
        body, name=name,
        grid_spec=pltpu.PrefetchScalarGridSpec(
            num_scalar_prefetch=1, grid=(P,),
            in_specs=[pl.BlockSpec((None, R2, C), lambda p, pc: (p, pc[1], 0)),
                      pl.BlockSpec((None, R2, C), lambda p, pc: (p, 0, 0))],
            out_specs=[pl.BlockSpec((R2, C), lambda p, pc: (0, 0)),
                       pl.BlockSpec((None, R2, C), lambda p, pc: (p, 0, 0))]),
        out_shape=[jax.ShapeDtypeStruct((R2, C), F32), jax.ShapeDtypeStruct((P, R2, C), jnp.bfloat16)],
    )(pc_idx, g, recv)


def _final_add(own, recv, pc_idx, name):
    R2, C = own.shape

    def body(pc_ref, s_ref, r_ref, o_ref):
        acc = s_ref[...]
        for k in range(3):
            acc = acc + r_ref[k].astype(F32)
        o_ref[...] = acc

    return pl.pallas_call(
        body, name=name,
        grid_spec=pltpu.PrefetchScalarGridSpec(
            num_scalar_prefetch=1, grid=(1,),
            in_specs=[pl.BlockSpec((R2, C), lambda i, pc: (0, 0)),
                      pl.BlockSpec((3, R2, C), lambda i, pc: (0, 0, 0))],
            out_specs=pl.BlockSpec((R2, C), lambda i, pc: (pc[1], 0))),
        out_shape=jax.ShapeDtypeStruct((2 * R2, C), F32),
    )(pc_idx, own, recv)


def _adamw_update(w_ref, g_ref, m_ref, v_ref, d_ref, nm_ref, nv_ref):
    c1 = 1.0 - ADAM_B1 ** ADAM_STEP
    c2 = 1.0 - ADAM_B2 ** ADAM_STEP
    gg = g_ref[...]
    nm = ADAM_B1 * m_ref[...] + (1.0 - ADAM_B1) * gg
    nv = ADAM_B2 * v_ref[...] + (1.0 - ADAM_B2) * (gg * gg)
    nm_ref[...] = nm
    nv_ref[...] = nv
    d_ref[...] = -ADAM_LR * ((nm / c1) / (jnp.sqrt(nv / c2) + ADAM_EPS) + ADAM_WD * w_ref[...])


def _adamw(w, g, m, v, name):
    R, C = w.shape
    tr = _row_tile(R, 256)
    spec = pl.BlockSpec((tr, C), lambda i: (i, 0))
    return pl.pallas_call(
        functools.partial(_adamw_update), grid=(R // tr,), name=name,
        in_specs=[spec] * 4, out_specs=[spec] * 3,
        out_shape=[jax.ShapeDtypeStruct((R, C), F32)] * 3,
    )(w, g, m, v)


def _startup(first, w_ada, b_cols, w_in_buf, *, Bl):
    rows, D = first.shape
    NA = w_ada.shape[1]
    n_dev = 8
    g_w = _WeightGather([w_in_buf.shape])
    g_c = _SmallGather(rows)
    g_m = _SmallGather(n_dev * Bl)

    def body(first_ref, wada_ref, b_ref, win_in, g0_ref, call_ref, gm_ref, win_out, modp,
             ws0, ws1, cs0, cs1, cs2, ms0, ms1, ms2):
        g_w.start([win_out], (ws0, ws1))
        for phase in (g_c.start, g_c.forward, g_c.finish):
            phase([first_ref], [g0_ref], (cs0, cs1, cs2))
        for d in range(n_dev):
            call_ref[Bl * d:Bl * (d + 1), :] = g0_ref[rows * d:rows * d + Bl, :]
        c = call_ref[...]
        modp[...] = jnp.dot(c * _sigmoid(c), wada_ref[...], preferred_element_type=F32,
                            precision=lax.Precision.HIGH) + b_ref[...]
        for phase in (g_m.start, g_m.forward, g_m.finish):
            phase([modp], [gm_ref], (ms0, ms1, ms2))
        g_w.forward([win_out], (ws0, ws1))
        g_w.finish([win_out], (ws0, ws1))

    vmem = pl.BlockSpec(memory_space=pltpu.VMEM)
    anyspec = pl.BlockSpec(memory_space=pl.ANY)
    return pl.pallas_call(
        body, name="startup",
        in_specs=[vmem, vmem, vmem, anyspec], out_specs=[vmem, vmem, vmem, anyspec],
        out_shape=[jax.ShapeDtypeStruct((n_dev * rows, D), F32), jax.ShapeDtypeStruct((n_dev * Bl, D), F32),
                   jax.ShapeDtypeStruct((n_dev * n_dev * Bl, NA), F32),
                   jax.ShapeDtypeStruct(w_in_buf.shape, w_in_buf.dtype)],
        input_output_aliases={3: 3},
        scratch_shapes=[pltpu.VMEM((n_dev * Bl, NA), F32)] + g_w.scratch() + g_c.scratch() + g_m.scratch(),
        compiler_params=_cp(),
    )(first, w_ada, b_cols, w_in_buf)


def _ada_bwd(c_all, dmod_cols):
    def body(c_ref, d_ref, o_ref):
        c = c_ref[...]
        o_ref[...] = _dot_tn((c * _sigmoid(c)).astype(MXU_DTYPE), d_ref[...].astype(MXU_DTYPE))
    return pl.pallas_call(
        body, name="ada_bwd", out_shape=jax.ShapeDtypeStruct((c_all.shape[1], dmod_cols.shape[1]), F32),
        compiler_params=_cp(),
    )(c_all, dmod_cols)


def _small_reduce(gathered, n_dev, Bl):
    mod_rows, _, rows = _small_layout(Bl)
    width = gathered.shape[1]

    def body(g_ref, red_ref, bada_ref):
        acc = g_ref[0:rows, :]
        for d in range(1, n_dev):
            acc = acc + g_ref[d * rows:(d + 1) * rows, :]
        red_ref[...] = acc[mod_rows:, :]
        b = acc[0:8, :]
        for q in range(1, Bl):
            b = b + acc[8 * q:8 * q + 8, :]
        bada_ref[...] = b
    return pl.pallas_call(
        body, name="small_reduce",
        out_shape=[jax.ShapeDtypeStruct((rows - mod_rows, width), F32), jax.ShapeDtypeStruct((8, width), F32)],
        compiler_params=_cp(),
    )(gathered)


def _mesh_pos():
    return lax.axis_index("x"), lax.axis_index("y"), lax.axis_index("c")


def _other_chips(x, y):
    return [(1 - x, y), (x, 1 - y), (1 - x, 1 - y)]


class _WeightGather:
    def __init__(self, shapes):
        self.shapes = shapes
        self.n = len(shapes)

    def scratch(self):
        return [pltpu.SemaphoreType.DMA((6 * self.n,)), pltpu.SemaphoreType.DMA((6 * self.n,))]

    def _copy(self, outs, sems, w, k, slot, h, to):
        r2 = self.shapes[w][1] // 2
        blk = outs[w].at[slot, pl.ds(h * r2, r2), :]
        return pltpu.make_async_remote_copy(
            src_ref=blk, dst_ref=blk, send_sem=sems[0].at[6 * w + k], recv_sem=sems[1].at[6 * w + k],
            device_id=to, device_id_type=MESH_DEV)

    def start(self, outs, sems):
        x, y, c = _mesh_pos()
        for w in range(self.n):
            for k, chip in enumerate(_other_chips(x, y)):
                self._copy(outs, sems, w, k, 2 * x + y, c, (*chip, c)).start()

    def forward(self, outs, sems):
        x, y, c = _mesh_pos()
        for w in range(self.n):
            for k, chip in enumerate(_other_chips(x, y)):
                slot = 2 * chip[0] + chip[1]
                self._copy(outs, sems, w, k, slot, c, (x, y, 1 - c)).wait_recv()
                self._copy(outs, sems, w, 3 + k, slot, c, (x, y, 1 - c)).start()

    def finish(self, outs, sems):
        x, y, c = _mesh_pos()
        for w in range(self.n):
            for k, chip in enumerate(_other_chips(x, y)):
                slot = 2 * chip[0] + chip[1]
                self._copy(outs, sems, w, 3 + k, slot, 1 - c, (x, y, 1 - c)).wait_recv()
                self._copy(outs, sems, w, k, 2 * x + y, c, (*chip, c)).wait_send()
                self._copy(outs, sems, w, 3 + k, slot, c, (x, y, 1 - c)).wait_send()


class _SiblingExchange:
    def __init__(self, shapes):
        self.shapes = shapes

    def scratch(self):
        n = sum(s[0] for s in self.shapes)
        return [pltpu.SemaphoreType.DMA((n,)), pltpu.SemaphoreType.DMA((n,))]

    def out_shapes(self, dtype):
        return [jax.ShapeDtypeStruct((s[0], s[1] // 2, s[2]), dtype) for s in self.shapes]

    def _copies(self, ins, outs, sems):
        x, y, c = _mesh_pos()
        cps, k = [], 0
        for w, (P, R, _) in enumerate(self.shapes):
            r2 = R // 2
            for p in range(P):
                cps.append(pltpu.make_async_remote_copy(
                    src_ref=ins[w].at[p, pl.ds((1 - c) * r2, r2), :], dst_ref=outs[w].at[p],
                    send_sem=sems[0].at[k], recv_sem=sems[1].at[k],
                    device_id=(x, y, 1 - c), device_id_type=MESH_DEV))
                k += 1
        return cps

    def start(self, ins, outs, sems):
        for cp in self._copies(ins, outs, sems):
            cp.start()

    def forward(self, ins, outs, sems):
        pass

    def finish(self, ins, outs, sems):
        for cp in self._copies(ins, outs, sems):
            cp.wait()


def _sibling_host(grads):
    plan = _SiblingExchange([g.shape for g in grads])
    return plan, tuple(grads), tuple(plan.out_shapes(grads[0].dtype))


def _rs_sibling(grads, name):
    n = len(grads)
    plan, _, out_shapes = _sibling_host(grads)

    def body(*refs):
        ins, outs, sems = refs[:n], refs[n:2 * n], refs[2 * n:]
        plan.start(ins, outs, sems)
        plan.finish(ins, outs, sems)

    anyspec = pl.BlockSpec(memory_space=pl.ANY)
    return pl.pallas_call(
        body, name=name, out_shape=list(out_shapes),
        in_specs=[anyspec] * n, out_specs=[anyspec] * n, scratch_shapes=plan.scratch(),
    )(*grads)


class _SmallGather:
    def __init__(self, m_per):
        self.m = m_per

    def scratch(self):
        return [pltpu.SemaphoreType.DMA((7,)), pltpu.SemaphoreType.DMA((7,)), pltpu.SemaphoreType.DMA]

    def _rows(self, out, pos):
        px, py, pc = pos
        return out.at[pl.ds((4 * px + 2 * py + pc) * self.m, self.m), :]

    def _copy(self, out, sems, k, block, to, src=None):
        dst = self._rows(out, block)
        return pltpu.make_async_remote_copy(
            src_ref=dst if src is None else src, dst_ref=dst, send_sem=sems[0].at[k], recv_sem=sems[1].at[k],
            device_id=to, device_id_type=MESH_DEV)

    def start(self, ins, outs, sems):
        x, y, c = _mesh_pos()
        me = (x, y, c)
        pltpu.make_async_copy(ins[0], self._rows(outs[0], me), sems[2]).start()
        self._copy(outs[0], sems, 0, me, (x, y, 1 - c), src=ins[0]).start()
        for j, chip in enumerate(_other_chips(x, y)):
            self._copy(outs[0], sems, 1 + j, me, (*chip, c), src=ins[0]).start()

    def forward(self, ins, outs, sems):
        x, y, c = _mesh_pos()
        for j, chip in enumerate(_other_chips(x, y)):
            self._copy(outs[0], sems, 1 + j, (*chip, c), (x, y, c)).wait_recv()
            self._copy(outs[0], sems, 4 + j, (*chip, c), (x, y, 1 - c)).start()

    def finish(self, ins, outs, sems):
        x, y, c = _mesh_pos()
        me = (x, y, c)
        self._copy(outs[0], sems, 0, (x, y, 1 - c), me).wait_recv()
        for j, chip in enumerate(_other_chips(x, y)):
            self._copy(outs[0], sems, 4 + j, (*chip, 1 - c), me).wait_recv()
        self._copy(outs[0], sems, 0, me, (x, y, 1 - c), src=ins[0]).wait_send()
        for j, chip in enumerate(_other_chips(x, y)):
            self._copy(outs[0], sems, 1 + j, me, (*chip, c), src=ins[0]).wait_send()
            self._copy(outs[0], sems, 4 + j, (*chip, c), (x, y, 1 - c)).wait_send()
        pltpu.make_async_copy(ins[0], self._rows(outs[0], me), sems[2]).wait()


def _small_gather_host(packed):
    m, n = packed.shape
    return _SmallGather(m), (packed,), (jax.ShapeDtypeStruct((8 * m, n), packed.dtype),)


class _ChipExchange:
    def __init__(self, n):
        self.n = n

    def scratch(self):
        return [pltpu.SemaphoreType.DMA((3 * self.n,)), pltpu.SemaphoreType.DMA((3 * self.n,))]

    def _copies(self, ins, outs, sems):
        x, y, c = _mesh_pos()
        return [pltpu.make_async_remote_copy(
            src_ref=ins[w].at[2 * chip[0] + chip[1]], dst_ref=outs[w].at[k],
            send_sem=sems[0].at[3 * w + k], recv_sem=sems[1].at[3 * w + k],
            device_id=(*chip, c), device_id_type=MESH_DEV)
            for w in range(self.n) for k, chip in enumerate(_other_chips(x, y))]

    def start(self, ins, outs, sems):
        for cp in self._copies(ins, outs, sems):
            cp.start()

    def forward(self, ins, outs, sems):
        pass

    def finish(self, ins, outs, sems):
        for cp in self._copies(ins, outs, sems):
            cp.wait()


def _rs_final(bufs, name, chips=()):
    n, nc = len(bufs), len(chips)
    plan = _ChipExchange(nc)

    def body(*refs):
        cin = refs[n:n + nc]
        outs = refs[n + nc:2 * n + nc]
        cout = refs[2 * n + nc:2 * n + 2 * nc]
        send_sems, recv_sems = refs[2 * n + 2 * nc:2 * n + 2 * nc + 2]
        csems = refs[2 * n + 2 * nc + 2:]
        x, y, c = _mesh_pos()
        if nc:
            plan.start(cin, cout, csems)
        cps = []
        for w in range(n):
            r2 = bufs[w].shape[0] // 2
            mine = outs[w].at[pl.ds(c * r2, r2), :]
            cps.append(pltpu.make_async_remote_copy(
                src_ref=mine, dst_ref=mine, send_sem=send_sems.at[w], recv_sem=recv_sems.at[w],
                device_id=(x, y, 1 - c), device_id_type=MESH_DEV))
            cps[-1].start()
        for cp in cps:
            cp.wait()
        if nc:
            plan.finish(cin, cout, csems)

    anyspec = pl.BlockSpec(memory_space=pl.ANY)
    return pl.pallas_call(
        body, name=name,
        out_shape=[jax.ShapeDtypeStruct(b.shape, b.dtype) for b in bufs]
                  + [jax.ShapeDtypeStruct((3,) + s.shape[1:], s.dtype) for s in chips],
        in_specs=[anyspec] * (n + nc), out_specs=[anyspec] * (n + nc),
        input_output_aliases={w: w for w in range(n)},
        scratch_shapes=[pltpu.SemaphoreType.DMA((n,)), pltpu.SemaphoreType.DMA((n,))] + (plan.scratch() if nc else []),
    )(*bufs, *chips)


BIG = ("w_in", "w_out", "w_gate", "w_up", "w_down")
TRANSPOSED = ("w_gate", "w_up")
WEIGHTS = ("w_ada", "b_ada", "g_mix", "w_in", "w_dw", "b_dw", "g_conv_ln", "b_conv_ln", "g_q", "g_k",
           "w_out", "g_ffn", "w_gate", "w_up", "w_down")


def _pad_to(a, rows, cols):
    return jnp.pad(a, ((0, rows - a.shape[0]), (0, cols - a.shape[1])))


def kernel(x, c, w_ada, b_ada, g_mix, w_in, w_dw, b_dw, g_conv_ln, b_conv_ln, g_q, g_k, w_out, g_ffn, w_gate, w_up, w_down, loss_target, m_w_ada, m_b_ada, m_g_mix, m_w_in, m_w_dw, m_b_dw, m_g_conv_ln, m_b_conv_ln, m_g_q, m_g_k, m_w_out, m_g_ffn, m_w_gate, m_w_up, m_w_down, v_w_ada, v_b_ada, v_g_mix, v_w_in, v_w_dw, v_b_dw, v_g_conv_ln, v_b_conv_ln, v_g_q, v_g_k, v_w_out, v_g_ffn, v_w_gate, v_w_up, v_w_down):
    w = dict(w_ada=w_ada, b_ada=b_ada, g_mix=g_mix, w_in=w_in, w_dw=w_dw, b_dw=b_dw, g_conv_ln=g_conv_ln,
             b_conv_ln=b_conv_ln, g_q=g_q, g_k=g_k, w_out=w_out, g_ffn=g_ffn, w_gate=w_gate, w_up=w_up, w_down=w_down)
    m = dict(w_ada=m_w_ada, b_ada=m_b_ada, g_mix=m_g_mix, w_in=m_w_in, w_dw=m_w_dw, b_dw=m_b_dw, g_conv_ln=m_g_conv_ln,
             b_conv_ln=m_b_conv_ln, g_q=m_g_q, g_k=m_g_k, w_out=m_w_out, g_ffn=m_g_ffn, w_gate=m_w_gate, w_up=m_w_up,
             w_down=m_w_down)
    v = dict(w_ada=v_w_ada, b_ada=v_b_ada, g_mix=v_g_mix, w_in=v_w_in, w_dw=v_w_dw, b_dw=v_b_dw, g_conv_ln=v_g_conv_ln,
             b_conv_ln=v_b_conv_ln, g_q=v_g_q, g_k=v_g_k, w_out=v_w_out, g_ffn=v_g_ffn, w_gate=v_w_gate, w_up=v_w_up,
             w_down=v_w_down)
    Bl, S, D = x.shape
    DC = g_conv_ln.shape[1]
    NA = w_ada.shape[2]
    xi, yi, ci = _mesh_pos()
    p = 2 * xi + yi
    dev = 2 * p + ci
    n_dev = 8
    pidx = jnp.reshape(p, (1,)).astype(jnp.int32)
    pc_idx = jnp.stack([p, ci]).astype(jnp.int32)

    first = jnp.concatenate([_pad_to(c, 8, D), _pad_to(w_dw[0], CONV_ROWS, D)], axis=0)
    shard = lambda a, nm: a[0].T if nm in TRANSPOSED else a[0]
    owned = {nm: _cast_weight(shard(w[nm], nm), pidx, "cast_" + nm) for nm in BIG}
    b_cols = lax.dynamic_slice_in_dim(b_ada, p * NA, NA, axis=1)
    g0, c_all, gm, w_in_full = _startup(first, w_ada[0], b_cols, owned["w_in"], Bl=Bl)
    g0 = g0.reshape(n_dev, 8 + CONV_ROWS, D)
    taps = jnp.concatenate([g0[2 * q, 8:, :w_dw.shape[2]] for q in range(4)], axis=1)
    wdw = jnp.where(lax.broadcasted_iota(jnp.int32, taps.shape, 0) == CONV_WIDTH, b_dw, taps)
    gm = gm.reshape(n_dev, n_dev * Bl, NA)
    mod = jnp.concatenate([lax.dynamic_slice_in_dim(gm[2 * q], dev * Bl, Bl, axis=0) for q in range(4)], axis=1)

    loc = _local_step(x, loss_target, mod, g_mix, wdw, g_conv_ln, b_conv_ln, g_q, g_k, g_ffn,
                      w_in_full, owned["w_out"], owned["w_gate"], owned["w_up"], owned["w_down"], pc_idx=pc_idx)

    halves = [_final_add(s32, r, pc_idx, "final_add_" + nm)
              for nm, s32, r in zip(EARLY_WEIGHTS, loc["early_sums"], loc["early_recv"])]
    (late_sib,) = _rs_sibling([loc["grads"]["w_in"]], "rs_sibling_in")
    late32, late16 = _pair_add(loc["grads"]["w_in"], late_sib, pc_idx, "pair_add_w_in")
    *early_full, late_recv = _rs_final(halves, "rs_final_early", chips=(late16,))
    grad = dict(zip(EARLY_WEIGHTS, early_full))
    (grad["w_in"],) = _rs_final([_final_add(late32, late_recv, pc_idx, "final_add_w_in")], "rs_final_in")

    mod_rows, _, small_rows = _small_layout(Bl)
    gs = loc["gathered_small"]
    red, bada8 = _small_reduce(gs, n_dev, Bl)
    dmod_all = gs.reshape(n_dev, small_rows, D)[:, :mod_rows].reshape(n_dev * Bl, 8, D)[:, :N_MOD].reshape(n_dev * Bl, N_MOD * D)
    grad["w_ada"] = _ada_bwd(c_all, lax.dynamic_slice_in_dim(dmod_all, p * NA, NA, axis=1))
    grad["b_ada"] = bada8[:N_MOD].reshape(1, N_MOD * D)
    grad["g_mix"] = red[0:1]
    grad["g_ffn"] = red[1:2]
    grad["g_conv_ln"] = red[2:3, :DC]
    grad["b_conv_ln"] = red[2:3, DC:2 * DC]
    grad["g_q"] = red[3:4, :HEAD_DIM]
    grad["g_k"] = red[3:4, HEAD_DIM:2 * HEAD_DIM]
    loss = red[4, 0]
    dwdw = red[8:8 + CONV_ROWS, :DC]
    grad["w_dw"] = lax.dynamic_slice_in_dim(dwdw[:CONV_WIDTH], p * w_dw.shape[2], w_dw.shape[2], axis=1)
    grad["b_dw"] = dwdw[CONV_WIDTH:CONV_WIDTH + 1]

    delta, new_m, new_v = {}, {}, {}
    for nm in WEIGHTS:
        shp = w[nm].shape
        if nm in TRANSPOSED:
            d_, m_, v_ = _adamw(w[nm][0].T, grad[nm], m[nm][0].T, v[nm][0].T, "adamw_" + nm)
            grad[nm], delta[nm], new_m[nm], new_v[nm] = (a.T.reshape(shp) for a in (grad[nm], d_, m_, v_))
            continue
        two_d = (shp[-2], shp[-1]) if len(shp) == 3 else shp
        d_, m_, v_ = _adamw(w[nm].reshape(two_d), grad[nm].reshape(two_d), m[nm].reshape(two_d), v[nm].reshape(two_d),
                            "adamw_" + nm)
        grad[nm] = grad[nm].reshape(shp)
        delta[nm], new_m[nm], new_v[nm] = d_.reshape(shp), m_.reshape(shp), v_.reshape(shp)

    return (loss, loc["dx"], *[grad[nm] for nm in WEIGHTS], *[delta[nm] for nm in WEIGHTS],
            *[new_m[nm] for nm in WEIGHTS], *[new_v[nm] for nm in WEIGHTS])
```

```python
import functools

import jax
import jax.numpy as jnp
import numpy as np
from jax import lax
from jax.experimental import pallas as pl
from jax.experimental.pallas import tpu as pltpu

F32 = jnp.float32
MXU_DTYPE = jnp.bfloat16
ACT_DTYPE = jnp.bfloat16
EPS = 1e-6
NEG_INF = -1e30
HEAD_DIM = 64
LANES = 128
RADIUS = 64
QBLK = 128
DILATIONS = (1, 4, 16)
CONV_WIDTH = 31
CONV_PAD = CONV_WIDTH // 2
CONV_ROWS = 32
N_MOD = 6
ADAM_LR, ADAM_B1, ADAM_B2, ADAM_EPS, ADAM_WD, ADAM_STEP = 0.001, 0.9, 0.999, 1e-08, 0.01, 10
MESH_DEV = pl.DeviceIdType.MESH
VMEM_LIMIT = 56 << 20
ATTN_BWD_VMEM = 60 << 20


def _cp(sem=None, vmem=VMEM_LIMIT):
    kw = dict(vmem_limit_bytes=vmem)
    if sem is not None:
        kw["dimension_semantics"] = sem
    return pltpu.CompilerParams(**kw)


def _sigmoid(x):
    return 1.0 / (1.0 + jnp.exp(-x))


def _dot(a, b):
    return jnp.dot(a, b, preferred_element_type=F32)


def _dot_nt(a, b):
    return lax.dot_general(a, b, (((1,), (1,)), ((), ())), preferred_element_type=F32)


def _dot_tn(a, b):
    return lax.dot_general(a, b, (((0,), (0,)), ((), ())), preferred_element_type=F32)


def _colsum(v):
    return jnp.sum(v, axis=0, keepdims=True)


def _load_resident(i, pairs, sems):
    @pl.when(i == 0)
    def _():
        cps = [pltpu.make_async_copy(src, dst, sems.at[n]) for n, (src, dst) in enumerate(pairs)]
        for c in cps:
            c.start()
        for c in cps:
            c.wait()


def _fwd_in(x2, mod, g_mix, gq2, gk2, w_in, *, S, tm, n_ag):
    T, D = x2.shape
    P, _, Nb = w_in.shape
    n_in = P * Nb
    n_slab = (n_in - n_ag) // LANES
    NS = n_slab // 3
    tps = S // tm

    def body(x_ref, mod_ref, g_ref, gq_ref, gk_ref, w_ref, ag_ref, qkv_ref, qkh_ref, h_ref):
        x = x_ref[...]
        r = lax.rsqrt(jnp.mean(x * x, axis=-1, keepdims=True) + EPS)
        n = x * r * g_ref[...]
        h = n * (1.0 + mod_ref[:, D:2 * D]) + mod_ref[:, 0:D]
        hb = h.astype(MXU_DTYPE)
        h_ref[...] = hb
        parts = [_dot(hb, w_ref[p]) for p in range(P)]
        proj = jnp.concatenate(parts, axis=1) if P > 1 else parts[0]
        ag_ref[...] = proj[:, :n_ag]
        mm = _head_mean_matrix()
        for j in range(n_slab):
            v = proj[:, n_ag + LANES * j:n_ag + LANES * (j + 1)]
            qkv_ref[j] = v
            if j < 2 * NS:
                gain = gq_ref[...] * (HEAD_DIM ** -0.5 * LOG2E) if j < NS else gk_ref[...]
                qkh_ref[j] = v * lax.rsqrt(_head_mean(v * v, mm) + EPS) * gain

    return pl.pallas_call(
        body, grid=(T // tm,), name="fwd_in",
        in_specs=[pl.BlockSpec((tm, D), lambda i: (i, 0)),
                  pl.BlockSpec((None, 1, N_MOD * D), lambda i: (i // tps, 0, 0)),
                  pl.BlockSpec((1, D), lambda i: (0, 0)),
                  pl.BlockSpec((1, LANES), lambda i: (0, 0)), pl.BlockSpec((1, LANES), lambda i: (0, 0)),
                  pl.BlockSpec((P, D, Nb), lambda i: (0, 0, 0))],
        out_specs=[pl.BlockSpec((tm, n_ag), lambda i: (i, 0)),
                   pl.BlockSpec((n_slab, tm, LANES), lambda i: (0, i, 0)),
                   pl.BlockSpec((2 * NS, tm, LANES), lambda i: (0, i, 0)),
                   pl.BlockSpec((tm, D), lambda i: (i, 0))],
        out_shape=[jax.ShapeDtypeStruct((T, n_ag), F32),
                   jax.ShapeDtypeStruct((n_slab, T, LANES), F32),
                   jax.ShapeDtypeStruct((2 * NS, T, LANES), F32),
                   jax.ShapeDtypeStruct((T, D), MXU_DTYPE)],
        compiler_params=_cp(("arbitrary",)),
    )(x2, mod, g_mix, gq2, gk2, w_in)


CONV_CH = 128


def _conv_taps(win, w_ref, acc, reverse):
    n = win.shape[0]
    for b in range(8):
        wb = win if b == 0 else pltpu.roll(win, shift=n - b, axis=0)
        for a in range(4):
            o = 8 * a + b
            if o < 1 or o > CONV_WIDTH:
                continue
            k = (CONV_WIDTH - o) if reverse else (o - 1)
            acc = acc + w_ref[k:k + 1, :] * wb[8 * a:8 * a + CONV_CH, :]
    return acc


def _conv_fwd(ag, wdw, *, Bl, S, DC):
    T = ag.shape[0]
    nsc = DC // LANES
    CH = CONV_CH

    def body(a_ref, g_ref, w_ref, cv_ref, upad):
        zeros16 = jnp.zeros((16, LANES), F32)
        upad[0:16, :] = zeros16
        upad[S + 16:S + 32, :] = zeros16

        def fill(i, _):
            r0 = pl.multiple_of(i * CH, CH)
            a = a_ref[pl.ds(r0, CH), :]
            g = g_ref[pl.ds(r0, CH), :]
            upad[pl.ds(r0 + 16, CH), :] = a * _sigmoid(g)
            return 0
        lax.fori_loop(0, S // CH, fill, 0)

        def conv(i, _):
            r0 = pl.multiple_of(i * CH, CH)
            win = upad[pl.ds(r0, CH + 32), :]
            acc = jnp.zeros((CH, LANES), F32) + w_ref[CONV_WIDTH:CONV_WIDTH + 1, :]
            cv_ref[pl.ds(r0, CH), :] = _conv_taps(win, w_ref, acc, reverse=False)
            return 0
        lax.fori_loop(0, S // CH, conv, 0)

    return pl.pallas_call(
        body, grid=(Bl, nsc), name="conv_fwd",
        in_specs=[pl.BlockSpec((S, LANES), lambda b, j: (b, j)),
                  pl.BlockSpec((S, LANES), lambda b, j: (b, nsc + j)),
                  pl.BlockSpec((CONV_ROWS, LANES), lambda b, j: (0, j))],
        out_specs=pl.BlockSpec((S, LANES), lambda b, j: (b, j)),
        out_shape=jax.ShapeDtypeStruct((T, DC), F32),
        scratch_shapes=[pltpu.VMEM((S + 32, LANES), F32)],
        compiler_params=_cp(("arbitrary", "arbitrary")),
    )(ag, ag, wdw)


def _conv_bwd(ag, dcv, wdw, *, Bl, S, DC):
    T = ag.shape[0]
    nsc = DC // LANES
    CH = CONV_CH

    def body(a_ref, g_ref, d_ref, w_ref, da_ref, dg_ref, dw_ref, upad, dpad, wacc):
        b = pl.program_id(1)
        zeros16 = jnp.zeros((16, LANES), F32)
        upad[0:16, :] = zeros16
        upad[S + 16:S + 32, :] = zeros16
        dpad[0:16, :] = zeros16
        dpad[S + 16:S + 32, :] = zeros16

        @pl.when(b == 0)
        def _():
            wacc[...] = jnp.zeros_like(wacc)

        def fill(i, _):
            r0 = pl.multiple_of(i * CH, CH)
            a = a_ref[pl.ds(r0, CH), :]
            g = g_ref[pl.ds(r0, CH), :]
            upad[pl.ds(r0 + 16, CH), :] = a * _sigmoid(g)
            dpad[pl.ds(r0 + 16, CH), :] = d_ref[pl.ds(r0, CH), :]
            return 0
        lax.fori_loop(0, S // CH, fill, 0)

        def step(i, _):
            r0 = pl.multiple_of(i * CH, CH)
            dwin = dpad[pl.ds(r0, CH + 32), :]
            du = _conv_taps(dwin, w_ref, jnp.zeros((CH, LANES), F32), reverse=True)
            a = a_ref[pl.ds(r0, CH), :]
            g = g_ref[pl.ds(r0, CH), :]
            sg = _sigmoid(g)
            da_ref[pl.ds(r0, CH), :] = du * sg
            dg_ref[pl.ds(r0, CH), :] = du * a * sg * (1.0 - sg)
            dc = d_ref[pl.ds(r0, CH), :]
            uwin = upad[pl.ds(r0, CH + 32), :]
            n = CH + 32
            for bb in range(8):
                wb = uwin if bb == 0 else pltpu.roll(uwin, shift=n - bb, axis=0)
                for aa in range(4):
                    o = 8 * aa + bb
                    if o < 1 or o > CONV_WIDTH:
                        continue
                    k = o - 1
                    prod = dc * wb[8 * aa:8 * aa + CH, :]
                    part = prod[0:8, :]
                    for q in range(1, CH // 8):
                        part = part + prod[8 * q:8 * q + 8, :]
                    wacc[8 * k:8 * k + 8, :] += part
            part = dc[0:8, :]
            for q in range(1, CH // 8):
                part = part + dc[8 * q:8 * q + 8, :]
            wacc[8 * CONV_WIDTH:8 * CONV_WIDTH + 8, :] += part
            return 0
        lax.fori_loop(0, S // CH, step, 0)

        @pl.when(b == Bl - 1)
        def _():
            for k in range(CONV_ROWS):
                dw_ref[k:k + 1, :] = jnp.sum(wacc[8 * k:8 * k + 8, :], axis=0, keepdims=True)

    return pl.pallas_call(
        body, grid=(nsc, Bl), name="conv_bwd",
        in_specs=[pl.BlockSpec((S, LANES), lambda j, b: (b, j)),
                  pl.BlockSpec((S, LANES), lambda j, b: (b, nsc + j)),
                  pl.BlockSpec((S, LANES), lambda j, b: (b, j)),
                  pl.BlockSpec((CONV_ROWS, LANES), lambda j, b: (0, j))],
        out_specs=[pl.BlockSpec((S, LANES), lambda j, b: (b, j)),
                   pl.BlockSpec((S, LANES), lambda j, b: (b, j)),
                   pl.BlockSpec((CONV_ROWS, LANES), lambda j, b: (0, j))],
        out_shape=[jax.ShapeDtypeStruct((T, DC), F32), jax.ShapeDtypeStruct((T, DC), F32),
                   jax.ShapeDtypeStruct((CONV_ROWS, DC), F32)],
        scratch_shapes=[pltpu.VMEM((S + 32, LANES), F32), pltpu.VMEM((S + 32, LANES), F32),
                        pltpu.VMEM((8 * CONV_ROWS, LANES), F32)],
        compiler_params=_cp(("arbitrary", "arbitrary")),
    )(ag, ag, dcv, wdw)


ROWCH = 256


LOG2E = 1.4426950408889634
LN2 = 0.6931471805599453
N_EDGE = 4


def _head_mean_matrix():
    r = lax.broadcasted_iota(jnp.int32, (LANES, LANES), 0) // HEAD_DIM
    c = lax.broadcasted_iota(jnp.int32, (LANES, LANES), 1) // HEAD_DIM
    return jnp.where(r == c, 1.0 / HEAD_DIM, 0.0).astype(jnp.bfloat16)


def _head_mean(v, mm):
    hi = v.astype(jnp.bfloat16)
    lo = (v - hi.astype(F32)).astype(jnp.bfloat16)
    return _dot(hi, mm) + _dot(lo, mm)


def _stack_heads(blk, lane_lo):
    z = jnp.zeros_like(blk)
    return jnp.concatenate([jnp.where(lane_lo, blk, z), jnp.where(lane_lo, z, blk)], axis=0)


def _merge_heads(v2, lane_lo):
    return jnp.where(lane_lo, v2[:QBLK], v2[QBLK:])


def _bias_tables(bias_ref, slope_ref):
    row = lax.broadcasted_iota(jnp.int32, (2 * QBLK, 2 * QBLK), 0)
    col = lax.broadcasted_iota(jnp.int32, (2 * QBLK, 2 * QBLK), 1)
    rel = jnp.abs(col - RADIUS - (row % QBLK))
    slope = jnp.where(row < QBLK, slope_ref[0:1, 0:1], slope_ref[0:1, HEAD_DIM:HEAD_DIM + 1]) * LOG2E
    for pi, d in enumerate(DILATIONS):
        inside = jnp.where(rel <= RADIUS, -slope * (float(d) * rel.astype(F32)), NEG_INF)
        for e in range(N_EDGE):
            t = inside
            if e & 1:
                t = jnp.where(col < RADIUS, NEG_INF, t)
            if e & 2:
                t = jnp.where(col >= QBLK + RADIUS, NEG_INF, t)
            bias_ref[N_EDGE * pi + e] = t


def _edge_index(qb, nb):
    return jnp.where(qb == 0, 1, 0) + jnp.where(qb == nb - 1, 2, 0)


VIA = 4


def _residue(d, s):
    return (s % VIA) * VIA + s // VIA if d == VIA * VIA else s


def _gather_rows(src_ref, dst_ref, S, d, pad, f32_copy=None):
    n = S // d
    seg = n + 2 * RADIUS if pad else n
    step = min(n, 512)
    two_step = d == VIA * VIA and f32_copy is not None
    for s in range(d):
        base = s * seg
        if pad:
            dst_ref[base:base + RADIUS, :] = jnp.zeros((RADIUS, LANES), dst_ref.dtype)
            dst_ref[base + RADIUS + n:base + seg, :] = jnp.zeros((RADIUS, LANES), dst_ref.dtype)
            base += RADIUS
        for c0 in range(0, n, step):
            if d == 1:
                v = src_ref[c0:c0 + step, :]
            elif two_step:
                v = f32_copy[pl.ds((s // VIA) * (S // VIA) + s % VIA + c0 * VIA, step, stride=VIA), :]
            else:
                v = src_ref[pl.ds(_residue(d, s) + c0 * d, step, stride=d), :]
                if d == VIA and f32_copy is not None:
                    f32_copy[s * n + c0:s * n + c0 + step, :] = v
            dst_ref[base + c0:base + c0 + step, :] = v.astype(dst_ref.dtype)


def _scatter_rows(src_ref, dst_ref, S, d, pad, accumulate):
    n = S // d
    seg = n + 2 * RADIUS if pad else n
    step = min(n, 512)
    for s in range(d):
        base = s * seg + (RADIUS if pad else 0)
        for c0 in range(0, n, step):
            v = src_ref[base + c0:base + c0 + step, :]
            if d == 1:
                idx = pl.ds(c0, step)
            else:
                idx = pl.ds(_residue(d, s) + c0 * d, step, stride=d)
            if accumulate:
                dst_ref[idx, :] = dst_ref[idx, :] + v
            else:
                dst_ref[idx, :] = v


def _zero_uncovered(acc, S, d):
    n = S // d
    if (n // QBLK) % 2:
        return
    seg = n + 2 * RADIUS
    for r in range(d):
        acc[0, r * seg + n:r * seg + seg, :] = jnp.zeros((2 * RADIUS, LANES), F32)
        acc[1, r * seg:r * seg + 2 * RADIUS, :] = jnp.zeros((2 * RADIUS, LANES), F32)


def _scatter_parity(acc, dst_ref, S, d):
    n = S // d
    seg = n + 2 * RADIUS
    step = min(n, 512)
    one_block = (n // QBLK) % 2 == 1
    for r in range(d):
        base = r * seg + RADIUS
        for c0 in range(0, n, step):
            rows = slice(base + c0, base + c0 + step)
            v = acc[r % 2, rows, :] if one_block else acc[0, rows, :] + acc[1, rows, :]
            idx = pl.ds(c0, step) if d == 1 else pl.ds(_residue(d, r) + c0 * d, step, stride=d)
            dst_ref[idx, :] = dst_ref[idx, :] + v


PIPE_UNROLL = 4
PIPE_SLOTS = 16
BWD_SLOTS = 12


def _pipeline(n_items, stages, unroll):
    K = len(stages)
    assert n_items % unroll == 0 and K * unroll <= (PIPE_SLOTS if K == 4 else BWD_SLOTS)
    trips = n_items // unroll
    assert trips >= K - 1

    def trip(t, static):
        for s in reversed(range(K)):
            if static and not 0 <= t - s < trips:
                continue
            for u in range(unroll):
                item = unroll * (t - s) + u
                stages[s](jnp.int32(item) if static else item)

    for t in range(K - 1):
        trip(t, True)

    def full(t, carry):
        trip(t, False)
        return carry
    lax.fori_loop(K - 1, trips, full, 0)
    for t in range(trips, trips + K - 1):
        trip(t, True)


def _attn_fwd(qkh, qkv, slopes, *, Bl, S, hosted=()):
    n3, T, _ = qkv.shape
    NS = n3 // 3
    NB = S // QBLK
    PADR = S + 2 * RADIUS * DILATIONS[-1]
    nh = len(hosted)
    plan = _WeightGather([b.shape for b in hosted]) if nh else None
    n_steps = Bl * NS

    def body(qh, kh, v_ref, slope_ref, *rest):
        o_ref, lse_ref = rest[nh:nh + 2]
        wouts = rest[nh + 2:2 * nh + 2]
        (qp, kp, vp, op, lp, onat, lnat, bias_ref, sbuf, pbuf, mbuf, lbuf, tmps) = rest[2 * nh + 2:2 * nh + 15]
        sems = rest[2 * nh + 15:]
        step = pl.program_id(0) * Bl + pl.program_id(1)
        if nh:
            @pl.when(step == 0)
            def _():
                plan.start(wouts, sems)

            @pl.when(step == (3 * n_steps) // 4)
            def _():
                plan.forward(wouts, sems)

        lane_lo = lax.broadcasted_iota(jnp.int32, (QBLK, LANES), 1) < HEAD_DIM

        @pl.when(pl.program_id(1) == 0)
        def _():
            _bias_tables(bias_ref, slope_ref)

        for pi, d in enumerate(DILATIONS):
            n = S // d
            nb = n // QBLK
            _gather_rows(qh, qp, S, d, pad=False, f32_copy=tmps.at[0])
            _gather_rows(kh, kp, S, d, pad=True, f32_copy=tmps.at[1])
            _gather_rows(v_ref, vp, S, d, pad=True, f32_copy=tmps.at[2])

            def offsets(i, nb=nb):
                r = i // nb
                return pl.multiple_of(i * QBLK, QBLK), pl.multiple_of((i + r) * QBLK, QBLK), i % nb

            def scores(i, pi=pi, nb=nb):
                q0, k0, qb = offsets(i)
                qs = _stack_heads(qp[pl.ds(q0, QBLK), :], lane_lo)
                sbuf[i % PIPE_SLOTS] = (_dot_nt(qs, kp[pl.ds(k0, 2 * QBLK), :])
                                        + bias_ref[N_EDGE * pi + _edge_index(qb, nb)])

            def rowmax(i):
                m = jnp.max(sbuf[i % PIPE_SLOTS], axis=1, keepdims=True)
                mbuf[i % PIPE_SLOTS] = jnp.broadcast_to(m, (2 * QBLK, LANES))

            def expsum(i):
                m = mbuf[i % PIPE_SLOTS]
                p = jnp.exp2(sbuf[i % PIPE_SLOTS] - jnp.concatenate([m, m], axis=1))
                pbuf[i % PIPE_SLOTS] = p.astype(MXU_DTYPE)
                lbuf[i % PIPE_SLOTS] = jnp.broadcast_to(jnp.sum(p, axis=1, keepdims=True), (2 * QBLK, LANES))

            def values(i):
                q0, k0, _ = offsets(i)
                l = lbuf[i % PIPE_SLOTS]
                o2 = _dot(pbuf[i % PIPE_SLOTS], vp[pl.ds(k0, 2 * QBLK), :]) * (1.0 / l)
                op[pl.ds(q0, QBLK), :] = _merge_heads(o2, lane_lo)
                lp[pl.ds(q0, QBLK), :] = _merge_heads(mbuf[i % PIPE_SLOTS] + jnp.log2(l), lane_lo)

            _pipeline(NB, [scores, rowmax, expsum, values], PIPE_UNROLL)
            _scatter_rows(op, onat.at[pi], S, d, pad=False, accumulate=False)
            _scatter_rows(lp, lnat.at[pi], S, d, pad=False, accumulate=False)

        for c0 in range(0, S, ROWCH):
            ls = [lnat[pi, c0:c0 + ROWCH, :] for pi in range(len(DILATIONS))]
            mx = jnp.maximum(jnp.maximum(ls[0], ls[1]), ls[2])
            es = [jnp.exp2(l - mx) for l in ls]
            tot = es[0] + es[1] + es[2]
            inv = 1.0 / tot
            acc = (es[0] * inv) * onat[0, c0:c0 + ROWCH, :]
            for pi in (1, 2):
                acc = acc + (es[pi] * inv) * onat[pi, c0:c0 + ROWCH, :]
            o_ref[c0:c0 + ROWCH, :] = acc
            lse_ref[c0:c0 + ROWCH, :] = mx + jnp.log2(tot)

        if nh:
            @pl.when(step == n_steps - 1)
            def _():
                plan.finish(wouts, sems)

    spec_in = lambda off: pl.BlockSpec((None, S, LANES), lambda j, b: (off * NS + j, b, 0))
    out = pl.BlockSpec((S, LANES), lambda j, b: (b, j))
    anyspec = pl.BlockSpec(memory_space=pl.ANY)
    return pl.pallas_call(
        body, grid=(NS, Bl), name="attn_fwd",
        in_specs=[spec_in(0), spec_in(1), spec_in(2),
                  pl.BlockSpec((None, 8, LANES), lambda j, b: (j, 0, 0))] + [anyspec] * nh,
        out_specs=[out, out] + [anyspec] * nh,
        out_shape=[jax.ShapeDtypeStruct((T, NS * LANES), F32)] * 2
                  + [jax.ShapeDtypeStruct(b.shape, b.dtype) for b in hosted],
        input_output_aliases={4 + w: 2 + w for w in range(nh)},
        scratch_shapes=[pltpu.VMEM((S, LANES), MXU_DTYPE), pltpu.VMEM((PADR, LANES), MXU_DTYPE),
                        pltpu.VMEM((PADR, LANES), MXU_DTYPE),
                        pltpu.VMEM((S, LANES), F32), pltpu.VMEM((S, LANES), F32),
                        pltpu.VMEM((3, S, LANES), F32), pltpu.VMEM((3, S, LANES), F32),
                        pltpu.VMEM((N_EDGE * len(DILATIONS), 2 * QBLK, 2 * QBLK), F32),
                        pltpu.VMEM((PIPE_SLOTS, 2 * QBLK, 2 * QBLK), F32),
                        pltpu.VMEM((PIPE_SLOTS, 2 * QBLK, 2 * QBLK), MXU_DTYPE),
                        pltpu.VMEM((PIPE_SLOTS, 2 * QBLK, LANES), F32), pltpu.VMEM((PIPE_SLOTS, 2 * QBLK, LANES), F32),
                        pltpu.VMEM((3, S, LANES), F32)]
                       + (plan.scratch() if nh else []),
        compiler_params=_cp(("arbitrary", "arbitrary")),
    )(qkh, qkh, qkv, slopes, *hosted)


def _attn_bwd(qkh, qkv, o, lse, do, gq2, gk2, slopes, *, Bl, S, hosted=()):
    n3, T, _ = qkv.shape
    NS = n3 // 3
    NB = S // QBLK
    PADR = S + 2 * RADIUS * DILATIONS[-1]
    QSCALE = HEAD_DIM ** -0.5
    nh = len(hosted)
    plan = _ChipExchange(nh)
    n_steps = Bl * NS

    def body(qh, kh, q_ref, k_ref, v_ref, o_ref, lse_ref, do_ref, gq_ref, gk_ref, slope_ref, *rest):
        hin = rest[:nh]
        dq_ref, dk_ref, dv_ref, gacc_ref = rest[nh:nh + 4]
        hout = rest[nh + 4:2 * nh + 4]
        (ld, qp, kp, vp, dop, ldp, dqp, dkacc, dvacc, dqn, dkn, bias_ref,
         sbuf, dpbuf, pbuf, dsbuf, tmps) = rest[2 * nh + 4:2 * nh + 21]
        sems = rest[2 * nh + 21:]
        step = pl.program_id(0) * Bl + pl.program_id(1)

        @pl.when(step == 0)
        def _():
            gacc_ref[...] = jnp.zeros_like(gacc_ref)
            if nh:
                plan.start(hin, hout, sems)

        mm = _head_mean_matrix()
        lane_lo = lax.broadcasted_iota(jnp.int32, (QBLK, LANES), 1) < HEAD_DIM

        @pl.when(pl.program_id(1) == 0)
        def _():
            _bias_tables(bias_ref, slope_ref)

        lse_lanes = lax.broadcasted_iota(jnp.int32, (ROWCH, LANES), 1) % HEAD_DIM < HEAD_DIM // 2
        for c0 in range(0, S, ROWCH):
            delta = _head_mean(do_ref[c0:c0 + ROWCH, :] * o_ref[c0:c0 + ROWCH, :], mm) * HEAD_DIM
            ld[c0:c0 + ROWCH, :] = jnp.where(lse_lanes, lse_ref[c0:c0 + ROWCH, :], delta)
            dqn[c0:c0 + ROWCH, :] = jnp.zeros((ROWCH, LANES), F32)
            dkn[c0:c0 + ROWCH, :] = jnp.zeros((ROWCH, LANES), F32)
            dv_ref[c0:c0 + ROWCH, :] = jnp.zeros((ROWCH, LANES), F32)

        for pi, d in enumerate(DILATIONS):
            n = S // d
            nb = n // QBLK
            _gather_rows(qh, qp, S, d, pad=False, f32_copy=tmps.at[0])
            _gather_rows(kh, kp, S, d, pad=True, f32_copy=tmps.at[1])
            _gather_rows(v_ref, vp, S, d, pad=True, f32_copy=tmps.at[2])
            _gather_rows(do_ref, dop, S, d, pad=False, f32_copy=tmps.at[3])
            _gather_rows(ld, ldp, S, d, pad=False, f32_copy=tmps.at[4])
            _zero_uncovered(dkacc, S, d)
            _zero_uncovered(dvacc, S, d)

            def offsets(i, nb=nb):
                r = i // nb
                return pl.multiple_of(i * QBLK, QBLK), pl.multiple_of((i + r) * QBLK, QBLK), i % nb

            def scores(i, pi=pi, nb=nb):
                q0, k0, qb = offsets(i)
                qs = _stack_heads(qp[pl.ds(q0, QBLK), :], lane_lo)
                dos = _stack_heads(dop[pl.ds(q0, QBLK), :], lane_lo)
                sbuf[i % BWD_SLOTS] = (_dot_nt(qs, kp[pl.ds(k0, 2 * QBLK), :])
                                       + bias_ref[N_EDGE * pi + _edge_index(qb, nb)])
                dpbuf[i % BWD_SLOTS] = _dot_nt(dos, vp[pl.ds(k0, 2 * QBLK), :])

            def probs(i):
                q0, _, _ = offsets(i)
                blk = ldp[pl.ds(q0, QBLK), :]
                half = HEAD_DIM // 2
                lcol = jnp.concatenate([blk[:, 0:1], blk[:, HEAD_DIM:HEAD_DIM + 1]], axis=0)
                dcol = jnp.concatenate([blk[:, half:half + 1], blk[:, HEAD_DIM + half:HEAD_DIM + half + 1]], axis=0)
                p = jnp.exp2(sbuf[i % BWD_SLOTS] - lcol)
                pbuf[i % BWD_SLOTS] = p.astype(MXU_DTYPE)
                dsbuf[i % BWD_SLOTS] = (p * (dpbuf[i % BWD_SLOTS] - dcol)).astype(MXU_DTYPE)

            def grads(i):
                q0, k0, _ = offsets(i)
                qs = _stack_heads(qp[pl.ds(q0, QBLK), :], lane_lo)
                dos = _stack_heads(dop[pl.ds(q0, QBLK), :], lane_lo)
                ds = dsbuf[i % BWD_SLOTS]
                dvacc[i % 2, pl.ds(k0, 2 * QBLK), :] = _dot_tn(pbuf[i % BWD_SLOTS], dos)
                dkacc[i % 2, pl.ds(k0, 2 * QBLK), :] = _dot_tn(ds, qs)
                dqp[pl.ds(q0, QBLK), :] = _merge_heads(_dot(ds, kp[pl.ds(k0, 2 * QBLK), :]), lane_lo)

            _pipeline(NB, [scores, probs, grads], PIPE_UNROLL)
            _scatter_rows(dqp, dqn, S, d, pad=False, accumulate=True)
            _scatter_parity(dkacc, dkn, S, d)
            _scatter_parity(dvacc, dv_ref, S, d)

        gq_sum = jnp.zeros((8, LANES), F32)
        gk_sum = jnp.zeros((8, LANES), F32)
        for c0 in range(0, S, ROWCH):
            for src_ref, dn, g_ref, dst_ref, scale, is_q in ((q_ref, dqn, gq_ref, dq_ref, QSCALE, True),
                                                             (k_ref, dkn, gk_ref, dk_ref, LN2, False)):
                x = src_ref[c0:c0 + ROWCH, :]
                dh = dn[c0:c0 + ROWCH, :]
                rr = lax.rsqrt(_head_mean(x * x, mm) + EPS)
                e = dh * (g_ref[...] * scale)
                dst_ref[c0:c0 + ROWCH, :] = rr * e - x * (rr * rr * rr) * _head_mean(e * x, mm)
                gpart = dh * (x * rr * scale)
                acc8 = gpart[0:8, :]
                for q8 in range(1, ROWCH // 8):
                    acc8 = acc8 + gpart[8 * q8:8 * q8 + 8, :]
                if is_q:
                    gq_sum = gq_sum + acc8
                else:
                    gk_sum = gk_sum + acc8
        gacc_ref[0:1, :] += jnp.sum(gq_sum, axis=0, keepdims=True)
        gacc_ref[1:2, :] += jnp.sum(gk_sum, axis=0, keepdims=True)

        if nh:
            @pl.when(step == n_steps - 1)
            def _():
                plan.finish(hin, hout, sems)

    spec_in = lambda off: pl.BlockSpec((None, S, LANES), lambda j, b: (off * NS + j, b, 0))
    tok = pl.BlockSpec((S, LANES), lambda j, b: (b, j))
    vec = pl.BlockSpec((1, LANES), lambda j, b: (0, 0))
    slab_out = pl.BlockSpec((None, S, LANES), lambda j, b: (j, b, 0))
    f32buf = lambda rows: pltpu.VMEM((rows, LANES), F32)
    bfbuf = lambda rows: pltpu.VMEM((rows, LANES), MXU_DTYPE)
    anyspec = pl.BlockSpec(memory_space=pl.ANY)
    return pl.pallas_call(
        body, grid=(NS, Bl), name="attn_bwd",
        in_specs=[spec_in(0), spec_in(1), spec_in(0), spec_in(1), spec_in(2), tok, tok, tok, vec, vec,
                  pl.BlockSpec((None, 8, LANES), lambda j, b: (j, 0, 0))] + [anyspec] * nh,
        out_specs=[slab_out, slab_out, slab_out, pl.BlockSpec((8, LANES), lambda j, b: (0, 0))] + [anyspec] * nh,
        out_shape=[jax.ShapeDtypeStruct((NS, T, LANES), F32)] * 3 + [jax.ShapeDtypeStruct((8, LANES), F32)]
                  + [jax.ShapeDtypeStruct((3,) + h.shape[1:], h.dtype) for h in hosted],
        scratch_shapes=[f32buf(S),
                        bfbuf(S), bfbuf(PADR), bfbuf(PADR), bfbuf(S),
                        f32buf(S), f32buf(S),
                        pltpu.VMEM((2, PADR, LANES), F32), pltpu.VMEM((2, PADR, LANES), F32),
                        f32buf(S), f32buf(S),
                        pltpu.VMEM((N_EDGE * len(DILATIONS), 2 * QBLK, 2 * QBLK), F32),
                        pltpu.VMEM((BWD_SLOTS, 2 * QBLK, 2 * QBLK), F32),
                        pltpu.VMEM((BWD_SLOTS, 2 * QBLK, 2 * QBLK), F32),
                        pltpu.VMEM((BWD_SLOTS, 2 * QBLK, 2 * QBLK), MXU_DTYPE),
                        pltpu.VMEM((BWD_SLOTS, 2 * QBLK, 2 * QBLK), MXU_DTYPE),
                        pltpu.VMEM((5, S, LANES), F32)]
                       + (plan.scratch() if nh else []),
        compiler_params=_cp(("arbitrary", "arbitrary"), vmem=ATTN_BWD_VMEM),
    )(qkh, qkh, qkv, qkv, qkv, o, lse, do, gq2, gk2, slopes, *hosted)


def _layer_norm_parts(cv, g_ln, b_ln):
    mu = jnp.mean(cv, axis=-1, keepdims=True)
    cen = cv - mu
    rs = lax.rsqrt(jnp.mean(cen * cen, axis=-1, keepdims=True) + EPS)
    z = cen * rs
    return z, rs, z * g_ln + b_ln


def _ffn_fwd(x2, cv, ya, tgt, mod, g_ln, b_ln, g_ffn, w_out, w_gate, w_up, w_down, *, S, tm):
    T, D = x2.shape
    DC = cv.shape[1]
    P, Kb, _ = w_out.shape
    Fb = w_down.shape[1]
    tps = S // tm

    def body(x_ref, cv_ref, ya_ref, t_ref, mod_ref, gln_ref, bln_ref, gf_ref, wo_hbm, wg_hbm, wu_hbm, wd_hbm,
             x1_ref, ycat_ref, mix_ref, h2_ref, g_ref, u_ref, a_ref, f_ref, dy_ref, loss_ref,
             wo, wg, wu, wd, sems):
        i = pl.program_id(0)
        _load_resident(i, [(wo_hbm, wo), (wg_hbm, wg), (wu_hbm, wu), (wd_hbm, wd)], sems)

        @pl.when(i == 0)
        def _():
            loss_ref[...] = jnp.zeros_like(loss_ref)

        _, _, ln = _layer_norm_parts(cv_ref[...], gln_ref[...], bln_ref[...])
        yc = ln * _sigmoid(ln)
        ycat = jnp.concatenate([yc, ya_ref[...]], axis=1).astype(MXU_DTYPE)
        ycat_ref[...] = ycat
        mix = _dot(ycat[:, 0:Kb], wo[0])
        for p in range(1, P):
            mix = mix + _dot(ycat[:, Kb * p:Kb * (p + 1)], wo[p])
        mix_ref[...] = mix.astype(ACT_DTYPE)
        x1 = x_ref[...] + mod_ref[:, 2 * D:3 * D] * mix
        x1_ref[...] = x1
        r2 = lax.rsqrt(jnp.mean(x1 * x1, axis=-1, keepdims=True) + EPS)
        h2 = (x1 * r2 * gf_ref[...]) * (1.0 + mod_ref[:, 4 * D:5 * D]) + mod_ref[:, 3 * D:4 * D]
        h2b = h2.astype(MXU_DTYPE)
        h2_ref[...] = h2b
        f = jnp.zeros((tm, D), F32)
        for p in range(P):
            g = _dot_nt(h2b, wg[p])
            u = _dot_nt(h2b, wu[p])
            a = (g * _sigmoid(g) * u).astype(MXU_DTYPE)
            g_ref[p] = g.astype(ACT_DTYPE)
            u_ref[p] = u.astype(ACT_DTYPE)
            a_ref[p] = a
            f = f + _dot(a, wd[p])
        f_ref[...] = f.astype(ACT_DTYPE)
        err = x1 + mod_ref[:, 5 * D:6 * D] * f - t_ref[...]
        dy_ref[...] = err * (1.0 / D)
        tot = jnp.sum(_colsum(err * err), axis=1, keepdims=True)
        loss_ref[...] += tot * (0.5 / D)

    row = lambda w: pl.BlockSpec((tm, w), lambda i: (i, 0))
    vec = lambda w: pl.BlockSpec((1, w), lambda i: (0, 0))
    blk = pl.BlockSpec((P, tm, Fb), lambda i: (0, i, 0))
    anyspec = pl.BlockSpec(memory_space=pl.ANY)
    return pl.pallas_call(
        body, grid=(T // tm,), name="ffn_fwd",
        in_specs=[row(D), row(DC), row(D - DC), row(D),
                  pl.BlockSpec((None, 1, N_MOD * D), lambda i: (i // tps, 0, 0)),
                  vec(DC), vec(DC), vec(D), anyspec, anyspec, anyspec, anyspec],
        out_specs=[row(D), row(D), row(D), row(D), blk, blk, blk, row(D), row(D),
                   pl.BlockSpec((8, LANES), lambda i: (0, 0))],
        out_shape=[jax.ShapeDtypeStruct((T, D), F32), jax.ShapeDtypeStruct((T, D), MXU_DTYPE),
                   jax.ShapeDtypeStruct((T, D), ACT_DTYPE), jax.ShapeDtypeStruct((T, D), MXU_DTYPE),
                   jax.ShapeDtypeStruct((P, T, Fb), ACT_DTYPE), jax.ShapeDtypeStruct((P, T, Fb), ACT_DTYPE),
                   jax.ShapeDtypeStruct((P, T, Fb), MXU_DTYPE), jax.ShapeDtypeStruct((T, D), ACT_DTYPE),
                   jax.ShapeDtypeStruct((T, D), F32), jax.ShapeDtypeStruct((8, LANES), F32)],
        scratch_shapes=[pltpu.VMEM(w_out.shape, w_out.dtype), pltpu.VMEM(w_gate.shape, w_gate.dtype),
                        pltpu.VMEM(w_up.shape, w_up.dtype), pltpu.VMEM(w_down.shape, w_down.dtype),
                        pltpu.SemaphoreType.DMA((4,))],
        compiler_params=_cp(("arbitrary",)),
    )(x2, cv, ya, tgt, mod, g_ln, b_ln, g_ffn, w_out, w_gate, w_up, w_down)


def _ffn_bwd(dy, x1, gs, us, fo, mixb, cv, mod, g_ln, b_ln, g_ffn, w_out, w_gate, w_up, w_down, *, S, tm):
    T, D = dy.shape
    DC = cv.shape[1]
    P, Kb, _ = w_out.shape
    Fb = w_down.shape[1]
    tps = S // tm
    Bl = T // S

    def body(dy_ref, x1_ref, g_ref, u_ref, f_ref, mix_ref, cv_ref, mod_ref, gln_ref, bln_ref, gf_ref,
             wo_hbm, wg_hbm, wu_hbm, wd_hbm,
             dg_ref, du_ref, df_ref, dx1_ref, dmix_ref, dya_ref, dcv_ref, macc_ref, gacc_ref, lacc_ref,
             wo, wg, wu, wd, sems):
        i = pl.program_id(0)
        _load_resident(i, [(wo_hbm, wo), (wg_hbm, wg), (wu_hbm, wu), (wd_hbm, wd)], sems)

        @pl.when(i == 0)
        def _():
            gacc_ref[...] = jnp.zeros_like(gacc_ref)
            lacc_ref[...] = jnp.zeros_like(lacc_ref)

        @pl.when(i % tps == 0)
        def _():
            macc_ref[...] = jnp.zeros_like(macc_ref)

        dy_t = dy_ref[...]
        x1 = x1_ref[...]
        gate_f = mod_ref[:, 5 * D:6 * D]
        macc_ref[2:3, :] += _colsum(dy_t * f_ref[...].astype(F32))
        dfb = (dy_t * gate_f).astype(MXU_DTYPE)
        df_ref[...] = dfb
        dh2 = jnp.zeros((tm, D), F32)
        for p in range(P):
            da = _dot_nt(dfb, wd[p])
            g = g_ref[p].astype(F32)
            u = u_ref[p].astype(F32)
            sg = _sigmoid(g)
            dgp = (da * u * (sg * (1.0 + g * (1.0 - sg)))).astype(MXU_DTYPE)
            dup = (da * (g * sg)).astype(MXU_DTYPE)
            dg_ref[p] = dgp
            du_ref[p] = dup
            dh2 = dh2 + _dot(dgp, wg[p]) + _dot(dup, wu[p])
        r2 = lax.rsqrt(jnp.mean(x1 * x1, axis=-1, keepdims=True) + EPS)
        xr = x1 * r2
        n2 = xr * gf_ref[...]
        macc_ref[0:1, :] += _colsum(dh2)
        macc_ref[1:2, :] += _colsum(dh2 * n2)
        dn2 = dh2 * (1.0 + mod_ref[:, 4 * D:5 * D])
        gacc_ref[0:1, :] += _colsum(dn2 * xr)
        e = dn2 * gf_ref[...]
        dx1 = dy_t + r2 * e - xr * (r2 * jnp.mean(e * xr, axis=-1, keepdims=True))
        dx1_ref[...] = dx1
        macc_ref[3:4, :] += _colsum(dx1 * mix_ref[...].astype(F32))
        dmixb = (dx1 * mod_ref[:, 2 * D:3 * D]).astype(MXU_DTYPE)
        dmix_ref[...] = dmixb
        parts = [_dot_nt(dmixb, wo[p]) for p in range(P)]
        dycat = jnp.concatenate(parts, axis=1) if P > 1 else parts[0]
        dya_ref[...] = dycat[:, DC:]
        dyc = dycat[:, :DC]
        z, rs, ln = _layer_norm_parts(cv_ref[...], gln_ref[...], bln_ref[...])
        sg = _sigmoid(ln)
        dln = dyc * (sg * (1.0 + ln * (1.0 - sg)))
        lacc_ref[0:1, :] += _colsum(dln * z)
        lacc_ref[1:2, :] += _colsum(dln)
        dz = dln * gln_ref[...]
        dcv_ref[...] = rs * (dz - jnp.mean(dz, axis=-1, keepdims=True) - z * jnp.mean(dz * z, axis=-1, keepdims=True))

    row = lambda w: pl.BlockSpec((tm, w), lambda i: (i, 0))
    vec = lambda w: pl.BlockSpec((1, w), lambda i: (0, 0))
    blk = pl.BlockSpec((P, tm, Fb), lambda i: (0, i, 0))
    anyspec = pl.BlockSpec(memory_space=pl.ANY)
    return pl.pallas_call(
        body, grid=(T // tm,), name="ffn_bwd",
        in_specs=[row(D), row(D), blk, blk, row(D), row(D), row(DC),
                  pl.BlockSpec((None, 1, N_MOD * D), lambda i: (i // tps, 0, 0)),
                  vec(DC), vec(DC), vec(D), anyspec, anyspec, anyspec, anyspec],
        out_specs=[blk, blk, row(D), row(D), row(D), row(D - DC), row(DC),
                   pl.BlockSpec((None, 8, D), lambda i: (i // tps, 0, 0)),
                   pl.BlockSpec((8, D), lambda i: (0, 0)), pl.BlockSpec((8, DC), lambda i: (0, 0))],
        out_shape=[jax.ShapeDtypeStruct((P, T, Fb), MXU_DTYPE), jax.ShapeDtypeStruct((P, T, Fb), MXU_DTYPE),
                   jax.ShapeDtypeStruct((T, D), MXU_DTYPE), jax.ShapeDtypeStruct((T, D), F32),
                   jax.ShapeDtypeStruct((T, D), MXU_DTYPE), jax.ShapeDtypeStruct((T, D - DC), F32),
                   jax.ShapeDtypeStruct((T, DC), F32), jax.ShapeDtypeStruct((Bl, 8, D), F32),
                   jax.ShapeDtypeStruct((8, D), F32), jax.ShapeDtypeStruct((8, DC), F32)],
        scratch_shapes=[pltpu.VMEM(w_out.shape, w_out.dtype), pltpu.VMEM(w_gate.shape, w_gate.dtype),
                        pltpu.VMEM(w_up.shape, w_up.dtype), pltpu.VMEM(w_down.shape, w_down.dtype),
                        pltpu.SemaphoreType.DMA((4,))],
        compiler_params=_cp(("arbitrary",)),
    )(dy, x1, gs, us, fo, mixb, cv, mod, g_ln, b_ln, g_ffn, w_out, w_gate, w_up, w_down)


def _in_bwd(da, dg, dq, dk, dv, x2, dx1, mod, g_mix, w_in, *, S, tm):
    T, D = x2.shape
    P, _, Nb = w_in.shape
    DC = da.shape[1]
    NS = dq.shape[0]
    n_in = P * Nb
    tps = S // tm
    Bl = T // S

    def body(da_ref, dg_ref, dq_ref, dk_ref, dv_ref, x_ref, dx1_ref, mod_ref, g_ref, w_ref,
             dx_ref, dproj_ref, macc_ref, gacc_ref):
        i = pl.program_id(0)

        @pl.when(i == 0)
        def _():
            gacc_ref[...] = jnp.zeros_like(gacc_ref)

        @pl.when(i % tps == 0)
        def _():
            macc_ref[...] = jnp.zeros_like(macc_ref)

        pieces = [da_ref[...], dg_ref[...]] + [r[j] for r in (dq_ref, dk_ref, dv_ref) for j in range(NS)]
        dproj = jnp.concatenate(pieces, axis=1).astype(MXU_DTYPE)
        dproj_ref[...] = dproj
        dh = _dot_nt(dproj[:, 0:Nb], w_ref[0])
        for p in range(1, P):
            dh = dh + _dot_nt(dproj[:, Nb * p:Nb * (p + 1)], w_ref[p])
        x = x_ref[...]
        r = lax.rsqrt(jnp.mean(x * x, axis=-1, keepdims=True) + EPS)
        xr = x * r
        macc_ref[0:1, :] += _colsum(dh)
        macc_ref[1:2, :] += _colsum(dh * (xr * g_ref[...]))
        dn = dh * (1.0 + mod_ref[:, D:2 * D])
        gacc_ref[0:1, :] += _colsum(dn * xr)
        e = dn * g_ref[...]
        dx_ref[...] = dx1_ref[...] + r * e - xr * (r * jnp.mean(e * xr, axis=-1, keepdims=True))

    row = lambda w: pl.BlockSpec((tm, w), lambda i: (i, 0))
    slab = pl.BlockSpec((NS, tm, LANES), lambda i: (0, i, 0))
    return pl.pallas_call(
        body, grid=(T // tm,), name="in_bwd",
        in_specs=[row(DC), row(DC), slab, slab, slab, row(D), row(D),
                  pl.BlockSpec((None, 1, N_MOD * D), lambda i: (i // tps, 0, 0)),
                  pl.BlockSpec((1, D), lambda i: (0, 0)),
                  pl.BlockSpec((P, D, Nb), lambda i: (0, 0, 0))],
        out_specs=[row(D), row(n_in), pl.BlockSpec((None, 8, D), lambda i: (i // tps, 0, 0)),
                   pl.BlockSpec((8, D), lambda i: (0, 0))],
        out_shape=[jax.ShapeDtypeStruct((T, D), F32), jax.ShapeDtypeStruct((T, n_in), MXU_DTYPE),
                   jax.ShapeDtypeStruct((Bl, 8, D), F32), jax.ShapeDtypeStruct((8, D), F32)],
        compiler_params=_cp(("arbitrary",)),
    )(da, dg, dq, dk, dv, x2, dx1, mod, g_mix, w_in)


def _wgrad(a, b, *, P, name, tk, split=None, host=None):
    a_blk, b_blk = a.ndim == 3, b.ndim == 3
    plan, h_in, h_out = host if host is not None else (None, (), ())
    ni, no = len(h_in), len(h_out)
    T = a.shape[-2]
    if a_blk:
        R, C = a.shape[2], b.shape[1]
        a_of = lambda av, p: av[p]
        b_of = lambda bv, p: bv[...]
    elif b_blk:
        R, C = a.shape[1], b.shape[2]
        a_of = lambda av, p: av[...]
        b_of = lambda bv, p: bv[p]
    elif split == "a":
        R, C = a.shape[1] // P, b.shape[1]
        a_of = lambda av, p: av[:, R * p:R * (p + 1)]
        b_of = lambda bv, p: bv[...]
    else:
        R, C = a.shape[1], b.shape[1] // P
        a_of = lambda av, p: av[...]
        b_of = lambda bv, p: bv[:, C * p:C * (p + 1)]

    n_steps = T // tk

    def body(a_ref, b_ref, *rest):
        hin, o_ref, hout, sems = rest[:ni], rest[ni], rest[ni + 1:ni + 1 + no], rest[ni + 1 + no:]
        step = pl.program_id(0)

        @pl.when(step == 0)
        def _():
            o_ref[...] = jnp.zeros_like(o_ref)
            if plan is not None:
                plan.start(hin, hout, sems)

        if plan is not None:
            @pl.when(step == n_steps // 2)
            def _():
                plan.forward(hin, hout, sems)

        for p in range(P):
            o_ref[p] += _dot_tn(a_of(a_ref, p), b_of(b_ref, p))

        if plan is not None:
            @pl.when(step == n_steps - 1)
            def _():
                plan.finish(hin, hout, sems)

    def spec(v):
        if v.ndim == 3:
            return pl.BlockSpec((P, tk, v.shape[2]), lambda k: (0, k, 0))
        return pl.BlockSpec((tk, v.shape[1]), lambda k: (k, 0))

    anyspec = pl.BlockSpec(memory_space=pl.ANY)
    res = pl.pallas_call(
        body, grid=(n_steps,), name=name,
        in_specs=[spec(a), spec(b)] + [anyspec] * ni,
        out_specs=[pl.BlockSpec((P, R, C), lambda k: (0, 0, 0))] + [anyspec] * no,
        out_shape=[jax.ShapeDtypeStruct((P, R, C), F32)] + list(h_out),
        scratch_shapes=plan.scratch() if plan is not None else [],
        compiler_params=_cp(("arbitrary",)),
    )(a, b, *h_in)
    return res if plan is not None else res[0]


TM_IN = 512
TM_FFN = 256
TK_WGRAD = 1024


def _alibi_slabs(n_slab):
    heads = 2 * n_slab
    slopes = 2.0 ** (-8.0 * np.arange(1, heads + 1) / heads)
    return jnp.asarray(np.broadcast_to(np.repeat(slopes.reshape(n_slab, 1, 2), HEAD_DIM, axis=2), (n_slab, 8, LANES)),
                       dtype=F32)


def _local_step(x, tgt, mod, g_mix, wdw, g_ln, b_ln, g_q, g_k, g_ffn, w_in, w_out, w_gate, w_up, w_down,
                pc_idx=None):
    Bl, S, D = x.shape
    T = Bl * S
    DC = g_ln.shape[1]
    P = w_in.shape[0]
    n_slab = (D - DC) // LANES
    x2 = x.reshape(T, D)
    t2 = tgt.reshape(T, D)
    mod3 = mod.reshape(Bl, 1, N_MOD * D)
    gq2 = jnp.tile(g_q, (1, LANES // HEAD_DIM))
    gk2 = jnp.tile(g_k, (1, LANES // HEAD_DIM))
    slopes = _alibi_slabs(n_slab)

    ag, qkv, qkh, h1 = _fwd_in(x2, mod3, g_mix, gq2, gk2, w_in, S=S, tm=TM_IN, n_ag=2 * DC)
    cv = _conv_fwd(ag, wdw, Bl=Bl, S=S, DC=DC)
    if pc_idx is not None:
        ya, lse, w_out, w_gate, w_up, w_down = _attn_fwd(qkh, qkv, slopes, Bl=Bl, S=S,
                                                         hosted=(w_out, w_gate, w_up, w_down))
    else:
        ya, lse = _attn_fwd(qkh, qkv, slopes, Bl=Bl, S=S)
    x1, ycat, mixb, h2, gs, us, acts, fo, dy, lossb = _ffn_fwd(
        x2, cv, ya, t2, mod3, g_ln, b_ln, g_ffn, w_out, w_gate, w_up, w_down, S=S, tm=TM_FFN)
    dgs, dus, dfb, dx1, dmixb, dya, dcv, macc_f, gacc_f, lacc = _ffn_bwd(
        dy, x1, gs, us, fo, mixb, cv, mod3, g_ln, b_ln, g_ffn, w_out, w_gate, w_up, w_down, S=S, tm=TM_FFN)
    wg = functools.partial(_wgrad, P=P, tk=TK_WGRAD)
    out = {}
    if pc_idx is None:
        grads = dict(w_down=wg(acts, dfb, name="wgrad_down"), w_gate=wg(dgs, h2, name="wgrad_gate"),
                     w_up=wg(dus, h2, name="wgrad_up"), w_out=wg(ycat, dmixb, name="wgrad_out", split="a"))
        dq, dk, dv, gqk = _attn_bwd(qkh, qkv, ya, lse, dya, gq2, gk2, slopes, Bl=Bl, S=S)
    else:
        g_down = wg(acts, dfb, name="wgrad_down")
        g_gate, r_down = wg(dgs, h2, name="wgrad_gate", host=_sibling_host([g_down]))
        g_up, r_gate = wg(dus, h2, name="wgrad_up", host=_sibling_host([g_gate]))
        g_out, r_up = wg(ycat, dmixb, name="wgrad_out", split="a", host=_sibling_host([g_up]))
        (r_out,) = _rs_sibling([g_out], "rs_sibling_out")
        grads = dict(w_down=g_down, w_gate=g_gate, w_up=g_up, w_out=g_out)
        sums = [_pair_add(grads[nm], r, pc_idx, "pair_add_" + nm)
                for nm, r in zip(EARLY_WEIGHTS, (r_down, r_gate, r_up, r_out))]
        res = _attn_bwd(qkh, qkv, ya, lse, dya, gq2, gk2, slopes, Bl=Bl, S=S, hosted=tuple(sb for _, sb in sums))
        dq, dk, dv, gqk = res[:4]
        out["early_sums"] = [s32 for s32, _ in sums]
        out["early_recv"] = list(res[4:])
    da, dg, dwdw = _conv_bwd(ag, dcv, wdw, Bl=Bl, S=S, DC=DC)
    dx, dprojb, macc_m, gacc_m = _in_bwd(da, dg, dq, dk, dv, x2, dx1, mod3, g_mix, w_in, S=S, tm=TM_IN)
    packed = _pack_small(macc_m, macc_f, gacc_m, gacc_f, lacc, gqk, dwdw, lossb)
    if pc_idx is None:
        grads["w_in"] = wg(h1, dprojb, name="wgrad_in", split="b")
    else:
        grads["w_in"], out["gathered_small"] = wg(h1, dprojb, name="wgrad_in", split="b",
                                                  host=_small_gather_host(packed))
    out.update(dx=dx.reshape(Bl, S, D), grads=grads, packed=packed)
    return out


EARLY_WEIGHTS = ("w_down", "w_gate", "w_up", "w_out")


def _small_layout(Bl):
    return 8 * Bl, 8 * Bl + 8, 8 * Bl + 8 + CONV_ROWS


def _pack_small(macc_m, macc_f, gacc_m, gacc_f, lacc, gqk, dwdw, lossb):
    Bl, _, D = macc_m.shape
    DC = lacc.shape[1]
    assert 2 * DC <= D
    SMALL_GAIN_ROW, SMALL_TAP_ROW, SMALL_ROWS = _small_layout(Bl)

    def body(mm_ref, mf_ref, gm_ref, gf_ref, la_ref, qk_ref, dw_ref, loss_ref, o_ref):
        o_ref[...] = jnp.zeros_like(o_ref)
        for b in range(Bl):
            o_ref[8 * b + 0:8 * b + 2, :] = mm_ref[b, 0:2, :]
            o_ref[8 * b + 2:8 * b + 3, :] = mf_ref[b, 3:4, :]
            o_ref[8 * b + 3:8 * b + 6, :] = mf_ref[b, 0:3, :]
        r = SMALL_GAIN_ROW
        o_ref[r:r + 1, :] = gm_ref[0:1, :]
        o_ref[r + 1:r + 2, :] = gf_ref[0:1, :]
        o_ref[r + 2:r + 3, 0:DC] = la_ref[0:1, :]
        o_ref[r + 2:r + 3, DC:2 * DC] = la_ref[1:2, :]
        qk = qk_ref[0:2, 0:HEAD_DIM] + qk_ref[0:2, HEAD_DIM:2 * HEAD_DIM]
        o_ref[r + 3:r + 4, 0:HEAD_DIM] = qk[0:1, :]
        o_ref[r + 3:r + 4, HEAD_DIM:2 * HEAD_DIM] = qk[1:2, :]
        o_ref[r + 4:r + 5, 0:LANES] = loss_ref[0:1, :]
        o_ref[SMALL_TAP_ROW:SMALL_TAP_ROW + CONV_ROWS, 0:DC] = dw_ref[...]

    return pl.pallas_call(body, name="pack_small", out_shape=jax.ShapeDtypeStruct((SMALL_ROWS, D), F32),
                          compiler_params=_cp())(macc_m, macc_f, gacc_m, gacc_f, lacc, gqk, dwdw, lossb)


def _row_tile(rows, cap=512):
    if rows <= cap:
        return rows
    best = rows
    for t in range(8, cap + 1, 8):
        if rows % t == 0:
            best = t
    return best


def _cast_weight(w, pidx, name):
    def body(p_ref, w_ref, o_ref):
        o_ref[...] = w_ref[...].astype(MXU_DTYPE)
    R, C = w.shape
    tr = _row_tile(R)
    return pl.pallas_call(
        body, name=name,
        grid_spec=pltpu.PrefetchScalarGridSpec(
            num_scalar_prefetch=1, grid=(R // tr,),
            in_specs=[pl.BlockSpec((tr, C), lambda i, p: (i, 0))],
            out_specs=pl.BlockSpec((None, tr, C), lambda i, p: (p[0], i, 0))),
        out_shape=jax.ShapeDtypeStruct((4, R, C), MXU_DTYPE),
    )(pidx, w)


def _pair_add(g, recv, pc_idx, name):
    P, R, C = g.shape
    R2 = R // 2

    def body(pc_ref, g_ref, r_ref, o_ref, ob_ref):
        s = g_ref[...] + r_ref[...]
        ob_ref[...] = s.astype(jnp.bfloat16)

        @pl.when(pl.program_id(0) == pc_ref[0])
        def _():
            o_ref[...] = s

    return pl.pallas_call(
        body, name=name,
        grid_spec=pltpu.PrefetchScalarGridSpec(
            num_scalar_prefetch=1, grid=(P,),
            in_specs=[pl.BlockSpec((None, R2, C), lambda p, pc: (p, pc[1], 0)),
                      pl.BlockSpec((None, R2, C), lambda p, pc: (p, 0, 0))],
            out_specs=[pl.BlockSpec((R2, C), lambda p, pc: (0, 0)),
                       pl.BlockSpec((None, R2, C), lambda p, pc: (p, 0, 0))]),
        out_shape=[jax.ShapeDtypeStruct((R2, C), F32), jax.ShapeDtypeStruct((P, R2, C), jnp.bfloat16)],
    )(pc_idx, g, recv)


def _final_add(own, recv, pc_idx, name):
    R2, C = own.shape

    def body(pc_ref, s_ref, r_ref, o_ref):
        acc = s_ref[...]
        for k in range(3):
            acc = acc + r_ref[k].astype(F32)
        o_ref[...] = acc

    return pl.pallas_call(
        body, name=name,
        grid_spec=pltpu.PrefetchScalarGridSpec(
            num_scalar_prefetch=1, grid=(1,),
            in_specs=[pl.BlockSpec((R2, C), lambda i, pc: (0, 0)),
                      pl.BlockSpec((3, R2, C), lambda i, pc: (0, 0, 0))],
            out_specs=pl.BlockSpec((R2, C), lambda i, pc: (pc[1], 0))),
        out_shape=jax.ShapeDtypeStruct((2 * R2, C), F32),
    )(pc_idx, own, recv)


def _adamw_update(w_ref, g_ref, m_ref, v_ref, d_ref, nm_ref, nv_ref):
    c1 = 1.0 - ADAM_B1 ** ADAM_STEP
    c2 = 1.0 - ADAM_B2 ** ADAM_STEP
    gg = g_ref[...]
    nm = ADAM_B1 * m_ref[...] + (1.0 - ADAM_B1) * gg
    nv = ADAM_B2 * v_ref[...] + (1.0 - ADAM_B2) * (gg * gg)
    nm_ref[...] = nm
    nv_ref[...] = nv
    d_ref[...] = -ADAM_LR * ((nm / c1) / (jnp.sqrt(nv / c2) + ADAM_EPS) + ADAM_WD * w_ref[...])


def _adamw(w, g, m, v, name):
    R, C = w.shape
    tr = _row_tile(R, 256)
    spec = pl.BlockSpec((tr, C), lambda i: (i, 0))
    return pl.pallas_call(
        functools.partial(_adamw_update), grid=(R // tr,), name=name,
        in_specs=[spec] * 4, out_specs=[spec] * 3,
        out_shape=[jax.ShapeDtypeStruct((R, C), F32)] * 3,
    )(w, g, m, v)


def _startup(first, w_ada, b_cols, w_in_buf, *, Bl):
    rows, D = first.shape
    NA = w_ada.shape[1]
    n_dev = 8
    g_w = _WeightGather([w_in_buf.shape])
    g_c = _SmallGather(rows)
    g_m = _SmallGather(n_dev * Bl)

    def body(first_ref, wada_ref, b_ref, win_in, g0_ref, call_ref, gm_ref, win_out, modp,
             ws0, ws1, cs0, cs1, cs2, ms0, ms1, ms2):
        g_w.start([win_out], (ws0, ws1))
        for phase in (g_c.start, g_c.forward, g_c.finish):
            phase([first_ref], [g0_ref], (cs0, cs1, cs2))
        for d in range(n_dev):
            call_ref[Bl * d:Bl * (d + 1), :] = g0_ref[rows * d:rows * d + Bl, :]
        c = call_ref[...]
        modp[...] = jnp.dot(c * _sigmoid(c), wada_ref[...], preferred_element_type=F32,
                            precision=lax.Precision.HIGH) + b_ref[...]
        for phase in (g_m.start, g_m.forward, g_m.finish):
            phase([modp], [gm_ref], (ms0, ms1, ms2))
        g_w.forward([win_out], (ws0, ws1))
        g_w.finish([win_out], (ws0, ws1))

    vmem = pl.BlockSpec(memory_space=pltpu.VMEM)
    anyspec = pl.BlockSpec(memory_space=pl.ANY)
    return pl.pallas_call(
        body, name="startup",
        in_specs=[vmem, vmem, vmem, anyspec], out_specs=[vmem, vmem, vmem, anyspec],
        out_shape=[jax.ShapeDtypeStruct((n_dev * rows, D), F32), jax.ShapeDtypeStruct((n_dev * Bl, D), F32),
                   jax.ShapeDtypeStruct((n_dev * n_dev * Bl, NA), F32),
                   jax.ShapeDtypeStruct(w_in_buf.shape, w_in_buf.dtype)],
        input_output_aliases={3: 3},
        scratch_shapes=[pltpu.VMEM((n_dev * Bl, NA), F32)] + g_w.scratch() + g_c.scratch() + g_m.scratch(),
        compiler_params=_cp(),
    )(first, w_ada, b_cols, w_in_buf)


def _ada_bwd(c_all, dmod_cols):
    def body(c_ref, d_ref, o_ref):
        c = c_ref[...]
        o_ref[...] = _dot_tn((c * _sigmoid(c)).astype(MXU_DTYPE), d_ref[...].astype(MXU_DTYPE))
    return pl.pallas_call(
        body, name="ada_bwd", out_shape=jax.ShapeDtypeStruct((c_all.shape[1], dmod_cols.shape[1]), F32),
        compiler_params=_cp(),
    )(c_all, dmod_cols)


def _small_reduce(gathered, n_dev, Bl):
    mod_rows, _, rows = _small_layout(Bl)
    width = gathered.shape[1]

    def body(g_ref, red_ref, bada_ref):
        acc = g_ref[0:rows, :]
        for d in range(1, n_dev):
            acc = acc + g_ref[d * rows:(d + 1) * rows, :]
        red_ref[...] = acc[mod_rows:, :]
        b = acc[0:8, :]
        for q in range(1, Bl):
            b = b + acc[8 * q:8 * q + 8, :]
        bada_ref[...] = b
    return pl.pallas_call(
        body, name="small_reduce",
        out_shape=[jax.ShapeDtypeStruct((rows - mod_rows, width), F32), jax.ShapeDtypeStruct((8, width), F32)],
        compiler_params=_cp(),
    )(gathered)


def _mesh_pos():
    return lax.axis_index("x"), lax.axis_index("y"), lax.axis_index("c")


def _other_chips(x, y):
    return [(1 - x, y), (x, 1 - y), (1 - x, 1 - y)]


class _WeightGather:
    def __init__(self, shapes):
        self.shapes = shapes
        self.n = len(shapes)

    def scratch(self):
        return [pltpu.SemaphoreType.DMA((6 * self.n,)), pltpu.SemaphoreType.DMA((6 * self.n,))]

    def _copy(self, outs, sems, w, k, slot, h, to):
        r2 = self.shapes[w][1] // 2
        blk = outs[w].at[slot, pl.ds(h * r2, r2), :]
        return pltpu.make_async_remote_copy(
            src_ref=blk, dst_ref=blk, send_sem=sems[0].at[6 * w + k], recv_sem=sems[1].at[6 * w + k],
            device_id=to, device_id_type=MESH_DEV)

    def start(self, outs, sems):
        x, y, c = _mesh_pos()
        for w in range(self.n):
            for k, chip in enumerate(_other_chips(x, y)):
                self._copy(outs, sems, w, k, 2 * x + y, c, (*chip, c)).start()

    def forward(self, outs, sems):
        x, y, c = _mesh_pos()
        for w in range(self.n):
            for k, chip in enumerate(_other_chips(x, y)):
                slot = 2 * chip[0] + chip[1]
                self._copy(outs, sems, w, k, slot, c, (x, y, 1 - c)).wait_recv()
                self._copy(outs, sems, w, 3 + k, slot, c, (x, y, 1 - c)).start()

    def finish(self, outs, sems):
        x, y, c = _mesh_pos()
        for w in range(self.n):
            for k, chip in enumerate(_other_chips(x, y)):
                slot = 2 * chip[0] + chip[1]
                self._copy(outs, sems, w, 3 + k, slot, 1 - c, (x, y, 1 - c)).wait_recv()
                self._copy(outs, sems, w, k, 2 * x + y, c, (*chip, c)).wait_send()
                self._copy(outs, sems, w, 3 + k, slot, c, (x, y, 1 - c)).wait_send()


class _SiblingExchange:
    def __init__(self, shapes):
        self.shapes = shapes

    def scratch(self):
        n = sum(s[0] for s in self.shapes)
        return [pltpu.SemaphoreType.DMA((n,)), pltpu.SemaphoreType.DMA((n,))]

    def out_shapes(self, dtype):
        return [jax.ShapeDtypeStruct((s[0], s[1] // 2, s[2]), dtype) for s in self.shapes]

    def _copies(self, ins, outs, sems):
        x, y, c = _mesh_pos()
        cps, k = [], 0
        for w, (P, R, _) in enumerate(self.shapes):
            r2 = R // 2
            for p in range(P):
                cps.append(pltpu.make_async_remote_copy(
                    src_ref=ins[w].at[p, pl.ds((1 - c) * r2, r2), :], dst_ref=outs[w].at[p],
                    send_sem=sems[0].at[k], recv_sem=sems[1].at[k],
                    device_id=(x, y, 1 - c), device_id_type=MESH_DEV))
                k += 1
        return cps

    def start(self, ins, outs, sems):
        for cp in self._copies(ins, outs, sems):
            cp.start()

    def forward(self, ins, outs, sems):
        pass

    def finish(self, ins, outs, sems):
        for cp in self._copies(ins, outs, sems):
            cp.wait()


def _sibling_host(grads):
    plan = _SiblingExchange([g.shape for g in grads])
    return plan, tuple(grads), tuple(plan.out_shapes(grads[0].dtype))


def _rs_sibling(grads, name):
    n = len(grads)
    plan, _, out_shapes = _sibling_host(grads)

    def body(*refs):
        ins, outs, sems = refs[:n], refs[n:2 * n], refs[2 * n:]
        plan.start(ins, outs, sems)
        plan.finish(ins, outs, sems)

    anyspec = pl.BlockSpec(memory_space=pl.ANY)
    return pl.pallas_call(
        body, name=name, out_shape=list(out_shapes),
        in_specs=[anyspec] * n, out_specs=[anyspec] * n, scratch_shapes=plan.scratch(),
    )(*grads)


class _SmallGather:
    def __init__(self, m_per):
        self.m = m_per

    def scratch(self):
        return [pltpu.SemaphoreType.DMA((7,)), pltpu.SemaphoreType.DMA((7,)), pltpu.SemaphoreType.DMA]

    def _rows(self, out, pos):
        px, py, pc = pos
        return out.at[pl.ds((4 * px + 2 * py + pc) * self.m, self.m), :]

    def _copy(self, out, sems, k, block, to, src=None):
        dst = self._rows(out, block)
        return pltpu.make_async_remote_copy(
            src_ref=dst if src is None else src, dst_ref=dst, send_sem=sems[0].at[k], recv_sem=sems[1].at[k],
            device_id=to, device_id_type=MESH_DEV)

    def start(self, ins, outs, sems):
        x, y, c = _mesh_pos()
        me = (x, y, c)
        pltpu.make_async_copy(ins[0], self._rows(outs[0], me), sems[2]).start()
        self._copy(outs[0], sems, 0, me, (x, y, 1 - c), src=ins[0]).start()
        for j, chip in enumerate(_other_chips(x, y)):
            self._copy(outs[0], sems, 1 + j, me, (*chip, c), src=ins[0]).start()

    def forward(self, ins, outs, sems):
        x, y, c = _mesh_pos()
        for j, chip in enumerate(_other_chips(x, y)):
            self._copy(outs[0], sems, 1 + j, (*chip, c), (x, y, c)).wait_recv()
            self._copy(outs[0], sems, 4 + j, (*chip, c), (x, y, 1 - c)).start()

    def finish(self, ins, outs, sems):
        x, y, c = _mesh_pos()
        me = (x, y, c)
        self._copy(outs[0], sems, 0, (x, y, 1 - c), me).wait_recv()
        for j, chip in enumerate(_other_chips(x, y)):
            self._copy(outs[0], sems, 4 + j, (*chip, 1 - c), me).wait_recv()
        self._copy(outs[0], sems, 0, me, (x, y, 1 - c), src=ins[0]).wait_send()
        for j, chip in enumerate(_other_chips(x, y)):
            self._copy(outs[0], sems, 1 + j, me, (*chip, c), src=ins[0]).wait_send()
            self._copy(outs[0], sems, 4 + j, (*chip, c), (x, y, 1 - c)).wait_send()
        pltpu.make_async_copy(ins[0], self._rows(outs[0], me), sems[2]).wait()


def _small_gather_host(packed):
    m, n = packed.shape
    return _SmallGather(m), (packed,), (jax.ShapeDtypeStruct((8 * m, n), packed.dtype),)


class _ChipExchange:
    def __init__(self, n):
        self.n = n

    def scratch(self):
        return [pltpu.SemaphoreType.DMA((3 * self.n,)), pltpu.SemaphoreType.DMA((3 * self.n,))]

    def _copies(self, ins, outs, sems):
        x, y, c = _mesh_pos()
        return [pltpu.make_async_remote_copy(
            src_ref=ins[w].at[2 * chip[0] + chip[1]], dst_ref=outs[w].at[k],
            send_sem=sems[0].at[3 * w + k], recv_sem=sems[1].at[3 * w + k],
            device_id=(*chip, c), device_id_type=MESH_DEV)
            for w in range(self.n) for k, chip in enumerate(_other_chips(x, y))]

    def start(self, ins, outs, sems):
        for cp in self._copies(ins, outs, sems):
            cp.start()

    def forward(self, ins, outs, sems):
        pass

    def finish(self, ins, outs, sems):
        for cp in self._copies(ins, outs, sems):
            cp.wait()


def _rs_final(bufs, name, chips=()):
    n, nc = len(bufs), len(chips)
    plan = _ChipExchange(nc)

    def body(*refs):
        cin = refs[n:n + nc]
        outs = refs[n + nc:2 * n + nc]
        cout = refs[2 * n + nc:2 * n + 2 * nc]
        send_sems, recv_sems = refs[2 * n + 2 * nc:2 * n + 2 * nc + 2]
        csems = refs[2 * n + 2 * nc + 2:]
        x, y, c = _mesh_pos()
        if nc:
            plan.start(cin, cout, csems)
        cps = []
        for w in range(n):
            r2 = bufs[w].shape[0] // 2
            mine = outs[w].at[pl.ds(c * r2, r2), :]
            cps.append(pltpu.make_async_remote_copy(
                src_ref=mine, dst_ref=mine, send_sem=send_sems.at[w], recv_sem=recv_sems.at[w],
                device_id=(x, y, 1 - c), device_id_type=MESH_DEV))
            cps[-1].start()
        for cp in cps:
            cp.wait()
        if nc:
            plan.finish(cin, cout, csems)

    anyspec = pl.BlockSpec(memory_space=pl.ANY)
    return pl.pallas_call(
        body, name=name,
        out_shape=[jax.ShapeDtypeStruct(b.shape, b.dtype) for b in bufs]
                  + [jax.ShapeDtypeStruct((3,) + s.shape[1:], s.dtype) for s in chips],
        in_specs=[anyspec] * (n + nc), out_specs=[anyspec] * (n + nc),
        input_output_aliases={w: w for w in range(n)},
        scratch_shapes=[pltpu.SemaphoreType.DMA((n,)), pltpu.SemaphoreType.DMA((n,))] + (plan.scratch() if nc else []),
    )(*bufs, *chips)


BIG = ("w_in", "w_out", "w_gate", "w_up", "w_down")
TRANSPOSED = ("w_gate", "w_up")
WEIGHTS = ("w_ada", "b_ada", "g_mix", "w_in", "w_dw", "b_dw", "g_conv_ln", "b_conv_ln", "g_q", "g_k",
           "w_out", "g_ffn", "w_gate", "w_up", "w_down")


def _pad_to(a, rows, cols):
    return jnp.pad(a, ((0, rows - a.shape[0]), (0, cols - a.shape[1])))


def kernel(x, c, w_ada, b_ada, g_mix, w_in, w_dw, b_dw, g_conv_ln, b_conv_ln, g_q, g_k, w_out, g_ffn, w_gate, w_up, w_down, loss_target, m_w_ada, m_b_ada, m_g_mix, m_w_in, m_w_dw, m_b_dw, m_g_conv_ln, m_b_conv_ln, m_g_q, m_g_k, m_w_out, m_g_ffn, m_w_gate, m_w_up, m_w_down, v_w_ada, v_b_ada, v_g_mix, v_w_in, v_w_dw, v_b_dw, v_g_conv_ln, v_b_conv_ln, v_g_q, v_g_k, v_w_out, v_g_ffn, v_w_gate, v_w_up, v_w_down):
    w = dict(w_ada=w_ada, b_ada=b_ada, g_mix=g_mix, w_in=w_in, w_dw=w_dw, b_dw=b_dw, g_conv_ln=g_conv_ln,
             b_conv_ln=b_conv_ln, g_q=g_q, g_k=g_k, w_out=w_out, g_ffn=g_ffn, w_gate=w_gate, w_up=w_up, w_down=w_down)
    m = dict(w_ada=m_w_ada, b_ada=m_b_ada, g_mix=m_g_mix, w_in=m_w_in, w_dw=m_w_dw, b_dw=m_b_dw, g_conv_ln=m_g_conv_ln,
             b_conv_ln=m_b_conv_ln, g_q=m_g_q, g_k=m_g_k, w_out=m_w_out, g_ffn=m_g_ffn, w_gate=m_w_gate, w_up=m_w_up,
             w_down=m_w_down)
    v = dict(w_ada=v_w_ada, b_ada=v_b_ada, g_mix=v_g_mix, w_in=v_w_in, w_dw=v_w_dw, b_dw=v_b_dw, g_conv_ln=v_g_conv_ln,
             b_conv_ln=v_b_conv_ln, g_q=v_g_q, g_k=v_g_k, w_out=v_w_out, g_ffn=v_g_ffn, w_gate=v_w_gate, w_up=v_w_up,
             w_down=v_w_down)
    Bl, S, D = x.shape
    DC = g_conv_ln.shape[1]
    NA = w_ada.shape[2]
    xi, yi, ci = _mesh_pos()
    p = 2 * xi + yi
    dev = 2 * p + ci
    n_dev = 8
    pidx = jnp.reshape(p, (1,)).astype(jnp.int32)
    pc_idx = jnp.stack([p, ci]).astype(jnp.int32)

    first = jnp.concatenate([_pad_to(c, 8, D), _pad_to(w_dw[0], CONV_ROWS, D)], axis=0)
    shard = lambda a, nm: a[0].T if nm in TRANSPOSED else a[0]
    owned = {nm: _cast_weight(shard(w[nm], nm), pidx, "cast_" + nm) for nm in BIG}
    b_cols = lax.dynamic_slice_in_dim(b_ada, p * NA, NA, axis=1)
    g0, c_all, gm, w_in_full = _startup(first, w_ada[0], b_cols, owned["w_in"], Bl=Bl)
    g0 = g0.reshape(n_dev, 8 + CONV_ROWS, D)
    taps = jnp.concatenate([g0[2 * q, 8:, :w_dw.shape[2]] for q in range(4)], axis=1)
    wdw = jnp.where(lax.broadcasted_iota(jnp.int32, taps.shape, 0) == CONV_WIDTH, b_dw, taps)
    gm = gm.reshape(n_dev, n_dev * Bl, NA)
    mod = jnp.concatenate([lax.dynamic_slice_in_dim(gm[2 * q], dev * Bl, Bl, axis=0) for q in range(4)], axis=1)

    loc = _local_step(x, loss_target, mod, g_mix, wdw, g_conv_ln, b_conv_ln, g_q, g_k, g_ffn,
                      w_in_full, owned["w_out"], owned["w_gate"], owned["w_up"], owned["w_down"], pc_idx=pc_idx)

    halves = [_final_add(s32, r, pc_idx, "final_add_" + nm)
              for nm, s32, r in zip(EARLY_WEIGHTS, loc["early_sums"], loc["early_recv"])]
    (late_sib,) = _rs_sibling([loc["grads"]["w_in"]], "rs_sibling_in")
    late32, late16 = _pair_add(loc["grads"]["w_in"], late_sib, pc_idx, "pair_add_w_in")
    *early_full, late_recv = _rs_final(halves, "rs_final_early", chips=(late16,))
    grad = dict(zip(EARLY_WEIGHTS, early_full))
    (grad["w_in"],) = _rs_final([_final_add(late32, late_recv, pc_idx, "final_add_w_in")], "rs_final_in")

    mod_rows, _, small_rows = _small_layout(Bl)
    gs = loc["gathered_small"]
    red, bada8 = _small_reduce(gs, n_dev, Bl)
    dmod_all = gs.reshape(n_dev, small_rows, D)[:, :mod_rows].reshape(n_dev * Bl, 8, D)[:, :N_MOD].reshape(n_dev * Bl, N_MOD * D)
    grad["w_ada"] = _ada_bwd(c_all, lax.dynamic_slice_in_dim(dmod_all, p * NA, NA, axis=1))
    grad["b_ada"] = bada8[:N_MOD].reshape(1, N_MOD * D)
    grad["g_mix"] = red[0:1]
    grad["g_ffn"] = red[1:2]
    grad["g_conv_ln"] = red[2:3, :DC]
    grad["b_conv_ln"] = red[2:3, DC:2 * DC]
    grad["g_q"] = red[3:4, :HEAD_DIM]
    grad["g_k"] = red[3:4, HEAD_DIM:2 * HEAD_DIM]
    loss = red[4, 0]
    dwdw = red[8:8 + CONV_ROWS, :DC]
    grad["w_dw"] = lax.dynamic_slice_in_dim(dwdw[:CONV_WIDTH], p * w_dw.shape[2], w_dw.shape[2], axis=1)
    grad["b_dw"] = dwdw[CONV_WIDTH:CONV_WIDTH + 1]

    delta, new_m, new_v = {}, {}, {}
    for nm in WEIGHTS:
        shp = w[nm].shape
        if nm in TRANSPOSED:
            d_, m_, v_ = _adamw(w[nm][0].T, grad[nm], m[nm][0].T, v[nm][0].T, "adamw_" + nm)
            grad[nm], delta[nm], new_m[nm], new_v[nm] = (a.T.reshape(shp) for a in (grad[nm], d_, m_, v_))
            continue
        two_d = (shp[-2], shp[-1]) if len(shp) == 3 else shp
        d_, m_, v_ = _adamw(w[nm].reshape(two_d), grad[nm].reshape(two_d), m[nm].reshape(two_d), v[nm].reshape(two_d),
                            "adamw_" + nm)
        grad[nm] = grad[nm].reshape(shp)
        delta[nm], new_m[nm], new_v[nm] = d_.reshape(shp), m_.reshape(shp), v_.reshape(shp)

    return (loss, loc["dx"], *[grad[nm] for nm in WEIGHTS], *[delta[nm] for nm in WEIGHTS],
            *[new_m[nm] for nm in WEIGHTS], *[new_v[nm] for nm in WEIGHTS])
```

```python
import functools

import jax
import jax.numpy as jnp
import numpy as np
from jax import lax
from jax.experimental import pallas as pl
from jax.experimental.pallas import tpu as pltpu

F32 = jnp.float32
MXU_DTYPE = jnp.bfloat16
ACT_DTYPE = jnp.bfloat16
EPS = 1e-6
NEG_INF = -1e30
HEAD_DIM = 64
LANES = 128
RADIUS = 64
QBLK = 128
DILATIONS = (1, 4, 16)
CONV_WIDTH = 31
CONV_PAD = CONV_WIDTH // 2
CONV_ROWS = 32
N_MOD = 6
ADAM_LR, ADAM_B1, ADAM_B2, ADAM_EPS, ADAM_WD, ADAM_STEP = 0.001, 0.9, 0.999, 1e-08, 0.01, 10
MESH_DEV = pl.DeviceIdType.MESH
VMEM_LIMIT = 56 << 20
ATTN_BWD_VMEM = 60 << 20


def _cp(sem=None, vmem=VMEM_LIMIT):
    kw = dict(vmem_limit_bytes=vmem)
    if sem is not None:
        kw["dimension_semantics"] = sem
    return pltpu.CompilerParams(**kw)


def _sigmoid(x):
    return 1.0 / (1.0 + jnp.exp(-x))


def _dot(a, b):
    return jnp.dot(a, b, preferred_element_type=F32)


def _dot_nt(a, b):
    return lax.dot_general(a, b, (((1,), (1,)), ((), ())), preferred_element_type=F32)


def _dot_tn(a, b):
    return lax.dot_general(a, b, (((0,), (0,)), ((), ())), preferred_element_type=F32)


def _colsum(v):
    return jnp.sum(v, axis=0, keepdims=True)


def _load_resident(i, pairs, sems):
    @pl.when(i == 0)
    def _():
        cps = [pltpu.make_async_copy(src, dst, sems.at[n]) for n, (src, dst) in enumerate(pairs)]
        for c in cps:
            c.start()
        for c in cps:
            c.wait()


def _fwd_in(x2, mod, g_mix, gq2, gk2, w_in, *, S, tm, n_ag):
    T, D = x2.shape
    P, _, Nb = w_in.shape
    n_in = P * Nb
    n_slab = (n_in - n_ag) // LANES
    NS = n_slab // 3
    tps = S // tm

    def body(x_ref, mod_ref, g_ref, gq_ref, gk_ref, w_ref, ag_ref, qkv_ref, qkh_ref, h_ref):
        x = x_ref[...]
        r = lax.rsqrt(jnp.mean(x * x, axis=-1, keepdims=True) + EPS)
        n = x * r * g_ref[...]
        h = n * (1.0 + mod_ref[:, D:2 * D]) + mod_ref[:, 0:D]
        hb = h.astype(MXU_DTYPE)
        h_ref[...] = hb
        parts = [_dot(hb, w_ref[p]) for p in range(P)]
        proj = jnp.concatenate(parts, axis=1) if P > 1 else parts[0]
        ag_ref[...] = proj[:, :n_ag]
        mm = _head_mean_matrix()
        for j in range(n_slab):
            v = proj[:, n_ag + LANES * j:n_ag + LANES * (j + 1)]
            qkv_ref[j] = v
            if j < 2 * NS:
                gain = gq_ref[...] * (HEAD_DIM ** -0.5 * LOG2E) if j < NS else gk_ref[...]
                qkh_ref[j] = v * lax.rsqrt(_head_mean(v * v, mm) + EPS) * gain

    return pl.pallas_call(
        body, grid=(T // tm,), name="fwd_in",
        in_specs=[pl.BlockSpec((tm, D), lambda i: (i, 0)),
                  pl.BlockSpec((None, 1, N_MOD * D), lambda i: (i // tps, 0, 0)),
                  pl.BlockSpec((1, D), lambda i: (0, 0)),
                  pl.BlockSpec((1, LANES), lambda i: (0, 0)), pl.BlockSpec((1, LANES), lambda i: (0, 0)),
                  pl.BlockSpec((P, D, Nb), lambda i: (0, 0, 0))],
        out_specs=[pl.BlockSpec((tm, n_ag), lambda i: (i, 0)),
                   pl.BlockSpec((n_slab, tm, LANES), lambda i: (0, i, 0)),
                   pl.BlockSpec((2 * NS, tm, LANES), lambda i: (0, i, 0)),
                   pl.BlockSpec((tm, D), lambda i: (i, 0))],
        out_shape=[jax.ShapeDtypeStruct((T, n_ag), F32),
                   jax.ShapeDtypeStruct((n_slab, T, LANES), F32),
                   jax.ShapeDtypeStruct((2 * NS, T, LANES), F32),
                   jax.ShapeDtypeStruct((T, D), MXU_DTYPE)],
        compiler_params=_cp(("arbitrary",)),
    )(x2, mod, g_mix, gq2, gk2, w_in)


CONV_CH = 128


def _conv_taps(win, w_ref, acc, reverse):
    n = win.shape[0]
    for b in range(8):
        wb = win if b == 0 else pltpu.roll(win, shift=n - b, axis=0)
        for a in range(4):
            o = 8 * a + b
            if o < 1 or o > CONV_WIDTH:
                continue
            k = (CONV_WIDTH - o) if reverse else (o - 1)
            acc = acc + w_ref[k:k + 1, :] * wb[8 * a:8 * a + CONV_CH, :]
    return acc


def _conv_fwd(ag, wdw, *, Bl, S, DC):
    T = ag.shape[0]
    nsc = DC // LANES
    CH = CONV_CH

    def body(a_ref, g_ref, w_ref, cv_ref, upad):
        zeros16 = jnp.zeros((16, LANES), F32)
        upad[0:16, :] = zeros16
        upad[S + 16:S + 32, :] = zeros16

        def fill(i, _):
            r0 = pl.multiple_of(i * CH, CH)
            a = a_ref[pl.ds(r0, CH), :]
            g = g_ref[pl.ds(r0, CH), :]
            upad[pl.ds(r0 + 16, CH), :] = a * _sigmoid(g)
            return 0
        lax.fori_loop(0, S // CH, fill, 0)

        def conv(i, _):
            r0 = pl.multiple_of(i * CH, CH)
            win = upad[pl.ds(r0, CH + 32), :]
            acc = jnp.zeros((CH, LANES), F32) + w_ref[CONV_WIDTH:CONV_WIDTH + 1, :]
            cv_ref[pl.ds(r0, CH), :] = _conv_taps(win, w_ref, acc, reverse=False)
            return 0
        lax.fori_loop(0, S // CH, conv, 0)

    return pl.pallas_call(
        body, grid=(Bl, nsc), name="conv_fwd",
        in_specs=[pl.BlockSpec((S, LANES), lambda b, j: (b, j)),
                  pl.BlockSpec((S, LANES), lambda b, j: (b, nsc + j)),
                  pl.BlockSpec((CONV_ROWS, LANES), lambda b, j: (0, j))],
        out_specs=pl.BlockSpec((S, LANES), lambda b, j: (b, j)),
        out_shape=jax.ShapeDtypeStruct((T, DC), F32),
        scratch_shapes=[pltpu.VMEM((S + 32, LANES), F32)],
        compiler_params=_cp(("arbitrary", "arbitrary")),
    )(ag, ag, wdw)


def _conv_bwd(ag, dcv, wdw, *, Bl, S, DC):
    T = ag.shape[0]
    nsc = DC // LANES
    CH = CONV_CH

    def body(a_ref, g_ref, d_ref, w_ref, da_ref, dg_ref, dw_ref, upad, dpad, wacc):
        b = pl.program_id(1)
        zeros16 = jnp.zeros((16, LANES), F32)
        upad[0:16, :] = zeros16
        upad[S + 16:S + 32, :] = zeros16
        dpad[0:16, :] = zeros16
        dpad[S + 16:S + 32, :] = zeros16

        @pl.when(b == 0)
        def _():
            wacc[...] = jnp.zeros_like(wacc)

        def fill(i, _):
            r0 = pl.multiple_of(i * CH, CH)
            a = a_ref[pl.ds(r0, CH), :]
            g = g_ref[pl.ds(r0, CH), :]
            upad[pl.ds(r0 + 16, CH), :] = a * _sigmoid(g)
            dpad[pl.ds(r0 + 16, CH), :] = d_ref[pl.ds(r0, CH), :]
            return 0
        lax.fori_loop(0, S // CH, fill, 0)

        def step(i, _):
            r0 = pl.multiple_of(i * CH, CH)
            dwin = dpad[pl.ds(r0, CH + 32), :]
            du = _conv_taps(dwin, w_ref, jnp.zeros((CH, LANES), F32), reverse=True)
            a = a_ref[pl.ds(r0, CH), :]
            g = g_ref[pl.ds(r0, CH), :]
            sg = _sigmoid(g)
            da_ref[pl.ds(r0, CH), :] = du * sg
            dg_ref[pl.ds(r0, CH), :] = du * a * sg * (1.0 - sg)
            dc = d_ref[pl.ds(r0, CH), :]
            uwin = upad[pl.ds(r0, CH + 32), :]
            n = CH + 32
            for bb in range(8):
                wb = uwin if bb == 0 else pltpu.roll(uwin, shift=n - bb, axis=0)
                for aa in range(4):
                    o = 8 * aa + bb
                    if o < 1 or o > CONV_WIDTH:
                        continue
                    k = o - 1
                    prod = dc * wb[8 * aa:8 * aa + CH, :]
                    part = prod[0:8, :]
                    for q in range(1, CH // 8):
                        part = part + prod[8 * q:8 * q + 8, :]
                    wacc[8 * k:8 * k + 8, :] += part
            part = dc[0:8, :]
            for q in range(1, CH // 8):
                part = part + dc[8 * q:8 * q + 8, :]
            wacc[8 * CONV_WIDTH:8 * CONV_WIDTH + 8, :] += part
            return 0
        lax.fori_loop(0, S // CH, step, 0)

        @pl.when(b == Bl - 1)
        def _():
            for k in range(CONV_ROWS):
                dw_ref[k:k + 1, :] = jnp.sum(wacc[8 * k:8 * k + 8, :], axis=0, keepdims=True)

    return pl.pallas_call(
        body, grid=(nsc, Bl), name="conv_bwd",
        in_specs=[pl.BlockSpec((S, LANES), lambda j, b: (b, j)),
                  pl.BlockSpec((S, LANES), lambda j, b: (b, nsc + j)),
                  pl.BlockSpec((S, LANES), lambda j, b: (b, j)),
                  pl.BlockSpec((CONV_ROWS, LANES), lambda j, b: (0, j))],
        out_specs=[pl.BlockSpec((S, LANES), lambda j, b: (b, j)),
                   pl.BlockSpec((S, LANES), lambda j, b: (b, j)),
                   pl.BlockSpec((CONV_ROWS, LANES), lambda j, b: (0, j))],
        out_shape=[jax.ShapeDtypeStruct((T, DC), F32), jax.ShapeDtypeStruct((T, DC), F32),
                   jax.ShapeDtypeStruct((CONV_ROWS, DC), F32)],
        scratch_shapes=[pltpu.VMEM((S + 32, LANES), F32), pltpu.VMEM((S + 32, LANES), F32),
                        pltpu.VMEM((8 * CONV_ROWS, LANES), F32)],
        compiler_params=_cp(("arbitrary", "arbitrary")),
    )(ag, ag, dcv, wdw)


ROWCH = 256


LOG2E = 1.4426950408889634
LN2 = 0.6931471805599453
N_EDGE = 4


def _head_mean_matrix():
    r = lax.broadcasted_iota(jnp.int32, (LANES, LANES), 0) // HEAD_DIM
    c = lax.broadcasted_iota(jnp.int32, (LANES, LANES), 1) // HEAD_DIM
    return jnp.where(r == c, 1.0 / HEAD_DIM, 0.0).astype(jnp.bfloat16)


def _head_mean(v, mm):
    hi = v.astype(jnp.bfloat16)
    lo = (v - hi.astype(F32)).astype(jnp.bfloat16)
    return _dot(hi, mm) + _dot(lo, mm)


def _stack_heads(blk, lane_lo):
    z = jnp.zeros_like(blk)
    return jnp.concatenate([jnp.where(lane_lo, blk, z), jnp.where(lane_lo, z, blk)], axis=0)


def _merge_heads(v2, lane_lo):
    return jnp.where(lane_lo, v2[:QBLK], v2[QBLK:])


def _bias_tables(bias_ref, slope_ref):
    row = lax.broadcasted_iota(jnp.int32, (2 * QBLK, 2 * QBLK), 0)
    col = lax.broadcasted_iota(jnp.int32, (2 * QBLK, 2 * QBLK), 1)
    rel = jnp.abs(col - RADIUS - (row % QBLK))
    slope = jnp.where(row < QBLK, slope_ref[0:1, 0:1], slope_ref[0:1, HEAD_DIM:HEAD_DIM + 1]) * LOG2E
    for pi, d in enumerate(DILATIONS):
        inside = jnp.where(rel <= RADIUS, -slope * (float(d) * rel.astype(F32)), NEG_INF)
        for e in range(N_EDGE):
            t = inside
            if e & 1:
                t = jnp.where(col < RADIUS, NEG_INF, t)
            if e & 2:
                t = jnp.where(col >= QBLK + RADIUS, NEG_INF, t)
            bias_ref[N_EDGE * pi + e] = t


def _edge_index(qb, nb):
    return jnp.where(qb == 0, 1, 0) + jnp.where(qb == nb - 1, 2, 0)


VIA = 4


def _residue(d, s):
    return (s % VIA) * VIA + s // VIA if d == VIA * VIA else s


def _gather_rows(src_ref, dst_ref, S, d, pad, f32_copy=None):
    n = S // d
    seg = n + 2 * RADIUS if pad else n
    step = min(n, 512)
    two_step = d == VIA * VIA and f32_copy is not None
    for s in range(d):
        base = s * seg
        if pad:
            dst_ref[base:base + RADIUS, :] = jnp.zeros((RADIUS, LANES), dst_ref.dtype)
            dst_ref[base + RADIUS + n:base + seg, :] = jnp.zeros((RADIUS, LANES), dst_ref.dtype)
            base += RADIUS
        for c0 in range(0, n, step):
            if d == 1:
                v = src_ref[c0:c0 + step, :]
            elif two_step:
                v = f32_copy[pl.ds((s // VIA) * (S // VIA) + s % VIA + c0 * VIA, step, stride=VIA), :]
            else:
                v = src_ref[pl.ds(_residue(d, s) + c0 * d, step, stride=d), :]
                if d == VIA and f32_copy is not None:
                    f32_copy[s * n + c0:s * n + c0 + step, :] = v
            dst_ref[base + c0:base + c0 + step, :] = v.astype(dst_ref.dtype)


def _scatter_rows(src_ref, dst_ref, S, d, pad, accumulate, f32_tmp=None):
    n = S // d
    seg = n + 2 * RADIUS if pad else n
    first = RADIUS if pad else 0
    _unpermute(lambda s, c0, step: src_ref[s * seg + first + c0:s * seg + first + c0 + step, :],
               dst_ref, S, d, accumulate, f32_tmp)


def _unpermute(rows_of, dst_ref, S, d, accumulate, f32_tmp):
    n = S // d
    step = min(n, 512)
    if d == VIA * VIA and f32_tmp is not None:
        for s in range(d):
            f32_tmp[pl.ds((s // VIA) * (S // VIA) + s % VIA, n, stride=VIA), :] = rows_of(s, 0, n)
        n4 = S // VIA
        _unpermute(lambda s, c0, st: f32_tmp[s * n4 + c0:s * n4 + c0 + st, :], dst_ref, S, VIA, accumulate, None)
        return
    for s in range(d):
        for c0 in range(0, n, step):
            v = rows_of(s, c0, step)
            idx = pl.ds(c0, step) if d == 1 else pl.ds(_residue(d, s) + c0 * d, step, stride=d)
            if accumulate:
                dst_ref[idx, :] = dst_ref[idx, :] + v
            else:
                dst_ref[idx, :] = v


def _zero_uncovered(acc, S, d):
    n = S // d
    if (n // QBLK) % 2:
        return
    seg = n + 2 * RADIUS
    for r in range(d):
        acc[0, r * seg + n:r * seg + seg, :] = jnp.zeros((2 * RADIUS, LANES), F32)
        acc[1, r * seg:r * seg + 2 * RADIUS, :] = jnp.zeros((2 * RADIUS, LANES), F32)


def _scatter_parity(acc, dst_ref, S, d, f32_tmp=None):
    n = S // d
    seg = n + 2 * RADIUS
    one_block = (n // QBLK) % 2 == 1

    def rows_of(s, c0, step):
        rows = slice(s * seg + RADIUS + c0, s * seg + RADIUS + c0 + step)
        return acc[s % 2, rows, :] if one_block else acc[0, rows, :] + acc[1, rows, :]

    _unpermute(rows_of, dst_ref, S, d, True, f32_tmp)


PIPE_UNROLL = 4
PIPE_SLOTS = 16
BWD_SLOTS = 12


def _pipeline(n_items, stages, unroll):
    K = len(stages)
    assert n_items % unroll == 0 and K * unroll <= (PIPE_SLOTS if K == 4 else BWD_SLOTS)
    trips = n_items // unroll
    assert trips >= K - 1

    def trip(t, static):
        for s in reversed(range(K)):
            if static and not 0 <= t - s < trips:
                continue
            for u in range(unroll):
                item = unroll * (t - s) + u
                stages[s](jnp.int32(item) if static else item)

    for t in range(K - 1):
        trip(t, True)

    def full(t, carry):
        trip(t, False)
        return carry
    lax.fori_loop(K - 1, trips, full, 0)
    for t in range(trips, trips + K - 1):
        trip(t, True)


def _attn_fwd(qkh, qkv, slopes, *, Bl, S, hosted=()):
    n3, T, _ = qkv.shape
    NS = n3 // 3
    NB = S // QBLK
    PADR = S + 2 * RADIUS * DILATIONS[-1]
    nh = len(hosted)
    plan = _WeightGather([b.shape for b in hosted]) if nh else None
    n_steps = Bl * NS

    def body(qh, kh, v_ref, slope_ref, *rest):
        o_ref, lse_ref = rest[nh:nh + 2]
        wouts = rest[nh + 2:2 * nh + 2]
        (qp, kp, vp, op, lp, onat, lnat, bias_ref, sbuf, pbuf, mbuf, lbuf, tmps) = rest[2 * nh + 2:2 * nh + 15]
        sems = rest[2 * nh + 15:]
        step = pl.program_id(0) * Bl + pl.program_id(1)
        if nh:
            @pl.when(step == 0)
            def _():
                plan.start(wouts, sems)

            @pl.when(step == (3 * n_steps) // 4)
            def _():
                plan.forward(wouts, sems)

        lane_lo = lax.broadcasted_iota(jnp.int32, (QBLK, LANES), 1) < HEAD_DIM

        @pl.when(pl.program_id(1) == 0)
        def _():
            _bias_tables(bias_ref, slope_ref)

        for pi, d in enumerate(DILATIONS):
            n = S // d
            nb = n // QBLK
            _gather_rows(qh, qp, S, d, pad=False, f32_copy=tmps.at[0])
            _gather_rows(kh, kp, S, d, pad=True, f32_copy=tmps.at[1])
            _gather_rows(v_ref, vp, S, d, pad=True, f32_copy=tmps.at[2])

            def offsets(i, nb=nb):
                r = i // nb
                return pl.multiple_of(i * QBLK, QBLK), pl.multiple_of((i + r) * QBLK, QBLK), i % nb

            def scores(i, pi=pi, nb=nb):
                q0, k0, qb = offsets(i)
                qs = _stack_heads(qp[pl.ds(q0, QBLK), :], lane_lo)
                sbuf[i % PIPE_SLOTS] = (_dot_nt(qs, kp[pl.ds(k0, 2 * QBLK), :])
                                        + bias_ref[N_EDGE * pi + _edge_index(qb, nb)])

            def rowmax(i):
                m = jnp.max(sbuf[i % PIPE_SLOTS], axis=1, keepdims=True)
                mbuf[i % PIPE_SLOTS] = jnp.broadcast_to(m, (2 * QBLK, LANES))

            def expsum(i):
                m = mbuf[i % PIPE_SLOTS]
                p = jnp.exp2(sbuf[i % PIPE_SLOTS] - jnp.concatenate([m, m], axis=1))
                pbuf[i % PIPE_SLOTS] = p.astype(MXU_DTYPE)
                lbuf[i % PIPE_SLOTS] = jnp.broadcast_to(jnp.sum(p, axis=1, keepdims=True), (2 * QBLK, LANES))

            def values(i):
                q0, k0, _ = offsets(i)
                l = lbuf[i % PIPE_SLOTS]
                o2 = _dot(pbuf[i % PIPE_SLOTS], vp[pl.ds(k0, 2 * QBLK), :]) * (1.0 / l)
                op[pl.ds(q0, QBLK), :] = _merge_heads(o2, lane_lo)
                lp[pl.ds(q0, QBLK), :] = _merge_heads(mbuf[i % PIPE_SLOTS] + jnp.log2(l), lane_lo)

            _pipeline(NB, [scores, rowmax, expsum, values], PIPE_UNROLL)
            _scatter_rows(op, onat.at[pi], S, d, pad=False, accumulate=False, f32_tmp=tmps.at[0])
            _scatter_rows(lp, lnat.at[pi], S, d, pad=False, accumulate=False, f32_tmp=tmps.at[1])

        for c0 in range(0, S, ROWCH):
            ls = [lnat[pi, c0:c0 + ROWCH, :] for pi in range(len(DILATIONS))]
            mx = jnp.maximum(jnp.maximum(ls[0], ls[1]), ls[2])
            es = [jnp.exp2(l - mx) for l in ls]
            tot = es[0] + es[1] + es[2]
            inv = 1.0 / tot
            acc = (es[0] * inv) * onat[0, c0:c0 + ROWCH, :]
            for pi in (1, 2):
                acc = acc + (es[pi] * inv) * onat[pi, c0:c0 + ROWCH, :]
            o_ref[c0:c0 + ROWCH, :] = acc
            lse_ref[c0:c0 + ROWCH, :] = mx + jnp.log2(tot)

        if nh:
            @pl.when(step == n_steps - 1)
            def _():
                plan.finish(wouts, sems)

    spec_in = lambda off: pl.BlockSpec((None, S, LANES), lambda j, b: (off * NS + j, b, 0))
    out = pl.BlockSpec((S, LANES), lambda j, b: (b, j))
    anyspec = pl.BlockSpec(memory_space=pl.ANY)
    return pl.pallas_call(
        body, grid=(NS, Bl), name="attn_fwd",
        in_specs=[spec_in(0), spec_in(1), spec_in(2),
                  pl.BlockSpec((None, 8, LANES), lambda j, b: (j, 0, 0))] + [anyspec] * nh,
        out_specs=[out, out] + [anyspec] * nh,
        out_shape=[jax.ShapeDtypeStruct((T, NS * LANES), F32)] * 2
                  + [jax.ShapeDtypeStruct(b.shape, b.dtype) for b in hosted],
        input_output_aliases={4 + w: 2 + w for w in range(nh)},
        scratch_shapes=[pltpu.VMEM((S, LANES), MXU_DTYPE), pltpu.VMEM((PADR, LANES), MXU_DTYPE),
                        pltpu.VMEM((PADR, LANES), MXU_DTYPE),
                        pltpu.VMEM((S, LANES), F32), pltpu.VMEM((S, LANES), F32),
                        pltpu.VMEM((3, S, LANES), F32), pltpu.VMEM((3, S, LANES), F32),
                        pltpu.VMEM((N_EDGE * len(DILATIONS), 2 * QBLK, 2 * QBLK), F32),
                        pltpu.VMEM((PIPE_SLOTS, 2 * QBLK, 2 * QBLK), F32),
                        pltpu.VMEM((PIPE_SLOTS, 2 * QBLK, 2 * QBLK), MXU_DTYPE),
                        pltpu.VMEM((PIPE_SLOTS, 2 * QBLK, LANES), F32), pltpu.VMEM((PIPE_SLOTS, 2 * QBLK, LANES), F32),
                        pltpu.VMEM((3, S, LANES), F32)]
                       + (plan.scratch() if nh else []),
        compiler_params=_cp(("arbitrary", "arbitrary")),
    )(qkh, qkh, qkv, slopes, *hosted)


def _attn_bwd(qkh, qkv, o, lse, do, gq2, gk2, slopes, *, Bl, S, hosted=()):
    n3, T, _ = qkv.shape
    NS = n3 // 3
    NB = S // QBLK
    PADR = S + 2 * RADIUS * DILATIONS[-1]
    QSCALE = HEAD_DIM ** -0.5
    nh = len(hosted)
    plan = _ChipExchange(nh)
    n_steps = Bl * NS

    def body(qh, kh, q_ref, k_ref, v_ref, o_ref, lse_ref, do_ref, gq_ref, gk_ref, slope_ref, *rest):
        hin = rest[:nh]
        dq_ref, dk_ref, dv_ref, gacc_ref = rest[nh:nh + 4]
        hout = rest[nh + 4:2 * nh + 4]
        (ld, qp, kp, vp, dop, ldp, dqp, dkacc, dvacc, dqn, dkn, bias_ref,
         sbuf, dpbuf, pbuf, dsbuf, tmps) = rest[2 * nh + 4:2 * nh + 21]
        sems = rest[2 * nh + 21:]
        step = pl.program_id(0) * Bl + pl.program_id(1)

        @pl.when(step == 0)
        def _():
            gacc_ref[...] = jnp.zeros_like(gacc_ref)
            if nh:
                plan.start(hin, hout, sems)

        mm = _head_mean_matrix()
        lane_lo = lax.broadcasted_iota(jnp.int32, (QBLK, LANES), 1) < HEAD_DIM

        @pl.when(pl.program_id(1) == 0)
        def _():
            _bias_tables(bias_ref, slope_ref)

        lse_lanes = lax.broadcasted_iota(jnp.int32, (ROWCH, LANES), 1) % HEAD_DIM < HEAD_DIM // 2
        for c0 in range(0, S, ROWCH):
            delta = _head_mean(do_ref[c0:c0 + ROWCH, :] * o_ref[c0:c0 + ROWCH, :], mm) * HEAD_DIM
            ld[c0:c0 + ROWCH, :] = jnp.where(lse_lanes, lse_ref[c0:c0 + ROWCH, :], delta)
            dqn[c0:c0 + ROWCH, :] = jnp.zeros((ROWCH, LANES), F32)
            dkn[c0:c0 + ROWCH, :] = jnp.zeros((ROWCH, LANES), F32)
            dv_ref[c0:c0 + ROWCH, :] = jnp.zeros((ROWCH, LANES), F32)

        for pi, d in enumerate(DILATIONS):
            n = S // d
            nb = n // QBLK
            _gather_rows(qh, qp, S, d, pad=False, f32_copy=tmps.at[0])
            _gather_rows(kh, kp, S, d, pad=True, f32_copy=tmps.at[1])
            _gather_rows(v_ref, vp, S, d, pad=True, f32_copy=tmps.at[2])
            _gather_rows(do_ref, dop, S, d, pad=False, f32_copy=tmps.at[3])
            _gather_rows(ld, ldp, S, d, pad=False, f32_copy=tmps.at[4])
            _zero_uncovered(dkacc, S, d)
            _zero_uncovered(dvacc, S, d)

            def offsets(i, nb=nb):
                r = i // nb
                return pl.multiple_of(i * QBLK, QBLK), pl.multiple_of((i + r) * QBLK, QBLK), i % nb

            def scores(i, pi=pi, nb=nb):
                q0, k0, qb = offsets(i)
                qs = _stack_heads(qp[pl.ds(q0, QBLK), :], lane_lo)
                dos = _stack_heads(dop[pl.ds(q0, QBLK), :], lane_lo)
                sbuf[i % BWD_SLOTS] = (_dot_nt(qs, kp[pl.ds(k0, 2 * QBLK), :])
                                       + bias_ref[N_EDGE * pi + _edge_index(qb, nb)])
                dpbuf[i % BWD_SLOTS] = _dot_nt(dos, vp[pl.ds(k0, 2 * QBLK), :])

            def probs(i):
                q0, _, _ = offsets(i)
                blk = ldp[pl.ds(q0, QBLK), :]
                half = HEAD_DIM // 2
                lcol = jnp.concatenate([blk[:, 0:1], blk[:, HEAD_DIM:HEAD_DIM + 1]], axis=0)
                dcol = jnp.concatenate([blk[:, half:half + 1], blk[:, HEAD_DIM + half:HEAD_DIM + half + 1]], axis=0)
                p = jnp.exp2(sbuf[i % BWD_SLOTS] - lcol)
                pbuf[i % BWD_SLOTS] = p.astype(MXU_DTYPE)
                dsbuf[i % BWD_SLOTS] = (p * (dpbuf[i % BWD_SLOTS] - dcol)).astype(MXU_DTYPE)

            def grads(i):
                q0, k0, _ = offsets(i)
                qs = _stack_heads(qp[pl.ds(q0, QBLK), :], lane_lo)
                dos = _stack_heads(dop[pl.ds(q0, QBLK), :], lane_lo)
                ds = dsbuf[i % BWD_SLOTS]
                dvacc[i % 2, pl.ds(k0, 2 * QBLK), :] = _dot_tn(pbuf[i % BWD_SLOTS], dos)
                dkacc[i % 2, pl.ds(k0, 2 * QBLK), :] = _dot_tn(ds, qs)
                dqp[pl.ds(q0, QBLK), :] = _merge_heads(_dot(ds, kp[pl.ds(k0, 2 * QBLK), :]), lane_lo)

            _pipeline(NB, [scores, probs, grads], PIPE_UNROLL)
            _scatter_rows(dqp, dqn, S, d, pad=False, accumulate=True, f32_tmp=tmps.at[0])
            _scatter_parity(dkacc, dkn, S, d, f32_tmp=tmps.at[1])
            _scatter_parity(dvacc, dv_ref, S, d, f32_tmp=tmps.at[2])

        gq_sum = jnp.zeros((8, LANES), F32)
        gk_sum = jnp.zeros((8, LANES), F32)
        for c0 in range(0, S, ROWCH):
            for src_ref, dn, g_ref, dst_ref, scale, is_q in ((q_ref, dqn, gq_ref, dq_ref, QSCALE, True),
                                                             (k_ref, dkn, gk_ref, dk_ref, LN2, False)):
                x = src_ref[c0:c0 + ROWCH, :]
                dh = dn[c0:c0 + ROWCH, :]
                rr = lax.rsqrt(_head_mean(x * x, mm) + EPS)
                e = dh * (g_ref[...] * scale)
                dst_ref[c0:c0 + ROWCH, :] = rr * e - x * (rr * rr * rr) * _head_mean(e * x, mm)
                gpart = dh * (x * rr * scale)
                acc8 = gpart[0:8, :]
                for q8 in range(1, ROWCH // 8):
                    acc8 = acc8 + gpart[8 * q8:8 * q8 + 8, :]
                if is_q:
                    gq_sum = gq_sum + acc8
                else:
                    gk_sum = gk_sum + acc8
        gacc_ref[0:1, :] += jnp.sum(gq_sum, axis=0, keepdims=True)
        gacc_ref[1:2, :] += jnp.sum(gk_sum, axis=0, keepdims=True)

        if nh:
            @pl.when(step == n_steps - 1)
            def _():
                plan.finish(hin, hout, sems)

    spec_in = lambda off: pl.BlockSpec((None, S, LANES), lambda j, b: (off * NS + j, b, 0))
    tok = pl.BlockSpec((S, LANES), lambda j, b: (b, j))
    vec = pl.BlockSpec((1, LANES), lambda j, b: (0, 0))
    slab_out = pl.BlockSpec((None, S, LANES), lambda j, b: (j, b, 0))
    f32buf = lambda rows: pltpu.VMEM((rows, LANES), F32)
    bfbuf = lambda rows: pltpu.VMEM((rows, LANES), MXU_DTYPE)
    anyspec = pl.BlockSpec(memory_space=pl.ANY)
    return pl.pallas_call(
        body, grid=(NS, Bl), name="attn_bwd",
        in_specs=[spec_in(0), spec_in(1), spec_in(0), spec_in(1), spec_in(2), tok, tok, tok, vec, vec,
                  pl.BlockSpec((None, 8, LANES), lambda j, b: (j, 0, 0))] + [anyspec] * nh,
        out_specs=[slab_out, slab_out, slab_out, pl.BlockSpec((8, LANES), lambda j, b: (0, 0))] + [anyspec] * nh,
        out_shape=[jax.ShapeDtypeStruct((NS, T, LANES), F32)] * 3 + [jax.ShapeDtypeStruct((8, LANES), F32)]
                  + [jax.ShapeDtypeStruct((3,) + h.shape[1:], h.dtype) for h in hosted],
        scratch_shapes=[f32buf(S),
                        bfbuf(S), bfbuf(PADR), bfbuf(PADR), bfbuf(S),
                        f32buf(S), f32buf(S),
                        pltpu.VMEM((2, PADR, LANES), F32), pltpu.VMEM((2, PADR, LANES), F32),
                        f32buf(S), f32buf(S),
                        pltpu.VMEM((N_EDGE * len(DILATIONS), 2 * QBLK, 2 * QBLK), F32),
                        pltpu.VMEM((BWD_SLOTS, 2 * QBLK, 2 * QBLK), F32),
                        pltpu.VMEM((BWD_SLOTS, 2 * QBLK, 2 * QBLK), F32),
                        pltpu.VMEM((BWD_SLOTS, 2 * QBLK, 2 * QBLK), MXU_DTYPE),
                        pltpu.VMEM((BWD_SLOTS, 2 * QBLK, 2 * QBLK), MXU_DTYPE),
                        pltpu.VMEM((5, S, LANES), F32)]
                       + (plan.scratch() if nh else []),
        compiler_params=_cp(("arbitrary", "arbitrary"), vmem=ATTN_BWD_VMEM),
    )(qkh, qkh, qkv, qkv, qkv, o, lse, do, gq2, gk2, slopes, *hosted)


def _layer_norm_parts(cv, g_ln, b_ln):
    mu = jnp.mean(cv, axis=-1, keepdims=True)
    cen = cv - mu
    rs = lax.rsqrt(jnp.mean(cen * cen, axis=-1, keepdims=True) + EPS)
    z = cen * rs
    return z, rs, z * g_ln + b_ln


def _ffn_fwd(x2, cv, ya, tgt, mod, g_ln, b_ln, g_ffn, w_out, w_gate, w_up, w_down, *, S, tm):
    T, D = x2.shape
    DC = cv.shape[1]
    P, Kb, _ = w_out.shape
    Fb = w_down.shape[1]
    tps = S // tm

    def body(x_ref, cv_ref, ya_ref, t_ref, mod_ref, gln_ref, bln_ref, gf_ref, wo_hbm, wg_hbm, wu_hbm, wd_hbm,
             x1_ref, ycat_ref, mix_ref, h2_ref, g_ref, u_ref, a_ref, f_ref, dy_ref, loss_ref,
             wo, wg, wu, wd, sems):
        i = pl.program_id(0)
        _load_resident(i, [(wo_hbm, wo), (wg_hbm, wg), (wu_hbm, wu), (wd_hbm, wd)], sems)

        @pl.when(i == 0)
        def _():
            loss_ref[...] = jnp.zeros_like(loss_ref)

        _, _, ln = _layer_norm_parts(cv_ref[...], gln_ref[...], bln_ref[...])
        yc = ln * _sigmoid(ln)
        ycat = jnp.concatenate([yc, ya_ref[...]], axis=1).astype(MXU_DTYPE)
        ycat_ref[...] = ycat
        mix = _dot(ycat[:, 0:Kb], wo[0])
        for p in range(1, P):
            mix = mix + _dot(ycat[:, Kb * p:Kb * (p + 1)], wo[p])
        mix_ref[...] = mix.astype(ACT_DTYPE)
        x1 = x_ref[...] + mod_ref[:, 2 * D:3 * D] * mix
        x1_ref[...] = x1
        r2 = lax.rsqrt(jnp.mean(x1 * x1, axis=-1, keepdims=True) + EPS)
        h2 = (x1 * r2 * gf_ref[...]) * (1.0 + mod_ref[:, 4 * D:5 * D]) + mod_ref[:, 3 * D:4 * D]
        h2b = h2.astype(MXU_DTYPE)
        h2_ref[...] = h2b
        f = jnp.zeros((tm, D), F32)
        for p in range(P):
            g = _dot_nt(h2b, wg[p])
            u = _dot_nt(h2b, wu[p])
            a = (g * _sigmoid(g) * u).astype(MXU_DTYPE)
            g_ref[p] = g.astype(ACT_DTYPE)
            u_ref[p] = u.astype(ACT_DTYPE)
            a_ref[p] = a
            f = f + _dot(a, wd[p])
        f_ref[...] = f.astype(ACT_DTYPE)
        err = x1 + mod_ref[:, 5 * D:6 * D] * f - t_ref[...]
        dy_ref[...] = err * (1.0 / D)
        tot = jnp.sum(_colsum(err * err), axis=1, keepdims=True)
        loss_ref[...] += tot * (0.5 / D)

    row = lambda w: pl.BlockSpec((tm, w), lambda i: (i, 0))
    vec = lambda w: pl.BlockSpec((1, w), lambda i: (0, 0))
    blk = pl.BlockSpec((P, tm, Fb), lambda i: (0, i, 0))
    anyspec = pl.BlockSpec(memory_space=pl.ANY)
    return pl.pallas_call(
        body, grid=(T // tm,), name="ffn_fwd",
        in_specs=[row(D), row(DC), row(D - DC), row(D),
                  pl.BlockSpec((None, 1, N_MOD * D), lambda i: (i // tps, 0, 0)),
                  vec(DC), vec(DC), vec(D), anyspec, anyspec, anyspec, anyspec],
        out_specs=[row(D), row(D), row(D), row(D), blk, blk, blk, row(D), row(D),
                   pl.BlockSpec((8, LANES), lambda i: (0, 0))],
        out_shape=[jax.ShapeDtypeStruct((T, D), F32), jax.ShapeDtypeStruct((T, D), MXU_DTYPE),
                   jax.ShapeDtypeStruct((T, D), ACT_DTYPE), jax.ShapeDtypeStruct((T, D), MXU_DTYPE),
                   jax.ShapeDtypeStruct((P, T, Fb), ACT_DTYPE), jax.ShapeDtypeStruct((P, T, Fb), ACT_DTYPE),
                   jax.ShapeDtypeStruct((P, T, Fb), MXU_DTYPE), jax.ShapeDtypeStruct((T, D), ACT_DTYPE),
                   jax.ShapeDtypeStruct((T, D), F32), jax.ShapeDtypeStruct((8, LANES), F32)],
        scratch_shapes=[pltpu.VMEM(w_out.shape, w_out.dtype), pltpu.VMEM(w_gate.shape, w_gate.dtype),
                        pltpu.VMEM(w_up.shape, w_up.dtype), pltpu.VMEM(w_down.shape, w_down.dtype),
                        pltpu.SemaphoreType.DMA((4,))],
        compiler_params=_cp(("arbitrary",)),
    )(x2, cv, ya, tgt, mod, g_ln, b_ln, g_ffn, w_out, w_gate, w_up, w_down)


def _ffn_bwd(dy, x1, gs, us, fo, mixb, cv, mod, g_ln, b_ln, g_ffn, w_out, w_gate, w_up, w_down, *, S, tm):
    T, D = dy.shape
    DC = cv.shape[1]
    P, Kb, _ = w_out.shape
    Fb = w_down.shape[1]
    tps = S // tm
    Bl = T // S

    def body(dy_ref, x1_ref, g_ref, u_ref, f_ref, mix_ref, cv_ref, mod_ref, gln_ref, bln_ref, gf_ref,
             wo_hbm, wg_hbm, wu_hbm, wd_hbm,
             dg_ref, du_ref, df_ref, dx1_ref, dmix_ref, dya_ref, dcv_ref, macc_ref, gacc_ref, lacc_ref,
             wo, wg, wu, wd, sems):
        i = pl.program_id(0)
        _load_resident(i, [(wo_hbm, wo), (wg_hbm, wg), (wu_hbm, wu), (wd_hbm, wd)], sems)

        @pl.when(i == 0)
        def _():
            gacc_ref[...] = jnp.zeros_like(gacc_ref)
            lacc_ref[...] = jnp.zeros_like(lacc_ref)

        @pl.when(i % tps == 0)
        def _():
            macc_ref[...] = jnp.zeros_like(macc_ref)

        dy_t = dy_ref[...]
        x1 = x1_ref[...]
        gate_f = mod_ref[:, 5 * D:6 * D]
        macc_ref[2:3, :] += _colsum(dy_t * f_ref[...].astype(F32))
        dfb = (dy_t * gate_f).astype(MXU_DTYPE)
        df_ref[...] = dfb
        dh2 = jnp.zeros((tm, D), F32)
        for p in range(P):
            da = _dot_nt(dfb, wd[p])
            g = g_ref[p].astype(F32)
            u = u_ref[p].astype(F32)
            sg = _sigmoid(g)
            dgp = (da * u * (sg * (1.0 + g * (1.0 - sg)))).astype(MXU_DTYPE)
            dup = (da * (g * sg)).astype(MXU_DTYPE)
            dg_ref[p] = dgp
            du_ref[p] = dup
            dh2 = dh2 + _dot(dgp, wg[p]) + _dot(dup, wu[p])
        r2 = lax.rsqrt(jnp.mean(x1 * x1, axis=-1, keepdims=True) + EPS)
        xr = x1 * r2
        n2 = xr * gf_ref[...]
        macc_ref[0:1, :] += _colsum(dh2)
        macc_ref[1:2, :] += _colsum(dh2 * n2)
        dn2 = dh2 * (1.0 + mod_ref[:, 4 * D:5 * D])
        gacc_ref[0:1, :] += _colsum(dn2 * xr)
        e = dn2 * gf_ref[...]
        dx1 = dy_t + r2 * e - xr * (r2 * jnp.mean(e * xr, axis=-1, keepdims=True))
        dx1_ref[...] = dx1
        macc_ref[3:4, :] += _colsum(dx1 * mix_ref[...].astype(F32))
        dmixb = (dx1 * mod_ref[:, 2 * D:3 * D]).astype(MXU_DTYPE)
        dmix_ref[...] = dmixb
        parts = [_dot_nt(dmixb, wo[p]) for p in range(P)]
        dycat = jnp.concatenate(parts, axis=1) if P > 1 else parts[0]
        dya_ref[...] = dycat[:, DC:]
        dyc = dycat[:, :DC]
        z, rs, ln = _layer_norm_parts(cv_ref[...], gln_ref[...], bln_ref[...])
        sg = _sigmoid(ln)
        dln = dyc * (sg * (1.0 + ln * (1.0 - sg)))
        lacc_ref[0:1, :] += _colsum(dln * z)
        lacc_ref[1:2, :] += _colsum(dln)
        dz = dln * gln_ref[...]
        dcv_ref[...] = rs * (dz - jnp.mean(dz, axis=-1, keepdims=True) - z * jnp.mean(dz * z, axis=-1, keepdims=True))

    row = lambda w: pl.BlockSpec((tm, w), lambda i: (i, 0))
    vec = lambda w: pl.BlockSpec((1, w), lambda i: (0, 0))
    blk = pl.BlockSpec((P, tm, Fb), lambda i: (0, i, 0))
    anyspec = pl.BlockSpec(memory_space=pl.ANY)
    return pl.pallas_call(
        body, grid=(T // tm,), name="ffn_bwd",
        in_specs=[row(D), row(D), blk, blk, row(D), row(D), row(DC),
                  pl.BlockSpec((None, 1, N_MOD * D), lambda i: (i // tps, 0, 0)),
                  vec(DC), vec(DC), vec(D), anyspec, anyspec, anyspec, anyspec],
        out_specs=[blk, blk, row(D), row(D), row(D), row(D - DC), row(DC),
                   pl.BlockSpec((None, 8, D), lambda i: (i // tps, 0, 0)),
                   pl.BlockSpec((8, D), lambda i: (0, 0)), pl.BlockSpec((8, DC), lambda i: (0, 0))],
        out_shape=[jax.ShapeDtypeStruct((P, T, Fb), MXU_DTYPE), jax.ShapeDtypeStruct((P, T, Fb), MXU_DTYPE),
                   jax.ShapeDtypeStruct((T, D), MXU_DTYPE), jax.ShapeDtypeStruct((T, D), F32),
                   jax.ShapeDtypeStruct((T, D), MXU_DTYPE), jax.ShapeDtypeStruct((T, D - DC), F32),
                   jax.ShapeDtypeStruct((T, DC), F32), jax.ShapeDtypeStruct((Bl, 8, D), F32),
                   jax.ShapeDtypeStruct((8, D), F32), jax.ShapeDtypeStruct((8, DC), F32)],
        scratch_shapes=[pltpu.VMEM(w_out.shape, w_out.dtype), pltpu.VMEM(w_gate.shape, w_gate.dtype),
                        pltpu.VMEM(w_up.shape, w_up.dtype), pltpu.VMEM(w_down.shape, w_down.dtype),
                        pltpu.SemaphoreType.DMA((4,))],
        compiler_params=_cp(("arbitrary",)),
    )(dy, x1, gs, us, fo, mixb, cv, mod, g_ln, b_ln, g_ffn, w_out, w_gate, w_up, w_down)


def _in_bwd(da, dg, dq, dk, dv, x2, dx1, mod, g_mix, w_in, *, S, tm):
    T, D = x2.shape
    P, _, Nb = w_in.shape
    DC = da.shape[1]
    NS = dq.shape[0]
    n_in = P * Nb
    tps = S // tm
    Bl = T // S

    def body(da_ref, dg_ref, dq_ref, dk_ref, dv_ref, x_ref, dx1_ref, mod_ref, g_ref, w_ref,
             dx_ref, dproj_ref, macc_ref, gacc_ref):
        i = pl.program_id(0)

        @pl.when(i == 0)
        def _():
            gacc_ref[...] = jnp.zeros_like(gacc_ref)

        @pl.when(i % tps == 0)
        def _():
            macc_ref[...] = jnp.zeros_like(macc_ref)

        pieces = [da_ref[...], dg_ref[...]] + [r[j] for r in (dq_ref, dk_ref, dv_ref) for j in range(NS)]
        dproj = jnp.concatenate(pieces, axis=1).astype(MXU_DTYPE)
        dproj_ref[...] = dproj
        dh = _dot_nt(dproj[:, 0:Nb], w_ref[0])
        for p in range(1, P):
            dh = dh + _dot_nt(dproj[:, Nb * p:Nb * (p + 1)], w_ref[p])
        x = x_ref[...]
        r = lax.rsqrt(jnp.mean(x * x, axis=-1, keepdims=True) + EPS)
        xr = x * r
        macc_ref[0:1, :] += _colsum(dh)
        macc_ref[1:2, :] += _colsum(dh * (xr * g_ref[...]))
        dn = dh * (1.0 + mod_ref[:, D:2 * D])
        gacc_ref[0:1, :] += _colsum(dn * xr)
        e = dn * g_ref[...]
        dx_ref[...] = dx1_ref[...] + r * e - xr * (r * jnp.mean(e * xr, axis=-1, keepdims=True))

    row = lambda w: pl.BlockSpec((tm, w), lambda i: (i, 0))
    slab = pl.BlockSpec((NS, tm, LANES), lambda i: (0, i, 0))
    return pl.pallas_call(
        body, grid=(T // tm,), name="in_bwd",
        in_specs=[row(DC), row(DC), slab, slab, slab, row(D), row(D),
                  pl.BlockSpec((None, 1, N_MOD * D), lambda i: (i // tps, 0, 0)),
                  pl.BlockSpec((1, D), lambda i: (0, 0)),
                  pl.BlockSpec((P, D, Nb), lambda i: (0, 0, 0))],
        out_specs=[row(D), row(n_in), pl.BlockSpec((None, 8, D), lambda i: (i // tps, 0, 0)),
                   pl.BlockSpec((8, D), lambda i: (0, 0))],
        out_shape=[jax.ShapeDtypeStruct((T, D), F32), jax.ShapeDtypeStruct((T, n_in), MXU_DTYPE),
                   jax.ShapeDtypeStruct((Bl, 8, D), F32), jax.ShapeDtypeStruct((8, D), F32)],
        compiler_params=_cp(("arbitrary",)),
    )(da, dg, dq, dk, dv, x2, dx1, mod, g_mix, w_in)


def _wgrad(a, b, *, P, name, tk, split=None, host=None):
    a_blk, b_blk = a.ndim == 3, b.ndim == 3
    plan, h_in, h_out = host if host is not None else (None, (), ())
    ni, no = len(h_in), len(h_out)
    T = a.shape[-2]
    if a_blk:
        R, C = a.shape[2], b.shape[1]
        a_of = lambda av, p: av[p]
        b_of = lambda bv, p: bv[...]
    elif b_blk:
        R, C = a.shape[1], b.shape[2]
        a_of = lambda av, p: av[...]
        b_of = lambda bv, p: bv[p]
    elif split == "a":
        R, C = a.shape[1] // P, b.shape[1]
        a_of = lambda av, p: av[:, R * p:R * (p + 1)]
        b_of = lambda bv, p: bv[...]
    else:
        R, C = a.shape[1], b.shape[1] // P
        a_of = lambda av, p: av[...]
        b_of = lambda bv, p: bv[:, C * p:C * (p + 1)]

    n_steps = T // tk

    def body(a_ref, b_ref, *rest):
        hin, o_ref, hout, sems = rest[:ni], rest[ni], rest[ni + 1:ni + 1 + no], rest[ni + 1 + no:]
        step = pl.program_id(0)

        @pl.when(step == 0)
        def _():
            o_ref[...] = jnp.zeros_like(o_ref)
            if plan is not None:
                plan.start(hin, hout, sems)

        if plan is not None:
            @pl.when(step == n_steps // 2)
            def _():
                plan.forward(hin, hout, sems)

        for p in range(P):
            o_ref[p] += _dot_tn(a_of(a_ref, p), b_of(b_ref, p))

        if plan is not None:
            @pl.when(step == n_steps - 1)
            def _():
                plan.finish(hin, hout, sems)

    def spec(v):
        if v.ndim == 3:
            return pl.BlockSpec((P, tk, v.shape[2]), lambda k: (0, k, 0))
        return pl.BlockSpec((tk, v.shape[1]), lambda k: (k, 0))

    anyspec = pl.BlockSpec(memory_space=pl.ANY)
    res = pl.pallas_call(
        body, grid=(n_steps,), name=name,
        in_specs=[spec(a), spec(b)] + [anyspec] * ni,
        out_specs=[pl.BlockSpec((P, R, C), lambda k: (0, 0, 0))] + [anyspec] * no,
        out_shape=[jax.ShapeDtypeStruct((P, R, C), F32)] + list(h_out),
        scratch_shapes=plan.scratch() if plan is not None else [],
        compiler_params=_cp(("arbitrary",)),
    )(a, b, *h_in)
    return res if plan is not None else res[0]


TM_IN = 512
TM_FFN = 256
TK_WGRAD = 1024


def _alibi_slabs(n_slab):
    heads = 2 * n_slab
    slopes = 2.0 ** (-8.0 * np.arange(1, heads + 1) / heads)
    return jnp.asarray(np.broadcast_to(np.repeat(slopes.reshape(n_slab, 1, 2), HEAD_DIM, axis=2), (n_slab, 8, LANES)),
                       dtype=F32)


def _local_step(x, tgt, mod, g_mix, wdw, g_ln, b_ln, g_q, g_k, g_ffn, w_in, w_out, w_gate, w_up, w_down,
                pc_idx=None):
    Bl, S, D = x.shape
    T = Bl * S
    DC = g_ln.shape[1]
    P = w_in.shape[0]
    n_slab = (D - DC) // LANES
    x2 = x.reshape(T, D)
    t2 = tgt.reshape(T, D)
    mod3 = mod.reshape(Bl, 1, N_MOD * D)
    gq2 = jnp.tile(g_q, (1, LANES // HEAD_DIM))
    gk2 = jnp.tile(g_k, (1, LANES // HEAD_DIM))
    slopes = _alibi_slabs(n_slab)

    ag, qkv, qkh, h1 = _fwd_in(x2, mod3, g_mix, gq2, gk2, w_in, S=S, tm=TM_IN, n_ag=2 * DC)
    cv = _conv_fwd(ag, wdw, Bl=Bl, S=S, DC=DC)
    if pc_idx is not None:
        ya, lse, w_out, w_gate, w_up, w_down = _attn_fwd(qkh, qkv, slopes, Bl=Bl, S=S,
                                                         hosted=(w_out, w_gate, w_up, w_down))
    else:
        ya, lse = _attn_fwd(qkh, qkv, slopes, Bl=Bl, S=S)
    x1, ycat, mixb, h2, gs, us, acts, fo, dy, lossb = _ffn_fwd(
        x2, cv, ya, t2, mod3, g_ln, b_ln, g_ffn, w_out, w_gate, w_up, w_down, S=S, tm=TM_FFN)
    dgs, dus, dfb, dx1, dmixb, dya, dcv, macc_f, gacc_f, lacc = _ffn_bwd(
        dy, x1, gs, us, fo, mixb, cv, mod3, g_ln, b_ln, g_ffn, w_out, w_gate, w_up, w_down, S=S, tm=TM_FFN)
    wg = functools.partial(_wgrad, P=P, tk=TK_WGRAD)
    out = {}
    if pc_idx is None:
        grads = dict(w_down=wg(acts, dfb, name="wgrad_down"), w_gate=wg(dgs, h2, name="wgrad_gate"),
                     w_up=wg(dus, h2, name="wgrad_up"), w_out=wg(ycat, dmixb, name="wgrad_out", split="a"))
        dq, dk, dv, gqk = _attn_bwd(qkh, qkv, ya, lse, dya, gq2, gk2, slopes, Bl=Bl, S=S)
    else:
        g_down = wg(acts, dfb, name="wgrad_down")
        g_gate, r_down = wg(dgs, h2, name="wgrad_gate", host=_sibling_host([g_down]))
        g_up, r_gate = wg(dus, h2, name="wgrad_up", host=_sibling_host([g_gate]))
        g_out, r_up = wg(ycat, dmixb, name="wgrad_out", split="a", host=_sibling_host([g_up]))
        (r_out,) = _rs_sibling([g_out], "rs_sibling_out")
        grads = dict(w_down=g_down, w_gate=g_gate, w_up=g_up, w_out=g_out)
        sums = [_pair_add(grads[nm], r, pc_idx, "pair_add_" + nm)
                for nm, r in zip(EARLY_WEIGHTS, (r_down, r_gate, r_up, r_out))]
        res = _attn_bwd(qkh, qkv, ya, lse, dya, gq2, gk2, slopes, Bl=Bl, S=S, hosted=tuple(sb for _, sb in sums))
        dq, dk, dv, gqk = res[:4]
        out["early_sums"] = [s32 for s32, _ in sums]
        out["early_recv"] = list(res[4:])
    da, dg, dwdw = _conv_bwd(ag, dcv, wdw, Bl=Bl, S=S, DC=DC)
    dx, dprojb, macc_m, gacc_m = _in_bwd(da, dg, dq, dk, dv, x2, dx1, mod3, g_mix, w_in, S=S, tm=TM_IN)
    packed = _pack_small(macc_m, macc_f, gacc_m, gacc_f, lacc, gqk, dwdw, lossb)
    if pc_idx is None:
        grads["w_in"] = wg(h1, dprojb, name="wgrad_in", split="b")
    else:
        grads["w_in"], out["gathered_small"] = wg(h1, dprojb, name="wgrad_in", split="b",
                                                  host=_small_gather_host(packed))
    out.update(dx=dx.reshape(Bl, S, D), grads=grads, packed=packed)
    return out


EARLY_WEIGHTS = ("w_down", "w_gate", "w_up", "w_out")


def _small_layout(Bl):
    return 8 * Bl, 8 * Bl + 8, 8 * Bl + 8 + CONV_ROWS


def _pack_small(macc_m, macc_f, gacc_m, gacc_f, lacc, gqk, dwdw, lossb):
    Bl, _, D = macc_m.shape
    DC = lacc.shape[1]
    assert 2 * DC <= D
    SMALL_GAIN_ROW, SMALL_TAP_ROW, SMALL_ROWS = _small_layout(Bl)

    def body(mm_ref, mf_ref, gm_ref, gf_ref, la_ref, qk_ref, dw_ref, loss_ref, o_ref):
        o_ref[...] = jnp.zeros_like(o_ref)
        for b in range(Bl):
            o_ref[8 * b + 0:8 * b + 2, :] = mm_ref[b, 0:2, :]
            o_ref[8 * b + 2:8 * b + 3, :] = mf_ref[b, 3:4, :]
            o_ref[8 * b + 3:8 * b + 6, :] = mf_ref[b, 0:3, :]
        r = SMALL_GAIN_ROW
        o_ref[r:r + 1, :] = gm_ref[0:1, :]
        o_ref[r + 1:r + 2, :] = gf_ref[0:1, :]
        o_ref[r + 2:r + 3, 0:DC] = la_ref[0:1, :]
        o_ref[r + 2:r + 3, DC:2 * DC] = la_ref[1:2, :]
        qk = qk_ref[0:2, 0:HEAD_DIM] + qk_ref[0:2, HEAD_DIM:2 * HEAD_DIM]
        o_ref[r + 3:r + 4, 0:HEAD_DIM] = qk[0:1, :]
        o_ref[r + 3:r + 4, HEAD_DIM:2 * HEAD_DIM] = qk[1:2, :]
        o_ref[r + 4:r + 5, 0:LANES] = loss_ref[0:1, :]
        o_ref[SMALL_TAP_ROW:SMALL_TAP_ROW + CONV_ROWS, 0:DC] = dw_ref[...]

    return pl.pallas_call(body, name="pack_small", out_shape=jax.ShapeDtypeStruct((SMALL_ROWS, D), F32),
                          compiler_params=_cp())(macc_m, macc_f, gacc_m, gacc_f, lacc, gqk, dwdw, lossb)


def _row_tile(rows, cap=512):
    if rows <= cap:
        return rows
    best = rows
    for t in range(8, cap + 1, 8):
        if rows % t == 0:
            best = t
    return best


def _cast_weight(w, pidx, name):
    def body(p_ref, w_ref, o_ref):
        o_ref[...] = w_ref[...].astype(MXU_DTYPE)
    R, C = w.shape
    tr = _row_tile(R)
    return pl.pallas_call(
        body, name=name,
        grid_spec=pltpu.PrefetchScalarGridSpec(
            num_scalar_prefetch=1, grid=(R // tr,),
            in_specs=[pl.BlockSpec((tr, C), lambda i, p: (i, 0))],
            out_specs=pl.BlockSpec((None, tr, C), lambda i, p: (p[0], i, 0))),
        out_shape=jax.ShapeDtypeStruct((4, R, C), MXU_DTYPE),
    )(pidx, w)


def _pair_add(g, recv, pc_idx, name):
    P, R, C = g.shape
    R2 = R // 2

    def body(pc_ref, g_ref, r_ref, o_ref, ob_ref):
        s = g_ref[...] + r_ref[...]
        ob_ref[...] = s.astype(jnp.bfloat16)

        @pl.when(pl.program_id(0) == pc_ref[0])
        def _():
            o_ref[...] = s

    return pl.pallas_call(
        body, name=name,
        grid_spec=pltpu.PrefetchScalarGridSpec(
            num_scalar_prefetch=1, grid=(P,),
            in_specs=[pl.BlockSpec((None, R2, C), lambda p, pc: (p, pc[1], 0)),
                      pl.BlockSpec((None, R2, C), lambda p, pc: (p, 0, 0))],
            out_specs=[pl.BlockSpec((R2, C), lambda p, pc: (0, 0)),
                       pl.BlockSpec((None, R2, C), lambda p, pc: (p, 0, 0))]),
        out_shape=[jax.ShapeDtypeStruct((R2, C), F32), jax.ShapeDtypeStruct((P, R2, C), jnp.bfloat16)],
    )(pc_idx, g, recv)


def _final_add(own, recv, pc_idx, name):
    R2, C = own.shape

    def body(pc_ref, s_ref, r_ref, o_ref):
        acc = s_ref[...]
        for k in range(3):
            acc = acc + r_ref[k].astype(F32)
        o_ref[...] = acc

    return pl.pallas_call(
        body, name=name,
        grid_spec=pltpu.PrefetchScalarGridSpec(
            num_scalar_prefetch=1, grid=(1,),
            in_specs=[pl.BlockSpec((R2, C), lambda i, pc: (0, 0)),
                      pl.BlockSpec((3, R2, C), lambda i, pc: (0, 0, 0))],
            out_specs=pl.BlockSpec((R2, C), lambda i, pc: (pc[1], 0))),
        out_shape=jax.ShapeDtypeStruct((2 * R2, C), F32),
    )(pc_idx, own, recv)


def _adamw_update(w_ref, g_ref, m_ref, v_ref, d_ref, nm_ref, nv_ref):
    c1 = 1.0 - ADAM_B1 ** ADAM_STEP
    c2 = 1.0 - ADAM_B2 ** ADAM_STEP
    gg = g_ref[...]
    nm = ADAM_B1 * m_ref[...] + (1.0 - ADAM_B1) * gg
    nv = ADAM_B2 * v_ref[...] + (1.0 - ADAM_B2) * (gg * gg)
    nm_ref[...] = nm
    nv_ref[...] = nv
    d_ref[...] = -ADAM_LR * ((nm / c1) / (jnp.sqrt(nv / c2) + ADAM_EPS) + ADAM_WD * w_ref[...])


def _adamw(w, g, m, v, name):
    R, C = w.shape
    tr = _row_tile(R, 256)
    spec = pl.BlockSpec((tr, C), lambda i: (i, 0))
    return pl.pallas_call(
        functools.partial(_adamw_update), grid=(R // tr,), name=name,
        in_specs=[spec] * 4, out_specs=[spec] * 3,
        out_shape=[jax.ShapeDtypeStruct((R, C), F32)] * 3,
    )(w, g, m, v)


def _startup(first, w_ada, b_cols, w_in_buf, *, Bl):
    rows, D = first.shape
    NA = w_ada.shape[1]
    n_dev = 8
    g_w = _WeightGather([w_in_buf.shape])
    g_c = _SmallGather(rows)
    g_m = _SmallGather(n_dev * Bl)

    def body(first_ref, wada_ref, b_ref, win_in, g0_ref, call_ref, gm_ref, win_out, modp,
             ws0, ws1, cs0, cs1, cs2, ms0, ms1, ms2):
        g_w.start([win_out], (ws0, ws1))
        for phase in (g_c.start, g_c.forward, g_c.finish):
            phase([first_ref], [g0_ref], (cs0, cs1, cs2))
        for d in range(n_dev):
            call_ref[Bl * d:Bl * (d + 1), :] = g0_ref[rows * d:rows * d + Bl, :]
        c = call_ref[...]
        modp[...] = jnp.dot(c * _sigmoid(c), wada_ref[...], preferred_element_type=F32,
                            precision=lax.Precision.HIGH) + b_ref[...]
        for phase in (g_m.start, g_m.forward, g_m.finish):
            phase([modp], [gm_ref], (ms0, ms1, ms2))
        g_w.forward([win_out], (ws0, ws1))
        g_w.finish([win_out], (ws0, ws1))

    vmem = pl.BlockSpec(memory_space=pltpu.VMEM)
    anyspec = pl.BlockSpec(memory_space=pl.ANY)
    return pl.pallas_call(
        body, name="startup",
        in_specs=[vmem, vmem, vmem, anyspec], out_specs=[vmem, vmem, vmem, anyspec],
        out_shape=[jax.ShapeDtypeStruct((n_dev * rows, D), F32), jax.ShapeDtypeStruct((n_dev * Bl, D), F32),
                   jax.ShapeDtypeStruct((n_dev * n_dev * Bl, NA), F32),
                   jax.ShapeDtypeStruct(w_in_buf.shape, w_in_buf.dtype)],
        input_output_aliases={3: 3},
        scratch_shapes=[pltpu.VMEM((n_dev * Bl, NA), F32)] + g_w.scratch() + g_c.scratch() + g_m.scratch(),
        compiler_params=_cp(),
    )(first, w_ada, b_cols, w_in_buf)


def _ada_bwd(c_all, dmod_cols):
    def body(c_ref, d_ref, o_ref):
        c = c_ref[...]
        o_ref[...] = _dot_tn((c * _sigmoid(c)).astype(MXU_DTYPE), d_ref[...].astype(MXU_DTYPE))
    return pl.pallas_call(
        body, name="ada_bwd", out_shape=jax.ShapeDtypeStruct((c_all.shape[1], dmod_cols.shape[1]), F32),
        compiler_params=_cp(),
    )(c_all, dmod_cols)


def _small_reduce(gathered, n_dev, Bl):
    mod_rows, _, rows = _small_layout(Bl)
    width = gathered.shape[1]

    def body(g_ref, red_ref, bada_ref):
        acc = g_ref[0:rows, :]
        for d in range(1, n_dev):
            acc = acc + g_ref[d * rows:(d + 1) * rows, :]
        red_ref[...] = acc[mod_rows:, :]
        b = acc[0:8, :]
        for q in range(1, Bl):
            b = b + acc[8 * q:8 * q + 8, :]
        bada_ref[...] = b
    return pl.pallas_call(
        body, name="small_reduce",
        out_shape=[jax.ShapeDtypeStruct((rows - mod_rows, width), F32), jax.ShapeDtypeStruct((8, width), F32)],
        compiler_params=_cp(),
    )(gathered)


def _mesh_pos():
    return lax.axis_index("x"), lax.axis_index("y"), lax.axis_index("c")


def _other_chips(x, y):
    return [(1 - x, y), (x, 1 - y), (1 - x, 1 - y)]


class _WeightGather:
    def __init__(self, shapes):
        self.shapes = shapes
        self.n = len(shapes)

    def scratch(self):
        return [pltpu.SemaphoreType.DMA((6 * self.n,)), pltpu.SemaphoreType.DMA((6 * self.n,))]

    def _copy(self, outs, sems, w, k, slot, h, to):
        r2 = self.shapes[w][1] // 2
        blk = outs[w].at[slot, pl.ds(h * r2, r2), :]
        return pltpu.make_async_remote_copy(
            src_ref=blk, dst_ref=blk, send_sem=sems[0].at[6 * w + k], recv_sem=sems[1].at[6 * w + k],
            device_id=to, device_id_type=MESH_DEV)

    def start(self, outs, sems):
        x, y, c = _mesh_pos()
        for w in range(self.n):
            for k, chip in enumerate(_other_chips(x, y)):
                self._copy(outs, sems, w, k, 2 * x + y, c, (*chip, c)).start()

    def forward(self, outs, sems):
        x, y, c = _mesh_pos()
        for w in range(self.n):
            for k, chip in enumerate(_other_chips(x, y)):
                slot = 2 * chip[0] + chip[1]
                self._copy(outs, sems, w, k, slot, c, (x, y, 1 - c)).wait_recv()
                self._copy(outs, sems, w, 3 + k, slot, c, (x, y, 1 - c)).start()

    def finish(self, outs, sems):
        x, y, c = _mesh_pos()
        for w in range(self.n):
            for k, chip in enumerate(_other_chips(x, y)):
                slot = 2 * chip[0] + chip[1]
                self._copy(outs, sems, w, 3 + k, slot, 1 - c, (x, y, 1 - c)).wait_recv()
                self._copy(outs, sems, w, k, 2 * x + y, c, (*chip, c)).wait_send()
                self._copy(outs, sems, w, 3 + k, slot, c, (x, y, 1 - c)).wait_send()


class _SiblingExchange:
    def __init__(self, shapes):
        self.shapes = shapes

    def scratch(self):
        n = sum(s[0] for s in self.shapes)
        return [pltpu.SemaphoreType.DMA((n,)), pltpu.SemaphoreType.DMA((n,))]

    def out_shapes(self, dtype):
        return [jax.ShapeDtypeStruct((s[0], s[1] // 2, s[2]), dtype) for s in self.shapes]

    def _copies(self, ins, outs, sems):
        x, y, c = _mesh_pos()
        cps, k = [], 0
        for w, (P, R, _) in enumerate(self.shapes):
            r2 = R // 2
            for p in range(P):
                cps.append(pltpu.make_async_remote_copy(
                    src_ref=ins[w].at[p, pl.ds((1 - c) * r2, r2), :], dst_ref=outs[w].at[p],
                    send_sem=sems[0].at[k], recv_sem=sems[1].at[k],
                    device_id=(x, y, 1 - c), device_id_type=MESH_DEV))
                k += 1
        return cps

    def start(self, ins, outs, sems):
        for cp in self._copies(ins, outs, sems):
            cp.start()

    def forward(self, ins, outs, sems):
        pass

    def finish(self, ins, outs, sems):
        for cp in self._copies(ins, outs, sems):
            cp.wait()


def _sibling_host(grads):
    plan = _SiblingExchange([g.shape for g in grads])
    return plan, tuple(grads), tuple(plan.out_shapes(grads[0].dtype))


def _rs_sibling(grads, name):
    n = len(grads)
    plan, _, out_shapes = _sibling_host(grads)

    def body(*refs):
        ins, outs, sems = refs[:n], refs[n:2 * n], refs[2 * n:]
        plan.start(ins, outs, sems)
        plan.finish(ins, outs, sems)

    anyspec = pl.BlockSpec(memory_space=pl.ANY)
    return pl.pallas_call(
        body, name=name, out_shape=list(out_shapes),
        in_specs=[anyspec] * n, out_specs=[anyspec] * n, scratch_shapes=plan.scratch(),
    )(*grads)


class _SmallGather:
    def __init__(self, m_per):
        self.m = m_per

    def scratch(self):
        return [pltpu.SemaphoreType.DMA((7,)), pltpu.SemaphoreType.DMA((7,)), pltpu.SemaphoreType.DMA]

    def _rows(self, out, pos):
        px, py, pc = pos
        return out.at[pl.ds((4 * px + 2 * py + pc) * self.m, self.m), :]

    def _copy(self, out, sems, k, block, to, src=None):
        dst = self._rows(out, block)
        return pltpu.make_async_remote_copy(
            src_ref=dst if src is None else src, dst_ref=dst, send_sem=sems[0].at[k], recv_sem=sems[1].at[k],
            device_id=to, device_id_type=MESH_DEV)

    def start(self, ins, outs, sems):
        x, y, c = _mesh_pos()
        me = (x, y, c)
        pltpu.make_async_copy(ins[0], self._rows(outs[0], me), sems[2]).start()
        self._copy(outs[0], sems, 0, me, (x, y, 1 - c), src=ins[0]).start()
        for j, chip in enumerate(_other_chips(x, y)):
            self._copy(outs[0], sems, 1 + j, me, (*chip, c), src=ins[0]).start()

    def forward(self, ins, outs, sems):
        x, y, c = _mesh_pos()
        for j, chip in enumerate(_other_chips(x, y)):
            self._copy(outs[0], sems, 1 + j, (*chip, c), (x, y, c)).wait_recv()
            self._copy(outs[0], sems, 4 + j, (*chip, c), (x, y, 1 - c)).start()

    def finish(self, ins, outs, sems):
        x, y, c = _mesh_pos()
        me = (x, y, c)
        self._copy(outs[0], sems, 0, (x, y, 1 - c), me).wait_recv()
        for j, chip in enumerate(_other_chips(x, y)):
            self._copy(outs[0], sems, 4 + j, (*chip, 1 - c), me).wait_recv()
        self._copy(outs[0], sems, 0, me, (x, y, 1 - c), src=ins[0]).wait_send()
        for j, chip in enumerate(_other_chips(x, y)):
            self._copy(outs[0], sems, 1 + j, me, (*chip, c), src=ins[0]).wait_send()
            self._copy(outs[0], sems, 4 + j, (*chip, c), (x, y, 1 - c)).wait_send()
        pltpu.make_async_copy(ins[0], self._rows(outs[0], me), sems[2]).wait()


def _small_gather_host(packed):
    m, n = packed.shape
    return _SmallGather(m), (packed,), (jax.ShapeDtypeStruct((8 * m, n), packed.dtype),)


class _ChipExchange:
    def __init__(self, n):
        self.n = n

    def scratch(self):
        return [pltpu.SemaphoreType.DMA((3 * self.n,)), pltpu.SemaphoreType.DMA((3 * self.n,))]

    def _copies(self, ins, outs, sems):
        x, y, c = _mesh_pos()
        return [pltpu.make_async_remote_copy(
            src_ref=ins[w].at[2 * chip[0] + chip[1]], dst_ref=outs[w].at[k],
            send_sem=sems[0].at[3 * w + k], recv_sem=sems[1].at[3 * w + k],
            device_id=(*chip, c), device_id_type=MESH_DEV)
            for w in range(self.n) for k, chip in enumerate(_other_chips(x, y))]

    def start(self, ins, outs, sems):
        for cp in self._copies(ins, outs, sems):
            cp.start()

    def forward(self, ins, outs, sems):
        pass

    def finish(self, ins, outs, sems):
        for cp in self._copies(ins, outs, sems):
            cp.wait()


def _rs_final(bufs, name, chips=()):
    n, nc = len(bufs), len(chips)
    plan = _ChipExchange(nc)

    def body(*refs):
        cin = refs[n:n + nc]
        outs = refs[n + nc:2 * n + nc]
        cout = refs[2 * n + nc:2 * n + 2 * nc]
        send_sems, recv_sems = refs[2 * n + 2 * nc:2 * n + 2 * nc + 2]
        csems = refs[2 * n + 2 * nc + 2:]
        x, y, c = _mesh_pos()
        if nc:
            plan.start(cin, cout, csems)
        cps = []
        for w in range(n):
            r2 = bufs[w].shape[0] // 2
            mine = outs[w].at[pl.ds(c * r2, r2), :]
            cps.append(pltpu.make_async_remote_copy(
                src_ref=mine, dst_ref=mine, send_sem=send_sems.at[w], recv_sem=recv_sems.at[w],
                device_id=(x, y, 1 - c), device_id_type=MESH_DEV))
            cps[-1].start()
        for cp in cps:
            cp.wait()
        if nc:
            plan.finish(cin, cout, csems)

    anyspec = pl.BlockSpec(memory_space=pl.ANY)
    return pl.pallas_call(
        body, name=name,
        out_shape=[jax.ShapeDtypeStruct(b.shape, b.dtype) for b in bufs]
                  + [jax.ShapeDtypeStruct((3,) + s.shape[1:], s.dtype) for s in chips],
        in_specs=[anyspec] * (n + nc), out_specs=[anyspec] * (n + nc),
        input_output_aliases={w: w for w in range(n)},
        scratch_shapes=[pltpu.SemaphoreType.DMA((n,)), pltpu.SemaphoreType.DMA((n,))] + (plan.scratch() if nc else []),
    )(*bufs, *chips)


BIG = ("w_in", "w_out", "w_gate", "w_up", "w_down")
TRANSPOSED = ("w_gate", "w_up")
WEIGHTS = ("w_ada", "b_ada", "g_mix", "w_in", "w_dw", "b_dw", "g_conv_ln", "b_conv_ln", "g_q", "g_k",
           "w_out", "g_ffn", "w_gate", "w_up", "w_down")


def _pad_to(a, rows, cols):
    return jnp.pad(a, ((0, rows - a.shape[0]), (0, cols - a.shape[1])))


def kernel(x, c, w_ada, b_ada, g_mix, w_in, w_dw, b_dw, g_conv_ln, b_conv_ln, g_q, g_k, w_out, g_ffn, w_gate, w_up, w_down, loss_target, m_w_ada, m_b_ada, m_g_mix, m_w_in, m_w_dw, m_b_dw, m_g_conv_ln, m_b_conv_ln, m_g_q, m_g_k, m_w_out, m_g_ffn, m_w_gate, m_w_up, m_w_down, v_w_ada, v_b_ada, v_g_mix, v_w_in, v_w_dw, v_b_dw, v_g_conv_ln, v_b_conv_ln, v_g_q, v_g_k, v_w_out, v_g_ffn, v_w_gate, v_w_up, v_w_down):
    w = dict(w_ada=w_ada, b_ada=b_ada, g_mix=g_mix, w_in=w_in, w_dw=w_dw, b_dw=b_dw, g_conv_ln=g_conv_ln,
             b_conv_ln=b_conv_ln, g_q=g_q, g_k=g_k, w_out=w_out, g_ffn=g_ffn, w_gate=w_gate, w_up=w_up, w_down=w_down)
    m = dict(w_ada=m_w_ada, b_ada=m_b_ada, g_mix=m_g_mix, w_in=m_w_in, w_dw=m_w_dw, b_dw=m_b_dw, g_conv_ln=m_g_conv_ln,
             b_conv_ln=m_b_conv_ln, g_q=m_g_q, g_k=m_g_k, w_out=m_w_out, g_ffn=m_g_ffn, w_gate=m_w_gate, w_up=m_w_up,
             w_down=m_w_down)
    v = dict(w_ada=v_w_ada, b_ada=v_b_ada, g_mix=v_g_mix, w_in=v_w_in, w_dw=v_w_dw, b_dw=v_b_dw, g_conv_ln=v_g_conv_ln,
             b_conv_ln=v_b_conv_ln, g_q=v_g_q, g_k=v_g_k, w_out=v_w_out, g_ffn=v_g_ffn, w_gate=v_w_gate, w_up=v_w_up,
             w_down=v_w_down)
    Bl, S, D = x.shape
    DC = g_conv_ln.shape[1]
    NA = w_ada.shape[2]
    xi, yi, ci = _mesh_pos()
    p = 2 * xi + yi
    dev = 2 * p + ci
    n_dev = 8
    pidx = jnp.reshape(p, (1,)).astype(jnp.int32)
    pc_idx = jnp.stack([p, ci]).astype(jnp.int32)

    first = jnp.concatenate([_pad_to(c, 8, D), _pad_to(w_dw[0], CONV_ROWS, D)], axis=0)
    shard = lambda a, nm: a[0].T if nm in TRANSPOSED else a[0]
    owned = {nm: _cast_weight(shard(w[nm], nm), pidx, "cast_" + nm) for nm in BIG}
    b_cols = lax.dynamic_slice_in_dim(b_ada, p * NA, NA, axis=1)
    g0, c_all, gm, w_in_full = _startup(first, w_ada[0], b_cols, owned["w_in"], Bl=Bl)
    g0 = g0.reshape(n_dev, 8 + CONV_ROWS, D)
    taps = jnp.concatenate([g0[2 * q, 8:, :w_dw.shape[2]] for q in range(4)], axis=1)
    wdw = jnp.where(lax.broadcasted_iota(jnp.int32, taps.shape, 0) == CONV_WIDTH, b_dw, taps)
    gm = gm.reshape(n_dev, n_dev * Bl, NA)
    mod = jnp.concatenate([lax.dynamic_slice_in_dim(gm[2 * q], dev * Bl, Bl, axis=0) for q in range(4)], axis=1)

    loc = _local_step(x, loss_target, mod, g_mix, wdw, g_conv_ln, b_conv_ln, g_q, g_k, g_ffn,
                      w_in_full, owned["w_out"], owned["w_gate"], owned["w_up"], owned["w_down"], pc_idx=pc_idx)

    halves = [_final_add(s32, r, pc_idx, "final_add_" + nm)
              for nm, s32, r in zip(EARLY_WEIGHTS, loc["early_sums"], loc["early_recv"])]
    (late_sib,) = _rs_sibling([loc["grads"]["w_in"]], "rs_sibling_in")
    late32, late16 = _pair_add(loc["grads"]["w_in"], late_sib, pc_idx, "pair_add_w_in")
    *early_full, late_recv = _rs_final(halves, "rs_final_early", chips=(late16,))
    grad = dict(zip(EARLY_WEIGHTS, early_full))
    (grad["w_in"],) = _rs_final([_final_add(late32, late_recv, pc_idx, "final_add_w_in")], "rs_final_in")

    mod_rows, _, small_rows = _small_layout(Bl)
    gs = loc["gathered_small"]
    red, bada8 = _small_reduce(gs, n_dev, Bl)
    dmod_all = gs.reshape(n_dev, small_rows, D)[:, :mod_rows].reshape(n_dev * Bl, 8, D)[:, :N_MOD].reshape(n_dev * Bl, N_MOD * D)
    grad["w_ada"] = _ada_bwd(c_all, lax.dynamic_slice_in_dim(dmod_all, p * NA, NA, axis=1))
    grad["b_ada"] = bada8[:N_MOD].reshape(1, N_MOD * D)
    grad["g_mix"] = red[0:1]
    grad["g_ffn"] = red[1:2]
    grad["g_conv_ln"] = red[2:3, :DC]
    grad["b_conv_ln"] = red[2:3, DC:2 * DC]
    grad["g_q"] = red[3:4, :HEAD_DIM]
    grad["g_k"] = red[3:4, HEAD_DIM:2 * HEAD_DIM]
    loss = red[4, 0]
    dwdw = red[8:8 + CONV_ROWS, :DC]
    grad["w_dw"] = lax.dynamic_slice_in_dim(dwdw[:CONV_WIDTH], p * w_dw.shape[2], w_dw.shape[2], axis=1)
    grad["b_dw"] = dwdw[CONV_WIDTH:CONV_WIDTH + 1]

    delta, new_m, new_v = {}, {}, {}
    for nm in WEIGHTS:
        shp = w[nm].shape
        if nm in TRANSPOSED:
            d_, m_, v_ = _adamw(w[nm][0].T, grad[nm], m[nm][0].T, v[nm][0].T, "adamw_" + nm)
            grad[nm], delta[nm], new_m[nm], new_v[nm] = (a.T.reshape(shp) for a in (grad[nm], d_, m_, v_))
            continue
        two_d = (shp[-2], shp[-1]) if len(shp) == 3 else shp
        d_, m_, v_ = _adamw(w[nm].reshape(two_d), grad[nm].reshape(two_d), m[nm].reshape(two_d), v[nm].reshape(two_d),
                            "adamw_" + nm)
        grad[nm] = grad[nm].reshape(shp)
        delta[nm], new_m[nm], new_v[nm] = d_.reshape(shp), m_.reshape(shp), v_.reshape(shp)

    return (loss, loc["dx"], *[grad[nm] for nm in WEIGHTS], *[delta[nm] for nm in WEIGHTS],
            *[new_m[nm] for nm in WEIGHTS], *[new_v[nm] for nm in WEIGHTS])
```

```python
import functools

import jax
import jax.numpy as jnp
import numpy as np
from jax import lax
from jax.experimental import pallas as pl
from jax.experimental.pallas import tpu as pltpu

F32 = jnp.float32
MXU_DTYPE = jnp.bfloat16
ACT_DTYPE = jnp.bfloat16
EPS = 1e-6
NEG_INF = -1e30
HEAD_DIM = 64
LANES = 128
RADIUS = 64
QBLK = 128
DILATIONS = (1, 4, 16)
CONV_WIDTH = 31
CONV_PAD = CONV_WIDTH // 2
CONV_ROWS = 32
N_MOD = 6
ADAM_LR, ADAM_B1, ADAM_B2, ADAM_EPS, ADAM_WD, ADAM_STEP = 0.001, 0.9, 0.999, 1e-08, 0.01, 10
MESH_DEV = pl.DeviceIdType.MESH
VMEM_LIMIT = 56 << 20
ATTN_BWD_VMEM = 60 << 20


def _cp(sem=None, vmem=VMEM_LIMIT):
    kw = dict(vmem_limit_bytes=vmem)
    if sem is not None:
        kw["dimension_semantics"] = sem
    return pltpu.CompilerParams(**kw)


def _sigmoid(x):
    return 1.0 / (1.0 + jnp.exp(-x))


def _dot(a, b):
    return jnp.dot(a, b, preferred_element_type=F32)


def _dot_nt(a, b):
    return lax.dot_general(a, b, (((1,), (1,)), ((), ())), preferred_element_type=F32)


def _dot_tn(a, b):
    return lax.dot_general(a, b, (((0,), (0,)), ((), ())), preferred_element_type=F32)


def _colsum(v):
    return jnp.sum(v, axis=0, keepdims=True)


def _load_resident(i, pairs, sems):
    @pl.when(i == 0)
    def _():
        cps = [pltpu.make_async_copy(src, dst, sems.at[n]) for n, (src, dst) in enumerate(pairs)]
        for c in cps:
            c.start()
        for c in cps:
            c.wait()


def _fwd_in(x2, mod, g_mix, gq2, gk2, w_in, *, S, tm, n_ag):
    T, D = x2.shape
    P, _, Nb = w_in.shape
    n_in = P * Nb
    n_slab = (n_in - n_ag) // LANES
    NS = n_slab // 3
    tps = S // tm

    def body(x_ref, mod_ref, g_ref, gq_ref, gk_ref, w_ref, ag_ref, qkv_ref, qkh_ref, h_ref):
        x = x_ref[...]
        r = lax.rsqrt(jnp.mean(x * x, axis=-1, keepdims=True) + EPS)
        n = x * r * g_ref[...]
        h = n * (1.0 + mod_ref[:, D:2 * D]) + mod_ref[:, 0:D]
        hb = h.astype(MXU_DTYPE)
        h_ref[...] = hb
        parts = [_dot(hb, w_ref[p]) for p in range(P)]
        proj = jnp.concatenate(parts, axis=1) if P > 1 else parts[0]
        ag_ref[...] = proj[:, :n_ag]
        mm = _head_mean_matrix()
        for j in range(n_slab):
            v = proj[:, n_ag + LANES * j:n_ag + LANES * (j + 1)]
            qkv_ref[j] = v
            if j < 2 * NS:
                gain = gq_ref[...] * (HEAD_DIM ** -0.5 * LOG2E) if j < NS else gk_ref[...]
                qkh_ref[j] = v * lax.rsqrt(_head_mean(v * v, mm) + EPS) * gain

    return pl.pallas_call(
        body, grid=(T // tm,), name="fwd_in",
        in_specs=[pl.BlockSpec((tm, D), lambda i: (i, 0)),
                  pl.BlockSpec((None, 1, N_MOD * D), lambda i: (i // tps, 0, 0)),
                  pl.BlockSpec((1, D), lambda i: (0, 0)),
                  pl.BlockSpec((1, LANES), lambda i: (0, 0)), pl.BlockSpec((1, LANES), lambda i: (0, 0)),
                  pl.BlockSpec((P, D, Nb), lambda i: (0, 0, 0))],
        out_specs=[pl.BlockSpec((tm, n_ag), lambda i: (i, 0)),
                   pl.BlockSpec((n_slab, tm, LANES), lambda i: (0, i, 0)),
                   pl.BlockSpec((2 * NS, tm, LANES), lambda i: (0, i, 0)),
                   pl.BlockSpec((tm, D), lambda i: (i, 0))],
        out_shape=[jax.ShapeDtypeStruct((T, n_ag), F32),
                   jax.ShapeDtypeStruct((n_slab, T, LANES), F32),
                   jax.ShapeDtypeStruct((2 * NS, T, LANES), F32),
                   jax.ShapeDtypeStruct((T, D), MXU_DTYPE)],
        compiler_params=_cp(("arbitrary",)),
    )(x2, mod, g_mix, gq2, gk2, w_in)


CONV_CH = 128


def _conv_taps(win, w_ref, acc, reverse):
    n = win.shape[0]
    for b in range(8):
        wb = win if b == 0 else pltpu.roll(win, shift=n - b, axis=0)
        for a in range(4):
            o = 8 * a + b
            if o < 1 or o > CONV_WIDTH:
                continue
            k = (CONV_WIDTH - o) if reverse else (o - 1)
            acc = acc + w_ref[k:k + 1, :] * wb[8 * a:8 * a + CONV_CH, :]
    return acc


def _conv_fwd(ag, wdw, *, Bl, S, DC):
    T = ag.shape[0]
    nsc = DC // LANES
    CH = CONV_CH

    def body(a_ref, g_ref, w_ref, cv_ref, upad):
        zeros16 = jnp.zeros((16, LANES), F32)
        upad[0:16, :] = zeros16
        upad[S + 16:S + 32, :] = zeros16

        def fill(i, _):
            r0 = pl.multiple_of(i * CH, CH)
            a = a_ref[pl.ds(r0, CH), :]
            g = g_ref[pl.ds(r0, CH), :]
            upad[pl.ds(r0 + 16, CH), :] = a * _sigmoid(g)
            return 0
        lax.fori_loop(0, S // CH, fill, 0)

        def conv(i, _):
            r0 = pl.multiple_of(i * CH, CH)
            win = upad[pl.ds(r0, CH + 32), :]
            acc = jnp.zeros((CH, LANES), F32) + w_ref[CONV_WIDTH:CONV_WIDTH + 1, :]
            cv_ref[pl.ds(r0, CH), :] = _conv_taps(win, w_ref, acc, reverse=False)
            return 0
        lax.fori_loop(0, S // CH, conv, 0)

    return pl.pallas_call(
        body, grid=(Bl, nsc), name="conv_fwd",
        in_specs=[pl.BlockSpec((S, LANES), lambda b, j: (b, j)),
                  pl.BlockSpec((S, LANES), lambda b, j: (b, nsc + j)),
                  pl.BlockSpec((CONV_ROWS, LANES), lambda b, j: (0, j))],
        out_specs=pl.BlockSpec((S, LANES), lambda b, j: (b, j)),
        out_shape=jax.ShapeDtypeStruct((T, DC), F32),
        scratch_shapes=[pltpu.VMEM((S + 32, LANES), F32)],
        compiler_params=_cp(("arbitrary", "arbitrary")),
    )(ag, ag, wdw)


def _conv_bwd(ag, dcv, wdw, *, Bl, S, DC):
    T = ag.shape[0]
    nsc = DC // LANES
    CH = CONV_CH

    def body(a_ref, g_ref, d_ref, w_ref, da_ref, dg_ref, dw_ref, upad, dpad, wacc):
        b = pl.program_id(1)
        zeros16 = jnp.zeros((16, LANES), F32)
        upad[0:16, :] = zeros16
        upad[S + 16:S + 32, :] = zeros16
        dpad[0:16, :] = zeros16
        dpad[S + 16:S + 32, :] = zeros16

        @pl.when(b == 0)
        def _():
            wacc[...] = jnp.zeros_like(wacc)

        def fill(i, _):
            r0 = pl.multiple_of(i * CH, CH)
            a = a_ref[pl.ds(r0, CH), :]
            g = g_ref[pl.ds(r0, CH), :]
            upad[pl.ds(r0 + 16, CH), :] = a * _sigmoid(g)
            dpad[pl.ds(r0 + 16, CH), :] = d_ref[pl.ds(r0, CH), :]
            return 0
        lax.fori_loop(0, S // CH, fill, 0)

        def step(i, _):
            r0 = pl.multiple_of(i * CH, CH)
            dwin = dpad[pl.ds(r0, CH + 32), :]
            du = _conv_taps(dwin, w_ref, jnp.zeros((CH, LANES), F32), reverse=True)
            a = a_ref[pl.ds(r0, CH), :]
            g = g_ref[pl.ds(r0, CH), :]
            sg = _sigmoid(g)
            da_ref[pl.ds(r0, CH), :] = du * sg
            dg_ref[pl.ds(r0, CH), :] = du * a * sg * (1.0 - sg)
            dc = d_ref[pl.ds(r0, CH), :]
            uwin = upad[pl.ds(r0, CH + 32), :]
            n = CH + 32
            for bb in range(8):
                wb = uwin if bb == 0 else pltpu.roll(uwin, shift=n - bb, axis=0)
                for aa in range(4):
                    o = 8 * aa + bb
                    if o < 1 or o > CONV_WIDTH:
                        continue
                    k = o - 1
                    prod = dc * wb[8 * aa:8 * aa + CH, :]
                    part = prod[0:8, :]
                    for q in range(1, CH // 8):
                        part = part + prod[8 * q:8 * q + 8, :]
                    wacc[8 * k:8 * k + 8, :] += part
            part = dc[0:8, :]
            for q in range(1, CH // 8):
                part = part + dc[8 * q:8 * q + 8, :]
            wacc[8 * CONV_WIDTH:8 * CONV_WIDTH + 8, :] += part
            return 0
        lax.fori_loop(0, S // CH, step, 0)

        @pl.when(b == Bl - 1)
        def _():
            for k in range(CONV_ROWS):
                dw_ref[k:k + 1, :] = jnp.sum(wacc[8 * k:8 * k + 8, :], axis=0, keepdims=True)

    return pl.pallas_call(
        body, grid=(nsc, Bl), name="conv_bwd",
        in_specs=[pl.BlockSpec((S, LANES), lambda j, b: (b, j)),
                  pl.BlockSpec((S, LANES), lambda j, b: (b, nsc + j)),
                  pl.BlockSpec((S, LANES), lambda j, b: (b, j)),
                  pl.BlockSpec((CONV_ROWS, LANES), lambda j, b: (0, j))],
        out_specs=[pl.BlockSpec((S, LANES), lambda j, b: (b, j)),
                   pl.BlockSpec((S, LANES), lambda j, b: (b, j)),
                   pl.BlockSpec((CONV_ROWS, LANES), lambda j, b: (0, j))],
        out_shape=[jax.ShapeDtypeStruct((T, DC), F32), jax.ShapeDtypeStruct((T, DC), F32),
                   jax.ShapeDtypeStruct((CONV_ROWS, DC), F32)],
        scratch_shapes=[pltpu.VMEM((S + 32, LANES), F32), pltpu.VMEM((S + 32, LANES), F32),
                        pltpu.VMEM((8 * CONV_ROWS, LANES), F32)],
        compiler_params=_cp(("arbitrary", "arbitrary")),
    )(ag, ag, dcv, wdw)


ROWCH = 256


LOG2E = 1.4426950408889634
LN2 = 0.6931471805599453
N_EDGE = 4


def _head_mean_matrix():
    r = lax.broadcasted_iota(jnp.int32, (LANES, LANES), 0) // HEAD_DIM
    c = lax.broadcasted_iota(jnp.int32, (LANES, LANES), 1) // HEAD_DIM
    return jnp.where(r == c, 1.0 / HEAD_DIM, 0.0).astype(jnp.bfloat16)


def _head_mean(v, mm):
    hi = v.astype(jnp.bfloat16)
    lo = (v - hi.astype(F32)).astype(jnp.bfloat16)
    return _dot(hi, mm) + _dot(lo, mm)


def _stack_heads(blk, lane_lo):
    z = jnp.zeros_like(blk)
    return jnp.concatenate([jnp.where(lane_lo, blk, z), jnp.where(lane_lo, z, blk)], axis=0)


def _merge_heads(v2, lane_lo):
    return jnp.where(lane_lo, v2[:QBLK], v2[QBLK:])


def _bias_tables(bias_ref, slope_ref):
    row = lax.broadcasted_iota(jnp.int32, (2 * QBLK, 2 * QBLK), 0)
    col = lax.broadcasted_iota(jnp.int32, (2 * QBLK, 2 * QBLK), 1)
    rel = jnp.abs(col - RADIUS - (row % QBLK))
    slope = jnp.where(row < QBLK, slope_ref[0:1, 0:1], slope_ref[0:1, HEAD_DIM:HEAD_DIM + 1]) * LOG2E
    for pi, d in enumerate(DILATIONS):
        inside = jnp.where(rel <= RADIUS, -slope * (float(d) * rel.astype(F32)), NEG_INF)
        for e in range(N_EDGE):
            t = inside
            if e & 1:
                t = jnp.where(col < RADIUS, NEG_INF, t)
            if e & 2:
                t = jnp.where(col >= QBLK + RADIUS, NEG_INF, t)
            bias_ref[N_EDGE * pi + e] = t


def _edge_index(qb, nb):
    return jnp.where(qb == 0, 1, 0) + jnp.where(qb == nb - 1, 2, 0)


VIA = 4


def _residue(d, s):
    return (s % VIA) * VIA + s // VIA if d == VIA * VIA else s


def _gather_rows(src_ref, dst_ref, S, d, pad, f32_copy=None):
    n = S // d
    seg = n + 2 * RADIUS if pad else n
    step = min(n, 512)
    two_step = d == VIA * VIA and f32_copy is not None
    for s in range(d):
        base = s * seg
        if pad:
            dst_ref[base:base + RADIUS, :] = jnp.zeros((RADIUS, LANES), dst_ref.dtype)
            dst_ref[base + RADIUS + n:base + seg, :] = jnp.zeros((RADIUS, LANES), dst_ref.dtype)
            base += RADIUS
        for c0 in range(0, n, step):
            if d == 1:
                v = src_ref[c0:c0 + step, :]
            elif two_step:
                v = f32_copy[pl.ds((s // VIA) * (S // VIA) + s % VIA + c0 * VIA, step, stride=VIA), :]
            else:
                v = src_ref[pl.ds(_residue(d, s) + c0 * d, step, stride=d), :]
                if d == VIA and f32_copy is not None:
                    f32_copy[s * n + c0:s * n + c0 + step, :] = v
            dst_ref[base + c0:base + c0 + step, :] = v.astype(dst_ref.dtype)


def _scatter_rows(src_ref, dst_ref, S, d, pad, accumulate, f32_tmp=None):
    n = S // d
    seg = n + 2 * RADIUS if pad else n
    first = RADIUS if pad else 0
    _unpermute(lambda s, c0, step: src_ref[s * seg + first + c0:s * seg + first + c0 + step, :],
               dst_ref, S, d, accumulate, f32_tmp)


def _unpermute(rows_of, dst_ref, S, d, accumulate, f32_tmp):
    n = S // d
    step = min(n, 512)
    if d == VIA * VIA and f32_tmp is not None:
        for s in range(d):
            f32_tmp[pl.ds((s // VIA) * (S // VIA) + s % VIA, n, stride=VIA), :] = rows_of(s, 0, n)
        n4 = S // VIA
        _unpermute(lambda s, c0, st: f32_tmp[s * n4 + c0:s * n4 + c0 + st, :], dst_ref, S, VIA, accumulate, None)
        return
    for s in range(d):
        for c0 in range(0, n, step):
            v = rows_of(s, c0, step)
            idx = pl.ds(c0, step) if d == 1 else pl.ds(_residue(d, s) + c0 * d, step, stride=d)
            if accumulate:
                dst_ref[idx, :] = dst_ref[idx, :] + v
            else:
                dst_ref[idx, :] = v


def _zero_uncovered(acc, S, d):
    n = S // d
    if (n // QBLK) % 2:
        return
    seg = n + 2 * RADIUS
    for r in range(d):
        acc[0, r * seg + n:r * seg + seg, :] = jnp.zeros((2 * RADIUS, LANES), F32)
        acc[1, r * seg:r * seg + 2 * RADIUS, :] = jnp.zeros((2 * RADIUS, LANES), F32)


def _scatter_parity(acc, dst_ref, S, d, f32_tmp=None):
    n = S // d
    seg = n + 2 * RADIUS
    one_block = (n // QBLK) % 2 == 1

    def rows_of(s, c0, step):
        rows = slice(s * seg + RADIUS + c0, s * seg + RADIUS + c0 + step)
        return acc[s % 2, rows, :] if one_block else acc[0, rows, :] + acc[1, rows, :]

    _unpermute(rows_of, dst_ref, S, d, True, f32_tmp)


PIPE_UNROLL = 4
PIPE_SLOTS = 16
BWD_SLOTS = 12


def _pipeline(n_items, stages, unroll):
    K = len(stages)
    assert n_items % unroll == 0 and K * unroll <= (PIPE_SLOTS if K == 4 else BWD_SLOTS)
    trips = n_items // unroll
    assert trips >= K - 1

    def trip(t, static):
        for s in reversed(range(K)):
            if static and not 0 <= t - s < trips:
                continue
            for u in range(unroll):
                item = unroll * (t - s) + u
                stages[s](jnp.int32(item) if static else item)

    for t in range(K - 1):
        trip(t, True)

    def full(t, carry):
        trip(t, False)
        return carry
    lax.fori_loop(K - 1, trips, full, 0)
    for t in range(trips, trips + K - 1):
        trip(t, True)


def _attn_fwd(qkh, qkv, slopes, *, Bl, S, hosted=()):
    n3, T, _ = qkv.shape
    NS = n3 // 3
    NB = S // QBLK
    PADR = S + 2 * RADIUS * DILATIONS[-1]
    nh = len(hosted)
    plan = _WeightGather([b.shape for b in hosted]) if nh else None
    n_steps = Bl * NS

    def body(qh, kh, v_ref, slope_ref, *rest):
        o_ref, lse_ref = rest[nh:nh + 2]
        wouts = rest[nh + 2:2 * nh + 2]
        (qp, kp, vp, op, lp, onat, lnat, bias_ref, sbuf, pbuf, mbuf, lbuf, tmps) = rest[2 * nh + 2:2 * nh + 15]
        sems = rest[2 * nh + 15:]
        step = pl.program_id(0) * Bl + pl.program_id(1)
        if nh:
            @pl.when(step == 0)
            def _():
                plan.start(wouts, sems)

            @pl.when(step == (13 * n_steps) // 16)
            def _():
                plan.forward(wouts, sems)

        lane_lo = lax.broadcasted_iota(jnp.int32, (QBLK, LANES), 1) < HEAD_DIM

        @pl.when(pl.program_id(1) == 0)
        def _():
            _bias_tables(bias_ref, slope_ref)

        for pi, d in enumerate(DILATIONS):
            n = S // d
            nb = n // QBLK
            _gather_rows(qh, qp, S, d, pad=False, f32_copy=tmps.at[0])
            _gather_rows(kh, kp, S, d, pad=True, f32_copy=tmps.at[1])
            _gather_rows(v_ref, vp, S, d, pad=True, f32_copy=tmps.at[2])

            def offsets(i, nb=nb):
                r = i // nb
                return pl.multiple_of(i * QBLK, QBLK), pl.multiple_of((i + r) * QBLK, QBLK), i % nb

            def scores(i, pi=pi, nb=nb):
                q0, k0, qb = offsets(i)
                qs = _stack_heads(qp[pl.ds(q0, QBLK), :], lane_lo)
                sbuf[i % PIPE_SLOTS] = (_dot_nt(qs, kp[pl.ds(k0, 2 * QBLK), :])
                                        + bias_ref[N_EDGE * pi + _edge_index(qb, nb)])

            def rowmax(i):
                m = jnp.max(sbuf[i % PIPE_SLOTS], axis=1, keepdims=True)
                mbuf[i % PIPE_SLOTS] = jnp.broadcast_to(m, (2 * QBLK, LANES))

            def expsum(i):
                m = mbuf[i % PIPE_SLOTS]
                p = jnp.exp2(sbuf[i % PIPE_SLOTS] - jnp.concatenate([m, m], axis=1))
                pbuf[i % PIPE_SLOTS] = p.astype(MXU_DTYPE)
                lbuf[i % PIPE_SLOTS] = jnp.broadcast_to(jnp.sum(p, axis=1, keepdims=True), (2 * QBLK, LANES))

            def values(i):
                q0, k0, _ = offsets(i)
                l = lbuf[i % PIPE_SLOTS]
                o2 = _dot(pbuf[i % PIPE_SLOTS], vp[pl.ds(k0, 2 * QBLK), :]) * (1.0 / l)
                op[pl.ds(q0, QBLK), :] = _merge_heads(o2, lane_lo)
                lp[pl.ds(q0, QBLK), :] = _merge_heads(mbuf[i % PIPE_SLOTS] + jnp.log2(l), lane_lo)

            _pipeline(NB, [scores, rowmax, expsum, values], PIPE_UNROLL)
            _scatter_rows(op, onat.at[pi], S, d, pad=False, accumulate=False, f32_tmp=tmps.at[0])
            _scatter_rows(lp, lnat.at[pi], S, d, pad=False, accumulate=False, f32_tmp=tmps.at[1])

        for c0 in range(0, S, ROWCH):
            ls = [lnat[pi, c0:c0 + ROWCH, :] for pi in range(len(DILATIONS))]
            mx = jnp.maximum(jnp.maximum(ls[0], ls[1]), ls[2])
            es = [jnp.exp2(l - mx) for l in ls]
            tot = es[0] + es[1] + es[2]
            inv = 1.0 / tot
            acc = (es[0] * inv) * onat[0, c0:c0 + ROWCH, :]
            for pi in (1, 2):
                acc = acc + (es[pi] * inv) * onat[pi, c0:c0 + ROWCH, :]
            o_ref[c0:c0 + ROWCH, :] = acc
            lse_ref[c0:c0 + ROWCH, :] = mx + jnp.log2(tot)

        if nh:
            @pl.when(step == n_steps - 1)
            def _():
                plan.finish(wouts, sems)

    spec_in = lambda off: pl.BlockSpec((None, S, LANES), lambda j, b: (off * NS + j, b, 0))
    out = pl.BlockSpec((S, LANES), lambda j, b: (b, j))
    anyspec = pl.BlockSpec(memory_space=pl.ANY)
    return pl.pallas_call(
        body, grid=(NS, Bl), name="attn_fwd",
        in_specs=[spec_in(0), spec_in(1), spec_in(2),
                  pl.BlockSpec((None, 8, LANES), lambda j, b: (j, 0, 0))] + [anyspec] * nh,
        out_specs=[out, out] + [anyspec] * nh,
        out_shape=[jax.ShapeDtypeStruct((T, NS * LANES), F32)] * 2
                  + [jax.ShapeDtypeStruct(b.shape, b.dtype) for b in hosted],
        input_output_aliases={4 + w: 2 + w for w in range(nh)},
        scratch_shapes=[pltpu.VMEM((S, LANES), MXU_DTYPE), pltpu.VMEM((PADR, LANES), MXU_DTYPE),
                        pltpu.VMEM((PADR, LANES), MXU_DTYPE),
                        pltpu.VMEM((S, LANES), F32), pltpu.VMEM((S, LANES), F32),
                        pltpu.VMEM((3, S, LANES), F32), pltpu.VMEM((3, S, LANES), F32),
                        pltpu.VMEM((N_EDGE * len(DILATIONS), 2 * QBLK, 2 * QBLK), F32),
                        pltpu.VMEM((PIPE_SLOTS, 2 * QBLK, 2 * QBLK), F32),
                        pltpu.VMEM((PIPE_SLOTS, 2 * QBLK, 2 * QBLK), MXU_DTYPE),
                        pltpu.VMEM((PIPE_SLOTS, 2 * QBLK, LANES), F32), pltpu.VMEM((PIPE_SLOTS, 2 * QBLK, LANES), F32),
                        pltpu.VMEM((3, S, LANES), F32)]
                       + (plan.scratch() if nh else []),
        compiler_params=_cp(("arbitrary", "arbitrary")),
    )(qkh, qkh, qkv, slopes, *hosted)


def _attn_bwd(qkh, qkv, o, lse, do, gq2, gk2, slopes, *, Bl, S, hosted=()):
    n3, T, _ = qkv.shape
    NS = n3 // 3
    NB = S // QBLK
    PADR = S + 2 * RADIUS * DILATIONS[-1]
    QSCALE = HEAD_DIM ** -0.5
    nh = len(hosted)
    plan = _ChipExchange(nh)
    n_steps = Bl * NS

    def body(qh, kh, q_ref, k_ref, v_ref, o_ref, lse_ref, do_ref, gq_ref, gk_ref, slope_ref, *rest):
        hin = rest[:nh]
        dq_ref, dk_ref, dv_ref, gacc_ref = rest[nh:nh + 4]
        hout = rest[nh + 4:2 * nh + 4]
        (ld, qp, kp, vp, dop, ldp, dqp, dkacc, dvacc, dqn, dkn, bias_ref,
         sbuf, dpbuf, pbuf, dsbuf, tmps) = rest[2 * nh + 4:2 * nh + 21]
        sems = rest[2 * nh + 21:]
        step = pl.program_id(0) * Bl + pl.program_id(1)

        @pl.when(step == 0)
        def _():
            gacc_ref[...] = jnp.zeros_like(gacc_ref)
            if nh:
                plan.start(hin, hout, sems)

        mm = _head_mean_matrix()
        lane_lo = lax.broadcasted_iota(jnp.int32, (QBLK, LANES), 1) < HEAD_DIM

        @pl.when(pl.program_id(1) == 0)
        def _():
            _bias_tables(bias_ref, slope_ref)

        lse_lanes = lax.broadcasted_iota(jnp.int32, (ROWCH, LANES), 1) % HEAD_DIM < HEAD_DIM // 2
        for c0 in range(0, S, ROWCH):
            delta = _head_mean(do_ref[c0:c0 + ROWCH, :] * o_ref[c0:c0 + ROWCH, :], mm) * HEAD_DIM
            ld[c0:c0 + ROWCH, :] = jnp.where(lse_lanes, lse_ref[c0:c0 + ROWCH, :], delta)
            dqn[c0:c0 + ROWCH, :] = jnp.zeros((ROWCH, LANES), F32)
            dkn[c0:c0 + ROWCH, :] = jnp.zeros((ROWCH, LANES), F32)
            dv_ref[c0:c0 + ROWCH, :] = jnp.zeros((ROWCH, LANES), F32)

        for pi, d in enumerate(DILATIONS):
            n = S // d
            nb = n // QBLK
            _gather_rows(qh, qp, S, d, pad=False, f32_copy=tmps.at[0])
            _gather_rows(kh, kp, S, d, pad=True, f32_copy=tmps.at[1])
            _gather_rows(v_ref, vp, S, d, pad=True, f32_copy=tmps.at[2])
            _gather_rows(do_ref, dop, S, d, pad=False, f32_copy=tmps.at[3])
            _gather_rows(ld, ldp, S, d, pad=False, f32_copy=tmps.at[4])
            _zero_uncovered(dkacc, S, d)
            _zero_uncovered(dvacc, S, d)

            def offsets(i, nb=nb):
                r = i // nb
                return pl.multiple_of(i * QBLK, QBLK), pl.multiple_of((i + r) * QBLK, QBLK), i % nb

            def scores(i, pi=pi, nb=nb):
                q0, k0, qb = offsets(i)
                qs = _stack_heads(qp[pl.ds(q0, QBLK), :], lane_lo)
                dos = _stack_heads(dop[pl.ds(q0, QBLK), :], lane_lo)
                sbuf[i % BWD_SLOTS] = (_dot_nt(qs, kp[pl.ds(k0, 2 * QBLK), :])
                                       + bias_ref[N_EDGE * pi + _edge_index(qb, nb)])
                dpbuf[i % BWD_SLOTS] = _dot_nt(dos, vp[pl.ds(k0, 2 * QBLK), :])

            def probs(i):
                q0, _, _ = offsets(i)
                blk = ldp[pl.ds(q0, QBLK), :]
                half = HEAD_DIM // 2
                lcol = jnp.concatenate([blk[:, 0:1], blk[:, HEAD_DIM:HEAD_DIM + 1]], axis=0)
                dcol = jnp.concatenate([blk[:, half:half + 1], blk[:, HEAD_DIM + half:HEAD_DIM + half + 1]], axis=0)
                p = jnp.exp2(sbuf[i % BWD_SLOTS] - lcol)
                pbuf[i % BWD_SLOTS] = p.astype(MXU_DTYPE)
                dsbuf[i % BWD_SLOTS] = (p * (dpbuf[i % BWD_SLOTS] - dcol)).astype(MXU_DTYPE)

            def grads(i):
                q0, k0, _ = offsets(i)
                qs = _stack_heads(qp[pl.ds(q0, QBLK), :], lane_lo)
                dos = _stack_heads(dop[pl.ds(q0, QBLK), :], lane_lo)
                ds = dsbuf[i % BWD_SLOTS]
                dvacc[i % 2, pl.ds(k0, 2 * QBLK), :] = _dot_tn(pbuf[i % BWD_SLOTS], dos)
                dkacc[i % 2, pl.ds(k0, 2 * QBLK), :] = _dot_tn(ds, qs)
                dqp[pl.ds(q0, QBLK), :] = _merge_heads(_dot(ds, kp[pl.ds(k0, 2 * QBLK), :]), lane_lo)

            _pipeline(NB, [scores, probs, grads], PIPE_UNROLL)
            _scatter_rows(dqp, dqn, S, d, pad=False, accumulate=True, f32_tmp=tmps.at[0])
            _scatter_parity(dkacc, dkn, S, d, f32_tmp=tmps.at[1])
            _scatter_parity(dvacc, dv_ref, S, d, f32_tmp=tmps.at[2])

        gq_sum = jnp.zeros((8, LANES), F32)
        gk_sum = jnp.zeros((8, LANES), F32)
        for c0 in range(0, S, ROWCH):
            for src_ref, dn, g_ref, dst_ref, scale, is_q in ((q_ref, dqn, gq_ref, dq_ref, QSCALE, True),
                                                             (k_ref, dkn, gk_ref, dk_ref, LN2, False)):
                x = src_ref[c0:c0 + ROWCH, :]
                dh = dn[c0:c0 + ROWCH, :]
                rr = lax.rsqrt(_head_mean(x * x, mm) + EPS)
                e = dh * (g_ref[...] * scale)
                dst_ref[c0:c0 + ROWCH, :] = rr * e - x * (rr * rr * rr) * _head_mean(e * x, mm)
                gpart = dh * (x * rr * scale)
                acc8 = gpart[0:8, :]
                for q8 in range(1, ROWCH // 8):
                    acc8 = acc8 + gpart[8 * q8:8 * q8 + 8, :]
                if is_q:
                    gq_sum = gq_sum + acc8
                else:
                    gk_sum = gk_sum + acc8
        gacc_ref[0:1, :] += jnp.sum(gq_sum, axis=0, keepdims=True)
        gacc_ref[1:2, :] += jnp.sum(gk_sum, axis=0, keepdims=True)

        if nh:
            @pl.when(step == n_steps - 1)
            def _():
                plan.finish(hin, hout, sems)

    spec_in = lambda off: pl.BlockSpec((None, S, LANES), lambda j, b: (off * NS + j, b, 0))
    tok = pl.BlockSpec((S, LANES), lambda j, b: (b, j))
    vec = pl.BlockSpec((1, LANES), lambda j, b: (0, 0))
    slab_out = pl.BlockSpec((None, S, LANES), lambda j, b: (j, b, 0))
    f32buf = lambda rows: pltpu.VMEM((rows, LANES), F32)
    bfbuf = lambda rows: pltpu.VMEM((rows, LANES), MXU_DTYPE)
    anyspec = pl.BlockSpec(memory_space=pl.ANY)
    return pl.pallas_call(
        body, grid=(NS, Bl), name="attn_bwd",
        in_specs=[spec_in(0), spec_in(1), spec_in(0), spec_in(1), spec_in(2), tok, tok, tok, vec, vec,
                  pl.BlockSpec((None, 8, LANES), lambda j, b: (j, 0, 0))] + [anyspec] * nh,
        out_specs=[slab_out, slab_out, slab_out, pl.BlockSpec((8, LANES), lambda j, b: (0, 0))] + [anyspec] * nh,
        out_shape=[jax.ShapeDtypeStruct((NS, T, LANES), F32)] * 3 + [jax.ShapeDtypeStruct((8, LANES), F32)]
                  + [jax.ShapeDtypeStruct((3,) + h.shape[1:], h.dtype) for h in hosted],
        scratch_shapes=[f32buf(S),
                        bfbuf(S), bfbuf(PADR), bfbuf(PADR), bfbuf(S),
                        f32buf(S), f32buf(S),
                        pltpu.VMEM((2, PADR, LANES), F32), pltpu.VMEM((2, PADR, LANES), F32),
                        f32buf(S), f32buf(S),
                        pltpu.VMEM((N_EDGE * len(DILATIONS), 2 * QBLK, 2 * QBLK), F32),
                        pltpu.VMEM((BWD_SLOTS, 2 * QBLK, 2 * QBLK), F32),
                        pltpu.VMEM((BWD_SLOTS, 2 * QBLK, 2 * QBLK), F32),
                        pltpu.VMEM((BWD_SLOTS, 2 * QBLK, 2 * QBLK), MXU_DTYPE),
                        pltpu.VMEM((BWD_SLOTS, 2 * QBLK, 2 * QBLK), MXU_DTYPE),
                        pltpu.VMEM((5, S, LANES), F32)]
                       + (plan.scratch() if nh else []),
        compiler_params=_cp(("arbitrary", "arbitrary"), vmem=ATTN_BWD_VMEM),
    )(qkh, qkh, qkv, qkv, qkv, o, lse, do, gq2, gk2, slopes, *hosted)


def _layer_norm_parts(cv, g_ln, b_ln):
    mu = jnp.mean(cv, axis=-1, keepdims=True)
    cen = cv - mu
    rs = lax.rsqrt(jnp.mean(cen * cen, axis=-1, keepdims=True) + EPS)
    z = cen * rs
    return z, rs, z * g_ln + b_ln


def _ffn_fwd(x2, cv, ya, tgt, mod, g_ln, b_ln, g_ffn, w_out, w_gate, w_up, w_down, *, S, tm):
    T, D = x2.shape
    DC = cv.shape[1]
    P, Kb, _ = w_out.shape
    Fb = w_down.shape[1]
    tps = S // tm

    def body(x_ref, cv_ref, ya_ref, t_ref, mod_ref, gln_ref, bln_ref, gf_ref, wo_hbm, wg_hbm, wu_hbm, wd_hbm,
             x1_ref, ycat_ref, mix_ref, h2_ref, g_ref, u_ref, a_ref, f_ref, dy_ref, loss_ref,
             wo, wg, wu, wd, sems):
        i = pl.program_id(0)
        _load_resident(i, [(wo_hbm, wo), (wg_hbm, wg), (wu_hbm, wu), (wd_hbm, wd)], sems)

        @pl.when(i == 0)
        def _():
            loss_ref[...] = jnp.zeros_like(loss_ref)

        _, _, ln = _layer_norm_parts(cv_ref[...], gln_ref[...], bln_ref[...])
        yc = ln * _sigmoid(ln)
        ycat = jnp.concatenate([yc, ya_ref[...]], axis=1).astype(MXU_DTYPE)
        ycat_ref[...] = ycat
        mix = _dot(ycat[:, 0:Kb], wo[0])
        for p in range(1, P):
            mix = mix + _dot(ycat[:, Kb * p:Kb * (p + 1)], wo[p])
        mix_ref[...] = mix.astype(ACT_DTYPE)
        x1 = x_ref[...] + mod_ref[:, 2 * D:3 * D] * mix
        x1_ref[...] = x1
        r2 = lax.rsqrt(jnp.mean(x1 * x1, axis=-1, keepdims=True) + EPS)
        h2 = (x1 * r2 * gf_ref[...]) * (1.0 + mod_ref[:, 4 * D:5 * D]) + mod_ref[:, 3 * D:4 * D]
        h2b = h2.astype(MXU_DTYPE)
        h2_ref[...] = h2b
        f = jnp.zeros((tm, D), F32)
        for p in range(P):
            g = _dot_nt(h2b, wg[p])
            u = _dot_nt(h2b, wu[p])
            a = (g * _sigmoid(g) * u).astype(MXU_DTYPE)
            g_ref[p] = g.astype(ACT_DTYPE)
            u_ref[p] = u.astype(ACT_DTYPE)
            a_ref[p] = a
            f = f + _dot(a, wd[p])
        f_ref[...] = f.astype(ACT_DTYPE)
        err = x1 + mod_ref[:, 5 * D:6 * D] * f - t_ref[...]
        dy_ref[...] = err * (1.0 / D)
        tot = jnp.sum(_colsum(err * err), axis=1, keepdims=True)
        loss_ref[...] += tot * (0.5 / D)

    row = lambda w: pl.BlockSpec((tm, w), lambda i: (i, 0))
    vec = lambda w: pl.BlockSpec((1, w), lambda i: (0, 0))
    blk = pl.BlockSpec((P, tm, Fb), lambda i: (0, i, 0))
    anyspec = pl.BlockSpec(memory_space=pl.ANY)
    return pl.pallas_call(
        body, grid=(T // tm,), name="ffn_fwd",
        in_specs=[row(D), row(DC), row(D - DC), row(D),
                  pl.BlockSpec((None, 1, N_MOD * D), lambda i: (i // tps, 0, 0)),
                  vec(DC), vec(DC), vec(D), anyspec, anyspec, anyspec, anyspec],
        out_specs=[row(D), row(D), row(D), row(D), blk, blk, blk, row(D), row(D),
                   pl.BlockSpec((8, LANES), lambda i: (0, 0))],
        out_shape=[jax.ShapeDtypeStruct((T, D), F32), jax.ShapeDtypeStruct((T, D), MXU_DTYPE),
                   jax.ShapeDtypeStruct((T, D), ACT_DTYPE), jax.ShapeDtypeStruct((T, D), MXU_DTYPE),
                   jax.ShapeDtypeStruct((P, T, Fb), ACT_DTYPE), jax.ShapeDtypeStruct((P, T, Fb), ACT_DTYPE),
                   jax.ShapeDtypeStruct((P, T, Fb), MXU_DTYPE), jax.ShapeDtypeStruct((T, D), ACT_DTYPE),
                   jax.ShapeDtypeStruct((T, D), F32), jax.ShapeDtypeStruct((8, LANES), F32)],
        scratch_shapes=[pltpu.VMEM(w_out.shape, w_out.dtype), pltpu.VMEM(w_gate.shape, w_gate.dtype),
                        pltpu.VMEM(w_up.shape, w_up.dtype), pltpu.VMEM(w_down.shape, w_down.dtype),
                        pltpu.SemaphoreType.DMA((4,))],
        compiler_params=_cp(("arbitrary",)),
    )(x2, cv, ya, tgt, mod, g_ln, b_ln, g_ffn, w_out, w_gate, w_up, w_down)


def _ffn_bwd(dy, x1, gs, us, fo, mixb, cv, mod, g_ln, b_ln, g_ffn, w_out, w_gate, w_up, w_down, *, S, tm):
    T, D = dy.shape
    DC = cv.shape[1]
    P, Kb, _ = w_out.shape
    Fb = w_down.shape[1]
    tps = S // tm
    Bl = T // S

    def body(dy_ref, x1_ref, g_ref, u_ref, f_ref, mix_ref, cv_ref, mod_ref, gln_ref, bln_ref, gf_ref,
             wo_hbm, wg_hbm, wu_hbm, wd_hbm,
             dg_ref, du_ref, df_ref, dx1_ref, dmix_ref, dya_ref, dcv_ref, macc_ref, gacc_ref, lacc_ref,
             wo, wg, wu, wd, sems):
        i = pl.program_id(0)
        _load_resident(i, [(wo_hbm, wo), (wg_hbm, wg), (wu_hbm, wu), (wd_hbm, wd)], sems)

        @pl.when(i == 0)
        def _():
            gacc_ref[...] = jnp.zeros_like(gacc_ref)
            lacc_ref[...] = jnp.zeros_like(lacc_ref)

        @pl.when(i % tps == 0)
        def _():
            macc_ref[...] = jnp.zeros_like(macc_ref)

        dy_t = dy_ref[...]
        x1 = x1_ref[...]
        gate_f = mod_ref[:, 5 * D:6 * D]
        macc_ref[2:3, :] += _colsum(dy_t * f_ref[...].astype(F32))
        dfb = (dy_t * gate_f).astype(MXU_DTYPE)
        df_ref[...] = dfb
        dh2 = jnp.zeros((tm, D), F32)
        for p in range(P):
            da = _dot_nt(dfb, wd[p])
            g = g_ref[p].astype(F32)
            u = u_ref[p].astype(F32)
            sg = _sigmoid(g)
            dgp = (da * u * (sg * (1.0 + g * (1.0 - sg)))).astype(MXU_DTYPE)
            dup = (da * (g * sg)).astype(MXU_DTYPE)
            dg_ref[p] = dgp
            du_ref[p] = dup
            dh2 = dh2 + _dot(dgp, wg[p]) + _dot(dup, wu[p])
        r2 = lax.rsqrt(jnp.mean(x1 * x1, axis=-1, keepdims=True) + EPS)
        xr = x1 * r2
        n2 = xr * gf_ref[...]
        macc_ref[0:1, :] += _colsum(dh2)
        macc_ref[1:2, :] += _colsum(dh2 * n2)
        dn2 = dh2 * (1.0 + mod_ref[:, 4 * D:5 * D])
        gacc_ref[0:1, :] += _colsum(dn2 * xr)
        e = dn2 * gf_ref[...]
        dx1 = dy_t + r2 * e - xr * (r2 * jnp.mean(e * xr, axis=-1, keepdims=True))
        dx1_ref[...] = dx1
        macc_ref[3:4, :] += _colsum(dx1 * mix_ref[...].astype(F32))
        dmixb = (dx1 * mod_ref[:, 2 * D:3 * D]).astype(MXU_DTYPE)
        dmix_ref[...] = dmixb
        parts = [_dot_nt(dmixb, wo[p]) for p in range(P)]
        dycat = jnp.concatenate(parts, axis=1) if P > 1 else parts[0]
        dya_ref[...] = dycat[:, DC:]
        dyc = dycat[:, :DC]
        z, rs, ln = _layer_norm_parts(cv_ref[...], gln_ref[...], bln_ref[...])
        sg = _sigmoid(ln)
        dln = dyc * (sg * (1.0 + ln * (1.0 - sg)))
        lacc_ref[0:1, :] += _colsum(dln * z)
        lacc_ref[1:2, :] += _colsum(dln)
        dz = dln * gln_ref[...]
        dcv_ref[...] = rs * (dz - jnp.mean(dz, axis=-1, keepdims=True) - z * jnp.mean(dz * z, axis=-1, keepdims=True))

    row = lambda w: pl.BlockSpec((tm, w), lambda i: (i, 0))
    vec = lambda w: pl.BlockSpec((1, w), lambda i: (0, 0))
    blk = pl.BlockSpec((P, tm, Fb), lambda i: (0, i, 0))
    anyspec = pl.BlockSpec(memory_space=pl.ANY)
    return pl.pallas_call(
        body, grid=(T // tm,), name="ffn_bwd",
        in_specs=[row(D), row(D), blk, blk, row(D), row(D), row(DC),
                  pl.BlockSpec((None, 1, N_MOD * D), lambda i: (i // tps, 0, 0)),
                  vec(DC), vec(DC), vec(D), anyspec, anyspec, anyspec, anyspec],
        out_specs=[blk, blk, row(D), row(D), row(D), row(D - DC), row(DC),
                   pl.BlockSpec((None, 8, D), lambda i: (i // tps, 0, 0)),
                   pl.BlockSpec((8, D), lambda i: (0, 0)), pl.BlockSpec((8, DC), lambda i: (0, 0))],
        out_shape=[jax.ShapeDtypeStruct((P, T, Fb), MXU_DTYPE), jax.ShapeDtypeStruct((P, T, Fb), MXU_DTYPE),
                   jax.ShapeDtypeStruct((T, D), MXU_DTYPE), jax.ShapeDtypeStruct((T, D), F32),
                   jax.ShapeDtypeStruct((T, D), MXU_DTYPE), jax.ShapeDtypeStruct((T, D - DC), F32),
                   jax.ShapeDtypeStruct((T, DC), F32), jax.ShapeDtypeStruct((Bl, 8, D), F32),
                   jax.ShapeDtypeStruct((8, D), F32), jax.ShapeDtypeStruct((8, DC), F32)],
        scratch_shapes=[pltpu.VMEM(w_out.shape, w_out.dtype), pltpu.VMEM(w_gate.shape, w_gate.dtype),
                        pltpu.VMEM(w_up.shape, w_up.dtype), pltpu.VMEM(w_down.shape, w_down.dtype),
                        pltpu.SemaphoreType.DMA((4,))],
        compiler_params=_cp(("arbitrary",)),
    )(dy, x1, gs, us, fo, mixb, cv, mod, g_ln, b_ln, g_ffn, w_out, w_gate, w_up, w_down)


def _in_bwd(da, dg, dq, dk, dv, x2, dx1, mod, g_mix, w_in, *, S, tm):
    T, D = x2.shape
    P, _, Nb = w_in.shape
    DC = da.shape[1]
    NS = dq.shape[0]
    n_in = P * Nb
    tps = S // tm
    Bl = T // S

    def body(da_ref, dg_ref, dq_ref, dk_ref, dv_ref, x_ref, dx1_ref, mod_ref, g_ref, w_ref,
             dx_ref, dproj_ref, macc_ref, gacc_ref):
        i = pl.program_id(0)

        @pl.when(i == 0)
        def _():
            gacc_ref[...] = jnp.zeros_like(gacc_ref)

        @pl.when(i % tps == 0)
        def _():
            macc_ref[...] = jnp.zeros_like(macc_ref)

        pieces = [da_ref[...], dg_ref[...]] + [r[j] for r in (dq_ref, dk_ref, dv_ref) for j in range(NS)]
        dproj = jnp.concatenate(pieces, axis=1).astype(MXU_DTYPE)
        dproj_ref[...] = dproj
        dh = _dot_nt(dproj[:, 0:Nb], w_ref[0])
        for p in range(1, P):
            dh = dh + _dot_nt(dproj[:, Nb * p:Nb * (p + 1)], w_ref[p])
        x = x_ref[...]
        r = lax.rsqrt(jnp.mean(x * x, axis=-1, keepdims=True) + EPS)
        xr = x * r
        macc_ref[0:1, :] += _colsum(dh)
        macc_ref[1:2, :] += _colsum(dh * (xr * g_ref[...]))
        dn = dh * (1.0 + mod_ref[:, D:2 * D])
        gacc_ref[0:1, :] += _colsum(dn * xr)
        e = dn * g_ref[...]
        dx_ref[...] = dx1_ref[...] + r * e - xr * (r * jnp.mean(e * xr, axis=-1, keepdims=True))

    row = lambda w: pl.BlockSpec((tm, w), lambda i: (i, 0))
    slab = pl.BlockSpec((NS, tm, LANES), lambda i: (0, i, 0))
    return pl.pallas_call(
        body, grid=(T // tm,), name="in_bwd",
        in_specs=[row(DC), row(DC), slab, slab, slab, row(D), row(D),
                  pl.BlockSpec((None, 1, N_MOD * D), lambda i: (i // tps, 0, 0)),
                  pl.BlockSpec((1, D), lambda i: (0, 0)),
                  pl.BlockSpec((P, D, Nb), lambda i: (0, 0, 0))],
        out_specs=[row(D), row(n_in), pl.BlockSpec((None, 8, D), lambda i: (i // tps, 0, 0)),
                   pl.BlockSpec((8, D), lambda i: (0, 0))],
        out_shape=[jax.ShapeDtypeStruct((T, D), F32), jax.ShapeDtypeStruct((T, n_in), MXU_DTYPE),
                   jax.ShapeDtypeStruct((Bl, 8, D), F32), jax.ShapeDtypeStruct((8, D), F32)],
        compiler_params=_cp(("arbitrary",)),
    )(da, dg, dq, dk, dv, x2, dx1, mod, g_mix, w_in)


def _wgrad(a, b, *, P, name, tk, split=None, host=None):
    a_blk, b_blk = a.ndim == 3, b.ndim == 3
    plan, h_in, h_out = host if host is not None else (None, (), ())
    ni, no = len(h_in), len(h_out)
    T = a.shape[-2]
    if a_blk:
        R, C = a.shape[2], b.shape[1]
        a_of = lambda av, p: av[p]
        b_of = lambda bv, p: bv[...]
    elif b_blk:
        R, C = a.shape[1], b.shape[2]
        a_of = lambda av, p: av[...]
        b_of = lambda bv, p: bv[p]
    elif split == "a":
        R, C = a.shape[1] // P, b.shape[1]
        a_of = lambda av, p: av[:, R * p:R * (p + 1)]
        b_of = lambda bv, p: bv[...]
    else:
        R, C = a.shape[1], b.shape[1] // P
        a_of = lambda av, p: av[...]
        b_of = lambda bv, p: bv[:, C * p:C * (p + 1)]

    n_steps = T // tk

    def body(a_ref, b_ref, *rest):
        hin, o_ref, hout, sems = rest[:ni], rest[ni], rest[ni + 1:ni + 1 + no], rest[ni + 1 + no:]
        step = pl.program_id(0)

        @pl.when(step == 0)
        def _():
            o_ref[...] = jnp.zeros_like(o_ref)
            if plan is not None:
                plan.start(hin, hout, sems)

        if plan is not None:
            @pl.when(step == n_steps // 2)
            def _():
                plan.forward(hin, hout, sems)

        for p in range(P):
            o_ref[p] += _dot_tn(a_of(a_ref, p), b_of(b_ref, p))

        if plan is not None:
            @pl.when(step == n_steps - 1)
            def _():
                plan.finish(hin, hout, sems)

    def spec(v):
        if v.ndim == 3:
            return pl.BlockSpec((P, tk, v.shape[2]), lambda k: (0, k, 0))
        return pl.BlockSpec((tk, v.shape[1]), lambda k: (k, 0))

    anyspec = pl.BlockSpec(memory_space=pl.ANY)
    res = pl.pallas_call(
        body, grid=(n_steps,), name=name,
        in_specs=[spec(a), spec(b)] + [anyspec] * ni,
        out_specs=[pl.BlockSpec((P, R, C), lambda k: (0, 0, 0))] + [anyspec] * no,
        out_shape=[jax.ShapeDtypeStruct((P, R, C), F32)] + list(h_out),
        scratch_shapes=plan.scratch() if plan is not None else [],
        compiler_params=_cp(("arbitrary",)),
    )(a, b, *h_in)
    return res if plan is not None else res[0]


TM_IN = 512
TM_FFN = 256
TK_WGRAD = 1024


def _alibi_slabs(n_slab):
    heads = 2 * n_slab
    slopes = 2.0 ** (-8.0 * np.arange(1, heads + 1) / heads)
    return jnp.asarray(np.broadcast_to(np.repeat(slopes.reshape(n_slab, 1, 2), HEAD_DIM, axis=2), (n_slab, 8, LANES)),
                       dtype=F32)


def _local_step(x, tgt, mod, g_mix, wdw, g_ln, b_ln, g_q, g_k, g_ffn, w_in, w_out, w_gate, w_up, w_down,
                pc_idx=None):
    Bl, S, D = x.shape
    T = Bl * S
    DC = g_ln.shape[1]
    P = w_in.shape[0]
    n_slab = (D - DC) // LANES
    x2 = x.reshape(T, D)
    t2 = tgt.reshape(T, D)
    mod3 = mod.reshape(Bl, 1, N_MOD * D)
    gq2 = jnp.tile(g_q, (1, LANES // HEAD_DIM))
    gk2 = jnp.tile(g_k, (1, LANES // HEAD_DIM))
    slopes = _alibi_slabs(n_slab)

    ag, qkv, qkh, h1 = _fwd_in(x2, mod3, g_mix, gq2, gk2, w_in, S=S, tm=TM_IN, n_ag=2 * DC)
    cv = _conv_fwd(ag, wdw, Bl=Bl, S=S, DC=DC)
    if pc_idx is not None:
        ya, lse, w_out, w_gate, w_up, w_down = _attn_fwd(qkh, qkv, slopes, Bl=Bl, S=S,
                                                         hosted=(w_out, w_gate, w_up, w_down))
    else:
        ya, lse = _attn_fwd(qkh, qkv, slopes, Bl=Bl, S=S)
    x1, ycat, mixb, h2, gs, us, acts, fo, dy, lossb = _ffn_fwd(
        x2, cv, ya, t2, mod3, g_ln, b_ln, g_ffn, w_out, w_gate, w_up, w_down, S=S, tm=TM_FFN)
    dgs, dus, dfb, dx1, dmixb, dya, dcv, macc_f, gacc_f, lacc = _ffn_bwd(
        dy, x1, gs, us, fo, mixb, cv, mod3, g_ln, b_ln, g_ffn, w_out, w_gate, w_up, w_down, S=S, tm=TM_FFN)
    wg = functools.partial(_wgrad, P=P, tk=TK_WGRAD)
    out = {}
    if pc_idx is None:
        grads = dict(w_down=wg(acts, dfb, name="wgrad_down"), w_gate=wg(dgs, h2, name="wgrad_gate"),
                     w_up=wg(dus, h2, name="wgrad_up"), w_out=wg(ycat, dmixb, name="wgrad_out", split="a"))
        dq, dk, dv, gqk = _attn_bwd(qkh, qkv, ya, lse, dya, gq2, gk2, slopes, Bl=Bl, S=S)
    else:
        g_down = wg(acts, dfb, name="wgrad_down")
        g_gate, r_down = wg(dgs, h2, name="wgrad_gate", host=_sibling_host([g_down]))
        g_up, r_gate = wg(dus, h2, name="wgrad_up", host=_sibling_host([g_gate]))
        g_out, r_up = wg(ycat, dmixb, name="wgrad_out", split="a", host=_sibling_host([g_up]))
        (r_out,) = _rs_sibling([g_out], "rs_sibling_out")
        grads = dict(w_down=g_down, w_gate=g_gate, w_up=g_up, w_out=g_out)
        sums = _pair_add([grads[nm] for nm in EARLY_WEIGHTS], [r_down, r_gate, r_up, r_out], pc_idx, "pair_add_early")
        res = _attn_bwd(qkh, qkv, ya, lse, dya, gq2, gk2, slopes, Bl=Bl, S=S, hosted=tuple(sb for _, sb in sums))
        dq, dk, dv, gqk = res[:4]
        out["early_sums"] = [s32 for s32, _ in sums]
        out["early_recv"] = list(res[4:])
    da, dg, dwdw = _conv_bwd(ag, dcv, wdw, Bl=Bl, S=S, DC=DC)
    dx, dprojb, macc_m, gacc_m = _in_bwd(da, dg, dq, dk, dv, x2, dx1, mod3, g_mix, w_in, S=S, tm=TM_IN)
    packed = _pack_small(macc_m, macc_f, gacc_m, gacc_f, lacc, gqk, dwdw, lossb)
    if pc_idx is None:
        grads["w_in"] = wg(h1, dprojb, name="wgrad_in", split="b")
    else:
        grads["w_in"], out["gathered_small"] = wg(h1, dprojb, name="wgrad_in", split="b",
                                                  host=_small_gather_host(packed))
    out.update(dx=dx.reshape(Bl, S, D), grads=grads, packed=packed)
    return out


EARLY_WEIGHTS = ("w_down", "w_gate", "w_up", "w_out")


def _small_layout(Bl):
    return 8 * Bl, 8 * Bl + 8, 8 * Bl + 8 + CONV_ROWS


def _pack_small(macc_m, macc_f, gacc_m, gacc_f, lacc, gqk, dwdw, lossb):
    Bl, _, D = macc_m.shape
    DC = lacc.shape[1]
    assert 2 * DC <= D
    SMALL_GAIN_ROW, SMALL_TAP_ROW, SMALL_ROWS = _small_layout(Bl)

    def body(mm_ref, mf_ref, gm_ref, gf_ref, la_ref, qk_ref, dw_ref, loss_ref, o_ref):
        o_ref[...] = jnp.zeros_like(o_ref)
        for b in range(Bl):
            o_ref[8 * b + 0:8 * b + 2, :] = mm_ref[b, 0:2, :]
            o_ref[8 * b + 2:8 * b + 3, :] = mf_ref[b, 3:4, :]
            o_ref[8 * b + 3:8 * b + 6, :] = mf_ref[b, 0:3, :]
        r = SMALL_GAIN_ROW
        o_ref[r:r + 1, :] = gm_ref[0:1, :]
        o_ref[r + 1:r + 2, :] = gf_ref[0:1, :]
        o_ref[r + 2:r + 3, 0:DC] = la_ref[0:1, :]
        o_ref[r + 2:r + 3, DC:2 * DC] = la_ref[1:2, :]
        qk = qk_ref[0:2, 0:HEAD_DIM] + qk_ref[0:2, HEAD_DIM:2 * HEAD_DIM]
        o_ref[r + 3:r + 4, 0:HEAD_DIM] = qk[0:1, :]
        o_ref[r + 3:r + 4, HEAD_DIM:2 * HEAD_DIM] = qk[1:2, :]
        o_ref[r + 4:r + 5, 0:LANES] = loss_ref[0:1, :]
        o_ref[SMALL_TAP_ROW:SMALL_TAP_ROW + CONV_ROWS, 0:DC] = dw_ref[...]

    return pl.pallas_call(body, name="pack_small", out_shape=jax.ShapeDtypeStruct((SMALL_ROWS, D), F32),
                          compiler_params=_cp())(macc_m, macc_f, gacc_m, gacc_f, lacc, gqk, dwdw, lossb)


def _row_tile(rows, cap=512):
    if rows <= cap:
        return rows
    best = rows
    for t in range(8, cap + 1, 8):
        if rows % t == 0:
            best = t
    return best


def _cast_weight(w, pidx, name):
    def body(p_ref, w_ref, o_ref):
        o_ref[...] = w_ref[...].astype(MXU_DTYPE)
    R, C = w.shape
    tr = _row_tile(R)
    return pl.pallas_call(
        body, name=name,
        grid_spec=pltpu.PrefetchScalarGridSpec(
            num_scalar_prefetch=1, grid=(R // tr,),
            in_specs=[pl.BlockSpec((tr, C), lambda i, p: (i, 0))],
            out_specs=pl.BlockSpec((None, tr, C), lambda i, p: (p[0], i, 0))),
        out_shape=jax.ShapeDtypeStruct((4, R, C), MXU_DTYPE),
    )(pidx, w)


def _pair_add(gs, recvs, pc_idx, name):
    n = len(gs)
    P = gs[0].shape[0]
    halves = [(g.shape[1] // 2, g.shape[2]) for g in gs]

    def body(pc_ref, *refs):
        for k in range(n):
            g_ref, r_ref, o_ref, ob_ref = refs[k], refs[n + k], refs[2 * n + 2 * k], refs[2 * n + 2 * k + 1]
            s = g_ref[...] + r_ref[...]
            ob_ref[...] = s.astype(jnp.bfloat16)

            @pl.when(pl.program_id(0) == pc_ref[0])
            def _(o_ref=o_ref, s=s):
                o_ref[...] = s

    res = pl.pallas_call(
        body, name=name,
        grid_spec=pltpu.PrefetchScalarGridSpec(
            num_scalar_prefetch=1, grid=(P,),
            in_specs=[pl.BlockSpec((None,) + h, lambda p, pc: (p, pc[1], 0)) for h in halves]
                     + [pl.BlockSpec((None,) + h, lambda p, pc: (p, 0, 0)) for h in halves],
            out_specs=[spec for h in halves for spec in (pl.BlockSpec(h, lambda p, pc: (0, 0)),
                                                         pl.BlockSpec((None,) + h, lambda p, pc: (p, 0, 0)))]),
        out_shape=[shape for h in halves for shape in (jax.ShapeDtypeStruct(h, F32),
                                                       jax.ShapeDtypeStruct((P,) + h, jnp.bfloat16))],
        compiler_params=_cp(),
    )(pc_idx, *gs, *recvs)
    return [(res[2 * k], res[2 * k + 1]) for k in range(n)]


def _final_add(owns, recvs, pc_idx, name):
    n = len(owns)

    def body(pc_ref, *refs):
        for k in range(n):
            acc = refs[k][...]
            for j in range(3):
                acc = acc + refs[n + k][j].astype(F32)
            refs[2 * n + k][...] = acc

    return pl.pallas_call(
        body, name=name,
        grid_spec=pltpu.PrefetchScalarGridSpec(
            num_scalar_prefetch=1, grid=(1,),
            in_specs=[pl.BlockSpec(o.shape, lambda i, pc: (0, 0)) for o in owns]
                     + [pl.BlockSpec((3,) + o.shape, lambda i, pc: (0, 0, 0)) for o in owns],
            out_specs=[pl.BlockSpec(o.shape, lambda i, pc: (pc[1], 0)) for o in owns]),
        out_shape=[jax.ShapeDtypeStruct((2 * o.shape[0], o.shape[1]), F32) for o in owns],
        compiler_params=_cp(),
    )(pc_idx, *owns, *recvs)


def _adamw_update(w_ref, g_ref, m_ref, v_ref, d_ref, nm_ref, nv_ref):
    c1 = 1.0 - ADAM_B1 ** ADAM_STEP
    c2 = 1.0 - ADAM_B2 ** ADAM_STEP
    gg = g_ref[...]
    nm = ADAM_B1 * m_ref[...] + (1.0 - ADAM_B1) * gg
    nv = ADAM_B2 * v_ref[...] + (1.0 - ADAM_B2) * (gg * gg)
    nm_ref[...] = nm
    nv_ref[...] = nv
    d_ref[...] = -ADAM_LR * ((nm / c1) / (jnp.sqrt(nv / c2) + ADAM_EPS) + ADAM_WD * w_ref[...])


def _adamw(w, g, m, v, name):
    R, C = w.shape
    tr = _row_tile(R, 256)
    spec = pl.BlockSpec((tr, C), lambda i: (i, 0))
    return pl.pallas_call(
        functools.partial(_adamw_update), grid=(R // tr,), name=name,
        in_specs=[spec] * 4, out_specs=[spec] * 3,
        out_shape=[jax.ShapeDtypeStruct((R, C), F32)] * 3,
    )(w, g, m, v)


def _startup(first, w_ada, b_cols, w_in_buf, *, Bl):
    rows, D = first.shape
    NA = w_ada.shape[1]
    n_dev = 8
    g_w = _WeightGather([w_in_buf.shape])
    g_c = _SmallGather(rows)
    g_m = _SmallGather(n_dev * Bl)

    def body(first_ref, wada_ref, b_ref, win_in, g0_ref, call_ref, gm_ref, win_out, modp,
             ws0, ws1, cs0, cs1, cs2, ms0, ms1, ms2):
        g_w.start([win_out], (ws0, ws1))
        for phase in (g_c.start, g_c.forward, g_c.finish):
            phase([first_ref], [g0_ref], (cs0, cs1, cs2))
        for d in range(n_dev):
            call_ref[Bl * d:Bl * (d + 1), :] = g0_ref[rows * d:rows * d + Bl, :]
        c = call_ref[...]
        modp[...] = jnp.dot(c * _sigmoid(c), wada_ref[...], preferred_element_type=F32,
                            precision=lax.Precision.HIGH) + b_ref[...]
        for phase in (g_m.start, g_m.forward, g_m.finish):
            phase([modp], [gm_ref], (ms0, ms1, ms2))
        g_w.forward([win_out], (ws0, ws1))
        g_w.finish([win_out], (ws0, ws1))

    vmem = pl.BlockSpec(memory_space=pltpu.VMEM)
    anyspec = pl.BlockSpec(memory_space=pl.ANY)
    return pl.pallas_call(
        body, name="startup",
        in_specs=[vmem, vmem, vmem, anyspec], out_specs=[vmem, vmem, vmem, anyspec],
        out_shape=[jax.ShapeDtypeStruct((n_dev * rows, D), F32), jax.ShapeDtypeStruct((n_dev * Bl, D), F32),
                   jax.ShapeDtypeStruct((n_dev * n_dev * Bl, NA), F32),
                   jax.ShapeDtypeStruct(w_in_buf.shape, w_in_buf.dtype)],
        input_output_aliases={3: 3},
        scratch_shapes=[pltpu.VMEM((n_dev * Bl, NA), F32)] + g_w.scratch() + g_c.scratch() + g_m.scratch(),
        compiler_params=_cp(),
    )(first, w_ada, b_cols, w_in_buf)


def _ada_bwd(c_all, dmod_cols):
    def body(c_ref, d_ref, o_ref):
        c = c_ref[...]
        o_ref[...] = _dot_tn((c * _sigmoid(c)).astype(MXU_DTYPE), d_ref[...].astype(MXU_DTYPE))
    return pl.pallas_call(
        body, name="ada_bwd", out_shape=jax.ShapeDtypeStruct((c_all.shape[1], dmod_cols.shape[1]), F32),
        compiler_params=_cp(),
    )(c_all, dmod_cols)


def _small_reduce(gathered, n_dev, Bl):
    mod_rows, _, rows = _small_layout(Bl)
    width = gathered.shape[1]

    def body(g_ref, red_ref, bada_ref):
        acc = g_ref[0:rows, :]
        for d in range(1, n_dev):
            acc = acc + g_ref[d * rows:(d + 1) * rows, :]
        red_ref[...] = acc[mod_rows:, :]
        b = acc[0:8, :]
        for q in range(1, Bl):
            b = b + acc[8 * q:8 * q + 8, :]
        bada_ref[...] = b
    return pl.pallas_call(
        body, name="small_reduce",
        out_shape=[jax.ShapeDtypeStruct((rows - mod_rows, width), F32), jax.ShapeDtypeStruct((8, width), F32)],
        compiler_params=_cp(),
    )(gathered)


def _mesh_pos():
    return lax.axis_index("x"), lax.axis_index("y"), lax.axis_index("c")


def _other_chips(x, y):
    return [(1 - x, y), (x, 1 - y), (1 - x, 1 - y)]


class _WeightGather:
    def __init__(self, shapes):
        self.shapes = shapes
        self.n = len(shapes)

    def scratch(self):
        return [pltpu.SemaphoreType.DMA((6 * self.n,)), pltpu.SemaphoreType.DMA((6 * self.n,))]

    def _copy(self, outs, sems, w, k, slot, h, to):
        r2 = self.shapes[w][1] // 2
        blk = outs[w].at[slot, pl.ds(h * r2, r2), :]
        return pltpu.make_async_remote_copy(
            src_ref=blk, dst_ref=blk, send_sem=sems[0].at[6 * w + k], recv_sem=sems[1].at[6 * w + k],
            device_id=to, device_id_type=MESH_DEV)

    def start(self, outs, sems):
        x, y, c = _mesh_pos()
        for w in range(self.n):
            for k, chip in enumerate(_other_chips(x, y)):
                self._copy(outs, sems, w, k, 2 * x + y, c, (*chip, c)).start()

    def forward(self, outs, sems):
        x, y, c = _mesh_pos()
        for w in range(self.n):
            for k, chip in enumerate(_other_chips(x, y)):
                slot = 2 * chip[0] + chip[1]
                self._copy(outs, sems, w, k, slot, c, (x, y, 1 - c)).wait_recv()
                self._copy(outs, sems, w, 3 + k, slot, c, (x, y, 1 - c)).start()

    def finish(self, outs, sems):
        x, y, c = _mesh_pos()
        for w in range(self.n):
            for k, chip in enumerate(_other_chips(x, y)):
                slot = 2 * chip[0] + chip[1]
                self._copy(outs, sems, w, 3 + k, slot, 1 - c, (x, y, 1 - c)).wait_recv()
                self._copy(outs, sems, w, k, 2 * x + y, c, (*chip, c)).wait_send()
                self._copy(outs, sems, w, 3 + k, slot, c, (x, y, 1 - c)).wait_send()


class _SiblingExchange:
    def __init__(self, shapes):
        self.shapes = shapes

    def scratch(self):
        n = sum(s[0] for s in self.shapes)
        return [pltpu.SemaphoreType.DMA((n,)), pltpu.SemaphoreType.DMA((n,))]

    def out_shapes(self, dtype):
        return [jax.ShapeDtypeStruct((s[0], s[1] // 2, s[2]), dtype) for s in self.shapes]

    def _copies(self, ins, outs, sems):
        x, y, c = _mesh_pos()
        cps, k = [], 0
        for w, (P, R, _) in enumerate(self.shapes):
            r2 = R // 2
            for p in range(P):
                cps.append(pltpu.make_async_remote_copy(
                    src_ref=ins[w].at[p, pl.ds((1 - c) * r2, r2), :], dst_ref=outs[w].at[p],
                    send_sem=sems[0].at[k], recv_sem=sems[1].at[k],
                    device_id=(x, y, 1 - c), device_id_type=MESH_DEV))
                k += 1
        return cps

    def start(self, ins, outs, sems):
        for cp in self._copies(ins, outs, sems):
            cp.start()

    def forward(self, ins, outs, sems):
        pass

    def finish(self, ins, outs, sems):
        for cp in self._copies(ins, outs, sems):
            cp.wait()


def _sibling_host(grads):
    plan = _SiblingExchange([g.shape for g in grads])
    return plan, tuple(grads), tuple(plan.out_shapes(grads[0].dtype))


def _rs_sibling(grads, name):
    n = len(grads)
    plan, _, out_shapes = _sibling_host(grads)

    def body(*refs):
        ins, outs, sems = refs[:n], refs[n:2 * n], refs[2 * n:]
        plan.start(ins, outs, sems)
        plan.finish(ins, outs, sems)

    anyspec = pl.BlockSpec(memory_space=pl.ANY)
    return pl.pallas_call(
        body, name=name, out_shape=list(out_shapes),
        in_specs=[anyspec] * n, out_specs=[anyspec] * n, scratch_shapes=plan.scratch(),
    )(*grads)


class _SmallGather:
    def __init__(self, m_per):
        self.m = m_per

    def scratch(self):
        return [pltpu.SemaphoreType.DMA((7,)), pltpu.SemaphoreType.DMA((7,)), pltpu.SemaphoreType.DMA]

    def _rows(self, out, pos):
        px, py, pc = pos
        return out.at[pl.ds((4 * px + 2 * py + pc) * self.m, self.m), :]

    def _copy(self, out, sems, k, block, to, src=None):
        dst = self._rows(out, block)
        return pltpu.make_async_remote_copy(
            src_ref=dst if src is None else src, dst_ref=dst, send_sem=sems[0].at[k], recv_sem=sems[1].at[k],
            device_id=to, device_id_type=MESH_DEV)

    def start(self, ins, outs, sems):
        x, y, c = _mesh_pos()
        me = (x, y, c)
        pltpu.make_async_copy(ins[0], self._rows(outs[0], me), sems[2]).start()
        self._copy(outs[0], sems, 0, me, (x, y, 1 - c), src=ins[0]).start()
        for j, chip in enumerate(_other_chips(x, y)):
            self._copy(outs[0], sems, 1 + j, me, (*chip, c), src=ins[0]).start()

    def forward(self, ins, outs, sems):
        x, y, c = _mesh_pos()
        for j, chip in enumerate(_other_chips(x, y)):
            self._copy(outs[0], sems, 1 + j, (*chip, c), (x, y, c)).wait_recv()
            self._copy(outs[0], sems, 4 + j, (*chip, c), (x, y, 1 - c)).start()

    def finish(self, ins, outs, sems):
        x, y, c = _mesh_pos()
        me = (x, y, c)
        self._copy(outs[0], sems, 0, (x, y, 1 - c), me).wait_recv()
        for j, chip in enumerate(_other_chips(x, y)):
            self._copy(outs[0], sems, 4 + j, (*chip, 1 - c), me).wait_recv()
        self._copy(outs[0], sems, 0, me, (x, y, 1 - c), src=ins[0]).wait_send()
        for j, chip in enumerate(_other_chips(x, y)):
            self._copy(outs[0], sems, 1 + j, me, (*chip, c), src=ins[0]).wait_send()
            self._copy(outs[0], sems, 4 + j, (*chip, c), (x, y, 1 - c)).wait_send()
        pltpu.make_async_copy(ins[0], self._rows(outs[0], me), sems[2]).wait()


def _small_gather_host(packed):
    m, n = packed.shape
    return _SmallGather(m), (packed,), (jax.ShapeDtypeStruct((8 * m, n), packed.dtype),)


class _ChipExchange:
    def __init__(self, n):
        self.n = n

    def scratch(self):
        return [pltpu.SemaphoreType.DMA((3 * self.n,)), pltpu.SemaphoreType.DMA((3 * self.n,))]

    def _copies(self, ins, outs, sems):
        x, y, c = _mesh_pos()
        return [pltpu.make_async_remote_copy(
            src_ref=ins[w].at[2 * chip[0] + chip[1]], dst_ref=outs[w].at[k],
            send_sem=sems[0].at[3 * w + k], recv_sem=sems[1].at[3 * w + k],
            device_id=(*chip, c), device_id_type=MESH_DEV)
            for w in range(self.n) for k, chip in enumerate(_other_chips(x, y))]

    def start(self, ins, outs, sems):
        for cp in self._copies(ins, outs, sems):
            cp.start()

    def forward(self, ins, outs, sems):
        pass

    def finish(self, ins, outs, sems):
        for cp in self._copies(ins, outs, sems):
            cp.wait()


def _rs_final(bufs, name, chips=()):
    n, nc = len(bufs), len(chips)
    plan = _ChipExchange(nc)

    def body(*refs):
        cin = refs[n:n + nc]
        outs = refs[n + nc:2 * n + nc]
        cout = refs[2 * n + nc:2 * n + 2 * nc]
        send_sems, recv_sems = refs[2 * n + 2 * nc:2 * n + 2 * nc + 2]
        csems = refs[2 * n + 2 * nc + 2:]
        x, y, c = _mesh_pos()
        if nc:
            plan.start(cin, cout, csems)
        cps = []
        for w in range(n):
            r2 = bufs[w].shape[0] // 2
            mine = outs[w].at[pl.ds(c * r2, r2), :]
            cps.append(pltpu.make_async_remote_copy(
                src_ref=mine, dst_ref=mine, send_sem=send_sems.at[w], recv_sem=recv_sems.at[w],
                device_id=(x, y, 1 - c), device_id_type=MESH_DEV))
            cps[-1].start()
        for cp in cps:
            cp.wait()
        if nc:
            plan.finish(cin, cout, csems)

    anyspec = pl.BlockSpec(memory_space=pl.ANY)
    return pl.pallas_call(
        body, name=name,
        out_shape=[jax.ShapeDtypeStruct(b.shape, b.dtype) for b in bufs]
                  + [jax.ShapeDtypeStruct((3,) + s.shape[1:], s.dtype) for s in chips],
        in_specs=[anyspec] * (n + nc), out_specs=[anyspec] * (n + nc),
        input_output_aliases={w: w for w in range(n)},
        scratch_shapes=[pltpu.SemaphoreType.DMA((n,)), pltpu.SemaphoreType.DMA((n,))] + (plan.scratch() if nc else []),
    )(*bufs, *chips)


BIG = ("w_in", "w_out", "w_gate", "w_up", "w_down")
TRANSPOSED = ("w_gate", "w_up")
WEIGHTS = ("w_ada", "b_ada", "g_mix", "w_in", "w_dw", "b_dw", "g_conv_ln", "b_conv_ln", "g_q", "g_k",
           "w_out", "g_ffn", "w_gate", "w_up", "w_down")


def _pad_to(a, rows, cols):
    return jnp.pad(a, ((0, rows - a.shape[0]), (0, cols - a.shape[1])))


def kernel(x, c, w_ada, b_ada, g_mix, w_in, w_dw, b_dw, g_conv_ln, b_conv_ln, g_q, g_k, w_out, g_ffn, w_gate, w_up, w_down, loss_target, m_w_ada, m_b_ada, m_g_mix, m_w_in, m_w_dw, m_b_dw, m_g_conv_ln, m_b_conv_ln, m_g_q, m_g_k, m_w_out, m_g_ffn, m_w_gate, m_w_up, m_w_down, v_w_ada, v_b_ada, v_g_mix, v_w_in, v_w_dw, v_b_dw, v_g_conv_ln, v_b_conv_ln, v_g_q, v_g_k, v_w_out, v_g_ffn, v_w_gate, v_w_up, v_w_down):
    w = dict(w_ada=w_ada, b_ada=b_ada, g_mix=g_mix, w_in=w_in, w_dw=w_dw, b_dw=b_dw, g_conv_ln=g_conv_ln,
             b_conv_ln=b_conv_ln, g_q=g_q, g_k=g_k, w_out=w_out, g_ffn=g_ffn, w_gate=w_gate, w_up=w_up, w_down=w_down)
    m = dict(w_ada=m_w_ada, b_ada=m_b_ada, g_mix=m_g_mix, w_in=m_w_in, w_dw=m_w_dw, b_dw=m_b_dw, g_conv_ln=m_g_conv_ln,
             b_conv_ln=m_b_conv_ln, g_q=m_g_q, g_k=m_g_k, w_out=m_w_out, g_ffn=m_g_ffn, w_gate=m_w_gate, w_up=m_w_up,
             w_down=m_w_down)
    v = dict(w_ada=v_w_ada, b_ada=v_b_ada, g_mix=v_g_mix, w_in=v_w_in, w_dw=v_w_dw, b_dw=v_b_dw, g_conv_ln=v_g_conv_ln,
             b_conv_ln=v_b_conv_ln, g_q=v_g_q, g_k=v_g_k, w_out=v_w_out, g_ffn=v_g_ffn, w_gate=v_w_gate, w_up=v_w_up,
             w_down=v_w_down)
    Bl, S, D = x.shape
    DC = g_conv_ln.shape[1]
    NA = w_ada.shape[2]
    xi, yi, ci = _mesh_pos()
    p = 2 * xi + yi
    dev = 2 * p + ci
    n_dev = 8
    pidx = jnp.reshape(p, (1,)).astype(jnp.int32)
    pc_idx = jnp.stack([p, ci]).astype(jnp.int32)

    first = jnp.concatenate([_pad_to(c, 8, D), _pad_to(w_dw[0], CONV_ROWS, D)], axis=0)
    shard = lambda a, nm: a[0].T if nm in TRANSPOSED else a[0]
    owned = {nm: _cast_weight(shard(w[nm], nm), pidx, "cast_" + nm) for nm in BIG}
    b_cols = lax.dynamic_slice_in_dim(b_ada, p * NA, NA, axis=1)
    g0, c_all, gm, w_in_full = _startup(first, w_ada[0], b_cols, owned["w_in"], Bl=Bl)
    g0 = g0.reshape(n_dev, 8 + CONV_ROWS, D)
    taps = jnp.concatenate([g0[2 * q, 8:, :w_dw.shape[2]] for q in range(4)], axis=1)
    wdw = jnp.where(lax.broadcasted_iota(jnp.int32, taps.shape, 0) == CONV_WIDTH, b_dw, taps)
    gm = gm.reshape(n_dev, n_dev * Bl, NA)
    mod = jnp.concatenate([lax.dynamic_slice_in_dim(gm[2 * q], dev * Bl, Bl, axis=0) for q in range(4)], axis=1)

    loc = _local_step(x, loss_target, mod, g_mix, wdw, g_conv_ln, b_conv_ln, g_q, g_k, g_ffn,
                      w_in_full, owned["w_out"], owned["w_gate"], owned["w_up"], owned["w_down"], pc_idx=pc_idx)

    halves = _final_add(loc["early_sums"], loc["early_recv"], pc_idx, "final_add_early")
    (late_sib,) = _rs_sibling([loc["grads"]["w_in"]], "rs_sibling_in")
    ((late32, late16),) = _pair_add([loc["grads"]["w_in"]], [late_sib], pc_idx, "pair_add_w_in")
    *early_full, late_recv = _rs_final(halves, "rs_final_early", chips=(late16,))
    grad = dict(zip(EARLY_WEIGHTS, early_full))
    grad["w_in"], = _rs_final(_final_add([late32], [late_recv], pc_idx, "final_add_w_in"), "rs_final_in")

    mod_rows, _, small_rows = _small_layout(Bl)
    gs = loc["gathered_small"]
    red, bada8 = _small_reduce(gs, n_dev, Bl)
    dmod_all = gs.reshape(n_dev, small_rows, D)[:, :mod_rows].reshape(n_dev * Bl, 8, D)[:, :N_MOD].reshape(n_dev * Bl, N_MOD * D)
    grad["w_ada"] = _ada_bwd(c_all, lax.dynamic_slice_in_dim(dmod_all, p * NA, NA, axis=1))
    grad["b_ada"] = bada8[:N_MOD].reshape(1, N_MOD * D)
    grad["g_mix"] = red[0:1]
    grad["g_ffn"] = red[1:2]
    grad["g_conv_ln"] = red[2:3, :DC]
    grad["b_conv_ln"] = red[2:3, DC:2 * DC]
    grad["g_q"] = red[3:4, :HEAD_DIM]
    grad["g_k"] = red[3:4, HEAD_DIM:2 * HEAD_DIM]
    loss = red[4, 0]
    dwdw = red[8:8 + CONV_ROWS, :DC]
    grad["w_dw"] = lax.dynamic_slice_in_dim(dwdw[:CONV_WIDTH], p * w_dw.shape[2], w_dw.shape[2], axis=1)
    grad["b_dw"] = dwdw[CONV_WIDTH:CONV_WIDTH + 1]

    delta, new_m, new_v = {}, {}, {}
    for nm in WEIGHTS:
        shp = w[nm].shape
        if nm in TRANSPOSED:
            d_, m_, v_ = _adamw(w[nm][0].T, grad[nm], m[nm][0].T, v[nm][0].T, "adamw_" + nm)
            grad[nm], delta[nm], new_m[nm], new_v[nm] = (a.T.reshape(shp) for a in (grad[nm], d_, m_, v_))
            continue
        two_d = (shp[-2], shp[-1]) if len(shp) == 3 else shp
        d_, m_, v_ = _adamw(w[nm].reshape(two_d), grad[nm].reshape(two_d), m[nm].reshape(two_d), v[nm].reshape(two_d),
                            "adamw_" + nm)
        grad[nm] = grad[nm].reshape(shp)
        delta[nm], new_m[nm], new_v[nm] = d_.reshape(shp), m_.reshape(shp), v_.reshape(shp)

    return (loss, loc["dx"], *[grad[nm] for nm in WEIGHTS], *[delta[nm] for nm in WEIGHTS],
            *[new_m[nm] for nm in WEIGHTS], *[new_v[nm] for nm in WEIGHTS])
```

```python
import functools

import jax
import jax.numpy as jnp
import numpy as np
from jax import lax
from jax.experimental import pallas as pl
from jax.experimental.pallas import tpu as pltpu

F32 = jnp.float32
MXU_DTYPE = jnp.bfloat16
ACT_DTYPE = jnp.bfloat16
EPS = 1e-6
NEG_INF = -1e30
HEAD_DIM = 64
LANES = 128
RADIUS = 64
QBLK = 128
DILATIONS = (1, 4, 16)
CONV_WIDTH = 31
CONV_PAD = CONV_WIDTH // 2
CONV_ROWS = 32
N_MOD = 6
ADAM_LR, ADAM_B1, ADAM_B2, ADAM_EPS, ADAM_WD, ADAM_STEP = 0.001, 0.9, 0.999, 1e-08, 0.01, 10
MESH_DEV = pl.DeviceIdType.MESH
VMEM_LIMIT = 56 << 20
ATTN_BWD_VMEM = 60 << 20


def _cp(sem=None, vmem=VMEM_LIMIT):
    kw = dict(vmem_limit_bytes=vmem)
    if sem is not None:
        kw["dimension_semantics"] = sem
    return pltpu.CompilerParams(**kw)


def _sigmoid(x):
    return 1.0 / (1.0 + jnp.exp(-x))


def _dot(a, b):
    return jnp.dot(a, b, preferred_element_type=F32)


def _dot_nt(a, b):
    return lax.dot_general(a, b, (((1,), (1,)), ((), ())), preferred_element_type=F32)


def _dot_tn(a, b):
    return lax.dot_general(a, b, (((0,), (0,)), ((), ())), preferred_element_type=F32)


def _colsum(v):
    return jnp.sum(v, axis=0, keepdims=True)


def _load_resident(i, pairs, sems):
    @pl.when(i == 0)
    def _():
        cps = [pltpu.make_async_copy(src, dst, sems.at[n]) for n, (src, dst) in enumerate(pairs)]
        for c in cps:
            c.start()
        for c in cps:
            c.wait()


def _fwd_in(x2, mod, g_mix, gq2, gk2, w_in, *, S, tm, n_ag):
    T, D = x2.shape
    P, _, Nb = w_in.shape
    n_in = P * Nb
    n_slab = (n_in - n_ag) // LANES
    NS = n_slab // 3
    tps = S // tm

    def body(x_ref, mod_ref, g_ref, gq_ref, gk_ref, w_ref, ag_ref, qkv_ref, qkh_ref, h_ref):
        x = x_ref[...]
        r = lax.rsqrt(jnp.mean(x * x, axis=-1, keepdims=True) + EPS)
        n = x * r * g_ref[...]
        h = n * (1.0 + mod_ref[:, D:2 * D]) + mod_ref[:, 0:D]
        hb = h.astype(MXU_DTYPE)
        h_ref[...] = hb
        parts = [_dot(hb, w_ref[p]) for p in range(P)]
        proj = jnp.concatenate(parts, axis=1) if P > 1 else parts[0]
        ag_ref[...] = proj[:, :n_ag]
        mm = _head_mean_matrix()
        for j in range(n_slab):
            v = proj[:, n_ag + LANES * j:n_ag + LANES * (j + 1)]
            qkv_ref[j] = v
            if j < 2 * NS:
                gain = gq_ref[...] * (HEAD_DIM ** -0.5 * LOG2E) if j < NS else gk_ref[...]
                qkh_ref[j] = v * lax.rsqrt(_head_mean(v * v, mm) + EPS) * gain

    return pl.pallas_call(
        body, grid=(T // tm,), name="fwd_in",
        in_specs=[pl.BlockSpec((tm, D), lambda i: (i, 0)),
                  pl.BlockSpec((None, 1, N_MOD * D), lambda i: (i // tps, 0, 0)),
                  pl.BlockSpec((1, D), lambda i: (0, 0)),
                  pl.BlockSpec((1, LANES), lambda i: (0, 0)), pl.BlockSpec((1, LANES), lambda i: (0, 0)),
                  pl.BlockSpec((P, D, Nb), lambda i: (0, 0, 0))],
        out_specs=[pl.BlockSpec((tm, n_ag), lambda i: (i, 0)),
                   pl.BlockSpec((n_slab, tm, LANES), lambda i: (0, i, 0)),
                   pl.BlockSpec((2 * NS, tm, LANES), lambda i: (0, i, 0)),
                   pl.BlockSpec((tm, D), lambda i: (i, 0))],
        out_shape=[jax.ShapeDtypeStruct((T, n_ag), F32),
                   jax.ShapeDtypeStruct((n_slab, T, LANES), F32),
                   jax.ShapeDtypeStruct((2 * NS, T, LANES), F32),
                   jax.ShapeDtypeStruct((T, D), MXU_DTYPE)],
        compiler_params=_cp(("arbitrary",)),
    )(x2, mod, g_mix, gq2, gk2, w_in)


CONV_CH = 128


def _conv_taps(win, w_ref, acc, reverse):
    n = win.shape[0]
    for b in range(8):
        wb = win if b == 0 else pltpu.roll(win, shift=n - b, axis=0)
        for a in range(4):
            o = 8 * a + b
            if o < 1 or o > CONV_WIDTH:
                continue
            k = (CONV_WIDTH - o) if reverse else (o - 1)
            acc = acc + w_ref[k:k + 1, :] * wb[8 * a:8 * a + CONV_CH, :]
    return acc


def _conv_fwd(ag, wdw, *, Bl, S, DC):
    T = ag.shape[0]
    nsc = DC // LANES
    CH = CONV_CH

    def body(a_ref, g_ref, w_ref, cv_ref, upad):
        zeros16 = jnp.zeros((16, LANES), F32)
        upad[0:16, :] = zeros16
        upad[S + 16:S + 32, :] = zeros16

        def fill(i, _):
            r0 = pl.multiple_of(i * CH, CH)
            a = a_ref[pl.ds(r0, CH), :]
            g = g_ref[pl.ds(r0, CH), :]
            upad[pl.ds(r0 + 16, CH), :] = a * _sigmoid(g)
            return 0
        lax.fori_loop(0, S // CH, fill, 0)

        def conv(i, _):
            r0 = pl.multiple_of(i * CH, CH)
            win = upad[pl.ds(r0, CH + 32), :]
            acc = jnp.zeros((CH, LANES), F32) + w_ref[CONV_WIDTH:CONV_WIDTH + 1, :]
            cv_ref[pl.ds(r0, CH), :] = _conv_taps(win, w_ref, acc, reverse=False)
            return 0
        lax.fori_loop(0, S // CH, conv, 0)

    return pl.pallas_call(
        body, grid=(Bl, nsc), name="conv_fwd",
        in_specs=[pl.BlockSpec((S, LANES), lambda b, j: (b, j)),
                  pl.BlockSpec((S, LANES), lambda b, j: (b, nsc + j)),
                  pl.BlockSpec((CONV_ROWS, LANES), lambda b, j: (0, j))],
        out_specs=pl.BlockSpec((S, LANES), lambda b, j: (b, j)),
        out_shape=jax.ShapeDtypeStruct((T, DC), F32),
        scratch_shapes=[pltpu.VMEM((S + 32, LANES), F32)],
        compiler_params=_cp(("arbitrary", "arbitrary")),
    )(ag, ag, wdw)


def _conv_bwd(ag, dcv, wdw, *, Bl, S, DC):
    T = ag.shape[0]
    nsc = DC // LANES
    CH = CONV_CH

    def body(a_ref, g_ref, d_ref, w_ref, da_ref, dg_ref, dw_ref, upad, dpad, wacc):
        b = pl.program_id(1)
        zeros16 = jnp.zeros((16, LANES), F32)
        upad[0:16, :] = zeros16
        upad[S + 16:S + 32, :] = zeros16
        dpad[0:16, :] = zeros16
        dpad[S + 16:S + 32, :] = zeros16

        @pl.when(b == 0)
        def _():
            wacc[...] = jnp.zeros_like(wacc)

        def fill(i, _):
            r0 = pl.multiple_of(i * CH, CH)
            a = a_ref[pl.ds(r0, CH), :]
            g = g_ref[pl.ds(r0, CH), :]
            upad[pl.ds(r0 + 16, CH), :] = a * _sigmoid(g)
            dpad[pl.ds(r0 + 16, CH), :] = d_ref[pl.ds(r0, CH), :]
            return 0
        lax.fori_loop(0, S // CH, fill, 0)

        def step(i, _):
            r0 = pl.multiple_of(i * CH, CH)
            dwin = dpad[pl.ds(r0, CH + 32), :]
            du = _conv_taps(dwin, w_ref, jnp.zeros((CH, LANES), F32), reverse=True)
            a = a_ref[pl.ds(r0, CH), :]
            g = g_ref[pl.ds(r0, CH), :]
            sg = _sigmoid(g)
            da_ref[pl.ds(r0, CH), :] = du * sg
            dg_ref[pl.ds(r0, CH), :] = du * a * sg * (1.0 - sg)
            dc = d_ref[pl.ds(r0, CH), :]
            uwin = upad[pl.ds(r0, CH + 32), :]
            n = CH + 32
            for bb in range(8):
                wb = uwin if bb == 0 else pltpu.roll(uwin, shift=n - bb, axis=0)
                for aa in range(4):
                    o = 8 * aa + bb
                    if o < 1 or o > CONV_WIDTH:
                        continue
                    k = o - 1
                    prod = dc * wb[8 * aa:8 * aa + CH, :]
                    part = prod[0:8, :]
                    for q in range(1, CH // 8):
                        part = part + prod[8 * q:8 * q + 8, :]
                    wacc[8 * k:8 * k + 8, :] += part
            part = dc[0:8, :]
            for q in range(1, CH // 8):
                part = part + dc[8 * q:8 * q + 8, :]
            wacc[8 * CONV_WIDTH:8 * CONV_WIDTH + 8, :] += part
            return 0
        lax.fori_loop(0, S // CH, step, 0)

        @pl.when(b == Bl - 1)
        def _():
            for k in range(CONV_ROWS):
                dw_ref[k:k + 1, :] = jnp.sum(wacc[8 * k:8 * k + 8, :], axis=0, keepdims=True)

    return pl.pallas_call(
        body, grid=(nsc, Bl), name="conv_bwd",
        in_specs=[pl.BlockSpec((S, LANES), lambda j, b: (b, j)),
                  pl.BlockSpec((S, LANES), lambda j, b: (b, nsc + j)),
                  pl.BlockSpec((S, LANES), lambda j, b: (b, j)),
                  pl.BlockSpec((CONV_ROWS, LANES), lambda j, b: (0, j))],
        out_specs=[pl.BlockSpec((S, LANES), lambda j, b: (b, j)),
                   pl.BlockSpec((S, LANES), lambda j, b: (b, j)),
                   pl.BlockSpec((CONV_ROWS, LANES), lambda j, b: (0, j))],
        out_shape=[jax.ShapeDtypeStruct((T, DC), F32), jax.ShapeDtypeStruct((T, DC), F32),
                   jax.ShapeDtypeStruct((CONV_ROWS, DC), F32)],
        scratch_shapes=[pltpu.VMEM((S + 32, LANES), F32), pltpu.VMEM((S + 32, LANES), F32),
                        pltpu.VMEM((8 * CONV_ROWS, LANES), F32)],
        compiler_params=_cp(("arbitrary", "arbitrary")),
    )(ag, ag, dcv, wdw)


ROWCH = 256


LOG2E = 1.4426950408889634
LN2 = 0.6931471805599453
N_EDGE = 4


def _head_mean_matrix():
    r = lax.broadcasted_iota(jnp.int32, (LANES, LANES), 0) // HEAD_DIM
    c = lax.broadcasted_iota(jnp.int32, (LANES, LANES), 1) // HEAD_DIM
    return jnp.where(r == c, 1.0 / HEAD_DIM, 0.0).astype(jnp.bfloat16)


def _head_mean(v, mm):
    hi = v.astype(jnp.bfloat16)
    lo = (v - hi.astype(F32)).astype(jnp.bfloat16)
    return _dot(hi, mm) + _dot(lo, mm)


def _stack_heads(blk, lane_lo):
    z = jnp.zeros_like(blk)
    return jnp.concatenate([jnp.where(lane_lo, blk, z), jnp.where(lane_lo, z, blk)], axis=0)


def _merge_heads(v2, lane_lo):
    return jnp.where(lane_lo, v2[:QBLK], v2[QBLK:])


def _bias_tables(bias_ref, slope_ref):
    row = lax.broadcasted_iota(jnp.int32, (2 * QBLK, 2 * QBLK), 0)
    col = lax.broadcasted_iota(jnp.int32, (2 * QBLK, 2 * QBLK), 1)
    rel = jnp.abs(col - RADIUS - (row % QBLK))
    slope = jnp.where(row < QBLK, slope_ref[0:1, 0:1], slope_ref[0:1, HEAD_DIM:HEAD_DIM + 1]) * LOG2E
    for pi, d in enumerate(DILATIONS):
        inside = jnp.where(rel <= RADIUS, -slope * (float(d) * rel.astype(F32)), NEG_INF)
        for e in range(N_EDGE):
            t = inside
            if e & 1:
                t = jnp.where(col < RADIUS, NEG_INF, t)
            if e & 2:
                t = jnp.where(col >= QBLK + RADIUS, NEG_INF, t)
            bias_ref[N_EDGE * pi + e] = t


def _edge_index(qb, nb):
    return jnp.where(qb == 0, 1, 0) + jnp.where(qb == nb - 1, 2, 0)


VIA = 4


def _residue(d, s):
    return (s % VIA) * VIA + s // VIA if d == VIA * VIA else s


def _gather_rows(src_ref, dst_ref, S, d, pad, f32_copy=None):
    n = S // d
    seg = n + 2 * RADIUS if pad else n
    step = min(n, 512)
    two_step = d == VIA * VIA and f32_copy is not None
    for s in range(d):
        base = s * seg
        if pad:
            dst_ref[base:base + RADIUS, :] = jnp.zeros((RADIUS, LANES), dst_ref.dtype)
            dst_ref[base + RADIUS + n:base + seg, :] = jnp.zeros((RADIUS, LANES), dst_ref.dtype)
            base += RADIUS
        for c0 in range(0, n, step):
            if d == 1:
                v = src_ref[c0:c0 + step, :]
            elif two_step:
                v = f32_copy[pl.ds((s // VIA) * (S // VIA) + s % VIA + c0 * VIA, step, stride=VIA), :]
            else:
                v = src_ref[pl.ds(_residue(d, s) + c0 * d, step, stride=d), :]
                if d == VIA and f32_copy is not None:
                    f32_copy[s * n + c0:s * n + c0 + step, :] = v
            dst_ref[base + c0:base + c0 + step, :] = v.astype(dst_ref.dtype)


def _scatter_rows(src_ref, dst_ref, S, d, pad, accumulate, f32_tmp=None):
    n = S // d
    seg = n + 2 * RADIUS if pad else n
    first = RADIUS if pad else 0
    _unpermute(lambda s, c0, step: src_ref[s * seg + first + c0:s * seg + first + c0 + step, :],
               dst_ref, S, d, accumulate, f32_tmp)


def _unpermute(rows_of, dst_ref, S, d, accumulate, f32_tmp):
    n = S // d
    step = min(n, 512)
    if d == VIA * VIA and f32_tmp is not None:
        for s in range(d):
            f32_tmp[pl.ds((s // VIA) * (S // VIA) + s % VIA, n, stride=VIA), :] = rows_of(s, 0, n)
        n4 = S // VIA
        _unpermute(lambda s, c0, st: f32_tmp[s * n4 + c0:s * n4 + c0 + st, :], dst_ref, S, VIA, accumulate, None)
        return
    for s in range(d):
        for c0 in range(0, n, step):
            v = rows_of(s, c0, step)
            idx = pl.ds(c0, step) if d == 1 else pl.ds(_residue(d, s) + c0 * d, step, stride=d)
            if accumulate:
                dst_ref[idx, :] = dst_ref[idx, :] + v
            else:
                dst_ref[idx, :] = v


def _zero_uncovered(acc, S, d):
    n = S // d
    if (n // QBLK) % 2:
        return
    seg = n + 2 * RADIUS
    for r in range(d):
        acc[0, r * seg + n:r * seg + seg, :] = jnp.zeros((2 * RADIUS, LANES), F32)
        acc[1, r * seg:r * seg + 2 * RADIUS, :] = jnp.zeros((2 * RADIUS, LANES), F32)


def _scatter_parity(acc, dst_ref, S, d, f32_tmp=None):
    n = S // d
    seg = n + 2 * RADIUS
    one_block = (n // QBLK) % 2 == 1

    def rows_of(s, c0, step):
        rows = slice(s * seg + RADIUS + c0, s * seg + RADIUS + c0 + step)
        return acc[s % 2, rows, :] if one_block else acc[0, rows, :] + acc[1, rows, :]

    _unpermute(rows_of, dst_ref, S, d, True, f32_tmp)


PIPE_UNROLL = 4
PIPE_SLOTS = 16
BWD_SLOTS = 12


def _pipeline(n_items, stages, unroll):
    K = len(stages)
    assert n_items % unroll == 0 and K * unroll <= (PIPE_SLOTS if K == 4 else BWD_SLOTS)
    trips = n_items // unroll
    assert trips >= K - 1

    def trip(t, static):
        for s in reversed(range(K)):
            if static and not 0 <= t - s < trips:
                continue
            for u in range(unroll):
                item = unroll * (t - s) + u
                stages[s](jnp.int32(item) if static else item)

    for t in range(K - 1):
        trip(t, True)

    def full(t, carry):
        trip(t, False)
        return carry
    lax.fori_loop(K - 1, trips, full, 0)
    for t in range(trips, trips + K - 1):
        trip(t, True)


def _attn_fwd(qkh, qkv, slopes, *, Bl, S, hosted=()):
    n3, T, _ = qkv.shape
    NS = n3 // 3
    NB = S // QBLK
    PADR = S + 2 * RADIUS * DILATIONS[-1]
    nh = len(hosted)
    plan = _WeightGather([b.shape for b in hosted]) if nh else None
    n_steps = Bl * NS

    def body(qh, kh, v_ref, slope_ref, *rest):
        o_ref, lse_ref = rest[nh:nh + 2]
        wouts = rest[nh + 2:2 * nh + 2]
        (qp, kp, vp, op, lp, onat, lnat, bias_ref, sbuf, pbuf, mbuf, lbuf, tmps) = rest[2 * nh + 2:2 * nh + 15]
        sems = rest[2 * nh + 15:]
        step = pl.program_id(0) * Bl + pl.program_id(1)
        if nh:
            @pl.when(step == 0)
            def _():
                plan.start(wouts, sems)

            @pl.when(step == (7 * n_steps) // 8)
            def _():
                plan.forward(wouts, sems)

        lane_lo = lax.broadcasted_iota(jnp.int32, (QBLK, LANES), 1) < HEAD_DIM

        @pl.when(pl.program_id(1) == 0)
        def _():
            _bias_tables(bias_ref, slope_ref)

        for pi, d in enumerate(DILATIONS):
            n = S // d
            nb = n // QBLK
            _gather_rows(qh, qp, S, d, pad=False, f32_copy=tmps.at[0])
            _gather_rows(kh, kp, S, d, pad=True, f32_copy=tmps.at[1])
            _gather_rows(v_ref, vp, S, d, pad=True, f32_copy=tmps.at[2])

            def offsets(i, nb=nb):
                r = i // nb
                return pl.multiple_of(i * QBLK, QBLK), pl.multiple_of((i + r) * QBLK, QBLK), i % nb

            def scores(i, pi=pi, nb=nb):
                q0, k0, qb = offsets(i)
                qs = _stack_heads(qp[pl.ds(q0, QBLK), :], lane_lo)
                sbuf[i % PIPE_SLOTS] = (_dot_nt(qs, kp[pl.ds(k0, 2 * QBLK), :])
                                        + bias_ref[N_EDGE * pi + _edge_index(qb, nb)])

            def rowmax(i):
                m = jnp.max(sbuf[i % PIPE_SLOTS], axis=1, keepdims=True)
                mbuf[i % PIPE_SLOTS] = jnp.broadcast_to(m, (2 * QBLK, LANES))

            def expsum(i):
                m = mbuf[i % PIPE_SLOTS]
                p = jnp.exp2(sbuf[i % PIPE_SLOTS] - jnp.concatenate([m, m], axis=1))
                pbuf[i % PIPE_SLOTS] = p.astype(MXU_DTYPE)
                lbuf[i % PIPE_SLOTS] = jnp.broadcast_to(jnp.sum(p, axis=1, keepdims=True), (2 * QBLK, LANES))

            def values(i):
                q0, k0, _ = offsets(i)
                l = lbuf[i % PIPE_SLOTS]
                o2 = _dot(pbuf[i % PIPE_SLOTS], vp[pl.ds(k0, 2 * QBLK), :]) * (1.0 / l)
                op[pl.ds(q0, QBLK), :] = _merge_heads(o2, lane_lo)
                lp[pl.ds(q0, QBLK), :] = _merge_heads(mbuf[i % PIPE_SLOTS] + jnp.log2(l), lane_lo)

            _pipeline(NB, [scores, rowmax, expsum, values], PIPE_UNROLL)
            _scatter_rows(op, onat.at[pi], S, d, pad=False, accumulate=False, f32_tmp=tmps.at[0])
            _scatter_rows(lp, lnat.at[pi], S, d, pad=False, accumulate=False, f32_tmp=tmps.at[1])

        for c0 in range(0, S, ROWCH):
            ls = [lnat[pi, c0:c0 + ROWCH, :] for pi in range(len(DILATIONS))]
            mx = jnp.maximum(jnp.maximum(ls[0], ls[1]), ls[2])
            es = [jnp.exp2(l - mx) for l in ls]
            tot = es[0] + es[1] + es[2]
            inv = 1.0 / tot
            acc = (es[0] * inv) * onat[0, c0:c0 + ROWCH, :]
            for pi in (1, 2):
                acc = acc + (es[pi] * inv) * onat[pi, c0:c0 + ROWCH, :]
            o_ref[c0:c0 + ROWCH, :] = acc
            lse_ref[c0:c0 + ROWCH, :] = mx + jnp.log2(tot)

        if nh:
            @pl.when(step == n_steps - 1)
            def _():
                plan.finish(wouts, sems)

    spec_in = lambda off: pl.BlockSpec((None, S, LANES), lambda j, b: (off * NS + j, b, 0))
    out = pl.BlockSpec((S, LANES), lambda j, b: (b, j))
    anyspec = pl.BlockSpec(memory_space=pl.ANY)
    return pl.pallas_call(
        body, grid=(NS, Bl), name="attn_fwd",
        in_specs=[spec_in(0), spec_in(1), spec_in(2),
                  pl.BlockSpec((None, 8, LANES), lambda j, b: (j, 0, 0))] + [anyspec] * nh,
        out_specs=[out, out] + [anyspec] * nh,
        out_shape=[jax.ShapeDtypeStruct((T, NS * LANES), F32)] * 2
                  + [jax.ShapeDtypeStruct(b.shape, b.dtype) for b in hosted],
        input_output_aliases={4 + w: 2 + w for w in range(nh)},
        scratch_shapes=[pltpu.VMEM((S, LANES), MXU_DTYPE), pltpu.VMEM((PADR, LANES), MXU_DTYPE),
                        pltpu.VMEM((PADR, LANES), MXU_DTYPE),
                        pltpu.VMEM((S, LANES), F32), pltpu.VMEM((S, LANES), F32),
                        pltpu.VMEM((3, S, LANES), F32), pltpu.VMEM((3, S, LANES), F32),
                        pltpu.VMEM((N_EDGE * len(DILATIONS), 2 * QBLK, 2 * QBLK), F32),
                        pltpu.VMEM((PIPE_SLOTS, 2 * QBLK, 2 * QBLK), F32),
                        pltpu.VMEM((PIPE_SLOTS, 2 * QBLK, 2 * QBLK), MXU_DTYPE),
                        pltpu.VMEM((PIPE_SLOTS, 2 * QBLK, LANES), F32), pltpu.VMEM((PIPE_SLOTS, 2 * QBLK, LANES), F32),
                        pltpu.VMEM((3, S, LANES), F32)]
                       + (plan.scratch() if nh else []),
        compiler_params=_cp(("arbitrary", "arbitrary")),
    )(qkh, qkh, qkv, slopes, *hosted)


def _attn_bwd(qkh, qkv, o, lse, do, gq2, gk2, slopes, *, Bl, S, hosted=()):
    n3, T, _ = qkv.shape
    NS = n3 // 3
    NB = S // QBLK
    PADR = S + 2 * RADIUS * DILATIONS[-1]
    QSCALE = HEAD_DIM ** -0.5
    nh = len(hosted)
    plan = _ChipExchange(nh)
    n_steps = Bl * NS

    def body(qh, kh, q_ref, k_ref, v_ref, o_ref, lse_ref, do_ref, gq_ref, gk_ref, slope_ref, *rest):
        hin = rest[:nh]
        dq_ref, dk_ref, dv_ref, gacc_ref = rest[nh:nh + 4]
        hout = rest[nh + 4:2 * nh + 4]
        (ld, qp, kp, vp, dop, ldp, dqp, dkacc, dvacc, dqn, dkn, bias_ref,
         sbuf, dpbuf, pbuf, dsbuf, tmps) = rest[2 * nh + 4:2 * nh + 21]
        sems = rest[2 * nh + 21:]
        step = pl.program_id(0) * Bl + pl.program_id(1)

        @pl.when(step == 0)
        def _():
            gacc_ref[...] = jnp.zeros_like(gacc_ref)
            if nh:
                plan.start(hin, hout, sems)

        mm = _head_mean_matrix()
        lane_lo = lax.broadcasted_iota(jnp.int32, (QBLK, LANES), 1) < HEAD_DIM

        @pl.when(pl.program_id(1) == 0)
        def _():
            _bias_tables(bias_ref, slope_ref)

        lse_lanes = lax.broadcasted_iota(jnp.int32, (ROWCH, LANES), 1) % HEAD_DIM < HEAD_DIM // 2
        for c0 in range(0, S, ROWCH):
            delta = _head_mean(do_ref[c0:c0 + ROWCH, :] * o_ref[c0:c0 + ROWCH, :], mm) * HEAD_DIM
            ld[c0:c0 + ROWCH, :] = jnp.where(lse_lanes, lse_ref[c0:c0 + ROWCH, :], delta)
            dqn[c0:c0 + ROWCH, :] = jnp.zeros((ROWCH, LANES), F32)
            dkn[c0:c0 + ROWCH, :] = jnp.zeros((ROWCH, LANES), F32)
            dv_ref[c0:c0 + ROWCH, :] = jnp.zeros((ROWCH, LANES), F32)

        for pi, d in enumerate(DILATIONS):
            n = S // d
            nb = n // QBLK
            _gather_rows(qh, qp, S, d, pad=False, f32_copy=tmps.at[0])
            _gather_rows(kh, kp, S, d, pad=True, f32_copy=tmps.at[1])
            _gather_rows(v_ref, vp, S, d, pad=True, f32_copy=tmps.at[2])
            _gather_rows(do_ref, dop, S, d, pad=False, f32_copy=tmps.at[3])
            _gather_rows(ld, ldp, S, d, pad=False, f32_copy=tmps.at[4])
            _zero_uncovered(dkacc, S, d)
            _zero_uncovered(dvacc, S, d)

            def offsets(i, nb=nb):
                r = i // nb
                return pl.multiple_of(i * QBLK, QBLK), pl.multiple_of((i + r) * QBLK, QBLK), i % nb

            def scores(i, pi=pi, nb=nb):
                q0, k0, qb = offsets(i)
                qs = _stack_heads(qp[pl.ds(q0, QBLK), :], lane_lo)
                dos = _stack_heads(dop[pl.ds(q0, QBLK), :], lane_lo)
                sbuf[i % BWD_SLOTS] = (_dot_nt(qs, kp[pl.ds(k0, 2 * QBLK), :])
                                       + bias_ref[N_EDGE * pi + _edge_index(qb, nb)])
                dpbuf[i % BWD_SLOTS] = _dot_nt(dos, vp[pl.ds(k0, 2 * QBLK), :])

            def probs(i):
                q0, _, _ = offsets(i)
                blk = ldp[pl.ds(q0, QBLK), :]
                half = HEAD_DIM // 2
                lcol = jnp.concatenate([blk[:, 0:1], blk[:, HEAD_DIM:HEAD_DIM + 1]], axis=0)
                dcol = jnp.concatenate([blk[:, half:half + 1], blk[:, HEAD_DIM + half:HEAD_DIM + half + 1]], axis=0)
                p = jnp.exp2(sbuf[i % BWD_SLOTS] - lcol)
                pbuf[i % BWD_SLOTS] = p.astype(MXU_DTYPE)
                dsbuf[i % BWD_SLOTS] = (p * (dpbuf[i % BWD_SLOTS] - dcol)).astype(MXU_DTYPE)

            def grads(i):
                q0, k0, _ = offsets(i)
                qs = _stack_heads(qp[pl.ds(q0, QBLK), :], lane_lo)
                dos = _stack_heads(dop[pl.ds(q0, QBLK), :], lane_lo)
                ds = dsbuf[i % BWD_SLOTS]
                dvacc[i % 2, pl.ds(k0, 2 * QBLK), :] = _dot_tn(pbuf[i % BWD_SLOTS], dos)
                dkacc[i % 2, pl.ds(k0, 2 * QBLK), :] = _dot_tn(ds, qs)
                dqp[pl.ds(q0, QBLK), :] = _merge_heads(_dot(ds, kp[pl.ds(k0, 2 * QBLK), :]), lane_lo)

            _pipeline(NB, [scores, probs, grads], PIPE_UNROLL)
            _scatter_rows(dqp, dqn, S, d, pad=False, accumulate=True, f32_tmp=tmps.at[0])
            _scatter_parity(dkacc, dkn, S, d, f32_tmp=tmps.at[1])
            _scatter_parity(dvacc, dv_ref, S, d, f32_tmp=tmps.at[2])

        gq_sum = jnp.zeros((8, LANES), F32)
        gk_sum = jnp.zeros((8, LANES), F32)
        for c0 in range(0, S, ROWCH):
            for src_ref, dn, g_ref, dst_ref, scale, is_q in ((q_ref, dqn, gq_ref, dq_ref, QSCALE, True),
                                                             (k_ref, dkn, gk_ref, dk_ref, LN2, False)):
                x = src_ref[c0:c0 + ROWCH, :]
                dh = dn[c0:c0 + ROWCH, :]
                rr = lax.rsqrt(_head_mean(x * x, mm) + EPS)
                e = dh * (g_ref[...] * scale)
                dst_ref[c0:c0 + ROWCH, :] = rr * e - x * (rr * rr * rr) * _head_mean(e * x, mm)
                gpart = dh * (x * rr * scale)
                acc8 = gpart[0:8, :]
                for q8 in range(1, ROWCH // 8):
                    acc8 = acc8 + gpart[8 * q8:8 * q8 + 8, :]
                if is_q:
                    gq_sum = gq_sum + acc8
                else:
                    gk_sum = gk_sum + acc8
        gacc_ref[0:1, :] += jnp.sum(gq_sum, axis=0, keepdims=True)
        gacc_ref[1:2, :] += jnp.sum(gk_sum, axis=0, keepdims=True)

        if nh:
            @pl.when(step == n_steps - 1)
            def _():
                plan.finish(hin, hout, sems)

    spec_in = lambda off: pl.BlockSpec((None, S, LANES), lambda j, b: (off * NS + j, b, 0))
    tok = pl.BlockSpec((S, LANES), lambda j, b: (b, j))
    vec = pl.BlockSpec((1, LANES), lambda j, b: (0, 0))
    slab_out = pl.BlockSpec((None, S, LANES), lambda j, b: (j, b, 0))
    f32buf = lambda rows: pltpu.VMEM((rows, LANES), F32)
    bfbuf = lambda rows: pltpu.VMEM((rows, LANES), MXU_DTYPE)
    anyspec = pl.BlockSpec(memory_space=pl.ANY)
    return pl.pallas_call(
        body, grid=(NS, Bl), name="attn_bwd",
        in_specs=[spec_in(0), spec_in(1), spec_in(0), spec_in(1), spec_in(2), tok, tok, tok, vec, vec,
                  pl.BlockSpec((None, 8, LANES), lambda j, b: (j, 0, 0))] + [anyspec] * nh,
        out_specs=[slab_out, slab_out, slab_out, pl.BlockSpec((8, LANES), lambda j, b: (0, 0))] + [anyspec] * nh,
        out_shape=[jax.ShapeDtypeStruct((NS, T, LANES), F32)] * 3 + [jax.ShapeDtypeStruct((8, LANES), F32)]
                  + [jax.ShapeDtypeStruct((3,) + h.shape[1:], h.dtype) for h in hosted],
        scratch_shapes=[f32buf(S),
                        bfbuf(S), bfbuf(PADR), bfbuf(PADR), bfbuf(S),
                        f32buf(S), f32buf(S),
                        pltpu.VMEM((2, PADR, LANES), F32), pltpu.VMEM((2, PADR, LANES), F32),
                        f32buf(S), f32buf(S),
                        pltpu.VMEM((N_EDGE * len(DILATIONS), 2 * QBLK, 2 * QBLK), F32),
                        pltpu.VMEM((BWD_SLOTS, 2 * QBLK, 2 * QBLK), F32),
                        pltpu.VMEM((BWD_SLOTS, 2 * QBLK, 2 * QBLK), F32),
                        pltpu.VMEM((BWD_SLOTS, 2 * QBLK, 2 * QBLK), MXU_DTYPE),
                        pltpu.VMEM((BWD_SLOTS, 2 * QBLK, 2 * QBLK), MXU_DTYPE),
                        pltpu.VMEM((5, S, LANES), F32)]
                       + (plan.scratch() if nh else []),
        compiler_params=_cp(("arbitrary", "arbitrary"), vmem=ATTN_BWD_VMEM),
    )(qkh, qkh, qkv, qkv, qkv, o, lse, do, gq2, gk2, slopes, *hosted)


def _layer_norm_parts(cv, g_ln, b_ln):
    mu = jnp.mean(cv, axis=-1, keepdims=True)
    cen = cv - mu
    rs = lax.rsqrt(jnp.mean(cen * cen, axis=-1, keepdims=True) + EPS)
    z = cen * rs
    return z, rs, z * g_ln + b_ln


def _ffn_fwd(x2, cv, ya, tgt, mod, g_ln, b_ln, g_ffn, w_out, w_gate, w_up, w_down, *, S, tm):
    T, D = x2.shape
    DC = cv.shape[1]
    P, Kb, _ = w_out.shape
    Fb = w_down.shape[1]
    tps = S // tm

    def body(x_ref, cv_ref, ya_ref, t_ref, mod_ref, gln_ref, bln_ref, gf_ref, wo_hbm, wg_hbm, wu_hbm, wd_hbm,
             x1_ref, ycat_ref, mix_ref, h2_ref, g_ref, u_ref, a_ref, f_ref, dy_ref, loss_ref,
             wo, wg, wu, wd, sems):
        i = pl.program_id(0)
        _load_resident(i, [(wo_hbm, wo), (wg_hbm, wg), (wu_hbm, wu), (wd_hbm, wd)], sems)

        @pl.when(i == 0)
        def _():
            loss_ref[...] = jnp.zeros_like(loss_ref)

        _, _, ln = _layer_norm_parts(cv_ref[...], gln_ref[...], bln_ref[...])
        yc = ln * _sigmoid(ln)
        ycat = jnp.concatenate([yc, ya_ref[...]], axis=1).astype(MXU_DTYPE)
        ycat_ref[...] = ycat
        mix = _dot(ycat[:, 0:Kb], wo[0])
        for p in range(1, P):
            mix = mix + _dot(ycat[:, Kb * p:Kb * (p + 1)], wo[p])
        mix_ref[...] = mix.astype(ACT_DTYPE)
        x1 = x_ref[...] + mod_ref[:, 2 * D:3 * D] * mix
        x1_ref[...] = x1
        r2 = lax.rsqrt(jnp.mean(x1 * x1, axis=-1, keepdims=True) + EPS)
        h2 = (x1 * r2 * gf_ref[...]) * (1.0 + mod_ref[:, 4 * D:5 * D]) + mod_ref[:, 3 * D:4 * D]
        h2b = h2.astype(MXU_DTYPE)
        h2_ref[...] = h2b
        f = jnp.zeros((tm, D), F32)
        for p in range(P):
            g = _dot_nt(h2b, wg[p])
            u = _dot_nt(h2b, wu[p])
            a = (g * _sigmoid(g) * u).astype(MXU_DTYPE)
            g_ref[p] = g.astype(ACT_DTYPE)
            u_ref[p] = u.astype(ACT_DTYPE)
            a_ref[p] = a
            f = f + _dot(a, wd[p])
        f_ref[...] = f.astype(ACT_DTYPE)
        err = x1 + mod_ref[:, 5 * D:6 * D] * f - t_ref[...]
        dy_ref[...] = err * (1.0 / D)
        tot = jnp.sum(_colsum(err * err), axis=1, keepdims=True)
        loss_ref[...] += tot * (0.5 / D)

    row = lambda w: pl.BlockSpec((tm, w), lambda i: (i, 0))
    vec = lambda w: pl.BlockSpec((1, w), lambda i: (0, 0))
    blk = pl.BlockSpec((P, tm, Fb), lambda i: (0, i, 0))
    anyspec = pl.BlockSpec(memory_space=pl.ANY)
    return pl.pallas_call(
        body, grid=(T // tm,), name="ffn_fwd",
        in_specs=[row(D), row(DC), row(D - DC), row(D),
                  pl.BlockSpec((None, 1, N_MOD * D), lambda i: (i // tps, 0, 0)),
                  vec(DC), vec(DC), vec(D), anyspec, anyspec, anyspec, anyspec],
        out_specs=[row(D), row(D), row(D), row(D), blk, blk, blk, row(D), row(D),
                   pl.BlockSpec((8, LANES), lambda i: (0, 0))],
        out_shape=[jax.ShapeDtypeStruct((T, D), F32), jax.ShapeDtypeStruct((T, D), MXU_DTYPE),
                   jax.ShapeDtypeStruct((T, D), ACT_DTYPE), jax.ShapeDtypeStruct((T, D), MXU_DTYPE),
                   jax.ShapeDtypeStruct((P, T, Fb), ACT_DTYPE), jax.ShapeDtypeStruct((P, T, Fb), ACT_DTYPE),
                   jax.ShapeDtypeStruct((P, T, Fb), MXU_DTYPE), jax.ShapeDtypeStruct((T, D), ACT_DTYPE),
                   jax.ShapeDtypeStruct((T, D), F32), jax.ShapeDtypeStruct((8, LANES), F32)],
        scratch_shapes=[pltpu.VMEM(w_out.shape, w_out.dtype), pltpu.VMEM(w_gate.shape, w_gate.dtype),
                        pltpu.VMEM(w_up.shape, w_up.dtype), pltpu.VMEM(w_down.shape, w_down.dtype),
                        pltpu.SemaphoreType.DMA((4,))],
        compiler_params=_cp(("arbitrary",)),
    )(x2, cv, ya, tgt, mod, g_ln, b_ln, g_ffn, w_out, w_gate, w_up, w_down)


def _ffn_bwd(dy, x1, gs, us, fo, mixb, cv, mod, g_ln, b_ln, g_ffn, w_out, w_gate, w_up, w_down, *, S, tm):
    T, D = dy.shape
    DC = cv.shape[1]
    P, Kb, _ = w_out.shape
    Fb = w_down.shape[1]
    tps = S // tm
    Bl = T // S

    def body(dy_ref, x1_ref, g_ref, u_ref, f_ref, mix_ref, cv_ref, mod_ref, gln_ref, bln_ref, gf_ref,
             wo_hbm, wg_hbm, wu_hbm, wd_hbm,
             dg_ref, du_ref, df_ref, dx1_ref, dmix_ref, dya_ref, dcv_ref, macc_ref, gacc_ref, lacc_ref,
             wo, wg, wu, wd, sems):
        i = pl.program_id(0)
        _load_resident(i, [(wo_hbm, wo), (wg_hbm, wg), (wu_hbm, wu), (wd_hbm, wd)], sems)

        @pl.when(i == 0)
        def _():
            gacc_ref[...] = jnp.zeros_like(gacc_ref)
            lacc_ref[...] = jnp.zeros_like(lacc_ref)

        @pl.when(i % tps == 0)
        def _():
            macc_ref[...] = jnp.zeros_like(macc_ref)

        dy_t = dy_ref[...]
        x1 = x1_ref[...]
        gate_f = mod_ref[:, 5 * D:6 * D]
        macc_ref[2:3, :] += _colsum(dy_t * f_ref[...].astype(F32))
        dfb = (dy_t * gate_f).astype(MXU_DTYPE)
        df_ref[...] = dfb
        dh2 = jnp.zeros((tm, D), F32)
        for p in range(P):
            da = _dot_nt(dfb, wd[p])
            g = g_ref[p].astype(F32)
            u = u_ref[p].astype(F32)
            sg = _sigmoid(g)
            dgp = (da * u * (sg * (1.0 + g * (1.0 - sg)))).astype(MXU_DTYPE)
            dup = (da * (g * sg)).astype(MXU_DTYPE)
            dg_ref[p] = dgp
            du_ref[p] = dup
            dh2 = dh2 + _dot(dgp, wg[p]) + _dot(dup, wu[p])
        r2 = lax.rsqrt(jnp.mean(x1 * x1, axis=-1, keepdims=True) + EPS)
        xr = x1 * r2
        n2 = xr * gf_ref[...]
        macc_ref[0:1, :] += _colsum(dh2)
        macc_ref[1:2, :] += _colsum(dh2 * n2)
        dn2 = dh2 * (1.0 + mod_ref[:, 4 * D:5 * D])
        gacc_ref[0:1, :] += _colsum(dn2 * xr)
        e = dn2 * gf_ref[...]
        dx1 = dy_t + r2 * e - xr * (r2 * jnp.mean(e * xr, axis=-1, keepdims=True))
        dx1_ref[...] = dx1
        macc_ref[3:4, :] += _colsum(dx1 * mix_ref[...].astype(F32))
        dmixb = (dx1 * mod_ref[:, 2 * D:3 * D]).astype(MXU_DTYPE)
        dmix_ref[...] = dmixb
        parts = [_dot_nt(dmixb, wo[p]) for p in range(P)]
        dycat = jnp.concatenate(parts, axis=1) if P > 1 else parts[0]
        dya_ref[...] = dycat[:, DC:]
        dyc = dycat[:, :DC]
        z, rs, ln = _layer_norm_parts(cv_ref[...], gln_ref[...], bln_ref[...])
        sg = _sigmoid(ln)
        dln = dyc * (sg * (1.0 + ln * (1.0 - sg)))
        lacc_ref[0:1, :] += _colsum(dln * z)
        lacc_ref[1:2, :] += _colsum(dln)
        dz = dln * gln_ref[...]
        dcv_ref[...] = rs * (dz - jnp.mean(dz, axis=-1, keepdims=True) - z * jnp.mean(dz * z, axis=-1, keepdims=True))

    row = lambda w: pl.BlockSpec((tm, w), lambda i: (i, 0))
    vec = lambda w: pl.BlockSpec((1, w), lambda i: (0, 0))
    blk = pl.BlockSpec((P, tm, Fb), lambda i: (0, i, 0))
    anyspec = pl.BlockSpec(memory_space=pl.ANY)
    return pl.pallas_call(
        body, grid=(T // tm,), name="ffn_bwd",
        in_specs=[row(D), row(D), blk, blk, row(D), row(D), row(DC),
                  pl.BlockSpec((None, 1, N_MOD * D), lambda i: (i // tps, 0, 0)),
                  vec(DC), vec(DC), vec(D), anyspec, anyspec, anyspec, anyspec],
        out_specs=[blk, blk, row(D), row(D), row(D), row(D - DC), row(DC),
                   pl.BlockSpec((None, 8, D), lambda i: (i // tps, 0, 0)),
                   pl.BlockSpec((8, D), lambda i: (0, 0)), pl.BlockSpec((8, DC), lambda i: (0, 0))],
        out_shape=[jax.ShapeDtypeStruct((P, T, Fb), MXU_DTYPE), jax.ShapeDtypeStruct((P, T, Fb), MXU_DTYPE),
                   jax.ShapeDtypeStruct((T, D), MXU_DTYPE), jax.ShapeDtypeStruct((T, D), F32),
                   jax.ShapeDtypeStruct((T, D), MXU_DTYPE), jax.ShapeDtypeStruct((T, D - DC), F32),
                   jax.ShapeDtypeStruct((T, DC), F32), jax.ShapeDtypeStruct((Bl, 8, D), F32),
                   jax.ShapeDtypeStruct((8, D), F32), jax.ShapeDtypeStruct((8, DC), F32)],
        scratch_shapes=[pltpu.VMEM(w_out.shape, w_out.dtype), pltpu.VMEM(w_gate.shape, w_gate.dtype),
                        pltpu.VMEM(w_up.shape, w_up.dtype), pltpu.VMEM(w_down.shape, w_down.dtype),
                        pltpu.SemaphoreType.DMA((4,))],
        compiler_params=_cp(("arbitrary",)),
    )(dy, x1, gs, us, fo, mixb, cv, mod, g_ln, b_ln, g_ffn, w_out, w_gate, w_up, w_down)


def _in_bwd(da, dg, dq, dk, dv, x2, dx1, mod, g_mix, w_in, *, S, tm):
    T, D = x2.shape
    P, _, Nb = w_in.shape
    DC = da.shape[1]
    NS = dq.shape[0]
    n_in = P * Nb
    tps = S // tm
    Bl = T // S

    def body(da_ref, dg_ref, dq_ref, dk_ref, dv_ref, x_ref, dx1_ref, mod_ref, g_ref, w_ref,
             dx_ref, dproj_ref, macc_ref, gacc_ref):
        i = pl.program_id(0)

        @pl.when(i == 0)
        def _():
            gacc_ref[...] = jnp.zeros_like(gacc_ref)

        @pl.when(i % tps == 0)
        def _():
            macc_ref[...] = jnp.zeros_like(macc_ref)

        pieces = [da_ref[...], dg_ref[...]] + [r[j] for r in (dq_ref, dk_ref, dv_ref) for j in range(NS)]
        dproj = jnp.concatenate(pieces, axis=1).astype(MXU_DTYPE)
        dproj_ref[...] = dproj
        dh = _dot_nt(dproj[:, 0:Nb], w_ref[0])
        for p in range(1, P):
            dh = dh + _dot_nt(dproj[:, Nb * p:Nb * (p + 1)], w_ref[p])
        x = x_ref[...]
        r = lax.rsqrt(jnp.mean(x * x, axis=-1, keepdims=True) + EPS)
        xr = x * r
        macc_ref[0:1, :] += _colsum(dh)
        macc_ref[1:2, :] += _colsum(dh * (xr * g_ref[...]))
        dn = dh * (1.0 + mod_ref[:, D:2 * D])
        gacc_ref[0:1, :] += _colsum(dn * xr)
        e = dn * g_ref[...]
        dx_ref[...] = dx1_ref[...] + r * e - xr * (r * jnp.mean(e * xr, axis=-1, keepdims=True))

    row = lambda w: pl.BlockSpec((tm, w), lambda i: (i, 0))
    slab = pl.BlockSpec((NS, tm, LANES), lambda i: (0, i, 0))
    return pl.pallas_call(
        body, grid=(T // tm,), name="in_bwd",
        in_specs=[row(DC), row(DC), slab, slab, slab, row(D), row(D),
                  pl.BlockSpec((None, 1, N_MOD * D), lambda i: (i // tps, 0, 0)),
                  pl.BlockSpec((1, D), lambda i: (0, 0)),
                  pl.BlockSpec((P, D, Nb), lambda i: (0, 0, 0))],
        out_specs=[row(D), row(n_in), pl.BlockSpec((None, 8, D), lambda i: (i // tps, 0, 0)),
                   pl.BlockSpec((8, D), lambda i: (0, 0))],
        out_shape=[jax.ShapeDtypeStruct((T, D), F32), jax.ShapeDtypeStruct((T, n_in), MXU_DTYPE),
                   jax.ShapeDtypeStruct((Bl, 8, D), F32), jax.ShapeDtypeStruct((8, D), F32)],
        compiler_params=_cp(("arbitrary",)),
    )(da, dg, dq, dk, dv, x2, dx1, mod, g_mix, w_in)


def _wgrad(a, b, *, P, name, tk, split=None, host=None):
    a_blk, b_blk = a.ndim == 3, b.ndim == 3
    plan, h_in, h_out = host if host is not None else (None, (), ())
    ni, no = len(h_in), len(h_out)
    T = a.shape[-2]
    if a_blk:
        R, C = a.shape[2], b.shape[1]
        a_of = lambda av, p: av[p]
        b_of = lambda bv, p: bv[...]
    elif b_blk:
        R, C = a.shape[1], b.shape[2]
        a_of = lambda av, p: av[...]
        b_of = lambda bv, p: bv[p]
    elif split == "a":
        R, C = a.shape[1] // P, b.shape[1]
        a_of = lambda av, p: av[:, R * p:R * (p + 1)]
        b_of = lambda bv, p: bv[...]
    else:
        R, C = a.shape[1], b.shape[1] // P
        a_of = lambda av, p: av[...]
        b_of = lambda bv, p: bv[:, C * p:C * (p + 1)]

    n_steps = T // tk

    def body(a_ref, b_ref, *rest):
        hin, o_ref, hout, sems = rest[:ni], rest[ni], rest[ni + 1:ni + 1 + no], rest[ni + 1 + no:]
        step = pl.program_id(0)

        @pl.when(step == 0)
        def _():
            o_ref[...] = jnp.zeros_like(o_ref)
            if plan is not None:
                plan.start(hin, hout, sems)

        if plan is not None:
            @pl.when(step == n_steps // 2)
            def _():
                plan.forward(hin, hout, sems)

        for p in range(P):
            o_ref[p] += _dot_tn(a_of(a_ref, p), b_of(b_ref, p))

        if plan is not None:
            @pl.when(step == n_steps - 1)
            def _():
                plan.finish(hin, hout, sems)

    def spec(v):
        if v.ndim == 3:
            return pl.BlockSpec((P, tk, v.shape[2]), lambda k: (0, k, 0))
        return pl.BlockSpec((tk, v.shape[1]), lambda k: (k, 0))

    anyspec = pl.BlockSpec(memory_space=pl.ANY)
    res = pl.pallas_call(
        body, grid=(n_steps,), name=name,
        in_specs=[spec(a), spec(b)] + [anyspec] * ni,
        out_specs=[pl.BlockSpec((P, R, C), lambda k: (0, 0, 0))] + [anyspec] * no,
        out_shape=[jax.ShapeDtypeStruct((P, R, C), F32)] + list(h_out),
        scratch_shapes=plan.scratch() if plan is not None else [],
        compiler_params=_cp(("arbitrary",)),
    )(a, b, *h_in)
    return res if plan is not None else res[0]


TM_IN = 512
TM_FFN = 256
TK_WGRAD = 1024


def _alibi_slabs(n_slab):
    heads = 2 * n_slab
    slopes = 2.0 ** (-8.0 * np.arange(1, heads + 1) / heads)
    return jnp.asarray(np.broadcast_to(np.repeat(slopes.reshape(n_slab, 1, 2), HEAD_DIM, axis=2), (n_slab, 8, LANES)),
                       dtype=F32)


def _local_step(x, tgt, mod, g_mix, wdw, g_ln, b_ln, g_q, g_k, g_ffn, w_in, w_out, w_gate, w_up, w_down,
                pc_idx=None):
    Bl, S, D = x.shape
    T = Bl * S
    DC = g_ln.shape[1]
    P = w_in.shape[0]
    n_slab = (D - DC) // LANES
    x2 = x.reshape(T, D)
    t2 = tgt.reshape(T, D)
    mod3 = mod.reshape(Bl, 1, N_MOD * D)
    gq2 = jnp.tile(g_q, (1, LANES // HEAD_DIM))
    gk2 = jnp.tile(g_k, (1, LANES // HEAD_DIM))
    slopes = _alibi_slabs(n_slab)

    ag, qkv, qkh, h1 = _fwd_in(x2, mod3, g_mix, gq2, gk2, w_in, S=S, tm=TM_IN, n_ag=2 * DC)
    cv = _conv_fwd(ag, wdw, Bl=Bl, S=S, DC=DC)
    if pc_idx is not None:
        ya, lse, w_out, w_gate, w_up, w_down = _attn_fwd(qkh, qkv, slopes, Bl=Bl, S=S,
                                                         hosted=(w_out, w_gate, w_up, w_down))
    else:
        ya, lse = _attn_fwd(qkh, qkv, slopes, Bl=Bl, S=S)
    x1, ycat, mixb, h2, gs, us, acts, fo, dy, lossb = _ffn_fwd(
        x2, cv, ya, t2, mod3, g_ln, b_ln, g_ffn, w_out, w_gate, w_up, w_down, S=S, tm=TM_FFN)
    dgs, dus, dfb, dx1, dmixb, dya, dcv, macc_f, gacc_f, lacc = _ffn_bwd(
        dy, x1, gs, us, fo, mixb, cv, mod3, g_ln, b_ln, g_ffn, w_out, w_gate, w_up, w_down, S=S, tm=TM_FFN)
    wg = functools.partial(_wgrad, P=P, tk=TK_WGRAD)
    out = {}
    if pc_idx is None:
        grads = dict(w_down=wg(acts, dfb, name="wgrad_down"), w_gate=wg(dgs, h2, name="wgrad_gate"),
                     w_up=wg(dus, h2, name="wgrad_up"), w_out=wg(ycat, dmixb, name="wgrad_out", split="a"))
        dq, dk, dv, gqk = _attn_bwd(qkh, qkv, ya, lse, dya, gq2, gk2, slopes, Bl=Bl, S=S)
    else:
        g_down = wg(acts, dfb, name="wgrad_down")
        g_gate, r_down = wg(dgs, h2, name="wgrad_gate", host=_sibling_host([g_down]))
        g_up, r_gate = wg(dus, h2, name="wgrad_up", host=_sibling_host([g_gate]))
        g_out, r_up = wg(ycat, dmixb, name="wgrad_out", split="a", host=_sibling_host([g_up]))
        (r_out,) = _rs_sibling([g_out], "rs_sibling_out")
        grads = dict(w_down=g_down, w_gate=g_gate, w_up=g_up, w_out=g_out)
        sums = _pair_add([grads[nm] for nm in EARLY_WEIGHTS], [r_down, r_gate, r_up, r_out], pc_idx, "pair_add_early")
        res = _attn_bwd(qkh, qkv, ya, lse, dya, gq2, gk2, slopes, Bl=Bl, S=S, hosted=tuple(sb for _, sb in sums))
        dq, dk, dv, gqk = res[:4]
        out["early_sums"] = [s32 for s32, _ in sums]
        out["early_recv"] = list(res[4:])
    da, dg, dwdw = _conv_bwd(ag, dcv, wdw, Bl=Bl, S=S, DC=DC)
    dx, dprojb, macc_m, gacc_m = _in_bwd(da, dg, dq, dk, dv, x2, dx1, mod3, g_mix, w_in, S=S, tm=TM_IN)
    packed = _pack_small(macc_m, macc_f, gacc_m, gacc_f, lacc, gqk, dwdw, lossb)
    if pc_idx is None:
        grads["w_in"] = wg(h1, dprojb, name="wgrad_in", split="b")
    else:
        grads["w_in"], out["gathered_small"] = wg(h1, dprojb, name="wgrad_in", split="b",
                                                  host=_small_gather_host(packed))
    out.update(dx=dx.reshape(Bl, S, D), grads=grads, packed=packed)
    return out


EARLY_WEIGHTS = ("w_down", "w_gate", "w_up", "w_out")


def _small_layout(Bl):
    return 8 * Bl, 8 * Bl + 8, 8 * Bl + 8 + CONV_ROWS


def _pack_small(macc_m, macc_f, gacc_m, gacc_f, lacc, gqk, dwdw, lossb):
    Bl, _, D = macc_m.shape
    DC = lacc.shape[1]
    assert 2 * DC <= D
    SMALL_GAIN_ROW, SMALL_TAP_ROW, SMALL_ROWS = _small_layout(Bl)

    def body(mm_ref, mf_ref, gm_ref, gf_ref, la_ref, qk_ref, dw_ref, loss_ref, o_ref):
        o_ref[...] = jnp.zeros_like(o_ref)
        for b in range(Bl):
            o_ref[8 * b + 0:8 * b + 2, :] = mm_ref[b, 0:2, :]
            o_ref[8 * b + 2:8 * b + 3, :] = mf_ref[b, 3:4, :]
            o_ref[8 * b + 3:8 * b + 6, :] = mf_ref[b, 0:3, :]
        r = SMALL_GAIN_ROW
        o_ref[r:r + 1, :] = gm_ref[0:1, :]
        o_ref[r + 1:r + 2, :] = gf_ref[0:1, :]
        o_ref[r + 2:r + 3, 0:DC] = la_ref[0:1, :]
        o_ref[r + 2:r + 3, DC:2 * DC] = la_ref[1:2, :]
        qk = qk_ref[0:2, 0:HEAD_DIM] + qk_ref[0:2, HEAD_DIM:2 * HEAD_DIM]
        o_ref[r + 3:r + 4, 0:HEAD_DIM] = qk[0:1, :]
        o_ref[r + 3:r + 4, HEAD_DIM:2 * HEAD_DIM] = qk[1:2, :]
        o_ref[r + 4:r + 5, 0:LANES] = loss_ref[0:1, :]
        o_ref[SMALL_TAP_ROW:SMALL_TAP_ROW + CONV_ROWS, 0:DC] = dw_ref[...]

    return pl.pallas_call(body, name="pack_small", out_shape=jax.ShapeDtypeStruct((SMALL_ROWS, D), F32),
                          compiler_params=_cp())(macc_m, macc_f, gacc_m, gacc_f, lacc, gqk, dwdw, lossb)


def _row_tile(rows, cap=512):
    if rows <= cap:
        return rows
    best = rows
    for t in range(8, cap + 1, 8):
        if rows % t == 0:
            best = t
    return best


def _cast_weights(ws, pidx, name):
    n = len(ws)
    halves = [(w.shape[0] // 2, w.shape[1]) for w in ws]

    def body(p_ref, *refs):
        for k in range(n):
            refs[n + k][...] = refs[k][...].astype(MXU_DTYPE)

    return pl.pallas_call(
        body, name=name,
        grid_spec=pltpu.PrefetchScalarGridSpec(
            num_scalar_prefetch=1, grid=(2,),
            in_specs=[pl.BlockSpec(h, lambda i, p: (i, 0)) for h in halves],
            out_specs=[pl.BlockSpec((None,) + h, lambda i, p: (p[0], i, 0)) for h in halves]),
        out_shape=[jax.ShapeDtypeStruct((4,) + w.shape, MXU_DTYPE) for w in ws],
        compiler_params=_cp(),
    )(pidx, *ws)


def _pair_add(gs, recvs, pc_idx, name):
    n = len(gs)
    P = gs[0].shape[0]
    halves = [(g.shape[1] // 2, g.shape[2]) for g in gs]

    def body(pc_ref, *refs):
        for k in range(n):
            g_ref, r_ref, o_ref, ob_ref = refs[k], refs[n + k], refs[2 * n + 2 * k], refs[2 * n + 2 * k + 1]
            s = g_ref[...] + r_ref[...]
            ob_ref[...] = s.astype(jnp.bfloat16)

            @pl.when(pl.program_id(0) == pc_ref[0])
            def _(o_ref=o_ref, s=s):
                o_ref[...] = s

    res = pl.pallas_call(
        body, name=name,
        grid_spec=pltpu.PrefetchScalarGridSpec(
            num_scalar_prefetch=1, grid=(P,),
            in_specs=[pl.BlockSpec((None,) + h, lambda p, pc: (p, pc[1], 0)) for h in halves]
                     + [pl.BlockSpec((None,) + h, lambda p, pc: (p, 0, 0)) for h in halves],
            out_specs=[spec for h in halves for spec in (pl.BlockSpec(h, lambda p, pc: (0, 0)),
                                                         pl.BlockSpec((None,) + h, lambda p, pc: (p, 0, 0)))]),
        out_shape=[shape for h in halves for shape in (jax.ShapeDtypeStruct(h, F32),
                                                       jax.ShapeDtypeStruct((P,) + h, jnp.bfloat16))],
        compiler_params=_cp(),
    )(pc_idx, *gs, *recvs)
    return [(res[2 * k], res[2 * k + 1]) for k in range(n)]


def _final_add(owns, recvs, pc_idx, name):
    n = len(owns)

    def body(pc_ref, *refs):
        for k in range(n):
            acc = refs[k][...]
            for j in range(3):
                acc = acc + refs[n + k][j].astype(F32)
            refs[2 * n + k][...] = acc

    return pl.pallas_call(
        body, name=name,
        grid_spec=pltpu.PrefetchScalarGridSpec(
            num_scalar_prefetch=1, grid=(1,),
            in_specs=[pl.BlockSpec(o.shape, lambda i, pc: (0, 0)) for o in owns]
                     + [pl.BlockSpec((3,) + o.shape, lambda i, pc: (0, 0, 0)) for o in owns],
            out_specs=[pl.BlockSpec(o.shape, lambda i, pc: (pc[1], 0)) for o in owns]),
        out_shape=[jax.ShapeDtypeStruct((2 * o.shape[0], o.shape[1]), F32) for o in owns],
        compiler_params=_cp(),
    )(pc_idx, *owns, *recvs)


def _adamw_update(w_ref, g_ref, m_ref, v_ref, d_ref, nm_ref, nv_ref):
    c1 = 1.0 - ADAM_B1 ** ADAM_STEP
    c2 = 1.0 - ADAM_B2 ** ADAM_STEP
    gg = g_ref[...]
    nm = ADAM_B1 * m_ref[...] + (1.0 - ADAM_B1) * gg
    nv = ADAM_B2 * v_ref[...] + (1.0 - ADAM_B2) * (gg * gg)
    nm_ref[...] = nm
    nv_ref[...] = nv
    d_ref[...] = -ADAM_LR * ((nm / c1) / (jnp.sqrt(nv / c2) + ADAM_EPS) + ADAM_WD * w_ref[...])


def _adamw(w, g, m, v, name):
    R, C = w.shape
    tr = _row_tile(R, 256)
    spec = pl.BlockSpec((tr, C), lambda i: (i, 0))
    return pl.pallas_call(
        functools.partial(_adamw_update), grid=(R // tr,), name=name,
        in_specs=[spec] * 4, out_specs=[spec] * 3,
        out_shape=[jax.ShapeDtypeStruct((R, C), F32)] * 3,
    )(w, g, m, v)


def _adamw_small(quads, name):
    n = len(quads)

    def body(*refs):
        for k in range(n):
            _adamw_update(*refs[4 * k:4 * k + 4], *refs[4 * n + 3 * k:4 * n + 3 * k + 3])

    whole = lambda a: pl.BlockSpec(a.shape, lambda i: (0, 0))
    res = pl.pallas_call(
        body, grid=(1,), name=name,
        in_specs=[whole(a) for q in quads for a in q],
        out_specs=[whole(q[0]) for q in quads for _ in range(3)],
        out_shape=[jax.ShapeDtypeStruct(q[0].shape, F32) for q in quads for _ in range(3)],
    )(*[a for q in quads for a in q])
    return [tuple(res[3 * k:3 * k + 3]) for k in range(n)]


def _startup(first, w_ada, b_cols, w_in_buf, *, Bl):
    rows, D = first.shape
    NA = w_ada.shape[1]
    n_dev = 8
    g_w = _WeightGather([w_in_buf.shape])
    g_c = _SmallGather(rows)
    g_m = _SmallGather(n_dev * Bl)

    def body(first_ref, wada_ref, b_ref, win_in, g0_ref, call_ref, gm_ref, win_out, modp,
             ws0, ws1, cs0, cs1, cs2, ms0, ms1, ms2):
        g_w.start([win_out], (ws0, ws1))
        for phase in (g_c.start, g_c.forward, g_c.finish):
            phase([first_ref], [g0_ref], (cs0, cs1, cs2))
        for d in range(n_dev):
            call_ref[Bl * d:Bl * (d + 1), :] = g0_ref[rows * d:rows * d + Bl, :]
        c = call_ref[...]
        modp[...] = jnp.dot(c * _sigmoid(c), wada_ref[...], preferred_element_type=F32,
                            precision=lax.Precision.HIGH) + b_ref[...]
        for phase in (g_m.start, g_m.forward, g_m.finish):
            phase([modp], [gm_ref], (ms0, ms1, ms2))
        g_w.forward([win_out], (ws0, ws1))
        g_w.finish([win_out], (ws0, ws1))

    vmem = pl.BlockSpec(memory_space=pltpu.VMEM)
    anyspec = pl.BlockSpec(memory_space=pl.ANY)
    return pl.pallas_call(
        body, name="startup",
        in_specs=[vmem, vmem, vmem, anyspec], out_specs=[vmem, vmem, vmem, anyspec],
        out_shape=[jax.ShapeDtypeStruct((n_dev * rows, D), F32), jax.ShapeDtypeStruct((n_dev * Bl, D), F32),
                   jax.ShapeDtypeStruct((n_dev * n_dev * Bl, NA), F32),
                   jax.ShapeDtypeStruct(w_in_buf.shape, w_in_buf.dtype)],
        input_output_aliases={3: 3},
        scratch_shapes=[pltpu.VMEM((n_dev * Bl, NA), F32)] + g_w.scratch() + g_c.scratch() + g_m.scratch(),
        compiler_params=_cp(),
    )(first, w_ada, b_cols, w_in_buf)


def _ada_bwd(c_all, dmod_cols):
    def body(c_ref, d_ref, o_ref):
        c = c_ref[...]
        o_ref[...] = _dot_tn((c * _sigmoid(c)).astype(MXU_DTYPE), d_ref[...].astype(MXU_DTYPE))
    return pl.pallas_call(
        body, name="ada_bwd", out_shape=jax.ShapeDtypeStruct((c_all.shape[1], dmod_cols.shape[1]), F32),
        compiler_params=_cp(),
    )(c_all, dmod_cols)


def _small_reduce(gathered, n_dev, Bl):
    mod_rows, _, rows = _small_layout(Bl)
    width = gathered.shape[1]

    def body(g_ref, red_ref, bada_ref):
        acc = g_ref[0:rows, :]
        for d in range(1, n_dev):
            acc = acc + g_ref[d * rows:(d + 1) * rows, :]
        red_ref[...] = acc[mod_rows:, :]
        b = acc[0:8, :]
        for q in range(1, Bl):
            b = b + acc[8 * q:8 * q + 8, :]
        bada_ref[...] = b
    return pl.pallas_call(
        body, name="small_reduce",
        out_shape=[jax.ShapeDtypeStruct((rows - mod_rows, width), F32), jax.ShapeDtypeStruct((8, width), F32)],
        compiler_params=_cp(),
    )(gathered)


def _mesh_pos():
    return lax.axis_index("x"), lax.axis_index("y"), lax.axis_index("c")


def _other_chips(x, y):
    return [(1 - x, y), (x, 1 - y), (1 - x, 1 - y)]


class _WeightGather:
    def __init__(self, shapes):
        self.shapes = shapes
        self.n = len(shapes)

    def scratch(self):
        return [pltpu.SemaphoreType.DMA((6 * self.n,)), pltpu.SemaphoreType.DMA((6 * self.n,))]

    def _copy(self, outs, sems, w, k, slot, h, to):
        r2 = self.shapes[w][1] // 2
        blk = outs[w].at[slot, pl.ds(h * r2, r2), :]
        return pltpu.make_async_remote_copy(
            src_ref=blk, dst_ref=blk, send_sem=sems[0].at[6 * w + k], recv_sem=sems[1].at[6 * w + k],
            device_id=to, device_id_type=MESH_DEV)

    def start(self, outs, sems):
        x, y, c = _mesh_pos()
        for w in range(self.n):
            for k, chip in enumerate(_other_chips(x, y)):
                self._copy(outs, sems, w, k, 2 * x + y, c, (*chip, c)).start()

    def forward(self, outs, sems):
        x, y, c = _mesh_pos()
        for w in range(self.n):
            for k, chip in enumerate(_other_chips(x, y)):
                slot = 2 * chip[0] + chip[1]
                self._copy(outs, sems, w, k, slot, c, (x, y, 1 - c)).wait_recv()
                self._copy(outs, sems, w, 3 + k, slot, c, (x, y, 1 - c)).start()

    def finish(self, outs, sems):
        x, y, c = _mesh_pos()
        for w in range(self.n):
            for k, chip in enumerate(_other_chips(x, y)):
                slot = 2 * chip[0] + chip[1]
                self._copy(outs, sems, w, 3 + k, slot, 1 - c, (x, y, 1 - c)).wait_recv()
                self._copy(outs, sems, w, k, 2 * x + y, c, (*chip, c)).wait_send()
                self._copy(outs, sems, w, 3 + k, slot, c, (x, y, 1 - c)).wait_send()


class _SiblingExchange:
    def __init__(self, shapes):
        self.shapes = shapes

    def scratch(self):
        n = sum(s[0] for s in self.shapes)
        return [pltpu.SemaphoreType.DMA((n,)), pltpu.SemaphoreType.DMA((n,))]

    def out_shapes(self, dtype):
        return [jax.ShapeDtypeStruct((s[0], s[1] // 2, s[2]), dtype) for s in self.shapes]

    def _copies(self, ins, outs, sems):
        x, y, c = _mesh_pos()
        cps, k = [], 0
        for w, (P, R, _) in enumerate(self.shapes):
            r2 = R // 2
            for p in range(P):
                cps.append(pltpu.make_async_remote_copy(
                    src_ref=ins[w].at[p, pl.ds((1 - c) * r2, r2), :], dst_ref=outs[w].at[p],
                    send_sem=sems[0].at[k], recv_sem=sems[1].at[k],
                    device_id=(x, y, 1 - c), device_id_type=MESH_DEV))
                k += 1
        return cps

    def start(self, ins, outs, sems):
        for cp in self._copies(ins, outs, sems):
            cp.start()

    def forward(self, ins, outs, sems):
        pass

    def finish(self, ins, outs, sems):
        for cp in self._copies(ins, outs, sems):
            cp.wait()


def _sibling_host(grads):
    plan = _SiblingExchange([g.shape for g in grads])
    return plan, tuple(grads), tuple(plan.out_shapes(grads[0].dtype))


def _rs_sibling(grads, name):
    n = len(grads)
    plan, _, out_shapes = _sibling_host(grads)

    def body(*refs):
        ins, outs, sems = refs[:n], refs[n:2 * n], refs[2 * n:]
        plan.start(ins, outs, sems)
        plan.finish(ins, outs, sems)

    anyspec = pl.BlockSpec(memory_space=pl.ANY)
    return pl.pallas_call(
        body, name=name, out_shape=list(out_shapes),
        in_specs=[anyspec] * n, out_specs=[anyspec] * n, scratch_shapes=plan.scratch(),
    )(*grads)


class _SmallGather:
    def __init__(self, m_per):
        self.m = m_per

    def scratch(self):
        return [pltpu.SemaphoreType.DMA((7,)), pltpu.SemaphoreType.DMA((7,)), pltpu.SemaphoreType.DMA]

    def _rows(self, out, pos):
        px, py, pc = pos
        return out.at[pl.ds((4 * px + 2 * py + pc) * self.m, self.m), :]

    def _copy(self, out, sems, k, block, to, src=None):
        dst = self._rows(out, block)
        return pltpu.make_async_remote_copy(
            src_ref=dst if src is None else src, dst_ref=dst, send_sem=sems[0].at[k], recv_sem=sems[1].at[k],
            device_id=to, device_id_type=MESH_DEV)

    def start(self, ins, outs, sems):
        x, y, c = _mesh_pos()
        me = (x, y, c)
        pltpu.make_async_copy(ins[0], self._rows(outs[0], me), sems[2]).start()
        self._copy(outs[0], sems, 0, me, (x, y, 1 - c), src=ins[0]).start()
        for j, chip in enumerate(_other_chips(x, y)):
            self._copy(outs[0], sems, 1 + j, me, (*chip, c), src=ins[0]).start()

    def forward(self, ins, outs, sems):
        x, y, c = _mesh_pos()
        for j, chip in enumerate(_other_chips(x, y)):
            self._copy(outs[0], sems, 1 + j, (*chip, c), (x, y, c)).wait_recv()
            self._copy(outs[0], sems, 4 + j, (*chip, c), (x, y, 1 - c)).start()

    def finish(self, ins, outs, sems):
        x, y, c = _mesh_pos()
        me = (x, y, c)
        self._copy(outs[0], sems, 0, (x, y, 1 - c), me).wait_recv()
        for j, chip in enumerate(_other_chips(x, y)):
            self._copy(outs[0], sems, 4 + j, (*chip, 1 - c), me).wait_recv()
        self._copy(outs[0], sems, 0, me, (x, y, 1 - c), src=ins[0]).wait_send()
        for j, chip in enumerate(_other_chips(x, y)):
            self._copy(outs[0], sems, 1 + j, me, (*chip, c), src=ins[0]).wait_send()
            self._copy(outs[0], sems, 4 + j, (*chip, c), (x, y, 1 - c)).wait_send()
        pltpu.make_async_copy(ins[0], self._rows(outs[0], me), sems[2]).wait()


def _small_gather_host(packed):
    m, n = packed.shape
    return _SmallGather(m), (packed,), (jax.ShapeDtypeStruct((8 * m, n), packed.dtype),)


class _ChipExchange:
    def __init__(self, n):
        self.n = n

    def scratch(self):
        return [pltpu.SemaphoreType.DMA((3 * self.n,)), pltpu.SemaphoreType.DMA((3 * self.n,))]

    def _copies(self, ins, outs, sems):
        x, y, c = _mesh_pos()
        return [pltpu.make_async_remote_copy(
            src_ref=ins[w].at[2 * chip[0] + chip[1]], dst_ref=outs[w].at[k],
            send_sem=sems[0].at[3 * w + k], recv_sem=sems[1].at[3 * w + k],
            device_id=(*chip, c), device_id_type=MESH_DEV)
            for w in range(self.n) for k, chip in enumerate(_other_chips(x, y))]

    def start(self, ins, outs, sems):
        for cp in self._copies(ins, outs, sems):
            cp.start()

    def forward(self, ins, outs, sems):
        pass

    def finish(self, ins, outs, sems):
        for cp in self._copies(ins, outs, sems):
            cp.wait()


def _rs_final(bufs, name, chips=()):
    n, nc = len(bufs), len(chips)
    plan = _ChipExchange(nc)

    def body(*refs):
        cin = refs[n:n + nc]
        outs = refs[n + nc:2 * n + nc]
        cout = refs[2 * n + nc:2 * n + 2 * nc]
        send_sems, recv_sems = refs[2 * n + 2 * nc:2 * n + 2 * nc + 2]
        csems = refs[2 * n + 2 * nc + 2:]
        x, y, c = _mesh_pos()
        if nc:
            plan.start(cin, cout, csems)
        cps = []
        for w in range(n):
            r2 = bufs[w].shape[0] // 2
            mine = outs[w].at[pl.ds(c * r2, r2), :]
            cps.append(pltpu.make_async_remote_copy(
                src_ref=mine, dst_ref=mine, send_sem=send_sems.at[w], recv_sem=recv_sems.at[w],
                device_id=(x, y, 1 - c), device_id_type=MESH_DEV))
            cps[-1].start()
        for cp in cps:
            cp.wait()
        if nc:
            plan.finish(cin, cout, csems)

    anyspec = pl.BlockSpec(memory_space=pl.ANY)
    return pl.pallas_call(
        body, name=name,
        out_shape=[jax.ShapeDtypeStruct(b.shape, b.dtype) for b in bufs]
                  + [jax.ShapeDtypeStruct((3,) + s.shape[1:], s.dtype) for s in chips],
        in_specs=[anyspec] * (n + nc), out_specs=[anyspec] * (n + nc),
        input_output_aliases={w: w for w in range(n)},
        scratch_shapes=[pltpu.SemaphoreType.DMA((n,)), pltpu.SemaphoreType.DMA((n,))] + (plan.scratch() if nc else []),
    )(*bufs, *chips)


BIG = ("w_in", "w_out", "w_gate", "w_up", "w_down")
TRANSPOSED = ("w_gate", "w_up")
WEIGHTS = ("w_ada", "b_ada", "g_mix", "w_in", "w_dw", "b_dw", "g_conv_ln", "b_conv_ln", "g_q", "g_k",
           "w_out", "g_ffn", "w_gate", "w_up", "w_down")


def _pad_to(a, rows, cols):
    return jnp.pad(a, ((0, rows - a.shape[0]), (0, cols - a.shape[1])))


def kernel(x, c, w_ada, b_ada, g_mix, w_in, w_dw, b_dw, g_conv_ln, b_conv_ln, g_q, g_k, w_out, g_ffn, w_gate, w_up, w_down, loss_target, m_w_ada, m_b_ada, m_g_mix, m_w_in, m_w_dw, m_b_dw, m_g_conv_ln, m_b_conv_ln, m_g_q, m_g_k, m_w_out, m_g_ffn, m_w_gate, m_w_up, m_w_down, v_w_ada, v_b_ada, v_g_mix, v_w_in, v_w_dw, v_b_dw, v_g_conv_ln, v_b_conv_ln, v_g_q, v_g_k, v_w_out, v_g_ffn, v_w_gate, v_w_up, v_w_down):
    w = dict(w_ada=w_ada, b_ada=b_ada, g_mix=g_mix, w_in=w_in, w_dw=w_dw, b_dw=b_dw, g_conv_ln=g_conv_ln,
             b_conv_ln=b_conv_ln, g_q=g_q, g_k=g_k, w_out=w_out, g_ffn=g_ffn, w_gate=w_gate, w_up=w_up, w_down=w_down)
    m = dict(w_ada=m_w_ada, b_ada=m_b_ada, g_mix=m_g_mix, w_in=m_w_in, w_dw=m_w_dw, b_dw=m_b_dw, g_conv_ln=m_g_conv_ln,
             b_conv_ln=m_b_conv_ln, g_q=m_g_q, g_k=m_g_k, w_out=m_w_out, g_ffn=m_g_ffn, w_gate=m_w_gate, w_up=m_w_up,
             w_down=m_w_down)
    v = dict(w_ada=v_w_ada, b_ada=v_b_ada, g_mix=v_g_mix, w_in=v_w_in, w_dw=v_w_dw, b_dw=v_b_dw, g_conv_ln=v_g_conv_ln,
             b_conv_ln=v_b_conv_ln, g_q=v_g_q, g_k=v_g_k, w_out=v_w_out, g_ffn=v_g_ffn, w_gate=v_w_gate, w_up=v_w_up,
             w_down=v_w_down)
    Bl, S, D = x.shape
    DC = g_conv_ln.shape[1]
    NA = w_ada.shape[2]
    xi, yi, ci = _mesh_pos()
    p = 2 * xi + yi
    dev = 2 * p + ci
    n_dev = 8
    pidx = jnp.reshape(p, (1,)).astype(jnp.int32)
    pc_idx = jnp.stack([p, ci]).astype(jnp.int32)

    first = jnp.concatenate([_pad_to(c, 8, D), _pad_to(w_dw[0], CONV_ROWS, D)], axis=0)
    shard = lambda a, nm: a[0].T if nm in TRANSPOSED else a[0]
    owned = dict(zip(BIG, _cast_weights([shard(w[nm], nm) for nm in BIG], pidx, "cast_weights")))
    b_cols = lax.dynamic_slice_in_dim(b_ada, p * NA, NA, axis=1)
    g0, c_all, gm, w_in_full = _startup(first, w_ada[0], b_cols, owned["w_in"], Bl=Bl)
    g0 = g0.reshape(n_dev, 8 + CONV_ROWS, D)
    taps = jnp.concatenate([g0[2 * q, 8:, :w_dw.shape[2]] for q in range(4)], axis=1)
    wdw = jnp.where(lax.broadcasted_iota(jnp.int32, taps.shape, 0) == CONV_WIDTH, b_dw, taps)
    gm = gm.reshape(n_dev, n_dev * Bl, NA)
    mod = jnp.concatenate([lax.dynamic_slice_in_dim(gm[2 * q], dev * Bl, Bl, axis=0) for q in range(4)], axis=1)

    loc = _local_step(x, loss_target, mod, g_mix, wdw, g_conv_ln, b_conv_ln, g_q, g_k, g_ffn,
                      w_in_full, owned["w_out"], owned["w_gate"], owned["w_up"], owned["w_down"], pc_idx=pc_idx)

    halves = _final_add(loc["early_sums"], loc["early_recv"], pc_idx, "final_add_early")
    (late_sib,) = _rs_sibling([loc["grads"]["w_in"]], "rs_sibling_in")
    ((late32, late16),) = _pair_add([loc["grads"]["w_in"]], [late_sib], pc_idx, "pair_add_w_in")
    *early_full, late_recv = _rs_final(halves, "rs_final_early", chips=(late16,))
    grad = dict(zip(EARLY_WEIGHTS, early_full))
    grad["w_in"], = _rs_final(_final_add([late32], [late_recv], pc_idx, "final_add_w_in"), "rs_final_in")

    mod_rows, _, small_rows = _small_layout(Bl)
    gs = loc["gathered_small"]
    red, bada8 = _small_reduce(gs, n_dev, Bl)
    dmod_all = gs.reshape(n_dev, small_rows, D)[:, :mod_rows].reshape(n_dev * Bl, 8, D)[:, :N_MOD].reshape(n_dev * Bl, N_MOD * D)
    grad["w_ada"] = _ada_bwd(c_all, lax.dynamic_slice_in_dim(dmod_all, p * NA, NA, axis=1))
    grad["b_ada"] = bada8[:N_MOD].reshape(1, N_MOD * D)
    grad["g_mix"] = red[0:1]
    grad["g_ffn"] = red[1:2]
    grad["g_conv_ln"] = red[2:3, :DC]
    grad["b_conv_ln"] = red[2:3, DC:2 * DC]
    grad["g_q"] = red[3:4, :HEAD_DIM]
    grad["g_k"] = red[3:4, HEAD_DIM:2 * HEAD_DIM]
    loss = red[4, 0]
    dwdw = red[8:8 + CONV_ROWS, :DC]
    grad["w_dw"] = lax.dynamic_slice_in_dim(dwdw[:CONV_WIDTH], p * w_dw.shape[2], w_dw.shape[2], axis=1)
    grad["b_dw"] = dwdw[CONV_WIDTH:CONV_WIDTH + 1]

    delta, new_m, new_v = {}, {}, {}
    two_d = lambda nm: w[nm].shape[-2:]
    small = [nm for nm in WEIGHTS if nm not in BIG and nm != "w_ada"]
    small_res = dict(zip(small, _adamw_small(
        [tuple(a.reshape(two_d(nm)) for a in (w[nm], grad[nm], m[nm], v[nm])) for nm in small], "adamw_small")))
    for nm in WEIGHTS:
        shp = w[nm].shape
        if nm in TRANSPOSED:
            d_, m_, v_ = _adamw(w[nm][0].T, grad[nm], m[nm][0].T, v[nm][0].T, "adamw_" + nm)
            grad[nm], delta[nm], new_m[nm], new_v[nm] = (a.T.reshape(shp) for a in (grad[nm], d_, m_, v_))
            continue
        if nm in small_res:
            d_, m_, v_ = small_res[nm]
        else:
            d_, m_, v_ = _adamw(*(a.reshape(two_d(nm)) for a in (w[nm], grad[nm], m[nm], v[nm])), "adamw_" + nm)
        grad[nm] = grad[nm].reshape(shp)
        delta[nm], new_m[nm], new_v[nm] = d_.reshape(shp), m_.reshape(shp), v_.reshape(shp)

    return (loss, loc["dx"], *[grad[nm] for nm in WEIGHTS], *[delta[nm] for nm in WEIGHTS],
            *[new_m[nm] for nm in WEIGHTS], *[new_v[nm] for nm in WEIGHTS])
```

```python
import functools

import jax
import jax.numpy as jnp
import numpy as np
from jax import lax
from jax.experimental import pallas as pl
from jax.experimental.pallas import tpu as pltpu

F32 = jnp.float32
MXU_DTYPE = jnp.bfloat16
ACT_DTYPE = jnp.bfloat16
EPS = 1e-6
NEG_INF = -1e30
HEAD_DIM = 64
LANES = 128
MXU_COLS = 256
RADIUS = 64
QBLK = 128
DILATIONS = (1, 4, 16)
CONV_WIDTH = 31
CONV_PAD = CONV_WIDTH // 2
CONV_ROWS = 32
N_MOD = 6
ADAM_LR, ADAM_B1, ADAM_B2, ADAM_EPS, ADAM_WD, ADAM_STEP = 0.001, 0.9, 0.999, 1e-08, 0.01, 10
MESH_DEV = pl.DeviceIdType.MESH
VMEM_LIMIT = 56 << 20
ATTN_BWD_VMEM = 60 << 20


def _cp(sem=None, vmem=VMEM_LIMIT):
    kw = dict(vmem_limit_bytes=vmem)
    if sem is not None:
        kw["dimension_semantics"] = sem
    return pltpu.CompilerParams(**kw)


def _sigmoid(x):
    return 1.0 / (1.0 + jnp.exp(-x))


def _dot(a, b):
    return jnp.dot(a, b, preferred_element_type=F32)


def _dot_nt(a, b):
    return lax.dot_general(a, b, (((1,), (1,)), ((), ())), preferred_element_type=F32)


def _dot_tn(a, b):
    return lax.dot_general(a, b, (((0,), (0,)), ((), ())), preferred_element_type=F32)


def _colsum(v):
    return jnp.sum(v, axis=0, keepdims=True)


def _load_resident(i, pairs, sems):
    @pl.when(i == 0)
    def _():
        cps = [pltpu.make_async_copy(src, dst, sems.at[n]) for n, (src, dst) in enumerate(pairs)]
        for c in cps:
            c.start()
        for c in cps:
            c.wait()


def _fwd_in(x2, mod, g_mix, gq2, gk2, w_in, *, S, tm, n_ag):
    T, D = x2.shape
    P, _, Nb = w_in.shape
    n_in = P * Nb
    n_slab = (n_in - n_ag) // LANES
    NS = n_slab // 3
    tps = S // tm

    def body(x_ref, mod_ref, g_ref, gq_ref, gk_ref, w_ref, ag_ref, qkv_ref, qkh_ref, h_ref):
        x = x_ref[...]
        r = lax.rsqrt(jnp.mean(x * x, axis=-1, keepdims=True) + EPS)
        n = x * r * g_ref[...]
        h = n * (1.0 + mod_ref[:, D:2 * D]) + mod_ref[:, 0:D]
        hb = h.astype(MXU_DTYPE)
        h_ref[...] = hb
        parts = [_dot(hb, w_ref[p]) for p in range(P)]
        proj = jnp.concatenate(parts, axis=1) if P > 1 else parts[0]
        ag_ref[...] = proj[:, :n_ag]
        mm = _head_mean_matrix()
        for j in range(n_slab):
            v = proj[:, n_ag + LANES * j:n_ag + LANES * (j + 1)]
            qkv_ref[j] = v
            if j < 2 * NS:
                gain = gq_ref[...] * (HEAD_DIM ** -0.5 * LOG2E) if j < NS else gk_ref[...]
                qkh_ref[j] = v * lax.rsqrt(_head_mean(v * v, mm) + EPS) * gain

    return pl.pallas_call(
        body, grid=(T // tm,), name="fwd_in",
        in_specs=[pl.BlockSpec((tm, D), lambda i: (i, 0)),
                  pl.BlockSpec((None, 1, N_MOD * D), lambda i: (i // tps, 0, 0)),
                  pl.BlockSpec((1, D), lambda i: (0, 0)),
                  pl.BlockSpec((1, LANES), lambda i: (0, 0)), pl.BlockSpec((1, LANES), lambda i: (0, 0)),
                  pl.BlockSpec((P, D, Nb), lambda i: (0, 0, 0))],
        out_specs=[pl.BlockSpec((tm, n_ag), lambda i: (i, 0)),
                   pl.BlockSpec((n_slab, tm, LANES), lambda i: (0, i, 0)),
                   pl.BlockSpec((2 * NS, tm, LANES), lambda i: (0, i, 0)),
                   pl.BlockSpec((tm, D), lambda i: (i, 0))],
        out_shape=[jax.ShapeDtypeStruct((T, n_ag), F32),
                   jax.ShapeDtypeStruct((n_slab, T, LANES), F32),
                   jax.ShapeDtypeStruct((2 * NS, T, LANES), F32),
                   jax.ShapeDtypeStruct((T, D), MXU_DTYPE)],
        compiler_params=_cp(("arbitrary",)),
    )(x2, mod, g_mix, gq2, gk2, w_in)


CONV_CH = 128


def _conv_taps(win, w_ref, acc, reverse):
    n = win.shape[0]
    for b in range(8):
        wb = win if b == 0 else pltpu.roll(win, shift=n - b, axis=0)
        for a in range(4):
            o = 8 * a + b
            if o < 1 or o > CONV_WIDTH:
                continue
            k = (CONV_WIDTH - o) if reverse else (o - 1)
            acc = acc + w_ref[k:k + 1, :] * wb[8 * a:8 * a + CONV_CH, :]
    return acc


def _conv_fwd(ag, wdw, *, Bl, S, DC):
    T = ag.shape[0]
    nsc = DC // LANES
    CH = CONV_CH

    def body(a_ref, g_ref, w_ref, cv_ref, upad):
        zeros16 = jnp.zeros((16, LANES), F32)
        upad[0:16, :] = zeros16
        upad[S + 16:S + 32, :] = zeros16

        def fill(i, _):
            r0 = pl.multiple_of(i * CH, CH)
            a = a_ref[pl.ds(r0, CH), :]
            g = g_ref[pl.ds(r0, CH), :]
            upad[pl.ds(r0 + 16, CH), :] = a * _sigmoid(g)
            return 0
        lax.fori_loop(0, S // CH, fill, 0)

        def conv(i, _):
            r0 = pl.multiple_of(i * CH, CH)
            win = upad[pl.ds(r0, CH + 32), :]
            acc = jnp.zeros((CH, LANES), F32) + w_ref[CONV_WIDTH:CONV_WIDTH + 1, :]
            cv_ref[pl.ds(r0, CH), :] = _conv_taps(win, w_ref, acc, reverse=False)
            return 0
        lax.fori_loop(0, S // CH, conv, 0)

    return pl.pallas_call(
        body, grid=(Bl, nsc), name="conv_fwd",
        in_specs=[pl.BlockSpec((S, LANES), lambda b, j: (b, j)),
                  pl.BlockSpec((S, LANES), lambda b, j: (b, nsc + j)),
                  pl.BlockSpec((CONV_ROWS, LANES), lambda b, j: (0, j))],
        out_specs=pl.BlockSpec((S, LANES), lambda b, j: (b, j)),
        out_shape=jax.ShapeDtypeStruct((T, DC), F32),
        scratch_shapes=[pltpu.VMEM((S + 32, LANES), F32)],
        compiler_params=_cp(("arbitrary", "arbitrary")),
    )(ag, ag, wdw)


def _conv_bwd(ag, dcv, wdw, *, Bl, S, DC):
    T = ag.shape[0]
    nsc = DC // LANES
    CH = CONV_CH

    def body(a_ref, g_ref, d_ref, w_ref, da_ref, dg_ref, dw_ref, upad, dpad, wacc):
        b = pl.program_id(1)
        zeros16 = jnp.zeros((16, LANES), F32)
        upad[0:16, :] = zeros16
        upad[S + 16:S + 32, :] = zeros16
        dpad[0:16, :] = zeros16
        dpad[S + 16:S + 32, :] = zeros16

        @pl.when(b == 0)
        def _():
            wacc[...] = jnp.zeros_like(wacc)

        def fill(i, _):
            r0 = pl.multiple_of(i * CH, CH)
            a = a_ref[pl.ds(r0, CH), :]
            g = g_ref[pl.ds(r0, CH), :]
            upad[pl.ds(r0 + 16, CH), :] = a * _sigmoid(g)
            dpad[pl.ds(r0 + 16, CH), :] = d_ref[pl.ds(r0, CH), :]
            return 0
        lax.fori_loop(0, S // CH, fill, 0)

        def step(i, _):
            r0 = pl.multiple_of(i * CH, CH)
            dwin = dpad[pl.ds(r0, CH + 32), :]
            du = _conv_taps(dwin, w_ref, jnp.zeros((CH, LANES), F32), reverse=True)
            a = a_ref[pl.ds(r0, CH), :]
            g = g_ref[pl.ds(r0, CH), :]
            sg = _sigmoid(g)
            da_ref[pl.ds(r0, CH), :] = du * sg
            dg_ref[pl.ds(r0, CH), :] = du * a * sg * (1.0 - sg)
            dc = d_ref[pl.ds(r0, CH), :]
            uwin = upad[pl.ds(r0, CH + 32), :]
            n = CH + 32
            for bb in range(8):
                wb = uwin if bb == 0 else pltpu.roll(uwin, shift=n - bb, axis=0)
                for aa in range(4):
                    o = 8 * aa + bb
                    if o < 1 or o > CONV_WIDTH:
                        continue
                    k = o - 1
                    prod = dc * wb[8 * aa:8 * aa + CH, :]
                    part = prod[0:8, :]
                    for q in range(1, CH // 8):
                        part = part + prod[8 * q:8 * q + 8, :]
                    wacc[8 * k:8 * k + 8, :] += part
            part = dc[0:8, :]
            for q in range(1, CH // 8):
                part = part + dc[8 * q:8 * q + 8, :]
            wacc[8 * CONV_WIDTH:8 * CONV_WIDTH + 8, :] += part
            return 0
        lax.fori_loop(0, S // CH, step, 0)

        @pl.when(b == Bl - 1)
        def _():
            for k in range(CONV_ROWS):
                dw_ref[k:k + 1, :] = jnp.sum(wacc[8 * k:8 * k + 8, :], axis=0, keepdims=True)

    return pl.pallas_call(
        body, grid=(nsc, Bl), name="conv_bwd",
        in_specs=[pl.BlockSpec((S, LANES), lambda j, b: (b, j)),
                  pl.BlockSpec((S, LANES), lambda j, b: (b, nsc + j)),
                  pl.BlockSpec((S, LANES), lambda j, b: (b, j)),
                  pl.BlockSpec((CONV_ROWS, LANES), lambda j, b: (0, j))],
        out_specs=[pl.BlockSpec((S, LANES), lambda j, b: (b, j)),
                   pl.BlockSpec((S, LANES), lambda j, b: (b, j)),
                   pl.BlockSpec((CONV_ROWS, LANES), lambda j, b: (0, j))],
        out_shape=[jax.ShapeDtypeStruct((T, DC), F32), jax.ShapeDtypeStruct((T, DC), F32),
                   jax.ShapeDtypeStruct((CONV_ROWS, DC), F32)],
        scratch_shapes=[pltpu.VMEM((S + 32, LANES), F32), pltpu.VMEM((S + 32, LANES), F32),
                        pltpu.VMEM((8 * CONV_ROWS, LANES), F32)],
        compiler_params=_cp(("arbitrary", "arbitrary")),
    )(ag, ag, dcv, wdw)


ROWCH = 256


LOG2E = 1.4426950408889634
LN2 = 0.6931471805599453
N_EDGE = 4


def _head_mean_matrix():
    r = lax.broadcasted_iota(jnp.int32, (LANES, LANES), 0) // HEAD_DIM
    c = lax.broadcasted_iota(jnp.int32, (LANES, LANES), 1) // HEAD_DIM
    return jnp.where(r == c, 1.0 / HEAD_DIM, 0.0).astype(jnp.bfloat16)


def _head_mean(v, mm):
    hi = v.astype(jnp.bfloat16)
    lo = (v - hi.astype(F32)).astype(jnp.bfloat16)
    return _dot(hi, mm) + _dot(lo, mm)


def _stack_heads(blk, lane_lo):
    z = jnp.zeros_like(blk)
    return jnp.concatenate([jnp.where(lane_lo, blk, z), jnp.where(lane_lo, z, blk)], axis=0)


def _merge_heads(v2, lane_lo):
    return jnp.where(lane_lo, v2[:QBLK], v2[QBLK:])


def _bias_tables(bias_ref, slope_ref):
    row = lax.broadcasted_iota(jnp.int32, (2 * QBLK, 2 * QBLK), 0)
    col = lax.broadcasted_iota(jnp.int32, (2 * QBLK, 2 * QBLK), 1)
    rel = jnp.abs(col - RADIUS - (row % QBLK))
    slope = jnp.where(row < QBLK, slope_ref[0:1, 0:1], slope_ref[0:1, HEAD_DIM:HEAD_DIM + 1]) * LOG2E
    for pi, d in enumerate(DILATIONS):
        inside = jnp.where(rel <= RADIUS, -slope * (float(d) * rel.astype(F32)), NEG_INF)
        for e in range(N_EDGE):
            t = inside
            if e & 1:
                t = jnp.where(col < RADIUS, NEG_INF, t)
            if e & 2:
                t = jnp.where(col >= QBLK + RADIUS, NEG_INF, t)
            bias_ref[N_EDGE * pi + e] = t


def _edge_index(qb, nb):
    return jnp.where(qb == 0, 1, 0) + jnp.where(qb == nb - 1, 2, 0)


VIA = 4


def _residue(d, s):
    return (s % VIA) * VIA + s // VIA if d == VIA * VIA else s


def _gather_rows(src_ref, dst_ref, S, d, pad, f32_copy=None):
    n = S // d
    seg = n + 2 * RADIUS if pad else n
    step = min(n, 512)
    two_step = d == VIA * VIA and f32_copy is not None
    for s in range(d):
        base = s * seg
        if pad:
            dst_ref[base:base + RADIUS, :] = jnp.zeros((RADIUS, LANES), dst_ref.dtype)
            dst_ref[base + RADIUS + n:base + seg, :] = jnp.zeros((RADIUS, LANES), dst_ref.dtype)
            base += RADIUS
        for c0 in range(0, n, step):
            if d == 1:
                v = src_ref[c0:c0 + step, :]
            elif two_step:
                v = f32_copy[pl.ds((s // VIA) * (S // VIA) + s % VIA + c0 * VIA, step, stride=VIA), :]
            else:
                v = src_ref[pl.ds(_residue(d, s) + c0 * d, step, stride=d), :]
                if d == VIA and f32_copy is not None:
                    f32_copy[s * n + c0:s * n + c0 + step, :] = v
            dst_ref[base + c0:base + c0 + step, :] = v.astype(dst_ref.dtype)


def _scatter_rows(src_ref, dst_ref, S, d, pad, accumulate, f32_tmp=None):
    n = S // d
    seg = n + 2 * RADIUS if pad else n
    first = RADIUS if pad else 0
    _unpermute(lambda s, c0, step: src_ref[s * seg + first + c0:s * seg + first + c0 + step, :],
               dst_ref, S, d, accumulate, f32_tmp)


def _unpermute(rows_of, dst_ref, S, d, accumulate, f32_tmp):
    n = S // d
    step = min(n, 512)
    if d == VIA * VIA and f32_tmp is not None:
        for s in range(d):
            f32_tmp[pl.ds((s // VIA) * (S // VIA) + s % VIA, n, stride=VIA), :] = rows_of(s, 0, n)
        n4 = S // VIA
        _unpermute(lambda s, c0, st: f32_tmp[s * n4 + c0:s * n4 + c0 + st, :], dst_ref, S, VIA, accumulate, None)
        return
    for s in range(d):
        for c0 in range(0, n, step):
            v = rows_of(s, c0, step)
            idx = pl.ds(c0, step) if d == 1 else pl.ds(_residue(d, s) + c0 * d, step, stride=d)
            if accumulate:
                dst_ref[idx, :] = dst_ref[idx, :] + v
            else:
                dst_ref[idx, :] = v


def _zero_uncovered(acc, S, d):
    n = S // d
    if (n // QBLK) % 2:
        return
    seg = n + 2 * RADIUS
    for r in range(d):
        acc[0, r * seg + n:r * seg + seg, :] = jnp.zeros((2 * RADIUS, LANES), F32)
        acc[1, r * seg:r * seg + 2 * RADIUS, :] = jnp.zeros((2 * RADIUS, LANES), F32)


def _scatter_parity(acc, dst_ref, S, d, f32_tmp=None):
    n = S // d
    seg = n + 2 * RADIUS
    one_block = (n // QBLK) % 2 == 1

    def rows_of(s, c0, step):
        rows = slice(s * seg + RADIUS + c0, s * seg + RADIUS + c0 + step)
        return acc[s % 2, rows, :] if one_block else acc[0, rows, :] + acc[1, rows, :]

    _unpermute(rows_of, dst_ref, S, d, True, f32_tmp)


PIPE_UNROLL = 4
PIPE_SLOTS = 16
BWD_SLOTS = 12


def _pipeline(n_items, stages, unroll):
    K = len(stages)
    assert n_items % unroll == 0 and K * unroll <= (PIPE_SLOTS if K == 4 else BWD_SLOTS)
    trips = n_items // unroll
    assert trips >= K - 1

    def trip(t, static):
        for s in reversed(range(K)):
            if static and not 0 <= t - s < trips:
                continue
            for u in range(unroll):
                item = unroll * (t - s) + u
                stages[s](jnp.int32(item) if static else item)

    for t in range(K - 1):
        trip(t, True)

    def full(t, carry):
        trip(t, False)
        return carry
    lax.fori_loop(K - 1, trips, full, 0)
    for t in range(trips, trips + K - 1):
        trip(t, True)


def _attn_fwd(qkh, qkv, slopes, *, Bl, S, hosted=()):
    n3, T, _ = qkv.shape
    NS = n3 // 3
    NB = S // QBLK
    PADR = S + 2 * RADIUS * DILATIONS[-1]
    nh = len(hosted)
    plan = _WeightGather([b.shape for b in hosted]) if nh else None
    n_steps = Bl * NS

    def body(qh, kh, v_ref, slope_ref, *rest):
        o_ref, lse_ref = rest[nh:nh + 2]
        wouts = rest[nh + 2:2 * nh + 2]
        (qp, kp, vp, op, lp, onat, lnat, bias_ref, sbuf, pbuf, mbuf, lbuf, tmps) = rest[2 * nh + 2:2 * nh + 15]
        sems = rest[2 * nh + 15:]
        step = pl.program_id(0) * Bl + pl.program_id(1)
        if nh:
            @pl.when(step == 0)
            def _():
                plan.start(wouts, sems)

            @pl.when(step == (7 * n_steps) // 8)
            def _():
                plan.forward(wouts, sems)

        lane_lo = lax.broadcasted_iota(jnp.int32, (QBLK, LANES), 1) < HEAD_DIM

        @pl.when(pl.program_id(1) == 0)
        def _():
            _bias_tables(bias_ref, slope_ref)

        for pi, d in enumerate(DILATIONS):
            n = S // d
            nb = n // QBLK
            _gather_rows(qh, qp, S, d, pad=False, f32_copy=tmps.at[0])
            _gather_rows(kh, kp, S, d, pad=True, f32_copy=tmps.at[1])
            _gather_rows(v_ref, vp, S, d, pad=True, f32_copy=tmps.at[2])

            def offsets(i, nb=nb):
                r = i // nb
                return pl.multiple_of(i * QBLK, QBLK), pl.multiple_of((i + r) * QBLK, QBLK), i % nb

            def scores(i, pi=pi, nb=nb):
                q0, k0, qb = offsets(i)
                qs = _stack_heads(qp[pl.ds(q0, QBLK), :], lane_lo)
                sbuf[i % PIPE_SLOTS] = (_dot_nt(qs, kp[pl.ds(k0, 2 * QBLK), :])
                                        + bias_ref[N_EDGE * pi + _edge_index(qb, nb)])

            def rowmax(i):
                m = jnp.max(sbuf[i % PIPE_SLOTS], axis=1, keepdims=True)
                mbuf[i % PIPE_SLOTS] = jnp.broadcast_to(m, (2 * QBLK, LANES))

            def expsum(i):
                m = mbuf[i % PIPE_SLOTS]
                p = jnp.exp2(sbuf[i % PIPE_SLOTS] - jnp.concatenate([m, m], axis=1))
                pbuf[i % PIPE_SLOTS] = p.astype(MXU_DTYPE)
                lbuf[i % PIPE_SLOTS] = jnp.broadcast_to(jnp.sum(p, axis=1, keepdims=True), (2 * QBLK, LANES))

            def values(i):
                q0, k0, _ = offsets(i)
                l = lbuf[i % PIPE_SLOTS]
                o2 = _dot(pbuf[i % PIPE_SLOTS], vp[pl.ds(k0, 2 * QBLK), :]) * (1.0 / l)
                op[pl.ds(q0, QBLK), :] = _merge_heads(o2, lane_lo)
                lp[pl.ds(q0, QBLK), :] = _merge_heads(mbuf[i % PIPE_SLOTS] + jnp.log2(l), lane_lo)

            _pipeline(NB, [scores, rowmax, expsum, values], PIPE_UNROLL)
            _scatter_rows(op, onat.at[pi], S, d, pad=False, accumulate=False, f32_tmp=tmps.at[0])
            _scatter_rows(lp, lnat.at[pi], S, d, pad=False, accumulate=False, f32_tmp=tmps.at[1])

        for c0 in range(0, S, ROWCH):
            ls = [lnat[pi, c0:c0 + ROWCH, :] for pi in range(len(DILATIONS))]
            mx = jnp.maximum(jnp.maximum(ls[0], ls[1]), ls[2])
            es = [jnp.exp2(l - mx) for l in ls]
            tot = es[0] + es[1] + es[2]
            inv = 1.0 / tot
            acc = (es[0] * inv) * onat[0, c0:c0 + ROWCH, :]
            for pi in (1, 2):
                acc = acc + (es[pi] * inv) * onat[pi, c0:c0 + ROWCH, :]
            o_ref[c0:c0 + ROWCH, :] = acc
            lse_ref[c0:c0 + ROWCH, :] = mx + jnp.log2(tot)

        if nh:
            @pl.when(step == n_steps - 1)
            def _():
                plan.finish(wouts, sems)

    spec_in = lambda off: pl.BlockSpec((None, S, LANES), lambda j, b: (off * NS + j, b, 0))
    out = pl.BlockSpec((S, LANES), lambda j, b: (b, j))
    anyspec = pl.BlockSpec(memory_space=pl.ANY)
    return pl.pallas_call(
        body, grid=(NS, Bl), name="attn_fwd",
        in_specs=[spec_in(0), spec_in(1), spec_in(2),
                  pl.BlockSpec((None, 8, LANES), lambda j, b: (j, 0, 0))] + [anyspec] * nh,
        out_specs=[out, out] + [anyspec] * nh,
        out_shape=[jax.ShapeDtypeStruct((T, NS * LANES), F32)] * 2
                  + [jax.ShapeDtypeStruct(b.shape, b.dtype) for b in hosted],
        input_output_aliases={4 + w: 2 + w for w in range(nh)},
        scratch_shapes=[pltpu.VMEM((S, LANES), MXU_DTYPE), pltpu.VMEM((PADR, LANES), MXU_DTYPE),
                        pltpu.VMEM((PADR, LANES), MXU_DTYPE),
                        pltpu.VMEM((S, LANES), F32), pltpu.VMEM((S, LANES), F32),
                        pltpu.VMEM((3, S, LANES), F32), pltpu.VMEM((3, S, LANES), F32),
                        pltpu.VMEM((N_EDGE * len(DILATIONS), 2 * QBLK, 2 * QBLK), F32),
                        pltpu.VMEM((PIPE_SLOTS, 2 * QBLK, 2 * QBLK), F32),
                        pltpu.VMEM((PIPE_SLOTS, 2 * QBLK, 2 * QBLK), MXU_DTYPE),
                        pltpu.VMEM((PIPE_SLOTS, 2 * QBLK, LANES), F32), pltpu.VMEM((PIPE_SLOTS, 2 * QBLK, LANES), F32),
                        pltpu.VMEM((3, S, LANES), F32)]
                       + (plan.scratch() if nh else []),
        compiler_params=_cp(("arbitrary", "arbitrary")),
    )(qkh, qkh, qkv, slopes, *hosted)


def _attn_bwd(qkh, qkv, o, lse, do, gq2, gk2, slopes, *, Bl, S, hosted=()):
    n3, T, _ = qkv.shape
    NS = n3 // 3
    NB = S // QBLK
    PADR = S + 2 * RADIUS * DILATIONS[-1]
    QSCALE = HEAD_DIM ** -0.5
    nh = len(hosted)
    plan = _ChipExchange(nh)
    n_steps = Bl * NS

    def body(qh, kh, q_ref, k_ref, v_ref, o_ref, lse_ref, do_ref, gq_ref, gk_ref, slope_ref, *rest):
        hin = rest[:nh]
        dq_ref, dk_ref, dv_ref, gacc_ref = rest[nh:nh + 4]
        hout = rest[nh + 4:2 * nh + 4]
        (ld, qp, kp, vp, dop, ldp, dqp, dkacc, dvacc, dqn, dkn, bias_ref,
         sbuf, dpbuf, pbuf, dsbuf, tmps) = rest[2 * nh + 4:2 * nh + 21]
        sems = rest[2 * nh + 21:]
        step = pl.program_id(0) * Bl + pl.program_id(1)

        @pl.when(step == 0)
        def _():
            gacc_ref[...] = jnp.zeros_like(gacc_ref)
            if nh:
                plan.start(hin, hout, sems)

        mm = _head_mean_matrix()
        lane_lo = lax.broadcasted_iota(jnp.int32, (QBLK, LANES), 1) < HEAD_DIM

        @pl.when(pl.program_id(1) == 0)
        def _():
            _bias_tables(bias_ref, slope_ref)

        lse_lanes = lax.broadcasted_iota(jnp.int32, (ROWCH, LANES), 1) % HEAD_DIM < HEAD_DIM // 2
        for c0 in range(0, S, ROWCH):
            delta = _head_mean(do_ref[c0:c0 + ROWCH, :] * o_ref[c0:c0 + ROWCH, :], mm) * HEAD_DIM
            ld[c0:c0 + ROWCH, :] = jnp.where(lse_lanes, lse_ref[c0:c0 + ROWCH, :], delta)
            dqn[c0:c0 + ROWCH, :] = jnp.zeros((ROWCH, LANES), F32)
            dkn[c0:c0 + ROWCH, :] = jnp.zeros((ROWCH, LANES), F32)
            dv_ref[c0:c0 + ROWCH, :] = jnp.zeros((ROWCH, LANES), F32)

        for pi, d in enumerate(DILATIONS):
            n = S // d
            nb = n // QBLK
            _gather_rows(qh, qp, S, d, pad=False, f32_copy=tmps.at[0])
            _gather_rows(kh, kp, S, d, pad=True, f32_copy=tmps.at[1])
            _gather_rows(v_ref, vp, S, d, pad=True, f32_copy=tmps.at[2])
            _gather_rows(do_ref, dop, S, d, pad=False, f32_copy=tmps.at[3])
            _gather_rows(ld, ldp, S, d, pad=False, f32_copy=tmps.at[4])
            _zero_uncovered(dkacc, S, d)
            _zero_uncovered(dvacc, S, d)

            def offsets(i, nb=nb):
                r = i // nb
                return pl.multiple_of(i * QBLK, QBLK), pl.multiple_of((i + r) * QBLK, QBLK), i % nb

            def scores(i, pi=pi, nb=nb):
                q0, k0, qb = offsets(i)
                qs = _stack_heads(qp[pl.ds(q0, QBLK), :], lane_lo)
                dos = _stack_heads(dop[pl.ds(q0, QBLK), :], lane_lo)
                sbuf[i % BWD_SLOTS] = (_dot_nt(qs, kp[pl.ds(k0, 2 * QBLK), :])
                                       + bias_ref[N_EDGE * pi + _edge_index(qb, nb)])
                dpbuf[i % BWD_SLOTS] = _dot_nt(dos, vp[pl.ds(k0, 2 * QBLK), :])

            def probs(i):
                q0, _, _ = offsets(i)
                blk = ldp[pl.ds(q0, QBLK), :]
                half = HEAD_DIM // 2
                lcol = jnp.concatenate([blk[:, 0:1], blk[:, HEAD_DIM:HEAD_DIM + 1]], axis=0)
                dcol = jnp.concatenate([blk[:, half:half + 1], blk[:, HEAD_DIM + half:HEAD_DIM + half + 1]], axis=0)
                p = jnp.exp2(sbuf[i % BWD_SLOTS] - lcol)
                pbuf[i % BWD_SLOTS] = p.astype(MXU_DTYPE)
                dsbuf[i % BWD_SLOTS] = (p * (dpbuf[i % BWD_SLOTS] - dcol)).astype(MXU_DTYPE)

            def grads(i):
                q0, k0, _ = offsets(i)
                qs = _stack_heads(qp[pl.ds(q0, QBLK), :], lane_lo)
                dos = _stack_heads(dop[pl.ds(q0, QBLK), :], lane_lo)
                ds = dsbuf[i % BWD_SLOTS]
                dvacc[i % 2, pl.ds(k0, 2 * QBLK), :] = _dot_tn(pbuf[i % BWD_SLOTS], dos)
                dkacc[i % 2, pl.ds(k0, 2 * QBLK), :] = _dot_tn(ds, qs)
                dqp[pl.ds(q0, QBLK), :] = _merge_heads(_dot(ds, kp[pl.ds(k0, 2 * QBLK), :]), lane_lo)

            _pipeline(NB, [scores, probs, grads], PIPE_UNROLL)
            _scatter_rows(dqp, dqn, S, d, pad=False, accumulate=True, f32_tmp=tmps.at[0])
            _scatter_parity(dkacc, dkn, S, d, f32_tmp=tmps.at[1])
            _scatter_parity(dvacc, dv_ref, S, d, f32_tmp=tmps.at[2])

        gq_sum = jnp.zeros((8, LANES), F32)
        gk_sum = jnp.zeros((8, LANES), F32)
        for c0 in range(0, S, ROWCH):
            for src_ref, dn, g_ref, dst_ref, scale, is_q in ((q_ref, dqn, gq_ref, dq_ref, QSCALE, True),
                                                             (k_ref, dkn, gk_ref, dk_ref, LN2, False)):
                x = src_ref[c0:c0 + ROWCH, :]
                dh = dn[c0:c0 + ROWCH, :]
                rr = lax.rsqrt(_head_mean(x * x, mm) + EPS)
                e = dh * (g_ref[...] * scale)
                dst_ref[c0:c0 + ROWCH, :] = rr * e - x * (rr * rr * rr) * _head_mean(e * x, mm)
                gpart = dh * (x * rr * scale)
                acc8 = gpart[0:8, :]
                for q8 in range(1, ROWCH // 8):
                    acc8 = acc8 + gpart[8 * q8:8 * q8 + 8, :]
                if is_q:
                    gq_sum = gq_sum + acc8
                else:
                    gk_sum = gk_sum + acc8
        gacc_ref[0:1, :] += jnp.sum(gq_sum, axis=0, keepdims=True)
        gacc_ref[1:2, :] += jnp.sum(gk_sum, axis=0, keepdims=True)

        if nh:
            @pl.when(step == n_steps - 1)
            def _():
                plan.finish(hin, hout, sems)

    spec_in = lambda off: pl.BlockSpec((None, S, LANES), lambda j, b: (off * NS + j, b, 0))
    tok = pl.BlockSpec((S, LANES), lambda j, b: (b, j))
    vec = pl.BlockSpec((1, LANES), lambda j, b: (0, 0))
    slab_out = pl.BlockSpec((None, S, LANES), lambda j, b: (j, b, 0))
    f32buf = lambda rows: pltpu.VMEM((rows, LANES), F32)
    bfbuf = lambda rows: pltpu.VMEM((rows, LANES), MXU_DTYPE)
    anyspec = pl.BlockSpec(memory_space=pl.ANY)
    return pl.pallas_call(
        body, grid=(NS, Bl), name="attn_bwd",
        in_specs=[spec_in(0), spec_in(1), spec_in(0), spec_in(1), spec_in(2), tok, tok, tok, vec, vec,
                  pl.BlockSpec((None, 8, LANES), lambda j, b: (j, 0, 0))] + [anyspec] * nh,
        out_specs=[slab_out, slab_out, slab_out, pl.BlockSpec((8, LANES), lambda j, b: (0, 0))] + [anyspec] * nh,
        out_shape=[jax.ShapeDtypeStruct((NS, T, LANES), F32)] * 3 + [jax.ShapeDtypeStruct((8, LANES), F32)]
                  + [jax.ShapeDtypeStruct((3,) + h.shape[1:], h.dtype) for h in hosted],
        scratch_shapes=[f32buf(S),
                        bfbuf(S), bfbuf(PADR), bfbuf(PADR), bfbuf(S),
                        f32buf(S), f32buf(S),
                        pltpu.VMEM((2, PADR, LANES), F32), pltpu.VMEM((2, PADR, LANES), F32),
                        f32buf(S), f32buf(S),
                        pltpu.VMEM((N_EDGE * len(DILATIONS), 2 * QBLK, 2 * QBLK), F32),
                        pltpu.VMEM((BWD_SLOTS, 2 * QBLK, 2 * QBLK), F32),
                        pltpu.VMEM((BWD_SLOTS, 2 * QBLK, 2 * QBLK), F32),
                        pltpu.VMEM((BWD_SLOTS, 2 * QBLK, 2 * QBLK), MXU_DTYPE),
                        pltpu.VMEM((BWD_SLOTS, 2 * QBLK, 2 * QBLK), MXU_DTYPE),
                        pltpu.VMEM((5, S, LANES), F32)]
                       + (plan.scratch() if nh else []),
        compiler_params=_cp(("arbitrary", "arbitrary"), vmem=ATTN_BWD_VMEM),
    )(qkh, qkh, qkv, qkv, qkv, o, lse, do, gq2, gk2, slopes, *hosted)


def _layer_norm_parts(cv, g_ln, b_ln):
    mu = jnp.mean(cv, axis=-1, keepdims=True)
    cen = cv - mu
    rs = lax.rsqrt(jnp.mean(cen * cen, axis=-1, keepdims=True) + EPS)
    z = cen * rs
    return z, rs, z * g_ln + b_ln


def _ffn_fwd(x2, cv, ya, tgt, mod, g_ln, b_ln, g_ffn, w_out, w_gate, w_up, w_down, *, S, tm):
    T, D = x2.shape
    DC = cv.shape[1]
    P, Kb, _ = w_out.shape
    Fb = w_down.shape[1]
    tps = S // tm

    def body(x_ref, cv_ref, ya_ref, t_ref, mod_ref, gln_ref, bln_ref, gf_ref, wo_hbm, wg_hbm, wu_hbm, wd_hbm,
             x1_ref, ycat_ref, mix_ref, h2_ref, g_ref, u_ref, a_ref, f_ref, dy_ref, loss_ref,
             wo, wg, wu, wd, sems):
        i = pl.program_id(0)
        _load_resident(i, [(wo_hbm, wo), (wg_hbm, wg), (wu_hbm, wu), (wd_hbm, wd)], sems)

        @pl.when(i == 0)
        def _():
            loss_ref[...] = jnp.zeros_like(loss_ref)

        _, _, ln = _layer_norm_parts(cv_ref[...], gln_ref[...], bln_ref[...])
        yc = ln * _sigmoid(ln)
        ycat = jnp.concatenate([yc, ya_ref[...]], axis=1).astype(MXU_DTYPE)
        ycat_ref[...] = ycat
        mix = _dot(ycat[:, 0:Kb], wo[0])
        for p in range(1, P):
            mix = mix + _dot(ycat[:, Kb * p:Kb * (p + 1)], wo[p])
        mix_ref[...] = mix.astype(ACT_DTYPE)
        x1 = x_ref[...] + mod_ref[:, 2 * D:3 * D] * mix
        x1_ref[...] = x1
        r2 = lax.rsqrt(jnp.mean(x1 * x1, axis=-1, keepdims=True) + EPS)
        h2 = (x1 * r2 * gf_ref[...]) * (1.0 + mod_ref[:, 4 * D:5 * D]) + mod_ref[:, 3 * D:4 * D]
        h2b = h2.astype(MXU_DTYPE)
        h2_ref[...] = h2b
        f = jnp.zeros((tm, D), F32)
        for p in range(P):
            g = _dot_nt(h2b, wg[p])
            u = _dot_nt(h2b, wu[p])
            a = (g * _sigmoid(g) * u).astype(MXU_DTYPE)
            g_ref[p] = g.astype(ACT_DTYPE)
            u_ref[p] = u.astype(ACT_DTYPE)
            a_ref[p] = a
            f = f + _dot(a, wd[p])
        f_ref[...] = f.astype(ACT_DTYPE)
        err = x1 + mod_ref[:, 5 * D:6 * D] * f - t_ref[...]
        dy_ref[...] = err * (1.0 / D)
        tot = jnp.sum(_colsum(err * err), axis=1, keepdims=True)
        loss_ref[...] += tot * (0.5 / D)

    row = lambda w: pl.BlockSpec((tm, w), lambda i: (i, 0))
    vec = lambda w: pl.BlockSpec((1, w), lambda i: (0, 0))
    blk = pl.BlockSpec((P, tm, Fb), lambda i: (0, i, 0))
    anyspec = pl.BlockSpec(memory_space=pl.ANY)
    return pl.pallas_call(
        body, grid=(T // tm,), name="ffn_fwd",
        in_specs=[row(D), row(DC), row(D - DC), row(D),
                  pl.BlockSpec((None, 1, N_MOD * D), lambda i: (i // tps, 0, 0)),
                  vec(DC), vec(DC), vec(D), anyspec, anyspec, anyspec, anyspec],
        out_specs=[row(D), row(D), row(D), row(D), blk, blk, blk, row(D), row(D),
                   pl.BlockSpec((8, LANES), lambda i: (0, 0))],
        out_shape=[jax.ShapeDtypeStruct((T, D), F32), jax.ShapeDtypeStruct((T, D), MXU_DTYPE),
                   jax.ShapeDtypeStruct((T, D), ACT_DTYPE), jax.ShapeDtypeStruct((T, D), MXU_DTYPE),
                   jax.ShapeDtypeStruct((P, T, Fb), ACT_DTYPE), jax.ShapeDtypeStruct((P, T, Fb), ACT_DTYPE),
                   jax.ShapeDtypeStruct((P, T, Fb), MXU_DTYPE), jax.ShapeDtypeStruct((T, D), ACT_DTYPE),
                   jax.ShapeDtypeStruct((T, D), F32), jax.ShapeDtypeStruct((8, LANES), F32)],
        scratch_shapes=[pltpu.VMEM(w_out.shape, w_out.dtype), pltpu.VMEM(w_gate.shape, w_gate.dtype),
                        pltpu.VMEM(w_up.shape, w_up.dtype), pltpu.VMEM(w_down.shape, w_down.dtype),
                        pltpu.SemaphoreType.DMA((4,))],
        compiler_params=_cp(("arbitrary",)),
    )(x2, cv, ya, tgt, mod, g_ln, b_ln, g_ffn, w_out, w_gate, w_up, w_down)


def _ffn_bwd(dy, x1, gs, us, fo, mixb, cv, mod, g_ln, b_ln, g_ffn, w_out, w_gate, w_up, w_down, *, S, tm):
    T, D = dy.shape
    DC = cv.shape[1]
    P, Kb, _ = w_out.shape
    Fb = w_down.shape[1]
    tps = S // tm
    Bl = T // S

    def body(dy_ref, x1_ref, g_ref, u_ref, f_ref, mix_ref, cv_ref, mod_ref, gln_ref, bln_ref, gf_ref,
             wo_hbm, wg_hbm, wu_hbm, wd_hbm,
             dg_ref, du_ref, df_ref, dx1_ref, dmix_ref, dya_ref, dcv_ref, macc_ref, gacc_ref, lacc_ref,
             wo, wg, wu, wd, sems):
        i = pl.program_id(0)
        _load_resident(i, [(wo_hbm, wo), (wg_hbm, wg), (wu_hbm, wu), (wd_hbm, wd)], sems)

        @pl.when(i == 0)
        def _():
            gacc_ref[...] = jnp.zeros_like(gacc_ref)
            lacc_ref[...] = jnp.zeros_like(lacc_ref)

        @pl.when(i % tps == 0)
        def _():
            macc_ref[...] = jnp.zeros_like(macc_ref)

        dy_t = dy_ref[...]
        x1 = x1_ref[...]
        gate_f = mod_ref[:, 5 * D:6 * D]
        macc_ref[2:3, :] += _colsum(dy_t * f_ref[...].astype(F32))
        dfb = (dy_t * gate_f).astype(MXU_DTYPE)
        df_ref[...] = dfb
        dh2 = jnp.zeros((tm, D), F32)
        for p in range(P):
            da = _dot_nt(dfb, wd[p])
            g = g_ref[p].astype(F32)
            u = u_ref[p].astype(F32)
            sg = _sigmoid(g)
            dgp = (da * u * (sg * (1.0 + g * (1.0 - sg)))).astype(MXU_DTYPE)
            dup = (da * (g * sg)).astype(MXU_DTYPE)
            dg_ref[p] = dgp
            du_ref[p] = dup
            dh2 = dh2 + _dot(dgp, wg[p]) + _dot(dup, wu[p])
        r2 = lax.rsqrt(jnp.mean(x1 * x1, axis=-1, keepdims=True) + EPS)
        xr = x1 * r2
        n2 = xr * gf_ref[...]
        macc_ref[0:1, :] += _colsum(dh2)
        macc_ref[1:2, :] += _colsum(dh2 * n2)
        dn2 = dh2 * (1.0 + mod_ref[:, 4 * D:5 * D])
        gacc_ref[0:1, :] += _colsum(dn2 * xr)
        e = dn2 * gf_ref[...]
        dx1 = dy_t + r2 * e - xr * (r2 * jnp.mean(e * xr, axis=-1, keepdims=True))
        dx1_ref[...] = dx1
        macc_ref[3:4, :] += _colsum(dx1 * mix_ref[...].astype(F32))
        dmixb = (dx1 * mod_ref[:, 2 * D:3 * D]).astype(MXU_DTYPE)
        dmix_ref[...] = dmixb
        parts = [_dot_nt(dmixb, wo[p]) for p in range(P)]
        dycat = jnp.concatenate(parts, axis=1) if P > 1 else parts[0]
        dya_ref[...] = dycat[:, DC:]
        dyc = dycat[:, :DC]
        z, rs, ln = _layer_norm_parts(cv_ref[...], gln_ref[...], bln_ref[...])
        sg = _sigmoid(ln)
        dln = dyc * (sg * (1.0 + ln * (1.0 - sg)))
        lacc_ref[0:1, :] += _colsum(dln * z)
        lacc_ref[1:2, :] += _colsum(dln)
        dz = dln * gln_ref[...]
        dcv_ref[...] = rs * (dz - jnp.mean(dz, axis=-1, keepdims=True) - z * jnp.mean(dz * z, axis=-1, keepdims=True))

    row = lambda w: pl.BlockSpec((tm, w), lambda i: (i, 0))
    vec = lambda w: pl.BlockSpec((1, w), lambda i: (0, 0))
    blk = pl.BlockSpec((P, tm, Fb), lambda i: (0, i, 0))
    anyspec = pl.BlockSpec(memory_space=pl.ANY)
    return pl.pallas_call(
        body, grid=(T // tm,), name="ffn_bwd",
        in_specs=[row(D), row(D), blk, blk, row(D), row(D), row(DC),
                  pl.BlockSpec((None, 1, N_MOD * D), lambda i: (i // tps, 0, 0)),
                  vec(DC), vec(DC), vec(D), anyspec, anyspec, anyspec, anyspec],
        out_specs=[blk, blk, row(D), row(D), row(D), row(D - DC), row(DC),
                   pl.BlockSpec((None, 8, D), lambda i: (i // tps, 0, 0)),
                   pl.BlockSpec((8, D), lambda i: (0, 0)), pl.BlockSpec((8, DC), lambda i: (0, 0))],
        out_shape=[jax.ShapeDtypeStruct((P, T, Fb), MXU_DTYPE), jax.ShapeDtypeStruct((P, T, Fb), MXU_DTYPE),
                   jax.ShapeDtypeStruct((T, D), MXU_DTYPE), jax.ShapeDtypeStruct((T, D), F32),
                   jax.ShapeDtypeStruct((T, D), MXU_DTYPE), jax.ShapeDtypeStruct((T, D - DC), F32),
                   jax.ShapeDtypeStruct((T, DC), F32), jax.ShapeDtypeStruct((Bl, 8, D), F32),
                   jax.ShapeDtypeStruct((8, D), F32), jax.ShapeDtypeStruct((8, DC), F32)],
        scratch_shapes=[pltpu.VMEM(w_out.shape, w_out.dtype), pltpu.VMEM(w_gate.shape, w_gate.dtype),
                        pltpu.VMEM(w_up.shape, w_up.dtype), pltpu.VMEM(w_down.shape, w_down.dtype),
                        pltpu.SemaphoreType.DMA((4,))],
        compiler_params=_cp(("arbitrary",)),
    )(dy, x1, gs, us, fo, mixb, cv, mod, g_ln, b_ln, g_ffn, w_out, w_gate, w_up, w_down)


def _in_bwd(da, dg, dq, dk, dv, x2, dx1, mod, g_mix, w_in, *, S, tm):
    T, D = x2.shape
    P, _, Nb = w_in.shape
    DC = da.shape[1]
    NS = dq.shape[0]
    n_in = P * Nb
    tps = S // tm
    Bl = T // S

    def body(da_ref, dg_ref, dq_ref, dk_ref, dv_ref, x_ref, dx1_ref, mod_ref, g_ref, w_ref,
             dx_ref, dproj_ref, macc_ref, gacc_ref):
        i = pl.program_id(0)

        @pl.when(i == 0)
        def _():
            gacc_ref[...] = jnp.zeros_like(gacc_ref)

        @pl.when(i % tps == 0)
        def _():
            macc_ref[...] = jnp.zeros_like(macc_ref)

        pieces = [da_ref[...], dg_ref[...]] + [r[j] for r in (dq_ref, dk_ref, dv_ref) for j in range(NS)]
        dproj = jnp.concatenate(pieces, axis=1).astype(MXU_DTYPE)
        dproj_ref[...] = dproj
        dh = _dot_nt(dproj[:, 0:Nb], w_ref[0])
        for p in range(1, P):
            dh = dh + _dot_nt(dproj[:, Nb * p:Nb * (p + 1)], w_ref[p])
        x = x_ref[...]
        r = lax.rsqrt(jnp.mean(x * x, axis=-1, keepdims=True) + EPS)
        xr = x * r
        macc_ref[0:1, :] += _colsum(dh)
        macc_ref[1:2, :] += _colsum(dh * (xr * g_ref[...]))
        dn = dh * (1.0 + mod_ref[:, D:2 * D])
        gacc_ref[0:1, :] += _colsum(dn * xr)
        e = dn * g_ref[...]
        dx_ref[...] = dx1_ref[...] + r * e - xr * (r * jnp.mean(e * xr, axis=-1, keepdims=True))

    row = lambda w: pl.BlockSpec((tm, w), lambda i: (i, 0))
    slab = pl.BlockSpec((NS, tm, LANES), lambda i: (0, i, 0))
    return pl.pallas_call(
        body, grid=(T // tm,), name="in_bwd",
        in_specs=[row(DC), row(DC), slab, slab, slab, row(D), row(D),
                  pl.BlockSpec((None, 1, N_MOD * D), lambda i: (i // tps, 0, 0)),
                  pl.BlockSpec((1, D), lambda i: (0, 0)),
                  pl.BlockSpec((P, D, Nb), lambda i: (0, 0, 0))],
        out_specs=[row(D), row(n_in), pl.BlockSpec((None, 8, D), lambda i: (i // tps, 0, 0)),
                   pl.BlockSpec((8, D), lambda i: (0, 0))],
        out_shape=[jax.ShapeDtypeStruct((T, D), F32), jax.ShapeDtypeStruct((T, n_in), MXU_DTYPE),
                   jax.ShapeDtypeStruct((Bl, 8, D), F32), jax.ShapeDtypeStruct((8, D), F32)],
        compiler_params=_cp(("arbitrary",)),
    )(da, dg, dq, dk, dv, x2, dx1, mod, g_mix, w_in)


def _wgrad(a, b, *, P, name, tk, split=None, host=None):
    a_blk, b_blk = a.ndim == 3, b.ndim == 3
    plan, h_in, h_out = host if host is not None else (None, (), ())
    ni, no = len(h_in), len(h_out)
    T = a.shape[-2]
    if a_blk:
        R, C = a.shape[2], b.shape[1]

        def accumulate(a_ref, b_ref, o_ref):
            for p in range(P):
                o_ref[p] += _dot_tn(a_ref[p], b_ref[...])
    elif b_blk:
        R, C = a.shape[1], b.shape[2]

        def accumulate(a_ref, b_ref, o_ref):
            for p in range(P):
                o_ref[p] += _dot_tn(a_ref[...], b_ref[p])
    elif split == "a":
        R, C = a.shape[1] // P, b.shape[1]

        def accumulate(a_ref, b_ref, o_ref):
            full = _dot_tn(a_ref[...], b_ref[...])
            for p in range(P):
                o_ref[p] += full[R * p:R * (p + 1)]
    else:
        R, C = a.shape[1], b.shape[1] // P
        per = 1 if C % MXU_COLS == 0 else 2
        assert P % per == 0 and (per * C) % MXU_COLS == 0

        def accumulate(a_ref, b_ref, o_ref):
            for p0 in range(0, P, per):
                full = _dot_tn(a_ref[...], b_ref[:, C * p0:C * (p0 + per)])
                for j in range(per):
                    o_ref[p0 + j] += full[:, C * j:C * (j + 1)]

    n_steps = T // tk

    def body(a_ref, b_ref, *rest):
        hin, o_ref, hout, sems = rest[:ni], rest[ni], rest[ni + 1:ni + 1 + no], rest[ni + 1 + no:]
        step = pl.program_id(0)

        @pl.when(step == 0)
        def _():
            o_ref[...] = jnp.zeros_like(o_ref)
            if plan is not None:
                plan.start(hin, hout, sems)

        if plan is not None:
            @pl.when(step == n_steps // 2)
            def _():
                plan.forward(hin, hout, sems)

        accumulate(a_ref, b_ref, o_ref)

        if plan is not None:
            @pl.when(step == n_steps - 1)
            def _():
                plan.finish(hin, hout, sems)

    def spec(v):
        if v.ndim == 3:
            return pl.BlockSpec((P, tk, v.shape[2]), lambda k: (0, k, 0))
        return pl.BlockSpec((tk, v.shape[1]), lambda k: (k, 0))

    anyspec = pl.BlockSpec(memory_space=pl.ANY)
    res = pl.pallas_call(
        body, grid=(n_steps,), name=name,
        in_specs=[spec(a), spec(b)] + [anyspec] * ni,
        out_specs=[pl.BlockSpec((P, R, C), lambda k: (0, 0, 0))] + [anyspec] * no,
        out_shape=[jax.ShapeDtypeStruct((P, R, C), F32)] + list(h_out),
        scratch_shapes=plan.scratch() if plan is not None else [],
        compiler_params=_cp(("arbitrary",)),
    )(a, b, *h_in)
    return res if plan is not None else res[0]


TM_IN = 512
TM_FFN = 256
TK_WGRAD = 1024


def _alibi_slabs(n_slab):
    heads = 2 * n_slab
    slopes = 2.0 ** (-8.0 * np.arange(1, heads + 1) / heads)
    return jnp.asarray(np.broadcast_to(np.repeat(slopes.reshape(n_slab, 1, 2), HEAD_DIM, axis=2), (n_slab, 8, LANES)),
                       dtype=F32)


def _local_step(x, tgt, mod, g_mix, wdw, g_ln, b_ln, g_q, g_k, g_ffn, w_in, w_out, w_gate, w_up, w_down,
                pc_idx=None):
    Bl, S, D = x.shape
    T = Bl * S
    DC = g_ln.shape[1]
    P = w_in.shape[0]
    n_slab = (D - DC) // LANES
    x2 = x.reshape(T, D)
    t2 = tgt.reshape(T, D)
    mod3 = mod.reshape(Bl, 1, N_MOD * D)
    gq2 = jnp.tile(g_q, (1, LANES // HEAD_DIM))
    gk2 = jnp.tile(g_k, (1, LANES // HEAD_DIM))
    slopes = _alibi_slabs(n_slab)

    ag, qkv, qkh, h1 = _fwd_in(x2, mod3, g_mix, gq2, gk2, w_in, S=S, tm=TM_IN, n_ag=2 * DC)
    cv = _conv_fwd(ag, wdw, Bl=Bl, S=S, DC=DC)
    if pc_idx is not None:
        ya, lse, w_out, w_gate, w_up, w_down = _attn_fwd(qkh, qkv, slopes, Bl=Bl, S=S,
                                                         hosted=(w_out, w_gate, w_up, w_down))
    else:
        ya, lse = _attn_fwd(qkh, qkv, slopes, Bl=Bl, S=S)
    x1, ycat, mixb, h2, gs, us, acts, fo, dy, lossb = _ffn_fwd(
        x2, cv, ya, t2, mod3, g_ln, b_ln, g_ffn, w_out, w_gate, w_up, w_down, S=S, tm=TM_FFN)
    dgs, dus, dfb, dx1, dmixb, dya, dcv, macc_f, gacc_f, lacc = _ffn_bwd(
        dy, x1, gs, us, fo, mixb, cv, mod3, g_ln, b_ln, g_ffn, w_out, w_gate, w_up, w_down, S=S, tm=TM_FFN)
    wg = functools.partial(_wgrad, P=P, tk=TK_WGRAD)
    out = {}
    if pc_idx is None:
        grads = dict(w_down=wg(acts, dfb, name="wgrad_down"), w_gate=wg(dgs, h2, name="wgrad_gate"),
                     w_up=wg(dus, h2, name="wgrad_up"), w_out=wg(ycat, dmixb, name="wgrad_out", split="a"))
        dq, dk, dv, gqk = _attn_bwd(qkh, qkv, ya, lse, dya, gq2, gk2, slopes, Bl=Bl, S=S)
    else:
        g_down = wg(acts, dfb, name="wgrad_down")
        g_gate, r_down = wg(dgs, h2, name="wgrad_gate", host=_sibling_host([g_down]))
        g_up, r_gate = wg(dus, h2, name="wgrad_up", host=_sibling_host([g_gate]))
        g_out, r_up = wg(ycat, dmixb, name="wgrad_out", split="a", host=_sibling_host([g_up]))
        (r_out,) = _rs_sibling([g_out], "rs_sibling_out")
        grads = dict(w_down=g_down, w_gate=g_gate, w_up=g_up, w_out=g_out)
        sums = _pair_add([grads[nm] for nm in EARLY_WEIGHTS], [r_down, r_gate, r_up, r_out], pc_idx, "pair_add_early")
        res = _attn_bwd(qkh, qkv, ya, lse, dya, gq2, gk2, slopes, Bl=Bl, S=S, hosted=tuple(sb for _, sb in sums))
        dq, dk, dv, gqk = res[:4]
        out["early_sums"] = [s32 for s32, _ in sums]
        out["early_recv"] = list(res[4:])
    da, dg, dwdw = _conv_bwd(ag, dcv, wdw, Bl=Bl, S=S, DC=DC)
    dx, dprojb, macc_m, gacc_m = _in_bwd(da, dg, dq, dk, dv, x2, dx1, mod3, g_mix, w_in, S=S, tm=TM_IN)
    packed = _pack_small(macc_m, macc_f, gacc_m, gacc_f, lacc, gqk, dwdw, lossb)
    if pc_idx is None:
        grads["w_in"] = wg(h1, dprojb, name="wgrad_in", split="b")
    else:
        grads["w_in"], out["gathered_small"] = wg(h1, dprojb, name="wgrad_in", split="b",
                                                  host=_small_gather_host(packed))
    out.update(dx=dx.reshape(Bl, S, D), grads=grads, packed=packed)
    return out


EARLY_WEIGHTS = ("w_down", "w_gate", "w_up", "w_out")


def _small_layout(Bl):
    return 8 * Bl, 8 * Bl + 8, 8 * Bl + 8 + CONV_ROWS


def _pack_small(macc_m, macc_f, gacc_m, gacc_f, lacc, gqk, dwdw, lossb):
    Bl, _, D = macc_m.shape
    DC = lacc.shape[1]
    assert 2 * DC <= D
    SMALL_GAIN_ROW, SMALL_TAP_ROW, SMALL_ROWS = _small_layout(Bl)

    def body(mm_ref, mf_ref, gm_ref, gf_ref, la_ref, qk_ref, dw_ref, loss_ref, o_ref):
        o_ref[...] = jnp.zeros_like(o_ref)
        for b in range(Bl):
            o_ref[8 * b + 0:8 * b + 2, :] = mm_ref[b, 0:2, :]
            o_ref[8 * b + 2:8 * b + 3, :] = mf_ref[b, 3:4, :]
            o_ref[8 * b + 3:8 * b + 6, :] = mf_ref[b, 0:3, :]
        r = SMALL_GAIN_ROW
        o_ref[r:r + 1, :] = gm_ref[0:1, :]
        o_ref[r + 1:r + 2, :] = gf_ref[0:1, :]
        o_ref[r + 2:r + 3, 0:DC] = la_ref[0:1, :]
        o_ref[r + 2:r + 3, DC:2 * DC] = la_ref[1:2, :]
        qk = qk_ref[0:2, 0:HEAD_DIM] + qk_ref[0:2, HEAD_DIM:2 * HEAD_DIM]
        o_ref[r + 3:r + 4, 0:HEAD_DIM] = qk[0:1, :]
        o_ref[r + 3:r + 4, HEAD_DIM:2 * HEAD_DIM] = qk[1:2, :]
        o_ref[r + 4:r + 5, 0:LANES] = loss_ref[0:1, :]
        o_ref[SMALL_TAP_ROW:SMALL_TAP_ROW + CONV_ROWS, 0:DC] = dw_ref[...]

    return pl.pallas_call(body, name="pack_small", out_shape=jax.ShapeDtypeStruct((SMALL_ROWS, D), F32),
                          compiler_params=_cp())(macc_m, macc_f, gacc_m, gacc_f, lacc, gqk, dwdw, lossb)


def _row_tile(rows, cap=512):
    if rows <= cap:
        return rows
    best = rows
    for t in range(8, cap + 1, 8):
        if rows % t == 0:
            best = t
    return best


def _cast_weights(ws, pidx, name):
    n = len(ws)
    halves = [(w.shape[0] // 2, w.shape[1]) for w in ws]

    def body(p_ref, *refs):
        for k in range(n):
            refs[n + k][...] = refs[k][...].astype(MXU_DTYPE)

    return pl.pallas_call(
        body, name=name,
        grid_spec=pltpu.PrefetchScalarGridSpec(
            num_scalar_prefetch=1, grid=(2,),
            in_specs=[pl.BlockSpec(h, lambda i, p: (i, 0)) for h in halves],
            out_specs=[pl.BlockSpec((None,) + h, lambda i, p: (p[0], i, 0)) for h in halves]),
        out_shape=[jax.ShapeDtypeStruct((4,) + w.shape, MXU_DTYPE) for w in ws],
        compiler_params=_cp(),
    )(pidx, *ws)


def _pair_add(gs, recvs, pc_idx, name):
    n = len(gs)
    P = gs[0].shape[0]
    halves = [(g.shape[1] // 2, g.shape[2]) for g in gs]

    def body(pc_ref, *refs):
        for k in range(n):
            g_ref, r_ref, o_ref, ob_ref = refs[k], refs[n + k], refs[2 * n + 2 * k], refs[2 * n + 2 * k + 1]
            s = g_ref[...] + r_ref[...]
            ob_ref[...] = s.astype(jnp.bfloat16)

            @pl.when(pl.program_id(0) == pc_ref[0])
            def _(o_ref=o_ref, s=s):
                o_ref[...] = s

    res = pl.pallas_call(
        body, name=name,
        grid_spec=pltpu.PrefetchScalarGridSpec(
            num_scalar_prefetch=1, grid=(P,),
            in_specs=[pl.BlockSpec((None,) + h, lambda p, pc: (p, pc[1], 0)) for h in halves]
                     + [pl.BlockSpec((None,) + h, lambda p, pc: (p, 0, 0)) for h in halves],
            out_specs=[spec for h in halves for spec in (pl.BlockSpec(h, lambda p, pc: (0, 0)),
                                                         pl.BlockSpec((None,) + h, lambda p, pc: (p, 0, 0)))]),
        out_shape=[shape for h in halves for shape in (jax.ShapeDtypeStruct(h, F32),
                                                       jax.ShapeDtypeStruct((P,) + h, jnp.bfloat16))],
        compiler_params=_cp(),
    )(pc_idx, *gs, *recvs)
    return [(res[2 * k], res[2 * k + 1]) for k in range(n)]


def _final_add(owns, recvs, pc_idx, name):
    n = len(owns)

    def body(pc_ref, *refs):
        for k in range(n):
            acc = refs[k][...]
            for j in range(3):
                acc = acc + refs[n + k][j].astype(F32)
            refs[2 * n + k][...] = acc

    return pl.pallas_call(
        body, name=name,
        grid_spec=pltpu.PrefetchScalarGridSpec(
            num_scalar_prefetch=1, grid=(1,),
            in_specs=[pl.BlockSpec(o.shape, lambda i, pc: (0, 0)) for o in owns]
                     + [pl.BlockSpec((3,) + o.shape, lambda i, pc: (0, 0, 0)) for o in owns],
            out_specs=[pl.BlockSpec(o.shape, lambda i, pc: (pc[1], 0)) for o in owns]),
        out_shape=[jax.ShapeDtypeStruct((2 * o.shape[0], o.shape[1]), F32) for o in owns],
        compiler_params=_cp(),
    )(pc_idx, *owns, *recvs)


def _adamw_update(w_ref, g_ref, m_ref, v_ref, d_ref, nm_ref, nv_ref):
    c1 = 1.0 - ADAM_B1 ** ADAM_STEP
    c2 = 1.0 - ADAM_B2 ** ADAM_STEP
    gg = g_ref[...]
    nm = ADAM_B1 * m_ref[...] + (1.0 - ADAM_B1) * gg
    nv = ADAM_B2 * v_ref[...] + (1.0 - ADAM_B2) * (gg * gg)
    nm_ref[...] = nm
    nv_ref[...] = nv
    d_ref[...] = -ADAM_LR * ((nm / c1) / (jnp.sqrt(nv / c2) + ADAM_EPS) + ADAM_WD * w_ref[...])


def _adamw(w, g, m, v, name):
    R, C = w.shape
    tr = _row_tile(R, 256)
    spec = pl.BlockSpec((tr, C), lambda i: (i, 0))
    return pl.pallas_call(
        functools.partial(_adamw_update), grid=(R // tr,), name=name,
        in_specs=[spec] * 4, out_specs=[spec] * 3,
        out_shape=[jax.ShapeDtypeStruct((R, C), F32)] * 3,
    )(w, g, m, v)


def _adamw_small(quads, name):
    n = len(quads)

    def body(*refs):
        for k in range(n):
            _adamw_update(*refs[4 * k:4 * k + 4], *refs[4 * n + 3 * k:4 * n + 3 * k + 3])

    whole = lambda a: pl.BlockSpec(a.shape, lambda i: (0, 0))
    res = pl.pallas_call(
        body, grid=(1,), name=name,
        in_specs=[whole(a) for q in quads for a in q],
        out_specs=[whole(q[0]) for q in quads for _ in range(3)],
        out_shape=[jax.ShapeDtypeStruct(q[0].shape, F32) for q in quads for _ in range(3)],
    )(*[a for q in quads for a in q])
    return [tuple(res[3 * k:3 * k + 3]) for k in range(n)]


def _startup(first, w_ada, b_cols, w_in_buf, *, Bl):
    rows, D = first.shape
    NA = w_ada.shape[1]
    n_dev = 8
    g_w = _WeightGather([w_in_buf.shape])
    g_c = _SmallGather(rows)
    g_m = _SmallGather(n_dev * Bl)

    def body(first_ref, wada_ref, b_ref, win_in, g0_ref, call_ref, gm_ref, win_out, modp,
             ws0, ws1, cs0, cs1, cs2, ms0, ms1, ms2):
        g_w.start([win_out], (ws0, ws1))
        for phase in (g_c.start, g_c.forward, g_c.finish):
            phase([first_ref], [g0_ref], (cs0, cs1, cs2))
        for d in range(n_dev):
            call_ref[Bl * d:Bl * (d + 1), :] = g0_ref[rows * d:rows * d + Bl, :]
        c = call_ref[...]
        modp[...] = jnp.dot(c * _sigmoid(c), wada_ref[...], preferred_element_type=F32,
                            precision=lax.Precision.HIGH) + b_ref[...]
        for phase in (g_m.start, g_m.forward, g_m.finish):
            phase([modp], [gm_ref], (ms0, ms1, ms2))
        g_w.forward([win_out], (ws0, ws1))
        g_w.finish([win_out], (ws0, ws1))

    vmem = pl.BlockSpec(memory_space=pltpu.VMEM)
    anyspec = pl.BlockSpec(memory_space=pl.ANY)
    return pl.pallas_call(
        body, name="startup",
        in_specs=[vmem, vmem, vmem, anyspec], out_specs=[vmem, vmem, vmem, anyspec],
        out_shape=[jax.ShapeDtypeStruct((n_dev * rows, D), F32), jax.ShapeDtypeStruct((n_dev * Bl, D), F32),
                   jax.ShapeDtypeStruct((n_dev * n_dev * Bl, NA), F32),
                   jax.ShapeDtypeStruct(w_in_buf.shape, w_in_buf.dtype)],
        input_output_aliases={3: 3},
        scratch_shapes=[pltpu.VMEM((n_dev * Bl, NA), F32)] + g_w.scratch() + g_c.scratch() + g_m.scratch(),
        compiler_params=_cp(),
    )(first, w_ada, b_cols, w_in_buf)


def _ada_bwd(c_all, dmod_cols):
    def body(c_ref, d_ref, o_ref):
        c = c_ref[...]
        o_ref[...] = _dot_tn((c * _sigmoid(c)).astype(MXU_DTYPE), d_ref[...].astype(MXU_DTYPE))
    return pl.pallas_call(
        body, name="ada_bwd", out_shape=jax.ShapeDtypeStruct((c_all.shape[1], dmod_cols.shape[1]), F32),
        compiler_params=_cp(),
    )(c_all, dmod_cols)


def _small_reduce(gathered, n_dev, Bl):
    mod_rows, _, rows = _small_layout(Bl)
    width = gathered.shape[1]

    def body(g_ref, red_ref, bada_ref):
        acc = g_ref[0:rows, :]
        for d in range(1, n_dev):
            acc = acc + g_ref[d * rows:(d + 1) * rows, :]
        red_ref[...] = acc[mod_rows:, :]
        b = acc[0:8, :]
        for q in range(1, Bl):
            b = b + acc[8 * q:8 * q + 8, :]
        bada_ref[...] = b
    return pl.pallas_call(
        body, name="small_reduce",
        out_shape=[jax.ShapeDtypeStruct((rows - mod_rows, width), F32), jax.ShapeDtypeStruct((8, width), F32)],
        compiler_params=_cp(),
    )(gathered)


def _mesh_pos():
    return lax.axis_index("x"), lax.axis_index("y"), lax.axis_index("c")


def _other_chips(x, y):
    return [(1 - x, y), (x, 1 - y), (1 - x, 1 - y)]


class _WeightGather:
    def __init__(self, shapes):
        self.shapes = shapes
        self.n = len(shapes)

    def scratch(self):
        return [pltpu.SemaphoreType.DMA((6 * self.n,)), pltpu.SemaphoreType.DMA((6 * self.n,))]

    def _copy(self, outs, sems, w, k, slot, h, to):
        r2 = self.shapes[w][1] // 2
        blk = outs[w].at[slot, pl.ds(h * r2, r2), :]
        return pltpu.make_async_remote_copy(
            src_ref=blk, dst_ref=blk, send_sem=sems[0].at[6 * w + k], recv_sem=sems[1].at[6 * w + k],
            device_id=to, device_id_type=MESH_DEV)

    def start(self, outs, sems):
        x, y, c = _mesh_pos()
        for w in range(self.n):
            for k, chip in enumerate(_other_chips(x, y)):
                self._copy(outs, sems, w, k, 2 * x + y, c, (*chip, c)).start()

    def forward(self, outs, sems):
        x, y, c = _mesh_pos()
        for w in range(self.n):
            for k, chip in enumerate(_other_chips(x, y)):
                slot = 2 * chip[0] + chip[1]
                self._copy(outs, sems, w, k, slot, c, (x, y, 1 - c)).wait_recv()
                self._copy(outs, sems, w, 3 + k, slot, c, (x, y, 1 - c)).start()

    def finish(self, outs, sems):
        x, y, c = _mesh_pos()
        for w in range(self.n):
            for k, chip in enumerate(_other_chips(x, y)):
                slot = 2 * chip[0] + chip[1]
                self._copy(outs, sems, w, 3 + k, slot, 1 - c, (x, y, 1 - c)).wait_recv()
                self._copy(outs, sems, w, k, 2 * x + y, c, (*chip, c)).wait_send()
                self._copy(outs, sems, w, 3 + k, slot, c, (x, y, 1 - c)).wait_send()


class _SiblingExchange:
    def __init__(self, shapes):
        self.shapes = shapes

    def scratch(self):
        n = sum(s[0] for s in self.shapes)
        return [pltpu.SemaphoreType.DMA((n,)), pltpu.SemaphoreType.DMA((n,))]

    def out_shapes(self, dtype):
        return [jax.ShapeDtypeStruct((s[0], s[1] // 2, s[2]), dtype) for s in self.shapes]

    def _copies(self, ins, outs, sems):
        x, y, c = _mesh_pos()
        cps, k = [], 0
        for w, (P, R, _) in enumerate(self.shapes):
            r2 = R // 2
            for p in range(P):
                cps.append(pltpu.make_async_remote_copy(
                    src_ref=ins[w].at[p, pl.ds((1 - c) * r2, r2), :], dst_ref=outs[w].at[p],
                    send_sem=sems[0].at[k], recv_sem=sems[1].at[k],
                    device_id=(x, y, 1 - c), device_id_type=MESH_DEV))
                k += 1
        return cps

    def start(self, ins, outs, sems):
        for cp in self._copies(ins, outs, sems):
            cp.start()

    def forward(self, ins, outs, sems):
        pass

    def finish(self, ins, outs, sems):
        for cp in self._copies(ins, outs, sems):
            cp.wait()


def _sibling_host(grads):
    plan = _SiblingExchange([g.shape for g in grads])
    return plan, tuple(grads), tuple(plan.out_shapes(grads[0].dtype))


def _rs_sibling(grads, name):
    n = len(grads)
    plan, _, out_shapes = _sibling_host(grads)

    def body(*refs):
        ins, outs, sems = refs[:n], refs[n:2 * n], refs[2 * n:]
        plan.start(ins, outs, sems)
        plan.finish(ins, outs, sems)

    anyspec = pl.BlockSpec(memory_space=pl.ANY)
    return pl.pallas_call(
        body, name=name, out_shape=list(out_shapes),
        in_specs=[anyspec] * n, out_specs=[anyspec] * n, scratch_shapes=plan.scratch(),
    )(*grads)


class _SmallGather:
    def __init__(self, m_per):
        self.m = m_per

    def scratch(self):
        return [pltpu.SemaphoreType.DMA((7,)), pltpu.SemaphoreType.DMA((7,)), pltpu.SemaphoreType.DMA]

    def _rows(self, out, pos):
        px, py, pc = pos
        return out.at[pl.ds((4 * px + 2 * py + pc) * self.m, self.m), :]

    def _copy(self, out, sems, k, block, to, src=None):
        dst = self._rows(out, block)
        return pltpu.make_async_remote_copy(
            src_ref=dst if src is None else src, dst_ref=dst, send_sem=sems[0].at[k], recv_sem=sems[1].at[k],
            device_id=to, device_id_type=MESH_DEV)

    def start(self, ins, outs, sems):
        x, y, c = _mesh_pos()
        me = (x, y, c)
        pltpu.make_async_copy(ins[0], self._rows(outs[0], me), sems[2]).start()
        self._copy(outs[0], sems, 0, me, (x, y, 1 - c), src=ins[0]).start()
        for j, chip in enumerate(_other_chips(x, y)):
            self._copy(outs[0], sems, 1 + j, me, (*chip, c), src=ins[0]).start()

    def forward(self, ins, outs, sems):
        x, y, c = _mesh_pos()
        for j, chip in enumerate(_other_chips(x, y)):
            self._copy(outs[0], sems, 1 + j, (*chip, c), (x, y, c)).wait_recv()
            self._copy(outs[0], sems, 4 + j, (*chip, c), (x, y, 1 - c)).start()

    def finish(self, ins, outs, sems):
        x, y, c = _mesh_pos()
        me = (x, y, c)
        self._copy(outs[0], sems, 0, (x, y, 1 - c), me).wait_recv()
        for j, chip in enumerate(_other_chips(x, y)):
            self._copy(outs[0], sems, 4 + j, (*chip, 1 - c), me).wait_recv()
        self._copy(outs[0], sems, 0, me, (x, y, 1 - c), src=ins[0]).wait_send()
        for j, chip in enumerate(_other_chips(x, y)):
            self._copy(outs[0], sems, 1 + j, me, (*chip, c), src=ins[0]).wait_send()
            self._copy(outs[0], sems, 4 + j, (*chip, c), (x, y, 1 - c)).wait_send()
        pltpu.make_async_copy(ins[0], self._rows(outs[0], me), sems[2]).wait()


def _small_gather_host(packed):
    m, n = packed.shape
    return _SmallGather(m), (packed,), (jax.ShapeDtypeStruct((8 * m, n), packed.dtype),)


class _ChipExchange:
    def __init__(self, n):
        self.n = n

    def scratch(self):
        return [pltpu.SemaphoreType.DMA((3 * self.n,)), pltpu.SemaphoreType.DMA((3 * self.n,))]

    def _copies(self, ins, outs, sems):
        x, y, c = _mesh_pos()
        return [pltpu.make_async_remote_copy(
            src_ref=ins[w].at[2 * chip[0] + chip[1]], dst_ref=outs[w].at[k],
            send_sem=sems[0].at[3 * w + k], recv_sem=sems[1].at[3 * w + k],
            device_id=(*chip, c), device_id_type=MESH_DEV)
            for w in range(self.n) for k, chip in enumerate(_other_chips(x, y))]

    def start(self, ins, outs, sems):
        for cp in self._copies(ins, outs, sems):
            cp.start()

    def forward(self, ins, outs, sems):
        pass

    def finish(self, ins, outs, sems):
        for cp in self._copies(ins, outs, sems):
            cp.wait()


def _rs_final(bufs, name, chips=()):
    n, nc = len(bufs), len(chips)
    plan = _ChipExchange(nc)

    def body(*refs):
        cin = refs[n:n + nc]
        outs = refs[n + nc:2 * n + nc]
        cout = refs[2 * n + nc:2 * n + 2 * nc]
        send_sems, recv_sems = refs[2 * n + 2 * nc:2 * n + 2 * nc + 2]
        csems = refs[2 * n + 2 * nc + 2:]
        x, y, c = _mesh_pos()
        if nc:
            plan.start(cin, cout, csems)
        cps = []
        for w in range(n):
            r2 = bufs[w].shape[0] // 2
            mine = outs[w].at[pl.ds(c * r2, r2), :]
            cps.append(pltpu.make_async_remote_copy(
                src_ref=mine, dst_ref=mine, send_sem=send_sems.at[w], recv_sem=recv_sems.at[w],
                device_id=(x, y, 1 - c), device_id_type=MESH_DEV))
            cps[-1].start()
        for cp in cps:
            cp.wait()
        if nc:
            plan.finish(cin, cout, csems)

    anyspec = pl.BlockSpec(memory_space=pl.ANY)
    return pl.pallas_call(
        body, name=name,
        out_shape=[jax.ShapeDtypeStruct(b.shape, b.dtype) for b in bufs]
                  + [jax.ShapeDtypeStruct((3,) + s.shape[1:], s.dtype) for s in chips],
        in_specs=[anyspec] * (n + nc), out_specs=[anyspec] * (n + nc),
        input_output_aliases={w: w for w in range(n)},
        scratch_shapes=[pltpu.SemaphoreType.DMA((n,)), pltpu.SemaphoreType.DMA((n,))] + (plan.scratch() if nc else []),
    )(*bufs, *chips)


BIG = ("w_in", "w_out", "w_gate", "w_up", "w_down")
TRANSPOSED = ("w_gate", "w_up")
WEIGHTS = ("w_ada", "b_ada", "g_mix", "w_in", "w_dw", "b_dw", "g_conv_ln", "b_conv_ln", "g_q", "g_k",
           "w_out", "g_ffn", "w_gate", "w_up", "w_down")


def _pad_to(a, rows, cols):
    return jnp.pad(a, ((0, rows - a.shape[0]), (0, cols - a.shape[1])))


def kernel(x, c, w_ada, b_ada, g_mix, w_in, w_dw, b_dw, g_conv_ln, b_conv_ln, g_q, g_k, w_out, g_ffn, w_gate, w_up, w_down, loss_target, m_w_ada, m_b_ada, m_g_mix, m_w_in, m_w_dw, m_b_dw, m_g_conv_ln, m_b_conv_ln, m_g_q, m_g_k, m_w_out, m_g_ffn, m_w_gate, m_w_up, m_w_down, v_w_ada, v_b_ada, v_g_mix, v_w_in, v_w_dw, v_b_dw, v_g_conv_ln, v_b_conv_ln, v_g_q, v_g_k, v_w_out, v_g_ffn, v_w_gate, v_w_up, v_w_down):
    w = dict(w_ada=w_ada, b_ada=b_ada, g_mix=g_mix, w_in=w_in, w_dw=w_dw, b_dw=b_dw, g_conv_ln=g_conv_ln,
             b_conv_ln=b_conv_ln, g_q=g_q, g_k=g_k, w_out=w_out, g_ffn=g_ffn, w_gate=w_gate, w_up=w_up, w_down=w_down)
    m = dict(w_ada=m_w_ada, b_ada=m_b_ada, g_mix=m_g_mix, w_in=m_w_in, w_dw=m_w_dw, b_dw=m_b_dw, g_conv_ln=m_g_conv_ln,
             b_conv_ln=m_b_conv_ln, g_q=m_g_q, g_k=m_g_k, w_out=m_w_out, g_ffn=m_g_ffn, w_gate=m_w_gate, w_up=m_w_up,
             w_down=m_w_down)
    v = dict(w_ada=v_w_ada, b_ada=v_b_ada, g_mix=v_g_mix, w_in=v_w_in, w_dw=v_w_dw, b_dw=v_b_dw, g_conv_ln=v_g_conv_ln,
             b_conv_ln=v_b_conv_ln, g_q=v_g_q, g_k=v_g_k, w_out=v_w_out, g_ffn=v_g_ffn, w_gate=v_w_gate, w_up=v_w_up,
             w_down=v_w_down)
    Bl, S, D = x.shape
    DC = g_conv_ln.shape[1]
    NA = w_ada.shape[2]
    xi, yi, ci = _mesh_pos()
    p = 2 * xi + yi
    dev = 2 * p + ci
    n_dev = 8
    pidx = jnp.reshape(p, (1,)).astype(jnp.int32)
    pc_idx = jnp.stack([p, ci]).astype(jnp.int32)

    first = jnp.concatenate([_pad_to(c, 8, D), _pad_to(w_dw[0], CONV_ROWS, D)], axis=0)
    shard = lambda a, nm: a[0].T if nm in TRANSPOSED else a[0]
    owned = dict(zip(BIG, _cast_weights([shard(w[nm], nm) for nm in BIG], pidx, "cast_weights")))
    b_cols = lax.dynamic_slice_in_dim(b_ada, p * NA, NA, axis=1)
    g0, c_all, gm, w_in_full = _startup(first, w_ada[0], b_cols, owned["w_in"], Bl=Bl)
    g0 = g0.reshape(n_dev, 8 + CONV_ROWS, D)
    taps = jnp.concatenate([g0[2 * q, 8:, :w_dw.shape[2]] for q in range(4)], axis=1)
    wdw = jnp.where(lax.broadcasted_iota(jnp.int32, taps.shape, 0) == CONV_WIDTH, b_dw, taps)
    gm = gm.reshape(n_dev, n_dev * Bl, NA)
    mod = jnp.concatenate([lax.dynamic_slice_in_dim(gm[2 * q], dev * Bl, Bl, axis=0) for q in range(4)], axis=1)

    loc = _local_step(x, loss_target, mod, g_mix, wdw, g_conv_ln, b_conv_ln, g_q, g_k, g_ffn,
                      w_in_full, owned["w_out"], owned["w_gate"], owned["w_up"], owned["w_down"], pc_idx=pc_idx)

    halves = _final_add(loc["early_sums"], loc["early_recv"], pc_idx, "final_add_early")
    (late_sib,) = _rs_sibling([loc["grads"]["w_in"]], "rs_sibling_in")
    ((late32, late16),) = _pair_add([loc["grads"]["w_in"]], [late_sib], pc_idx, "pair_add_w_in")
    *early_full, late_recv = _rs_final(halves, "rs_final_early", chips=(late16,))
    grad = dict(zip(EARLY_WEIGHTS, early_full))
    grad["w_in"], = _rs_final(_final_add([late32], [late_recv], pc_idx, "final_add_w_in"), "rs_final_in")

    mod_rows, _, small_rows = _small_layout(Bl)
    gs = loc["gathered_small"]
    red, bada8 = _small_reduce(gs, n_dev, Bl)
    dmod_all = gs.reshape(n_dev, small_rows, D)[:, :mod_rows].reshape(n_dev * Bl, 8, D)[:, :N_MOD].reshape(n_dev * Bl, N_MOD * D)
    grad["w_ada"] = _ada_bwd(c_all, lax.dynamic_slice_in_dim(dmod_all, p * NA, NA, axis=1))
    grad["b_ada"] = bada8[:N_MOD].reshape(1, N_MOD * D)
    grad["g_mix"] = red[0:1]
    grad["g_ffn"] = red[1:2]
    grad["g_conv_ln"] = red[2:3, :DC]
    grad["b_conv_ln"] = red[2:3, DC:2 * DC]
    grad["g_q"] = red[3:4, :HEAD_DIM]
    grad["g_k"] = red[3:4, HEAD_DIM:2 * HEAD_DIM]
    loss = red[4, 0]
    dwdw = red[8:8 + CONV_ROWS, :DC]
    grad["w_dw"] = lax.dynamic_slice_in_dim(dwdw[:CONV_WIDTH], p * w_dw.shape[2], w_dw.shape[2], axis=1)
    grad["b_dw"] = dwdw[CONV_WIDTH:CONV_WIDTH + 1]

    delta, new_m, new_v = {}, {}, {}
    two_d = lambda nm: w[nm].shape[-2:]
    small = [nm for nm in WEIGHTS if nm not in BIG and nm != "w_ada"]
    small_res = dict(zip(small, _adamw_small(
        [tuple(a.reshape(two_d(nm)) for a in (w[nm], grad[nm], m[nm], v[nm])) for nm in small], "adamw_small")))
    for nm in WEIGHTS:
        shp = w[nm].shape
        if nm in TRANSPOSED:
            d_, m_, v_ = _adamw(w[nm][0].T, grad[nm], m[nm][0].T, v[nm][0].T, "adamw_" + nm)
            grad[nm], delta[nm], new_m[nm], new_v[nm] = (a.T.reshape(shp) for a in (grad[nm], d_, m_, v_))
            continue
        if nm in small_res:
            d_, m_, v_ = small_res[nm]
        else:
            d_, m_, v_ = _adamw(*(a.reshape(two_d(nm)) for a in (w[nm], grad[nm], m[nm], v[nm])), "adamw_" + nm)
        grad[nm] = grad[nm].reshape(shp)
        delta[nm], new_m[nm], new_v[nm] = d_.reshape(shp), m_.reshape(shp), v_.reshape(shp)

    return (loss, loc["dx"], *[grad[nm] for nm in WEIGHTS], *[delta[nm] for nm in WEIGHTS],
            *[new_m[nm] for nm in WEIGHTS], *[new_v[nm] for nm in WEIGHTS])
```

```python
import functools

import jax
import jax.numpy as jnp
import numpy as np
from jax import lax
from jax.experimental import pallas as pl
from jax.experimental.pallas import tpu as pltpu

F32 = jnp.float32
MXU_DTYPE = jnp.bfloat16
ACT_DTYPE = jnp.bfloat16
EPS = 1e-6
NEG_INF = -1e30
HEAD_DIM = 64
LANES = 128
MXU_COLS = 256
RADIUS = 64
QBLK = 128
DILATIONS = (1, 4, 16)
CONV_WIDTH = 31
CONV_PAD = CONV_WIDTH // 2
CONV_ROWS = 32
N_MOD = 6
ADAM_LR, ADAM_B1, ADAM_B2, ADAM_EPS, ADAM_WD, ADAM_STEP = 0.001, 0.9, 0.999, 1e-08, 0.01, 10
MESH_DEV = pl.DeviceIdType.MESH
VMEM_LIMIT = 56 << 20
ATTN_BWD_VMEM = 60 << 20


def _cp(sem=None, vmem=VMEM_LIMIT):
    kw = dict(vmem_limit_bytes=vmem)
    if sem is not None:
        kw["dimension_semantics"] = sem
    return pltpu.CompilerParams(**kw)


def _sigmoid(x):
    return 1.0 / (1.0 + jnp.exp(-x))


def _dot(a, b):
    return jnp.dot(a, b, preferred_element_type=F32)


def _dot_nt(a, b):
    return lax.dot_general(a, b, (((1,), (1,)), ((), ())), preferred_element_type=F32)


def _dot_tn(a, b):
    return lax.dot_general(a, b, (((0,), (0,)), ((), ())), preferred_element_type=F32)


def _colsum(v):
    return jnp.sum(v, axis=0, keepdims=True)


def _load_resident(i, pairs, sems):
    @pl.when(i == 0)
    def _():
        cps = [pltpu.make_async_copy(src, dst, sems.at[n]) for n, (src, dst) in enumerate(pairs)]
        for c in cps:
            c.start()
        for c in cps:
            c.wait()


def _join_owner_blocks(w_ref, w_full):
    P, _, Nb = w_ref.shape

    @pl.when(pl.program_id(0) == 0)
    def _():
        for p in range(P):
            w_full[:, Nb * p:Nb * (p + 1)] = w_ref[p]


def _fwd_in(x2, mod, g_mix, gq2, gk2, w_in, *, S, tm, n_ag):
    T, D = x2.shape
    P, _, Nb = w_in.shape
    n_in = P * Nb
    n_slab = (n_in - n_ag) // LANES
    NS = n_slab // 3
    tps = S // tm

    def body(x_ref, mod_ref, g_ref, gq_ref, gk_ref, w_ref, ag_ref, qkv_ref, qkh_ref, h_ref, w_full):
        _join_owner_blocks(w_ref, w_full)
        x = x_ref[...]
        r = lax.rsqrt(jnp.mean(x * x, axis=-1, keepdims=True) + EPS)
        n = x * r * g_ref[...]
        h = n * (1.0 + mod_ref[:, D:2 * D]) + mod_ref[:, 0:D]
        hb = h.astype(MXU_DTYPE)
        h_ref[...] = hb
        proj = _dot(hb, w_full[...])
        ag_ref[...] = proj[:, :n_ag]
        mm = _head_mean_matrix()
        for j in range(n_slab):
            v = proj[:, n_ag + LANES * j:n_ag + LANES * (j + 1)]
            qkv_ref[j] = v
            if j < 2 * NS:
                gain = gq_ref[...] * (HEAD_DIM ** -0.5 * LOG2E) if j < NS else gk_ref[...]
                qkh_ref[j] = v * lax.rsqrt(_head_mean(v * v, mm) + EPS) * gain

    return pl.pallas_call(
        body, grid=(T // tm,), name="fwd_in",
        in_specs=[pl.BlockSpec((tm, D), lambda i: (i, 0)),
                  pl.BlockSpec((None, 1, N_MOD * D), lambda i: (i // tps, 0, 0)),
                  pl.BlockSpec((1, D), lambda i: (0, 0)),
                  pl.BlockSpec((1, LANES), lambda i: (0, 0)), pl.BlockSpec((1, LANES), lambda i: (0, 0)),
                  pl.BlockSpec((P, D, Nb), lambda i: (0, 0, 0))],
        out_specs=[pl.BlockSpec((tm, n_ag), lambda i: (i, 0)),
                   pl.BlockSpec((n_slab, tm, LANES), lambda i: (0, i, 0)),
                   pl.BlockSpec((2 * NS, tm, LANES), lambda i: (0, i, 0)),
                   pl.BlockSpec((tm, D), lambda i: (i, 0))],
        out_shape=[jax.ShapeDtypeStruct((T, n_ag), F32),
                   jax.ShapeDtypeStruct((n_slab, T, LANES), F32),
                   jax.ShapeDtypeStruct((2 * NS, T, LANES), F32),
                   jax.ShapeDtypeStruct((T, D), MXU_DTYPE)],
        scratch_shapes=[pltpu.VMEM((D, n_in), w_in.dtype)],
        compiler_params=_cp(("arbitrary",)),
    )(x2, mod, g_mix, gq2, gk2, w_in)


CONV_CH = 128


def _conv_taps(win, w_ref, acc, reverse):
    n = win.shape[0]
    for b in range(8):
        wb = win if b == 0 else pltpu.roll(win, shift=n - b, axis=0)
        for a in range(4):
            o = 8 * a + b
            if o < 1 or o > CONV_WIDTH:
                continue
            k = (CONV_WIDTH - o) if reverse else (o - 1)
            acc = acc + w_ref[k:k + 1, :] * wb[8 * a:8 * a + CONV_CH, :]
    return acc


def _conv_fwd(ag, wdw, *, Bl, S, DC):
    T = ag.shape[0]
    nsc = DC // LANES
    CH = CONV_CH

    def body(a_ref, g_ref, w_ref, cv_ref, upad):
        zeros16 = jnp.zeros((16, LANES), F32)
        upad[0:16, :] = zeros16
        upad[S + 16:S + 32, :] = zeros16

        def fill(i, _):
            r0 = pl.multiple_of(i * CH, CH)
            a = a_ref[pl.ds(r0, CH), :]
            g = g_ref[pl.ds(r0, CH), :]
            upad[pl.ds(r0 + 16, CH), :] = a * _sigmoid(g)
            return 0
        lax.fori_loop(0, S // CH, fill, 0)

        def conv(i, _):
            r0 = pl.multiple_of(i * CH, CH)
            win = upad[pl.ds(r0, CH + 32), :]
            acc = jnp.zeros((CH, LANES), F32) + w_ref[CONV_WIDTH:CONV_WIDTH + 1, :]
            cv_ref[pl.ds(r0, CH), :] = _conv_taps(win, w_ref, acc, reverse=False)
            return 0
        lax.fori_loop(0, S // CH, conv, 0)

    return pl.pallas_call(
        body, grid=(Bl, nsc), name="conv_fwd",
        in_specs=[pl.BlockSpec((S, LANES), lambda b, j: (b, j)),
                  pl.BlockSpec((S, LANES), lambda b, j: (b, nsc + j)),
                  pl.BlockSpec((CONV_ROWS, LANES), lambda b, j: (0, j))],
        out_specs=pl.BlockSpec((S, LANES), lambda b, j: (b, j)),
        out_shape=jax.ShapeDtypeStruct((T, DC), F32),
        scratch_shapes=[pltpu.VMEM((S + 32, LANES), F32)],
        compiler_params=_cp(("arbitrary", "arbitrary")),
    )(ag, ag, wdw)


def _conv_bwd(ag, dcv, wdw, *, Bl, S, DC):
    T = ag.shape[0]
    nsc = DC // LANES
    CH = CONV_CH

    def body(a_ref, g_ref, d_ref, w_ref, da_ref, dg_ref, dw_ref, upad, dpad, wacc):
        b = pl.program_id(1)
        zeros16 = jnp.zeros((16, LANES), F32)
        upad[0:16, :] = zeros16
        upad[S + 16:S + 32, :] = zeros16
        dpad[0:16, :] = zeros16
        dpad[S + 16:S + 32, :] = zeros16

        @pl.when(b == 0)
        def _():
            wacc[...] = jnp.zeros_like(wacc)

        def fill(i, _):
            r0 = pl.multiple_of(i * CH, CH)
            a = a_ref[pl.ds(r0, CH), :]
            g = g_ref[pl.ds(r0, CH), :]
            upad[pl.ds(r0 + 16, CH), :] = a * _sigmoid(g)
            dpad[pl.ds(r0 + 16, CH), :] = d_ref[pl.ds(r0, CH), :]
            return 0
        lax.fori_loop(0, S // CH, fill, 0)

        def step(i, _):
            r0 = pl.multiple_of(i * CH, CH)
            dwin = dpad[pl.ds(r0, CH + 32), :]
            du = _conv_taps(dwin, w_ref, jnp.zeros((CH, LANES), F32), reverse=True)
            a = a_ref[pl.ds(r0, CH), :]
            g = g_ref[pl.ds(r0, CH), :]
            sg = _sigmoid(g)
            da_ref[pl.ds(r0, CH), :] = du * sg
            dg_ref[pl.ds(r0, CH), :] = du * a * sg * (1.0 - sg)
            dc = d_ref[pl.ds(r0, CH), :]
            uwin = upad[pl.ds(r0, CH + 32), :]
            n = CH + 32
            for bb in range(8):
                wb = uwin if bb == 0 else pltpu.roll(uwin, shift=n - bb, axis=0)
                for aa in range(4):
                    o = 8 * aa + bb
                    if o < 1 or o > CONV_WIDTH:
                        continue
                    k = o - 1
                    prod = dc * wb[8 * aa:8 * aa + CH, :]
                    part = prod[0:8, :]
                    for q in range(1, CH // 8):
                        part = part + prod[8 * q:8 * q + 8, :]
                    wacc[8 * k:8 * k + 8, :] += part
            part = dc[0:8, :]
            for q in range(1, CH // 8):
                part = part + dc[8 * q:8 * q + 8, :]
            wacc[8 * CONV_WIDTH:8 * CONV_WIDTH + 8, :] += part
            return 0
        lax.fori_loop(0, S // CH, step, 0)

        @pl.when(b == Bl - 1)
        def _():
            for k in range(CONV_ROWS):
                dw_ref[k:k + 1, :] = jnp.sum(wacc[8 * k:8 * k + 8, :], axis=0, keepdims=True)

    return pl.pallas_call(
        body, grid=(nsc, Bl), name="conv_bwd",
        in_specs=[pl.BlockSpec((S, LANES), lambda j, b: (b, j)),
                  pl.BlockSpec((S, LANES), lambda j, b: (b, nsc + j)),
                  pl.BlockSpec((S, LANES), lambda j, b: (b, j)),
                  pl.BlockSpec((CONV_ROWS, LANES), lambda j, b: (0, j))],
        out_specs=[pl.BlockSpec((S, LANES), lambda j, b: (b, j)),
                   pl.BlockSpec((S, LANES), lambda j, b: (b, j)),
                   pl.BlockSpec((CONV_ROWS, LANES), lambda j, b: (0, j))],
        out_shape=[jax.ShapeDtypeStruct((T, DC), F32), jax.ShapeDtypeStruct((T, DC), F32),
                   jax.ShapeDtypeStruct((CONV_ROWS, DC), F32)],
        scratch_shapes=[pltpu.VMEM((S + 32, LANES), F32), pltpu.VMEM((S + 32, LANES), F32),
                        pltpu.VMEM((8 * CONV_ROWS, LANES), F32)],
        compiler_params=_cp(("arbitrary", "arbitrary")),
    )(ag, ag, dcv, wdw)


ROWCH = 256


LOG2E = 1.4426950408889634
LN2 = 0.6931471805599453
N_EDGE = 4


def _head_mean_matrix():
    r = lax.broadcasted_iota(jnp.int32, (LANES, LANES), 0) // HEAD_DIM
    c = lax.broadcasted_iota(jnp.int32, (LANES, LANES), 1) // HEAD_DIM
    return jnp.where(r == c, 1.0 / HEAD_DIM, 0.0).astype(jnp.bfloat16)


def _head_mean(v, mm):
    hi = v.astype(jnp.bfloat16)
    lo = (v - hi.astype(F32)).astype(jnp.bfloat16)
    return _dot(hi, mm) + _dot(lo, mm)


def _stack_heads(blk, lane_lo):
    z = jnp.zeros_like(blk)
    return jnp.concatenate([jnp.where(lane_lo, blk, z), jnp.where(lane_lo, z, blk)], axis=0)


def _merge_heads(v2, lane_lo):
    return jnp.where(lane_lo, v2[:QBLK], v2[QBLK:])


def _bias_tables(bias_ref, slope_ref):
    row = lax.broadcasted_iota(jnp.int32, (2 * QBLK, 2 * QBLK), 0)
    col = lax.broadcasted_iota(jnp.int32, (2 * QBLK, 2 * QBLK), 1)
    rel = jnp.abs(col - RADIUS - (row % QBLK))
    slope = jnp.where(row < QBLK, slope_ref[0:1, 0:1], slope_ref[0:1, HEAD_DIM:HEAD_DIM + 1]) * LOG2E
    for pi, d in enumerate(DILATIONS):
        inside = jnp.where(rel <= RADIUS, -slope * (float(d) * rel.astype(F32)), NEG_INF)
        for e in range(N_EDGE):
            t = inside
            if e & 1:
                t = jnp.where(col < RADIUS, NEG_INF, t)
            if e & 2:
                t = jnp.where(col >= QBLK + RADIUS, NEG_INF, t)
            bias_ref[N_EDGE * pi + e] = t


def _edge_index(qb, nb):
    return jnp.where(qb == 0, 1, 0) + jnp.where(qb == nb - 1, 2, 0)


VIA = 4


def _residue(d, s):
    return (s % VIA) * VIA + s // VIA if d == VIA * VIA else s


def _gather_rows(src_ref, dst_ref, S, d, pad, f32_copy=None):
    n = S // d
    seg = n + 2 * RADIUS if pad else n
    step = min(n, 512)
    two_step = d == VIA * VIA and f32_copy is not None
    for s in range(d):
        base = s * seg
        if pad:
            dst_ref[base:base + RADIUS, :] = jnp.zeros((RADIUS, LANES), dst_ref.dtype)
            dst_ref[base + RADIUS + n:base + seg, :] = jnp.zeros((RADIUS, LANES), dst_ref.dtype)
            base += RADIUS
        for c0 in range(0, n, step):
            if d == 1:
                v = src_ref[c0:c0 + step, :]
            elif two_step:
                v = f32_copy[pl.ds((s // VIA) * (S // VIA) + s % VIA + c0 * VIA, step, stride=VIA), :]
            else:
                v = src_ref[pl.ds(_residue(d, s) + c0 * d, step, stride=d), :]
                if d == VIA and f32_copy is not None:
                    f32_copy[s * n + c0:s * n + c0 + step, :] = v
            dst_ref[base + c0:base + c0 + step, :] = v.astype(dst_ref.dtype)


def _scatter_rows(src_ref, dst_ref, S, d, pad, accumulate, f32_tmp=None):
    n = S // d
    seg = n + 2 * RADIUS if pad else n
    first = RADIUS if pad else 0
    _unpermute(lambda s, c0, step: src_ref[s * seg + first + c0:s * seg + first + c0 + step, :],
               dst_ref, S, d, accumulate, f32_tmp)


def _unpermute(rows_of, dst_ref, S, d, accumulate, f32_tmp):
    n = S // d
    step = min(n, 512)
    if d == VIA * VIA and f32_tmp is not None:
        for s in range(d):
            f32_tmp[pl.ds((s // VIA) * (S // VIA) + s % VIA, n, stride=VIA), :] = rows_of(s, 0, n)
        n4 = S // VIA
        _unpermute(lambda s, c0, st: f32_tmp[s * n4 + c0:s * n4 + c0 + st, :], dst_ref, S, VIA, accumulate, None)
        return
    for s in range(d):
        for c0 in range(0, n, step):
            v = rows_of(s, c0, step)
            idx = pl.ds(c0, step) if d == 1 else pl.ds(_residue(d, s) + c0 * d, step, stride=d)
            if accumulate:
                dst_ref[idx, :] = dst_ref[idx, :] + v
            else:
                dst_ref[idx, :] = v


def _zero_uncovered(acc, S, d):
    n = S // d
    if (n // QBLK) % 2:
        return
    seg = n + 2 * RADIUS
    for r in range(d):
        acc[0, r * seg + n:r * seg + seg, :] = jnp.zeros((2 * RADIUS, LANES), F32)
        acc[1, r * seg:r * seg + 2 * RADIUS, :] = jnp.zeros((2 * RADIUS, LANES), F32)


def _scatter_parity(acc, dst_ref, S, d, f32_tmp=None):
    n = S // d
    seg = n + 2 * RADIUS
    one_block = (n // QBLK) % 2 == 1

    def rows_of(s, c0, step):
        rows = slice(s * seg + RADIUS + c0, s * seg + RADIUS + c0 + step)
        return acc[s % 2, rows, :] if one_block else acc[0, rows, :] + acc[1, rows, :]

    _unpermute(rows_of, dst_ref, S, d, True, f32_tmp)


PIPE_UNROLL = 4
PIPE_SLOTS = 16
BWD_SLOTS = 12


def _pipeline(n_items, stages, unroll):
    K = len(stages)
    assert n_items % unroll == 0 and K * unroll <= (PIPE_SLOTS if K == 4 else BWD_SLOTS)
    trips = n_items // unroll
    assert trips >= K - 1

    def trip(t, static):
        for s in reversed(range(K)):
            if static and not 0 <= t - s < trips:
                continue
            for u in range(unroll):
                item = unroll * (t - s) + u
                stages[s](jnp.int32(item) if static else item)

    for t in range(K - 1):
        trip(t, True)

    def full(t, carry):
        trip(t, False)
        return carry
    lax.fori_loop(K - 1, trips, full, 0)
    for t in range(trips, trips + K - 1):
        trip(t, True)


def _attn_fwd(qkh, qkv, slopes, *, Bl, S, hosted=()):
    n3, T, _ = qkv.shape
    NS = n3 // 3
    NB = S // QBLK
    PADR = S + 2 * RADIUS * DILATIONS[-1]
    nh = len(hosted)
    plan = _WeightGather([b.shape for b in hosted]) if nh else None
    n_steps = Bl * NS

    def body(qh, kh, v_ref, slope_ref, *rest):
        o_ref, lse_ref = rest[nh:nh + 2]
        wouts = rest[nh + 2:2 * nh + 2]
        (qp, kp, vp, op, lp, onat, lnat, bias_ref, sbuf, pbuf, mbuf, lbuf, tmps) = rest[2 * nh + 2:2 * nh + 15]
        sems = rest[2 * nh + 15:]
        step = pl.program_id(0) * Bl + pl.program_id(1)
        if nh:
            @pl.when(step == 0)
            def _():
                plan.start(wouts, sems)

            @pl.when(step == (7 * n_steps) // 8)
            def _():
                plan.forward(wouts, sems)

        lane_lo = lax.broadcasted_iota(jnp.int32, (QBLK, LANES), 1) < HEAD_DIM

        @pl.when(pl.program_id(1) == 0)
        def _():
            _bias_tables(bias_ref, slope_ref)

        for pi, d in enumerate(DILATIONS):
            n = S // d
            nb = n // QBLK
            _gather_rows(qh, qp, S, d, pad=False, f32_copy=tmps.at[0])
            _gather_rows(kh, kp, S, d, pad=True, f32_copy=tmps.at[1])
            _gather_rows(v_ref, vp, S, d, pad=True, f32_copy=tmps.at[2])

            def offsets(i, nb=nb):
                r = i // nb
                return pl.multiple_of(i * QBLK, QBLK), pl.multiple_of((i + r) * QBLK, QBLK), i % nb

            def scores(i, pi=pi, nb=nb):
                q0, k0, qb = offsets(i)
                qs = _stack_heads(qp[pl.ds(q0, QBLK), :], lane_lo)
                sbuf[i % PIPE_SLOTS] = (_dot_nt(qs, kp[pl.ds(k0, 2 * QBLK), :])
                                        + bias_ref[N_EDGE * pi + _edge_index(qb, nb)])

            def rowmax(i):
                m = jnp.max(sbuf[i % PIPE_SLOTS], axis=1, keepdims=True)
                mbuf[i % PIPE_SLOTS] = jnp.broadcast_to(m, (2 * QBLK, LANES))

            def expsum(i):
                m = mbuf[i % PIPE_SLOTS]
                p = jnp.exp2(sbuf[i % PIPE_SLOTS] - jnp.concatenate([m, m], axis=1))
                pbuf[i % PIPE_SLOTS] = p.astype(MXU_DTYPE)
                lbuf[i % PIPE_SLOTS] = jnp.broadcast_to(jnp.sum(p, axis=1, keepdims=True), (2 * QBLK, LANES))

            def values(i):
                q0, k0, _ = offsets(i)
                l = lbuf[i % PIPE_SLOTS]
                o2 = _dot(pbuf[i % PIPE_SLOTS], vp[pl.ds(k0, 2 * QBLK), :]) * (1.0 / l)
                op[pl.ds(q0, QBLK), :] = _merge_heads(o2, lane_lo)
                lp[pl.ds(q0, QBLK), :] = _merge_heads(mbuf[i % PIPE_SLOTS] + jnp.log2(l), lane_lo)

            _pipeline(NB, [scores, rowmax, expsum, values], PIPE_UNROLL)
            _scatter_rows(op, onat.at[pi], S, d, pad=False, accumulate=False, f32_tmp=tmps.at[0])
            _scatter_rows(lp, lnat.at[pi], S, d, pad=False, accumulate=False, f32_tmp=tmps.at[1])

        for c0 in range(0, S, ROWCH):
            ls = [lnat[pi, c0:c0 + ROWCH, :] for pi in range(len(DILATIONS))]
            mx = jnp.maximum(jnp.maximum(ls[0], ls[1]), ls[2])
            es = [jnp.exp2(l - mx) for l in ls]
            tot = es[0] + es[1] + es[2]
            inv = 1.0 / tot
            acc = (es[0] * inv) * onat[0, c0:c0 + ROWCH, :]
            for pi in (1, 2):
                acc = acc + (es[pi] * inv) * onat[pi, c0:c0 + ROWCH, :]
            o_ref[c0:c0 + ROWCH, :] = acc
            lse_ref[c0:c0 + ROWCH, :] = mx + jnp.log2(tot)

        if nh:
            @pl.when(step == n_steps - 1)
            def _():
                plan.finish(wouts, sems)

    spec_in = lambda off: pl.BlockSpec((None, S, LANES), lambda j, b: (off * NS + j, b, 0))
    out = pl.BlockSpec((S, LANES), lambda j, b: (b, j))
    anyspec = pl.BlockSpec(memory_space=pl.ANY)
    return pl.pallas_call(
        body, grid=(NS, Bl), name="attn_fwd",
        in_specs=[spec_in(0), spec_in(1), spec_in(2),
                  pl.BlockSpec((None, 8, LANES), lambda j, b: (j, 0, 0))] + [anyspec] * nh,
        out_specs=[out, out] + [anyspec] * nh,
        out_shape=[jax.ShapeDtypeStruct((T, NS * LANES), F32)] * 2
                  + [jax.ShapeDtypeStruct(b.shape, b.dtype) for b in hosted],
        input_output_aliases={4 + w: 2 + w for w in range(nh)},
        scratch_shapes=[pltpu.VMEM((S, LANES), MXU_DTYPE), pltpu.VMEM((PADR, LANES), MXU_DTYPE),
                        pltpu.VMEM((PADR, LANES), MXU_DTYPE),
                        pltpu.VMEM((S, LANES), F32), pltpu.VMEM((S, LANES), F32),
                        pltpu.VMEM((3, S, LANES), F32), pltpu.VMEM((3, S, LANES), F32),
                        pltpu.VMEM((N_EDGE * len(DILATIONS), 2 * QBLK, 2 * QBLK), F32),
                        pltpu.VMEM((PIPE_SLOTS, 2 * QBLK, 2 * QBLK), F32),
                        pltpu.VMEM((PIPE_SLOTS, 2 * QBLK, 2 * QBLK), MXU_DTYPE),
                        pltpu.VMEM((PIPE_SLOTS, 2 * QBLK, LANES), F32), pltpu.VMEM((PIPE_SLOTS, 2 * QBLK, LANES), F32),
                        pltpu.VMEM((3, S, LANES), F32)]
                       + (plan.scratch() if nh else []),
        compiler_params=_cp(("arbitrary", "arbitrary")),
    )(qkh, qkh, qkv, slopes, *hosted)


def _attn_bwd(qkh, qkv, o, lse, do, gq2, gk2, slopes, *, Bl, S, hosted=()):
    n3, T, _ = qkv.shape
    NS = n3 // 3
    NB = S // QBLK
    PADR = S + 2 * RADIUS * DILATIONS[-1]
    QSCALE = HEAD_DIM ** -0.5
    nh = len(hosted)
    plan = _ChipExchange(nh)
    n_steps = Bl * NS

    def body(qh, kh, q_ref, k_ref, v_ref, o_ref, lse_ref, do_ref, gq_ref, gk_ref, slope_ref, *rest):
        hin = rest[:nh]
        dq_ref, dk_ref, dv_ref, gacc_ref = rest[nh:nh + 4]
        hout = rest[nh + 4:2 * nh + 4]
        (ld, qp, kp, vp, dop, ldp, dqp, dkacc, dvacc, dqn, dkn, bias_ref,
         sbuf, dpbuf, pbuf, dsbuf, tmps) = rest[2 * nh + 4:2 * nh + 21]
        sems = rest[2 * nh + 21:]
        step = pl.program_id(0) * Bl + pl.program_id(1)

        @pl.when(step == 0)
        def _():
            gacc_ref[...] = jnp.zeros_like(gacc_ref)
            if nh:
                plan.start(hin, hout, sems)

        mm = _head_mean_matrix()
        lane_lo = lax.broadcasted_iota(jnp.int32, (QBLK, LANES), 1) < HEAD_DIM

        @pl.when(pl.program_id(1) == 0)
        def _():
            _bias_tables(bias_ref, slope_ref)

        lse_lanes = lax.broadcasted_iota(jnp.int32, (ROWCH, LANES), 1) % HEAD_DIM < HEAD_DIM // 2
        for c0 in range(0, S, ROWCH):
            delta = _head_mean(do_ref[c0:c0 + ROWCH, :] * o_ref[c0:c0 + ROWCH, :], mm) * HEAD_DIM
            ld[c0:c0 + ROWCH, :] = jnp.where(lse_lanes, lse_ref[c0:c0 + ROWCH, :], delta)
            dqn[c0:c0 + ROWCH, :] = jnp.zeros((ROWCH, LANES), F32)
            dkn[c0:c0 + ROWCH, :] = jnp.zeros((ROWCH, LANES), F32)
            dv_ref[c0:c0 + ROWCH, :] = jnp.zeros((ROWCH, LANES), F32)

        for pi, d in enumerate(DILATIONS):
            n = S // d
            nb = n // QBLK
            _gather_rows(qh, qp, S, d, pad=False, f32_copy=tmps.at[0])
            _gather_rows(kh, kp, S, d, pad=True, f32_copy=tmps.at[1])
            _gather_rows(v_ref, vp, S, d, pad=True, f32_copy=tmps.at[2])
            _gather_rows(do_ref, dop, S, d, pad=False, f32_copy=tmps.at[3])
            _gather_rows(ld, ldp, S, d, pad=False, f32_copy=tmps.at[4])
            _zero_uncovered(dkacc, S, d)
            _zero_uncovered(dvacc, S, d)

            def offsets(i, nb=nb):
                r = i // nb
                return pl.multiple_of(i * QBLK, QBLK), pl.multiple_of((i + r) * QBLK, QBLK), i % nb

            def scores(i, pi=pi, nb=nb):
                q0, k0, qb = offsets(i)
                qs = _stack_heads(qp[pl.ds(q0, QBLK), :], lane_lo)
                dos = _stack_heads(dop[pl.ds(q0, QBLK), :], lane_lo)
                sbuf[i % BWD_SLOTS] = (_dot_nt(qs, kp[pl.ds(k0, 2 * QBLK), :])
                                       + bias_ref[N_EDGE * pi + _edge_index(qb, nb)])
                dpbuf[i % BWD_SLOTS] = _dot_nt(dos, vp[pl.ds(k0, 2 * QBLK), :])

            def probs(i):
                q0, _, _ = offsets(i)
                blk = ldp[pl.ds(q0, QBLK), :]
                half = HEAD_DIM // 2
                lcol = jnp.concatenate([blk[:, 0:1], blk[:, HEAD_DIM:HEAD_DIM + 1]], axis=0)
                dcol = jnp.concatenate([blk[:, half:half + 1], blk[:, HEAD_DIM + half:HEAD_DIM + half + 1]], axis=0)
                p = jnp.exp2(sbuf[i % BWD_SLOTS] - lcol)
                pbuf[i % BWD_SLOTS] = p.astype(MXU_DTYPE)
                dsbuf[i % BWD_SLOTS] = (p * (dpbuf[i % BWD_SLOTS] - dcol)).astype(MXU_DTYPE)

            def grads(i):
                q0, k0, _ = offsets(i)
                qs = _stack_heads(qp[pl.ds(q0, QBLK), :], lane_lo)
                dos = _stack_heads(dop[pl.ds(q0, QBLK), :], lane_lo)
                ds = dsbuf[i % BWD_SLOTS]
                dvacc[i % 2, pl.ds(k0, 2 * QBLK), :] = _dot_tn(pbuf[i % BWD_SLOTS], dos)
                dkacc[i % 2, pl.ds(k0, 2 * QBLK), :] = _dot_tn(ds, qs)
                dqp[pl.ds(q0, QBLK), :] = _merge_heads(_dot(ds, kp[pl.ds(k0, 2 * QBLK), :]), lane_lo)

            _pipeline(NB, [scores, probs, grads], PIPE_UNROLL)
            _scatter_rows(dqp, dqn, S, d, pad=False, accumulate=True, f32_tmp=tmps.at[0])
            _scatter_parity(dkacc, dkn, S, d, f32_tmp=tmps.at[1])
            _scatter_parity(dvacc, dv_ref, S, d, f32_tmp=tmps.at[2])

        gq_sum = jnp.zeros((8, LANES), F32)
        gk_sum = jnp.zeros((8, LANES), F32)
        for c0 in range(0, S, ROWCH):
            for src_ref, dn, g_ref, dst_ref, scale, is_q in ((q_ref, dqn, gq_ref, dq_ref, QSCALE, True),
                                                             (k_ref, dkn, gk_ref, dk_ref, LN2, False)):
                x = src_ref[c0:c0 + ROWCH, :]
                dh = dn[c0:c0 + ROWCH, :]
                rr = lax.rsqrt(_head_mean(x * x, mm) + EPS)
                e = dh * (g_ref[...] * scale)
                dst_ref[c0:c0 + ROWCH, :] = rr * e - x * (rr * rr * rr) * _head_mean(e * x, mm)
                gpart = dh * (x * rr * scale)
                acc8 = gpart[0:8, :]
                for q8 in range(1, ROWCH // 8):
                    acc8 = acc8 + gpart[8 * q8:8 * q8 + 8, :]
                if is_q:
                    gq_sum = gq_sum + acc8
                else:
                    gk_sum = gk_sum + acc8
        gacc_ref[0:1, :] += jnp.sum(gq_sum, axis=0, keepdims=True)
        gacc_ref[1:2, :] += jnp.sum(gk_sum, axis=0, keepdims=True)

        if nh:
            @pl.when(step == n_steps - 1)
            def _():
                plan.finish(hin, hout, sems)

    spec_in = lambda off: pl.BlockSpec((None, S, LANES), lambda j, b: (off * NS + j, b, 0))
    tok = pl.BlockSpec((S, LANES), lambda j, b: (b, j))
    vec = pl.BlockSpec((1, LANES), lambda j, b: (0, 0))
    slab_out = pl.BlockSpec((None, S, LANES), lambda j, b: (j, b, 0))
    f32buf = lambda rows: pltpu.VMEM((rows, LANES), F32)
    bfbuf = lambda rows: pltpu.VMEM((rows, LANES), MXU_DTYPE)
    anyspec = pl.BlockSpec(memory_space=pl.ANY)
    return pl.pallas_call(
        body, grid=(NS, Bl), name="attn_bwd",
        in_specs=[spec_in(0), spec_in(1), spec_in(0), spec_in(1), spec_in(2), tok, tok, tok, vec, vec,
                  pl.BlockSpec((None, 8, LANES), lambda j, b: (j, 0, 0))] + [anyspec] * nh,
        out_specs=[slab_out, slab_out, slab_out, pl.BlockSpec((8, LANES), lambda j, b: (0, 0))] + [anyspec] * nh,
        out_shape=[jax.ShapeDtypeStruct((NS, T, LANES), F32)] * 3 + [jax.ShapeDtypeStruct((8, LANES), F32)]
                  + [jax.ShapeDtypeStruct((3,) + h.shape[1:], h.dtype) for h in hosted],
        scratch_shapes=[f32buf(S),
                        bfbuf(S), bfbuf(PADR), bfbuf(PADR), bfbuf(S),
                        f32buf(S), f32buf(S),
                        pltpu.VMEM((2, PADR, LANES), F32), pltpu.VMEM((2, PADR, LANES), F32),
                        f32buf(S), f32buf(S),
                        pltpu.VMEM((N_EDGE * len(DILATIONS), 2 * QBLK, 2 * QBLK), F32),
                        pltpu.VMEM((BWD_SLOTS, 2 * QBLK, 2 * QBLK), F32),
                        pltpu.VMEM((BWD_SLOTS, 2 * QBLK, 2 * QBLK), F32),
                        pltpu.VMEM((BWD_SLOTS, 2 * QBLK, 2 * QBLK), MXU_DTYPE),
                        pltpu.VMEM((BWD_SLOTS, 2 * QBLK, 2 * QBLK), MXU_DTYPE),
                        pltpu.VMEM((5, S, LANES), F32)]
                       + (plan.scratch() if nh else []),
        compiler_params=_cp(("arbitrary", "arbitrary"), vmem=ATTN_BWD_VMEM),
    )(qkh, qkh, qkv, qkv, qkv, o, lse, do, gq2, gk2, slopes, *hosted)


def _layer_norm_parts(cv, g_ln, b_ln):
    mu = jnp.mean(cv, axis=-1, keepdims=True)
    cen = cv - mu
    rs = lax.rsqrt(jnp.mean(cen * cen, axis=-1, keepdims=True) + EPS)
    z = cen * rs
    return z, rs, z * g_ln + b_ln


def _ffn_fwd(x2, cv, ya, tgt, mod, g_ln, b_ln, g_ffn, w_out, w_gate, w_up, w_down, *, S, tm):
    T, D = x2.shape
    DC = cv.shape[1]
    P, Kb, _ = w_out.shape
    Fb = w_down.shape[1]
    tps = S // tm

    def body(x_ref, cv_ref, ya_ref, t_ref, mod_ref, gln_ref, bln_ref, gf_ref, wo_hbm, wg_hbm, wu_hbm, wd_hbm,
             x1_ref, ycat_ref, mix_ref, h2_ref, g_ref, u_ref, a_ref, f_ref, dy_ref, loss_ref,
             wo, wg, wu, wd, sems):
        i = pl.program_id(0)
        _load_resident(i, [(wo_hbm, wo), (wg_hbm, wg), (wu_hbm, wu), (wd_hbm, wd)], sems)

        @pl.when(i == 0)
        def _():
            loss_ref[...] = jnp.zeros_like(loss_ref)

        _, _, ln = _layer_norm_parts(cv_ref[...], gln_ref[...], bln_ref[...])
        yc = ln * _sigmoid(ln)
        ycat = jnp.concatenate([yc, ya_ref[...]], axis=1).astype(MXU_DTYPE)
        ycat_ref[...] = ycat
        mix = _dot(ycat[:, 0:Kb], wo[0])
        for p in range(1, P):
            mix = mix + _dot(ycat[:, Kb * p:Kb * (p + 1)], wo[p])
        mix_ref[...] = mix.astype(ACT_DTYPE)
        x1 = x_ref[...] + mod_ref[:, 2 * D:3 * D] * mix
        x1_ref[...] = x1
        r2 = lax.rsqrt(jnp.mean(x1 * x1, axis=-1, keepdims=True) + EPS)
        h2 = (x1 * r2 * gf_ref[...]) * (1.0 + mod_ref[:, 4 * D:5 * D]) + mod_ref[:, 3 * D:4 * D]
        h2b = h2.astype(MXU_DTYPE)
        h2_ref[...] = h2b
        f = jnp.zeros((tm, D), F32)
        for p in range(P):
            g = _dot_nt(h2b, wg[p])
            u = _dot_nt(h2b, wu[p])
            a = (g * _sigmoid(g) * u).astype(MXU_DTYPE)
            g_ref[p] = g.astype(ACT_DTYPE)
            u_ref[p] = u.astype(ACT_DTYPE)
            a_ref[p] = a
            f = f + _dot(a, wd[p])
        f_ref[...] = f.astype(ACT_DTYPE)
        err = x1 + mod_ref[:, 5 * D:6 * D] * f - t_ref[...]
        dy_ref[...] = err * (1.0 / D)
        tot = jnp.sum(_colsum(err * err), axis=1, keepdims=True)
        loss_ref[...] += tot * (0.5 / D)

    row = lambda w: pl.BlockSpec((tm, w), lambda i: (i, 0))
    vec = lambda w: pl.BlockSpec((1, w), lambda i: (0, 0))
    blk = pl.BlockSpec((P, tm, Fb), lambda i: (0, i, 0))
    anyspec = pl.BlockSpec(memory_space=pl.ANY)
    return pl.pallas_call(
        body, grid=(T // tm,), name="ffn_fwd",
        in_specs=[row(D), row(DC), row(D - DC), row(D),
                  pl.BlockSpec((None, 1, N_MOD * D), lambda i: (i // tps, 0, 0)),
                  vec(DC), vec(DC), vec(D), anyspec, anyspec, anyspec, anyspec],
        out_specs=[row(D), row(D), row(D), row(D), blk, blk, blk, row(D), row(D),
                   pl.BlockSpec((8, LANES), lambda i: (0, 0))],
        out_shape=[jax.ShapeDtypeStruct((T, D), F32), jax.ShapeDtypeStruct((T, D), MXU_DTYPE),
                   jax.ShapeDtypeStruct((T, D), ACT_DTYPE), jax.ShapeDtypeStruct((T, D), MXU_DTYPE),
                   jax.ShapeDtypeStruct((P, T, Fb), ACT_DTYPE), jax.ShapeDtypeStruct((P, T, Fb), ACT_DTYPE),
                   jax.ShapeDtypeStruct((P, T, Fb), MXU_DTYPE), jax.ShapeDtypeStruct((T, D), ACT_DTYPE),
                   jax.ShapeDtypeStruct((T, D), F32), jax.ShapeDtypeStruct((8, LANES), F32)],
        scratch_shapes=[pltpu.VMEM(w_out.shape, w_out.dtype), pltpu.VMEM(w_gate.shape, w_gate.dtype),
                        pltpu.VMEM(w_up.shape, w_up.dtype), pltpu.VMEM(w_down.shape, w_down.dtype),
                        pltpu.SemaphoreType.DMA((4,))],
        compiler_params=_cp(("arbitrary",)),
    )(x2, cv, ya, tgt, mod, g_ln, b_ln, g_ffn, w_out, w_gate, w_up, w_down)


def _ffn_bwd(dy, x1, gs, us, fo, mixb, cv, mod, g_ln, b_ln, g_ffn, w_out, w_gate, w_up, w_down, *, S, tm):
    T, D = dy.shape
    DC = cv.shape[1]
    P, Kb, _ = w_out.shape
    Fb = w_down.shape[1]
    tps = S // tm
    Bl = T // S

    def body(dy_ref, x1_ref, g_ref, u_ref, f_ref, mix_ref, cv_ref, mod_ref, gln_ref, bln_ref, gf_ref,
             wo_hbm, wg_hbm, wu_hbm, wd_hbm,
             dg_ref, du_ref, df_ref, dx1_ref, dmix_ref, dya_ref, dcv_ref, macc_ref, gacc_ref, lacc_ref,
             wo, wg, wu, wd, sems):
        i = pl.program_id(0)
        _load_resident(i, [(wo_hbm, wo), (wg_hbm, wg), (wu_hbm, wu), (wd_hbm, wd)], sems)

        @pl.when(i == 0)
        def _():
            gacc_ref[...] = jnp.zeros_like(gacc_ref)
            lacc_ref[...] = jnp.zeros_like(lacc_ref)

        @pl.when(i % tps == 0)
        def _():
            macc_ref[...] = jnp.zeros_like(macc_ref)

        dy_t = dy_ref[...]
        x1 = x1_ref[...]
        gate_f = mod_ref[:, 5 * D:6 * D]
        macc_ref[2:3, :] += _colsum(dy_t * f_ref[...].astype(F32))
        dfb = (dy_t * gate_f).astype(MXU_DTYPE)
        df_ref[...] = dfb
        dh2 = jnp.zeros((tm, D), F32)
        for p in range(P):
            da = _dot_nt(dfb, wd[p])
            g = g_ref[p].astype(F32)
            u = u_ref[p].astype(F32)
            sg = _sigmoid(g)
            dgp = (da * u * (sg * (1.0 + g * (1.0 - sg)))).astype(MXU_DTYPE)
            dup = (da * (g * sg)).astype(MXU_DTYPE)
            dg_ref[p] = dgp
            du_ref[p] = dup
            dh2 = dh2 + _dot(dgp, wg[p]) + _dot(dup, wu[p])
        r2 = lax.rsqrt(jnp.mean(x1 * x1, axis=-1, keepdims=True) + EPS)
        xr = x1 * r2
        n2 = xr * gf_ref[...]
        macc_ref[0:1, :] += _colsum(dh2)
        macc_ref[1:2, :] += _colsum(dh2 * n2)
        dn2 = dh2 * (1.0 + mod_ref[:, 4 * D:5 * D])
        gacc_ref[0:1, :] += _colsum(dn2 * xr)
        e = dn2 * gf_ref[...]
        dx1 = dy_t + r2 * e - xr * (r2 * jnp.mean(e * xr, axis=-1, keepdims=True))
        dx1_ref[...] = dx1
        macc_ref[3:4, :] += _colsum(dx1 * mix_ref[...].astype(F32))
        dmixb = (dx1 * mod_ref[:, 2 * D:3 * D]).astype(MXU_DTYPE)
        dmix_ref[...] = dmixb
        parts = [_dot_nt(dmixb, wo[p]) for p in range(P)]
        dycat = jnp.concatenate(parts, axis=1) if P > 1 else parts[0]
        dya_ref[...] = dycat[:, DC:]
        dyc = dycat[:, :DC]
        z, rs, ln = _layer_norm_parts(cv_ref[...], gln_ref[...], bln_ref[...])
        sg = _sigmoid(ln)
        dln = dyc * (sg * (1.0 + ln * (1.0 - sg)))
        lacc_ref[0:1, :] += _colsum(dln * z)
        lacc_ref[1:2, :] += _colsum(dln)
        dz = dln * gln_ref[...]
        dcv_ref[...] = rs * (dz - jnp.mean(dz, axis=-1, keepdims=True) - z * jnp.mean(dz * z, axis=-1, keepdims=True))

    row = lambda w: pl.BlockSpec((tm, w), lambda i: (i, 0))
    vec = lambda w: pl.BlockSpec((1, w), lambda i: (0, 0))
    blk = pl.BlockSpec((P, tm, Fb), lambda i: (0, i, 0))
    anyspec = pl.BlockSpec(memory_space=pl.ANY)
    return pl.pallas_call(
        body, grid=(T // tm,), name="ffn_bwd",
        in_specs=[row(D), row(D), blk, blk, row(D), row(D), row(DC),
                  pl.BlockSpec((None, 1, N_MOD * D), lambda i: (i // tps, 0, 0)),
                  vec(DC), vec(DC), vec(D), anyspec, anyspec, anyspec, anyspec],
        out_specs=[blk, blk, row(D), row(D), row(D), row(D - DC), row(DC),
                   pl.BlockSpec((None, 8, D), lambda i: (i // tps, 0, 0)),
                   pl.BlockSpec((8, D), lambda i: (0, 0)), pl.BlockSpec((8, DC), lambda i: (0, 0))],
        out_shape=[jax.ShapeDtypeStruct((P, T, Fb), MXU_DTYPE), jax.ShapeDtypeStruct((P, T, Fb), MXU_DTYPE),
                   jax.ShapeDtypeStruct((T, D), MXU_DTYPE), jax.ShapeDtypeStruct((T, D), F32),
                   jax.ShapeDtypeStruct((T, D), MXU_DTYPE), jax.ShapeDtypeStruct((T, D - DC), F32),
                   jax.ShapeDtypeStruct((T, DC), F32), jax.ShapeDtypeStruct((Bl, 8, D), F32),
                   jax.ShapeDtypeStruct((8, D), F32), jax.ShapeDtypeStruct((8, DC), F32)],
        scratch_shapes=[pltpu.VMEM(w_out.shape, w_out.dtype), pltpu.VMEM(w_gate.shape, w_gate.dtype),
                        pltpu.VMEM(w_up.shape, w_up.dtype), pltpu.VMEM(w_down.shape, w_down.dtype),
                        pltpu.SemaphoreType.DMA((4,))],
        compiler_params=_cp(("arbitrary",)),
    )(dy, x1, gs, us, fo, mixb, cv, mod, g_ln, b_ln, g_ffn, w_out, w_gate, w_up, w_down)


def _in_bwd(da, dg, dq, dk, dv, x2, dx1, mod, g_mix, w_in, *, S, tm):
    T, D = x2.shape
    P, _, Nb = w_in.shape
    DC = da.shape[1]
    NS = dq.shape[0]
    n_in = P * Nb
    tps = S // tm
    Bl = T // S

    def body(da_ref, dg_ref, dq_ref, dk_ref, dv_ref, x_ref, dx1_ref, mod_ref, g_ref, w_ref,
             dx_ref, dproj_ref, macc_ref, gacc_ref, w_full):
        i = pl.program_id(0)
        _join_owner_blocks(w_ref, w_full)

        @pl.when(i == 0)
        def _():
            gacc_ref[...] = jnp.zeros_like(gacc_ref)

        @pl.when(i % tps == 0)
        def _():
            macc_ref[...] = jnp.zeros_like(macc_ref)

        pieces = [da_ref[...], dg_ref[...]] + [r[j] for r in (dq_ref, dk_ref, dv_ref) for j in range(NS)]
        dproj = jnp.concatenate(pieces, axis=1).astype(MXU_DTYPE)
        dproj_ref[...] = dproj
        dh = _dot_nt(dproj, w_full[...])
        x = x_ref[...]
        r = lax.rsqrt(jnp.mean(x * x, axis=-1, keepdims=True) + EPS)
        xr = x * r
        macc_ref[0:1, :] += _colsum(dh)
        macc_ref[1:2, :] += _colsum(dh * (xr * g_ref[...]))
        dn = dh * (1.0 + mod_ref[:, D:2 * D])
        gacc_ref[0:1, :] += _colsum(dn * xr)
        e = dn * g_ref[...]
        dx_ref[...] = dx1_ref[...] + r * e - xr * (r * jnp.mean(e * xr, axis=-1, keepdims=True))

    row = lambda w: pl.BlockSpec((tm, w), lambda i: (i, 0))
    slab = pl.BlockSpec((NS, tm, LANES), lambda i: (0, i, 0))
    return pl.pallas_call(
        body, grid=(T // tm,), name="in_bwd",
        in_specs=[row(DC), row(DC), slab, slab, slab, row(D), row(D),
                  pl.BlockSpec((None, 1, N_MOD * D), lambda i: (i // tps, 0, 0)),
                  pl.BlockSpec((1, D), lambda i: (0, 0)),
                  pl.BlockSpec((P, D, Nb), lambda i: (0, 0, 0))],
        out_specs=[row(D), row(n_in), pl.BlockSpec((None, 8, D), lambda i: (i // tps, 0, 0)),
                   pl.BlockSpec((8, D), lambda i: (0, 0))],
        out_shape=[jax.ShapeDtypeStruct((T, D), F32), jax.ShapeDtypeStruct((T, n_in), MXU_DTYPE),
                   jax.ShapeDtypeStruct((Bl, 8, D), F32), jax.ShapeDtypeStruct((8, D), F32)],
        scratch_shapes=[pltpu.VMEM((D, n_in), w_in.dtype)],
        compiler_params=_cp(("arbitrary",)),
    )(da, dg, dq, dk, dv, x2, dx1, mod, g_mix, w_in)


def _wgrad(a, b, *, P, name, tk, split=None, host=None):
    a_blk, b_blk = a.ndim == 3, b.ndim == 3
    plan, h_in, h_out = host if host is not None else (None, (), ())
    ni, no = len(h_in), len(h_out)
    T = a.shape[-2]
    if a_blk:
        R, C = a.shape[2], b.shape[1]

        def accumulate(a_ref, b_ref, o_ref):
            for p in range(P):
                o_ref[p] += _dot_tn(a_ref[p], b_ref[...])
    elif b_blk:
        R, C = a.shape[1], b.shape[2]

        def accumulate(a_ref, b_ref, o_ref):
            for p in range(P):
                o_ref[p] += _dot_tn(a_ref[...], b_ref[p])
    elif split == "a":
        R, C = a.shape[1] // P, b.shape[1]

        def accumulate(a_ref, b_ref, o_ref):
            full = _dot_tn(a_ref[...], b_ref[...])
            for p in range(P):
                o_ref[p] += full[R * p:R * (p + 1)]
    else:
        R, C = a.shape[1], b.shape[1] // P
        per = 1 if C % MXU_COLS == 0 else 2
        assert P % per == 0 and (per * C) % MXU_COLS == 0

        def accumulate(a_ref, b_ref, o_ref):
            for p0 in range(0, P, per):
                full = _dot_tn(a_ref[...], b_ref[:, C * p0:C * (p0 + per)])
                for j in range(per):
                    o_ref[p0 + j] += full[:, C * j:C * (j + 1)]

    n_steps = T // tk

    def body(a_ref, b_ref, *rest):
        hin, o_ref, hout, sems = rest[:ni], rest[ni], rest[ni + 1:ni + 1 + no], rest[ni + 1 + no:]
        step = pl.program_id(0)

        @pl.when(step == 0)
        def _():
            o_ref[...] = jnp.zeros_like(o_ref)
            if plan is not None:
                plan.start(hin, hout, sems)

        if plan is not None:
            @pl.when(step == n_steps // 2)
            def _():
                plan.forward(hin, hout, sems)

        accumulate(a_ref, b_ref, o_ref)

        if plan is not None:
            @pl.when(step == n_steps - 1)
            def _():
                plan.finish(hin, hout, sems)

    def spec(v):
        if v.ndim == 3:
            return pl.BlockSpec((P, tk, v.shape[2]), lambda k: (0, k, 0))
        return pl.BlockSpec((tk, v.shape[1]), lambda k: (k, 0))

    anyspec = pl.BlockSpec(memory_space=pl.ANY)
    res = pl.pallas_call(
        body, grid=(n_steps,), name=name,
        in_specs=[spec(a), spec(b)] + [anyspec] * ni,
        out_specs=[pl.BlockSpec((P, R, C), lambda k: (0, 0, 0))] + [anyspec] * no,
        out_shape=[jax.ShapeDtypeStruct((P, R, C), F32)] + list(h_out),
        scratch_shapes=plan.scratch() if plan is not None else [],
        compiler_params=_cp(("arbitrary",)),
    )(a, b, *h_in)
    return res if plan is not None else res[0]


TM_IN = 512
TM_FFN = 256
TK_WGRAD = 1024


def _alibi_slabs(n_slab):
    heads = 2 * n_slab
    slopes = 2.0 ** (-8.0 * np.arange(1, heads + 1) / heads)
    return jnp.asarray(np.broadcast_to(np.repeat(slopes.reshape(n_slab, 1, 2), HEAD_DIM, axis=2), (n_slab, 8, LANES)),
                       dtype=F32)


def _local_step(x, tgt, mod, g_mix, wdw, g_ln, b_ln, g_q, g_k, g_ffn, w_in, w_out, w_gate, w_up, w_down,
                pc_idx=None):
    Bl, S, D = x.shape
    T = Bl * S
    DC = g_ln.shape[1]
    P = w_in.shape[0]
    n_slab = (D - DC) // LANES
    x2 = x.reshape(T, D)
    t2 = tgt.reshape(T, D)
    mod3 = mod.reshape(Bl, 1, N_MOD * D)
    gq2 = jnp.tile(g_q, (1, LANES // HEAD_DIM))
    gk2 = jnp.tile(g_k, (1, LANES // HEAD_DIM))
    slopes = _alibi_slabs(n_slab)

    ag, qkv, qkh, h1 = _fwd_in(x2, mod3, g_mix, gq2, gk2, w_in, S=S, tm=TM_IN, n_ag=2 * DC)
    cv = _conv_fwd(ag, wdw, Bl=Bl, S=S, DC=DC)
    if pc_idx is not None:
        ya, lse, w_out, w_gate, w_up, w_down = _attn_fwd(qkh, qkv, slopes, Bl=Bl, S=S,
                                                         hosted=(w_out, w_gate, w_up, w_down))
    else:
        ya, lse = _attn_fwd(qkh, qkv, slopes, Bl=Bl, S=S)
    x1, ycat, mixb, h2, gs, us, acts, fo, dy, lossb = _ffn_fwd(
        x2, cv, ya, t2, mod3, g_ln, b_ln, g_ffn, w_out, w_gate, w_up, w_down, S=S, tm=TM_FFN)
    dgs, dus, dfb, dx1, dmixb, dya, dcv, macc_f, gacc_f, lacc = _ffn_bwd(
        dy, x1, gs, us, fo, mixb, cv, mod3, g_ln, b_ln, g_ffn, w_out, w_gate, w_up, w_down, S=S, tm=TM_FFN)
    wg = functools.partial(_wgrad, P=P, tk=TK_WGRAD)
    out = {}
    if pc_idx is None:
        grads = dict(w_down=wg(acts, dfb, name="wgrad_down"), w_gate=wg(dgs, h2, name="wgrad_gate"),
                     w_up=wg(dus, h2, name="wgrad_up"), w_out=wg(ycat, dmixb, name="wgrad_out", split="a"))
        dq, dk, dv, gqk = _attn_bwd(qkh, qkv, ya, lse, dya, gq2, gk2, slopes, Bl=Bl, S=S)
    else:
        g_down = wg(acts, dfb, name="wgrad_down")
        g_gate, r_down = wg(dgs, h2, name="wgrad_gate", host=_sibling_host([g_down]))
        g_up, r_gate = wg(dus, h2, name="wgrad_up", host=_sibling_host([g_gate]))
        g_out, r_up = wg(ycat, dmixb, name="wgrad_out", split="a", host=_sibling_host([g_up]))
        (r_out,) = _rs_sibling([g_out], "rs_sibling_out")
        grads = dict(w_down=g_down, w_gate=g_gate, w_up=g_up, w_out=g_out)
        sums = _pair_add([grads[nm] for nm in EARLY_WEIGHTS], [r_down, r_gate, r_up, r_out], pc_idx, "pair_add_early")
        res = _attn_bwd(qkh, qkv, ya, lse, dya, gq2, gk2, slopes, Bl=Bl, S=S, hosted=tuple(sb for _, sb in sums))
        dq, dk, dv, gqk = res[:4]
        out["early_sums"] = [s32 for s32, _ in sums]
        out["early_recv"] = list(res[4:])
    da, dg, dwdw = _conv_bwd(ag, dcv, wdw, Bl=Bl, S=S, DC=DC)
    dx, dprojb, macc_m, gacc_m = _in_bwd(da, dg, dq, dk, dv, x2, dx1, mod3, g_mix, w_in, S=S, tm=TM_IN)
    packed = _pack_small(macc_m, macc_f, gacc_m, gacc_f, lacc, gqk, dwdw, lossb)
    if pc_idx is None:
        grads["w_in"] = wg(h1, dprojb, name="wgrad_in", split="b")
    else:
        grads["w_in"], out["gathered_small"] = wg(h1, dprojb, name="wgrad_in", split="b",
                                                  host=_small_gather_host(packed))
    out.update(dx=dx.reshape(Bl, S, D), grads=grads, packed=packed)
    return out


EARLY_WEIGHTS = ("w_down", "w_gate", "w_up", "w_out")


def _small_layout(Bl):
    return 8 * Bl, 8 * Bl + 8, 8 * Bl + 8 + CONV_ROWS


def _pack_small(macc_m, macc_f, gacc_m, gacc_f, lacc, gqk, dwdw, lossb):
    Bl, _, D = macc_m.shape
    DC = lacc.shape[1]
    assert 2 * DC <= D
    SMALL_GAIN_ROW, SMALL_TAP_ROW, SMALL_ROWS = _small_layout(Bl)

    def body(mm_ref, mf_ref, gm_ref, gf_ref, la_ref, qk_ref, dw_ref, loss_ref, o_ref):
        o_ref[...] = jnp.zeros_like(o_ref)
        for b in range(Bl):
            o_ref[8 * b + 0:8 * b + 2, :] = mm_ref[b, 0:2, :]
            o_ref[8 * b + 2:8 * b + 3, :] = mf_ref[b, 3:4, :]
            o_ref[8 * b + 3:8 * b + 6, :] = mf_ref[b, 0:3, :]
        r = SMALL_GAIN_ROW
        o_ref[r:r + 1, :] = gm_ref[0:1, :]
        o_ref[r + 1:r + 2, :] = gf_ref[0:1, :]
        o_ref[r + 2:r + 3, 0:DC] = la_ref[0:1, :]
        o_ref[r + 2:r + 3, DC:2 * DC] = la_ref[1:2, :]
        qk = qk_ref[0:2, 0:HEAD_DIM] + qk_ref[0:2, HEAD_DIM:2 * HEAD_DIM]
        o_ref[r + 3:r + 4, 0:HEAD_DIM] = qk[0:1, :]
        o_ref[r + 3:r + 4, HEAD_DIM:2 * HEAD_DIM] = qk[1:2, :]
        o_ref[r + 4:r + 5, 0:LANES] = loss_ref[0:1, :]
        o_ref[SMALL_TAP_ROW:SMALL_TAP_ROW + CONV_ROWS, 0:DC] = dw_ref[...]

    return pl.pallas_call(body, name="pack_small", out_shape=jax.ShapeDtypeStruct((SMALL_ROWS, D), F32),
                          compiler_params=_cp())(macc_m, macc_f, gacc_m, gacc_f, lacc, gqk, dwdw, lossb)


def _row_tile(rows, cap=512):
    if rows <= cap:
        return rows
    best = rows
    for t in range(8, cap + 1, 8):
        if rows % t == 0:
            best = t
    return best


def _cast_weights(ws, pidx, name):
    n = len(ws)
    halves = [(w.shape[0] // 2, w.shape[1]) for w in ws]

    def body(p_ref, *refs):
        for k in range(n):
            refs[n + k][...] = refs[k][...].astype(MXU_DTYPE)

    return pl.pallas_call(
        body, name=name,
        grid_spec=pltpu.PrefetchScalarGridSpec(
            num_scalar_prefetch=1, grid=(2,),
            in_specs=[pl.BlockSpec(h, lambda i, p: (i, 0)) for h in halves],
            out_specs=[pl.BlockSpec((None,) + h, lambda i, p: (p[0], i, 0)) for h in halves]),
        out_shape=[jax.ShapeDtypeStruct((4,) + w.shape, MXU_DTYPE) for w in ws],
        compiler_params=_cp(),
    )(pidx, *ws)


def _pair_add(gs, recvs, pc_idx, name):
    n = len(gs)
    P = gs[0].shape[0]
    halves = [(g.shape[1] // 2, g.shape[2]) for g in gs]

    def body(pc_ref, *refs):
        for k in range(n):
            g_ref, r_ref, o_ref, ob_ref = refs[k], refs[n + k], refs[2 * n + 2 * k], refs[2 * n + 2 * k + 1]
            s = g_ref[...] + r_ref[...]
            ob_ref[...] = s.astype(jnp.bfloat16)

            @pl.when(pl.program_id(0) == pc_ref[0])
            def _(o_ref=o_ref, s=s):
                o_ref[...] = s

    res = pl.pallas_call(
        body, name=name,
        grid_spec=pltpu.PrefetchScalarGridSpec(
            num_scalar_prefetch=1, grid=(P,),
            in_specs=[pl.BlockSpec((None,) + h, lambda p, pc: (p, pc[1], 0)) for h in halves]
                     + [pl.BlockSpec((None,) + h, lambda p, pc: (p, 0, 0)) for h in halves],
            out_specs=[spec for h in halves for spec in (pl.BlockSpec(h, lambda p, pc: (0, 0)),
                                                         pl.BlockSpec((None,) + h, lambda p, pc: (p, 0, 0)))]),
        out_shape=[shape for h in halves for shape in (jax.ShapeDtypeStruct(h, F32),
                                                       jax.ShapeDtypeStruct((P,) + h, jnp.bfloat16))],
        compiler_params=_cp(),
    )(pc_idx, *gs, *recvs)
    return [(res[2 * k], res[2 * k + 1]) for k in range(n)]


def _final_add(owns, recvs, pc_idx, name):
    n = len(owns)

    def body(pc_ref, *refs):
        for k in range(n):
            acc = refs[k][...]
            for j in range(3):
                acc = acc + refs[n + k][j].astype(F32)
            refs[2 * n + k][...] = acc

    return pl.pallas_call(
        body, name=name,
        grid_spec=pltpu.PrefetchScalarGridSpec(
            num_scalar_prefetch=1, grid=(1,),
            in_specs=[pl.BlockSpec(o.shape, lambda i, pc: (0, 0)) for o in owns]
                     + [pl.BlockSpec((3,) + o.shape, lambda i, pc: (0, 0, 0)) for o in owns],
            out_specs=[pl.BlockSpec(o.shape, lambda i, pc: (pc[1], 0)) for o in owns]),
        out_shape=[jax.ShapeDtypeStruct((2 * o.shape[0], o.shape[1]), F32) for o in owns],
        compiler_params=_cp(),
    )(pc_idx, *owns, *recvs)


def _adamw_update(w_ref, g_ref, m_ref, v_ref, d_ref, nm_ref, nv_ref):
    c1 = 1.0 - ADAM_B1 ** ADAM_STEP
    c2 = 1.0 - ADAM_B2 ** ADAM_STEP
    gg = g_ref[...]
    nm = ADAM_B1 * m_ref[...] + (1.0 - ADAM_B1) * gg
    nv = ADAM_B2 * v_ref[...] + (1.0 - ADAM_B2) * (gg * gg)
    nm_ref[...] = nm
    nv_ref[...] = nv
    d_ref[...] = -ADAM_LR * ((nm / c1) / (jnp.sqrt(nv / c2) + ADAM_EPS) + ADAM_WD * w_ref[...])


def _adamw(w, g, m, v, name):
    R, C = w.shape
    tr = _row_tile(R, 256)
    spec = pl.BlockSpec((tr, C), lambda i: (i, 0))
    return pl.pallas_call(
        functools.partial(_adamw_update), grid=(R // tr,), name=name,
        in_specs=[spec] * 4, out_specs=[spec] * 3,
        out_shape=[jax.ShapeDtypeStruct((R, C), F32)] * 3,
    )(w, g, m, v)


def _adamw_small(quads, name):
    n = len(quads)

    def body(*refs):
        for k in range(n):
            _adamw_update(*refs[4 * k:4 * k + 4], *refs[4 * n + 3 * k:4 * n + 3 * k + 3])

    whole = lambda a: pl.BlockSpec(a.shape, lambda i: (0, 0))
    res = pl.pallas_call(
        body, grid=(1,), name=name,
        in_specs=[whole(a) for q in quads for a in q],
        out_specs=[whole(q[0]) for q in quads for _ in range(3)],
        out_shape=[jax.ShapeDtypeStruct(q[0].shape, F32) for q in quads for _ in range(3)],
    )(*[a for q in quads for a in q])
    return [tuple(res[3 * k:3 * k + 3]) for k in range(n)]


def _startup(first, w_ada, b_cols, w_in_buf, *, Bl):
    rows, D = first.shape
    NA = w_ada.shape[1]
    n_dev = 8
    g_w = _WeightGather([w_in_buf.shape])
    g_c = _SmallGather(rows)
    g_m = _SmallGather(n_dev * Bl)

    def body(first_ref, wada_ref, b_ref, win_in, g0_ref, call_ref, gm_ref, win_out, modp,
             ws0, ws1, cs0, cs1, cs2, ms0, ms1, ms2):
        g_w.start([win_out], (ws0, ws1))
        for phase in (g_c.start, g_c.forward, g_c.finish):
            phase([first_ref], [g0_ref], (cs0, cs1, cs2))
        for d in range(n_dev):
            call_ref[Bl * d:Bl * (d + 1), :] = g0_ref[rows * d:rows * d + Bl, :]
        c = call_ref[...]
        modp[...] = jnp.dot(c * _sigmoid(c), wada_ref[...], preferred_element_type=F32,
                            precision=lax.Precision.HIGH) + b_ref[...]
        for phase in (g_m.start, g_m.forward, g_m.finish):
            phase([modp], [gm_ref], (ms0, ms1, ms2))
        g_w.forward([win_out], (ws0, ws1))
        g_w.finish([win_out], (ws0, ws1))

    vmem = pl.BlockSpec(memory_space=pltpu.VMEM)
    anyspec = pl.BlockSpec(memory_space=pl.ANY)
    return pl.pallas_call(
        body, name="startup",
        in_specs=[vmem, vmem, vmem, anyspec], out_specs=[vmem, vmem, vmem, anyspec],
        out_shape=[jax.ShapeDtypeStruct((n_dev * rows, D), F32), jax.ShapeDtypeStruct((n_dev * Bl, D), F32),
                   jax.ShapeDtypeStruct((n_dev * n_dev * Bl, NA), F32),
                   jax.ShapeDtypeStruct(w_in_buf.shape, w_in_buf.dtype)],
        input_output_aliases={3: 3},
        scratch_shapes=[pltpu.VMEM((n_dev * Bl, NA), F32)] + g_w.scratch() + g_c.scratch() + g_m.scratch(),
        compiler_params=_cp(),
    )(first, w_ada, b_cols, w_in_buf)


def _ada_bwd(c_all, dmod_cols):
    def body(c_ref, d_ref, o_ref):
        c = c_ref[...]
        o_ref[...] = _dot_tn((c * _sigmoid(c)).astype(MXU_DTYPE), d_ref[...].astype(MXU_DTYPE))
    return pl.pallas_call(
        body, name="ada_bwd", out_shape=jax.ShapeDtypeStruct((c_all.shape[1], dmod_cols.shape[1]), F32),
        compiler_params=_cp(),
    )(c_all, dmod_cols)


def _small_reduce(gathered, n_dev, Bl):
    mod_rows, _, rows = _small_layout(Bl)
    width = gathered.shape[1]

    def body(g_ref, red_ref, bada_ref):
        acc = g_ref[0:rows, :]
        for d in range(1, n_dev):
            acc = acc + g_ref[d * rows:(d + 1) * rows, :]
        red_ref[...] = acc[mod_rows:, :]
        b = acc[0:8, :]
        for q in range(1, Bl):
            b = b + acc[8 * q:8 * q + 8, :]
        bada_ref[...] = b
    return pl.pallas_call(
        body, name="small_reduce",
        out_shape=[jax.ShapeDtypeStruct((rows - mod_rows, width), F32), jax.ShapeDtypeStruct((8, width), F32)],
        compiler_params=_cp(),
    )(gathered)


def _mesh_pos():
    return lax.axis_index("x"), lax.axis_index("y"), lax.axis_index("c")


def _other_chips(x, y):
    return [(1 - x, y), (x, 1 - y), (1 - x, 1 - y)]


class _WeightGather:
    def __init__(self, shapes):
        self.shapes = shapes
        self.n = len(shapes)

    def scratch(self):
        return [pltpu.SemaphoreType.DMA((6 * self.n,)), pltpu.SemaphoreType.DMA((6 * self.n,))]

    def _copy(self, outs, sems, w, k, slot, h, to):
        r2 = self.shapes[w][1] // 2
        blk = outs[w].at[slot, pl.ds(h * r2, r2), :]
        return pltpu.make_async_remote_copy(
            src_ref=blk, dst_ref=blk, send_sem=sems[0].at[6 * w + k], recv_sem=sems[1].at[6 * w + k],
            device_id=to, device_id_type=MESH_DEV)

    def start(self, outs, sems):
        x, y, c = _mesh_pos()
        for w in range(self.n):
            for k, chip in enumerate(_other_chips(x, y)):
                self._copy(outs, sems, w, k, 2 * x + y, c, (*chip, c)).start()

    def forward(self, outs, sems):
        x, y, c = _mesh_pos()
        for w in range(self.n):
            for k, chip in enumerate(_other_chips(x, y)):
                slot = 2 * chip[0] + chip[1]
                self._copy(outs, sems, w, k, slot, c, (x, y, 1 - c)).wait_recv()
                self._copy(outs, sems, w, 3 + k, slot, c, (x, y, 1 - c)).start()

    def finish(self, outs, sems):
        x, y, c = _mesh_pos()
        for w in range(self.n):
            for k, chip in enumerate(_other_chips(x, y)):
                slot = 2 * chip[0] + chip[1]
                self._copy(outs, sems, w, 3 + k, slot, 1 - c, (x, y, 1 - c)).wait_recv()
                self._copy(outs, sems, w, k, 2 * x + y, c, (*chip, c)).wait_send()
                self._copy(outs, sems, w, 3 + k, slot, c, (x, y, 1 - c)).wait_send()


class _SiblingExchange:
    def __init__(self, shapes):
        self.shapes = shapes

    def scratch(self):
        n = sum(s[0] for s in self.shapes)
        return [pltpu.SemaphoreType.DMA((n,)), pltpu.SemaphoreType.DMA((n,))]

    def out_shapes(self, dtype):
        return [jax.ShapeDtypeStruct((s[0], s[1] // 2, s[2]), dtype) for s in self.shapes]

    def _copies(self, ins, outs, sems):
        x, y, c = _mesh_pos()
        cps, k = [], 0
        for w, (P, R, _) in enumerate(self.shapes):
            r2 = R // 2
            for p in range(P):
                cps.append(pltpu.make_async_remote_copy(
                    src_ref=ins[w].at[p, pl.ds((1 - c) * r2, r2), :], dst_ref=outs[w].at[p],
                    send_sem=sems[0].at[k], recv_sem=sems[1].at[k],
                    device_id=(x, y, 1 - c), device_id_type=MESH_DEV))
                k += 1
        return cps

    def start(self, ins, outs, sems):
        for cp in self._copies(ins, outs, sems):
            cp.start()

    def forward(self, ins, outs, sems):
        pass

    def finish(self, ins, outs, sems):
        for cp in self._copies(ins, outs, sems):
            cp.wait()


def _sibling_host(grads):
    plan = _SiblingExchange([g.shape for g in grads])
    return plan, tuple(grads), tuple(plan.out_shapes(grads[0].dtype))


def _rs_sibling(grads, name):
    n = len(grads)
    plan, _, out_shapes = _sibling_host(grads)

    def body(*refs):
        ins, outs, sems = refs[:n], refs[n:2 * n], refs[2 * n:]
        plan.start(ins, outs, sems)
        plan.finish(ins, outs, sems)

    anyspec = pl.BlockSpec(memory_space=pl.ANY)
    return pl.pallas_call(
        body, name=name, out_shape=list(out_shapes),
        in_specs=[anyspec] * n, out_specs=[anyspec] * n, scratch_shapes=plan.scratch(),
    )(*grads)


class _SmallGather:
    def __init__(self, m_per):
        self.m = m_per

    def scratch(self):
        return [pltpu.SemaphoreType.DMA((7,)), pltpu.SemaphoreType.DMA((7,)), pltpu.SemaphoreType.DMA]

    def _rows(self, out, pos):
        px, py, pc = pos
        return out.at[pl.ds((4 * px + 2 * py + pc) * self.m, self.m), :]

    def _copy(self, out, sems, k, block, to, src=None):
        dst = self._rows(out, block)
        return pltpu.make_async_remote_copy(
            src_ref=dst if src is None else src, dst_ref=dst, send_sem=sems[0].at[k], recv_sem=sems[1].at[k],
            device_id=to, device_id_type=MESH_DEV)

    def start(self, ins, outs, sems):
        x, y, c = _mesh_pos()
        me = (x, y, c)
        pltpu.make_async_copy(ins[0], self._rows(outs[0], me), sems[2]).start()
        self._copy(outs[0], sems, 0, me, (x, y, 1 - c), src=ins[0]).start()
        for j, chip in enumerate(_other_chips(x, y)):
            self._copy(outs[0], sems, 1 + j, me, (*chip, c), src=ins[0]).start()

    def forward(self, ins, outs, sems):
        x, y, c = _mesh_pos()
        for j, chip in enumerate(_other_chips(x, y)):
            self._copy(outs[0], sems, 1 + j, (*chip, c), (x, y, c)).wait_recv()
            self._copy(outs[0], sems, 4 + j, (*chip, c), (x, y, 1 - c)).start()

    def finish(self, ins, outs, sems):
        x, y, c = _mesh_pos()
        me = (x, y, c)
        self._copy(outs[0], sems, 0, (x, y, 1 - c), me).wait_recv()
        for j, chip in enumerate(_other_chips(x, y)):
            self._copy(outs[0], sems, 4 + j, (*chip, 1 - c), me).wait_recv()
        self._copy(outs[0], sems, 0, me, (x, y, 1 - c), src=ins[0]).wait_send()
        for j, chip in enumerate(_other_chips(x, y)):
            self._copy(outs[0], sems, 1 + j, me, (*chip, c), src=ins[0]).wait_send()
            self._copy(outs[0], sems, 4 + j, (*chip, c), (x, y, 1 - c)).wait_send()
        pltpu.make_async_copy(ins[0], self._rows(outs[0], me), sems[2]).wait()


def _small_gather_host(packed):
    m, n = packed.shape
    return _SmallGather(m), (packed,), (jax.ShapeDtypeStruct((8 * m, n), packed.dtype),)


class _ChipExchange:
    def __init__(self, n):
        self.n = n

    def scratch(self):
        return [pltpu.SemaphoreType.DMA((3 * self.n,)), pltpu.SemaphoreType.DMA((3 * self.n,))]

    def _copies(self, ins, outs, sems):
        x, y, c = _mesh_pos()
        return [pltpu.make_async_remote_copy(
            src_ref=ins[w].at[2 * chip[0] + chip[1]], dst_ref=outs[w].at[k],
            send_sem=sems[0].at[3 * w + k], recv_sem=sems[1].at[3 * w + k],
            device_id=(*chip, c), device_id_type=MESH_DEV)
            for w in range(self.n) for k, chip in enumerate(_other_chips(x, y))]

    def start(self, ins, outs, sems):
        for cp in self._copies(ins, outs, sems):
            cp.start()

    def forward(self, ins, outs, sems):
        pass

    def finish(self, ins, outs, sems):
        for cp in self._copies(ins, outs, sems):
            cp.wait()


def _rs_final(bufs, name, chips=()):
    n, nc = len(bufs), len(chips)
    plan = _ChipExchange(nc)

    def body(*refs):
        cin = refs[n:n + nc]
        outs = refs[n + nc:2 * n + nc]
        cout = refs[2 * n + nc:2 * n + 2 * nc]
        send_sems, recv_sems = refs[2 * n + 2 * nc:2 * n + 2 * nc + 2]
        csems = refs[2 * n + 2 * nc + 2:]
        x, y, c = _mesh_pos()
        if nc:
            plan.start(cin, cout, csems)
        cps = []
        for w in range(n):
            r2 = bufs[w].shape[0] // 2
            mine = outs[w].at[pl.ds(c * r2, r2), :]
            cps.append(pltpu.make_async_remote_copy(
                src_ref=mine, dst_ref=mine, send_sem=send_sems.at[w], recv_sem=recv_sems.at[w],
                device_id=(x, y, 1 - c), device_id_type=MESH_DEV))
            cps[-1].start()
        for cp in cps:
            cp.wait()
        if nc:
            plan.finish(cin, cout, csems)

    anyspec = pl.BlockSpec(memory_space=pl.ANY)
    return pl.pallas_call(
        body, name=name,
        out_shape=[jax.ShapeDtypeStruct(b.shape, b.dtype) for b in bufs]
                  + [jax.ShapeDtypeStruct((3,) + s.shape[1:], s.dtype) for s in chips],
        in_specs=[anyspec] * (n + nc), out_specs=[anyspec] * (n + nc),
        input_output_aliases={w: w for w in range(n)},
        scratch_shapes=[pltpu.SemaphoreType.DMA((n,)), pltpu.SemaphoreType.DMA((n,))] + (plan.scratch() if nc else []),
    )(*bufs, *chips)


BIG = ("w_in", "w_out", "w_gate", "w_up", "w_down")
TRANSPOSED = ("w_gate", "w_up")
WEIGHTS = ("w_ada", "b_ada", "g_mix", "w_in", "w_dw", "b_dw", "g_conv_ln", "b_conv_ln", "g_q", "g_k",
           "w_out", "g_ffn", "w_gate", "w_up", "w_down")


def _pad_to(a, rows, cols):
    return jnp.pad(a, ((0, rows - a.shape[0]), (0, cols - a.shape[1])))


def kernel(x, c, w_ada, b_ada, g_mix, w_in, w_dw, b_dw, g_conv_ln, b_conv_ln, g_q, g_k, w_out, g_ffn, w_gate, w_up, w_down, loss_target, m_w_ada, m_b_ada, m_g_mix, m_w_in, m_w_dw, m_b_dw, m_g_conv_ln, m_b_conv_ln, m_g_q, m_g_k, m_w_out, m_g_ffn, m_w_gate, m_w_up, m_w_down, v_w_ada, v_b_ada, v_g_mix, v_w_in, v_w_dw, v_b_dw, v_g_conv_ln, v_b_conv_ln, v_g_q, v_g_k, v_w_out, v_g_ffn, v_w_gate, v_w_up, v_w_down):
    w = dict(w_ada=w_ada, b_ada=b_ada, g_mix=g_mix, w_in=w_in, w_dw=w_dw, b_dw=b_dw, g_conv_ln=g_conv_ln,
             b_conv_ln=b_conv_ln, g_q=g_q, g_k=g_k, w_out=w_out, g_ffn=g_ffn, w_gate=w_gate, w_up=w_up, w_down=w_down)
    m = dict(w_ada=m_w_ada, b_ada=m_b_ada, g_mix=m_g_mix, w_in=m_w_in, w_dw=m_w_dw, b_dw=m_b_dw, g_conv_ln=m_g_conv_ln,
             b_conv_ln=m_b_conv_ln, g_q=m_g_q, g_k=m_g_k, w_out=m_w_out, g_ffn=m_g_ffn, w_gate=m_w_gate, w_up=m_w_up,
             w_down=m_w_down)
    v = dict(w_ada=v_w_ada, b_ada=v_b_ada, g_mix=v_g_mix, w_in=v_w_in, w_dw=v_w_dw, b_dw=v_b_dw, g_conv_ln=v_g_conv_ln,
             b_conv_ln=v_b_conv_ln, g_q=v_g_q, g_k=v_g_k, w_out=v_w_out, g_ffn=v_g_ffn, w_gate=v_w_gate, w_up=v_w_up,
             w_down=v_w_down)
    Bl, S, D = x.shape
    DC = g_conv_ln.shape[1]
    NA = w_ada.shape[2]
    xi, yi, ci = _mesh_pos()
    p = 2 * xi + yi
    dev = 2 * p + ci
    n_dev = 8
    pidx = jnp.reshape(p, (1,)).astype(jnp.int32)
    pc_idx = jnp.stack([p, ci]).astype(jnp.int32)

    first = jnp.concatenate([_pad_to(c, 8, D), _pad_to(w_dw[0], CONV_ROWS, D)], axis=0)
    shard = lambda a, nm: a[0].T if nm in TRANSPOSED else a[0]
    owned = dict(zip(BIG, _cast_weights([shard(w[nm], nm) for nm in BIG], pidx, "cast_weights")))
    b_cols = lax.dynamic_slice_in_dim(b_ada, p * NA, NA, axis=1)
    g0, c_all, gm, w_in_full = _startup(first, w_ada[0], b_cols, owned["w_in"], Bl=Bl)
    g0 = g0.reshape(n_dev, 8 + CONV_ROWS, D)
    taps = jnp.concatenate([g0[2 * q, 8:, :w_dw.shape[2]] for q in range(4)], axis=1)
    wdw = jnp.where(lax.broadcasted_iota(jnp.int32, taps.shape, 0) == CONV_WIDTH, b_dw, taps)
    gm = gm.reshape(n_dev, n_dev * Bl, NA)
    mod = jnp.concatenate([lax.dynamic_slice_in_dim(gm[2 * q], dev * Bl, Bl, axis=0) for q in range(4)], axis=1)

    loc = _local_step(x, loss_target, mod, g_mix, wdw, g_conv_ln, b_conv_ln, g_q, g_k, g_ffn,
                      w_in_full, owned["w_out"], owned["w_gate"], owned["w_up"], owned["w_down"], pc_idx=pc_idx)

    halves = _final_add(loc["early_sums"], loc["early_recv"], pc_idx, "final_add_early")
    (late_sib,) = _rs_sibling([loc["grads"]["w_in"]], "rs_sibling_in")
    ((late32, late16),) = _pair_add([loc["grads"]["w_in"]], [late_sib], pc_idx, "pair_add_w_in")
    *early_full, late_recv = _rs_final(halves, "rs_final_early", chips=(late16,))
    grad = dict(zip(EARLY_WEIGHTS, early_full))
    grad["w_in"], = _rs_final(_final_add([late32], [late_recv], pc_idx, "final_add_w_in"), "rs_final_in")

    mod_rows, _, small_rows = _small_layout(Bl)
    gs = loc["gathered_small"]
    red, bada8 = _small_reduce(gs, n_dev, Bl)
    dmod_all = gs.reshape(n_dev, small_rows, D)[:, :mod_rows].reshape(n_dev * Bl, 8, D)[:, :N_MOD].reshape(n_dev * Bl, N_MOD * D)
    grad["w_ada"] = _ada_bwd(c_all, lax.dynamic_slice_in_dim(dmod_all, p * NA, NA, axis=1))
    grad["b_ada"] = bada8[:N_MOD].reshape(1, N_MOD * D)
    grad["g_mix"] = red[0:1]
    grad["g_ffn"] = red[1:2]
    grad["g_conv_ln"] = red[2:3, :DC]
    grad["b_conv_ln"] = red[2:3, DC:2 * DC]
    grad["g_q"] = red[3:4, :HEAD_DIM]
    grad["g_k"] = red[3:4, HEAD_DIM:2 * HEAD_DIM]
    loss = red[4, 0]
    dwdw = red[8:8 + CONV_ROWS, :DC]
    grad["w_dw"] = lax.dynamic_slice_in_dim(dwdw[:CONV_WIDTH], p * w_dw.shape[2], w_dw.shape[2], axis=1)
    grad["b_dw"] = dwdw[CONV_WIDTH:CONV_WIDTH + 1]

    delta, new_m, new_v = {}, {}, {}
    two_d = lambda nm: w[nm].shape[-2:]
    small = [nm for nm in WEIGHTS if nm not in BIG and nm != "w_ada"]
    small_res = dict(zip(small, _adamw_small(
        [tuple(a.reshape(two_d(nm)) for a in (w[nm], grad[nm], m[nm], v[nm])) for nm in small], "adamw_small")))
    for nm in WEIGHTS:
        shp = w[nm].shape
        if nm in TRANSPOSED:
            d_, m_, v_ = _adamw(w[nm][0].T, grad[nm], m[nm][0].T, v[nm][0].T, "adamw_" + nm)
            grad[nm], delta[nm], new_m[nm], new_v[nm] = (a.T.reshape(shp) for a in (grad[nm], d_, m_, v_))
            continue
        if nm in small_res:
            d_, m_, v_ = small_res[nm]
        else:
            d_, m_, v_ = _adamw(*(a.reshape(two_d(nm)) for a in (w[nm], grad[nm], m[nm], v[nm])), "adamw_" + nm)
        grad[nm] = grad[nm].reshape(shp)
        delta[nm], new_m[nm], new_v[nm] = d_.reshape(shp), m_.reshape(shp), v_.reshape(shp)

    return (loss, loc["dx"], *[grad[nm] for nm in WEIGHTS], *[delta[nm] for nm in WEIGHTS],
            *[new_m[nm] for nm in WEIGHTS], *[new_v[nm] for nm in WEIGHTS])
```

```python
import functools

import jax
import jax.numpy as jnp
import numpy as np
from jax import lax
from jax.experimental import pallas as pl
from jax.experimental.pallas import tpu as pltpu

F32 = jnp.float32
MXU_DTYPE = jnp.bfloat16
ACT_DTYPE = jnp.bfloat16
EPS = 1e-6
NEG_INF = -1e30
HEAD_DIM = 64
LANES = 128
MXU_COLS = 256
RADIUS = 64
QBLK = 128
DILATIONS = (1, 4, 16)
CONV_WIDTH = 31
CONV_PAD = CONV_WIDTH // 2
CONV_ROWS = 32
N_MOD = 6
ADAM_LR, ADAM_B1, ADAM_B2, ADAM_EPS, ADAM_WD, ADAM_STEP = 0.001, 0.9, 0.999, 1e-08, 0.01, 10
MESH_DEV = pl.DeviceIdType.MESH
VMEM_LIMIT = 56 << 20
ATTN_BWD_VMEM = 60 << 20


def _cp(sem=None, vmem=VMEM_LIMIT):
    kw = dict(vmem_limit_bytes=vmem)
    if sem is not None:
        kw["dimension_semantics"] = sem
    return pltpu.CompilerParams(**kw)


def _sigmoid(x):
    return 1.0 / (1.0 + jnp.exp(-x))


def _dot(a, b):
    return jnp.dot(a, b, preferred_element_type=F32)


def _dot_nt(a, b):
    return lax.dot_general(a, b, (((1,), (1,)), ((), ())), preferred_element_type=F32)


def _dot_tn(a, b):
    return lax.dot_general(a, b, (((0,), (0,)), ((), ())), preferred_element_type=F32)


def _colsum(v):
    return jnp.sum(v, axis=0, keepdims=True)


def _load_resident(i, pairs, sems):
    @pl.when(i == 0)
    def _():
        cps = [pltpu.make_async_copy(src, dst, sems.at[n]) for n, (src, dst) in enumerate(pairs)]
        for c in cps:
            c.start()
        for c in cps:
            c.wait()


def _join_owner_blocks(w_ref, w_full):
    P, _, Nb = w_ref.shape

    @pl.when(pl.program_id(0) == 0)
    def _():
        for p in range(P):
            w_full[:, Nb * p:Nb * (p + 1)] = w_ref[p]


def _fwd_in(x2, mod, g_mix, gq2, gk2, w_in, *, S, tm, n_ag):
    T, D = x2.shape
    P, _, Nb = w_in.shape
    n_in = P * Nb
    n_slab = (n_in - n_ag) // LANES
    NS = n_slab // 3
    tps = S // tm

    def body(x_ref, mod_ref, g_ref, gq_ref, gk_ref, w_ref, ag_ref, qkv_ref, qkh_ref, h_ref, w_full):
        _join_owner_blocks(w_ref, w_full)
        x = x_ref[...]
        r = lax.rsqrt(jnp.mean(x * x, axis=-1, keepdims=True) + EPS)
        n = x * r * g_ref[...]
        h = n * (1.0 + mod_ref[:, D:2 * D]) + mod_ref[:, 0:D]
        hb = h.astype(MXU_DTYPE)
        h_ref[...] = hb
        proj = _dot(hb, w_full[...])
        ag_ref[...] = proj[:, :n_ag]
        mm = _head_mean_matrix()
        for j in range(n_slab):
            v = proj[:, n_ag + LANES * j:n_ag + LANES * (j + 1)]
            qkv_ref[j] = v
            if j < 2 * NS:
                gain = gq_ref[...] * (HEAD_DIM ** -0.5 * LOG2E) if j < NS else gk_ref[...]
                qkh_ref[j] = v * lax.rsqrt(_head_mean(v * v, mm) + EPS) * gain

    return pl.pallas_call(
        body, grid=(T // tm,), name="fwd_in",
        in_specs=[pl.BlockSpec((tm, D), lambda i: (i, 0)),
                  pl.BlockSpec((None, 1, N_MOD * D), lambda i: (i // tps, 0, 0)),
                  pl.BlockSpec((1, D), lambda i: (0, 0)),
                  pl.BlockSpec((1, LANES), lambda i: (0, 0)), pl.BlockSpec((1, LANES), lambda i: (0, 0)),
                  pl.BlockSpec((P, D, Nb), lambda i: (0, 0, 0))],
        out_specs=[pl.BlockSpec((tm, n_ag), lambda i: (i, 0)),
                   pl.BlockSpec((n_slab, tm, LANES), lambda i: (0, i, 0)),
                   pl.BlockSpec((2 * NS, tm, LANES), lambda i: (0, i, 0)),
                   pl.BlockSpec((tm, D), lambda i: (i, 0))],
        out_shape=[jax.ShapeDtypeStruct((T, n_ag), F32),
                   jax.ShapeDtypeStruct((n_slab, T, LANES), F32),
                   jax.ShapeDtypeStruct((2 * NS, T, LANES), F32),
                   jax.ShapeDtypeStruct((T, D), MXU_DTYPE)],
        scratch_shapes=[pltpu.VMEM((D, n_in), w_in.dtype)],
        compiler_params=_cp(("arbitrary",)),
    )(x2, mod, g_mix, gq2, gk2, w_in)


CONV_CH = 128


def _conv_taps(win, w_ref, acc, reverse):
    n = win.shape[0]
    for b in range(8):
        wb = win if b == 0 else pltpu.roll(win, shift=n - b, axis=0)
        for a in range(4):
            o = 8 * a + b
            if o < 1 or o > CONV_WIDTH:
                continue
            k = (CONV_WIDTH - o) if reverse else (o - 1)
            acc = acc + w_ref[k:k + 1, :] * wb[8 * a:8 * a + CONV_CH, :]
    return acc


def _conv_fwd(ag, wdw, *, Bl, S, DC):
    T = ag.shape[0]
    nsc = DC // LANES
    CH = CONV_CH

    def body(a_ref, g_ref, w_ref, cv_ref, upad):
        zeros16 = jnp.zeros((16, LANES), F32)
        upad[0:16, :] = zeros16
        upad[S + 16:S + 32, :] = zeros16

        def fill(i, _):
            r0 = pl.multiple_of(i * CH, CH)
            a = a_ref[pl.ds(r0, CH), :]
            g = g_ref[pl.ds(r0, CH), :]
            upad[pl.ds(r0 + 16, CH), :] = a * _sigmoid(g)
            return 0
        lax.fori_loop(0, S // CH, fill, 0)

        def conv(i, _):
            r0 = pl.multiple_of(i * CH, CH)
            win = upad[pl.ds(r0, CH + 32), :]
            acc = jnp.zeros((CH, LANES), F32) + w_ref[CONV_WIDTH:CONV_WIDTH + 1, :]
            cv_ref[pl.ds(r0, CH), :] = _conv_taps(win, w_ref, acc, reverse=False)
            return 0
        lax.fori_loop(0, S // CH, conv, 0)

    return pl.pallas_call(
        body, grid=(Bl, nsc), name="conv_fwd",
        in_specs=[pl.BlockSpec((S, LANES), lambda b, j: (b, j)),
                  pl.BlockSpec((S, LANES), lambda b, j: (b, nsc + j)),
                  pl.BlockSpec((CONV_ROWS, LANES), lambda b, j: (0, j))],
        out_specs=pl.BlockSpec((S, LANES), lambda b, j: (b, j)),
        out_shape=jax.ShapeDtypeStruct((T, DC), F32),
        scratch_shapes=[pltpu.VMEM((S + 32, LANES), F32)],
        compiler_params=_cp(("arbitrary", "arbitrary")),
    )(ag, ag, wdw)


def _conv_bwd(ag, dcv, wdw, *, Bl, S, DC):
    T = ag.shape[0]
    nsc = DC // LANES
    CH = CONV_CH

    def body(a_ref, g_ref, d_ref, w_ref, da_ref, dg_ref, dw_ref, upad, dpad, wacc):
        b = pl.program_id(1)
        zeros16 = jnp.zeros((16, LANES), F32)
        upad[0:16, :] = zeros16
        upad[S + 16:S + 32, :] = zeros16
        dpad[0:16, :] = zeros16
        dpad[S + 16:S + 32, :] = zeros16

        @pl.when(b == 0)
        def _():
            wacc[...] = jnp.zeros_like(wacc)

        def fill(i, _):
            r0 = pl.multiple_of(i * CH, CH)
            a = a_ref[pl.ds(r0, CH), :]
            g = g_ref[pl.ds(r0, CH), :]
            upad[pl.ds(r0 + 16, CH), :] = a * _sigmoid(g)
            dpad[pl.ds(r0 + 16, CH), :] = d_ref[pl.ds(r0, CH), :]
            return 0
        lax.fori_loop(0, S // CH, fill, 0)

        def step(i, _):
            r0 = pl.multiple_of(i * CH, CH)
            dwin = dpad[pl.ds(r0, CH + 32), :]
            du = _conv_taps(dwin, w_ref, jnp.zeros((CH, LANES), F32), reverse=True)
            a = a_ref[pl.ds(r0, CH), :]
            g = g_ref[pl.ds(r0, CH), :]
            sg = _sigmoid(g)
            da_ref[pl.ds(r0, CH), :] = du * sg
            dg_ref[pl.ds(r0, CH), :] = du * a * sg * (1.0 - sg)
            dc = d_ref[pl.ds(r0, CH), :]
            uwin = upad[pl.ds(r0, CH + 32), :]
            n = CH + 32
            for bb in range(8):
                wb = uwin if bb == 0 else pltpu.roll(uwin, shift=n - bb, axis=0)
                for aa in range(4):
                    o = 8 * aa + bb
                    if o < 1 or o > CONV_WIDTH:
                        continue
                    k = o - 1
                    prod = dc * wb[8 * aa:8 * aa + CH, :]
                    part = prod[0:8, :]
                    for q in range(1, CH // 8):
                        part = part + prod[8 * q:8 * q + 8, :]
                    wacc[8 * k:8 * k + 8, :] += part
            part = dc[0:8, :]
            for q in range(1, CH // 8):
                part = part + dc[8 * q:8 * q + 8, :]
            wacc[8 * CONV_WIDTH:8 * CONV_WIDTH + 8, :] += part
            return 0
        lax.fori_loop(0, S // CH, step, 0)

        @pl.when(b == Bl - 1)
        def _():
            for k in range(CONV_ROWS):
                dw_ref[k:k + 1, :] = jnp.sum(wacc[8 * k:8 * k + 8, :], axis=0, keepdims=True)

    return pl.pallas_call(
        body, grid=(nsc, Bl), name="conv_bwd",
        in_specs=[pl.BlockSpec((S, LANES), lambda j, b: (b, j)),
                  pl.BlockSpec((S, LANES), lambda j, b: (b, nsc + j)),
                  pl.BlockSpec((S, LANES), lambda j, b: (b, j)),
                  pl.BlockSpec((CONV_ROWS, LANES), lambda j, b: (0, j))],
        out_specs=[pl.BlockSpec((S, LANES), lambda j, b: (b, j)),
                   pl.BlockSpec((S, LANES), lambda j, b: (b, j)),
                   pl.BlockSpec((CONV_ROWS, LANES), lambda j, b: (0, j))],
        out_shape=[jax.ShapeDtypeStruct((T, DC), F32), jax.ShapeDtypeStruct((T, DC), F32),
                   jax.ShapeDtypeStruct((CONV_ROWS, DC), F32)],
        scratch_shapes=[pltpu.VMEM((S + 32, LANES), F32), pltpu.VMEM((S + 32, LANES), F32),
                        pltpu.VMEM((8 * CONV_ROWS, LANES), F32)],
        compiler_params=_cp(("arbitrary", "arbitrary")),
    )(ag, ag, dcv, wdw)


ROWCH = 256


LOG2E = 1.4426950408889634
LN2 = 0.6931471805599453
N_EDGE = 4


def _head_mean_matrix():
    r = lax.broadcasted_iota(jnp.int32, (LANES, LANES), 0) // HEAD_DIM
    c = lax.broadcasted_iota(jnp.int32, (LANES, LANES), 1) // HEAD_DIM
    return jnp.where(r == c, 1.0 / HEAD_DIM, 0.0).astype(jnp.bfloat16)


def _head_mean(v, mm):
    hi = v.astype(jnp.bfloat16)
    lo = (v - hi.astype(F32)).astype(jnp.bfloat16)
    return _dot(hi, mm) + _dot(lo, mm)


def _stack_heads(blk, lane_lo):
    z = jnp.zeros_like(blk)
    return jnp.concatenate([jnp.where(lane_lo, blk, z), jnp.where(lane_lo, z, blk)], axis=0)


def _merge_heads(v2, lane_lo):
    return jnp.where(lane_lo, v2[:QBLK], v2[QBLK:])


def _bias_tables(bias_ref, slope_ref):
    row = lax.broadcasted_iota(jnp.int32, (2 * QBLK, 2 * QBLK), 0)
    col = lax.broadcasted_iota(jnp.int32, (2 * QBLK, 2 * QBLK), 1)
    rel = jnp.abs(col - RADIUS - (row % QBLK))
    slope = jnp.where(row < QBLK, slope_ref[0:1, 0:1], slope_ref[0:1, HEAD_DIM:HEAD_DIM + 1]) * LOG2E
    for pi, d in enumerate(DILATIONS):
        inside = jnp.where(rel <= RADIUS, -slope * (float(d) * rel.astype(F32)), NEG_INF)
        for e in range(N_EDGE):
            t = inside
            if e & 1:
                t = jnp.where(col < RADIUS, NEG_INF, t)
            if e & 2:
                t = jnp.where(col >= QBLK + RADIUS, NEG_INF, t)
            bias_ref[N_EDGE * pi + e] = t


def _edge_index(qb, nb):
    return jnp.where(qb == 0, 1, 0) + jnp.where(qb == nb - 1, 2, 0)


VIA = 4


def _residue(d, s):
    return (s % VIA) * VIA + s // VIA if d == VIA * VIA else s


def _gather_rows(src_ref, dst_ref, S, d, pad, f32_copy=None):
    n = S // d
    seg = n + 2 * RADIUS if pad else n
    step = min(n, 512)
    two_step = d == VIA * VIA and f32_copy is not None
    for s in range(d):
        base = s * seg
        if pad:
            dst_ref[base:base + RADIUS, :] = jnp.zeros((RADIUS, LANES), dst_ref.dtype)
            dst_ref[base + RADIUS + n:base + seg, :] = jnp.zeros((RADIUS, LANES), dst_ref.dtype)
            base += RADIUS
        for c0 in range(0, n, step):
            if d == 1:
                v = src_ref[c0:c0 + step, :]
            elif two_step:
                v = f32_copy[pl.ds((s // VIA) * (S // VIA) + s % VIA + c0 * VIA, step, stride=VIA), :]
            else:
                v = src_ref[pl.ds(_residue(d, s) + c0 * d, step, stride=d), :]
                if d == VIA and f32_copy is not None:
                    f32_copy[s * n + c0:s * n + c0 + step, :] = v
            dst_ref[base + c0:base + c0 + step, :] = v.astype(dst_ref.dtype)


def _scatter_rows(src_ref, dst_ref, S, d, pad, accumulate, f32_tmp=None):
    n = S // d
    seg = n + 2 * RADIUS if pad else n
    first = RADIUS if pad else 0
    _unpermute(lambda s, c0, step: src_ref[s * seg + first + c0:s * seg + first + c0 + step, :],
               dst_ref, S, d, accumulate, f32_tmp)


def _unpermute(rows_of, dst_ref, S, d, accumulate, f32_tmp):
    n = S // d
    step = min(n, 512)
    if d == VIA * VIA and f32_tmp is not None:
        for s in range(d):
            f32_tmp[pl.ds((s // VIA) * (S // VIA) + s % VIA, n, stride=VIA), :] = rows_of(s, 0, n)
        n4 = S // VIA
        _unpermute(lambda s, c0, st: f32_tmp[s * n4 + c0:s * n4 + c0 + st, :], dst_ref, S, VIA, accumulate, None)
        return
    for s in range(d):
        for c0 in range(0, n, step):
            v = rows_of(s, c0, step)
            idx = pl.ds(c0, step) if d == 1 else pl.ds(_residue(d, s) + c0 * d, step, stride=d)
            if accumulate:
                dst_ref[idx, :] = dst_ref[idx, :] + v
            else:
                dst_ref[idx, :] = v


def _zero_uncovered(acc, S, d):
    n = S // d
    if (n // QBLK) % 2:
        return
    seg = n + 2 * RADIUS
    for r in range(d):
        acc[0, r * seg + n:r * seg + seg, :] = jnp.zeros((2 * RADIUS, LANES), F32)
        acc[1, r * seg:r * seg + 2 * RADIUS, :] = jnp.zeros((2 * RADIUS, LANES), F32)


def _scatter_parity(acc, dst_ref, S, d, f32_tmp=None):
    n = S // d
    seg = n + 2 * RADIUS
    one_block = (n // QBLK) % 2 == 1

    def rows_of(s, c0, step):
        rows = slice(s * seg + RADIUS + c0, s * seg + RADIUS + c0 + step)
        return acc[s % 2, rows, :] if one_block else acc[0, rows, :] + acc[1, rows, :]

    _unpermute(rows_of, dst_ref, S, d, True, f32_tmp)


PIPE_UNROLL = 4
PIPE_SLOTS = 16
BWD_SLOTS = 12


def _pipeline(n_items, stages, unroll):
    K = len(stages)
    assert n_items % unroll == 0 and K * unroll <= (PIPE_SLOTS if K == 4 else BWD_SLOTS)
    trips = n_items // unroll
    assert trips >= K - 1

    def trip(t, static):
        for s in reversed(range(K)):
            if static and not 0 <= t - s < trips:
                continue
            for u in range(unroll):
                item = unroll * (t - s) + u
                stages[s](jnp.int32(item) if static else item)

    for t in range(K - 1):
        trip(t, True)

    def full(t, carry):
        trip(t, False)
        return carry
    lax.fori_loop(K - 1, trips, full, 0)
    for t in range(trips, trips + K - 1):
        trip(t, True)


def _attn_fwd(qkh, qkv, slopes, *, Bl, S, hosted=()):
    n3, T, _ = qkv.shape
    NS = n3 // 3
    NB = S // QBLK
    PADR = S + 2 * RADIUS * DILATIONS[-1]
    nh = len(hosted)
    plan = _WeightGather([b.shape for b in hosted]) if nh else None
    n_steps = Bl * NS

    def body(qh, kh, v_ref, slope_ref, *rest):
        o_ref, lse_ref = rest[nh:nh + 2]
        wouts = rest[nh + 2:2 * nh + 2]
        (qp, kp, vp, op, lp, onat, lnat, bias_ref, sbuf, pbuf, mbuf, lbuf, tmps) = rest[2 * nh + 2:2 * nh + 15]
        sems = rest[2 * nh + 15:]
        step = pl.program_id(0) * Bl + pl.program_id(1)
        if nh:
            @pl.when(step == 0)
            def _():
                plan.start(wouts, sems)

            @pl.when(step == (7 * n_steps) // 8)
            def _():
                plan.forward(wouts, sems)

        lane_lo = lax.broadcasted_iota(jnp.int32, (QBLK, LANES), 1) < HEAD_DIM

        @pl.when(pl.program_id(1) == 0)
        def _():
            _bias_tables(bias_ref, slope_ref)

        for pi, d in enumerate(DILATIONS):
            n = S // d
            nb = n // QBLK
            _gather_rows(qh, qp, S, d, pad=False, f32_copy=tmps.at[0])
            _gather_rows(kh, kp, S, d, pad=True, f32_copy=tmps.at[1])
            _gather_rows(v_ref, vp, S, d, pad=True, f32_copy=tmps.at[2])

            def offsets(i, nb=nb):
                r = i // nb
                return pl.multiple_of(i * QBLK, QBLK), pl.multiple_of((i + r) * QBLK, QBLK), i % nb

            def scores(i, pi=pi, nb=nb):
                q0, k0, qb = offsets(i)
                qs = _stack_heads(qp[pl.ds(q0, QBLK), :], lane_lo)
                sbuf[i % PIPE_SLOTS] = (_dot_nt(qs, kp[pl.ds(k0, 2 * QBLK), :])
                                        + bias_ref[N_EDGE * pi + _edge_index(qb, nb)])

            def rowmax(i):
                m = jnp.max(sbuf[i % PIPE_SLOTS], axis=1, keepdims=True)
                mbuf[i % PIPE_SLOTS] = jnp.broadcast_to(m, (2 * QBLK, LANES))

            def expsum(i):
                m = mbuf[i % PIPE_SLOTS]
                p = jnp.exp2(sbuf[i % PIPE_SLOTS] - jnp.concatenate([m, m], axis=1))
                pbuf[i % PIPE_SLOTS] = p.astype(MXU_DTYPE)
                lbuf[i % PIPE_SLOTS] = jnp.broadcast_to(jnp.sum(p, axis=1, keepdims=True), (2 * QBLK, LANES))

            def values(i):
                q0, k0, _ = offsets(i)
                l = lbuf[i % PIPE_SLOTS]
                o2 = _dot(pbuf[i % PIPE_SLOTS], vp[pl.ds(k0, 2 * QBLK), :]) * (1.0 / l)
                op[pl.ds(q0, QBLK), :] = _merge_heads(o2, lane_lo)
                lp[pl.ds(q0, QBLK), :] = _merge_heads(mbuf[i % PIPE_SLOTS] + jnp.log2(l), lane_lo)

            _pipeline(NB, [scores, rowmax, expsum, values], PIPE_UNROLL)
            _scatter_rows(op, onat.at[pi], S, d, pad=False, accumulate=False, f32_tmp=tmps.at[0])
            _scatter_rows(lp, lnat.at[pi], S, d, pad=False, accumulate=False, f32_tmp=tmps.at[1])

        for c0 in range(0, S, ROWCH):
            ls = [lnat[pi, c0:c0 + ROWCH, :] for pi in range(len(DILATIONS))]
            mx = jnp.maximum(jnp.maximum(ls[0], ls[1]), ls[2])
            es = [jnp.exp2(l - mx) for l in ls]
            tot = es[0] + es[1] + es[2]
            inv = 1.0 / tot
            acc = (es[0] * inv) * onat[0, c0:c0 + ROWCH, :]
            for pi in (1, 2):
                acc = acc + (es[pi] * inv) * onat[pi, c0:c0 + ROWCH, :]
            o_ref[c0:c0 + ROWCH, :] = acc
            lse_ref[c0:c0 + ROWCH, :] = mx + jnp.log2(tot)

        if nh:
            @pl.when(step == n_steps - 1)
            def _():
                plan.finish(wouts, sems)

    spec_in = lambda off: pl.BlockSpec((None, S, LANES), lambda j, b: (off * NS + j, b, 0))
    out = pl.BlockSpec((S, LANES), lambda j, b: (b, j))
    anyspec = pl.BlockSpec(memory_space=pl.ANY)
    return pl.pallas_call(
        body, grid=(NS, Bl), name="attn_fwd",
        in_specs=[spec_in(0), spec_in(1), spec_in(2),
                  pl.BlockSpec((None, 8, LANES), lambda j, b: (j, 0, 0))] + [anyspec] * nh,
        out_specs=[out, out] + [anyspec] * nh,
        out_shape=[jax.ShapeDtypeStruct((T, NS * LANES), F32)] * 2
                  + [jax.ShapeDtypeStruct(b.shape, b.dtype) for b in hosted],
        input_output_aliases={4 + w: 2 + w for w in range(nh)},
        scratch_shapes=[pltpu.VMEM((S, LANES), MXU_DTYPE), pltpu.VMEM((PADR, LANES), MXU_DTYPE),
                        pltpu.VMEM((PADR, LANES), MXU_DTYPE),
                        pltpu.VMEM((S, LANES), F32), pltpu.VMEM((S, LANES), F32),
                        pltpu.VMEM((3, S, LANES), F32), pltpu.VMEM((3, S, LANES), F32),
                        pltpu.VMEM((N_EDGE * len(DILATIONS), 2 * QBLK, 2 * QBLK), F32),
                        pltpu.VMEM((PIPE_SLOTS, 2 * QBLK, 2 * QBLK), F32),
                        pltpu.VMEM((PIPE_SLOTS, 2 * QBLK, 2 * QBLK), MXU_DTYPE),
                        pltpu.VMEM((PIPE_SLOTS, 2 * QBLK, LANES), F32), pltpu.VMEM((PIPE_SLOTS, 2 * QBLK, LANES), F32),
                        pltpu.VMEM((3, S, LANES), F32)]
                       + (plan.scratch() if nh else []),
        compiler_params=_cp(("arbitrary", "arbitrary")),
    )(qkh, qkh, qkv, slopes, *hosted)


def _attn_bwd(qkh, qkv, o, lse, do, gq2, gk2, slopes, *, Bl, S, hosted=()):
    n3, T, _ = qkv.shape
    NS = n3 // 3
    NB = S // QBLK
    PADR = S + 2 * RADIUS * DILATIONS[-1]
    QSCALE = HEAD_DIM ** -0.5
    nh = len(hosted)
    plan = _ChipExchange(nh)
    n_steps = Bl * NS

    def body(qh, kh, q_ref, k_ref, v_ref, o_ref, lse_ref, do_ref, gq_ref, gk_ref, slope_ref, *rest):
        hin = rest[:nh]
        dq_ref, dk_ref, dv_ref, gacc_ref = rest[nh:nh + 4]
        hout = rest[nh + 4:2 * nh + 4]
        (ld, qp, kp, vp, dop, ldp, dqp, dkacc, dvacc, dqn, dkn, bias_ref,
         sbuf, dpbuf, pbuf, dsbuf, tmps) = rest[2 * nh + 4:2 * nh + 21]
        sems = rest[2 * nh + 21:]
        step = pl.program_id(0) * Bl + pl.program_id(1)

        @pl.when(step == 0)
        def _():
            gacc_ref[...] = jnp.zeros_like(gacc_ref)
            if nh:
                plan.start(hin, hout, sems)

        mm = _head_mean_matrix()
        lane_lo = lax.broadcasted_iota(jnp.int32, (QBLK, LANES), 1) < HEAD_DIM

        @pl.when(pl.program_id(1) == 0)
        def _():
            _bias_tables(bias_ref, slope_ref)

        lse_lanes = lax.broadcasted_iota(jnp.int32, (ROWCH, LANES), 1) % HEAD_DIM < HEAD_DIM // 2
        for c0 in range(0, S, ROWCH):
            delta = _head_mean(do_ref[c0:c0 + ROWCH, :] * o_ref[c0:c0 + ROWCH, :], mm) * HEAD_DIM
            ld[c0:c0 + ROWCH, :] = jnp.where(lse_lanes, lse_ref[c0:c0 + ROWCH, :], delta)
            dqn[c0:c0 + ROWCH, :] = jnp.zeros((ROWCH, LANES), F32)
            dkn[c0:c0 + ROWCH, :] = jnp.zeros((ROWCH, LANES), F32)
            dv_ref[c0:c0 + ROWCH, :] = jnp.zeros((ROWCH, LANES), F32)

        for pi, d in enumerate(DILATIONS):
            n = S // d
            nb = n // QBLK
            _gather_rows(qh, qp, S, d, pad=False, f32_copy=tmps.at[0])
            _gather_rows(kh, kp, S, d, pad=True, f32_copy=tmps.at[1])
            _gather_rows(v_ref, vp, S, d, pad=True, f32_copy=tmps.at[2])
            _gather_rows(do_ref, dop, S, d, pad=False, f32_copy=tmps.at[3])
            _gather_rows(ld, ldp, S, d, pad=False, f32_copy=tmps.at[4])
            _zero_uncovered(dkacc, S, d)
            _zero_uncovered(dvacc, S, d)

            def offsets(i, nb=nb):
                r = i // nb
                return pl.multiple_of(i * QBLK, QBLK), pl.multiple_of((i + r) * QBLK, QBLK), i % nb

            def scores(i, pi=pi, nb=nb):
                q0, k0, qb = offsets(i)
                qs = _stack_heads(qp[pl.ds(q0, QBLK), :], lane_lo)
                dos = _stack_heads(dop[pl.ds(q0, QBLK), :], lane_lo)
                sbuf[i % BWD_SLOTS] = (_dot_nt(qs, kp[pl.ds(k0, 2 * QBLK), :])
                                       + bias_ref[N_EDGE * pi + _edge_index(qb, nb)])
                dpbuf[i % BWD_SLOTS] = _dot_nt(dos, vp[pl.ds(k0, 2 * QBLK), :])

            def probs(i):
                q0, _, _ = offsets(i)
                blk = ldp[pl.ds(q0, QBLK), :]
                half = HEAD_DIM // 2
                lcol = jnp.concatenate([blk[:, 0:1], blk[:, HEAD_DIM:HEAD_DIM + 1]], axis=0)
                dcol = jnp.concatenate([blk[:, half:half + 1], blk[:, HEAD_DIM + half:HEAD_DIM + half + 1]], axis=0)
                p = jnp.exp2(sbuf[i % BWD_SLOTS] - lcol)
                pbuf[i % BWD_SLOTS] = p.astype(MXU_DTYPE)
                dsbuf[i % BWD_SLOTS] = (p * (dpbuf[i % BWD_SLOTS] - dcol)).astype(MXU_DTYPE)

            def grads(i):
                q0, k0, _ = offsets(i)
                qs = _stack_heads(qp[pl.ds(q0, QBLK), :], lane_lo)
                dos = _stack_heads(dop[pl.ds(q0, QBLK), :], lane_lo)
                ds = dsbuf[i % BWD_SLOTS]
                dvacc[i % 2, pl.ds(k0, 2 * QBLK), :] = _dot_tn(pbuf[i % BWD_SLOTS], dos)
                dkacc[i % 2, pl.ds(k0, 2 * QBLK), :] = _dot_tn(ds, qs)
                dqp[pl.ds(q0, QBLK), :] = _merge_heads(_dot(ds, kp[pl.ds(k0, 2 * QBLK), :]), lane_lo)

            _pipeline(NB, [scores, probs, grads], PIPE_UNROLL)
            _scatter_rows(dqp, dqn, S, d, pad=False, accumulate=True, f32_tmp=tmps.at[0])
            _scatter_parity(dkacc, dkn, S, d, f32_tmp=tmps.at[1])
            _scatter_parity(dvacc, dv_ref, S, d, f32_tmp=tmps.at[2])

        gq_sum = jnp.zeros((8, LANES), F32)
        gk_sum = jnp.zeros((8, LANES), F32)
        for c0 in range(0, S, ROWCH):
            for src_ref, dn, g_ref, dst_ref, scale, is_q in ((q_ref, dqn, gq_ref, dq_ref, QSCALE, True),
                                                             (k_ref, dkn, gk_ref, dk_ref, LN2, False)):
                x = src_ref[c0:c0 + ROWCH, :]
                dh = dn[c0:c0 + ROWCH, :]
                rr = lax.rsqrt(_head_mean(x * x, mm) + EPS)
                e = dh * (g_ref[...] * scale)
                dst_ref[c0:c0 + ROWCH, :] = rr * e - x * (rr * rr * rr) * _head_mean(e * x, mm)
                gpart = dh * (x * rr * scale)
                acc8 = gpart[0:8, :]
                for q8 in range(1, ROWCH // 8):
                    acc8 = acc8 + gpart[8 * q8:8 * q8 + 8, :]
                if is_q:
                    gq_sum = gq_sum + acc8
                else:
                    gk_sum = gk_sum + acc8
        gacc_ref[0:1, :] += jnp.sum(gq_sum, axis=0, keepdims=True)
        gacc_ref[1:2, :] += jnp.sum(gk_sum, axis=0, keepdims=True)

        if nh:
            @pl.when(step == n_steps - 1)
            def _():
                plan.finish(hin, hout, sems)

    spec_in = lambda off: pl.BlockSpec((None, S, LANES), lambda j, b: (off * NS + j, b, 0))
    tok = pl.BlockSpec((S, LANES), lambda j, b: (b, j))
    vec = pl.BlockSpec((1, LANES), lambda j, b: (0, 0))
    slab_out = pl.BlockSpec((None, S, LANES), lambda j, b: (j, b, 0))
    f32buf = lambda rows: pltpu.VMEM((rows, LANES), F32)
    bfbuf = lambda rows: pltpu.VMEM((rows, LANES), MXU_DTYPE)
    anyspec = pl.BlockSpec(memory_space=pl.ANY)
    return pl.pallas_call(
        body, grid=(NS, Bl), name="attn_bwd",
        in_specs=[spec_in(0), spec_in(1), spec_in(0), spec_in(1), spec_in(2), tok, tok, tok, vec, vec,
                  pl.BlockSpec((None, 8, LANES), lambda j, b: (j, 0, 0))] + [anyspec] * nh,
        out_specs=[slab_out, slab_out, slab_out, pl.BlockSpec((8, LANES), lambda j, b: (0, 0))] + [anyspec] * nh,
        out_shape=[jax.ShapeDtypeStruct((NS, T, LANES), F32)] * 3 + [jax.ShapeDtypeStruct((8, LANES), F32)]
                  + [jax.ShapeDtypeStruct((3,) + h.shape[1:], h.dtype) for h in hosted],
        scratch_shapes=[f32buf(S),
                        bfbuf(S), bfbuf(PADR), bfbuf(PADR), bfbuf(S),
                        f32buf(S), f32buf(S),
                        pltpu.VMEM((2, PADR, LANES), F32), pltpu.VMEM((2, PADR, LANES), F32),
                        f32buf(S), f32buf(S),
                        pltpu.VMEM((N_EDGE * len(DILATIONS), 2 * QBLK, 2 * QBLK), F32),
                        pltpu.VMEM((BWD_SLOTS, 2 * QBLK, 2 * QBLK), F32),
                        pltpu.VMEM((BWD_SLOTS, 2 * QBLK, 2 * QBLK), F32),
                        pltpu.VMEM((BWD_SLOTS, 2 * QBLK, 2 * QBLK), MXU_DTYPE),
                        pltpu.VMEM((BWD_SLOTS, 2 * QBLK, 2 * QBLK), MXU_DTYPE),
                        pltpu.VMEM((5, S, LANES), F32)]
                       + (plan.scratch() if nh else []),
        compiler_params=_cp(("arbitrary", "arbitrary"), vmem=ATTN_BWD_VMEM),
    )(qkh, qkh, qkv, qkv, qkv, o, lse, do, gq2, gk2, slopes, *hosted)


def _layer_norm_parts(cv, g_ln, b_ln):
    mu = jnp.mean(cv, axis=-1, keepdims=True)
    cen = cv - mu
    rs = lax.rsqrt(jnp.mean(cen * cen, axis=-1, keepdims=True) + EPS)
    z = cen * rs
    return z, rs, z * g_ln + b_ln


def _ffn_fwd(x2, cv, ya, tgt, mod, g_ln, b_ln, g_ffn, w_out, w_gate, w_up, w_down, *, S, tm):
    T, D = x2.shape
    DC = cv.shape[1]
    P, Kb, _ = w_out.shape
    Fb = w_down.shape[1]
    tps = S // tm

    def body(x_ref, cv_ref, ya_ref, t_ref, mod_ref, gln_ref, bln_ref, gf_ref, wo_hbm, wg_hbm, wu_hbm, wd_hbm,
             x1_ref, ycat_ref, mix_ref, h2_ref, g_ref, u_ref, a_ref, f_ref, dy_ref, loss_ref,
             wo, wg, wu, wd, sems):
        i = pl.program_id(0)
        _load_resident(i, [(wo_hbm, wo), (wg_hbm, wg), (wu_hbm, wu), (wd_hbm, wd)], sems)

        @pl.when(i == 0)
        def _():
            loss_ref[...] = jnp.zeros_like(loss_ref)

        _, _, ln = _layer_norm_parts(cv_ref[...], gln_ref[...], bln_ref[...])
        yc = ln * _sigmoid(ln)
        ycat = jnp.concatenate([yc, ya_ref[...]], axis=1).astype(MXU_DTYPE)
        ycat_ref[...] = ycat
        mix = _dot(ycat[:, 0:Kb], wo[0])
        for p in range(1, P):
            mix = mix + _dot(ycat[:, Kb * p:Kb * (p + 1)], wo[p])
        mix_ref[...] = mix.astype(ACT_DTYPE)
        x1 = x_ref[...] + mod_ref[:, 2 * D:3 * D] * mix
        x1_ref[...] = x1
        r2 = lax.rsqrt(jnp.mean(x1 * x1, axis=-1, keepdims=True) + EPS)
        h2 = (x1 * r2 * gf_ref[...]) * (1.0 + mod_ref[:, 4 * D:5 * D]) + mod_ref[:, 3 * D:4 * D]
        h2b = h2.astype(MXU_DTYPE)
        h2_ref[...] = h2b
        g = _dot_nt(h2b, wg[...])
        u = _dot_nt(h2b, wu[...])
        a = (g * _sigmoid(g) * u).astype(MXU_DTYPE)
        g_ref[...] = g.astype(ACT_DTYPE)
        u_ref[...] = u.astype(ACT_DTYPE)
        a_ref[...] = a
        f = _dot(a, wd[...])
        f_ref[...] = f.astype(ACT_DTYPE)
        err = x1 + mod_ref[:, 5 * D:6 * D] * f - t_ref[...]
        dy_ref[...] = err * (1.0 / D)
        tot = jnp.sum(_colsum(err * err), axis=1, keepdims=True)
        loss_ref[...] += tot * (0.5 / D)

    row = lambda w: pl.BlockSpec((tm, w), lambda i: (i, 0))
    vec = lambda w: pl.BlockSpec((1, w), lambda i: (0, 0))
    F = P * Fb
    blk = row(F)
    w_gate, w_up, w_down = (w.reshape(F, D) for w in (w_gate, w_up, w_down))
    anyspec = pl.BlockSpec(memory_space=pl.ANY)
    return pl.pallas_call(
        body, grid=(T // tm,), name="ffn_fwd",
        in_specs=[row(D), row(DC), row(D - DC), row(D),
                  pl.BlockSpec((None, 1, N_MOD * D), lambda i: (i // tps, 0, 0)),
                  vec(DC), vec(DC), vec(D), anyspec, anyspec, anyspec, anyspec],
        out_specs=[row(D), row(D), row(D), row(D), blk, blk, blk, row(D), row(D),
                   pl.BlockSpec((8, LANES), lambda i: (0, 0))],
        out_shape=[jax.ShapeDtypeStruct((T, D), F32), jax.ShapeDtypeStruct((T, D), MXU_DTYPE),
                   jax.ShapeDtypeStruct((T, D), ACT_DTYPE), jax.ShapeDtypeStruct((T, D), MXU_DTYPE),
                   jax.ShapeDtypeStruct((T, F), ACT_DTYPE), jax.ShapeDtypeStruct((T, F), ACT_DTYPE),
                   jax.ShapeDtypeStruct((T, F), MXU_DTYPE), jax.ShapeDtypeStruct((T, D), ACT_DTYPE),
                   jax.ShapeDtypeStruct((T, D), F32), jax.ShapeDtypeStruct((8, LANES), F32)],
        scratch_shapes=[pltpu.VMEM(w_out.shape, w_out.dtype), pltpu.VMEM(w_gate.shape, w_gate.dtype),
                        pltpu.VMEM(w_up.shape, w_up.dtype), pltpu.VMEM(w_down.shape, w_down.dtype),
                        pltpu.SemaphoreType.DMA((4,))],
        compiler_params=_cp(("arbitrary",)),
    )(x2, cv, ya, tgt, mod, g_ln, b_ln, g_ffn, w_out, w_gate, w_up, w_down)


def _ffn_bwd(dy, x1, gs, us, fo, mixb, cv, mod, g_ln, b_ln, g_ffn, w_out, w_gate, w_up, w_down, *, S, tm):
    T, D = dy.shape
    DC = cv.shape[1]
    P, Kb, _ = w_out.shape
    Fb = w_down.shape[1]
    tps = S // tm
    Bl = T // S

    def body(dy_ref, x1_ref, g_ref, u_ref, f_ref, mix_ref, cv_ref, mod_ref, gln_ref, bln_ref, gf_ref,
             wo_hbm, wg_hbm, wu_hbm, wd_hbm,
             dg_ref, du_ref, df_ref, dx1_ref, dmix_ref, dya_ref, dcv_ref, macc_ref, gacc_ref, lacc_ref,
             wo, wg, wu, wd, sems):
        i = pl.program_id(0)
        _load_resident(i, [(wo_hbm, wo), (wg_hbm, wg), (wu_hbm, wu), (wd_hbm, wd)], sems)

        @pl.when(i == 0)
        def _():
            gacc_ref[...] = jnp.zeros_like(gacc_ref)
            lacc_ref[...] = jnp.zeros_like(lacc_ref)

        @pl.when(i % tps == 0)
        def _():
            macc_ref[...] = jnp.zeros_like(macc_ref)

        dy_t = dy_ref[...]
        x1 = x1_ref[...]
        gate_f = mod_ref[:, 5 * D:6 * D]
        macc_ref[2:3, :] += _colsum(dy_t * f_ref[...].astype(F32))
        dfb = (dy_t * gate_f).astype(MXU_DTYPE)
        df_ref[...] = dfb
        da = _dot_nt(dfb, wd[...])
        g = g_ref[...].astype(F32)
        u = u_ref[...].astype(F32)
        sg = _sigmoid(g)
        dgp = (da * u * (sg * (1.0 + g * (1.0 - sg)))).astype(MXU_DTYPE)
        dup = (da * (g * sg)).astype(MXU_DTYPE)
        dg_ref[...] = dgp
        du_ref[...] = dup
        dh2 = _dot(dgp, wg[...]) + _dot(dup, wu[...])
        r2 = lax.rsqrt(jnp.mean(x1 * x1, axis=-1, keepdims=True) + EPS)
        xr = x1 * r2
        n2 = xr * gf_ref[...]
        macc_ref[0:1, :] += _colsum(dh2)
        macc_ref[1:2, :] += _colsum(dh2 * n2)
        dn2 = dh2 * (1.0 + mod_ref[:, 4 * D:5 * D])
        gacc_ref[0:1, :] += _colsum(dn2 * xr)
        e = dn2 * gf_ref[...]
        dx1 = dy_t + r2 * e - xr * (r2 * jnp.mean(e * xr, axis=-1, keepdims=True))
        dx1_ref[...] = dx1
        macc_ref[3:4, :] += _colsum(dx1 * mix_ref[...].astype(F32))
        dmixb = (dx1 * mod_ref[:, 2 * D:3 * D]).astype(MXU_DTYPE)
        dmix_ref[...] = dmixb
        parts = [_dot_nt(dmixb, wo[p]) for p in range(P)]
        dycat = jnp.concatenate(parts, axis=1) if P > 1 else parts[0]
        dya_ref[...] = dycat[:, DC:]
        dyc = dycat[:, :DC]
        z, rs, ln = _layer_norm_parts(cv_ref[...], gln_ref[...], bln_ref[...])
        sg = _sigmoid(ln)
        dln = dyc * (sg * (1.0 + ln * (1.0 - sg)))
        lacc_ref[0:1, :] += _colsum(dln * z)
        lacc_ref[1:2, :] += _colsum(dln)
        dz = dln * gln_ref[...]
        dcv_ref[...] = rs * (dz - jnp.mean(dz, axis=-1, keepdims=True) - z * jnp.mean(dz * z, axis=-1, keepdims=True))

    row = lambda w: pl.BlockSpec((tm, w), lambda i: (i, 0))
    vec = lambda w: pl.BlockSpec((1, w), lambda i: (0, 0))
    F = P * Fb
    blk = row(F)
    w_gate, w_up, w_down = (w.reshape(F, D) for w in (w_gate, w_up, w_down))
    anyspec = pl.BlockSpec(memory_space=pl.ANY)
    return pl.pallas_call(
        body, grid=(T // tm,), name="ffn_bwd",
        in_specs=[row(D), row(D), blk, blk, row(D), row(D), row(DC),
                  pl.BlockSpec((None, 1, N_MOD * D), lambda i: (i // tps, 0, 0)),
                  vec(DC), vec(DC), vec(D), anyspec, anyspec, anyspec, anyspec],
        out_specs=[blk, blk, row(D), row(D), row(D), row(D - DC), row(DC),
                   pl.BlockSpec((None, 8, D), lambda i: (i // tps, 0, 0)),
                   pl.BlockSpec((8, D), lambda i: (0, 0)), pl.BlockSpec((8, DC), lambda i: (0, 0))],
        out_shape=[jax.ShapeDtypeStruct((T, F), MXU_DTYPE), jax.ShapeDtypeStruct((T, F), MXU_DTYPE),
                   jax.ShapeDtypeStruct((T, D), MXU_DTYPE), jax.ShapeDtypeStruct((T, D), F32),
                   jax.ShapeDtypeStruct((T, D), MXU_DTYPE), jax.ShapeDtypeStruct((T, D - DC), F32),
                   jax.ShapeDtypeStruct((T, DC), F32), jax.ShapeDtypeStruct((Bl, 8, D), F32),
                   jax.ShapeDtypeStruct((8, D), F32), jax.ShapeDtypeStruct((8, DC), F32)],
        scratch_shapes=[pltpu.VMEM(w_out.shape, w_out.dtype), pltpu.VMEM(w_gate.shape, w_gate.dtype),
                        pltpu.VMEM(w_up.shape, w_up.dtype), pltpu.VMEM(w_down.shape, w_down.dtype),
                        pltpu.SemaphoreType.DMA((4,))],
        compiler_params=_cp(("arbitrary",)),
    )(dy, x1, gs, us, fo, mixb, cv, mod, g_ln, b_ln, g_ffn, w_out, w_gate, w_up, w_down)


def _in_bwd(da, dg, dq, dk, dv, x2, dx1, mod, g_mix, w_in, *, S, tm):
    T, D = x2.shape
    P, _, Nb = w_in.shape
    DC = da.shape[1]
    NS = dq.shape[0]
    n_in = P * Nb
    tps = S // tm
    Bl = T // S

    def body(da_ref, dg_ref, dq_ref, dk_ref, dv_ref, x_ref, dx1_ref, mod_ref, g_ref, w_ref,
             dx_ref, dproj_ref, macc_ref, gacc_ref, w_full):
        i = pl.program_id(0)
        _join_owner_blocks(w_ref, w_full)

        @pl.when(i == 0)
        def _():
            gacc_ref[...] = jnp.zeros_like(gacc_ref)

        @pl.when(i % tps == 0)
        def _():
            macc_ref[...] = jnp.zeros_like(macc_ref)

        pieces = [da_ref[...], dg_ref[...]] + [r[j] for r in (dq_ref, dk_ref, dv_ref) for j in range(NS)]
        dproj = jnp.concatenate(pieces, axis=1).astype(MXU_DTYPE)
        dproj_ref[...] = dproj
        dh = _dot_nt(dproj, w_full[...])
        x = x_ref[...]
        r = lax.rsqrt(jnp.mean(x * x, axis=-1, keepdims=True) + EPS)
        xr = x * r
        macc_ref[0:1, :] += _colsum(dh)
        macc_ref[1:2, :] += _colsum(dh * (xr * g_ref[...]))
        dn = dh * (1.0 + mod_ref[:, D:2 * D])
        gacc_ref[0:1, :] += _colsum(dn * xr)
        e = dn * g_ref[...]
        dx_ref[...] = dx1_ref[...] + r * e - xr * (r * jnp.mean(e * xr, axis=-1, keepdims=True))

    row = lambda w: pl.BlockSpec((tm, w), lambda i: (i, 0))
    slab = pl.BlockSpec((NS, tm, LANES), lambda i: (0, i, 0))
    return pl.pallas_call(
        body, grid=(T // tm,), name="in_bwd",
        in_specs=[row(DC), row(DC), slab, slab, slab, row(D), row(D),
                  pl.BlockSpec((None, 1, N_MOD * D), lambda i: (i // tps, 0, 0)),
                  pl.BlockSpec((1, D), lambda i: (0, 0)),
                  pl.BlockSpec((P, D, Nb), lambda i: (0, 0, 0))],
        out_specs=[row(D), row(n_in), pl.BlockSpec((None, 8, D), lambda i: (i // tps, 0, 0)),
                   pl.BlockSpec((8, D), lambda i: (0, 0))],
        out_shape=[jax.ShapeDtypeStruct((T, D), F32), jax.ShapeDtypeStruct((T, n_in), MXU_DTYPE),
                   jax.ShapeDtypeStruct((Bl, 8, D), F32), jax.ShapeDtypeStruct((8, D), F32)],
        scratch_shapes=[pltpu.VMEM((D, n_in), w_in.dtype)],
        compiler_params=_cp(("arbitrary",)),
    )(da, dg, dq, dk, dv, x2, dx1, mod, g_mix, w_in)


def _wgrad(a, b, *, P, name, tk, split=None, host=None):
    a_blk, b_blk = a.ndim == 3, b.ndim == 3
    plan, h_in, h_out = host if host is not None else (None, (), ())
    ni, no = len(h_in), len(h_out)
    T = a.shape[-2]
    if a_blk:
        R, C = a.shape[2], b.shape[1]

        def accumulate(a_ref, b_ref, o_ref):
            for p in range(P):
                o_ref[p] += _dot_tn(a_ref[p], b_ref[...])
    elif b_blk:
        R, C = a.shape[1], b.shape[2]

        def accumulate(a_ref, b_ref, o_ref):
            for p in range(P):
                o_ref[p] += _dot_tn(a_ref[...], b_ref[p])
    elif split == "a":
        R, C = a.shape[1] // P, b.shape[1]

        def accumulate(a_ref, b_ref, o_ref):
            o_ref[...] += _dot_tn(a_ref[...], b_ref[...])
    else:
        R, C = a.shape[1], b.shape[1] // P
        per = 1 if C % MXU_COLS == 0 else 2
        assert P % per == 0 and (per * C) % MXU_COLS == 0

        def accumulate(a_ref, b_ref, o_ref):
            for p0 in range(0, P, per):
                full = _dot_tn(a_ref[...], b_ref[:, C * p0:C * (p0 + per)])
                for j in range(per):
                    o_ref[p0 + j] += full[:, C * j:C * (j + 1)]

    n_steps = T // tk

    def body(a_ref, b_ref, *rest):
        hin, o_ref, hout, sems = rest[:ni], rest[ni], rest[ni + 1:ni + 1 + no], rest[ni + 1 + no:]
        step = pl.program_id(0)

        @pl.when(step == 0)
        def _():
            o_ref[...] = jnp.zeros_like(o_ref)
            if plan is not None:
                plan.start(hin, hout, sems)

        if plan is not None:
            @pl.when(step == n_steps // 2)
            def _():
                plan.forward(hin, hout, sems)

        accumulate(a_ref, b_ref, o_ref)

        if plan is not None:
            @pl.when(step == n_steps - 1)
            def _():
                plan.finish(hin, hout, sems)

    def spec(v):
        if v.ndim == 3:
            return pl.BlockSpec((P, tk, v.shape[2]), lambda k: (0, k, 0))
        return pl.BlockSpec((tk, v.shape[1]), lambda k: (k, 0))

    anyspec = pl.BlockSpec(memory_space=pl.ANY)
    o_shape = (P * R, C) if split == "a" else (P, R, C)
    res = pl.pallas_call(
        body, grid=(n_steps,), name=name,
        in_specs=[spec(a), spec(b)] + [anyspec] * ni,
        out_specs=[pl.BlockSpec(o_shape, lambda k: (0,) * len(o_shape))] + [anyspec] * no,
        out_shape=[jax.ShapeDtypeStruct(o_shape, F32)] + list(h_out),
        scratch_shapes=plan.scratch() if plan is not None else [],
        compiler_params=_cp(("arbitrary",)),
    )(a, b, *h_in)
    res = [res[0].reshape(P, R, C), *res[1:]]
    return res if plan is not None else res[0]


TM_IN = 512
TM_FFN = 256
TK_WGRAD = 1024


def _alibi_slabs(n_slab):
    heads = 2 * n_slab
    slopes = 2.0 ** (-8.0 * np.arange(1, heads + 1) / heads)
    return jnp.asarray(np.broadcast_to(np.repeat(slopes.reshape(n_slab, 1, 2), HEAD_DIM, axis=2), (n_slab, 8, LANES)),
                       dtype=F32)


def _local_step(x, tgt, mod, g_mix, wdw, g_ln, b_ln, g_q, g_k, g_ffn, w_in, w_out, w_gate, w_up, w_down,
                pc_idx=None):
    Bl, S, D = x.shape
    T = Bl * S
    DC = g_ln.shape[1]
    P = w_in.shape[0]
    n_slab = (D - DC) // LANES
    x2 = x.reshape(T, D)
    t2 = tgt.reshape(T, D)
    mod3 = mod.reshape(Bl, 1, N_MOD * D)
    gq2 = jnp.tile(g_q, (1, LANES // HEAD_DIM))
    gk2 = jnp.tile(g_k, (1, LANES // HEAD_DIM))
    slopes = _alibi_slabs(n_slab)

    ag, qkv, qkh, h1 = _fwd_in(x2, mod3, g_mix, gq2, gk2, w_in, S=S, tm=TM_IN, n_ag=2 * DC)
    cv = _conv_fwd(ag, wdw, Bl=Bl, S=S, DC=DC)
    if pc_idx is not None:
        ya, lse, w_out, w_gate, w_up, w_down = _attn_fwd(qkh, qkv, slopes, Bl=Bl, S=S,
                                                         hosted=(w_out, w_gate, w_up, w_down))
    else:
        ya, lse = _attn_fwd(qkh, qkv, slopes, Bl=Bl, S=S)
    x1, ycat, mixb, h2, gs, us, acts, fo, dy, lossb = _ffn_fwd(
        x2, cv, ya, t2, mod3, g_ln, b_ln, g_ffn, w_out, w_gate, w_up, w_down, S=S, tm=TM_FFN)
    dgs, dus, dfb, dx1, dmixb, dya, dcv, macc_f, gacc_f, lacc = _ffn_bwd(
        dy, x1, gs, us, fo, mixb, cv, mod3, g_ln, b_ln, g_ffn, w_out, w_gate, w_up, w_down, S=S, tm=TM_FFN)
    wg = functools.partial(_wgrad, P=P, tk=TK_WGRAD)
    out = {}
    if pc_idx is None:
        grads = dict(w_down=wg(acts, dfb, name="wgrad_down", split="a"), w_gate=wg(dgs, h2, name="wgrad_gate", split="a"),
                     w_up=wg(dus, h2, name="wgrad_up", split="a"), w_out=wg(ycat, dmixb, name="wgrad_out", split="a"))
        dq, dk, dv, gqk = _attn_bwd(qkh, qkv, ya, lse, dya, gq2, gk2, slopes, Bl=Bl, S=S)
    else:
        g_down = wg(acts, dfb, name="wgrad_down", split="a")
        g_gate, r_down = wg(dgs, h2, name="wgrad_gate", split="a", host=_sibling_host([g_down]))
        g_up, r_gate = wg(dus, h2, name="wgrad_up", split="a", host=_sibling_host([g_gate]))
        g_out, r_up = wg(ycat, dmixb, name="wgrad_out", split="a", host=_sibling_host([g_up]))
        (r_out,) = _rs_sibling([g_out], "rs_sibling_out")
        grads = dict(w_down=g_down, w_gate=g_gate, w_up=g_up, w_out=g_out)
        sums = _pair_add([grads[nm] for nm in EARLY_WEIGHTS], [r_down, r_gate, r_up, r_out], pc_idx, "pair_add_early")
        res = _attn_bwd(qkh, qkv, ya, lse, dya, gq2, gk2, slopes, Bl=Bl, S=S, hosted=tuple(sb for _, sb in sums))
        dq, dk, dv, gqk = res[:4]
        out["early_sums"] = [s32 for s32, _ in sums]
        out["early_recv"] = list(res[4:])
    da, dg, dwdw = _conv_bwd(ag, dcv, wdw, Bl=Bl, S=S, DC=DC)
    dx, dprojb, macc_m, gacc_m = _in_bwd(da, dg, dq, dk, dv, x2, dx1, mod3, g_mix, w_in, S=S, tm=TM_IN)
    packed = _pack_small(macc_m, macc_f, gacc_m, gacc_f, lacc, gqk, dwdw, lossb)
    if pc_idx is None:
        grads["w_in"] = wg(h1, dprojb, name="wgrad_in", split="b")
    else:
        grads["w_in"], out["gathered_small"] = wg(h1, dprojb, name="wgrad_in", split="b",
                                                  host=_small_gather_host(packed))
    out.update(dx=dx.reshape(Bl, S, D), grads=grads, packed=packed)
    return out


EARLY_WEIGHTS = ("w_down", "w_gate", "w_up", "w_out")


def _small_layout(Bl):
    return 8 * Bl, 8 * Bl + 8, 8 * Bl + 8 + CONV_ROWS


def _pack_small(macc_m, macc_f, gacc_m, gacc_f, lacc, gqk, dwdw, lossb):
    Bl, _, D = macc_m.shape
    DC = lacc.shape[1]
    assert 2 * DC <= D
    SMALL_GAIN_ROW, SMALL_TAP_ROW, SMALL_ROWS = _small_layout(Bl)

    def body(mm_ref, mf_ref, gm_ref, gf_ref, la_ref, qk_ref, dw_ref, loss_ref, o_ref):
        o_ref[...] = jnp.zeros_like(o_ref)
        for b in range(Bl):
            o_ref[8 * b + 0:8 * b + 2, :] = mm_ref[b, 0:2, :]
            o_ref[8 * b + 2:8 * b + 3, :] = mf_ref[b, 3:4, :]
            o_ref[8 * b + 3:8 * b + 6, :] = mf_ref[b, 0:3, :]
        r = SMALL_GAIN_ROW
        o_ref[r:r + 1, :] = gm_ref[0:1, :]
        o_ref[r + 1:r + 2, :] = gf_ref[0:1, :]
        o_ref[r + 2:r + 3, 0:DC] = la_ref[0:1, :]
        o_ref[r + 2:r + 3, DC:2 * DC] = la_ref[1:2, :]
        qk = qk_ref[0:2, 0:HEAD_DIM] + qk_ref[0:2, HEAD_DIM:2 * HEAD_DIM]
        o_ref[r + 3:r + 4, 0:HEAD_DIM] = qk[0:1, :]
        o_ref[r + 3:r + 4, HEAD_DIM:2 * HEAD_DIM] = qk[1:2, :]
        o_ref[r + 4:r + 5, 0:LANES] = loss_ref[0:1, :]
        o_ref[SMALL_TAP_ROW:SMALL_TAP_ROW + CONV_ROWS, 0:DC] = dw_ref[...]

    return pl.pallas_call(body, name="pack_small", out_shape=jax.ShapeDtypeStruct((SMALL_ROWS, D), F32),
                          compiler_params=_cp())(macc_m, macc_f, gacc_m, gacc_f, lacc, gqk, dwdw, lossb)


def _row_tile(rows, cap=512):
    if rows <= cap:
        return rows
    best = rows
    for t in range(8, cap + 1, 8):
        if rows % t == 0:
            best = t
    return best


def _cast_weights(ws, pidx, name):
    n = len(ws)
    halves = [(w.shape[0] // 2, w.shape[1]) for w in ws]

    def body(p_ref, *refs):
        for k in range(n):
            refs[n + k][...] = refs[k][...].astype(MXU_DTYPE)

    return pl.pallas_call(
        body, name=name,
        grid_spec=pltpu.PrefetchScalarGridSpec(
            num_scalar_prefetch=1, grid=(2,),
            in_specs=[pl.BlockSpec(h, lambda i, p: (i, 0)) for h in halves],
            out_specs=[pl.BlockSpec((None,) + h, lambda i, p: (p[0], i, 0)) for h in halves]),
        out_shape=[jax.ShapeDtypeStruct((4,) + w.shape, MXU_DTYPE) for w in ws],
        compiler_params=_cp(),
    )(pidx, *ws)


def _pair_add(gs, recvs, pc_idx, name):
    n = len(gs)
    P = gs[0].shape[0]
    halves = [(g.shape[1] // 2, g.shape[2]) for g in gs]

    def body(pc_ref, *refs):
        for k in range(n):
            g_ref, r_ref, o_ref, ob_ref = refs[k], refs[n + k], refs[2 * n + 2 * k], refs[2 * n + 2 * k + 1]
            s = g_ref[...] + r_ref[...]
            ob_ref[...] = s.astype(jnp.bfloat16)

            @pl.when(pl.program_id(0) == pc_ref[0])
            def _(o_ref=o_ref, s=s):
                o_ref[...] = s

    res = pl.pallas_call(
        body, name=name,
        grid_spec=pltpu.PrefetchScalarGridSpec(
            num_scalar_prefetch=1, grid=(P,),
            in_specs=[pl.BlockSpec((None,) + h, lambda p, pc: (p, pc[1], 0)) for h in halves]
                     + [pl.BlockSpec((None,) + h, lambda p, pc: (p, 0, 0)) for h in halves],
            out_specs=[spec for h in halves for spec in (pl.BlockSpec(h, lambda p, pc: (0, 0)),
                                                         pl.BlockSpec((None,) + h, lambda p, pc: (p, 0, 0)))]),
        out_shape=[shape for h in halves for shape in (jax.ShapeDtypeStruct(h, F32),
                                                       jax.ShapeDtypeStruct((P,) + h, jnp.bfloat16))],
        compiler_params=_cp(),
    )(pc_idx, *gs, *recvs)
    return [(res[2 * k], res[2 * k + 1]) for k in range(n)]


def _final_add(owns, recvs, pc_idx, name):
    n = len(owns)

    def body(pc_ref, *refs):
        for k in range(n):
            acc = refs[k][...]
            for j in range(3):
                acc = acc + refs[n + k][j].astype(F32)
            refs[2 * n + k][...] = acc

    return pl.pallas_call(
        body, name=name,
        grid_spec=pltpu.PrefetchScalarGridSpec(
            num_scalar_prefetch=1, grid=(1,),
            in_specs=[pl.BlockSpec(o.shape, lambda i, pc: (0, 0)) for o in owns]
                     + [pl.BlockSpec((3,) + o.shape, lambda i, pc: (0, 0, 0)) for o in owns],
            out_specs=[pl.BlockSpec(o.shape, lambda i, pc: (pc[1], 0)) for o in owns]),
        out_shape=[jax.ShapeDtypeStruct((2 * o.shape[0], o.shape[1]), F32) for o in owns],
        compiler_params=_cp(),
    )(pc_idx, *owns, *recvs)


def _adamw_update(w_ref, g_ref, m_ref, v_ref, d_ref, nm_ref, nv_ref):
    c1 = 1.0 - ADAM_B1 ** ADAM_STEP
    c2 = 1.0 - ADAM_B2 ** ADAM_STEP
    gg = g_ref[...]
    nm = ADAM_B1 * m_ref[...] + (1.0 - ADAM_B1) * gg
    nv = ADAM_B2 * v_ref[...] + (1.0 - ADAM_B2) * (gg * gg)
    nm_ref[...] = nm
    nv_ref[...] = nv
    d_ref[...] = -ADAM_LR * ((nm / c1) / (jnp.sqrt(nv / c2) + ADAM_EPS) + ADAM_WD * w_ref[...])


def _adamw(w, g, m, v, name):
    R, C = w.shape
    tr = _row_tile(R, 256)
    spec = pl.BlockSpec((tr, C), lambda i: (i, 0))
    return pl.pallas_call(
        functools.partial(_adamw_update), grid=(R // tr,), name=name,
        in_specs=[spec] * 4, out_specs=[spec] * 3,
        out_shape=[jax.ShapeDtypeStruct((R, C), F32)] * 3,
    )(w, g, m, v)


def _adamw_small(quads, name):
    n = len(quads)

    def body(*refs):
        for k in range(n):
            _adamw_update(*refs[4 * k:4 * k + 4], *refs[4 * n + 3 * k:4 * n + 3 * k + 3])

    whole = lambda a: pl.BlockSpec(a.shape, lambda i: (0, 0))
    res = pl.pallas_call(
        body, grid=(1,), name=name,
        in_specs=[whole(a) for q in quads for a in q],
        out_specs=[whole(q[0]) for q in quads for _ in range(3)],
        out_shape=[jax.ShapeDtypeStruct(q[0].shape, F32) for q in quads for _ in range(3)],
    )(*[a for q in quads for a in q])
    return [tuple(res[3 * k:3 * k + 3]) for k in range(n)]


def _startup(first, w_ada, b_cols, w_in_buf, *, Bl):
    rows, D = first.shape
    NA = w_ada.shape[1]
    n_dev = 8
    g_w = _WeightGather([w_in_buf.shape])
    g_c = _SmallGather(rows)
    g_m = _SmallGather(n_dev * Bl)

    def body(first_ref, wada_ref, b_ref, win_in, g0_ref, call_ref, gm_ref, win_out, modp,
             ws0, ws1, cs0, cs1, cs2, ms0, ms1, ms2):
        g_w.start([win_out], (ws0, ws1))
        for phase in (g_c.start, g_c.forward, g_c.finish):
            phase([first_ref], [g0_ref], (cs0, cs1, cs2))
        for d in range(n_dev):
            call_ref[Bl * d:Bl * (d + 1), :] = g0_ref[rows * d:rows * d + Bl, :]
        c = call_ref[...]
        modp[...] = jnp.dot(c * _sigmoid(c), wada_ref[...], preferred_element_type=F32,
                            precision=lax.Precision.HIGH) + b_ref[...]
        for phase in (g_m.start, g_m.forward, g_m.finish):
            phase([modp], [gm_ref], (ms0, ms1, ms2))
        g_w.forward([win_out], (ws0, ws1))
        g_w.finish([win_out], (ws0, ws1))

    vmem = pl.BlockSpec(memory_space=pltpu.VMEM)
    anyspec = pl.BlockSpec(memory_space=pl.ANY)
    return pl.pallas_call(
        body, name="startup",
        in_specs=[vmem, vmem, vmem, anyspec], out_specs=[vmem, vmem, vmem, anyspec],
        out_shape=[jax.ShapeDtypeStruct((n_dev * rows, D), F32), jax.ShapeDtypeStruct((n_dev * Bl, D), F32),
                   jax.ShapeDtypeStruct((n_dev * n_dev * Bl, NA), F32),
                   jax.ShapeDtypeStruct(w_in_buf.shape, w_in_buf.dtype)],
        input_output_aliases={3: 3},
        scratch_shapes=[pltpu.VMEM((n_dev * Bl, NA), F32)] + g_w.scratch() + g_c.scratch() + g_m.scratch(),
        compiler_params=_cp(),
    )(first, w_ada, b_cols, w_in_buf)


def _ada_bwd(c_all, dmod_cols):
    def body(c_ref, d_ref, o_ref):
        c = c_ref[...]
        o_ref[...] = _dot_tn((c * _sigmoid(c)).astype(MXU_DTYPE), d_ref[...].astype(MXU_DTYPE))
    return pl.pallas_call(
        body, name="ada_bwd", out_shape=jax.ShapeDtypeStruct((c_all.shape[1], dmod_cols.shape[1]), F32),
        compiler_params=_cp(),
    )(c_all, dmod_cols)


def _small_reduce(gathered, n_dev, Bl):
    mod_rows, _, rows = _small_layout(Bl)
    width = gathered.shape[1]

    def body(g_ref, red_ref, bada_ref):
        acc = g_ref[0:rows, :]
        for d in range(1, n_dev):
            acc = acc + g_ref[d * rows:(d + 1) * rows, :]
        red_ref[...] = acc[mod_rows:, :]
        b = acc[0:8, :]
        for q in range(1, Bl):
            b = b + acc[8 * q:8 * q + 8, :]
        bada_ref[...] = b
    return pl.pallas_call(
        body, name="small_reduce",
        out_shape=[jax.ShapeDtypeStruct((rows - mod_rows, width), F32), jax.ShapeDtypeStruct((8, width), F32)],
        compiler_params=_cp(),
    )(gathered)


def _mesh_pos():
    return lax.axis_index("x"), lax.axis_index("y"), lax.axis_index("c")


def _other_chips(x, y):
    return [(1 - x, y), (x, 1 - y), (1 - x, 1 - y)]


class _WeightGather:
    def __init__(self, shapes):
        self.shapes = shapes
        self.n = len(shapes)

    def scratch(self):
        return [pltpu.SemaphoreType.DMA((6 * self.n,)), pltpu.SemaphoreType.DMA((6 * self.n,))]

    def _copy(self, outs, sems, w, k, slot, h, to):
        r2 = self.shapes[w][1] // 2
        blk = outs[w].at[slot, pl.ds(h * r2, r2), :]
        return pltpu.make_async_remote_copy(
            src_ref=blk, dst_ref=blk, send_sem=sems[0].at[6 * w + k], recv_sem=sems[1].at[6 * w + k],
            device_id=to, device_id_type=MESH_DEV)

    def start(self, outs, sems):
        x, y, c = _mesh_pos()
        for w in range(self.n):
            for k, chip in enumerate(_other_chips(x, y)):
                self._copy(outs, sems, w, k, 2 * x + y, c, (*chip, c)).start()

    def forward(self, outs, sems):
        x, y, c = _mesh_pos()
        for w in range(self.n):
            for k, chip in enumerate(_other_chips(x, y)):
                slot = 2 * chip[0] + chip[1]
                self._copy(outs, sems, w, k, slot, c, (x, y, 1 - c)).wait_recv()
                self._copy(outs, sems, w, 3 + k, slot, c, (x, y, 1 - c)).start()

    def finish(self, outs, sems):
        x, y, c = _mesh_pos()
        for w in range(self.n):
            for k, chip in enumerate(_other_chips(x, y)):
                slot = 2 * chip[0] + chip[1]
                self._copy(outs, sems, w, 3 + k, slot, 1 - c, (x, y, 1 - c)).wait_recv()
                self._copy(outs, sems, w, k, 2 * x + y, c, (*chip, c)).wait_send()
                self._copy(outs, sems, w, 3 + k, slot, c, (x, y, 1 - c)).wait_send()


class _SiblingExchange:
    def __init__(self, shapes):
        self.shapes = shapes

    def scratch(self):
        n = sum(s[0] for s in self.shapes)
        return [pltpu.SemaphoreType.DMA((n,)), pltpu.SemaphoreType.DMA((n,))]

    def out_shapes(self, dtype):
        return [jax.ShapeDtypeStruct((s[0], s[1] // 2, s[2]), dtype) for s in self.shapes]

    def _copies(self, ins, outs, sems):
        x, y, c = _mesh_pos()
        cps, k = [], 0
        for w, (P, R, _) in enumerate(self.shapes):
            r2 = R // 2
            for p in range(P):
                cps.append(pltpu.make_async_remote_copy(
                    src_ref=ins[w].at[p, pl.ds((1 - c) * r2, r2), :], dst_ref=outs[w].at[p],
                    send_sem=sems[0].at[k], recv_sem=sems[1].at[k],
                    device_id=(x, y, 1 - c), device_id_type=MESH_DEV))
                k += 1
        return cps

    def start(self, ins, outs, sems):
        for cp in self._copies(ins, outs, sems):
            cp.start()

    def forward(self, ins, outs, sems):
        pass

    def finish(self, ins, outs, sems):
        for cp in self._copies(ins, outs, sems):
            cp.wait()


def _sibling_host(grads):
    plan = _SiblingExchange([g.shape for g in grads])
    return plan, tuple(grads), tuple(plan.out_shapes(grads[0].dtype))


def _rs_sibling(grads, name):
    n = len(grads)
    plan, _, out_shapes = _sibling_host(grads)

    def body(*refs):
        ins, outs, sems = refs[:n], refs[n:2 * n], refs[2 * n:]
        plan.start(ins, outs, sems)
        plan.finish(ins, outs, sems)

    anyspec = pl.BlockSpec(memory_space=pl.ANY)
    return pl.pallas_call(
        body, name=name, out_shape=list(out_shapes),
        in_specs=[anyspec] * n, out_specs=[anyspec] * n, scratch_shapes=plan.scratch(),
    )(*grads)


class _SmallGather:
    def __init__(self, m_per):
        self.m = m_per

    def scratch(self):
        return [pltpu.SemaphoreType.DMA((7,)), pltpu.SemaphoreType.DMA((7,)), pltpu.SemaphoreType.DMA]

    def _rows(self, out, pos):
        px, py, pc = pos
        return out.at[pl.ds((4 * px + 2 * py + pc) * self.m, self.m), :]

    def _copy(self, out, sems, k, block, to, src=None):
        dst = self._rows(out, block)
        return pltpu.make_async_remote_copy(
            src_ref=dst if src is None else src, dst_ref=dst, send_sem=sems[0].at[k], recv_sem=sems[1].at[k],
            device_id=to, device_id_type=MESH_DEV)

    def start(self, ins, outs, sems):
        x, y, c = _mesh_pos()
        me = (x, y, c)
        pltpu.make_async_copy(ins[0], self._rows(outs[0], me), sems[2]).start()
        self._copy(outs[0], sems, 0, me, (x, y, 1 - c), src=ins[0]).start()
        for j, chip in enumerate(_other_chips(x, y)):
            self._copy(outs[0], sems, 1 + j, me, (*chip, c), src=ins[0]).start()

    def forward(self, ins, outs, sems):
        x, y, c = _mesh_pos()
        for j, chip in enumerate(_other_chips(x, y)):
            self._copy(outs[0], sems, 1 + j, (*chip, c), (x, y, c)).wait_recv()
            self._copy(outs[0], sems, 4 + j, (*chip, c), (x, y, 1 - c)).start()

    def finish(self, ins, outs, sems):
        x, y, c = _mesh_pos()
        me = (x, y, c)
        self._copy(outs[0], sems, 0, (x, y, 1 - c), me).wait_recv()
        for j, chip in enumerate(_other_chips(x, y)):
            self._copy(outs[0], sems, 4 + j, (*chip, 1 - c), me).wait_recv()
        self._copy(outs[0], sems, 0, me, (x, y, 1 - c), src=ins[0]).wait_send()
        for j, chip in enumerate(_other_chips(x, y)):
            self._copy(outs[0], sems, 1 + j, me, (*chip, c), src=ins[0]).wait_send()
            self._copy(outs[0], sems, 4 + j, (*chip, c), (x, y, 1 - c)).wait_send()
        pltpu.make_async_copy(ins[0], self._rows(outs[0], me), sems[2]).wait()


def _small_gather_host(packed):
    m, n = packed.shape
    return _SmallGather(m), (packed,), (jax.ShapeDtypeStruct((8 * m, n), packed.dtype),)


class _ChipExchange:
    def __init__(self, n):
        self.n = n

    def scratch(self):
        return [pltpu.SemaphoreType.DMA((3 * self.n,)), pltpu.SemaphoreType.DMA((3 * self.n,))]

    def _copies(self, ins, outs, sems):
        x, y, c = _mesh_pos()
        return [pltpu.make_async_remote_copy(
            src_ref=ins[w].at[2 * chip[0] + chip[1]], dst_ref=outs[w].at[k],
            send_sem=sems[0].at[3 * w + k], recv_sem=sems[1].at[3 * w + k],
            device_id=(*chip, c), device_id_type=MESH_DEV)
            for w in range(self.n) for k, chip in enumerate(_other_chips(x, y))]

    def start(self, ins, outs, sems):
        for cp in self._copies(ins, outs, sems):
            cp.start()

    def forward(self, ins, outs, sems):
        pass

    def finish(self, ins, outs, sems):
        for cp in self._copies(ins, outs, sems):
            cp.wait()


def _rs_final(bufs, name, chips=()):
    n, nc = len(bufs), len(chips)
    plan = _ChipExchange(nc)

    def body(*refs):
        cin = refs[n:n + nc]
        outs = refs[n + nc:2 * n + nc]
        cout = refs[2 * n + nc:2 * n + 2 * nc]
        send_sems, recv_sems = refs[2 * n + 2 * nc:2 * n + 2 * nc + 2]
        csems = refs[2 * n + 2 * nc + 2:]
        x, y, c = _mesh_pos()
        if nc:
            plan.start(cin, cout, csems)
        cps = []
        for w in range(n):
            r2 = bufs[w].shape[0] // 2
            mine = outs[w].at[pl.ds(c * r2, r2), :]
            cps.append(pltpu.make_async_remote_copy(
                src_ref=mine, dst_ref=mine, send_sem=send_sems.at[w], recv_sem=recv_sems.at[w],
                device_id=(x, y, 1 - c), device_id_type=MESH_DEV))
            cps[-1].start()
        for cp in cps:
            cp.wait()
        if nc:
            plan.finish(cin, cout, csems)

    anyspec = pl.BlockSpec(memory_space=pl.ANY)
    return pl.pallas_call(
        body, name=name,
        out_shape=[jax.ShapeDtypeStruct(b.shape, b.dtype) for b in bufs]
                  + [jax.ShapeDtypeStruct((3,) + s.shape[1:], s.dtype) for s in chips],
        in_specs=[anyspec] * (n + nc), out_specs=[anyspec] * (n + nc),
        input_output_aliases={w: w for w in range(n)},
        scratch_shapes=[pltpu.SemaphoreType.DMA((n,)), pltpu.SemaphoreType.DMA((n,))] + (plan.scratch() if nc else []),
    )(*bufs, *chips)


BIG = ("w_in", "w_out", "w_gate", "w_up", "w_down")
TRANSPOSED = ("w_gate", "w_up")
WEIGHTS = ("w_ada", "b_ada", "g_mix", "w_in", "w_dw", "b_dw", "g_conv_ln", "b_conv_ln", "g_q", "g_k",
           "w_out", "g_ffn", "w_gate", "w_up", "w_down")


def _pad_to(a, rows, cols):
    return jnp.pad(a, ((0, rows - a.shape[0]), (0, cols - a.shape[1])))


def kernel(x, c, w_ada, b_ada, g_mix, w_in, w_dw, b_dw, g_conv_ln, b_conv_ln, g_q, g_k, w_out, g_ffn, w_gate, w_up, w_down, loss_target, m_w_ada, m_b_ada, m_g_mix, m_w_in, m_w_dw, m_b_dw, m_g_conv_ln, m_b_conv_ln, m_g_q, m_g_k, m_w_out, m_g_ffn, m_w_gate, m_w_up, m_w_down, v_w_ada, v_b_ada, v_g_mix, v_w_in, v_w_dw, v_b_dw, v_g_conv_ln, v_b_conv_ln, v_g_q, v_g_k, v_w_out, v_g_ffn, v_w_gate, v_w_up, v_w_down):
    w = dict(w_ada=w_ada, b_ada=b_ada, g_mix=g_mix, w_in=w_in, w_dw=w_dw, b_dw=b_dw, g_conv_ln=g_conv_ln,
             b_conv_ln=b_conv_ln, g_q=g_q, g_k=g_k, w_out=w_out, g_ffn=g_ffn, w_gate=w_gate, w_up=w_up, w_down=w_down)
    m = dict(w_ada=m_w_ada, b_ada=m_b_ada, g_mix=m_g_mix, w_in=m_w_in, w_dw=m_w_dw, b_dw=m_b_dw, g_conv_ln=m_g_conv_ln,
             b_conv_ln=m_b_conv_ln, g_q=m_g_q, g_k=m_g_k, w_out=m_w_out, g_ffn=m_g_ffn, w_gate=m_w_gate, w_up=m_w_up,
             w_down=m_w_down)
    v = dict(w_ada=v_w_ada, b_ada=v_b_ada, g_mix=v_g_mix, w_in=v_w_in, w_dw=v_w_dw, b_dw=v_b_dw, g_conv_ln=v_g_conv_ln,
             b_conv_ln=v_b_conv_ln, g_q=v_g_q, g_k=v_g_k, w_out=v_w_out, g_ffn=v_g_ffn, w_gate=v_w_gate, w_up=v_w_up,
             w_down=v_w_down)
    Bl, S, D = x.shape
    DC = g_conv_ln.shape[1]
    NA = w_ada.shape[2]
    xi, yi, ci = _mesh_pos()
    p = 2 * xi + yi
    dev = 2 * p + ci
    n_dev = 8
    pidx = jnp.reshape(p, (1,)).astype(jnp.int32)
    pc_idx = jnp.stack([p, ci]).astype(jnp.int32)

    first = jnp.concatenate([_pad_to(c, 8, D), _pad_to(w_dw[0], CONV_ROWS, D)], axis=0)
    shard = lambda a, nm: a[0].T if nm in TRANSPOSED else a[0]
    owned = dict(zip(BIG, _cast_weights([shard(w[nm], nm) for nm in BIG], pidx, "cast_weights")))
    b_cols = lax.dynamic_slice_in_dim(b_ada, p * NA, NA, axis=1)
    g0, c_all, gm, w_in_full = _startup(first, w_ada[0], b_cols, owned["w_in"], Bl=Bl)
    g0 = g0.reshape(n_dev, 8 + CONV_ROWS, D)
    taps = jnp.concatenate([g0[2 * q, 8:, :w_dw.shape[2]] for q in range(4)], axis=1)
    wdw = jnp.where(lax.broadcasted_iota(jnp.int32, taps.shape, 0) == CONV_WIDTH, b_dw, taps)
    gm = gm.reshape(n_dev, n_dev * Bl, NA)
    mod = jnp.concatenate([lax.dynamic_slice_in_dim(gm[2 * q], dev * Bl, Bl, axis=0) for q in range(4)], axis=1)

    loc = _local_step(x, loss_target, mod, g_mix, wdw, g_conv_ln, b_conv_ln, g_q, g_k, g_ffn,
                      w_in_full, owned["w_out"], owned["w_gate"], owned["w_up"], owned["w_down"], pc_idx=pc_idx)

    halves = _final_add(loc["early_sums"], loc["early_recv"], pc_idx, "final_add_early")
    (late_sib,) = _rs_sibling([loc["grads"]["w_in"]], "rs_sibling_in")
    ((late32, late16),) = _pair_add([loc["grads"]["w_in"]], [late_sib], pc_idx, "pair_add_w_in")
    *early_full, late_recv = _rs_final(halves, "rs_final_early", chips=(late16,))
    grad = dict(zip(EARLY_WEIGHTS, early_full))
    grad["w_in"], = _rs_final(_final_add([late32], [late_recv], pc_idx, "final_add_w_in"), "rs_final_in")

    mod_rows, _, small_rows = _small_layout(Bl)
    gs = loc["gathered_small"]
    red, bada8 = _small_reduce(gs, n_dev, Bl)
    dmod_all = gs.reshape(n_dev, small_rows, D)[:, :mod_rows].reshape(n_dev * Bl, 8, D)[:, :N_MOD].reshape(n_dev * Bl, N_MOD * D)
    grad["w_ada"] = _ada_bwd(c_all, lax.dynamic_slice_in_dim(dmod_all, p * NA, NA, axis=1))
    grad["b_ada"] = bada8[:N_MOD].reshape(1, N_MOD * D)
    grad["g_mix"] = red[0:1]
    grad["g_ffn"] = red[1:2]
    grad["g_conv_ln"] = red[2:3, :DC]
    grad["b_conv_ln"] = red[2:3, DC:2 * DC]
    grad["g_q"] = red[3:4, :HEAD_DIM]
    grad["g_k"] = red[3:4, HEAD_DIM:2 * HEAD_DIM]
    loss = red[4, 0]
    dwdw = red[8:8 + CONV_ROWS, :DC]
    grad["w_dw"] = lax.dynamic_slice_in_dim(dwdw[:CONV_WIDTH], p * w_dw.shape[2], w_dw.shape[2], axis=1)
    grad["b_dw"] = dwdw[CONV_WIDTH:CONV_WIDTH + 1]

    delta, new_m, new_v = {}, {}, {}
    two_d = lambda nm: w[nm].shape[-2:]
    small = [nm for nm in WEIGHTS if nm not in BIG and nm != "w_ada"]
    small_res = dict(zip(small, _adamw_small(
        [tuple(a.reshape(two_d(nm)) for a in (w[nm], grad[nm], m[nm], v[nm])) for nm in small], "adamw_small")))
    for nm in WEIGHTS:
        shp = w[nm].shape
        if nm in TRANSPOSED:
            d_, m_, v_ = _adamw(w[nm][0].T, grad[nm], m[nm][0].T, v[nm][0].T, "adamw_" + nm)
            grad[nm], delta[nm], new_m[nm], new_v[nm] = (a.T.reshape(shp) for a in (grad[nm], d_, m_, v_))
            continue
        if nm in small_res:
            d_, m_, v_ = small_res[nm]
        else:
            d_, m_, v_ = _adamw(*(a.reshape(two_d(nm)) for a in (w[nm], grad[nm], m[nm], v[nm])), "adamw_" + nm)
        grad[nm] = grad[nm].reshape(shp)
        delta[nm], new_m[nm], new_v[nm] = d_.reshape(shp), m_.reshape(shp), v_.reshape(shp)

    return (loss, loc["dx"], *[grad[nm] for nm in WEIGHTS], *[delta[nm] for nm in WEIGHTS],
            *[new_m[nm] for nm in WEIGHTS], *[new_v[nm] for nm in WEIGHTS])
```

```python
import functools

import jax
import jax.numpy as jnp
import numpy as np
from jax import lax
from jax.experimental import pallas as pl
from jax.experimental.pallas import tpu as pltpu

F32 = jnp.float32
MXU_DTYPE = jnp.bfloat16
ACT_DTYPE = jnp.bfloat16
EPS = 1e-6
NEG_INF = -1e30
HEAD_DIM = 64
LANES = 128
MXU_COLS = 256
RADIUS = 64
QBLK = 128
DILATIONS = (1, 4, 16)
CONV_WIDTH = 31
CONV_PAD = CONV_WIDTH // 2
CONV_ROWS = 32
N_MOD = 6
ADAM_LR, ADAM_B1, ADAM_B2, ADAM_EPS, ADAM_WD, ADAM_STEP = 0.001, 0.9, 0.999, 1e-08, 0.01, 10
MESH_DEV = pl.DeviceIdType.MESH
VMEM_LIMIT = 56 << 20
ATTN_BWD_VMEM = 60 << 20


def _cp(sem=None, vmem=VMEM_LIMIT):
    kw = dict(vmem_limit_bytes=vmem)
    if sem is not None:
        kw["dimension_semantics"] = sem
    return pltpu.CompilerParams(**kw)


def _sigmoid(x):
    return 1.0 / (1.0 + jnp.exp(-x))


def _dot(a, b):
    return jnp.dot(a, b, preferred_element_type=F32)


def _dot_nt(a, b):
    return lax.dot_general(a, b, (((1,), (1,)), ((), ())), preferred_element_type=F32)


def _dot_tn(a, b):
    return lax.dot_general(a, b, (((0,), (0,)), ((), ())), preferred_element_type=F32)


def _colsum(v):
    return jnp.sum(v, axis=0, keepdims=True)


def _load_resident(i, pairs, sems):
    @pl.when(i == 0)
    def _():
        cps = [pltpu.make_async_copy(src, dst, sems.at[n]) for n, (src, dst) in enumerate(pairs)]
        for c in cps:
            c.start()
        for c in cps:
            c.wait()


def _join_owner_blocks(w_ref, w_full):
    P, _, Nb = w_ref.shape

    @pl.when(pl.program_id(0) == 0)
    def _():
        for p in range(P):
            w_full[:, Nb * p:Nb * (p + 1)] = w_ref[p]


def _fwd_in(x2, mod, g_mix, gq2, gk2, w_in, *, S, tm, n_ag):
    T, D = x2.shape
    P, _, Nb = w_in.shape
    n_in = P * Nb
    n_slab = (n_in - n_ag) // LANES
    NS = n_slab // 3
    tps = S // tm

    def body(x_ref, mod_ref, g_ref, gq_ref, gk_ref, w_ref, ag_ref, qkv_ref, qkh_ref, h_ref, w_full):
        _join_owner_blocks(w_ref, w_full)
        x = x_ref[...]
        r = lax.rsqrt(jnp.mean(x * x, axis=-1, keepdims=True) + EPS)
        n = x * r * g_ref[...]
        h = n * (1.0 + mod_ref[:, D:2 * D]) + mod_ref[:, 0:D]
        hb = h.astype(MXU_DTYPE)
        h_ref[...] = hb
        proj = _dot(hb, w_full[...])
        ag_ref[...] = proj[:, :n_ag]
        mm = _head_mean_matrix()
        for j in range(n_slab):
            v = proj[:, n_ag + LANES * j:n_ag + LANES * (j + 1)]
            qkv_ref[j] = v
            if j < 2 * NS:
                gain = gq_ref[...] * (HEAD_DIM ** -0.5 * LOG2E) if j < NS else gk_ref[...]
                qkh_ref[j] = v * lax.rsqrt(_head_mean(v * v, mm) + EPS) * gain

    return pl.pallas_call(
        body, grid=(T // tm,), name="fwd_in",
        in_specs=[pl.BlockSpec((tm, D), lambda i: (i, 0)),
                  pl.BlockSpec((None, 1, N_MOD * D), lambda i: (i // tps, 0, 0)),
                  pl.BlockSpec((1, D), lambda i: (0, 0)),
                  pl.BlockSpec((1, LANES), lambda i: (0, 0)), pl.BlockSpec((1, LANES), lambda i: (0, 0)),
                  pl.BlockSpec((P, D, Nb), lambda i: (0, 0, 0))],
        out_specs=[pl.BlockSpec((tm, n_ag), lambda i: (i, 0)),
                   pl.BlockSpec((n_slab, tm, LANES), lambda i: (0, i, 0)),
                   pl.BlockSpec((2 * NS, tm, LANES), lambda i: (0, i, 0)),
                   pl.BlockSpec((tm, D), lambda i: (i, 0))],
        out_shape=[jax.ShapeDtypeStruct((T, n_ag), F32),
                   jax.ShapeDtypeStruct((n_slab, T, LANES), F32),
                   jax.ShapeDtypeStruct((2 * NS, T, LANES), F32),
                   jax.ShapeDtypeStruct((T, D), MXU_DTYPE)],
        scratch_shapes=[pltpu.VMEM((D, n_in), w_in.dtype)],
        compiler_params=_cp(("arbitrary",)),
    )(x2, mod, g_mix, gq2, gk2, w_in)


CONV_CH = 128


def _conv_taps(win, w_ref, acc, reverse):
    n = win.shape[0]
    for b in range(8):
        wb = win if b == 0 else pltpu.roll(win, shift=n - b, axis=0)
        for a in range(4):
            o = 8 * a + b
            if o < 1 or o > CONV_WIDTH:
                continue
            k = (CONV_WIDTH - o) if reverse else (o - 1)
            acc = acc + w_ref[k:k + 1, :] * wb[8 * a:8 * a + CONV_CH, :]
    return acc


def _conv_fwd(ag, wdw, *, Bl, S, DC):
    T = ag.shape[0]
    nsc = DC // LANES
    CH = CONV_CH

    def body(a_ref, g_ref, w_ref, cv_ref, upad):
        zeros16 = jnp.zeros((16, LANES), F32)
        upad[0:16, :] = zeros16
        upad[S + 16:S + 32, :] = zeros16

        def fill(i, _):
            r0 = pl.multiple_of(i * CH, CH)
            a = a_ref[pl.ds(r0, CH), :]
            g = g_ref[pl.ds(r0, CH), :]
            upad[pl.ds(r0 + 16, CH), :] = a * _sigmoid(g)
            return 0
        lax.fori_loop(0, S // CH, fill, 0)

        def conv(i, _):
            r0 = pl.multiple_of(i * CH, CH)
            win = upad[pl.ds(r0, CH + 32), :]
            acc = jnp.zeros((CH, LANES), F32) + w_ref[CONV_WIDTH:CONV_WIDTH + 1, :]
            cv_ref[pl.ds(r0, CH), :] = _conv_taps(win, w_ref, acc, reverse=False)
            return 0
        lax.fori_loop(0, S // CH, conv, 0)

    return pl.pallas_call(
        body, grid=(Bl, nsc), name="conv_fwd",
        in_specs=[pl.BlockSpec((S, LANES), lambda b, j: (b, j)),
                  pl.BlockSpec((S, LANES), lambda b, j: (b, nsc + j)),
                  pl.BlockSpec((CONV_ROWS, LANES), lambda b, j: (0, j))],
        out_specs=pl.BlockSpec((S, LANES), lambda b, j: (b, j)),
        out_shape=jax.ShapeDtypeStruct((T, DC), F32),
        scratch_shapes=[pltpu.VMEM((S + 32, LANES), F32)],
        compiler_params=_cp(("arbitrary", "arbitrary")),
    )(ag, ag, wdw)


def _conv_bwd(ag, dcv, wdw, *, Bl, S, DC):
    T = ag.shape[0]
    nsc = DC // LANES
    CH = CONV_CH

    def body(a_ref, g_ref, d_ref, w_ref, da_ref, dg_ref, dw_ref, upad, dpad, wacc):
        b = pl.program_id(1)
        zeros16 = jnp.zeros((16, LANES), F32)
        upad[0:16, :] = zeros16
        upad[S + 16:S + 32, :] = zeros16
        dpad[0:16, :] = zeros16
        dpad[S + 16:S + 32, :] = zeros16

        @pl.when(b == 0)
        def _():
            wacc[...] = jnp.zeros_like(wacc)

        def fill(i, _):
            r0 = pl.multiple_of(i * CH, CH)
            a = a_ref[pl.ds(r0, CH), :]
            g = g_ref[pl.ds(r0, CH), :]
            upad[pl.ds(r0 + 16, CH), :] = a * _sigmoid(g)
            dpad[pl.ds(r0 + 16, CH), :] = d_ref[pl.ds(r0, CH), :]
            return 0
        lax.fori_loop(0, S // CH, fill, 0)

        def step(i, _):
            r0 = pl.multiple_of(i * CH, CH)
            dwin = dpad[pl.ds(r0, CH + 32), :]
            du = _conv_taps(dwin, w_ref, jnp.zeros((CH, LANES), F32), reverse=True)
            a = a_ref[pl.ds(r0, CH), :]
            g = g_ref[pl.ds(r0, CH), :]
            sg = _sigmoid(g)
            da_ref[pl.ds(r0, CH), :] = du * sg
            dg_ref[pl.ds(r0, CH), :] = du * a * sg * (1.0 - sg)
            dc = d_ref[pl.ds(r0, CH), :]
            uwin = upad[pl.ds(r0, CH + 32), :]
            n = CH + 32
            for bb in range(8):
                wb = uwin if bb == 0 else pltpu.roll(uwin, shift=n - bb, axis=0)
                for aa in range(4):
                    o = 8 * aa + bb
                    if o < 1 or o > CONV_WIDTH:
                        continue
                    k = o - 1
                    prod = dc * wb[8 * aa:8 * aa + CH, :]
                    part = prod[0:8, :]
                    for q in range(1, CH // 8):
                        part = part + prod[8 * q:8 * q + 8, :]
                    wacc[8 * k:8 * k + 8, :] += part
            part = dc[0:8, :]
            for q in range(1, CH // 8):
                part = part + dc[8 * q:8 * q + 8, :]
            wacc[8 * CONV_WIDTH:8 * CONV_WIDTH + 8, :] += part
            return 0
        lax.fori_loop(0, S // CH, step, 0)

        @pl.when(b == Bl - 1)
        def _():
            for k in range(CONV_ROWS):
                dw_ref[k:k + 1, :] = jnp.sum(wacc[8 * k:8 * k + 8, :], axis=0, keepdims=True)

    return pl.pallas_call(
        body, grid=(nsc, Bl), name="conv_bwd",
        in_specs=[pl.BlockSpec((S, LANES), lambda j, b: (b, j)),
                  pl.BlockSpec((S, LANES), lambda j, b: (b, nsc + j)),
                  pl.BlockSpec((S, LANES), lambda j, b: (b, j)),
                  pl.BlockSpec((CONV_ROWS, LANES), lambda j, b: (0, j))],
        out_specs=[pl.BlockSpec((S, LANES), lambda j, b: (b, j)),
                   pl.BlockSpec((S, LANES), lambda j, b: (b, j)),
                   pl.BlockSpec((CONV_ROWS, LANES), lambda j, b: (0, j))],
        out_shape=[jax.ShapeDtypeStruct((T, DC), F32), jax.ShapeDtypeStruct((T, DC), F32),
                   jax.ShapeDtypeStruct((CONV_ROWS, DC), F32)],
        scratch_shapes=[pltpu.VMEM((S + 32, LANES), F32), pltpu.VMEM((S + 32, LANES), F32),
                        pltpu.VMEM((8 * CONV_ROWS, LANES), F32)],
        compiler_params=_cp(("arbitrary", "arbitrary")),
    )(ag, ag, dcv, wdw)


ROWCH = 256


LOG2E = 1.4426950408889634
LN2 = 0.6931471805599453
N_EDGE = 4


def _head_mean_matrix():
    r = lax.broadcasted_iota(jnp.int32, (LANES, LANES), 0) // HEAD_DIM
    c = lax.broadcasted_iota(jnp.int32, (LANES, LANES), 1) // HEAD_DIM
    return jnp.where(r == c, 1.0 / HEAD_DIM, 0.0).astype(jnp.bfloat16)


def _head_mean(v, mm):
    hi = v.astype(jnp.bfloat16)
    lo = (v - hi.astype(F32)).astype(jnp.bfloat16)
    return _dot(hi, mm) + _dot(lo, mm)


def _stack_heads(blk, lane_lo):
    z = jnp.zeros_like(blk)
    return jnp.concatenate([jnp.where(lane_lo, blk, z), jnp.where(lane_lo, z, blk)], axis=0)


def _merge_heads(v2, lane_lo):
    return jnp.where(lane_lo, v2[:QBLK], v2[QBLK:])


def _bias_tables(bias_ref, slope_ref):
    row = lax.broadcasted_iota(jnp.int32, (2 * QBLK, 2 * QBLK), 0)
    col = lax.broadcasted_iota(jnp.int32, (2 * QBLK, 2 * QBLK), 1)
    rel = jnp.abs(col - RADIUS - (row % QBLK))
    slope = jnp.where(row < QBLK, slope_ref[0:1, 0:1], slope_ref[0:1, HEAD_DIM:HEAD_DIM + 1]) * LOG2E
    for pi, d in enumerate(DILATIONS):
        inside = jnp.where(rel <= RADIUS, -slope * (float(d) * rel.astype(F32)), NEG_INF)
        for e in range(N_EDGE):
            t = inside
            if e & 1:
                t = jnp.where(col < RADIUS, NEG_INF, t)
            if e & 2:
                t = jnp.where(col >= QBLK + RADIUS, NEG_INF, t)
            bias_ref[N_EDGE * pi + e] = t


def _edge_index(qb, nb):
    return jnp.where(qb == 0, 1, 0) + jnp.where(qb == nb - 1, 2, 0)


VIA = 4


def _residue(d, s):
    return (s % VIA) * VIA + s // VIA if d == VIA * VIA else s


def _gather_rows(src_ref, dst_ref, S, d, pad, f32_copy=None):
    n = S // d
    seg = n + 2 * RADIUS if pad else n
    step = min(n, 512)
    two_step = d == VIA * VIA and f32_copy is not None
    for s in range(d):
        base = s * seg
        if pad:
            dst_ref[base:base + RADIUS, :] = jnp.zeros((RADIUS, LANES), dst_ref.dtype)
            dst_ref[base + RADIUS + n:base + seg, :] = jnp.zeros((RADIUS, LANES), dst_ref.dtype)
            base += RADIUS
        for c0 in range(0, n, step):
            if d == 1:
                v = src_ref[c0:c0 + step, :]
            elif two_step:
                v = f32_copy[pl.ds((s // VIA) * (S // VIA) + s % VIA + c0 * VIA, step, stride=VIA), :]
            else:
                v = src_ref[pl.ds(_residue(d, s) + c0 * d, step, stride=d), :]
                if d == VIA and f32_copy is not None:
                    f32_copy[s * n + c0:s * n + c0 + step, :] = v
            dst_ref[base + c0:base + c0 + step, :] = v.astype(dst_ref.dtype)


def _scatter_rows(src_ref, dst_ref, S, d, pad, accumulate, f32_tmp=None):
    n = S // d
    seg = n + 2 * RADIUS if pad else n
    first = RADIUS if pad else 0
    _unpermute(lambda s, c0, step: src_ref[s * seg + first + c0:s * seg + first + c0 + step, :],
               dst_ref, S, d, accumulate, f32_tmp)


def _unpermute(rows_of, dst_ref, S, d, accumulate, f32_tmp):
    n = S // d
    step = min(n, 512)
    if d == VIA * VIA and f32_tmp is not None:
        for s in range(d):
            f32_tmp[pl.ds((s // VIA) * (S // VIA) + s % VIA, n, stride=VIA), :] = rows_of(s, 0, n)
        n4 = S // VIA
        _unpermute(lambda s, c0, st: f32_tmp[s * n4 + c0:s * n4 + c0 + st, :], dst_ref, S, VIA, accumulate, None)
        return
    for s in range(d):
        for c0 in range(0, n, step):
            v = rows_of(s, c0, step)
            idx = pl.ds(c0, step) if d == 1 else pl.ds(_residue(d, s) + c0 * d, step, stride=d)
            if accumulate:
                dst_ref[idx, :] = dst_ref[idx, :] + v
            else:
                dst_ref[idx, :] = v


def _zero_uncovered(acc, S, d):
    n = S // d
    if (n // QBLK) % 2:
        return
    seg = n + 2 * RADIUS
    for r in range(d):
        acc[0, r * seg + n:r * seg + seg, :] = jnp.zeros((2 * RADIUS, LANES), F32)
        acc[1, r * seg:r * seg + 2 * RADIUS, :] = jnp.zeros((2 * RADIUS, LANES), F32)


def _scatter_parity(acc, dst_ref, S, d, f32_tmp=None):
    n = S // d
    seg = n + 2 * RADIUS
    one_block = (n // QBLK) % 2 == 1

    def rows_of(s, c0, step):
        rows = slice(s * seg + RADIUS + c0, s * seg + RADIUS + c0 + step)
        return acc[s % 2, rows, :] if one_block else acc[0, rows, :] + acc[1, rows, :]

    _unpermute(rows_of, dst_ref, S, d, True, f32_tmp)


PIPE_UNROLL = 4
PIPE_SLOTS = 16
BWD_SLOTS = 12


def _pipeline(n_items, stages, unroll):
    K = len(stages)
    assert n_items % unroll == 0 and K * unroll <= (PIPE_SLOTS if K == 4 else BWD_SLOTS)
    trips = n_items // unroll
    assert trips >= K - 1

    def trip(t, static):
        for s in reversed(range(K)):
            if static and not 0 <= t - s < trips:
                continue
            for u in range(unroll):
                item = unroll * (t - s) + u
                stages[s](jnp.int32(item) if static else item)

    for t in range(K - 1):
        trip(t, True)

    def full(t, carry):
        trip(t, False)
        return carry
    lax.fori_loop(K - 1, trips, full, 0)
    for t in range(trips, trips + K - 1):
        trip(t, True)


def _attn_fwd(qkh, qkv, slopes, *, Bl, S, hosted=()):
    n3, T, _ = qkv.shape
    NS = n3 // 3
    NB = S // QBLK
    PADR = S + 2 * RADIUS * DILATIONS[-1]
    nh = len(hosted)
    plan = _WeightGather([b.shape for b in hosted]) if nh else None
    n_steps = Bl * NS

    def body(qh, kh, v_ref, slope_ref, *rest):
        o_ref, lse_ref = rest[nh:nh + 2]
        wouts = rest[nh + 2:2 * nh + 2]
        (qp, kp, vp, op, lp, onat, lnat, bias_ref, sbuf, pbuf, mbuf, lbuf, tmps) = rest[2 * nh + 2:2 * nh + 15]
        sems = rest[2 * nh + 15:]
        step = pl.program_id(0) * Bl + pl.program_id(1)
        if nh:
            @pl.when(step == 0)
            def _():
                plan.start(wouts, sems)

            @pl.when(step == (7 * n_steps) // 8)
            def _():
                plan.forward(wouts, sems)

        lane_lo = lax.broadcasted_iota(jnp.int32, (QBLK, LANES), 1) < HEAD_DIM

        @pl.when(pl.program_id(1) == 0)
        def _():
            _bias_tables(bias_ref, slope_ref)

        for pi, d in enumerate(DILATIONS):
            n = S // d
            nb = n // QBLK
            _gather_rows(qh, qp, S, d, pad=False, f32_copy=tmps.at[0])
            _gather_rows(kh, kp, S, d, pad=True, f32_copy=tmps.at[1])
            _gather_rows(v_ref, vp, S, d, pad=True, f32_copy=tmps.at[2])

            def offsets(i, nb=nb):
                r = i // nb
                return pl.multiple_of(i * QBLK, QBLK), pl.multiple_of((i + r) * QBLK, QBLK), i % nb

            def scores(i, pi=pi, nb=nb):
                q0, k0, qb = offsets(i)
                qs = _stack_heads(qp[pl.ds(q0, QBLK), :], lane_lo)
                sbuf[i % PIPE_SLOTS] = (_dot_nt(qs, kp[pl.ds(k0, 2 * QBLK), :])
                                        + bias_ref[N_EDGE * pi + _edge_index(qb, nb)])

            def rowmax(i):
                m = jnp.max(sbuf[i % PIPE_SLOTS], axis=1, keepdims=True)
                mbuf[i % PIPE_SLOTS] = jnp.broadcast_to(m, (2 * QBLK, LANES))

            def expsum(i):
                m = mbuf[i % PIPE_SLOTS]
                p = jnp.exp2(sbuf[i % PIPE_SLOTS] - jnp.concatenate([m, m], axis=1))
                pbuf[i % PIPE_SLOTS] = p.astype(MXU_DTYPE)
                lbuf[i % PIPE_SLOTS] = jnp.broadcast_to(jnp.sum(p, axis=1, keepdims=True), (2 * QBLK, LANES))

            def values(i):
                q0, k0, _ = offsets(i)
                l = lbuf[i % PIPE_SLOTS]
                o2 = _dot(pbuf[i % PIPE_SLOTS], vp[pl.ds(k0, 2 * QBLK), :]) * (1.0 / l)
                op[pl.ds(q0, QBLK), :] = _merge_heads(o2, lane_lo)
                lp[pl.ds(q0, QBLK), :] = _merge_heads(mbuf[i % PIPE_SLOTS] + jnp.log2(l), lane_lo)

            _pipeline(NB, [scores, rowmax, expsum, values], PIPE_UNROLL)
            _scatter_rows(op, onat.at[pi], S, d, pad=False, accumulate=False, f32_tmp=tmps.at[0])
            _scatter_rows(lp, lnat.at[pi], S, d, pad=False, accumulate=False, f32_tmp=tmps.at[1])

        for c0 in range(0, S, ROWCH):
            ls = [lnat[pi, c0:c0 + ROWCH, :] for pi in range(len(DILATIONS))]
            mx = jnp.maximum(jnp.maximum(ls[0], ls[1]), ls[2])
            es = [jnp.exp2(l - mx) for l in ls]
            tot = es[0] + es[1] + es[2]
            inv = 1.0 / tot
            acc = (es[0] * inv) * onat[0, c0:c0 + ROWCH, :]
            for pi in (1, 2):
                acc = acc + (es[pi] * inv) * onat[pi, c0:c0 + ROWCH, :]
            o_ref[c0:c0 + ROWCH, :] = acc
            lse_ref[c0:c0 + ROWCH, :] = mx + jnp.log2(tot)

        if nh:
            @pl.when(step == n_steps - 1)
            def _():
                plan.finish(wouts, sems)

    spec_in = lambda off: pl.BlockSpec((None, S, LANES), lambda j, b: (off * NS + j, b, 0))
    out = pl.BlockSpec((S, LANES), lambda j, b: (b, j))
    anyspec = pl.BlockSpec(memory_space=pl.ANY)
    return pl.pallas_call(
        body, grid=(NS, Bl), name="attn_fwd",
        in_specs=[spec_in(0), spec_in(1), spec_in(2),
                  pl.BlockSpec((None, 8, LANES), lambda j, b: (j, 0, 0))] + [anyspec] * nh,
        out_specs=[out, out] + [anyspec] * nh,
        out_shape=[jax.ShapeDtypeStruct((T, NS * LANES), F32)] * 2
                  + [jax.ShapeDtypeStruct(b.shape, b.dtype) for b in hosted],
        input_output_aliases={4 + w: 2 + w for w in range(nh)},
        scratch_shapes=[pltpu.VMEM((S, LANES), MXU_DTYPE), pltpu.VMEM((PADR, LANES), MXU_DTYPE),
                        pltpu.VMEM((PADR, LANES), MXU_DTYPE),
                        pltpu.VMEM((S, LANES), F32), pltpu.VMEM((S, LANES), F32),
                        pltpu.VMEM((3, S, LANES), F32), pltpu.VMEM((3, S, LANES), F32),
                        pltpu.VMEM((N_EDGE * len(DILATIONS), 2 * QBLK, 2 * QBLK), F32),
                        pltpu.VMEM((PIPE_SLOTS, 2 * QBLK, 2 * QBLK), F32),
                        pltpu.VMEM((PIPE_SLOTS, 2 * QBLK, 2 * QBLK), MXU_DTYPE),
                        pltpu.VMEM((PIPE_SLOTS, 2 * QBLK, LANES), F32), pltpu.VMEM((PIPE_SLOTS, 2 * QBLK, LANES), F32),
                        pltpu.VMEM((3, S, LANES), F32)]
                       + (plan.scratch() if nh else []),
        compiler_params=_cp(("arbitrary", "arbitrary")),
    )(qkh, qkh, qkv, slopes, *hosted)


def _attn_bwd(qkh, qkv, o, lse, do, gq2, gk2, slopes, *, Bl, S, hosted=()):
    n3, T, _ = qkv.shape
    NS = n3 // 3
    NB = S // QBLK
    PADR = S + 2 * RADIUS * DILATIONS[-1]
    QSCALE = HEAD_DIM ** -0.5
    nh = len(hosted)
    plan = _ChipExchange(nh)
    n_steps = Bl * NS

    def body(qh, kh, q_ref, k_ref, v_ref, o_ref, lse_ref, do_ref, gq_ref, gk_ref, slope_ref, *rest):
        hin = rest[:nh]
        dq_ref, dk_ref, dv_ref, gacc_ref = rest[nh:nh + 4]
        hout = rest[nh + 4:2 * nh + 4]
        (ld, qp, kp, vp, dop, ldp, dqp, dkacc, dvacc, dqn, dkn, bias_ref,
         sbuf, dpbuf, pbuf, dsbuf, tmps) = rest[2 * nh + 4:2 * nh + 21]
        sems = rest[2 * nh + 21:]
        step = pl.program_id(0) * Bl + pl.program_id(1)

        @pl.when(step == 0)
        def _():
            gacc_ref[...] = jnp.zeros_like(gacc_ref)
            if nh:
                plan.start(hin, hout, sems)

        mm = _head_mean_matrix()
        lane_lo = lax.broadcasted_iota(jnp.int32, (QBLK, LANES), 1) < HEAD_DIM

        @pl.when(pl.program_id(1) == 0)
        def _():
            _bias_tables(bias_ref, slope_ref)

        lse_lanes = lax.broadcasted_iota(jnp.int32, (ROWCH, LANES), 1) % HEAD_DIM < HEAD_DIM // 2
        for c0 in range(0, S, ROWCH):
            delta = _head_mean(do_ref[c0:c0 + ROWCH, :] * o_ref[c0:c0 + ROWCH, :], mm) * HEAD_DIM
            ld[c0:c0 + ROWCH, :] = jnp.where(lse_lanes, lse_ref[c0:c0 + ROWCH, :], delta)
            dqn[c0:c0 + ROWCH, :] = jnp.zeros((ROWCH, LANES), F32)
            dkn[c0:c0 + ROWCH, :] = jnp.zeros((ROWCH, LANES), F32)
            dv_ref[c0:c0 + ROWCH, :] = jnp.zeros((ROWCH, LANES), F32)

        for pi, d in enumerate(DILATIONS):
            n = S // d
            nb = n // QBLK
            _gather_rows(qh, qp, S, d, pad=False, f32_copy=tmps.at[0])
            _gather_rows(kh, kp, S, d, pad=True, f32_copy=tmps.at[1])
            _gather_rows(v_ref, vp, S, d, pad=True, f32_copy=tmps.at[2])
            _gather_rows(do_ref, dop, S, d, pad=False, f32_copy=tmps.at[3])
            _gather_rows(ld, ldp, S, d, pad=False, f32_copy=tmps.at[4])
            _zero_uncovered(dkacc, S, d)
            _zero_uncovered(dvacc, S, d)

            def offsets(i, nb=nb):
                r = i // nb
                return pl.multiple_of(i * QBLK, QBLK), pl.multiple_of((i + r) * QBLK, QBLK), i % nb

            def scores(i, pi=pi, nb=nb):
                q0, k0, qb = offsets(i)
                qs = _stack_heads(qp[pl.ds(q0, QBLK), :], lane_lo)
                dos = _stack_heads(dop[pl.ds(q0, QBLK), :], lane_lo)
                sbuf[i % BWD_SLOTS] = (_dot_nt(qs, kp[pl.ds(k0, 2 * QBLK), :])
                                       + bias_ref[N_EDGE * pi + _edge_index(qb, nb)])
                dpbuf[i % BWD_SLOTS] = _dot_nt(dos, vp[pl.ds(k0, 2 * QBLK), :])

            def probs(i):
                q0, _, _ = offsets(i)
                blk = ldp[pl.ds(q0, QBLK), :]
                half = HEAD_DIM // 2
                lcol = jnp.concatenate([blk[:, 0:1], blk[:, HEAD_DIM:HEAD_DIM + 1]], axis=0)
                dcol = jnp.concatenate([blk[:, half:half + 1], blk[:, HEAD_DIM + half:HEAD_DIM + half + 1]], axis=0)
                p = jnp.exp2(sbuf[i % BWD_SLOTS] - lcol)
                pbuf[i % BWD_SLOTS] = p.astype(MXU_DTYPE)
                dsbuf[i % BWD_SLOTS] = (p * (dpbuf[i % BWD_SLOTS] - dcol)).astype(MXU_DTYPE)

            def grads(i):
                q0, k0, _ = offsets(i)
                qs = _stack_heads(qp[pl.ds(q0, QBLK), :], lane_lo)
                dos = _stack_heads(dop[pl.ds(q0, QBLK), :], lane_lo)
                ds = dsbuf[i % BWD_SLOTS]
                dvacc[i % 2, pl.ds(k0, 2 * QBLK), :] = _dot_tn(pbuf[i % BWD_SLOTS], dos)
                dkacc[i % 2, pl.ds(k0, 2 * QBLK), :] = _dot_tn(ds, qs)
                dqp[pl.ds(q0, QBLK), :] = _merge_heads(_dot(ds, kp[pl.ds(k0, 2 * QBLK), :]), lane_lo)

            _pipeline(NB, [scores, probs, grads], PIPE_UNROLL)
            _scatter_rows(dqp, dqn, S, d, pad=False, accumulate=True, f32_tmp=tmps.at[0])
            _scatter_parity(dkacc, dkn, S, d, f32_tmp=tmps.at[1])
            _scatter_parity(dvacc, dv_ref, S, d, f32_tmp=tmps.at[2])

        gq_sum = jnp.zeros((8, LANES), F32)
        gk_sum = jnp.zeros((8, LANES), F32)
        for c0 in range(0, S, ROWCH):
            for src_ref, dn, g_ref, dst_ref, scale, is_q in ((q_ref, dqn, gq_ref, dq_ref, QSCALE, True),
                                                             (k_ref, dkn, gk_ref, dk_ref, LN2, False)):
                x = src_ref[c0:c0 + ROWCH, :]
                dh = dn[c0:c0 + ROWCH, :]
                rr = lax.rsqrt(_head_mean(x * x, mm) + EPS)
                e = dh * (g_ref[...] * scale)
                dst_ref[c0:c0 + ROWCH, :] = rr * e - x * (rr * rr * rr) * _head_mean(e * x, mm)
                gpart = dh * (x * rr * scale)
                acc8 = gpart[0:8, :]
                for q8 in range(1, ROWCH // 8):
                    acc8 = acc8 + gpart[8 * q8:8 * q8 + 8, :]
                if is_q:
                    gq_sum = gq_sum + acc8
                else:
                    gk_sum = gk_sum + acc8
        gacc_ref[0:1, :] += jnp.sum(gq_sum, axis=0, keepdims=True)
        gacc_ref[1:2, :] += jnp.sum(gk_sum, axis=0, keepdims=True)

        if nh:
            @pl.when(step == n_steps - 1)
            def _():
                plan.finish(hin, hout, sems)

    spec_in = lambda off: pl.BlockSpec((None, S, LANES), lambda j, b: (off * NS + j, b, 0))
    tok = pl.BlockSpec((S, LANES), lambda j, b: (b, j))
    vec = pl.BlockSpec((1, LANES), lambda j, b: (0, 0))
    slab_out = pl.BlockSpec((None, S, LANES), lambda j, b: (j, b, 0))
    f32buf = lambda rows: pltpu.VMEM((rows, LANES), F32)
    bfbuf = lambda rows: pltpu.VMEM((rows, LANES), MXU_DTYPE)
    anyspec = pl.BlockSpec(memory_space=pl.ANY)
    return pl.pallas_call(
        body, grid=(NS, Bl), name="attn_bwd",
        in_specs=[spec_in(0), spec_in(1), spec_in(0), spec_in(1), spec_in(2), tok, tok, tok, vec, vec,
                  pl.BlockSpec((None, 8, LANES), lambda j, b: (j, 0, 0))] + [anyspec] * nh,
        out_specs=[slab_out, slab_out, slab_out, pl.BlockSpec((8, LANES), lambda j, b: (0, 0))] + [anyspec] * nh,
        out_shape=[jax.ShapeDtypeStruct((NS, T, LANES), F32)] * 3 + [jax.ShapeDtypeStruct((8, LANES), F32)]
                  + [jax.ShapeDtypeStruct((3,) + h.shape[1:], h.dtype) for h in hosted],
        scratch_shapes=[f32buf(S),
                        bfbuf(S), bfbuf(PADR), bfbuf(PADR), bfbuf(S),
                        f32buf(S), f32buf(S),
                        pltpu.VMEM((2, PADR, LANES), F32), pltpu.VMEM((2, PADR, LANES), F32),
                        f32buf(S), f32buf(S),
                        pltpu.VMEM((N_EDGE * len(DILATIONS), 2 * QBLK, 2 * QBLK), F32),
                        pltpu.VMEM((BWD_SLOTS, 2 * QBLK, 2 * QBLK), F32),
                        pltpu.VMEM((BWD_SLOTS, 2 * QBLK, 2 * QBLK), F32),
                        pltpu.VMEM((BWD_SLOTS, 2 * QBLK, 2 * QBLK), MXU_DTYPE),
                        pltpu.VMEM((BWD_SLOTS, 2 * QBLK, 2 * QBLK), MXU_DTYPE),
                        pltpu.VMEM((5, S, LANES), F32)]
                       + (plan.scratch() if nh else []),
        compiler_params=_cp(("arbitrary", "arbitrary"), vmem=ATTN_BWD_VMEM),
    )(qkh, qkh, qkv, qkv, qkv, o, lse, do, gq2, gk2, slopes, *hosted)


def _layer_norm_parts(cv, g_ln, b_ln):
    mu = jnp.mean(cv, axis=-1, keepdims=True)
    cen = cv - mu
    rs = lax.rsqrt(jnp.mean(cen * cen, axis=-1, keepdims=True) + EPS)
    z = cen * rs
    return z, rs, z * g_ln + b_ln


def _ffn_fwd(x2, cv, ya, tgt, mod, g_ln, b_ln, g_ffn, w_out, w_gate, w_up, w_down, *, S, tm):
    T, D = x2.shape
    DC = cv.shape[1]
    P, Kb, _ = w_out.shape
    Fb = w_down.shape[1]
    tps = S // tm

    def body(x_ref, cv_ref, ya_ref, t_ref, mod_ref, gln_ref, bln_ref, gf_ref, wo_hbm, wg_hbm, wu_hbm, wd_hbm,
             x1_ref, ycat_ref, mix_ref, h2_ref, g_ref, u_ref, a_ref, f_ref, dy_ref, loss_ref,
             wo, wg, wu, wd, sems):
        i = pl.program_id(0)
        _load_resident(i, [(wo_hbm, wo), (wg_hbm, wg), (wu_hbm, wu), (wd_hbm, wd)], sems)

        @pl.when(i == 0)
        def _():
            loss_ref[...] = jnp.zeros_like(loss_ref)

        _, _, ln = _layer_norm_parts(cv_ref[...], gln_ref[...], bln_ref[...])
        yc = ln * _sigmoid(ln)
        ycat = jnp.concatenate([yc, ya_ref[...]], axis=1).astype(MXU_DTYPE)
        ycat_ref[...] = ycat
        mix = _dot(ycat, wo[...])
        mix_ref[...] = mix.astype(ACT_DTYPE)
        x1 = x_ref[...] + mod_ref[:, 2 * D:3 * D] * mix
        x1_ref[...] = x1
        r2 = lax.rsqrt(jnp.mean(x1 * x1, axis=-1, keepdims=True) + EPS)
        h2 = (x1 * r2 * gf_ref[...]) * (1.0 + mod_ref[:, 4 * D:5 * D]) + mod_ref[:, 3 * D:4 * D]
        h2b = h2.astype(MXU_DTYPE)
        h2_ref[...] = h2b
        g = _dot_nt(h2b, wg[...])
        u = _dot_nt(h2b, wu[...])
        a = (g * _sigmoid(g) * u).astype(MXU_DTYPE)
        g_ref[...] = g.astype(ACT_DTYPE)
        u_ref[...] = u.astype(ACT_DTYPE)
        a_ref[...] = a
        f = _dot(a, wd[...])
        f_ref[...] = f.astype(ACT_DTYPE)
        err = x1 + mod_ref[:, 5 * D:6 * D] * f - t_ref[...]
        dy_ref[...] = err * (1.0 / D)
        tot = jnp.sum(_colsum(err * err), axis=1, keepdims=True)
        loss_ref[...] += tot * (0.5 / D)

    row = lambda w: pl.BlockSpec((tm, w), lambda i: (i, 0))
    vec = lambda w: pl.BlockSpec((1, w), lambda i: (0, 0))
    F = P * Fb
    blk = row(F)
    w_gate, w_up, w_down = (w.reshape(F, D) for w in (w_gate, w_up, w_down))
    w_out = w_out.reshape(P * Kb, D)
    anyspec = pl.BlockSpec(memory_space=pl.ANY)
    return pl.pallas_call(
        body, grid=(T // tm,), name="ffn_fwd",
        in_specs=[row(D), row(DC), row(D - DC), row(D),
                  pl.BlockSpec((None, 1, N_MOD * D), lambda i: (i // tps, 0, 0)),
                  vec(DC), vec(DC), vec(D), anyspec, anyspec, anyspec, anyspec],
        out_specs=[row(D), row(D), row(D), row(D), blk, blk, blk, row(D), row(D),
                   pl.BlockSpec((8, LANES), lambda i: (0, 0))],
        out_shape=[jax.ShapeDtypeStruct((T, D), F32), jax.ShapeDtypeStruct((T, D), MXU_DTYPE),
                   jax.ShapeDtypeStruct((T, D), ACT_DTYPE), jax.ShapeDtypeStruct((T, D), MXU_DTYPE),
                   jax.ShapeDtypeStruct((T, F), ACT_DTYPE), jax.ShapeDtypeStruct((T, F), ACT_DTYPE),
                   jax.ShapeDtypeStruct((T, F), MXU_DTYPE), jax.ShapeDtypeStruct((T, D), ACT_DTYPE),
                   jax.ShapeDtypeStruct((T, D), F32), jax.ShapeDtypeStruct((8, LANES), F32)],
        scratch_shapes=[pltpu.VMEM(w_out.shape, w_out.dtype), pltpu.VMEM(w_gate.shape, w_gate.dtype),
                        pltpu.VMEM(w_up.shape, w_up.dtype), pltpu.VMEM(w_down.shape, w_down.dtype),
                        pltpu.SemaphoreType.DMA((4,))],
        compiler_params=_cp(("arbitrary",)),
    )(x2, cv, ya, tgt, mod, g_ln, b_ln, g_ffn, w_out, w_gate, w_up, w_down)


def _ffn_bwd(dy, x1, gs, us, fo, mixb, cv, mod, g_ln, b_ln, g_ffn, w_out, w_gate, w_up, w_down, *, S, tm):
    T, D = dy.shape
    DC = cv.shape[1]
    P, Kb, _ = w_out.shape
    Fb = w_down.shape[1]
    tps = S // tm
    Bl = T // S

    def body(dy_ref, x1_ref, g_ref, u_ref, f_ref, mix_ref, cv_ref, mod_ref, gln_ref, bln_ref, gf_ref,
             wo_hbm, wg_hbm, wu_hbm, wd_hbm,
             dg_ref, du_ref, df_ref, dx1_ref, dmix_ref, dya_ref, dcv_ref, macc_ref, gacc_ref, lacc_ref,
             wo, wg, wu, wd, sems):
        i = pl.program_id(0)
        _load_resident(i, [(wo_hbm, wo), (wg_hbm, wg), (wu_hbm, wu), (wd_hbm, wd)], sems)

        @pl.when(i == 0)
        def _():
            gacc_ref[...] = jnp.zeros_like(gacc_ref)
            lacc_ref[...] = jnp.zeros_like(lacc_ref)

        @pl.when(i % tps == 0)
        def _():
            macc_ref[...] = jnp.zeros_like(macc_ref)

        dy_t = dy_ref[...]
        x1 = x1_ref[...]
        gate_f = mod_ref[:, 5 * D:6 * D]
        macc_ref[2:3, :] += _colsum(dy_t * f_ref[...].astype(F32))
        dfb = (dy_t * gate_f).astype(MXU_DTYPE)
        df_ref[...] = dfb
        da = _dot_nt(dfb, wd[...])
        g = g_ref[...].astype(F32)
        u = u_ref[...].astype(F32)
        sg = _sigmoid(g)
        dgp = (da * u * (sg * (1.0 + g * (1.0 - sg)))).astype(MXU_DTYPE)
        dup = (da * (g * sg)).astype(MXU_DTYPE)
        dg_ref[...] = dgp
        du_ref[...] = dup
        dh2 = _dot(dgp, wg[...]) + _dot(dup, wu[...])
        r2 = lax.rsqrt(jnp.mean(x1 * x1, axis=-1, keepdims=True) + EPS)
        xr = x1 * r2
        n2 = xr * gf_ref[...]
        macc_ref[0:1, :] += _colsum(dh2)
        macc_ref[1:2, :] += _colsum(dh2 * n2)
        dn2 = dh2 * (1.0 + mod_ref[:, 4 * D:5 * D])
        gacc_ref[0:1, :] += _colsum(dn2 * xr)
        e = dn2 * gf_ref[...]
        dx1 = dy_t + r2 * e - xr * (r2 * jnp.mean(e * xr, axis=-1, keepdims=True))
        dx1_ref[...] = dx1
        macc_ref[3:4, :] += _colsum(dx1 * mix_ref[...].astype(F32))
        dmixb = (dx1 * mod_ref[:, 2 * D:3 * D]).astype(MXU_DTYPE)
        dmix_ref[...] = dmixb
        dycat = _dot_nt(dmixb, wo[...])
        dya_ref[...] = dycat[:, DC:]
        dyc = dycat[:, :DC]
        z, rs, ln = _layer_norm_parts(cv_ref[...], gln_ref[...], bln_ref[...])
        sg = _sigmoid(ln)
        dln = dyc * (sg * (1.0 + ln * (1.0 - sg)))
        lacc_ref[0:1, :] += _colsum(dln * z)
        lacc_ref[1:2, :] += _colsum(dln)
        dz = dln * gln_ref[...]
        dcv_ref[...] = rs * (dz - jnp.mean(dz, axis=-1, keepdims=True) - z * jnp.mean(dz * z, axis=-1, keepdims=True))

    row = lambda w: pl.BlockSpec((tm, w), lambda i: (i, 0))
    vec = lambda w: pl.BlockSpec((1, w), lambda i: (0, 0))
    F = P * Fb
    blk = row(F)
    w_gate, w_up, w_down = (w.reshape(F, D) for w in (w_gate, w_up, w_down))
    w_out = w_out.reshape(P * Kb, D)
    anyspec = pl.BlockSpec(memory_space=pl.ANY)
    return pl.pallas_call(
        body, grid=(T // tm,), name="ffn_bwd",
        in_specs=[row(D), row(D), blk, blk, row(D), row(D), row(DC),
                  pl.BlockSpec((None, 1, N_MOD * D), lambda i: (i // tps, 0, 0)),
                  vec(DC), vec(DC), vec(D), anyspec, anyspec, anyspec, anyspec],
        out_specs=[blk, blk, row(D), row(D), row(D), row(D - DC), row(DC),
                   pl.BlockSpec((None, 8, D), lambda i: (i // tps, 0, 0)),
                   pl.BlockSpec((8, D), lambda i: (0, 0)), pl.BlockSpec((8, DC), lambda i: (0, 0))],
        out_shape=[jax.ShapeDtypeStruct((T, F), MXU_DTYPE), jax.ShapeDtypeStruct((T, F), MXU_DTYPE),
                   jax.ShapeDtypeStruct((T, D), MXU_DTYPE), jax.ShapeDtypeStruct((T, D), F32),
                   jax.ShapeDtypeStruct((T, D), MXU_DTYPE), jax.ShapeDtypeStruct((T, D - DC), F32),
                   jax.ShapeDtypeStruct((T, DC), F32), jax.ShapeDtypeStruct((Bl, 8, D), F32),
                   jax.ShapeDtypeStruct((8, D), F32), jax.ShapeDtypeStruct((8, DC), F32)],
        scratch_shapes=[pltpu.VMEM(w_out.shape, w_out.dtype), pltpu.VMEM(w_gate.shape, w_gate.dtype),
                        pltpu.VMEM(w_up.shape, w_up.dtype), pltpu.VMEM(w_down.shape, w_down.dtype),
                        pltpu.SemaphoreType.DMA((4,))],
        compiler_params=_cp(("arbitrary",)),
    )(dy, x1, gs, us, fo, mixb, cv, mod, g_ln, b_ln, g_ffn, w_out, w_gate, w_up, w_down)


def _in_bwd(da, dg, dq, dk, dv, x2, dx1, mod, g_mix, w_in, *, S, tm):
    T, D = x2.shape
    P, _, Nb = w_in.shape
    DC = da.shape[1]
    NS = dq.shape[0]
    n_in = P * Nb
    tps = S // tm
    Bl = T // S

    def body(da_ref, dg_ref, dq_ref, dk_ref, dv_ref, x_ref, dx1_ref, mod_ref, g_ref, w_ref,
             dx_ref, dproj_ref, macc_ref, gacc_ref, w_full):
        i = pl.program_id(0)
        _join_owner_blocks(w_ref, w_full)

        @pl.when(i == 0)
        def _():
            gacc_ref[...] = jnp.zeros_like(gacc_ref)

        @pl.when(i % tps == 0)
        def _():
            macc_ref[...] = jnp.zeros_like(macc_ref)

        pieces = [da_ref[...], dg_ref[...]] + [r[j] for r in (dq_ref, dk_ref, dv_ref) for j in range(NS)]
        dproj = jnp.concatenate(pieces, axis=1).astype(MXU_DTYPE)
        dproj_ref[...] = dproj
        dh = _dot_nt(dproj, w_full[...])
        x = x_ref[...]
        r = lax.rsqrt(jnp.mean(x * x, axis=-1, keepdims=True) + EPS)
        xr = x * r
        macc_ref[0:1, :] += _colsum(dh)
        macc_ref[1:2, :] += _colsum(dh * (xr * g_ref[...]))
        dn = dh * (1.0 + mod_ref[:, D:2 * D])
        gacc_ref[0:1, :] += _colsum(dn * xr)
        e = dn * g_ref[...]
        dx_ref[...] = dx1_ref[...] + r * e - xr * (r * jnp.mean(e * xr, axis=-1, keepdims=True))

    row = lambda w: pl.BlockSpec((tm, w), lambda i: (i, 0))
    slab = pl.BlockSpec((NS, tm, LANES), lambda i: (0, i, 0))
    return pl.pallas_call(
        body, grid=(T // tm,), name="in_bwd",
        in_specs=[row(DC), row(DC), slab, slab, slab, row(D), row(D),
                  pl.BlockSpec((None, 1, N_MOD * D), lambda i: (i // tps, 0, 0)),
                  pl.BlockSpec((1, D), lambda i: (0, 0)),
                  pl.BlockSpec((P, D, Nb), lambda i: (0, 0, 0))],
        out_specs=[row(D), row(n_in), pl.BlockSpec((None, 8, D), lambda i: (i // tps, 0, 0)),
                   pl.BlockSpec((8, D), lambda i: (0, 0))],
        out_shape=[jax.ShapeDtypeStruct((T, D), F32), jax.ShapeDtypeStruct((T, n_in), MXU_DTYPE),
                   jax.ShapeDtypeStruct((Bl, 8, D), F32), jax.ShapeDtypeStruct((8, D), F32)],
        scratch_shapes=[pltpu.VMEM((D, n_in), w_in.dtype)],
        compiler_params=_cp(("arbitrary",)),
    )(da, dg, dq, dk, dv, x2, dx1, mod, g_mix, w_in)


def _wgrad(a, b, *, P, name, tk, split=None, host=None):
    a_blk, b_blk = a.ndim == 3, b.ndim == 3
    plan, h_in, h_out = host if host is not None else (None, (), ())
    ni, no = len(h_in), len(h_out)
    T = a.shape[-2]
    if a_blk:
        R, C = a.shape[2], b.shape[1]

        def accumulate(a_ref, b_ref, o_ref):
            for p in range(P):
                o_ref[p] += _dot_tn(a_ref[p], b_ref[...])
    elif b_blk:
        R, C = a.shape[1], b.shape[2]

        def accumulate(a_ref, b_ref, o_ref):
            for p in range(P):
                o_ref[p] += _dot_tn(a_ref[...], b_ref[p])
    elif split == "a":
        R, C = a.shape[1] // P, b.shape[1]

        def accumulate(a_ref, b_ref, o_ref):
            o_ref[...] += _dot_tn(a_ref[...], b_ref[...])
    else:
        R, C = a.shape[1], b.shape[1] // P
        per = 1 if C % MXU_COLS == 0 else 2
        assert P % per == 0 and (per * C) % MXU_COLS == 0

        def accumulate(a_ref, b_ref, o_ref):
            for p0 in range(0, P, per):
                full = _dot_tn(a_ref[...], b_ref[:, C * p0:C * (p0 + per)])
                for j in range(per):
                    o_ref[p0 + j] += full[:, C * j:C * (j + 1)]

    n_steps = T // tk

    def body(a_ref, b_ref, *rest):
        hin, o_ref, hout, sems = rest[:ni], rest[ni], rest[ni + 1:ni + 1 + no], rest[ni + 1 + no:]
        step = pl.program_id(0)

        @pl.when(step == 0)
        def _():
            o_ref[...] = jnp.zeros_like(o_ref)
            if plan is not None:
                plan.start(hin, hout, sems)

        if plan is not None:
            @pl.when(step == n_steps // 2)
            def _():
                plan.forward(hin, hout, sems)

        accumulate(a_ref, b_ref, o_ref)

        if plan is not None:
            @pl.when(step == n_steps - 1)
            def _():
                plan.finish(hin, hout, sems)

    def spec(v):
        if v.ndim == 3:
            return pl.BlockSpec((P, tk, v.shape[2]), lambda k: (0, k, 0))
        return pl.BlockSpec((tk, v.shape[1]), lambda k: (k, 0))

    anyspec = pl.BlockSpec(memory_space=pl.ANY)
    o_shape = (P * R, C) if split == "a" else (P, R, C)
    res = pl.pallas_call(
        body, grid=(n_steps,), name=name,
        in_specs=[spec(a), spec(b)] + [anyspec] * ni,
        out_specs=[pl.BlockSpec(o_shape, lambda k: (0,) * len(o_shape))] + [anyspec] * no,
        out_shape=[jax.ShapeDtypeStruct(o_shape, F32)] + list(h_out),
        scratch_shapes=plan.scratch() if plan is not None else [],
        compiler_params=_cp(("arbitrary",)),
    )(a, b, *h_in)
    res = [res[0].reshape(P, R, C), *res[1:]]
    return res if plan is not None else res[0]


TM_IN = 512
TM_FFN = 256
TK_WGRAD = 1024


def _alibi_slabs(n_slab):
    heads = 2 * n_slab
    slopes = 2.0 ** (-8.0 * np.arange(1, heads + 1) / heads)
    return jnp.asarray(np.broadcast_to(np.repeat(slopes.reshape(n_slab, 1, 2), HEAD_DIM, axis=2), (n_slab, 8, LANES)),
                       dtype=F32)


def _local_step(x, tgt, mod, g_mix, wdw, g_ln, b_ln, g_q, g_k, g_ffn, w_in, w_out, w_gate, w_up, w_down,
                pc_idx=None):
    Bl, S, D = x.shape
    T = Bl * S
    DC = g_ln.shape[1]
    P = w_in.shape[0]
    n_slab = (D - DC) // LANES
    x2 = x.reshape(T, D)
    t2 = tgt.reshape(T, D)
    mod3 = mod.reshape(Bl, 1, N_MOD * D)
    gq2 = jnp.tile(g_q, (1, LANES // HEAD_DIM))
    gk2 = jnp.tile(g_k, (1, LANES // HEAD_DIM))
    slopes = _alibi_slabs(n_slab)

    ag, qkv, qkh, h1 = _fwd_in(x2, mod3, g_mix, gq2, gk2, w_in, S=S, tm=TM_IN, n_ag=2 * DC)
    cv = _conv_fwd(ag, wdw, Bl=Bl, S=S, DC=DC)
    if pc_idx is not None:
        ya, lse, w_out, w_gate, w_up, w_down = _attn_fwd(qkh, qkv, slopes, Bl=Bl, S=S,
                                                         hosted=(w_out, w_gate, w_up, w_down))
    else:
        ya, lse = _attn_fwd(qkh, qkv, slopes, Bl=Bl, S=S)
    x1, ycat, mixb, h2, gs, us, acts, fo, dy, lossb = _ffn_fwd(
        x2, cv, ya, t2, mod3, g_ln, b_ln, g_ffn, w_out, w_gate, w_up, w_down, S=S, tm=TM_FFN)
    dgs, dus, dfb, dx1, dmixb, dya, dcv, macc_f, gacc_f, lacc = _ffn_bwd(
        dy, x1, gs, us, fo, mixb, cv, mod3, g_ln, b_ln, g_ffn, w_out, w_gate, w_up, w_down, S=S, tm=TM_FFN)
    wg = functools.partial(_wgrad, P=P, tk=TK_WGRAD)
    out = {}
    if pc_idx is None:
        grads = dict(w_down=wg(acts, dfb, name="wgrad_down", split="a"), w_gate=wg(dgs, h2, name="wgrad_gate", split="a"),
                     w_up=wg(dus, h2, name="wgrad_up", split="a"), w_out=wg(ycat, dmixb, name="wgrad_out", split="a"))
        dq, dk, dv, gqk = _attn_bwd(qkh, qkv, ya, lse, dya, gq2, gk2, slopes, Bl=Bl, S=S)
    else:
        g_down = wg(acts, dfb, name="wgrad_down", split="a")
        g_gate, r_down = wg(dgs, h2, name="wgrad_gate", split="a", host=_sibling_host([g_down]))
        g_up, r_gate = wg(dus, h2, name="wgrad_up", split="a", host=_sibling_host([g_gate]))
        g_out, r_up = wg(ycat, dmixb, name="wgrad_out", split="a", host=_sibling_host([g_up]))
        (r_out,) = _rs_sibling([g_out], "rs_sibling_out")
        grads = dict(w_down=g_down, w_gate=g_gate, w_up=g_up, w_out=g_out)
        sums = _pair_add([grads[nm] for nm in EARLY_WEIGHTS], [r_down, r_gate, r_up, r_out], pc_idx, "pair_add_early")
        res = _attn_bwd(qkh, qkv, ya, lse, dya, gq2, gk2, slopes, Bl=Bl, S=S, hosted=tuple(sb for _, sb in sums))
        dq, dk, dv, gqk = res[:4]
        out["early_sums"] = [s32 for s32, _ in sums]
        out["early_recv"] = list(res[4:])
    da, dg, dwdw = _conv_bwd(ag, dcv, wdw, Bl=Bl, S=S, DC=DC)
    dx, dprojb, macc_m, gacc_m = _in_bwd(da, dg, dq, dk, dv, x2, dx1, mod3, g_mix, w_in, S=S, tm=TM_IN)
    packed = _pack_small(macc_m, macc_f, gacc_m, gacc_f, lacc, gqk, dwdw, lossb)
    if pc_idx is None:
        grads["w_in"] = wg(h1, dprojb, name="wgrad_in", split="b")
    else:
        grads["w_in"], out["gathered_small"] = wg(h1, dprojb, name="wgrad_in", split="b",
                                                  host=_small_gather_host(packed))
    out.update(dx=dx.reshape(Bl, S, D), grads=grads, packed=packed)
    return out


EARLY_WEIGHTS = ("w_down", "w_gate", "w_up", "w_out")


def _small_layout(Bl):
    return 8 * Bl, 8 * Bl + 8, 8 * Bl + 8 + CONV_ROWS


def _pack_small(macc_m, macc_f, gacc_m, gacc_f, lacc, gqk, dwdw, lossb):
    Bl, _, D = macc_m.shape
    DC = lacc.shape[1]
    assert 2 * DC <= D
    SMALL_GAIN_ROW, SMALL_TAP_ROW, SMALL_ROWS = _small_layout(Bl)

    def body(mm_ref, mf_ref, gm_ref, gf_ref, la_ref, qk_ref, dw_ref, loss_ref, o_ref):
        o_ref[...] = jnp.zeros_like(o_ref)
        for b in range(Bl):
            o_ref[8 * b + 0:8 * b + 2, :] = mm_ref[b, 0:2, :]
            o_ref[8 * b + 2:8 * b + 3, :] = mf_ref[b, 3:4, :]
            o_ref[8 * b + 3:8 * b + 6, :] = mf_ref[b, 0:3, :]
        r = SMALL_GAIN_ROW
        o_ref[r:r + 1, :] = gm_ref[0:1, :]
        o_ref[r + 1:r + 2, :] = gf_ref[0:1, :]
        o_ref[r + 2:r + 3, 0:DC] = la_ref[0:1, :]
        o_ref[r + 2:r + 3, DC:2 * DC] = la_ref[1:2, :]
        qk = qk_ref[0:2, 0:HEAD_DIM] + qk_ref[0:2, HEAD_DIM:2 * HEAD_DIM]
        o_ref[r + 3:r + 4, 0:HEAD_DIM] = qk[0:1, :]
        o_ref[r + 3:r + 4, HEAD_DIM:2 * HEAD_DIM] = qk[1:2, :]
        o_ref[r + 4:r + 5, 0:LANES] = loss_ref[0:1, :]
        o_ref[SMALL_TAP_ROW:SMALL_TAP_ROW + CONV_ROWS, 0:DC] = dw_ref[...]

    return pl.pallas_call(body, name="pack_small", out_shape=jax.ShapeDtypeStruct((SMALL_ROWS, D), F32),
                          compiler_params=_cp())(macc_m, macc_f, gacc_m, gacc_f, lacc, gqk, dwdw, lossb)


def _row_tile(rows, cap=512):
    if rows <= cap:
        return rows
    best = rows
    for t in range(8, cap + 1, 8):
        if rows % t == 0:
            best = t
    return best


def _cast_weights(ws, pidx, name):
    n = len(ws)
    halves = [(w.shape[0] // 2, w.shape[1]) for w in ws]

    def body(p_ref, *refs):
        for k in range(n):
            refs[n + k][...] = refs[k][...].astype(MXU_DTYPE)

    return pl.pallas_call(
        body, name=name,
        grid_spec=pltpu.PrefetchScalarGridSpec(
            num_scalar_prefetch=1, grid=(2,),
            in_specs=[pl.BlockSpec(h, lambda i, p: (i, 0)) for h in halves],
            out_specs=[pl.BlockSpec((None,) + h, lambda i, p: (p[0], i, 0)) for h in halves]),
        out_shape=[jax.ShapeDtypeStruct((4,) + w.shape, MXU_DTYPE) for w in ws],
        compiler_params=_cp(),
    )(pidx, *ws)


def _pair_add(gs, recvs, pc_idx, name):
    n = len(gs)
    P = gs[0].shape[0]
    halves = [(g.shape[1] // 2, g.shape[2]) for g in gs]

    def body(pc_ref, *refs):
        for k in range(n):
            g_ref, r_ref, o_ref, ob_ref = refs[k], refs[n + k], refs[2 * n + 2 * k], refs[2 * n + 2 * k + 1]
            s = g_ref[...] + r_ref[...]
            ob_ref[...] = s.astype(jnp.bfloat16)

            @pl.when(pl.program_id(0) == pc_ref[0])
            def _(o_ref=o_ref, s=s):
                o_ref[...] = s

    res = pl.pallas_call(
        body, name=name,
        grid_spec=pltpu.PrefetchScalarGridSpec(
            num_scalar_prefetch=1, grid=(P,),
            in_specs=[pl.BlockSpec((None,) + h, lambda p, pc: (p, pc[1], 0)) for h in halves]
                     + [pl.BlockSpec((None,) + h, lambda p, pc: (p, 0, 0)) for h in halves],
            out_specs=[spec for h in halves for spec in (pl.BlockSpec(h, lambda p, pc: (0, 0)),
                                                         pl.BlockSpec((None,) + h, lambda p, pc: (p, 0, 0)))]),
        out_shape=[shape for h in halves for shape in (jax.ShapeDtypeStruct(h, F32),
                                                       jax.ShapeDtypeStruct((P,) + h, jnp.bfloat16))],
        compiler_params=_cp(),
    )(pc_idx, *gs, *recvs)
    return [(res[2 * k], res[2 * k + 1]) for k in range(n)]


def _final_add(owns, recvs, pc_idx, name):
    n = len(owns)

    def body(pc_ref, *refs):
        for k in range(n):
            acc = refs[k][...]
            for j in range(3):
                acc = acc + refs[n + k][j].astype(F32)
            refs[2 * n + k][...] = acc

    return pl.pallas_call(
        body, name=name,
        grid_spec=pltpu.PrefetchScalarGridSpec(
            num_scalar_prefetch=1, grid=(1,),
            in_specs=[pl.BlockSpec(o.shape, lambda i, pc: (0, 0)) for o in owns]
                     + [pl.BlockSpec((3,) + o.shape, lambda i, pc: (0, 0, 0)) for o in owns],
            out_specs=[pl.BlockSpec(o.shape, lambda i, pc: (pc[1], 0)) for o in owns]),
        out_shape=[jax.ShapeDtypeStruct((2 * o.shape[0], o.shape[1]), F32) for o in owns],
        compiler_params=_cp(),
    )(pc_idx, *owns, *recvs)


def _adamw_update(w_ref, g_ref, m_ref, v_ref, d_ref, nm_ref, nv_ref):
    c1 = 1.0 - ADAM_B1 ** ADAM_STEP
    c2 = 1.0 - ADAM_B2 ** ADAM_STEP
    gg = g_ref[...]
    nm = ADAM_B1 * m_ref[...] + (1.0 - ADAM_B1) * gg
    nv = ADAM_B2 * v_ref[...] + (1.0 - ADAM_B2) * (gg * gg)
    nm_ref[...] = nm
    nv_ref[...] = nv
    d_ref[...] = -ADAM_LR * ((nm / c1) / (jnp.sqrt(nv / c2) + ADAM_EPS) + ADAM_WD * w_ref[...])


def _adamw(w, g, m, v, name):
    R, C = w.shape
    tr = _row_tile(R, 256)
    spec = pl.BlockSpec((tr, C), lambda i: (i, 0))
    return pl.pallas_call(
        functools.partial(_adamw_update), grid=(R // tr,), name=name,
        in_specs=[spec] * 4, out_specs=[spec] * 3,
        out_shape=[jax.ShapeDtypeStruct((R, C), F32)] * 3,
    )(w, g, m, v)


def _adamw_small(quads, name):
    n = len(quads)

    def body(*refs):
        for k in range(n):
            _adamw_update(*refs[4 * k:4 * k + 4], *refs[4 * n + 3 * k:4 * n + 3 * k + 3])

    whole = lambda a: pl.BlockSpec(a.shape, lambda i: (0, 0))
    res = pl.pallas_call(
        body, grid=(1,), name=name,
        in_specs=[whole(a) for q in quads for a in q],
        out_specs=[whole(q[0]) for q in quads for _ in range(3)],
        out_shape=[jax.ShapeDtypeStruct(q[0].shape, F32) for q in quads for _ in range(3)],
    )(*[a for q in quads for a in q])
    return [tuple(res[3 * k:3 * k + 3]) for k in range(n)]


def _startup(first, w_ada, b_cols, w_in_buf, *, Bl):
    rows, D = first.shape
    NA = w_ada.shape[1]
    n_dev = 8
    g_w = _WeightGather([w_in_buf.shape])
    g_c = _SmallGather(rows)
    g_m = _SmallGather(n_dev * Bl)

    def body(first_ref, wada_ref, b_ref, win_in, g0_ref, call_ref, gm_ref, win_out, modp,
             ws0, ws1, cs0, cs1, cs2, ms0, ms1, ms2):
        g_w.start([win_out], (ws0, ws1))
        for phase in (g_c.start, g_c.forward, g_c.finish):
            phase([first_ref], [g0_ref], (cs0, cs1, cs2))
        for d in range(n_dev):
            call_ref[Bl * d:Bl * (d + 1), :] = g0_ref[rows * d:rows * d + Bl, :]
        c = call_ref[...]
        modp[...] = jnp.dot(c * _sigmoid(c), wada_ref[...], preferred_element_type=F32,
                            precision=lax.Precision.HIGH) + b_ref[...]
        for phase in (g_m.start, g_m.forward, g_m.finish):
            phase([modp], [gm_ref], (ms0, ms1, ms2))
        g_w.forward([win_out], (ws0, ws1))
        g_w.finish([win_out], (ws0, ws1))

    vmem = pl.BlockSpec(memory_space=pltpu.VMEM)
    anyspec = pl.BlockSpec(memory_space=pl.ANY)
    return pl.pallas_call(
        body, name="startup",
        in_specs=[vmem, vmem, vmem, anyspec], out_specs=[vmem, vmem, vmem, anyspec],
        out_shape=[jax.ShapeDtypeStruct((n_dev * rows, D), F32), jax.ShapeDtypeStruct((n_dev * Bl, D), F32),
                   jax.ShapeDtypeStruct((n_dev * n_dev * Bl, NA), F32),
                   jax.ShapeDtypeStruct(w_in_buf.shape, w_in_buf.dtype)],
        input_output_aliases={3: 3},
        scratch_shapes=[pltpu.VMEM((n_dev * Bl, NA), F32)] + g_w.scratch() + g_c.scratch() + g_m.scratch(),
        compiler_params=_cp(),
    )(first, w_ada, b_cols, w_in_buf)


def _ada_bwd(c_all, dmod_cols):
    def body(c_ref, d_ref, o_ref):
        c = c_ref[...]
        o_ref[...] = _dot_tn((c * _sigmoid(c)).astype(MXU_DTYPE), d_ref[...].astype(MXU_DTYPE))
    return pl.pallas_call(
        body, name="ada_bwd", out_shape=jax.ShapeDtypeStruct((c_all.shape[1], dmod_cols.shape[1]), F32),
        compiler_params=_cp(),
    )(c_all, dmod_cols)


def _small_reduce(gathered, n_dev, Bl):
    mod_rows, _, rows = _small_layout(Bl)
    width = gathered.shape[1]

    def body(g_ref, red_ref, bada_ref):
        acc = g_ref[0:rows, :]
        for d in range(1, n_dev):
            acc = acc + g_ref[d * rows:(d + 1) * rows, :]
        red_ref[...] = acc[mod_rows:, :]
        b = acc[0:8, :]
        for q in range(1, Bl):
            b = b + acc[8 * q:8 * q + 8, :]
        bada_ref[...] = b
    return pl.pallas_call(
        body, name="small_reduce",
        out_shape=[jax.ShapeDtypeStruct((rows - mod_rows, width), F32), jax.ShapeDtypeStruct((8, width), F32)],
        compiler_params=_cp(),
    )(gathered)


def _mesh_pos():
    return lax.axis_index("x"), lax.axis_index("y"), lax.axis_index("c")


def _other_chips(x, y):
    return [(1 - x, y), (x, 1 - y), (1 - x, 1 - y)]


class _WeightGather:
    def __init__(self, shapes):
        self.shapes = shapes
        self.n = len(shapes)

    def scratch(self):
        return [pltpu.SemaphoreType.DMA((6 * self.n,)), pltpu.SemaphoreType.DMA((6 * self.n,))]

    def _copy(self, outs, sems, w, k, slot, h, to):
        r2 = self.shapes[w][1] // 2
        blk = outs[w].at[slot, pl.ds(h * r2, r2), :]
        return pltpu.make_async_remote_copy(
            src_ref=blk, dst_ref=blk, send_sem=sems[0].at[6 * w + k], recv_sem=sems[1].at[6 * w + k],
            device_id=to, device_id_type=MESH_DEV)

    def start(self, outs, sems):
        x, y, c = _mesh_pos()
        for w in range(self.n):
            for k, chip in enumerate(_other_chips(x, y)):
                self._copy(outs, sems, w, k, 2 * x + y, c, (*chip, c)).start()

    def forward(self, outs, sems):
        x, y, c = _mesh_pos()
        for w in range(self.n):
            for k, chip in enumerate(_other_chips(x, y)):
                slot = 2 * chip[0] + chip[1]
                self._copy(outs, sems, w, k, slot, c, (x, y, 1 - c)).wait_recv()
                self._copy(outs, sems, w, 3 + k, slot, c, (x, y, 1 - c)).start()

    def finish(self, outs, sems):
        x, y, c = _mesh_pos()
        for w in range(self.n):
            for k, chip in enumerate(_other_chips(x, y)):
                slot = 2 * chip[0] + chip[1]
                self._copy(outs, sems, w, 3 + k, slot, 1 - c, (x, y, 1 - c)).wait_recv()
                self._copy(outs, sems, w, k, 2 * x + y, c, (*chip, c)).wait_send()
                self._copy(outs, sems, w, 3 + k, slot, c, (x, y, 1 - c)).wait_send()


class _SiblingExchange:
    def __init__(self, shapes):
        self.shapes = shapes

    def scratch(self):
        n = sum(s[0] for s in self.shapes)
        return [pltpu.SemaphoreType.DMA((n,)), pltpu.SemaphoreType.DMA((n,))]

    def out_shapes(self, dtype):
        return [jax.ShapeDtypeStruct((s[0], s[1] // 2, s[2]), dtype) for s in self.shapes]

    def _copies(self, ins, outs, sems):
        x, y, c = _mesh_pos()
        cps, k = [], 0
        for w, (P, R, _) in enumerate(self.shapes):
            r2 = R // 2
            for p in range(P):
                cps.append(pltpu.make_async_remote_copy(
                    src_ref=ins[w].at[p, pl.ds((1 - c) * r2, r2), :], dst_ref=outs[w].at[p],
                    send_sem=sems[0].at[k], recv_sem=sems[1].at[k],
                    device_id=(x, y, 1 - c), device_id_type=MESH_DEV))
                k += 1
        return cps

    def start(self, ins, outs, sems):
        for cp in self._copies(ins, outs, sems):
            cp.start()

    def forward(self, ins, outs, sems):
        pass

    def finish(self, ins, outs, sems):
        for cp in self._copies(ins, outs, sems):
            cp.wait()


def _sibling_host(grads):
    plan = _SiblingExchange([g.shape for g in grads])
    return plan, tuple(grads), tuple(plan.out_shapes(grads[0].dtype))


def _rs_sibling(grads, name):
    n = len(grads)
    plan, _, out_shapes = _sibling_host(grads)

    def body(*refs):
        ins, outs, sems = refs[:n], refs[n:2 * n], refs[2 * n:]
        plan.start(ins, outs, sems)
        plan.finish(ins, outs, sems)

    anyspec = pl.BlockSpec(memory_space=pl.ANY)
    return pl.pallas_call(
        body, name=name, out_shape=list(out_shapes),
        in_specs=[anyspec] * n, out_specs=[anyspec] * n, scratch_shapes=plan.scratch(),
    )(*grads)


class _SmallGather:
    def __init__(self, m_per):
        self.m = m_per

    def scratch(self):
        return [pltpu.SemaphoreType.DMA((7,)), pltpu.SemaphoreType.DMA((7,)), pltpu.SemaphoreType.DMA]

    def _rows(self, out, pos):
        px, py, pc = pos
        return out.at[pl.ds((4 * px + 2 * py + pc) * self.m, self.m), :]

    def _copy(self, out, sems, k, block, to, src=None):
        dst = self._rows(out, block)
        return pltpu.make_async_remote_copy(
            src_ref=dst if src is None else src, dst_ref=dst, send_sem=sems[0].at[k], recv_sem=sems[1].at[k],
            device_id=to, device_id_type=MESH_DEV)

    def start(self, ins, outs, sems):
        x, y, c = _mesh_pos()
        me = (x, y, c)
        pltpu.make_async_copy(ins[0], self._rows(outs[0], me), sems[2]).start()
        self._copy(outs[0], sems, 0, me, (x, y, 1 - c), src=ins[0]).start()
        for j, chip in enumerate(_other_chips(x, y)):
            self._copy(outs[0], sems, 1 + j, me, (*chip, c), src=ins[0]).start()

    def forward(self, ins, outs, sems):
        x, y, c = _mesh_pos()
        for j, chip in enumerate(_other_chips(x, y)):
            self._copy(outs[0], sems, 1 + j, (*chip, c), (x, y, c)).wait_recv()
            self._copy(outs[0], sems, 4 + j, (*chip, c), (x, y, 1 - c)).start()

    def finish(self, ins, outs, sems):
        x, y, c = _mesh_pos()
        me = (x, y, c)
        self._copy(outs[0], sems, 0, (x, y, 1 - c), me).wait_recv()
        for j, chip in enumerate(_other_chips(x, y)):
            self._copy(outs[0], sems, 4 + j, (*chip, 1 - c), me).wait_recv()
        self._copy(outs[0], sems, 0, me, (x, y, 1 - c), src=ins[0]).wait_send()
        for j, chip in enumerate(_other_chips(x, y)):
            self._copy(outs[0], sems, 1 + j, me, (*chip, c), src=ins[0]).wait_send()
            self._copy(outs[0], sems, 4 + j, (*chip, c), (x, y, 1 - c)).wait_send()
        pltpu.make_async_copy(ins[0], self._rows(outs[0], me), sems[2]).wait()


def _small_gather_host(packed):
    m, n = packed.shape
    return _SmallGather(m), (packed,), (jax.ShapeDtypeStruct((8 * m, n), packed.dtype),)


class _ChipExchange:
    def __init__(self, n):
        self.n = n

    def scratch(self):
        return [pltpu.SemaphoreType.DMA((3 * self.n,)), pltpu.SemaphoreType.DMA((3 * self.n,))]

    def _copies(self, ins, outs, sems):
        x, y, c = _mesh_pos()
        return [pltpu.make_async_remote_copy(
            src_ref=ins[w].at[2 * chip[0] + chip[1]], dst_ref=outs[w].at[k],
            send_sem=sems[0].at[3 * w + k], recv_sem=sems[1].at[3 * w + k],
            device_id=(*chip, c), device_id_type=MESH_DEV)
            for w in range(self.n) for k, chip in enumerate(_other_chips(x, y))]

    def start(self, ins, outs, sems):
        for cp in self._copies(ins, outs, sems):
            cp.start()

    def forward(self, ins, outs, sems):
        pass

    def finish(self, ins, outs, sems):
        for cp in self._copies(ins, outs, sems):
            cp.wait()


def _rs_final(bufs, name, chips=()):
    n, nc = len(bufs), len(chips)
    plan = _ChipExchange(nc)

    def body(*refs):
        cin = refs[n:n + nc]
        outs = refs[n + nc:2 * n + nc]
        cout = refs[2 * n + nc:2 * n + 2 * nc]
        send_sems, recv_sems = refs[2 * n + 2 * nc:2 * n + 2 * nc + 2]
        csems = refs[2 * n + 2 * nc + 2:]
        x, y, c = _mesh_pos()
        if nc:
            plan.start(cin, cout, csems)
        cps = []
        for w in range(n):
            r2 = bufs[w].shape[0] // 2
            mine = outs[w].at[pl.ds(c * r2, r2), :]
            cps.append(pltpu.make_async_remote_copy(
                src_ref=mine, dst_ref=mine, send_sem=send_sems.at[w], recv_sem=recv_sems.at[w],
                device_id=(x, y, 1 - c), device_id_type=MESH_DEV))
            cps[-1].start()
        for cp in cps:
            cp.wait()
        if nc:
            plan.finish(cin, cout, csems)

    anyspec = pl.BlockSpec(memory_space=pl.ANY)
    return pl.pallas_call(
        body, name=name,
        out_shape=[jax.ShapeDtypeStruct(b.shape, b.dtype) for b in bufs]
                  + [jax.ShapeDtypeStruct((3,) + s.shape[1:], s.dtype) for s in chips],
        in_specs=[anyspec] * (n + nc), out_specs=[anyspec] * (n + nc),
        input_output_aliases={w: w for w in range(n)},
        scratch_shapes=[pltpu.SemaphoreType.DMA((n,)), pltpu.SemaphoreType.DMA((n,))] + (plan.scratch() if nc else []),
    )(*bufs, *chips)


BIG = ("w_in", "w_out", "w_gate", "w_up", "w_down")
TRANSPOSED = ("w_gate", "w_up")
WEIGHTS = ("w_ada", "b_ada", "g_mix", "w_in", "w_dw", "b_dw", "g_conv_ln", "b_conv_ln", "g_q", "g_k",
           "w_out", "g_ffn", "w_gate", "w_up", "w_down")


def _pad_to(a, rows, cols):
    return jnp.pad(a, ((0, rows - a.shape[0]), (0, cols - a.shape[1])))


def kernel(x, c, w_ada, b_ada, g_mix, w_in, w_dw, b_dw, g_conv_ln, b_conv_ln, g_q, g_k, w_out, g_ffn, w_gate, w_up, w_down, loss_target, m_w_ada, m_b_ada, m_g_mix, m_w_in, m_w_dw, m_b_dw, m_g_conv_ln, m_b_conv_ln, m_g_q, m_g_k, m_w_out, m_g_ffn, m_w_gate, m_w_up, m_w_down, v_w_ada, v_b_ada, v_g_mix, v_w_in, v_w_dw, v_b_dw, v_g_conv_ln, v_b_conv_ln, v_g_q, v_g_k, v_w_out, v_g_ffn, v_w_gate, v_w_up, v_w_down):
    w = dict(w_ada=w_ada, b_ada=b_ada, g_mix=g_mix, w_in=w_in, w_dw=w_dw, b_dw=b_dw, g_conv_ln=g_conv_ln,
             b_conv_ln=b_conv_ln, g_q=g_q, g_k=g_k, w_out=w_out, g_ffn=g_ffn, w_gate=w_gate, w_up=w_up, w_down=w_down)
    m = dict(w_ada=m_w_ada, b_ada=m_b_ada, g_mix=m_g_mix, w_in=m_w_in, w_dw=m_w_dw, b_dw=m_b_dw, g_conv_ln=m_g_conv_ln,
             b_conv_ln=m_b_conv_ln, g_q=m_g_q, g_k=m_g_k, w_out=m_w_out, g_ffn=m_g_ffn, w_gate=m_w_gate, w_up=m_w_up,
             w_down=m_w_down)
    v = dict(w_ada=v_w_ada, b_ada=v_b_ada, g_mix=v_g_mix, w_in=v_w_in, w_dw=v_w_dw, b_dw=v_b_dw, g_conv_ln=v_g_conv_ln,
             b_conv_ln=v_b_conv_ln, g_q=v_g_q, g_k=v_g_k, w_out=v_w_out, g_ffn=v_g_ffn, w_gate=v_w_gate, w_up=v_w_up,
             w_down=v_w_down)
    Bl, S, D = x.shape
    DC = g_conv_ln.shape[1]
    NA = w_ada.shape[2]
    xi, yi, ci = _mesh_pos()
    p = 2 * xi + yi
    dev = 2 * p + ci
    n_dev = 8
    pidx = jnp.reshape(p, (1,)).astype(jnp.int32)
    pc_idx = jnp.stack([p, ci]).astype(jnp.int32)

    first = jnp.concatenate([_pad_to(c, 8, D), _pad_to(w_dw[0], CONV_ROWS, D)], axis=0)
    shard = lambda a, nm: a[0].T if nm in TRANSPOSED else a[0]
    owned = dict(zip(BIG, _cast_weights([shard(w[nm], nm) for nm in BIG], pidx, "cast_weights")))
    b_cols = lax.dynamic_slice_in_dim(b_ada, p * NA, NA, axis=1)
    g0, c_all, gm, w_in_full = _startup(first, w_ada[0], b_cols, owned["w_in"], Bl=Bl)
    g0 = g0.reshape(n_dev, 8 + CONV_ROWS, D)
    taps = jnp.concatenate([g0[2 * q, 8:, :w_dw.shape[2]] for q in range(4)], axis=1)
    wdw = jnp.where(lax.broadcasted_iota(jnp.int32, taps.shape, 0) == CONV_WIDTH, b_dw, taps)
    gm = gm.reshape(n_dev, n_dev * Bl, NA)
    mod = jnp.concatenate([lax.dynamic_slice_in_dim(gm[2 * q], dev * Bl, Bl, axis=0) for q in range(4)], axis=1)

    loc = _local_step(x, loss_target, mod, g_mix, wdw, g_conv_ln, b_conv_ln, g_q, g_k, g_ffn,
                      w_in_full, owned["w_out"], owned["w_gate"], owned["w_up"], owned["w_down"], pc_idx=pc_idx)

    halves = _final_add(loc["early_sums"], loc["early_recv"], pc_idx, "final_add_early")
    (late_sib,) = _rs_sibling([loc["grads"]["w_in"]], "rs_sibling_in")
    ((late32, late16),) = _pair_add([loc["grads"]["w_in"]], [late_sib], pc_idx, "pair_add_w_in")
    *early_full, late_recv = _rs_final(halves, "rs_final_early", chips=(late16,))
    grad = dict(zip(EARLY_WEIGHTS, early_full))
    grad["w_in"], = _rs_final(_final_add([late32], [late_recv], pc_idx, "final_add_w_in"), "rs_final_in")

    mod_rows, _, small_rows = _small_layout(Bl)
    gs = loc["gathered_small"]
    red, bada8 = _small_reduce(gs, n_dev, Bl)
    dmod_all = gs.reshape(n_dev, small_rows, D)[:, :mod_rows].reshape(n_dev * Bl, 8, D)[:, :N_MOD].reshape(n_dev * Bl, N_MOD * D)
    grad["w_ada"] = _ada_bwd(c_all, lax.dynamic_slice_in_dim(dmod_all, p * NA, NA, axis=1))
    grad["b_ada"] = bada8[:N_MOD].reshape(1, N_MOD * D)
    grad["g_mix"] = red[0:1]
    grad["g_ffn"] = red[1:2]
    grad["g_conv_ln"] = red[2:3, :DC]
    grad["b_conv_ln"] = red[2:3, DC:2 * DC]
    grad["g_q"] = red[3:4, :HEAD_DIM]
    grad["g_k"] = red[3:4, HEAD_DIM:2 * HEAD_DIM]
    loss = red[4, 0]
    dwdw = red[8:8 + CONV_ROWS, :DC]
    grad["w_dw"] = lax.dynamic_slice_in_dim(dwdw[:CONV_WIDTH], p * w_dw.shape[2], w_dw.shape[2], axis=1)
    grad["b_dw"] = dwdw[CONV_WIDTH:CONV_WIDTH + 1]

    delta, new_m, new_v = {}, {}, {}
    two_d = lambda nm: w[nm].shape[-2:]
    small = [nm for nm in WEIGHTS if nm not in BIG and nm != "w_ada"]
    small_res = dict(zip(small, _adamw_small(
        [tuple(a.reshape(two_d(nm)) for a in (w[nm], grad[nm], m[nm], v[nm])) for nm in small], "adamw_small")))
    for nm in WEIGHTS:
        shp = w[nm].shape
        if nm in TRANSPOSED:
            d_, m_, v_ = _adamw(w[nm][0].T, grad[nm], m[nm][0].T, v[nm][0].T, "adamw_" + nm)
            grad[nm], delta[nm], new_m[nm], new_v[nm] = (a.T.reshape(shp) for a in (grad[nm], d_, m_, v_))
            continue
        if nm in small_res:
            d_, m_, v_ = small_res[nm]
        else:
            d_, m_, v_ = _adamw(*(a.reshape(two_d(nm)) for a in (w[nm], grad[nm], m[nm], v[nm])), "adamw_" + nm)
        grad[nm] = grad[nm].reshape(shp)
        delta[nm], new_m[nm], new_v[nm] = d_.reshape(shp), m_.reshape(shp), v_.reshape(shp)

    return (loss, loc["dx"], *[grad[nm] for nm in WEIGHTS], *[delta[nm] for nm in WEIGHTS],
            *[new_m[nm] for nm in WEIGHTS], *[new_v[nm] for nm in WEIGHTS])
```
